```python
import jax, jax.numpy as jnp
from jax import lax
import numpy as np

D_MODEL = 1024
BATCH = 2
SEQ = 8192
DEPTH = 2
DEC_BATCH = 128
DEC_SEQ = 8
PAST_LEN = 16384
PAGE_SIZE = 128

N_EVEN = (DEPTH + 1) // 2
N_ODD = DEPTH // 2
EPS = 1e-6
CONV_DIM = D_MODEL // 2
CONV_W = 3
N_Q_HEADS = 8
N_KV_HEADS = 2
HEAD_DIM = 64
GROUP = N_Q_HEADS // N_KV_HEADS
WINDOW = 128
ATTN_BLOCK = WINDOW
ROPE_THETA = 10000.0
AB_IN = 3 * CONV_DIM + (N_Q_HEADS + 2 * N_KV_HEADS) * HEAD_DIM
AB_OUT = CONV_DIM + N_Q_HEADS * HEAD_DIM
AB_SPLIT = (CONV_DIM, 2 * CONV_DIM, 3 * CONV_DIM,
            3 * CONV_DIM + N_Q_HEADS * HEAD_DIM,
            3 * CONV_DIM + (N_Q_HEADS + N_KV_HEADS) * HEAD_DIM)
ML_HEADS = 4
ML_QK = 128
ML_V = D_MODEL // ML_HEADS
ML_CHUNK = 128
ML_IN = 2 * ML_HEADS * ML_QK + 2 * ML_HEADS * ML_V + 2 * ML_HEADS
ML_SPLIT = (ML_HEADS * ML_QK, 2 * ML_HEADS * ML_QK,
            2 * ML_HEADS * ML_QK + ML_HEADS * ML_V,
            2 * ML_HEADS * ML_QK + 2 * ML_HEADS * ML_V)
N_KEYS = 128
N_EXPERTS = N_KEYS * N_KEYS
PEER_HEADS = 8
PEER_HALF = 128
PEER_QDIM = 2 * PEER_HALF
PEER_TOPK = 16
PEER_BLOCK = 256

kernel_name = 'hybrid_conv_swa_mlstm_peer_step'


def rmsnorm(x, g):
    xf = x.astype(jnp.float32)
    r = xf * lax.rsqrt(jnp.mean(xf * xf, axis=-1, keepdims=True) + EPS)
    return (r * g.astype(jnp.float32)).astype(x.dtype)


def rope(x, pos):
    half = HEAD_DIM // 2
    inv_freq = ROPE_THETA ** (-jnp.arange(half, dtype=jnp.float32) / half)
    ang = pos.astype(jnp.float32)[:, None] * inv_freq[None, :]
    cos = jnp.cos(ang)[:, None, :]
    sin = jnp.sin(ang)[:, None, :]
    xf = x.astype(jnp.float32)
    x1, x2 = xf[..., :half], xf[..., half:]
    return jnp.concatenate([x1 * cos - x2 * sin, x2 * cos + x1 * sin], axis=-1).astype(x.dtype)


def window_mask(q_pos, k_pos):
    rel = q_pos[..., :, None] - k_pos[..., None, :]
    return (rel >= 0) & (rel < WINDOW) & (k_pos[..., None, :] >= 0)


def sink_attention(q, k, v, mask, sinks):
    s = jnp.einsum('...qhgd,...khd->...hgqk', q, k).astype(jnp.float32) * (HEAD_DIM ** -0.5)
    s = jnp.where(mask, s, -1e30)
    sink = sinks.astype(jnp.float32).reshape(N_KV_HEADS, GROUP)[:, :, None, None]
    m = jnp.maximum(jnp.max(s, axis=-1, keepdims=True), sink)
    p = jnp.exp(s - m)
    denom = jnp.sum(p, axis=-1, keepdims=True) + jnp.exp(sink - m)
    p = (p / denom).astype(v.dtype)
    return jnp.einsum('...hgqk,...khd->...qhgd', p, v)


def banded_attention(q, k, v, sinks):
    b, t = q.shape[:2]
    nb = t // ATTN_BLOCK
    qb = q.reshape(b, nb, ATTN_BLOCK, N_KV_HEADS, GROUP, HEAD_DIM)
    kb = k.reshape(b, nb, ATTN_BLOCK, N_KV_HEADS, HEAD_DIM)
    vb = v.reshape(b, nb, ATTN_BLOCK, N_KV_HEADS, HEAD_DIM)
    pad = ((0, 0), (1, 0), (0, 0), (0, 0), (0, 0))
    kk = jnp.concatenate([jnp.pad(kb, pad)[:, :-1], kb], axis=2)
    vv = jnp.concatenate([jnp.pad(vb, pad)[:, :-1], vb], axis=2)
    q_pos = jnp.arange(t, dtype=jnp.int32).reshape(nb, ATTN_BLOCK)
    k_pos = q_pos[:, :1] - ATTN_BLOCK + jnp.arange(2 * ATTN_BLOCK, dtype=jnp.int32)[None, :]
    mask = window_mask(q_pos, k_pos)[:, None, None]
    o = sink_attention(qb, kk, vv, mask, sinks)
    return o.reshape(b, t, N_Q_HEADS * HEAD_DIM)


def cached_window_attention(q, k, v, k_buf, v_buf, sinks, pos):
    b, t = q.shape[:2]
    kk = jnp.concatenate([k_buf.astype(k.dtype), k], axis=1)
    vv = jnp.concatenate([v_buf.astype(v.dtype), v], axis=1)
    k_pos = jnp.concatenate([PAST_LEN - WINDOW + jnp.arange(WINDOW, dtype=jnp.int32), pos])
    mask = window_mask(pos, k_pos)
    o = sink_attention(q.reshape(b, t, N_KV_HEADS, GROUP, HEAD_DIM), kk, vv, mask, sinks)
    return o.reshape(b, t, N_Q_HEADS * HEAD_DIM), kk[:, -WINDOW:], vv[:, -WINDOW:]


def short_conv(u, buf, w):
    full = jnp.concatenate([buf.astype(u.dtype), u], axis=1)
    t = u.shape[1]
    y = w[0] * full[:, 0:t] + w[1] * full[:, 1:t + 1] + w[2] * full[:, 2:t + 2]
    return y, full[:, -(CONV_W - 1):]


def mixer_ab(h, pos, conv_buf, k_buf, v_buf, w_in, conv_w, q_gain, k_gain, sinks, w_out):
    b, t, _ = h.shape
    z = h @ w_in
    gate_b, gate_c, x_in, q, k, v = jnp.split(z, list(AB_SPLIT), axis=-1)
    q = rope(rmsnorm(q.reshape(b, t, N_Q_HEADS, HEAD_DIM), q_gain), pos)
    k = rope(rmsnorm(k.reshape(b, t, N_KV_HEADS, HEAD_DIM), k_gain), pos)
    v = v.reshape(b, t, N_KV_HEADS, HEAD_DIM)
    conv_out, new_conv = short_conv(gate_c * x_in, conv_buf, conv_w)
    y_conv = gate_b * conv_out
    if k_buf is None:
        attn = banded_attention(q, k, v, sinks)
        new_k, new_v = k[:, -WINDOW:], v[:, -WINDOW:]
    else:
        attn, new_k, new_v = cached_window_attention(q, k, v, k_buf, v_buf, sinks, pos)
    y = jnp.concatenate([y_conv, attn], axis=-1) @ w_out
    return y, new_conv, new_k, new_v


def mlstm_chunk(state, chunk):
    C0, n0, m0 = state
    q, k, v, li, lf = chunk
    L = q.shape[1]
    F = jnp.cumsum(lf, axis=1).transpose(0, 2, 1)
    li = li.transpose(0, 2, 1)
    causal = jnp.tril(jnp.ones((L, L), dtype=bool))
    D = jnp.where(causal, F[..., :, None] - F[..., None, :] + li[..., None, :], -jnp.inf)
    g = F + m0[..., None]
    m_t = jnp.maximum(jnp.max(D, axis=-1), g)
    W = jnp.exp(D - m_t[..., None])
    S = jnp.einsum('blhk,bshk->bhls', q, k) * W
    carry_w = jnp.exp(g - m_t)
    num = (jnp.einsum('bhls,bshv->blhv', S, v)
           + jnp.einsum('blhk,bhkv->blhv', q, C0) * carry_w.transpose(0, 2, 1)[..., None])
    den = jnp.sum(S, axis=-1) + carry_w * jnp.einsum('blhk,bhk->bhl', q, n0)
    h = num / jnp.maximum(jnp.abs(den), jnp.exp(-m_t)).transpose(0, 2, 1)[..., None]
    F_last = F[..., -1]
    w_end = F_last[..., None] - F + li
    m_new = jnp.maximum(F_last + m0, jnp.max(w_end, axis=-1))
    a_end = jnp.exp(w_end - m_new[..., None])
    scale = jnp.exp(F_last + m0 - m_new)
    C_new = scale[..., None, None] * C0 + jnp.einsum('bhs,bshk,bshv->bhkv', a_end, k, v)
    n_new = scale[..., None] * n0 + jnp.einsum('bhs,bshk->bhk', a_end, k)
    return (C_new, n_new, m_new), h


def mixer_mlstm(h, C0, n0, m0, chunked, w_in, gate_bias, out_gain, w_out):
    b, t, _ = h.shape
    z = h @ w_in
    q, k, v, o, gates = jnp.split(z, list(ML_SPLIT), axis=-1)
    f32 = jnp.float32
    q = q.reshape(b, t, ML_HEADS, ML_QK).astype(f32)
    k = k.reshape(b, t, ML_HEADS, ML_QK).astype(f32) * (ML_QK ** -0.5)
    v = v.reshape(b, t, ML_HEADS, ML_V).astype(f32)
    gates = gates.astype(f32) + gate_bias.astype(f32)
    log_i = gates[..., :ML_HEADS]
    log_f = jax.nn.log_sigmoid(gates[..., ML_HEADS:])
    state = (C0.astype(f32), n0.astype(f32), m0.astype(f32))
    if chunked:
        nc = t // ML_CHUNK
        def to_chunks(a):
            return a.reshape((b, nc, ML_CHUNK) + a.shape[2:]).swapaxes(0, 1)
        state, hs = lax.scan(mlstm_chunk, state, (to_chunks(q), to_chunks(k), to_chunks(v),
                                                   to_chunks(log_i), to_chunks(log_f)))
        hh = hs.swapaxes(0, 1).reshape(b, t, ML_HEADS, ML_V)
    else:
        state, hh = mlstm_chunk(state, (q, k, v, log_i, log_f))
    hh = rmsnorm(hh, out_gain.reshape(ML_HEADS, ML_V)).reshape(b, t, ML_HEADS * ML_V)
    out = (jax.nn.sigmoid(o.astype(f32)) * hh).astype(h.dtype)
    C_new, n_new, m_new = state
    return out @ w_out, C_new, n_new, m_new


def peer_block(x, w_q, sub_keys, u_tab, v_tab):
    n = x.shape[0]
    q = (x @ w_q).reshape(n, PEER_HEADS, 2, PEER_HALF).astype(jnp.float32)
    s = jnp.einsum('nhpd,hpkd->nhpk', q, sub_keys.astype(jnp.float32))
    vals, idx = lax.top_k(s, PEER_TOPK)
    cand = vals[:, :, 0, :, None] + vals[:, :, 1, None, :]
    best, flat = lax.top_k(cand.reshape(n, PEER_HEADS, PEER_TOPK * PEER_TOPK), PEER_TOPK)
    i1 = jnp.take_along_axis(idx[:, :, 0], flat // PEER_TOPK, axis=-1)
    i2 = jnp.take_along_axis(idx[:, :, 1], flat % PEER_TOPK, axis=-1)
    expert = i1 * N_KEYS + i2
    gate = jax.nn.softmax(best, axis=-1)
    act = jax.nn.gelu(jnp.einsum('nhkd,nd->nhk', u_tab[expert], x).astype(jnp.float32),
                      approximate=False)
    coef = (gate * act).astype(x.dtype)
    return jnp.einsum('nhk,nhkd->nd', coef, v_tab[expert])


def peer_ffn(x, w_q, sub_keys, u_tab, v_tab):
    b, t, d = x.shape
    n = b * t
    pad = (-n) % PEER_BLOCK
    flat = jnp.pad(x.reshape(n, d), ((0, pad), (0, 0))).reshape(-1, PEER_BLOCK, d)
    out = lax.map(lambda xb: peer_block(xb, w_q, sub_keys, u_tab, v_tab), flat)
    return out.reshape(-1, d)[:n].reshape(b, t, d)


def trunk(x, pos, caches, weights):
    (norm_mix, norm_ffn, ab_w_in, ab_conv_w, ab_q_gain, ab_k_gain, ab_sinks, ab_w_out,
     ml_w_in, ml_gate_bias, ml_out_gain, ml_w_out, peer_w_q, peer_sub_keys, peer_u, peer_v) = weights
    b = x.shape[0]
    conv_l, wk_l, wv_l, C_l, n_l, m_l = [], [], [], [], [], []
    for layer in range(DEPTH):
        h = rmsnorm(x, norm_mix[layer])
        j = layer // 2
        if layer % 2 == 0:
            if caches is None:
                conv_buf = jnp.zeros((b, CONV_W - 1, CONV_DIM), x.dtype)
                k_buf = v_buf = None
            else:
                conv_buf, k_buf, v_buf = caches[0][j], caches[1][j], caches[2][j]
            y, c_new, k_new, v_new = mixer_ab(h, pos, conv_buf, k_buf, v_buf, ab_w_in[j], ab_conv_w[j],
                                              ab_q_gain[j], ab_k_gain[j], ab_sinks[j], ab_w_out[j])
            conv_l.append(c_new)
            wk_l.append(k_new)
            wv_l.append(v_new)
        else:
            if caches is None:
                C0 = jnp.zeros((b, ML_HEADS, ML_QK, ML_V), jnp.float32)
                n0 = jnp.zeros((b, ML_HEADS, ML_QK), jnp.float32)
                m0 = jnp.zeros((b, ML_HEADS), jnp.float32)
                chunked = True
            else:
                C0, n0, m0 = caches[3][j], caches[4][j], caches[5][j]
                chunked = False
            y, C_new, n_new, m_new = mixer_mlstm(h, C0, n0, m0, chunked, ml_w_in[j], ml_gate_bias[j],
                                                 ml_out_gain[j], ml_w_out[j])
            C_l.append(C_new)
            n_l.append(n_new)
            m_l.append(m_new)
        x = x + y
        x = x + peer_ffn(rmsnorm(x, norm_ffn[layer]), peer_w_q[layer], peer_sub_keys[layer],
                         peer_u[layer], peer_v[layer])
    return x, (jnp.stack(conv_l), jnp.stack(wk_l), jnp.stack(wv_l),
               jnp.stack(C_l), jnp.stack(n_l), jnp.stack(m_l))


def setup_inputs(seed: int = 0) -> dict:
    key = jax.random.key(seed)
    ks = iter(jax.random.split(key, 32))
    def nrm(shape, scale):
        return scale * jax.random.normal(next(ks), shape, jnp.float32)
    return {
        'x_prompt': nrm((BATCH, SEQ, D_MODEL), 1.0),
        'x_sample': nrm((DEC_BATCH, DEC_SEQ, D_MODEL), 1.0),
        'cache_conv': nrm((N_EVEN, DEC_BATCH, CONV_W - 1, CONV_DIM), 1.0),
        'cache_win_k': nrm((N_EVEN, DEC_BATCH, WINDOW, N_KV_HEADS, HEAD_DIM), 1.0),
        'cache_win_v': nrm((N_EVEN, DEC_BATCH, WINDOW, N_KV_HEADS, HEAD_DIM), 1.0),
        'state_mlstm_C': nrm((N_ODD, DEC_BATCH, ML_HEADS, ML_QK, ML_V), 0.1),
        'state_mlstm_n': nrm((N_ODD, DEC_BATCH, ML_HEADS, ML_QK), 0.1),
        'state_mlstm_m': nrm((N_ODD, DEC_BATCH, ML_HEADS), 1.0),
        'norm_mix': 1.0 + nrm((DEPTH, D_MODEL), 0.05),
        'norm_ffn': 1.0 + nrm((DEPTH, D_MODEL), 0.05),
        'ab_w_in': nrm((N_EVEN, D_MODEL, AB_IN), D_MODEL ** -0.5),
        'ab_conv_w': nrm((N_EVEN, CONV_W, CONV_DIM), CONV_W ** -0.5),
        'ab_q_gain': 1.0 + nrm((N_EVEN, HEAD_DIM), 0.05),
        'ab_k_gain': 1.0 + nrm((N_EVEN, HEAD_DIM), 0.05),
        'ab_sinks': nrm((N_EVEN, N_Q_HEADS), 0.5),
        'ab_w_out': nrm((N_EVEN, AB_OUT, D_MODEL), AB_OUT ** -0.5),
        'ml_w_in': nrm((N_ODD, D_MODEL, ML_IN), D_MODEL ** -0.5),
        'ml_gate_bias': jnp.concatenate([nrm((N_ODD, ML_HEADS), 0.1),
                                         3.0 + nrm((N_ODD, ML_HEADS), 0.5)], axis=-1),
        'ml_out_gain': 1.0 + nrm((N_ODD, ML_HEADS * ML_V), 0.05),
        'ml_w_out': nrm((N_ODD, ML_HEADS * ML_V, D_MODEL), (ML_HEADS * ML_V) ** -0.5),
        'peer_w_q': nrm((DEPTH, D_MODEL, PEER_HEADS * PEER_QDIM), D_MODEL ** -0.5),
        'peer_sub_keys': nrm((DEPTH, PEER_HEADS, 2, N_KEYS, PEER_HALF), PEER_HALF ** -0.5),
        'peer_u': nrm((DEPTH, N_EXPERTS, D_MODEL), D_MODEL ** -0.5),
        'peer_v': nrm((DEPTH, N_EXPERTS, D_MODEL), PEER_HEADS ** -0.5),
    }


def reference(x_prompt, x_sample, cache_conv, cache_win_k, cache_win_v, state_mlstm_C, state_mlstm_n,
              state_mlstm_m, norm_mix, norm_ffn, ab_w_in, ab_conv_w, ab_q_gain, ab_k_gain, ab_sinks,
              ab_w_out, ml_w_in, ml_gate_bias, ml_out_gain, ml_w_out, peer_w_q, peer_sub_keys, peer_u, peer_v):
    weights = (norm_mix, norm_ffn, ab_w_in, ab_conv_w, ab_q_gain, ab_k_gain, ab_sinks, ab_w_out,
               ml_w_in, ml_gate_bias, ml_out_gain, ml_w_out, peer_w_q, peer_sub_keys, peer_u, peer_v)
    pos_p = jnp.arange(SEQ, dtype=jnp.int32)
    pos_s = PAST_LEN + jnp.arange(DEC_SEQ, dtype=jnp.int32)
    y_prompt, st_p = trunk(x_prompt, pos_p, None, weights)
    caches = (cache_conv, cache_win_k, cache_win_v, state_mlstm_C, state_mlstm_n, state_mlstm_m)
    y_sample, st_s = trunk(x_sample, pos_s, caches, weights)
    conv_p, win_k_p, win_v_p, C_p, n_p, m_p = st_p
    conv_s, win_k_s, win_v_s, C_s, n_s, m_s = st_s
    return (y_prompt, y_sample, conv_p, win_k_p, win_v_p, C_p, n_p, m_p,
            conv_s, win_k_s, win_v_s, C_s, n_s, m_s)
```

```python
import functools

import jax
import jax.numpy as jnp
from jax import lax
from jax.experimental import pallas as pl
from jax.experimental.pallas import tpu as pltpu

F32 = jnp.float32
BF16 = jnp.bfloat16
EPS = 1e-6

D_MODEL = 1024
CONV_DIM = 512
N_Q_HEADS = 8
N_KV_HEADS = 2
HEAD_DIM = 64
WINDOW = 128
ROPE_THETA = 10000.0
AB_IN = 2304
ML_HEADS = 4
ML_QK = 128
ML_V = 256
ML_CHUNK = 128
ML_GATE_COL = 3072
ML_IN_PAD = ML_GATE_COL + 128
N_KEYS = 128
PEER_HEADS = 8
PEER_TOPK = 16
PAST_LEN = 16384

LANES = 128
SUBLANES = 8
G_PITCH = N_KEYS + SUBLANES
VMEM_LIMIT = 56 * 1024 * 1024

NT_DIMS = (((1,), (1,)), ((), ()))
TN_DIMS = (((0,), (0,)), ((), ()))


def _cparams(n_axes, vmem=VMEM_LIMIT):
    return pltpu.CompilerParams(dimension_semantics=("arbitrary",) * n_axes, vmem_limit_bytes=vmem)


def _rmsnorm(x, g):
    return x * lax.rsqrt(jnp.mean(x * x, axis=-1, keepdims=True) + EPS) * g


def _bf16_pieces(a, terms):
    pieces = []
    rem = a
    for _ in range(terms):
        piece = rem.astype(BF16)
        rem = rem - piece.astype(F32)
        pieces.append(piece)
    return pieces


def _split_dot(a, b_bf16, terms=2):
    return sum(jnp.dot(p, b_bf16, preferred_element_type=F32) for p in _bf16_pieces(a, terms))


def _norm_proj_kernel(x_ref, g_ref, w_ref, o_ref):
    r = _rmsnorm(x_ref[...], g_ref[...])
    o_ref[...] = jnp.dot(r.astype(BF16), w_ref[...], preferred_element_type=F32)


def _norm_proj(x, g, w_bf16, tm=512):
    n, d = x.shape
    nout = w_bf16.shape[1]
    return pl.pallas_call(
        _norm_proj_kernel,
        grid=(n // tm,),
        in_specs=[pl.BlockSpec((tm, d), lambda i: (i, 0)),
                  pl.BlockSpec((1, d), lambda i: (0, 0)),
                  pl.BlockSpec((d, nout), lambda i: (0, 0))],
        out_specs=pl.BlockSpec((tm, nout), lambda i: (i, 0)),
        out_shape=jax.ShapeDtypeStruct((n, nout), F32),
        compiler_params=_cparams(1),
        name="norm_proj",
    )(x, g.reshape(1, d), w_bf16)


def _headnorm_rope(xc, gain, cos, sin, seg, hi_half):
    ss = _split_dot(xc * xc, seg)
    xn = xc * lax.rsqrt(ss * (1.0 / HEAD_DIM) + EPS) * gain
    partner = jnp.where(hi_half, pltpu.roll(xn, 32, 1), pltpu.roll(xn, 96, 1))
    return xn * cos + partner * sin


def _softmax_sink(s, mask, sink):
    s = jnp.where(mask, s, -1e30)
    m = jnp.maximum(jnp.max(s, axis=-1, keepdims=True), sink)
    p = jnp.exp(s - m)
    denom = jnp.sum(p, axis=-1, keepdims=True) + jnp.exp(sink - m)
    return (p / denom).astype(BF16)


def _ab_prompt_kernel(sink_ref, z_ref, x_ref, cos_ref, sin_ref, cw_ref, qg_ref, kg_ref, seg_ref, wo_ref,
                      y_ref, kst_ref, vst_ref, cst_ref, pk_ref, pv_ref, pu_ref):
    j = pl.program_id(1)
    blk = z_ref.shape[0]

    @pl.when(j == 0)
    def _():
        pk_ref[...] = jnp.zeros_like(pk_ref)
        pv_ref[...] = jnp.zeros_like(pv_ref)
        pu_ref[...] = jnp.zeros_like(pu_ref)

    cos = cos_ref[...]
    sin = sin_ref[...]
    seg = seg_ref[...]
    hi_half = (lax.broadcasted_iota(jnp.int32, (blk, LANES), 1) & 32) != 0

    gate_b = z_ref[:, 0:CONV_DIM]
    u = z_ref[:, CONV_DIM:2 * CONV_DIM] * z_ref[:, 2 * CONV_DIM:3 * CONV_DIM]
    ng = blk // SUBLANES
    u3 = u.reshape(ng, SUBLANES, CONV_DIM)
    ext = jnp.concatenate([pu_ref[...][None], u3], axis=0)
    t8 = lax.broadcasted_iota(jnp.int32, (ng, SUBLANES, CONV_DIM), 1)
    r1 = pltpu.roll(ext, 1, 1)
    r2 = pltpu.roll(ext, 2, 1)
    um1 = jnp.where(t8 >= 1, r1[1:], r1[:-1])
    um2 = jnp.where(t8 >= 2, r2[1:], r2[:-1])
    cw = cw_ref[...]
    conv = cw[0:1][None] * um2 + cw[1:2][None] * um1 + cw[2:3][None] * u3
    yconv = gate_b * conv.reshape(blk, CONV_DIM)

    q0 = 3 * CONV_DIM
    k0 = q0 + N_Q_HEADS * HEAD_DIM
    v0 = k0 + N_KV_HEADS * HEAD_DIM
    qg = qg_ref[...]
    qr = [_headnorm_rope(z_ref[:, q0 + c * LANES:q0 + (c + 1) * LANES], qg, cos, sin, seg, hi_half)
          for c in range(N_Q_HEADS * HEAD_DIM // LANES)]
    kr = _headnorm_rope(z_ref[:, k0:k0 + LANES], kg_ref[...], cos, sin, seg, hi_half)
    v = z_ref[:, v0:v0 + LANES]
    pk = pk_ref[...]
    pv = pv_ref[...]

    row = lax.broadcasted_iota(jnp.int32, (blk, 2 * blk), 0)
    col = lax.broadcasted_iota(jnp.int32, (blk, 2 * blk), 1)
    row_prev = row + jnp.where(j == 0, 2 * blk, 0)
    mask = ((col < blk) & (col > row_prev)) | ((col >= blk) & (col - blk <= row))

    kks, vvs = [], []
    for g in range(N_KV_HEADS):
        sl = slice(g * HEAD_DIM, (g + 1) * HEAD_DIM)
        kks.append(jnp.concatenate([pk[:, sl], kr[:, sl]], axis=0).astype(BF16))
        vvs.append(jnp.concatenate([pv[:, sl], v[:, sl]], axis=0).astype(BF16))
    outs = []
    for h in range(N_Q_HEADS):
        g = h // (N_Q_HEADS // N_KV_HEADS)
        qh = qr[h // 2][:, (h % 2) * HEAD_DIM:(h % 2 + 1) * HEAD_DIM].astype(BF16)
        s = lax.dot_general(qh, kks[g], NT_DIMS, preferred_element_type=F32) * (HEAD_DIM ** -0.5)
        p = _softmax_sink(s, mask, sink_ref[h])
        outs.append(jnp.dot(p, vvs[g], preferred_element_type=F32))
    attn = jnp.concatenate(outs, axis=1)

    y = (jnp.dot(yconv.astype(BF16), wo_ref[0:CONV_DIM, :], preferred_element_type=F32)
         + jnp.dot(attn.astype(BF16), wo_ref[CONV_DIM:2 * CONV_DIM, :], preferred_element_type=F32))
    y_ref[...] = y + x_ref[...]

    pk_ref[...] = kr
    pv_ref[...] = v
    pu_ref[...] = u3[ng - 1]
    kst_ref[0] = kr
    vst_ref[0] = v
    cst_ref[0] = u3[ng - 1]


def _ab_prompt(z, x, n_batch, seq, cos, sin, cw, qg, kg, sinks, seg, wo):
    blk = WINDOW
    nb = seq // blk
    tok = lambda b, j, s: (b * nb + j, 0)
    const = lambda b, j, s: (0, 0)
    state = lambda b, j, s: (b, 0, 0)
    grid_spec = pltpu.PrefetchScalarGridSpec(
        num_scalar_prefetch=1,
        grid=(n_batch, nb),
        in_specs=[pl.BlockSpec((blk, AB_IN), tok),
                  pl.BlockSpec((blk, D_MODEL), tok),
                  pl.BlockSpec((blk, LANES), lambda b, j, s: (j, 0)),
                  pl.BlockSpec((blk, LANES), lambda b, j, s: (j, 0)),
                  pl.BlockSpec((3, CONV_DIM), const),
                  pl.BlockSpec((1, LANES), const),
                  pl.BlockSpec((1, LANES), const),
                  pl.BlockSpec((LANES, LANES), const),
                  pl.BlockSpec((D_MODEL, D_MODEL), const)],
        out_specs=[pl.BlockSpec((blk, D_MODEL), tok),
                   pl.BlockSpec((1, blk, LANES), state),
                   pl.BlockSpec((1, blk, LANES), state),
                   pl.BlockSpec((1, SUBLANES, CONV_DIM), state)],
        scratch_shapes=[pltpu.VMEM((blk, LANES), F32), pltpu.VMEM((blk, LANES), F32),
                        pltpu.VMEM((SUBLANES, CONV_DIM), F32)])
    return pl.pallas_call(
        _ab_prompt_kernel,
        grid_spec=grid_spec,
        out_shape=[jax.ShapeDtypeStruct((n_batch * seq, D_MODEL), F32),
                   jax.ShapeDtypeStruct((n_batch, blk, LANES), F32),
                   jax.ShapeDtypeStruct((n_batch, blk, LANES), F32),
                   jax.ShapeDtypeStruct((n_batch, SUBLANES, CONV_DIM), F32)],
        compiler_params=_cparams(2),
        name="ab_prompt",
    )(sinks, z, x, cos, sin, cw, qg, kg, seg, wo)


def _ab_sample_kernel(sink_ref, z_ref, x_ref, cos_ref, sin_ref, cw_ref, qg_ref, kg_ref, seg_ref, wo_ref,
                      cc_ref, ck_ref, cv_ref, y_ref, cs_ref, ks_ref, vs_ref):
    rows = z_ref.shape[0]
    t_len = SUBLANES
    bt = rows // t_len
    cos = cos_ref[...]
    sin = sin_ref[...]
    seg = seg_ref[...]
    hi_half = (lax.broadcasted_iota(jnp.int32, (rows, LANES), 1) & 32) != 0

    gate_b = z_ref[:, 0:CONV_DIM]
    u = z_ref[:, CONV_DIM:2 * CONV_DIM] * z_ref[:, 2 * CONV_DIM:3 * CONV_DIM]
    u3 = u.reshape(bt, t_len, CONV_DIM)
    cc = cc_ref[...]
    c0 = cc[:, 0:1, :]
    c1 = cc[:, 1:2, :]
    t8 = lax.broadcasted_iota(jnp.int32, (bt, t_len, CONV_DIM), 1)
    r1 = pltpu.roll(u3, 1, 1)
    r2 = pltpu.roll(u3, 2, 1)
    um1 = jnp.where(t8 >= 1, r1, c1)
    um2 = jnp.where(t8 >= 2, r2, jnp.where(t8 == 1, c1, c0))
    cw = cw_ref[...]
    conv = cw[0:1][None] * um2 + cw[1:2][None] * um1 + cw[2:3][None] * u3
    yconv = gate_b * conv.reshape(rows, CONV_DIM)
    cs_ref[...] = r2[:, 0:2, :]

    q0 = 3 * CONV_DIM
    k0 = q0 + N_Q_HEADS * HEAD_DIM
    v0 = k0 + N_KV_HEADS * HEAD_DIM
    qg = qg_ref[...]
    qr = [_headnorm_rope(z_ref[:, q0 + c * LANES:q0 + (c + 1) * LANES], qg, cos, sin, seg, hi_half)
          for c in range(N_Q_HEADS * HEAD_DIM // LANES)]
    kr = _headnorm_rope(z_ref[:, k0:k0 + LANES], kg_ref[...], cos, sin, seg, hi_half)
    v = z_ref[:, v0:v0 + LANES]

    group = N_Q_HEADS // N_KV_HEADS
    nq = group * t_len
    nk = 2 * WINDOW
    qrow = lax.broadcasted_iota(jnp.int32, (nq, nk), 0)
    t_q = qrow & (t_len - 1)
    col = lax.broadcasted_iota(jnp.int32, (nq, nk), 1)
    mask = (((col < WINDOW) & (col > t_q)) | ((col >= WINDOW) & (col - WINDOW <= t_q)))[None]
    hrow = lax.broadcasted_iota(jnp.int32, (nq, 1), 0) // t_len
    pad = jnp.zeros((bt, nk - WINDOW - t_len, HEAD_DIM), F32)

    outs = [None] * N_Q_HEADS
    for g in range(N_KV_HEADS):
        sl = slice(g * HEAD_DIM, (g + 1) * HEAD_DIM)
        qs = jnp.concatenate(
            [qr[h // 2][:, (h % 2) * HEAD_DIM:(h % 2 + 1) * HEAD_DIM].reshape(bt, t_len, HEAD_DIM)
             for h in range(g * group, (g + 1) * group)], axis=1)
        kk = jnp.concatenate([ck_ref[:, :, sl], kr[:, sl].reshape(bt, t_len, HEAD_DIM), pad], axis=1)
        vv = jnp.concatenate([cv_ref[:, :, sl], v[:, sl].reshape(bt, t_len, HEAD_DIM), pad], axis=1)
        s = jnp.einsum('bqd,bkd->bqk', qs.astype(BF16), kk.astype(BF16),
                       preferred_element_type=F32) * (HEAD_DIM ** -0.5)
        sink = jnp.zeros((nq, 1), F32)
        for hh in range(group):
            sink = jnp.where(hrow == hh, sink_ref[g * group + hh], sink)
        p = _softmax_sink(s, mask, sink[None])
        o = jnp.einsum('bqk,bkd->bqd', p, vv.astype(BF16), preferred_element_type=F32)
        for hh in range(group):
            outs[g * group + hh] = o[:, hh * t_len:(hh + 1) * t_len, :].reshape(rows, HEAD_DIM)
    attn = jnp.concatenate(outs, axis=1)

    y = (jnp.dot(yconv.astype(BF16), wo_ref[0:CONV_DIM, :], preferred_element_type=F32)
         + jnp.dot(attn.astype(BF16), wo_ref[CONV_DIM:2 * CONV_DIM, :], preferred_element_type=F32))
    y_ref[...] = y + x_ref[...]

    keep = WINDOW - t_len
    ks_ref[:, 0:keep, :] = ck_ref[:, t_len:WINDOW, :]
    ks_ref[:, keep:WINDOW, :] = kr.reshape(bt, t_len, LANES)
    vs_ref[:, 0:keep, :] = cv_ref[:, t_len:WINDOW, :]
    vs_ref[:, keep:WINDOW, :] = v.reshape(bt, t_len, LANES)


def _ab_sample(z, x, row0, n_seq, t_len, cos, sin, cw, qg, kg, sinks, seg, wo, cc, ck, cv, bt=16):
    rows = bt * t_len
    blk0 = row0 // rows
    tok = lambda i, s: (blk0 + i, 0)
    const = lambda i, s: (0, 0)
    seq3 = lambda i, s: (i, 0, 0)
    grid_spec = pltpu.PrefetchScalarGridSpec(
        num_scalar_prefetch=1,
        grid=(n_seq // bt,),
        in_specs=[pl.BlockSpec((rows, AB_IN), tok),
                  pl.BlockSpec((rows, D_MODEL), tok),
                  pl.BlockSpec((rows, LANES), const),
                  pl.BlockSpec((rows, LANES), const),
                  pl.BlockSpec((3, CONV_DIM), const),
                  pl.BlockSpec((1, LANES), const),
                  pl.BlockSpec((1, LANES), const),
                  pl.BlockSpec((LANES, LANES), const),
                  pl.BlockSpec((D_MODEL, D_MODEL), const),
                  pl.BlockSpec((bt, 2, CONV_DIM), seq3),
                  pl.BlockSpec((bt, WINDOW, LANES), seq3),
                  pl.BlockSpec((bt, WINDOW, LANES), seq3)],
        out_specs=[pl.BlockSpec((rows, D_MODEL), lambda i, s: (i, 0)),
                   pl.BlockSpec((bt, 2, CONV_DIM), seq3),
                   pl.BlockSpec((bt, WINDOW, LANES), seq3),
                   pl.BlockSpec((bt, WINDOW, LANES), seq3)])
    return pl.pallas_call(
        _ab_sample_kernel,
        grid_spec=grid_spec,
        out_shape=[jax.ShapeDtypeStruct((n_seq * t_len, D_MODEL), F32),
                   jax.ShapeDtypeStruct((n_seq, 2, CONV_DIM), F32),
                   jax.ShapeDtypeStruct((n_seq, WINDOW, LANES), F32),
                   jax.ShapeDtypeStruct((n_seq, WINDOW, LANES), F32)],
        compiler_params=_cparams(1),
        name="ab_sample",
    )(sinks, z, x, cos, sin, cw, qg, kg, seg, wo, cc, ck, cv)


def _log_sigmoid(x):
    return jnp.minimum(x, 0.0) - jnp.log(1.0 + jnp.exp(-jnp.abs(x)))


def _mlstm_chunk(z, x, bias, og, tril, wo, c_src, n_src, m_src, c_dst, n_dst, m_dst, n_real):
    L = z.shape[0]
    gates = z[:, ML_GATE_COL:ML_GATE_COL + LANES] + bias
    if n_real < L:
        live = lax.broadcasted_iota(jnp.int32, (L, LANES), 0) < n_real
        li_all = jnp.where(live, gates, -1e30)
        lf_all = jnp.where(live, _log_sigmoid(gates), 0.0)
    else:
        li_all = gates
        lf_all = _log_sigmoid(gates)
    lf_pieces = _bf16_pieces(lf_all, 3)
    f_col_all = sum(jnp.dot(tril, p, preferred_element_type=F32) for p in lf_pieces)
    f_row_all = sum(lax.dot_general(p, tril, (((0,), (1,)), ((), ())), preferred_element_type=F32)
                    for p in lf_pieces)
    li_t = li_all.T
    rr = lax.broadcasted_iota(jnp.int32, (L, L), 0)
    cc = lax.broadcasted_iota(jnp.int32, (L, L), 1)
    causal = cc <= rr

    outs, m_new_all = [], []
    for h in range(ML_HEADS):
        f_col = f_col_all[:, ML_HEADS + h:ML_HEADS + h + 1]
        f_row = f_row_all[ML_HEADS + h:ML_HEADS + h + 1, :]
        li_row = li_t[h:h + 1, :]
        li_col = li_all[:, h:h + 1]
        m0 = m_src[0, 0:1, h:h + 1]
        c0 = c_src[0, h]
        n0 = n_src[0, h:h + 1, :]
        qh = z[:, h * ML_QK:(h + 1) * ML_QK]
        kh = z[:, ML_HEADS * ML_QK + h * ML_QK:ML_HEADS * ML_QK + (h + 1) * ML_QK] * (ML_QK ** -0.5)
        v_off = 2 * ML_HEADS * ML_QK
        vh = z[:, v_off + h * ML_V:v_off + (h + 1) * ML_V]
        o_off = v_off + ML_HEADS * ML_V
        oh = z[:, o_off + h * ML_V:o_off + (h + 1) * ML_V]
        qb = qh.astype(BF16)
        vb = vh.astype(BF16)

        dmat = jnp.where(causal, f_col - f_row + li_row, -jnp.inf)
        gcar = f_col + m0
        m_t = jnp.maximum(jnp.max(dmat, axis=-1, keepdims=True), gcar)
        w = jnp.exp(dmat - m_t)
        s = lax.dot_general(qb, kh.astype(BF16), NT_DIMS, preferred_element_type=F32) * w
        carry = jnp.exp(gcar - m_t)
        num = (jnp.dot(s.astype(BF16), vb, preferred_element_type=F32)
               + jnp.dot(qb, c0.astype(BF16), preferred_element_type=F32) * carry)
        den = jnp.sum(s, axis=-1, keepdims=True) + carry * jnp.sum(qh * n0, axis=-1, keepdims=True)
        hout = num / jnp.maximum(jnp.abs(den), jnp.exp(-m_t))

        f_last = f_col[L - 1:L, :]
        w_end = f_last - f_col + li_col
        m_new = jnp.maximum(f_last + m0, jnp.max(w_end, axis=0, keepdims=True))
        a_end = jnp.exp(w_end - m_new)
        scale = jnp.exp(f_last + m0 - m_new)
        ka = kh * a_end
        c_dst[0, h] = scale * c0 + lax.dot_general(ka.astype(BF16), vb, TN_DIMS, preferred_element_type=F32)
        n_dst[0, h:h + 1, :] = scale * n0 + jnp.sum(ka, axis=0, keepdims=True)
        m_new_all.append(m_new)

        hn = _rmsnorm(hout, og[:, h * ML_V:(h + 1) * ML_V])
        outs.append(jax.nn.sigmoid(oh) * hn)
    m_dst[0] = jnp.concatenate(m_new_all, axis=1)
    out = jnp.concatenate(outs, axis=1)
    return jnp.dot(out.astype(BF16), wo, preferred_element_type=F32) + x


def _mlstm_prompt_kernel(z_ref, x_ref, bias_ref, og_ref, tril_ref, wo_ref, y_ref, c_ref, n_ref, m_ref):
    @pl.when(pl.program_id(1) == 0)
    def _():
        c_ref[...] = jnp.zeros_like(c_ref)
        n_ref[...] = jnp.zeros_like(n_ref)
        m_ref[...] = jnp.zeros_like(m_ref)

    y_ref[...] = _mlstm_chunk(z_ref[...], x_ref[...], bias_ref[...], og_ref[...], tril_ref[...], wo_ref[...],
                              c_ref, n_ref, m_ref, c_ref, n_ref, m_ref, ML_CHUNK)


def _mlstm_sample_kernel(z_ref, x_ref, bias_ref, og_ref, tril_ref, wo_ref, c0_ref, n0_ref, m0_ref,
                         y_ref, c_ref, n_ref, m_ref):
    t_len = z_ref.shape[0]
    zpad = jnp.concatenate([z_ref[...], jnp.zeros((ML_CHUNK - t_len, ML_IN_PAD), F32)], axis=0)
    xpad = jnp.concatenate([x_ref[...], jnp.zeros((ML_CHUNK - t_len, D_MODEL), F32)], axis=0)
    y = _mlstm_chunk(zpad, xpad, bias_ref[...], og_ref[...], tril_ref[...], wo_ref[...],
                     c0_ref, n0_ref, m0_ref, c_ref, n_ref, m_ref, t_len)
    y_ref[...] = y[0:t_len, :]


def _mlstm_weight_specs(const):
    return [pl.BlockSpec((1, LANES), const),
            pl.BlockSpec((1, D_MODEL), const),
            pl.BlockSpec((ML_CHUNK, ML_CHUNK), const),
            pl.BlockSpec((D_MODEL, D_MODEL), const)]


def _mlstm_prompt(z, x, n_batch, seq, bias, og, tril, wo):
    nc = seq // ML_CHUNK
    tok = lambda b, j: (b * nc + j, 0)
    const = lambda b, j: (0, 0)
    return pl.pallas_call(
        _mlstm_prompt_kernel,
        grid=(n_batch, nc),
        in_specs=[pl.BlockSpec((ML_CHUNK, ML_IN_PAD), tok),
                  pl.BlockSpec((ML_CHUNK, D_MODEL), tok)] + _mlstm_weight_specs(const),
        out_specs=[pl.BlockSpec((ML_CHUNK, D_MODEL), tok),
                   pl.BlockSpec((1, ML_HEADS, ML_QK, ML_V), lambda b, j: (b, 0, 0, 0)),
                   pl.BlockSpec((1, ML_HEADS, ML_QK), lambda b, j: (b, 0, 0)),
                   pl.BlockSpec((1, 1, ML_HEADS), lambda b, j: (b, 0, 0))],
        out_shape=[jax.ShapeDtypeStruct((n_batch * seq, D_MODEL), F32),
                   jax.ShapeDtypeStruct((n_batch, ML_HEADS, ML_QK, ML_V), F32),
                   jax.ShapeDtypeStruct((n_batch, ML_HEADS, ML_QK), F32),
                   jax.ShapeDtypeStruct((n_batch, 1, ML_HEADS), F32)],
        compiler_params=_cparams(2),
        name="mlstm_prompt",
    )(z, x, bias, og, tril, wo)


def _mlstm_sample(z, x, row0, n_seq, t_len, bias, og, tril, wo, c0, n0, m0):
    blk0 = row0 // t_len
    tok = lambda i: (blk0 + i, 0)
    const = lambda i: (0, 0)
    st4 = lambda i: (i, 0, 0, 0)
    st3 = lambda i: (i, 0, 0)
    state_specs = [pl.BlockSpec((1, ML_HEADS, ML_QK, ML_V), st4),
                   pl.BlockSpec((1, ML_HEADS, ML_QK), st3),
                   pl.BlockSpec((1, 1, ML_HEADS), st3)]
    return pl.pallas_call(
        _mlstm_sample_kernel,
        grid=(n_seq,),
        in_specs=[pl.BlockSpec((t_len, ML_IN_PAD), tok),
                  pl.BlockSpec((t_len, D_MODEL), tok)] + _mlstm_weight_specs(const) + state_specs,
        out_specs=[pl.BlockSpec((t_len, D_MODEL), lambda i: (i, 0))] + state_specs,
        out_shape=[jax.ShapeDtypeStruct((n_seq * t_len, D_MODEL), F32),
                   jax.ShapeDtypeStruct((n_seq, ML_HEADS, ML_QK, ML_V), F32),
                   jax.ShapeDtypeStruct((n_seq, ML_HEADS, ML_QK), F32),
                   jax.ShapeDtypeStruct((n_seq, 1, ML_HEADS), F32)],
        compiler_params=_cparams(1),
        name="mlstm_sample",
    )(z, x, bias, og, tril, wo, c0, n0, m0)


def _topk_rows(s, k, payload=None):
    n = s.shape[0]
    rows = lax.broadcasted_iota(jnp.int32, s.shape, 0)
    vals, picks = [], []
    for _ in range(k):
        m = jnp.max(s, axis=0, keepdims=True)
        first = jnp.min(jnp.where(s == m, rows, n), axis=0, keepdims=True)
        sel = rows == first
        vals.append(m)
        if payload is None:
            picks.append(first)
        else:
            picks.append(jnp.max(jnp.where(sel, payload, -1), axis=0, keepdims=True))
        s = jnp.where(sel, -jnp.inf, s)
    return jnp.concatenate(vals, axis=0), jnp.concatenate(picks, axis=0)


def _peer_select_kernel(x_ref, g_ref, wq_ref, sk_ref, xn_ref, e_ref, gate_ref, q_scr):
    h = pl.program_id(1)

    @pl.when(h == 0)
    def _():
        xn = _rmsnorm(x_ref[...], g_ref[...]).astype(BF16)
        xn_ref[...] = xn
        q = jnp.dot(xn, wq_ref[...], preferred_element_type=F32)
        for c in range(2 * PEER_HEADS):
            q_scr[c] = q[:, c * LANES:(c + 1) * LANES]

    vals, idxs = [], []
    for p in range(2):
        qhp = q_scr[2 * h + p].astype(BF16)
        st = lax.dot_general(sk_ref[0, p], qhp, NT_DIMS, preferred_element_type=F32)
        v_, i_ = _topk_rows(st, PEER_TOPK)
        vals.append(v_)
        idxs.append(i_)
    cand = jnp.concatenate([vals[0][k1:k1 + 1] + vals[1] for k1 in range(PEER_TOPK)], axis=0)
    expert = jnp.concatenate([idxs[0][k1:k1 + 1] * N_KEYS + idxs[1] for k1 in range(PEER_TOPK)], axis=0)
    best, e_sel = _topk_rows(cand, PEER_TOPK, payload=expert)
    ex = jnp.exp(best - best[0:1])
    e_ref[...] = e_sel
    gate_ref[...] = ex / jnp.sum(ex, axis=0, keepdims=True)


def _peer_select(x, g, wq, sk, t=256):
    n, d = x.shape
    nslot = PEER_HEADS * PEER_TOPK
    return pl.pallas_call(
        _peer_select_kernel,
        grid=(n // t, PEER_HEADS),
        in_specs=[pl.BlockSpec((t, d), lambda i, h: (i, 0)),
                  pl.BlockSpec((1, d), lambda i, h: (0, 0)),
                  pl.BlockSpec((d, 2 * PEER_HEADS * LANES), lambda i, h: (0, 0)),
                  pl.BlockSpec((1, 2, N_KEYS, LANES), lambda i, h: (h, 0, 0, 0))],
        out_specs=[pl.BlockSpec((t, d), lambda i, h: (i, 0)),
                   pl.BlockSpec((PEER_TOPK, t), lambda i, h: (h, i)),
                   pl.BlockSpec((PEER_TOPK, t), lambda i, h: (h, i))],
        out_shape=[jax.ShapeDtypeStruct((n, d), BF16),
                   jax.ShapeDtypeStruct((nslot, n), jnp.int32),
                   jax.ShapeDtypeStruct((nslot, n), F32)],
        scratch_shapes=[pltpu.VMEM((2 * PEER_HEADS, t, LANES), F32)],
        compiler_params=_cparams(2),
        name="peer_select",
    )(x, g.reshape(1, d), wq, sk)


def _gelu(x):
    return 0.5 * x * (1.0 + lax.erf(x * 0.7071067811865476))


def _peer_main_kernel(xn_ref, e_ref, gate_ref, x_ref, u_ref, v_ref, o_ref,
                      a_scr, b_scr, g_scr, gmat, acc, *, a_per_step):
    e = pl.program_id(1)
    t = xn_ref.shape[0]

    @pl.when(e == 0)
    def _():
        ef = e_ref[...]
        a_scr[...] = (ef >> 7).astype(F32).T
        b_scr[...] = (ef & (N_KEYS - 1)).astype(F32).T
        g_scr[...] = gate_ref[...].T
        acc[...] = jnp.zeros_like(acc)
        key_rows = lax.broadcasted_iota(jnp.int32, (N_KEYS, LANES), 0).astype(F32)

        def body(n, carry):
            arow = a_scr[pl.ds(n, 1), :]
            brow = b_scr[pl.ds(n, 1), :]
            grow = g_scr[pl.ds(n, 1), :]
            pa = jnp.where(key_rows == arow, 1.0, 0.0).astype(BF16)
            qb = jnp.where(key_rows == brow, grow, 0.0).astype(BF16)
            tile = lax.dot_general(pa, qb, NT_DIMS, preferred_element_type=F32)
            gmat[pl.ds(pl.multiple_of(n * G_PITCH, SUBLANES), N_KEYS), :] = tile
            return carry

        lax.fori_loop(0, t, body, 0)

    act = lax.dot_general(xn_ref[...], u_ref[...], NT_DIMS, preferred_element_type=F32)
    gates = jnp.concatenate(
        [gmat[pl.ds(e * a_per_step + i, t, stride=G_PITCH), :] for i in range(a_per_step)], axis=1)
    coef = (_gelu(act) * gates).astype(BF16)
    acc[...] += jnp.dot(coef, v_ref[...], preferred_element_type=F32)

    @pl.when(e == pl.num_programs(1) - 1)
    def _():
        o_ref[...] = x_ref[...] + acc[...]


def _peer_main(xn, e_sel, gate, x, u, v, t=256, a_per_step=2):
    n, d = x.shape
    nslot = PEER_HEADS * PEER_TOPK
    eb = a_per_step * N_KEYS
    return pl.pallas_call(
        functools.partial(_peer_main_kernel, a_per_step=a_per_step),
        grid=(n // t, N_KEYS // a_per_step),
        in_specs=[pl.BlockSpec((t, d), lambda i, e: (i, 0)),
                  pl.BlockSpec((nslot, t), lambda i, e: (0, i)),
                  pl.BlockSpec((nslot, t), lambda i, e: (0, i)),
                  pl.BlockSpec((t, d), lambda i, e: (i, 0)),
                  pl.BlockSpec((eb, d), lambda i, e: (e, 0)),
                  pl.BlockSpec((eb, d), lambda i, e: (e, 0))],
        out_specs=pl.BlockSpec((t, d), lambda i, e: (i, 0)),
        out_shape=jax.ShapeDtypeStruct((n, d), F32),
        scratch_shapes=[pltpu.VMEM((t, nslot), F32), pltpu.VMEM((t, nslot), F32), pltpu.VMEM((t, nslot), F32),
                        pltpu.VMEM((t * G_PITCH, LANES), F32), pltpu.VMEM((t, d), F32)],
        compiler_params=_cparams(2),
        name="peer_main",
    )(xn, e_sel, gate, x, u, v)


def _peer(x, g, wq, sk, u, v):
    xn, e_sel, gate = _peer_select(x, g, wq, sk)
    return _peer_main(xn, e_sel, gate, x, u, v)


def _rope_tables(pos):
    half = HEAD_DIM // 2
    inv_freq = ROPE_THETA ** (-jnp.arange(half, dtype=F32) / half)
    ang = pos.astype(F32)[:, None] * inv_freq[None, :]
    cos, sin = jnp.cos(ang), jnp.sin(ang)
    reps = LANES // HEAD_DIM
    cos_t = jnp.tile(jnp.concatenate([cos, cos], axis=1), (1, reps))
    sin_t = jnp.tile(jnp.concatenate([-sin, sin], axis=1), (1, reps))
    return cos_t, sin_t


def kernel(x_prompt, x_sample, cache_conv, cache_win_k, cache_win_v, state_mlstm_C, state_mlstm_n,
           state_mlstm_m, norm_mix, norm_ffn, ab_w_in, ab_conv_w, ab_q_gain, ab_k_gain, ab_sinks, ab_w_out,
           ml_w_in, ml_gate_bias, ml_out_gain, ml_w_out, peer_w_q, peer_sub_keys, peer_u, peer_v):
    n_batch, seq, d = x_prompt.shape
    n_seq, t_len, _ = x_sample.shape
    n_prompt = n_batch * seq
    assert d == D_MODEL and t_len == SUBLANES and norm_mix.shape[0] == 2

    x = jnp.concatenate([x_prompt.reshape(n_prompt, d), x_sample.reshape(n_seq * t_len, d)], axis=0)

    cos_p, sin_p = _rope_tables(jnp.arange(seq, dtype=jnp.int32))
    cos_s, sin_s = _rope_tables(PAST_LEN + jnp.arange(t_len, dtype=jnp.int32))
    bt = 16
    cos_s, sin_s = jnp.tile(cos_s, (bt, 1)), jnp.tile(sin_s, (bt, 1))
    lane = jnp.arange(LANES)
    seg = (lane[:, None] // HEAD_DIM == lane[None, :] // HEAD_DIM).astype(BF16)
    reps = LANES // HEAD_DIM
    qg = jnp.tile(ab_q_gain[0], reps).reshape(1, LANES)
    kg = jnp.tile(ab_k_gain[0], reps).reshape(1, LANES)
    wo_ab = ab_w_out[0].astype(BF16)

    z = _norm_proj(x, norm_mix[0], ab_w_in[0].astype(BF16))
    y_p, k_p, v_p, c_p = _ab_prompt(z, x, n_batch, seq, cos_p, sin_p, ab_conv_w[0], qg, kg, ab_sinks[0],
                                    seg, wo_ab)
    y_s, c_s, k_s, v_s = _ab_sample(z, x, n_prompt, n_seq, t_len, cos_s, sin_s, ab_conv_w[0], qg, kg,
                                    ab_sinks[0], seg, wo_ab, cache_conv[0],
                                    cache_win_k[0].reshape(n_seq, WINDOW, LANES),
                                    cache_win_v[0].reshape(n_seq, WINDOW, LANES), bt=bt)
    x = jnp.concatenate([y_p, y_s], axis=0)
    x = _peer(x, norm_ffn[0], peer_w_q[0].astype(BF16), peer_sub_keys[0].astype(BF16),
              peer_u[0].astype(BF16), peer_v[0].astype(BF16))

    n_gate = 2 * ML_HEADS
    w_in = jnp.pad(ml_w_in[0], ((0, 0), (0, ML_IN_PAD - ml_w_in.shape[2]))).astype(BF16)
    bias = jnp.pad(ml_gate_bias[0], (0, LANES - n_gate)).reshape(1, LANES)
    og = ml_out_gain[0].reshape(1, D_MODEL)
    idx = jnp.arange(ML_CHUNK)
    tril = (idx[None, :] <= idx[:, None]).astype(BF16)
    wo_ml = ml_w_out[0].astype(BF16)

    z = _norm_proj(x, norm_mix[1], w_in)
    y_p, cm_p, nm_p, mm_p = _mlstm_prompt(z, x, n_batch, seq, bias, og, tril, wo_ml)
    y_s, cm_s, nm_s, mm_s = _mlstm_sample(z, x, n_prompt, n_seq, t_len, bias, og, tril, wo_ml,
                                          state_mlstm_C[0], state_mlstm_n[0],
                                          state_mlstm_m[0].reshape(n_seq, 1, ML_HEADS))
    x = jnp.concatenate([y_p, y_s], axis=0)
    x = _peer(x, norm_ffn[1], peer_w_q[1].astype(BF16), peer_sub_keys[1].astype(BF16),
              peer_u[1].astype(BF16), peer_v[1].astype(BF16))

    y_prompt = x[:n_prompt].reshape(n_batch, seq, d)
    y_sample = x[n_prompt:].reshape(n_seq, t_len, d)
    kv_shape_p = (1, n_batch, WINDOW, N_KV_HEADS, HEAD_DIM)
    kv_shape_s = (1, n_seq, WINDOW, N_KV_HEADS, HEAD_DIM)
    return (y_prompt, y_sample,
            c_p[:, SUBLANES - 2:, :][None], k_p.reshape(kv_shape_p), v_p.reshape(kv_shape_p),
            cm_p[None], nm_p[None], mm_p.reshape(1, n_batch, ML_HEADS),
            c_s[None], k_s.reshape(kv_shape_s), v_s.reshape(kv_shape_s),
            cm_s[None], nm_s[None], mm_s.reshape(1, n_seq, ML_HEADS))
```

```python
import functools

import jax
import jax.numpy as jnp
from jax import lax
from jax.experimental import pallas as pl
from jax.experimental.pallas import tpu as pltpu

F32 = jnp.float32
BF16 = jnp.bfloat16
EPS = 1e-6

D_MODEL = 1024
CONV_DIM = 512
N_Q_HEADS = 8
N_KV_HEADS = 2
HEAD_DIM = 64
WINDOW = 128
ROPE_THETA = 10000.0
AB_IN = 2304
ML_HEADS = 4
ML_QK = 128
ML_V = 256
ML_CHUNK = 128
ML_GATE_COL = 3072
ML_IN_PAD = ML_GATE_COL + 128
N_KEYS = 128
PEER_HEADS = 8
PEER_TOPK = 16
PAST_LEN = 16384

LANES = 128
SUBLANES = 8
G_PITCH = N_KEYS // 2 + SUBLANES
VMEM_LIMIT = 56 * 1024 * 1024

NT_DIMS = (((1,), (1,)), ((), ()))
TN_DIMS = (((0,), (0,)), ((), ()))


def _cparams(n_axes, vmem=VMEM_LIMIT):
    return pltpu.CompilerParams(dimension_semantics=("arbitrary",) * n_axes, vmem_limit_bytes=vmem)


def _rmsnorm(x, g):
    return x * lax.rsqrt(jnp.mean(x * x, axis=-1, keepdims=True) + EPS) * g


def _bf16_pieces(a, terms):
    pieces = []
    rem = a
    for _ in range(terms):
        piece = rem.astype(BF16)
        rem = rem - piece.astype(F32)
        pieces.append(piece)
    return pieces


def _split_dot(a, b_bf16, terms=2):
    return sum(jnp.dot(p, b_bf16, preferred_element_type=F32) for p in _bf16_pieces(a, terms))


def _norm_proj_kernel(x_ref, g_ref, w_ref, o_ref):
    r = _rmsnorm(x_ref[...], g_ref[...])
    o_ref[...] = jnp.dot(r.astype(BF16), w_ref[...], preferred_element_type=F32)


def _norm_proj(x, g, w_bf16, tm=512):
    n, d = x.shape
    nout = w_bf16.shape[1]
    return pl.pallas_call(
        _norm_proj_kernel,
        grid=(n // tm,),
        in_specs=[pl.BlockSpec((tm, d), lambda i: (i, 0)),
                  pl.BlockSpec((1, d), lambda i: (0, 0)),
                  pl.BlockSpec((d, nout), lambda i: (0, 0))],
        out_specs=pl.BlockSpec((tm, nout), lambda i: (i, 0)),
        out_shape=jax.ShapeDtypeStruct((n, nout), F32),
        compiler_params=_cparams(1),
        name="norm_proj",
    )(x, g.reshape(1, d), w_bf16)


def _headnorm_rope(xc, gain, cos, sin, seg, hi_half):
    ss = _split_dot(xc * xc, seg)
    xn = xc * lax.rsqrt(ss * (1.0 / HEAD_DIM) + EPS) * gain
    partner = jnp.where(hi_half, pltpu.roll(xn, 32, 1), pltpu.roll(xn, 96, 1))
    return xn * cos + partner * sin


def _softmax_sink(s, mask, sink):
    s = jnp.where(mask, s, -1e30)
    m = jnp.maximum(jnp.max(s, axis=-1, keepdims=True), sink)
    p = jnp.exp(s - m)
    denom = jnp.sum(p, axis=-1, keepdims=True) + jnp.exp(sink - m)
    return (p / denom).astype(BF16)


def _ab_prompt_kernel(sink_ref, z_ref, x_ref, cos_ref, sin_ref, cw_ref, qg_ref, kg_ref, seg_ref, wo_ref,
                      y_ref, kst_ref, vst_ref, cst_ref, pk_ref, pv_ref, pu_ref):
    j = pl.program_id(1)
    blk = z_ref.shape[0]

    @pl.when(j == 0)
    def _():
        pk_ref[...] = jnp.zeros_like(pk_ref)
        pv_ref[...] = jnp.zeros_like(pv_ref)
        pu_ref[...] = jnp.zeros_like(pu_ref)

    cos = cos_ref[...]
    sin = sin_ref[...]
    seg = seg_ref[...]
    hi_half = (lax.broadcasted_iota(jnp.int32, (blk, LANES), 1) & 32) != 0

    gate_b = z_ref[:, 0:CONV_DIM]
    u = z_ref[:, CONV_DIM:2 * CONV_DIM] * z_ref[:, 2 * CONV_DIM:3 * CONV_DIM]
    ng = blk // SUBLANES
    u3 = u.reshape(ng, SUBLANES, CONV_DIM)
    ext = jnp.concatenate([pu_ref[...][None], u3], axis=0)
    t8 = lax.broadcasted_iota(jnp.int32, (ng, SUBLANES, CONV_DIM), 1)
    r1 = pltpu.roll(ext, 1, 1)
    r2 = pltpu.roll(ext, 2, 1)
    um1 = jnp.where(t8 >= 1, r1[1:], r1[:-1])
    um2 = jnp.where(t8 >= 2, r2[1:], r2[:-1])
    cw = cw_ref[...]
    conv = cw[0:1][None] * um2 + cw[1:2][None] * um1 + cw[2:3][None] * u3
    yconv = gate_b * conv.reshape(blk, CONV_DIM)

    q0 = 3 * CONV_DIM
    k0 = q0 + N_Q_HEADS * HEAD_DIM
    v0 = k0 + N_KV_HEADS * HEAD_DIM
    qg = qg_ref[...]
    qr = [_headnorm_rope(z_ref[:, q0 + c * LANES:q0 + (c + 1) * LANES], qg, cos, sin, seg, hi_half)
          for c in range(N_Q_HEADS * HEAD_DIM // LANES)]
    kr = _headnorm_rope(z_ref[:, k0:k0 + LANES], kg_ref[...], cos, sin, seg, hi_half)
    v = z_ref[:, v0:v0 + LANES]
    pk = pk_ref[...]
    pv = pv_ref[...]

    row = lax.broadcasted_iota(jnp.int32, (blk, 2 * blk), 0)
    col = lax.broadcasted_iota(jnp.int32, (blk, 2 * blk), 1)
    row_prev = row + jnp.where(j == 0, 2 * blk, 0)
    mask = ((col < blk) & (col > row_prev)) | ((col >= blk) & (col - blk <= row))

    kks, vvs = [], []
    for g in range(N_KV_HEADS):
        sl = slice(g * HEAD_DIM, (g + 1) * HEAD_DIM)
        kks.append(jnp.concatenate([pk[:, sl], kr[:, sl]], axis=0).astype(BF16))
        vvs.append(jnp.concatenate([pv[:, sl], v[:, sl]], axis=0).astype(BF16))
    outs = []
    for h in range(N_Q_HEADS):
        g = h // (N_Q_HEADS // N_KV_HEADS)
        qh = qr[h // 2][:, (h % 2) * HEAD_DIM:(h % 2 + 1) * HEAD_DIM].astype(BF16)
        s = lax.dot_general(qh, kks[g], NT_DIMS, preferred_element_type=F32) * (HEAD_DIM ** -0.5)
        p = _softmax_sink(s, mask, sink_ref[h])
        outs.append(jnp.dot(p, vvs[g], preferred_element_type=F32))
    attn = jnp.concatenate(outs, axis=1)

    y = (jnp.dot(yconv.astype(BF16), wo_ref[0:CONV_DIM, :], preferred_element_type=F32)
         + jnp.dot(attn.astype(BF16), wo_ref[CONV_DIM:2 * CONV_DIM, :], preferred_element_type=F32))
    y_ref[...] = y + x_ref[...]

    pk_ref[...] = kr
    pv_ref[...] = v
    pu_ref[...] = u3[ng - 1]
    kst_ref[0] = kr
    vst_ref[0] = v
    cst_ref[0] = u3[ng - 1]


def _ab_prompt(z, x, n_batch, seq, cos, sin, cw, qg, kg, sinks, seg, wo):
    blk = WINDOW
    nb = seq // blk
    tok = lambda b, j, s: (b * nb + j, 0)
    const = lambda b, j, s: (0, 0)
    state = lambda b, j, s: (b, 0, 0)
    grid_spec = pltpu.PrefetchScalarGridSpec(
        num_scalar_prefetch=1,
        grid=(n_batch, nb),
        in_specs=[pl.BlockSpec((blk, AB_IN), tok),
                  pl.BlockSpec((blk, D_MODEL), tok),
                  pl.BlockSpec((blk, LANES), lambda b, j, s: (j, 0)),
                  pl.BlockSpec((blk, LANES), lambda b, j, s: (j, 0)),
                  pl.BlockSpec((3, CONV_DIM), const),
                  pl.BlockSpec((1, LANES), const),
                  pl.BlockSpec((1, LANES), const),
                  pl.BlockSpec((LANES, LANES), const),
                  pl.BlockSpec((D_MODEL, D_MODEL), const)],
        out_specs=[pl.BlockSpec((blk, D_MODEL), tok),
                   pl.BlockSpec((1, blk, LANES), state),
                   pl.BlockSpec((1, blk, LANES), state),
                   pl.BlockSpec((1, SUBLANES, CONV_DIM), state)],
        scratch_shapes=[pltpu.VMEM((blk, LANES), F32), pltpu.VMEM((blk, LANES), F32),
                        pltpu.VMEM((SUBLANES, CONV_DIM), F32)])
    return pl.pallas_call(
        _ab_prompt_kernel,
        grid_spec=grid_spec,
        out_shape=[jax.ShapeDtypeStruct((n_batch * seq, D_MODEL), F32),
                   jax.ShapeDtypeStruct((n_batch, blk, LANES), F32),
                   jax.ShapeDtypeStruct((n_batch, blk, LANES), F32),
                   jax.ShapeDtypeStruct((n_batch, SUBLANES, CONV_DIM), F32)],
        compiler_params=_cparams(2),
        name="ab_prompt",
    )(sinks, z, x, cos, sin, cw, qg, kg, seg, wo)


def _ab_sample_kernel(sink_ref, z_ref, x_ref, cos_ref, sin_ref, cw_ref, qg_ref, kg_ref, seg_ref, wo_ref,
                      cc_ref, ck_ref, cv_ref, y_ref, cs_ref, ks_ref, vs_ref):
    rows = z_ref.shape[0]
    t_len = SUBLANES
    bt = rows // t_len
    cos = cos_ref[...]
    sin = sin_ref[...]
    seg = seg_ref[...]
    hi_half = (lax.broadcasted_iota(jnp.int32, (rows, LANES), 1) & 32) != 0

    gate_b = z_ref[:, 0:CONV_DIM]
    u = z_ref[:, CONV_DIM:2 * CONV_DIM] * z_ref[:, 2 * CONV_DIM:3 * CONV_DIM]
    u3 = u.reshape(bt, t_len, CONV_DIM)
    cc = cc_ref[...]
    c0 = cc[:, 0:1, :]
    c1 = cc[:, 1:2, :]
    t8 = lax.broadcasted_iota(jnp.int32, (bt, t_len, CONV_DIM), 1)
    r1 = pltpu.roll(u3, 1, 1)
    r2 = pltpu.roll(u3, 2, 1)
    um1 = jnp.where(t8 >= 1, r1, c1)
    um2 = jnp.where(t8 >= 2, r2, jnp.where(t8 == 1, c1, c0))
    cw = cw_ref[...]
    conv = cw[0:1][None] * um2 + cw[1:2][None] * um1 + cw[2:3][None] * u3
    yconv = gate_b * conv.reshape(rows, CONV_DIM)
    cs_ref[...] = r2[:, 0:2, :]

    q0 = 3 * CONV_DIM
    k0 = q0 + N_Q_HEADS * HEAD_DIM
    v0 = k0 + N_KV_HEADS * HEAD_DIM
    qg = qg_ref[...]
    qr = [_headnorm_rope(z_ref[:, q0 + c * LANES:q0 + (c + 1) * LANES], qg, cos, sin, seg, hi_half)
          for c in range(N_Q_HEADS * HEAD_DIM // LANES)]
    kr = _headnorm_rope(z_ref[:, k0:k0 + LANES], kg_ref[...], cos, sin, seg, hi_half)
    v = z_ref[:, v0:v0 + LANES]

    group = N_Q_HEADS // N_KV_HEADS
    nq = group * t_len
    nk = 2 * WINDOW
    qrow = lax.broadcasted_iota(jnp.int32, (nq, nk), 0)
    t_q = qrow & (t_len - 1)
    col = lax.broadcasted_iota(jnp.int32, (nq, nk), 1)
    mask = (((col < WINDOW) & (col > t_q)) | ((col >= WINDOW) & (col - WINDOW <= t_q)))[None]
    hrow = lax.broadcasted_iota(jnp.int32, (nq, 1), 0) // t_len
    pad = jnp.zeros((bt, nk - WINDOW - t_len, HEAD_DIM), F32)

    outs = [None] * N_Q_HEADS
    for g in range(N_KV_HEADS):
        sl = slice(g * HEAD_DIM, (g + 1) * HEAD_DIM)
        qs = jnp.concatenate(
            [qr[h // 2][:, (h % 2) * HEAD_DIM:(h % 2 + 1) * HEAD_DIM].reshape(bt, t_len, HEAD_DIM)
             for h in range(g * group, (g + 1) * group)], axis=1)
        kk = jnp.concatenate([ck_ref[:, :, sl], kr[:, sl].reshape(bt, t_len, HEAD_DIM), pad], axis=1)
        vv = jnp.concatenate([cv_ref[:, :, sl], v[:, sl].reshape(bt, t_len, HEAD_DIM), pad], axis=1)
        s = jnp.einsum('bqd,bkd->bqk', qs.astype(BF16), kk.astype(BF16),
                       preferred_element_type=F32) * (HEAD_DIM ** -0.5)
        sink = jnp.zeros((nq, 1), F32)
        for hh in range(group):
            sink = jnp.where(hrow == hh, sink_ref[g * group + hh], sink)
        p = _softmax_sink(s, mask, sink[None])
        o = jnp.einsum('bqk,bkd->bqd', p, vv.astype(BF16), preferred_element_type=F32)
        for hh in range(group):
            outs[g * group + hh] = o[:, hh * t_len:(hh + 1) * t_len, :].reshape(rows, HEAD_DIM)
    attn = jnp.concatenate(outs, axis=1)

    y = (jnp.dot(yconv.astype(BF16), wo_ref[0:CONV_DIM, :], preferred_element_type=F32)
         + jnp.dot(attn.astype(BF16), wo_ref[CONV_DIM:2 * CONV_DIM, :], preferred_element_type=F32))
    y_ref[...] = y + x_ref[...]

    keep = WINDOW - t_len
    ks_ref[:, 0:keep, :] = ck_ref[:, t_len:WINDOW, :]
    ks_ref[:, keep:WINDOW, :] = kr.reshape(bt, t_len, LANES)
    vs_ref[:, 0:keep, :] = cv_ref[:, t_len:WINDOW, :]
    vs_ref[:, keep:WINDOW, :] = v.reshape(bt, t_len, LANES)


def _ab_sample(z, x, row0, n_seq, t_len, cos, sin, cw, qg, kg, sinks, seg, wo, cc, ck, cv, bt=16):
    rows = bt * t_len
    blk0 = row0 // rows
    tok = lambda i, s: (blk0 + i, 0)
    const = lambda i, s: (0, 0)
    seq3 = lambda i, s: (i, 0, 0)
    grid_spec = pltpu.PrefetchScalarGridSpec(
        num_scalar_prefetch=1,
        grid=(n_seq // bt,),
        in_specs=[pl.BlockSpec((rows, AB_IN), tok),
                  pl.BlockSpec((rows, D_MODEL), tok),
                  pl.BlockSpec((rows, LANES), const),
                  pl.BlockSpec((rows, LANES), const),
                  pl.BlockSpec((3, CONV_DIM), const),
                  pl.BlockSpec((1, LANES), const),
                  pl.BlockSpec((1, LANES), const),
                  pl.BlockSpec((LANES, LANES), const),
                  pl.BlockSpec((D_MODEL, D_MODEL), const),
                  pl.BlockSpec((bt, 2, CONV_DIM), seq3),
                  pl.BlockSpec((bt, WINDOW, LANES), seq3),
                  pl.BlockSpec((bt, WINDOW, LANES), seq3)],
        out_specs=[pl.BlockSpec((rows, D_MODEL), lambda i, s: (i, 0)),
                   pl.BlockSpec((bt, 2, CONV_DIM), seq3),
                   pl.BlockSpec((bt, WINDOW, LANES), seq3),
                   pl.BlockSpec((bt, WINDOW, LANES), seq3)])
    return pl.pallas_call(
        _ab_sample_kernel,
        grid_spec=grid_spec,
        out_shape=[jax.ShapeDtypeStruct((n_seq * t_len, D_MODEL), F32),
                   jax.ShapeDtypeStruct((n_seq, 2, CONV_DIM), F32),
                   jax.ShapeDtypeStruct((n_seq, WINDOW, LANES), F32),
                   jax.ShapeDtypeStruct((n_seq, WINDOW, LANES), F32)],
        compiler_params=_cparams(1),
        name="ab_sample",
    )(sinks, z, x, cos, sin, cw, qg, kg, seg, wo, cc, ck, cv)


def _log_sigmoid(x):
    return jnp.minimum(x, 0.0) - jnp.log(1.0 + jnp.exp(-jnp.abs(x)))


def _mlstm_chunk(z, x, bias, og, tril, wo, c_src, n_src, m_src, c_dst, n_dst, m_dst, n_real):
    L = z.shape[0]
    gates = z[:, ML_GATE_COL:ML_GATE_COL + LANES] + bias
    if n_real < L:
        live = lax.broadcasted_iota(jnp.int32, (L, LANES), 0) < n_real
        li_all = jnp.where(live, gates, -1e30)
        lf_all = jnp.where(live, _log_sigmoid(gates), 0.0)
    else:
        li_all = gates
        lf_all = _log_sigmoid(gates)
    lf_pieces = _bf16_pieces(lf_all, 3)
    f_col_all = sum(jnp.dot(tril, p, preferred_element_type=F32) for p in lf_pieces)
    f_row_all = sum(lax.dot_general(p, tril, (((0,), (1,)), ((), ())), preferred_element_type=F32)
                    for p in lf_pieces)
    li_t = li_all.T
    rr = lax.broadcasted_iota(jnp.int32, (L, L), 0)
    cc = lax.broadcasted_iota(jnp.int32, (L, L), 1)
    causal = cc <= rr

    outs, m_new_all = [], []
    for h in range(ML_HEADS):
        f_col = f_col_all[:, ML_HEADS + h:ML_HEADS + h + 1]
        f_row = f_row_all[ML_HEADS + h:ML_HEADS + h + 1, :]
        li_row = li_t[h:h + 1, :]
        li_col = li_all[:, h:h + 1]
        m0 = m_src[0, 0:1, h:h + 1]
        c0 = c_src[0, h]
        n0 = n_src[0, h:h + 1, :]
        qh = z[:, h * ML_QK:(h + 1) * ML_QK]
        kh = z[:, ML_HEADS * ML_QK + h * ML_QK:ML_HEADS * ML_QK + (h + 1) * ML_QK] * (ML_QK ** -0.5)
        v_off = 2 * ML_HEADS * ML_QK
        vh = z[:, v_off + h * ML_V:v_off + (h + 1) * ML_V]
        o_off = v_off + ML_HEADS * ML_V
        oh = z[:, o_off + h * ML_V:o_off + (h + 1) * ML_V]
        qb = qh.astype(BF16)
        vb = vh.astype(BF16)

        dmat = jnp.where(causal, f_col - f_row + li_row, -jnp.inf)
        gcar = f_col + m0
        m_t = jnp.maximum(jnp.max(dmat, axis=-1, keepdims=True), gcar)
        w = jnp.exp(dmat - m_t)
        s = lax.dot_general(qb, kh.astype(BF16), NT_DIMS, preferred_element_type=F32) * w
        carry = jnp.exp(gcar - m_t)
        num = (jnp.dot(s.astype(BF16), vb, preferred_element_type=F32)
               + jnp.dot(qb, c0.astype(BF16), preferred_element_type=F32) * carry)
        den = jnp.sum(s, axis=-1, keepdims=True) + carry * jnp.sum(qh * n0, axis=-1, keepdims=True)
        hout = num / jnp.maximum(jnp.abs(den), jnp.exp(-m_t))

        f_last = f_col[L - 1:L, :]
        w_end = f_last - f_col + li_col
        m_new = jnp.maximum(f_last + m0, jnp.max(w_end, axis=0, keepdims=True))
        a_end = jnp.exp(w_end - m_new)
        scale = jnp.exp(f_last + m0 - m_new)
        ka = kh * a_end
        c_dst[0, h] = scale * c0 + lax.dot_general(ka.astype(BF16), vb, TN_DIMS, preferred_element_type=F32)
        n_dst[0, h:h + 1, :] = scale * n0 + jnp.sum(ka, axis=0, keepdims=True)
        m_new_all.append(m_new)

        hn = _rmsnorm(hout, og[:, h * ML_V:(h + 1) * ML_V])
        outs.append(jax.nn.sigmoid(oh) * hn)
    m_dst[0] = jnp.concatenate(m_new_all, axis=1)
    out = jnp.concatenate(outs, axis=1)
    return jnp.dot(out.astype(BF16), wo, preferred_element_type=F32) + x


def _mlstm_prompt_kernel(z_ref, x_ref, bias_ref, og_ref, tril_ref, wo_ref, y_ref, c_ref, n_ref, m_ref):
    @pl.when(pl.program_id(1) == 0)
    def _():
        c_ref[...] = jnp.zeros_like(c_ref)
        n_ref[...] = jnp.zeros_like(n_ref)
        m_ref[...] = jnp.zeros_like(m_ref)

    y_ref[...] = _mlstm_chunk(z_ref[...], x_ref[...], bias_ref[...], og_ref[...], tril_ref[...], wo_ref[...],
                              c_ref, n_ref, m_ref, c_ref, n_ref, m_ref, ML_CHUNK)


def _mlstm_sample_kernel(z_ref, x_ref, bias_ref, og_ref, tril_ref, wo_ref, c0_ref, n0_ref, m0_ref,
                         y_ref, c_ref, n_ref, m_ref):
    t_len = z_ref.shape[0]
    zpad = jnp.concatenate([z_ref[...], jnp.zeros((ML_CHUNK - t_len, ML_IN_PAD), F32)], axis=0)
    xpad = jnp.concatenate([x_ref[...], jnp.zeros((ML_CHUNK - t_len, D_MODEL), F32)], axis=0)
    y = _mlstm_chunk(zpad, xpad, bias_ref[...], og_ref[...], tril_ref[...], wo_ref[...],
                     c0_ref, n0_ref, m0_ref, c_ref, n_ref, m_ref, t_len)
    y_ref[...] = y[0:t_len, :]


def _mlstm_weight_specs(const):
    return [pl.BlockSpec((1, LANES), const),
            pl.BlockSpec((1, D_MODEL), const),
            pl.BlockSpec((ML_CHUNK, ML_CHUNK), const),
            pl.BlockSpec((D_MODEL, D_MODEL), const)]


def _mlstm_prompt(z, x, n_batch, seq, bias, og, tril, wo):
    nc = seq // ML_CHUNK
    tok = lambda b, j: (b * nc + j, 0)
    const = lambda b, j: (0, 0)
    return pl.pallas_call(
        _mlstm_prompt_kernel,
        grid=(n_batch, nc),
        in_specs=[pl.BlockSpec((ML_CHUNK, ML_IN_PAD), tok),
                  pl.BlockSpec((ML_CHUNK, D_MODEL), tok)] + _mlstm_weight_specs(const),
        out_specs=[pl.BlockSpec((ML_CHUNK, D_MODEL), tok),
                   pl.BlockSpec((1, ML_HEADS, ML_QK, ML_V), lambda b, j: (b, 0, 0, 0)),
                   pl.BlockSpec((1, ML_HEADS, ML_QK), lambda b, j: (b, 0, 0)),
                   pl.BlockSpec((1, 1, ML_HEADS), lambda b, j: (b, 0, 0))],
        out_shape=[jax.ShapeDtypeStruct((n_batch * seq, D_MODEL), F32),
                   jax.ShapeDtypeStruct((n_batch, ML_HEADS, ML_QK, ML_V), F32),
                   jax.ShapeDtypeStruct((n_batch, ML_HEADS, ML_QK), F32),
                   jax.ShapeDtypeStruct((n_batch, 1, ML_HEADS), F32)],
        compiler_params=_cparams(2),
        name="mlstm_prompt",
    )(z, x, bias, og, tril, wo)


def _mlstm_sample(z, x, row0, n_seq, t_len, bias, og, tril, wo, c0, n0, m0):
    blk0 = row0 // t_len
    tok = lambda i: (blk0 + i, 0)
    const = lambda i: (0, 0)
    st4 = lambda i: (i, 0, 0, 0)
    st3 = lambda i: (i, 0, 0)
    state_specs = [pl.BlockSpec((1, ML_HEADS, ML_QK, ML_V), st4),
                   pl.BlockSpec((1, ML_HEADS, ML_QK), st3),
                   pl.BlockSpec((1, 1, ML_HEADS), st3)]
    return pl.pallas_call(
        _mlstm_sample_kernel,
        grid=(n_seq,),
        in_specs=[pl.BlockSpec((t_len, ML_IN_PAD), tok),
                  pl.BlockSpec((t_len, D_MODEL), tok)] + _mlstm_weight_specs(const) + state_specs,
        out_specs=[pl.BlockSpec((t_len, D_MODEL), lambda i: (i, 0))] + state_specs,
        out_shape=[jax.ShapeDtypeStruct((n_seq * t_len, D_MODEL), F32),
                   jax.ShapeDtypeStruct((n_seq, ML_HEADS, ML_QK, ML_V), F32),
                   jax.ShapeDtypeStruct((n_seq, ML_HEADS, ML_QK), F32),
                   jax.ShapeDtypeStruct((n_seq, 1, ML_HEADS), F32)],
        compiler_params=_cparams(1),
        name="mlstm_sample",
    )(z, x, bias, og, tril, wo, c0, n0, m0)


def _topk_rows(s, k, payload=None):
    n = s.shape[0]
    rows = lax.broadcasted_iota(jnp.int32, s.shape, 0)
    vals, picks = [], []
    for _ in range(k):
        m = jnp.max(s, axis=0, keepdims=True)
        first = jnp.min(jnp.where(s == m, rows, n), axis=0, keepdims=True)
        sel = rows == first
        vals.append(m)
        if payload is None:
            picks.append(first)
        else:
            picks.append(jnp.max(jnp.where(sel, payload, -1), axis=0, keepdims=True))
        s = jnp.where(sel, -jnp.inf, s)
    return jnp.concatenate(vals, axis=0), jnp.concatenate(picks, axis=0)


def _peer_select_kernel(x_ref, g_ref, wq_ref, sk_ref, xn_ref, e_ref, gate_ref, q_scr):
    h = pl.program_id(1)

    @pl.when(h == 0)
    def _():
        xn = _rmsnorm(x_ref[...], g_ref[...]).astype(BF16)
        xn_ref[...] = xn
        q = jnp.dot(xn, wq_ref[...], preferred_element_type=F32)
        for c in range(2 * PEER_HEADS):
            q_scr[c] = q[:, c * LANES:(c + 1) * LANES]

    vals, idxs = [], []
    for p in range(2):
        qhp = q_scr[2 * h + p].astype(BF16)
        st = lax.dot_general(sk_ref[0, p], qhp, NT_DIMS, preferred_element_type=F32)
        v_, i_ = _topk_rows(st, PEER_TOPK)
        vals.append(v_)
        idxs.append(i_)
    cand = jnp.concatenate([vals[0][k1:k1 + 1] + vals[1] for k1 in range(PEER_TOPK)], axis=0)
    expert = jnp.concatenate([idxs[0][k1:k1 + 1] * N_KEYS + idxs[1] for k1 in range(PEER_TOPK)], axis=0)
    best, e_sel = _topk_rows(cand, PEER_TOPK, payload=expert)
    ex = jnp.exp(best - best[0:1])
    e_ref[...] = e_sel
    gate_ref[...] = ex / jnp.sum(ex, axis=0, keepdims=True)


def _peer_select(x, g, wq, sk, t=256):
    n, d = x.shape
    nslot = PEER_HEADS * PEER_TOPK
    return pl.pallas_call(
        _peer_select_kernel,
        grid=(n // t, PEER_HEADS),
        in_specs=[pl.BlockSpec((t, d), lambda i, h: (i, 0)),
                  pl.BlockSpec((1, d), lambda i, h: (0, 0)),
                  pl.BlockSpec((d, 2 * PEER_HEADS * LANES), lambda i, h: (0, 0)),
                  pl.BlockSpec((1, 2, N_KEYS, LANES), lambda i, h: (h, 0, 0, 0))],
        out_specs=[pl.BlockSpec((t, d), lambda i, h: (i, 0)),
                   pl.BlockSpec((PEER_TOPK, t), lambda i, h: (h, i)),
                   pl.BlockSpec((PEER_TOPK, t), lambda i, h: (h, i))],
        out_shape=[jax.ShapeDtypeStruct((n, d), BF16),
                   jax.ShapeDtypeStruct((nslot, n), jnp.int32),
                   jax.ShapeDtypeStruct((nslot, n), F32)],
        scratch_shapes=[pltpu.VMEM((2 * PEER_HEADS, t, LANES), F32)],
        compiler_params=_cparams(2),
        name="peer_select",
    )(x, g.reshape(1, d), wq, sk)


def _gelu(x):
    return 0.5 * x * (1.0 + lax.erf(x * 0.7071067811865476))


def _peer_main_kernel(xn_ref, e_ref, gate_ref, x_ref, u_ref, v_ref, o_ref,
                      a_scr, b_scr, g_scr, gmat, coef_scr):
    e = pl.program_id(1)
    t = xn_ref.shape[0]
    half = N_KEYS // 2

    @pl.when(e == 0)
    def _():
        ef = e_ref[...]
        a_scr[...] = (ef >> 7).astype(F32).T
        b_scr[...] = (ef & (N_KEYS - 1)).astype(F32).T
        g_scr[...] = gate_ref[...].T
        o_ref[...] = x_ref[...]
        coef_scr[...] = jnp.zeros_like(coef_scr)
        r = lax.broadcasted_iota(jnp.int32, (N_KEYS, LANES), 0)
        a_of_row = jnp.where(r < half, 2 * r, 2 * (r - half) + 1).astype(F32)
        b_of_row = r.astype(F32)

        def body(n8, carry):
            base = pl.multiple_of(n8 * SUBLANES, SUBLANES)
            a8 = a_scr[pl.ds(base, SUBLANES), :]
            b8 = b_scr[pl.ds(base, SUBLANES), :]
            g8 = g_scr[pl.ds(base, SUBLANES), :]
            for i in range(SUBLANES):
                pa = jnp.where(a_of_row == a8[i:i + 1], 1.0, 0.0).astype(BF16)
                qb = jnp.where(b_of_row == b8[i:i + 1], g8[i:i + 1], 0.0).astype(BF16)
                tile = lax.dot_general(pa, qb, NT_DIMS, preferred_element_type=F32)
                lo = lax.bitcast_convert_type(tile[0:half].astype(BF16).astype(F32), jnp.uint32)
                hi = lax.bitcast_convert_type(tile[half:N_KEYS].astype(BF16).astype(F32), jnp.uint32)
                row0 = pl.multiple_of((base + i) * G_PITCH, SUBLANES)
                gmat[pl.ds(row0, half), :] = (lo >> 16) | (hi & jnp.uint32(0xFFFF0000))
            return carry

        lax.fori_loop(0, t // SUBLANES, body, 0)

    o_ref[...] += jnp.dot(coef_scr[...], v_ref[...], preferred_element_type=F32)

    act = lax.dot_general(xn_ref[...], u_ref[...], NT_DIMS, preferred_element_type=F32)
    blk = jnp.minimum(e, pl.num_programs(1) - 2)
    word = gmat[pl.ds(blk, t, stride=G_PITCH), :]
    g_even = lax.bitcast_convert_type(word << 16, F32)
    g_odd = lax.bitcast_convert_type(word & jnp.uint32(0xFFFF0000), F32)
    coef_scr[...] = (_gelu(act) * jnp.concatenate([g_even, g_odd], axis=1)).astype(BF16)


def _peer_main(xn, e_sel, gate, x, u, v, t=512):
    n, d = x.shape
    nslot = PEER_HEADS * PEER_TOPK
    eb = 2 * N_KEYS
    n_blk = N_KEYS // 2
    return pl.pallas_call(
        _peer_main_kernel,
        grid=(n // t, n_blk + 1),
        in_specs=[pl.BlockSpec((t, d), lambda i, e: (i, 0)),
                  pl.BlockSpec((nslot, t), lambda i, e: (0, i)),
                  pl.BlockSpec((nslot, t), lambda i, e: (0, i)),
                  pl.BlockSpec((t, d), lambda i, e: (i, 0)),
                  pl.BlockSpec((eb, d), lambda i, e: (jnp.minimum(e, n_blk - 1), 0)),
                  pl.BlockSpec((eb, d), lambda i, e: (jnp.maximum(e - 1, 0), 0))],
        out_specs=pl.BlockSpec((t, d), lambda i, e: (i, 0)),
        out_shape=jax.ShapeDtypeStruct((n, d), F32),
        scratch_shapes=[pltpu.VMEM((t, nslot), F32), pltpu.VMEM((t, nslot), F32), pltpu.VMEM((t, nslot), F32),
                        pltpu.VMEM((t * G_PITCH, LANES), jnp.uint32), pltpu.VMEM((t, eb), BF16)],
        compiler_params=_cparams(2),
        name="peer_main",
    )(xn, e_sel, gate, x, u, v)


def _peer(x, g, wq, sk, u, v):
    xn, e_sel, gate = _peer_select(x, g, wq, sk)
    return _peer_main(xn, e_sel, gate, x, u, v)


def _rope_tables(pos):
    half = HEAD_DIM // 2
    inv_freq = ROPE_THETA ** (-jnp.arange(half, dtype=F32) / half)
    ang = pos.astype(F32)[:, None] * inv_freq[None, :]
    cos, sin = jnp.cos(ang), jnp.sin(ang)
    reps = LANES // HEAD_DIM
    cos_t = jnp.tile(jnp.concatenate([cos, cos], axis=1), (1, reps))
    sin_t = jnp.tile(jnp.concatenate([-sin, sin], axis=1), (1, reps))
    return cos_t, sin_t


def kernel(x_prompt, x_sample, cache_conv, cache_win_k, cache_win_v, state_mlstm_C, state_mlstm_n,
           state_mlstm_m, norm_mix, norm_ffn, ab_w_in, ab_conv_w, ab_q_gain, ab_k_gain, ab_sinks, ab_w_out,
           ml_w_in, ml_gate_bias, ml_out_gain, ml_w_out, peer_w_q, peer_sub_keys, peer_u, peer_v):
    n_batch, seq, d = x_prompt.shape
    n_seq, t_len, _ = x_sample.shape
    n_prompt = n_batch * seq
    assert d == D_MODEL and t_len == SUBLANES and norm_mix.shape[0] == 2

    x = jnp.concatenate([x_prompt.reshape(n_prompt, d), x_sample.reshape(n_seq * t_len, d)], axis=0)

    cos_p, sin_p = _rope_tables(jnp.arange(seq, dtype=jnp.int32))
    cos_s, sin_s = _rope_tables(PAST_LEN + jnp.arange(t_len, dtype=jnp.int32))
    bt = 16
    cos_s, sin_s = jnp.tile(cos_s, (bt, 1)), jnp.tile(sin_s, (bt, 1))
    lane = jnp.arange(LANES)
    seg = (lane[:, None] // HEAD_DIM == lane[None, :] // HEAD_DIM).astype(BF16)
    reps = LANES // HEAD_DIM
    qg = jnp.tile(ab_q_gain[0], reps).reshape(1, LANES)
    kg = jnp.tile(ab_k_gain[0], reps).reshape(1, LANES)
    wo_ab = ab_w_out[0].astype(BF16)

    z = _norm_proj(x, norm_mix[0], ab_w_in[0].astype(BF16))
    y_p, k_p, v_p, c_p = _ab_prompt(z, x, n_batch, seq, cos_p, sin_p, ab_conv_w[0], qg, kg, ab_sinks[0],
                                    seg, wo_ab)
    y_s, c_s, k_s, v_s = _ab_sample(z, x, n_prompt, n_seq, t_len, cos_s, sin_s, ab_conv_w[0], qg, kg,
                                    ab_sinks[0], seg, wo_ab, cache_conv[0],
                                    cache_win_k[0].reshape(n_seq, WINDOW, LANES),
                                    cache_win_v[0].reshape(n_seq, WINDOW, LANES), bt=bt)
    x = jnp.concatenate([y_p, y_s], axis=0)
    x = _peer(x, norm_ffn[0], peer_w_q[0].astype(BF16), peer_sub_keys[0].astype(BF16),
              peer_u[0].astype(BF16), peer_v[0].astype(BF16))

    n_gate = 2 * ML_HEADS
    w_in = jnp.pad(ml_w_in[0], ((0, 0), (0, ML_IN_PAD - ml_w_in.shape[2]))).astype(BF16)
    bias = jnp.pad(ml_gate_bias[0], (0, LANES - n_gate)).reshape(1, LANES)
    og = ml_out_gain[0].reshape(1, D_MODEL)
    idx = jnp.arange(ML_CHUNK)
    tril = (idx[None, :] <= idx[:, None]).astype(BF16)
    wo_ml = ml_w_out[0].astype(BF16)

    z = _norm_proj(x, norm_mix[1], w_in)
    y_p, cm_p, nm_p, mm_p = _mlstm_prompt(z, x, n_batch, seq, bias, og, tril, wo_ml)
    y_s, cm_s, nm_s, mm_s = _mlstm_sample(z, x, n_prompt, n_seq, t_len, bias, og, tril, wo_ml,
                                          state_mlstm_C[0], state_mlstm_n[0],
                                          state_mlstm_m[0].reshape(n_seq, 1, ML_HEADS))
    x = jnp.concatenate([y_p, y_s], axis=0)
    x = _peer(x, norm_ffn[1], peer_w_q[1].astype(BF16), peer_sub_keys[1].astype(BF16),
              peer_u[1].astype(BF16), peer_v[1].astype(BF16))

    y_prompt = x[:n_prompt].reshape(n_batch, seq, d)
    y_sample = x[n_prompt:].reshape(n_seq, t_len, d)
    kv_shape_p = (1, n_batch, WINDOW, N_KV_HEADS, HEAD_DIM)
    kv_shape_s = (1, n_seq, WINDOW, N_KV_HEADS, HEAD_DIM)
    return (y_prompt, y_sample,
            c_p[:, SUBLANES - 2:, :][None], k_p.reshape(kv_shape_p), v_p.reshape(kv_shape_p),
            cm_p[None], nm_p[None], mm_p.reshape(1, n_batch, ML_HEADS),
            c_s[None], k_s.reshape(kv_shape_s), v_s.reshape(kv_shape_s),
            cm_s[None], nm_s[None], mm_s.reshape(1, n_seq, ML_HEADS))
```

```python
import functools

import jax
import jax.numpy as jnp
from jax import lax
from jax.experimental import pallas as pl
from jax.experimental.pallas import tpu as pltpu

F32 = jnp.float32
BF16 = jnp.bfloat16
EPS = 1e-6

D_MODEL = 1024
CONV_DIM = 512
N_Q_HEADS = 8
N_KV_HEADS = 2
HEAD_DIM = 64
WINDOW = 128
ROPE_THETA = 10000.0
AB_IN = 2304
ML_HEADS = 4
ML_QK = 128
ML_V = 256
ML_CHUNK = 128
ML_GATE_COL = 3072
ML_IN_PAD = ML_GATE_COL + 128
N_KEYS = 128
PEER_HEADS = 8
PEER_TOPK = 16
PAST_LEN = 16384

LANES = 128
SUBLANES = 8
G_PITCH = N_KEYS // 2 + SUBLANES
VMEM_LIMIT = 56 * 1024 * 1024

NT_DIMS = (((1,), (1,)), ((), ()))
TN_DIMS = (((0,), (0,)), ((), ()))


def _cparams(n_axes, vmem=VMEM_LIMIT):
    return pltpu.CompilerParams(dimension_semantics=("arbitrary",) * n_axes, vmem_limit_bytes=vmem)


def _rmsnorm(x, g):
    return x * lax.rsqrt(jnp.mean(x * x, axis=-1, keepdims=True) + EPS) * g


def _bf16_pieces(a, terms):
    pieces = []
    rem = a
    for _ in range(terms):
        piece = rem.astype(BF16)
        rem = rem - piece.astype(F32)
        pieces.append(piece)
    return pieces


def _split_dot(a, b_bf16, terms=2):
    return sum(jnp.dot(p, b_bf16, preferred_element_type=F32) for p in _bf16_pieces(a, terms))


def _norm_proj_kernel(x_ref, g_ref, w_ref, o_ref):
    r = _rmsnorm(x_ref[...], g_ref[...])
    o_ref[...] = jnp.dot(r.astype(BF16), w_ref[...], preferred_element_type=F32)


def _norm_proj(x, g, w_bf16, tm=512):
    n, d = x.shape
    nout = w_bf16.shape[1]
    return pl.pallas_call(
        _norm_proj_kernel,
        grid=(n // tm,),
        in_specs=[pl.BlockSpec((tm, d), lambda i: (i, 0)),
                  pl.BlockSpec((1, d), lambda i: (0, 0)),
                  pl.BlockSpec((d, nout), lambda i: (0, 0))],
        out_specs=pl.BlockSpec((tm, nout), lambda i: (i, 0)),
        out_shape=jax.ShapeDtypeStruct((n, nout), F32),
        compiler_params=_cparams(1),
        name="norm_proj",
    )(x, g.reshape(1, d), w_bf16)


def _headnorm_rope(xc, gain, cos, sin, seg, hi_half):
    ss = _split_dot(xc * xc, seg)
    xn = xc * lax.rsqrt(ss * (1.0 / HEAD_DIM) + EPS) * gain
    partner = jnp.where(hi_half, pltpu.roll(xn, 32, 1), pltpu.roll(xn, 96, 1))
    return xn * cos + partner * sin


def _softmax_sink(s, mask, sink):
    s = jnp.where(mask, s, -1e30)
    m = jnp.maximum(jnp.max(s, axis=-1, keepdims=True), sink)
    p = jnp.exp(s - m)
    denom = jnp.sum(p, axis=-1, keepdims=True) + jnp.exp(sink - m)
    return (p / denom).astype(BF16)


def _ab_prompt_kernel(sink_ref, z_ref, x_ref, cos_ref, sin_ref, cw_ref, qg_ref, kg_ref, seg_ref, wo_ref,
                      y_ref, kst_ref, vst_ref, cst_ref, pk_ref, pv_ref, pu_ref):
    j = pl.program_id(1)
    blk = z_ref.shape[0]

    @pl.when(j == 0)
    def _():
        pk_ref[...] = jnp.zeros_like(pk_ref)
        pv_ref[...] = jnp.zeros_like(pv_ref)
        pu_ref[...] = jnp.zeros_like(pu_ref)

    cos = cos_ref[...]
    sin = sin_ref[...]
    seg = seg_ref[...]
    hi_half = (lax.broadcasted_iota(jnp.int32, (blk, LANES), 1) & 32) != 0

    gate_b = z_ref[:, 0:CONV_DIM]
    u = z_ref[:, CONV_DIM:2 * CONV_DIM] * z_ref[:, 2 * CONV_DIM:3 * CONV_DIM]
    ng = blk // SUBLANES
    u3 = u.reshape(ng, SUBLANES, CONV_DIM)
    ext = jnp.concatenate([pu_ref[...][None], u3], axis=0)
    t8 = lax.broadcasted_iota(jnp.int32, (ng, SUBLANES, CONV_DIM), 1)
    r1 = pltpu.roll(ext, 1, 1)
    r2 = pltpu.roll(ext, 2, 1)
    um1 = jnp.where(t8 >= 1, r1[1:], r1[:-1])
    um2 = jnp.where(t8 >= 2, r2[1:], r2[:-1])
    cw = cw_ref[...]
    conv = cw[0:1][None] * um2 + cw[1:2][None] * um1 + cw[2:3][None] * u3
    yconv = gate_b * conv.reshape(blk, CONV_DIM)

    q0 = 3 * CONV_DIM
    k0 = q0 + N_Q_HEADS * HEAD_DIM
    v0 = k0 + N_KV_HEADS * HEAD_DIM
    qg = qg_ref[...]
    qr = [_headnorm_rope(z_ref[:, q0 + c * LANES:q0 + (c + 1) * LANES], qg, cos, sin, seg, hi_half)
          for c in range(N_Q_HEADS * HEAD_DIM // LANES)]
    kr = _headnorm_rope(z_ref[:, k0:k0 + LANES], kg_ref[...], cos, sin, seg, hi_half)
    v = z_ref[:, v0:v0 + LANES]
    pk = pk_ref[...]
    pv = pv_ref[...]

    row = lax.broadcasted_iota(jnp.int32, (blk, 2 * blk), 0)
    col = lax.broadcasted_iota(jnp.int32, (blk, 2 * blk), 1)
    row_prev = row + jnp.where(j == 0, 2 * blk, 0)
    mask = ((col < blk) & (col > row_prev)) | ((col >= blk) & (col - blk <= row))

    kks, vvs = [], []
    for g in range(N_KV_HEADS):
        sl = slice(g * HEAD_DIM, (g + 1) * HEAD_DIM)
        kks.append(jnp.concatenate([pk[:, sl], kr[:, sl]], axis=0).astype(BF16))
        vvs.append(jnp.concatenate([pv[:, sl], v[:, sl]], axis=0).astype(BF16))
    outs = []
    for h in range(N_Q_HEADS):
        g = h // (N_Q_HEADS // N_KV_HEADS)
        qh = qr[h // 2][:, (h % 2) * HEAD_DIM:(h % 2 + 1) * HEAD_DIM].astype(BF16)
        s = lax.dot_general(qh, kks[g], NT_DIMS, preferred_element_type=F32) * (HEAD_DIM ** -0.5)
        p = _softmax_sink(s, mask, sink_ref[h])
        outs.append(jnp.dot(p, vvs[g], preferred_element_type=F32))
    attn = jnp.concatenate(outs, axis=1)

    y = (jnp.dot(yconv.astype(BF16), wo_ref[0:CONV_DIM, :], preferred_element_type=F32)
         + jnp.dot(attn.astype(BF16), wo_ref[CONV_DIM:2 * CONV_DIM, :], preferred_element_type=F32))
    y_ref[...] = y + x_ref[...]

    pk_ref[...] = kr
    pv_ref[...] = v
    pu_ref[...] = u3[ng - 1]
    kst_ref[0] = kr
    vst_ref[0] = v
    cst_ref[0] = u3[ng - 1]


def _ab_prompt(z, x, n_batch, seq, cos, sin, cw, qg, kg, sinks, seg, wo):
    blk = WINDOW
    nb = seq // blk
    tok = lambda b, j, s: (b * nb + j, 0)
    const = lambda b, j, s: (0, 0)
    state = lambda b, j, s: (b, 0, 0)
    grid_spec = pltpu.PrefetchScalarGridSpec(
        num_scalar_prefetch=1,
        grid=(n_batch, nb),
        in_specs=[pl.BlockSpec((blk, AB_IN), tok),
                  pl.BlockSpec((blk, D_MODEL), tok),
                  pl.BlockSpec((blk, LANES), lambda b, j, s: (j, 0)),
                  pl.BlockSpec((blk, LANES), lambda b, j, s: (j, 0)),
                  pl.BlockSpec((3, CONV_DIM), const),
                  pl.BlockSpec((1, LANES), const),
                  pl.BlockSpec((1, LANES), const),
                  pl.BlockSpec((LANES, LANES), const),
                  pl.BlockSpec((D_MODEL, D_MODEL), const)],
        out_specs=[pl.BlockSpec((blk, D_MODEL), tok),
                   pl.BlockSpec((1, blk, LANES), state),
                   pl.BlockSpec((1, blk, LANES), state),
                   pl.BlockSpec((1, SUBLANES, CONV_DIM), state)],
        scratch_shapes=[pltpu.VMEM((blk, LANES), F32), pltpu.VMEM((blk, LANES), F32),
                        pltpu.VMEM((SUBLANES, CONV_DIM), F32)])
    return pl.pallas_call(
        _ab_prompt_kernel,
        grid_spec=grid_spec,
        out_shape=[jax.ShapeDtypeStruct((n_batch * seq, D_MODEL), F32),
                   jax.ShapeDtypeStruct((n_batch, blk, LANES), F32),
                   jax.ShapeDtypeStruct((n_batch, blk, LANES), F32),
                   jax.ShapeDtypeStruct((n_batch, SUBLANES, CONV_DIM), F32)],
        compiler_params=_cparams(2),
        name="ab_prompt",
    )(sinks, z, x, cos, sin, cw, qg, kg, seg, wo)


def _ab_sample_kernel(sink_ref, z_ref, x_ref, cos_ref, sin_ref, cw_ref, qg_ref, kg_ref, seg_ref, wo_ref,
                      cc_ref, ck_ref, cv_ref, y_ref, cs_ref, ks_ref, vs_ref):
    rows = z_ref.shape[0]
    t_len = SUBLANES
    bt = rows // t_len
    cos = cos_ref[...]
    sin = sin_ref[...]
    seg = seg_ref[...]
    hi_half = (lax.broadcasted_iota(jnp.int32, (rows, LANES), 1) & 32) != 0

    gate_b = z_ref[:, 0:CONV_DIM]
    u = z_ref[:, CONV_DIM:2 * CONV_DIM] * z_ref[:, 2 * CONV_DIM:3 * CONV_DIM]
    u3 = u.reshape(bt, t_len, CONV_DIM)
    cc = cc_ref[...]
    c0 = cc[:, 0:1, :]
    c1 = cc[:, 1:2, :]
    t8 = lax.broadcasted_iota(jnp.int32, (bt, t_len, CONV_DIM), 1)
    r1 = pltpu.roll(u3, 1, 1)
    r2 = pltpu.roll(u3, 2, 1)
    um1 = jnp.where(t8 >= 1, r1, c1)
    um2 = jnp.where(t8 >= 2, r2, jnp.where(t8 == 1, c1, c0))
    cw = cw_ref[...]
    conv = cw[0:1][None] * um2 + cw[1:2][None] * um1 + cw[2:3][None] * u3
    yconv = gate_b * conv.reshape(rows, CONV_DIM)
    cs_ref[...] = r2[:, 0:2, :]

    q0 = 3 * CONV_DIM
    k0 = q0 + N_Q_HEADS * HEAD_DIM
    v0 = k0 + N_KV_HEADS * HEAD_DIM
    qg = qg_ref[...]
    qr = [_headnorm_rope(z_ref[:, q0 + c * LANES:q0 + (c + 1) * LANES], qg, cos, sin, seg, hi_half)
          for c in range(N_Q_HEADS * HEAD_DIM // LANES)]
    kr = _headnorm_rope(z_ref[:, k0:k0 + LANES], kg_ref[...], cos, sin, seg, hi_half)
    v = z_ref[:, v0:v0 + LANES]

    group = N_Q_HEADS // N_KV_HEADS
    nq = group * t_len
    nk = 2 * WINDOW
    qrow = lax.broadcasted_iota(jnp.int32, (nq, nk), 0)
    t_q = qrow & (t_len - 1)
    col = lax.broadcasted_iota(jnp.int32, (nq, nk), 1)
    mask = (((col < WINDOW) & (col > t_q)) | ((col >= WINDOW) & (col - WINDOW <= t_q)))[None]
    hrow = lax.broadcasted_iota(jnp.int32, (nq, 1), 0) // t_len
    pad = jnp.zeros((bt, nk - WINDOW - t_len, HEAD_DIM), F32)

    outs = [None] * N_Q_HEADS
    for g in range(N_KV_HEADS):
        sl = slice(g * HEAD_DIM, (g + 1) * HEAD_DIM)
        qs = jnp.concatenate(
            [qr[h // 2][:, (h % 2) * HEAD_DIM:(h % 2 + 1) * HEAD_DIM].reshape(bt, t_len, HEAD_DIM)
             for h in range(g * group, (g + 1) * group)], axis=1)
        kk = jnp.concatenate([ck_ref[:, :, sl], kr[:, sl].reshape(bt, t_len, HEAD_DIM), pad], axis=1)
        vv = jnp.concatenate([cv_ref[:, :, sl], v[:, sl].reshape(bt, t_len, HEAD_DIM), pad], axis=1)
        s = jnp.einsum('bqd,bkd->bqk', qs.astype(BF16), kk.astype(BF16),
                       preferred_element_type=F32) * (HEAD_DIM ** -0.5)
        sink = jnp.zeros((nq, 1), F32)
        for hh in range(group):
            sink = jnp.where(hrow == hh, sink_ref[g * group + hh], sink)
        p = _softmax_sink(s, mask, sink[None])
        o = jnp.einsum('bqk,bkd->bqd', p, vv.astype(BF16), preferred_element_type=F32)
        for hh in range(group):
            outs[g * group + hh] = o[:, hh * t_len:(hh + 1) * t_len, :].reshape(rows, HEAD_DIM)
    attn = jnp.concatenate(outs, axis=1)

    y = (jnp.dot(yconv.astype(BF16), wo_ref[0:CONV_DIM, :], preferred_element_type=F32)
         + jnp.dot(attn.astype(BF16), wo_ref[CONV_DIM:2 * CONV_DIM, :], preferred_element_type=F32))
    y_ref[...] = y + x_ref[...]

    keep = WINDOW - t_len
    ks_ref[:, 0:keep, :] = ck_ref[:, t_len:WINDOW, :]
    ks_ref[:, keep:WINDOW, :] = kr.reshape(bt, t_len, LANES)
    vs_ref[:, 0:keep, :] = cv_ref[:, t_len:WINDOW, :]
    vs_ref[:, keep:WINDOW, :] = v.reshape(bt, t_len, LANES)


def _ab_sample(z, x, row0, n_seq, t_len, cos, sin, cw, qg, kg, sinks, seg, wo, cc, ck, cv, bt=16):
    rows = bt * t_len
    blk0 = row0 // rows
    tok = lambda i, s: (blk0 + i, 0)
    const = lambda i, s: (0, 0)
    seq3 = lambda i, s: (i, 0, 0)
    grid_spec = pltpu.PrefetchScalarGridSpec(
        num_scalar_prefetch=1,
        grid=(n_seq // bt,),
        in_specs=[pl.BlockSpec((rows, AB_IN), tok),
                  pl.BlockSpec((rows, D_MODEL), tok),
                  pl.BlockSpec((rows, LANES), const),
                  pl.BlockSpec((rows, LANES), const),
                  pl.BlockSpec((3, CONV_DIM), const),
                  pl.BlockSpec((1, LANES), const),
                  pl.BlockSpec((1, LANES), const),
                  pl.BlockSpec((LANES, LANES), const),
                  pl.BlockSpec((D_MODEL, D_MODEL), const),
                  pl.BlockSpec((bt, 2, CONV_DIM), seq3),
                  pl.BlockSpec((bt, WINDOW, LANES), seq3),
                  pl.BlockSpec((bt, WINDOW, LANES), seq3)],
        out_specs=[pl.BlockSpec((rows, D_MODEL), lambda i, s: (i, 0)),
                   pl.BlockSpec((bt, 2, CONV_DIM), seq3),
                   pl.BlockSpec((bt, WINDOW, LANES), seq3),
                   pl.BlockSpec((bt, WINDOW, LANES), seq3)])
    return pl.pallas_call(
        _ab_sample_kernel,
        grid_spec=grid_spec,
        out_shape=[jax.ShapeDtypeStruct((n_seq * t_len, D_MODEL), F32),
                   jax.ShapeDtypeStruct((n_seq, 2, CONV_DIM), F32),
                   jax.ShapeDtypeStruct((n_seq, WINDOW, LANES), F32),
                   jax.ShapeDtypeStruct((n_seq, WINDOW, LANES), F32)],
        compiler_params=_cparams(1),
        name="ab_sample",
    )(sinks, z, x, cos, sin, cw, qg, kg, seg, wo, cc, ck, cv)


def _log_sigmoid(x):
    return jnp.minimum(x, 0.0) - jnp.log(1.0 + jnp.exp(-jnp.abs(x)))


def _mlstm_chunk(z, x, bias, og, tril, wo, c_src, n_src, m_src, c_dst, n_dst, m_dst, n_real):
    L = z.shape[0]
    gates = z[:, ML_GATE_COL:ML_GATE_COL + LANES] + bias
    if n_real < L:
        live = lax.broadcasted_iota(jnp.int32, (L, LANES), 0) < n_real
        li_all = jnp.where(live, gates, -1e30)
        lf_all = jnp.where(live, _log_sigmoid(gates), 0.0)
    else:
        li_all = gates
        lf_all = _log_sigmoid(gates)
    lf_pieces = _bf16_pieces(lf_all, 3)
    f_col_all = sum(jnp.dot(tril, p, preferred_element_type=F32) for p in lf_pieces)
    f_row_all = sum(lax.dot_general(p, tril, (((0,), (1,)), ((), ())), preferred_element_type=F32)
                    for p in lf_pieces)
    li_t = li_all.T
    rr = lax.broadcasted_iota(jnp.int32, (L, L), 0)
    cc = lax.broadcasted_iota(jnp.int32, (L, L), 1)
    causal = cc <= rr

    outs, m_new_all = [], []
    for h in range(ML_HEADS):
        f_col = f_col_all[:, ML_HEADS + h:ML_HEADS + h + 1]
        f_row = f_row_all[ML_HEADS + h:ML_HEADS + h + 1, :]
        li_row = li_t[h:h + 1, :]
        li_col = li_all[:, h:h + 1]
        m0 = m_src[0, 0:1, h:h + 1]
        c0 = c_src[0, h]
        n0 = n_src[0, h:h + 1, :]
        qh = z[:, h * ML_QK:(h + 1) * ML_QK]
        kh = z[:, ML_HEADS * ML_QK + h * ML_QK:ML_HEADS * ML_QK + (h + 1) * ML_QK] * (ML_QK ** -0.5)
        v_off = 2 * ML_HEADS * ML_QK
        vh = z[:, v_off + h * ML_V:v_off + (h + 1) * ML_V]
        o_off = v_off + ML_HEADS * ML_V
        oh = z[:, o_off + h * ML_V:o_off + (h + 1) * ML_V]
        qb = qh.astype(BF16)
        vb = vh.astype(BF16)

        dmat = jnp.where(causal, f_col - f_row + li_row, -jnp.inf)
        gcar = f_col + m0
        m_t = jnp.maximum(jnp.max(dmat, axis=-1, keepdims=True), gcar)
        w = jnp.exp(dmat - m_t)
        s = lax.dot_general(qb, kh.astype(BF16), NT_DIMS, preferred_element_type=F32) * w
        carry = jnp.exp(gcar - m_t)
        num = (jnp.dot(s.astype(BF16), vb, preferred_element_type=F32)
               + jnp.dot(qb, c0.astype(BF16), preferred_element_type=F32) * carry)
        den = jnp.sum(s, axis=-1, keepdims=True) + carry * jnp.sum(qh * n0, axis=-1, keepdims=True)
        hout = num / jnp.maximum(jnp.abs(den), jnp.exp(-m_t))

        f_last = f_col[L - 1:L, :]
        w_end = f_last - f_col + li_col
        m_new = jnp.maximum(f_last + m0, jnp.max(w_end, axis=0, keepdims=True))
        a_end = jnp.exp(w_end - m_new)
        scale = jnp.exp(f_last + m0 - m_new)
        ka = kh * a_end
        c_dst[0, h] = scale * c0 + lax.dot_general(ka.astype(BF16), vb, TN_DIMS, preferred_element_type=F32)
        n_dst[0, h:h + 1, :] = scale * n0 + jnp.sum(ka, axis=0, keepdims=True)
        m_new_all.append(m_new)

        hn = _rmsnorm(hout, og[:, h * ML_V:(h + 1) * ML_V])
        outs.append(jax.nn.sigmoid(oh) * hn)
    m_dst[0] = jnp.concatenate(m_new_all, axis=1)
    out = jnp.concatenate(outs, axis=1)
    return jnp.dot(out.astype(BF16), wo, preferred_element_type=F32) + x


def _mlstm_prompt_kernel(z_ref, x_ref, bias_ref, og_ref, tril_ref, wo_ref, y_ref, c_ref, n_ref, m_ref):
    @pl.when(pl.program_id(1) == 0)
    def _():
        c_ref[...] = jnp.zeros_like(c_ref)
        n_ref[...] = jnp.zeros_like(n_ref)
        m_ref[...] = jnp.zeros_like(m_ref)

    y_ref[...] = _mlstm_chunk(z_ref[...], x_ref[...], bias_ref[...], og_ref[...], tril_ref[...], wo_ref[...],
                              c_ref, n_ref, m_ref, c_ref, n_ref, m_ref, ML_CHUNK)


def _mlstm_sample_kernel(z_ref, x_ref, bias_ref, og_ref, tril_ref, wo_ref, c0_ref, n0_ref, m0_ref,
                         y_ref, c_ref, n_ref, m_ref):
    t_len = z_ref.shape[0]
    zpad = jnp.concatenate([z_ref[...], jnp.zeros((ML_CHUNK - t_len, ML_IN_PAD), F32)], axis=0)
    xpad = jnp.concatenate([x_ref[...], jnp.zeros((ML_CHUNK - t_len, D_MODEL), F32)], axis=0)
    y = _mlstm_chunk(zpad, xpad, bias_ref[...], og_ref[...], tril_ref[...], wo_ref[...],
                     c0_ref, n0_ref, m0_ref, c_ref, n_ref, m_ref, t_len)
    y_ref[...] = y[0:t_len, :]


def _mlstm_weight_specs(const):
    return [pl.BlockSpec((1, LANES), const),
            pl.BlockSpec((1, D_MODEL), const),
            pl.BlockSpec((ML_CHUNK, ML_CHUNK), const),
            pl.BlockSpec((D_MODEL, D_MODEL), const)]


def _mlstm_prompt(z, x, n_batch, seq, bias, og, tril, wo):
    nc = seq // ML_CHUNK
    tok = lambda b, j: (b * nc + j, 0)
    const = lambda b, j: (0, 0)
    return pl.pallas_call(
        _mlstm_prompt_kernel,
        grid=(n_batch, nc),
        in_specs=[pl.BlockSpec((ML_CHUNK, ML_IN_PAD), tok),
                  pl.BlockSpec((ML_CHUNK, D_MODEL), tok)] + _mlstm_weight_specs(const),
        out_specs=[pl.BlockSpec((ML_CHUNK, D_MODEL), tok),
                   pl.BlockSpec((1, ML_HEADS, ML_QK, ML_V), lambda b, j: (b, 0, 0, 0)),
                   pl.BlockSpec((1, ML_HEADS, ML_QK), lambda b, j: (b, 0, 0)),
                   pl.BlockSpec((1, 1, ML_HEADS), lambda b, j: (b, 0, 0))],
        out_shape=[jax.ShapeDtypeStruct((n_batch * seq, D_MODEL), F32),
                   jax.ShapeDtypeStruct((n_batch, ML_HEADS, ML_QK, ML_V), F32),
                   jax.ShapeDtypeStruct((n_batch, ML_HEADS, ML_QK), F32),
                   jax.ShapeDtypeStruct((n_batch, 1, ML_HEADS), F32)],
        compiler_params=_cparams(2),
        name="mlstm_prompt",
    )(z, x, bias, og, tril, wo)


def _mlstm_sample(z, x, row0, n_seq, t_len, bias, og, tril, wo, c0, n0, m0):
    blk0 = row0 // t_len
    tok = lambda i: (blk0 + i, 0)
    const = lambda i: (0, 0)
    st4 = lambda i: (i, 0, 0, 0)
    st3 = lambda i: (i, 0, 0)
    state_specs = [pl.BlockSpec((1, ML_HEADS, ML_QK, ML_V), st4),
                   pl.BlockSpec((1, ML_HEADS, ML_QK), st3),
                   pl.BlockSpec((1, 1, ML_HEADS), st3)]
    return pl.pallas_call(
        _mlstm_sample_kernel,
        grid=(n_seq,),
        in_specs=[pl.BlockSpec((t_len, ML_IN_PAD), tok),
                  pl.BlockSpec((t_len, D_MODEL), tok)] + _mlstm_weight_specs(const) + state_specs,
        out_specs=[pl.BlockSpec((t_len, D_MODEL), lambda i: (i, 0))] + state_specs,
        out_shape=[jax.ShapeDtypeStruct((n_seq * t_len, D_MODEL), F32),
                   jax.ShapeDtypeStruct((n_seq, ML_HEADS, ML_QK, ML_V), F32),
                   jax.ShapeDtypeStruct((n_seq, ML_HEADS, ML_QK), F32),
                   jax.ShapeDtypeStruct((n_seq, 1, ML_HEADS), F32)],
        compiler_params=_cparams(1),
        name="mlstm_sample",
    )(z, x, bias, og, tril, wo, c0, n0, m0)


def _topk_rows(s, k, payload=None):
    n = s.shape[0]
    rows = lax.broadcasted_iota(jnp.int32, s.shape, 0).astype(F32)
    vals, picks = [], []
    for _ in range(k):
        m = jnp.max(s, axis=0, keepdims=True)
        first = jnp.min(jnp.where(s == m, rows, float(n)), axis=0, keepdims=True)
        sel = rows == first
        vals.append(m)
        if payload is None:
            picks.append(first)
        else:
            picks.append(jnp.max(jnp.where(sel, payload, -1.0), axis=0, keepdims=True))
        s = jnp.where(sel, -jnp.inf, s)
    return jnp.concatenate(vals, axis=0), jnp.concatenate(picks, axis=0)


def _peer_select_kernel(x_ref, g_ref, wq_ref, sk_ref, xn_ref, e_ref, gate_ref, q_scr):
    h = pl.program_id(1)

    @pl.when(h == 0)
    def _():
        xn = _rmsnorm(x_ref[...], g_ref[...]).astype(BF16)
        xn_ref[...] = xn
        q = jnp.dot(xn, wq_ref[...], preferred_element_type=F32)
        for c in range(2 * PEER_HEADS):
            q_scr[c] = q[:, c * LANES:(c + 1) * LANES]

    vals, idxs = [], []
    for p in range(2):
        qhp = q_scr[2 * h + p].astype(BF16)
        st = lax.dot_general(sk_ref[0, p], qhp, NT_DIMS, preferred_element_type=F32)
        v_, i_ = _topk_rows(st, PEER_TOPK)
        vals.append(v_)
        idxs.append(i_)
    def pairs(first, second, combine):
        h8 = SUBLANES
        rows = [combine(first[0:1], second)]
        rows += [combine(first[k1:k1 + 1], second[0:h8]) for k1 in range(1, h8)]
        rows.append(combine(first[h8:PEER_TOPK], second[0:1]))
        return jnp.concatenate(rows, axis=0)

    cand = pairs(vals[0], vals[1], lambda a, b: a + b)
    expert = pairs(idxs[0], idxs[1], lambda a, b: a * N_KEYS + b)
    best, e_sel = _topk_rows(cand, PEER_TOPK, payload=expert)
    ex = jnp.exp(best - best[0:1])
    e_ref[...] = e_sel
    gate_ref[...] = ex / jnp.sum(ex, axis=0, keepdims=True)


def _peer_select(x, g, wq, sk, t=256):
    n, d = x.shape
    nslot = PEER_HEADS * PEER_TOPK
    return pl.pallas_call(
        _peer_select_kernel,
        grid=(n // t, PEER_HEADS),
        in_specs=[pl.BlockSpec((t, d), lambda i, h: (i, 0)),
                  pl.BlockSpec((1, d), lambda i, h: (0, 0)),
                  pl.BlockSpec((d, 2 * PEER_HEADS * LANES), lambda i, h: (0, 0)),
                  pl.BlockSpec((1, 2, N_KEYS, LANES), lambda i, h: (h, 0, 0, 0))],
        out_specs=[pl.BlockSpec((t, d), lambda i, h: (i, 0)),
                   pl.BlockSpec((PEER_TOPK, t), lambda i, h: (h, i)),
                   pl.BlockSpec((PEER_TOPK, t), lambda i, h: (h, i))],
        out_shape=[jax.ShapeDtypeStruct((n, d), BF16),
                   jax.ShapeDtypeStruct((nslot, n), F32),
                   jax.ShapeDtypeStruct((nslot, n), F32)],
        scratch_shapes=[pltpu.VMEM((2 * PEER_HEADS, t, LANES), F32)],
        compiler_params=_cparams(2),
        name="peer_select",
    )(x, g.reshape(1, d), wq, sk)


def _gelu(x):
    return 0.5 * x * (1.0 + lax.erf(x * 0.7071067811865476))


def _peer_main_kernel(xn_ref, e_ref, gate_ref, x_ref, u_ref, v_ref, o_ref,
                      a_scr, b_scr, g_scr, gmat, coef_scr):
    e = pl.program_id(1)
    t = xn_ref.shape[0]
    half = N_KEYS // 2

    @pl.when(e == 0)
    def _():
        ef = e_ref[...]
        af = jnp.floor(ef * (1.0 / N_KEYS))
        a_scr[...] = af.T
        b_scr[...] = (ef - af * N_KEYS).T
        g_scr[...] = gate_ref[...].T
        o_ref[...] = x_ref[...]
        coef_scr[...] = jnp.zeros_like(coef_scr)
        r = lax.broadcasted_iota(jnp.int32, (N_KEYS, LANES), 0)
        a_of_row = jnp.where(r < half, 2 * r, 2 * (r - half) + 1).astype(F32)
        b_of_row = r.astype(F32)

        def body(n8, carry):
            base = pl.multiple_of(n8 * SUBLANES, SUBLANES)
            a8 = a_scr[pl.ds(base, SUBLANES), :]
            b8 = b_scr[pl.ds(base, SUBLANES), :]
            g8 = g_scr[pl.ds(base, SUBLANES), :]
            for i in range(SUBLANES):
                pa = jnp.where(a_of_row == a8[i:i + 1], 1.0, 0.0).astype(BF16)
                qb = jnp.where(b_of_row == b8[i:i + 1], g8[i:i + 1], 0.0).astype(BF16)
                tile = lax.dot_general(pa, qb, NT_DIMS, preferred_element_type=F32)
                lo = lax.bitcast_convert_type(tile[0:half].astype(BF16).astype(F32), jnp.uint32)
                hi = lax.bitcast_convert_type(tile[half:N_KEYS].astype(BF16).astype(F32), jnp.uint32)
                row0 = pl.multiple_of((base + i) * G_PITCH, SUBLANES)
                gmat[pl.ds(row0, half), :] = (lo >> 16) | (hi & jnp.uint32(0xFFFF0000))
            return carry

        lax.fori_loop(0, t // SUBLANES, body, 0)

    o_ref[...] += jnp.dot(coef_scr[...], v_ref[...], preferred_element_type=F32)

    act = lax.dot_general(xn_ref[...], u_ref[...], NT_DIMS, preferred_element_type=F32)
    blk = jnp.minimum(e, pl.num_programs(1) - 2)
    words_per_step = u_ref.shape[0] // (2 * N_KEYS)
    gates = []
    for i in range(words_per_step):
        word = gmat[pl.ds(blk * words_per_step + i, t, stride=G_PITCH), :]
        gates.append(lax.bitcast_convert_type(word << 16, F32))
        gates.append(lax.bitcast_convert_type(word & jnp.uint32(0xFFFF0000), F32))
    coef_scr[...] = (_gelu(act) * jnp.concatenate(gates, axis=1)).astype(BF16)


def _peer_main(xn, e_sel, gate, x, u, v, t=512, eb=1024):
    n, d = x.shape
    nslot = PEER_HEADS * PEER_TOPK
    n_blk = N_KEYS * N_KEYS // eb
    return pl.pallas_call(
        _peer_main_kernel,
        grid=(n // t, n_blk + 1),
        in_specs=[pl.BlockSpec((t, d), lambda i, e: (i, 0)),
                  pl.BlockSpec((nslot, t), lambda i, e: (0, i)),
                  pl.BlockSpec((nslot, t), lambda i, e: (0, i)),
                  pl.BlockSpec((t, d), lambda i, e: (i, 0)),
                  pl.BlockSpec((eb, d), lambda i, e: (jnp.minimum(e, n_blk - 1), 0)),
                  pl.BlockSpec((eb, d), lambda i, e: (jnp.maximum(e - 1, 0), 0))],
        out_specs=pl.BlockSpec((t, d), lambda i, e: (i, 0)),
        out_shape=jax.ShapeDtypeStruct((n, d), F32),
        scratch_shapes=[pltpu.VMEM((t, nslot), F32), pltpu.VMEM((t, nslot), F32), pltpu.VMEM((t, nslot), F32),
                        pltpu.VMEM((t * G_PITCH, LANES), jnp.uint32), pltpu.VMEM((t, eb), BF16)],
        compiler_params=_cparams(2),
        name="peer_main",
    )(xn, e_sel, gate, x, u, v)


def _peer(x, g, wq, sk, u, v):
    xn, e_sel, gate = _peer_select(x, g, wq, sk)
    return _peer_main(xn, e_sel, gate, x, u, v)


def _rope_tables(pos):
    half = HEAD_DIM // 2
    inv_freq = ROPE_THETA ** (-jnp.arange(half, dtype=F32) / half)
    ang = pos.astype(F32)[:, None] * inv_freq[None, :]
    cos, sin = jnp.cos(ang), jnp.sin(ang)
    reps = LANES // HEAD_DIM
    cos_t = jnp.tile(jnp.concatenate([cos, cos], axis=1), (1, reps))
    sin_t = jnp.tile(jnp.concatenate([-sin, sin], axis=1), (1, reps))
    return cos_t, sin_t


def kernel(x_prompt, x_sample, cache_conv, cache_win_k, cache_win_v, state_mlstm_C, state_mlstm_n,
           state_mlstm_m, norm_mix, norm_ffn, ab_w_in, ab_conv_w, ab_q_gain, ab_k_gain, ab_sinks, ab_w_out,
           ml_w_in, ml_gate_bias, ml_out_gain, ml_w_out, peer_w_q, peer_sub_keys, peer_u, peer_v):
    n_batch, seq, d = x_prompt.shape
    n_seq, t_len, _ = x_sample.shape
    n_prompt = n_batch * seq
    assert d == D_MODEL and t_len == SUBLANES and norm_mix.shape[0] == 2

    x = jnp.concatenate([x_prompt.reshape(n_prompt, d), x_sample.reshape(n_seq * t_len, d)], axis=0)

    cos_p, sin_p = _rope_tables(jnp.arange(seq, dtype=jnp.int32))
    cos_s, sin_s = _rope_tables(PAST_LEN + jnp.arange(t_len, dtype=jnp.int32))
    bt = 16
    cos_s, sin_s = jnp.tile(cos_s, (bt, 1)), jnp.tile(sin_s, (bt, 1))
    lane = jnp.arange(LANES)
    seg = (lane[:, None] // HEAD_DIM == lane[None, :] // HEAD_DIM).astype(BF16)
    reps = LANES // HEAD_DIM
    qg = jnp.tile(ab_q_gain[0], reps).reshape(1, LANES)
    kg = jnp.tile(ab_k_gain[0], reps).reshape(1, LANES)
    wo_ab = ab_w_out[0].astype(BF16)

    z = _norm_proj(x, norm_mix[0], ab_w_in[0].astype(BF16))
    y_p, k_p, v_p, c_p = _ab_prompt(z, x, n_batch, seq, cos_p, sin_p, ab_conv_w[0], qg, kg, ab_sinks[0],
                                    seg, wo_ab)
    y_s, c_s, k_s, v_s = _ab_sample(z, x, n_prompt, n_seq, t_len, cos_s, sin_s, ab_conv_w[0], qg, kg,
                                    ab_sinks[0], seg, wo_ab, cache_conv[0],
                                    cache_win_k[0].reshape(n_seq, WINDOW, LANES),
                                    cache_win_v[0].reshape(n_seq, WINDOW, LANES), bt=bt)
    x = jnp.concatenate([y_p, y_s], axis=0)
    x = _peer(x, norm_ffn[0], peer_w_q[0].astype(BF16), peer_sub_keys[0].astype(BF16),
              peer_u[0].astype(BF16), peer_v[0].astype(BF16))

    n_gate = 2 * ML_HEADS
    w_in = jnp.pad(ml_w_in[0], ((0, 0), (0, ML_IN_PAD - ml_w_in.shape[2]))).astype(BF16)
    bias = jnp.pad(ml_gate_bias[0], (0, LANES - n_gate)).reshape(1, LANES)
    og = ml_out_gain[0].reshape(1, D_MODEL)
    idx = jnp.arange(ML_CHUNK)
    tril = (idx[None, :] <= idx[:, None]).astype(BF16)
    wo_ml = ml_w_out[0].astype(BF16)

    z = _norm_proj(x, norm_mix[1], w_in)
    y_p, cm_p, nm_p, mm_p = _mlstm_prompt(z, x, n_batch, seq, bias, og, tril, wo_ml)
    y_s, cm_s, nm_s, mm_s = _mlstm_sample(z, x, n_prompt, n_seq, t_len, bias, og, tril, wo_ml,
                                          state_mlstm_C[0], state_mlstm_n[0],
                                          state_mlstm_m[0].reshape(n_seq, 1, ML_HEADS))
    x = jnp.concatenate([y_p, y_s], axis=0)
    x = _peer(x, norm_ffn[1], peer_w_q[1].astype(BF16), peer_sub_keys[1].astype(BF16),
              peer_u[1].astype(BF16), peer_v[1].astype(BF16))

    y_prompt = x[:n_prompt].reshape(n_batch, seq, d)
    y_sample = x[n_prompt:].reshape(n_seq, t_len, d)
    kv_shape_p = (1, n_batch, WINDOW, N_KV_HEADS, HEAD_DIM)
    kv_shape_s = (1, n_seq, WINDOW, N_KV_HEADS, HEAD_DIM)
    return (y_prompt, y_sample,
            c_p[:, SUBLANES - 2:, :][None], k_p.reshape(kv_shape_p), v_p.reshape(kv_shape_p),
            cm_p[None], nm_p[None], mm_p.reshape(1, n_batch, ML_HEADS),
            c_s[None], k_s.reshape(kv_shape_s), v_s.reshape(kv_shape_s),
            cm_s[None], nm_s[None], mm_s.reshape(1, n_seq, ML_HEADS))
```

```python
import functools

import jax
import jax.numpy as jnp
from jax import lax
from jax.experimental import pallas as pl
from jax.experimental.pallas import tpu as pltpu

F32 = jnp.float32
BF16 = jnp.bfloat16
EPS = 1e-6

D_MODEL = 1024
CONV_DIM = 512
N_Q_HEADS = 8
N_KV_HEADS = 2
HEAD_DIM = 64
WINDOW = 128
ROPE_THETA = 10000.0
AB_IN = 2304
ML_HEADS = 4
ML_QK = 128
ML_V = 256
ML_CHUNK = 128
ML_GATE_COL = 3072
ML_IN_PAD = ML_GATE_COL + 128
N_KEYS = 128
PEER_HEADS = 8
PEER_TOPK = 16
PAST_LEN = 16384

LANES = 128
SUBLANES = 8
G_PITCH = N_KEYS // 2 + SUBLANES
VMEM_LIMIT = 56 * 1024 * 1024

NT_DIMS = (((1,), (1,)), ((), ()))
TN_DIMS = (((0,), (0,)), ((), ()))


def _cparams(n_axes, vmem=VMEM_LIMIT):
    return pltpu.CompilerParams(dimension_semantics=("arbitrary",) * n_axes, vmem_limit_bytes=vmem)


def _rmsnorm(x, g):
    return x * lax.rsqrt(jnp.mean(x * x, axis=-1, keepdims=True) + EPS) * g


def _bf16_pieces(a, terms):
    pieces = []
    rem = a
    for _ in range(terms):
        piece = rem.astype(BF16)
        rem = rem - piece.astype(F32)
        pieces.append(piece)
    return pieces


def _split_dot(a, b_bf16, terms=2):
    return sum(jnp.dot(p, b_bf16, preferred_element_type=F32) for p in _bf16_pieces(a, terms))


def _norm_proj_kernel(x_ref, g_ref, w_ref, o_ref):
    r = _rmsnorm(x_ref[...], g_ref[...])
    o_ref[...] = jnp.dot(r.astype(BF16), w_ref[...], preferred_element_type=F32)


def _norm_proj(x, g, w_bf16, tm=512):
    n, d = x.shape
    nout = w_bf16.shape[1]
    return pl.pallas_call(
        _norm_proj_kernel,
        grid=(n // tm,),
        in_specs=[pl.BlockSpec((tm, d), lambda i: (i, 0)),
                  pl.BlockSpec((1, d), lambda i: (0, 0)),
                  pl.BlockSpec((d, nout), lambda i: (0, 0))],
        out_specs=pl.BlockSpec((tm, nout), lambda i: (i, 0)),
        out_shape=jax.ShapeDtypeStruct((n, nout), F32),
        compiler_params=_cparams(1),
        name="norm_proj",
    )(x, g.reshape(1, d), w_bf16)


def _headnorm_rope(xc, gain, cos, sin, seg, hi_half):
    ss = _split_dot(xc * xc, seg)
    xn = xc * lax.rsqrt(ss * (1.0 / HEAD_DIM) + EPS) * gain
    partner = jnp.where(hi_half, pltpu.roll(xn, 32, 1), pltpu.roll(xn, 96, 1))
    return xn * cos + partner * sin


def _softmax_sink(s, mask, sink):
    s = jnp.where(mask, s, -1e30)
    m = jnp.maximum(jnp.max(s, axis=-1, keepdims=True), sink)
    p = jnp.exp(s - m)
    denom = jnp.sum(p, axis=-1, keepdims=True) + jnp.exp(sink - m)
    return (p / denom).astype(BF16)


def _ab_prompt_kernel(sink_ref, z_ref, x_ref, cos_ref, sin_ref, cw_ref, qg_ref, kg_ref, seg_ref, wo_ref,
                      y_ref, kst_ref, vst_ref, cst_ref, pk_ref, pv_ref, pu_ref):
    j = pl.program_id(1)
    blk = z_ref.shape[0]

    @pl.when(j == 0)
    def _():
        pk_ref[...] = jnp.zeros_like(pk_ref)
        pv_ref[...] = jnp.zeros_like(pv_ref)
        pu_ref[...] = jnp.zeros_like(pu_ref)

    cos = cos_ref[...]
    sin = sin_ref[...]
    seg = seg_ref[...]
    hi_half = (lax.broadcasted_iota(jnp.int32, (blk, LANES), 1) & 32) != 0

    gate_b = z_ref[:, 0:CONV_DIM]
    u = z_ref[:, CONV_DIM:2 * CONV_DIM] * z_ref[:, 2 * CONV_DIM:3 * CONV_DIM]
    ng = blk // SUBLANES
    u3 = u.reshape(ng, SUBLANES, CONV_DIM)
    ext = jnp.concatenate([pu_ref[...][None], u3], axis=0)
    t8 = lax.broadcasted_iota(jnp.int32, (ng, SUBLANES, CONV_DIM), 1)
    r1 = pltpu.roll(ext, 1, 1)
    r2 = pltpu.roll(ext, 2, 1)
    um1 = jnp.where(t8 >= 1, r1[1:], r1[:-1])
    um2 = jnp.where(t8 >= 2, r2[1:], r2[:-1])
    cw = cw_ref[...]
    conv = cw[0:1][None] * um2 + cw[1:2][None] * um1 + cw[2:3][None] * u3
    yconv = gate_b * conv.reshape(blk, CONV_DIM)

    q0 = 3 * CONV_DIM
    k0 = q0 + N_Q_HEADS * HEAD_DIM
    v0 = k0 + N_KV_HEADS * HEAD_DIM
    qg = qg_ref[...]
    qr = [_headnorm_rope(z_ref[:, q0 + c * LANES:q0 + (c + 1) * LANES], qg, cos, sin, seg, hi_half)
          for c in range(N_Q_HEADS * HEAD_DIM // LANES)]
    kr = _headnorm_rope(z_ref[:, k0:k0 + LANES], kg_ref[...], cos, sin, seg, hi_half)
    v = z_ref[:, v0:v0 + LANES]
    pk = pk_ref[...]
    pv = pv_ref[...]

    row = lax.broadcasted_iota(jnp.int32, (blk, 2 * blk), 0)
    col = lax.broadcasted_iota(jnp.int32, (blk, 2 * blk), 1)
    row_prev = row + jnp.where(j == 0, 2 * blk, 0)
    mask = ((col < blk) & (col > row_prev)) | ((col >= blk) & (col - blk <= row))

    kks, vvs = [], []
    for g in range(N_KV_HEADS):
        sl = slice(g * HEAD_DIM, (g + 1) * HEAD_DIM)
        kks.append(jnp.concatenate([pk[:, sl], kr[:, sl]], axis=0).astype(BF16))
        vvs.append(jnp.concatenate([pv[:, sl], v[:, sl]], axis=0).astype(BF16))
    outs = []
    for h in range(N_Q_HEADS):
        g = h // (N_Q_HEADS // N_KV_HEADS)
        qh = qr[h // 2][:, (h % 2) * HEAD_DIM:(h % 2 + 1) * HEAD_DIM].astype(BF16)
        s = lax.dot_general(qh, kks[g], NT_DIMS, preferred_element_type=F32) * (HEAD_DIM ** -0.5)
        p = _softmax_sink(s, mask, sink_ref[h])
        outs.append(jnp.dot(p, vvs[g], preferred_element_type=F32))
    attn = jnp.concatenate(outs, axis=1)

    y = (jnp.dot(yconv.astype(BF16), wo_ref[0:CONV_DIM, :], preferred_element_type=F32)
         + jnp.dot(attn.astype(BF16), wo_ref[CONV_DIM:2 * CONV_DIM, :], preferred_element_type=F32))
    y_ref[...] = y + x_ref[...]

    pk_ref[...] = kr
    pv_ref[...] = v
    pu_ref[...] = u3[ng - 1]
    kst_ref[0] = kr
    vst_ref[0] = v
    cst_ref[0] = u3[ng - 1]


def _ab_prompt(z, x, n_batch, seq, cos, sin, cw, qg, kg, sinks, seg, wo):
    blk = WINDOW
    nb = seq // blk
    tok = lambda b, j, s: (b * nb + j, 0)
    const = lambda b, j, s: (0, 0)
    state = lambda b, j, s: (b, 0, 0)
    grid_spec = pltpu.PrefetchScalarGridSpec(
        num_scalar_prefetch=1,
        grid=(n_batch, nb),
        in_specs=[pl.BlockSpec((blk, AB_IN), tok),
                  pl.BlockSpec((blk, D_MODEL), tok),
                  pl.BlockSpec((blk, LANES), lambda b, j, s: (j, 0)),
                  pl.BlockSpec((blk, LANES), lambda b, j, s: (j, 0)),
                  pl.BlockSpec((3, CONV_DIM), const),
                  pl.BlockSpec((1, LANES), const),
                  pl.BlockSpec((1, LANES), const),
                  pl.BlockSpec((LANES, LANES), const),
                  pl.BlockSpec((D_MODEL, D_MODEL), const)],
        out_specs=[pl.BlockSpec((blk, D_MODEL), tok),
                   pl.BlockSpec((1, blk, LANES), state),
                   pl.BlockSpec((1, blk, LANES), state),
                   pl.BlockSpec((1, SUBLANES, CONV_DIM), state)],
        scratch_shapes=[pltpu.VMEM((blk, LANES), F32), pltpu.VMEM((blk, LANES), F32),
                        pltpu.VMEM((SUBLANES, CONV_DIM), F32)])
    return pl.pallas_call(
        _ab_prompt_kernel,
        grid_spec=grid_spec,
        out_shape=[jax.ShapeDtypeStruct((n_batch * seq, D_MODEL), F32),
                   jax.ShapeDtypeStruct((n_batch, blk, LANES), F32),
                   jax.ShapeDtypeStruct((n_batch, blk, LANES), F32),
                   jax.ShapeDtypeStruct((n_batch, SUBLANES, CONV_DIM), F32)],
        compiler_params=_cparams(2),
        name="ab_prompt",
    )(sinks, z, x, cos, sin, cw, qg, kg, seg, wo)


def _ab_sample_kernel(sink_ref, z_ref, x_ref, cos_ref, sin_ref, cw_ref, qg_ref, kg_ref, seg_ref, wo_ref,
                      cc_ref, ck_ref, cv_ref, y_ref, cs_ref, ks_ref, vs_ref):
    rows = z_ref.shape[0]
    t_len = SUBLANES
    bt = rows // t_len
    cos = cos_ref[...]
    sin = sin_ref[...]
    seg = seg_ref[...]
    hi_half = (lax.broadcasted_iota(jnp.int32, (rows, LANES), 1) & 32) != 0

    gate_b = z_ref[:, 0:CONV_DIM]
    u = z_ref[:, CONV_DIM:2 * CONV_DIM] * z_ref[:, 2 * CONV_DIM:3 * CONV_DIM]
    u3 = u.reshape(bt, t_len, CONV_DIM)
    cc = cc_ref[...]
    c0 = cc[:, 0:1, :]
    c1 = cc[:, 1:2, :]
    t8 = lax.broadcasted_iota(jnp.int32, (bt, t_len, CONV_DIM), 1)
    r1 = pltpu.roll(u3, 1, 1)
    r2 = pltpu.roll(u3, 2, 1)
    um1 = jnp.where(t8 >= 1, r1, c1)
    um2 = jnp.where(t8 >= 2, r2, jnp.where(t8 == 1, c1, c0))
    cw = cw_ref[...]
    conv = cw[0:1][None] * um2 + cw[1:2][None] * um1 + cw[2:3][None] * u3
    yconv = gate_b * conv.reshape(rows, CONV_DIM)
    cs_ref[...] = r2[:, 0:2, :]

    q0 = 3 * CONV_DIM
    k0 = q0 + N_Q_HEADS * HEAD_DIM
    v0 = k0 + N_KV_HEADS * HEAD_DIM
    qg = qg_ref[...]
    qr = [_headnorm_rope(z_ref[:, q0 + c * LANES:q0 + (c + 1) * LANES], qg, cos, sin, seg, hi_half)
          for c in range(N_Q_HEADS * HEAD_DIM // LANES)]
    kr = _headnorm_rope(z_ref[:, k0:k0 + LANES], kg_ref[...], cos, sin, seg, hi_half)
    v = z_ref[:, v0:v0 + LANES]

    group = N_Q_HEADS // N_KV_HEADS
    nq = group * t_len
    nk = 2 * WINDOW
    qrow = lax.broadcasted_iota(jnp.int32, (nq, nk), 0)
    t_q = qrow & (t_len - 1)
    col = lax.broadcasted_iota(jnp.int32, (nq, nk), 1)
    mask = (((col < WINDOW) & (col > t_q)) | ((col >= WINDOW) & (col - WINDOW <= t_q)))[None]
    hrow = lax.broadcasted_iota(jnp.int32, (nq, 1), 0) // t_len
    pad = jnp.zeros((bt, nk - WINDOW - t_len, HEAD_DIM), F32)

    outs = [None] * N_Q_HEADS
    for g in range(N_KV_HEADS):
        sl = slice(g * HEAD_DIM, (g + 1) * HEAD_DIM)
        qs = jnp.concatenate(
            [qr[h // 2][:, (h % 2) * HEAD_DIM:(h % 2 + 1) * HEAD_DIM].reshape(bt, t_len, HEAD_DIM)
             for h in range(g * group, (g + 1) * group)], axis=1)
        kk = jnp.concatenate([ck_ref[:, :, sl], kr[:, sl].reshape(bt, t_len, HEAD_DIM), pad], axis=1)
        vv = jnp.concatenate([cv_ref[:, :, sl], v[:, sl].reshape(bt, t_len, HEAD_DIM), pad], axis=1)
        s = jnp.einsum('bqd,bkd->bqk', qs.astype(BF16), kk.astype(BF16),
                       preferred_element_type=F32) * (HEAD_DIM ** -0.5)
        sink = jnp.zeros((nq, 1), F32)
        for hh in range(group):
            sink = jnp.where(hrow == hh, sink_ref[g * group + hh], sink)
        p = _softmax_sink(s, mask, sink[None])
        o = jnp.einsum('bqk,bkd->bqd', p, vv.astype(BF16), preferred_element_type=F32)
        for hh in range(group):
            outs[g * group + hh] = o[:, hh * t_len:(hh + 1) * t_len, :].reshape(rows, HEAD_DIM)
    attn = jnp.concatenate(outs, axis=1)

    y = (jnp.dot(yconv.astype(BF16), wo_ref[0:CONV_DIM, :], preferred_element_type=F32)
         + jnp.dot(attn.astype(BF16), wo_ref[CONV_DIM:2 * CONV_DIM, :], preferred_element_type=F32))
    y_ref[...] = y + x_ref[...]

    keep = WINDOW - t_len
    ks_ref[:, 0:keep, :] = ck_ref[:, t_len:WINDOW, :]
    ks_ref[:, keep:WINDOW, :] = kr.reshape(bt, t_len, LANES)
    vs_ref[:, 0:keep, :] = cv_ref[:, t_len:WINDOW, :]
    vs_ref[:, keep:WINDOW, :] = v.reshape(bt, t_len, LANES)


def _ab_sample(z, x, row0, n_seq, t_len, cos, sin, cw, qg, kg, sinks, seg, wo, cc, ck, cv, bt=16):
    rows = bt * t_len
    blk0 = row0 // rows
    tok = lambda i, s: (blk0 + i, 0)
    const = lambda i, s: (0, 0)
    seq3 = lambda i, s: (i, 0, 0)
    grid_spec = pltpu.PrefetchScalarGridSpec(
        num_scalar_prefetch=1,
        grid=(n_seq // bt,),
        in_specs=[pl.BlockSpec((rows, AB_IN), tok),
                  pl.BlockSpec((rows, D_MODEL), tok),
                  pl.BlockSpec((rows, LANES), const),
                  pl.BlockSpec((rows, LANES), const),
                  pl.BlockSpec((3, CONV_DIM), const),
                  pl.BlockSpec((1, LANES), const),
                  pl.BlockSpec((1, LANES), const),
                  pl.BlockSpec((LANES, LANES), const),
                  pl.BlockSpec((D_MODEL, D_MODEL), const),
                  pl.BlockSpec((bt, 2, CONV_DIM), seq3),
                  pl.BlockSpec((bt, WINDOW, LANES), seq3),
                  pl.BlockSpec((bt, WINDOW, LANES), seq3)],
        out_specs=[pl.BlockSpec((rows, D_MODEL), lambda i, s: (i, 0)),
                   pl.BlockSpec((bt, 2, CONV_DIM), seq3),
                   pl.BlockSpec((bt, WINDOW, LANES), seq3),
                   pl.BlockSpec((bt, WINDOW, LANES), seq3)])
    return pl.pallas_call(
        _ab_sample_kernel,
        grid_spec=grid_spec,
        out_shape=[jax.ShapeDtypeStruct((n_seq * t_len, D_MODEL), F32),
                   jax.ShapeDtypeStruct((n_seq, 2, CONV_DIM), F32),
                   jax.ShapeDtypeStruct((n_seq, WINDOW, LANES), F32),
                   jax.ShapeDtypeStruct((n_seq, WINDOW, LANES), F32)],
        compiler_params=_cparams(1),
        name="ab_sample",
    )(sinks, z, x, cos, sin, cw, qg, kg, seg, wo, cc, ck, cv)


def _log_sigmoid(x):
    return jnp.minimum(x, 0.0) - jnp.log(1.0 + jnp.exp(-jnp.abs(x)))


def _mlstm_chunk(z, x, bias, og, tril, wo, c_src, n_src, m_src, c_dst, n_dst, m_dst, n_real):
    L = z.shape[0]
    gates = z[:, ML_GATE_COL:ML_GATE_COL + LANES] + bias
    if n_real < L:
        live = lax.broadcasted_iota(jnp.int32, (L, LANES), 0) < n_real
        li_all = jnp.where(live, gates, -1e30)
        lf_all = jnp.where(live, _log_sigmoid(gates), 0.0)
    else:
        li_all = gates
        lf_all = _log_sigmoid(gates)
    lf_pieces = _bf16_pieces(lf_all, 3)
    f_col_all = sum(jnp.dot(tril, p, preferred_element_type=F32) for p in lf_pieces)
    f_row_all = sum(lax.dot_general(p, tril, (((0,), (1,)), ((), ())), preferred_element_type=F32)
                    for p in lf_pieces)
    li_t = li_all.T
    rr = lax.broadcasted_iota(jnp.int32, (L, L), 0)
    cc = lax.broadcasted_iota(jnp.int32, (L, L), 1)
    causal = cc <= rr

    outs, m_new_all = [], []
    for h in range(ML_HEADS):
        f_col = f_col_all[:, ML_HEADS + h:ML_HEADS + h + 1]
        f_row = f_row_all[ML_HEADS + h:ML_HEADS + h + 1, :]
        li_row = li_t[h:h + 1, :]
        li_col = li_all[:, h:h + 1]
        m0 = m_src[0, 0:1, h:h + 1]
        c0 = c_src[0, h]
        n0 = n_src[0, h:h + 1, :]
        qh = z[:, h * ML_QK:(h + 1) * ML_QK]
        kh = z[:, ML_HEADS * ML_QK + h * ML_QK:ML_HEADS * ML_QK + (h + 1) * ML_QK] * (ML_QK ** -0.5)
        v_off = 2 * ML_HEADS * ML_QK
        vh = z[:, v_off + h * ML_V:v_off + (h + 1) * ML_V]
        o_off = v_off + ML_HEADS * ML_V
        oh = z[:, o_off + h * ML_V:o_off + (h + 1) * ML_V]
        qb = qh.astype(BF16)
        vb = vh.astype(BF16)

        dmat = jnp.where(causal, f_col - f_row + li_row, -jnp.inf)
        gcar = f_col + m0
        m_t = jnp.maximum(jnp.max(dmat, axis=-1, keepdims=True), gcar)
        w = jnp.exp(dmat - m_t)
        s = lax.dot_general(qb, kh.astype(BF16), NT_DIMS, preferred_element_type=F32) * w
        carry = jnp.exp(gcar - m_t)
        num = (jnp.dot(s.astype(BF16), vb, preferred_element_type=F32)
               + jnp.dot(qb, c0.astype(BF16), preferred_element_type=F32) * carry)
        den = jnp.sum(s, axis=-1, keepdims=True) + carry * jnp.sum(qh * n0, axis=-1, keepdims=True)
        hout = num / jnp.maximum(jnp.abs(den), jnp.exp(-m_t))

        f_last = f_col[L - 1:L, :]
        w_end = f_last - f_col + li_col
        m_new = jnp.maximum(f_last + m0, jnp.max(w_end, axis=0, keepdims=True))
        a_end = jnp.exp(w_end - m_new)
        scale = jnp.exp(f_last + m0 - m_new)
        ka = kh * a_end
        c_dst[0, h] = scale * c0 + lax.dot_general(ka.astype(BF16), vb, TN_DIMS, preferred_element_type=F32)
        n_dst[0, h:h + 1, :] = scale * n0 + jnp.sum(ka, axis=0, keepdims=True)
        m_new_all.append(m_new)

        hn = _rmsnorm(hout, og[:, h * ML_V:(h + 1) * ML_V])
        outs.append(jax.nn.sigmoid(oh) * hn)
    m_dst[0] = jnp.concatenate(m_new_all, axis=1)
    out = jnp.concatenate(outs, axis=1)
    return jnp.dot(out.astype(BF16), wo, preferred_element_type=F32) + x


def _mlstm_prompt_kernel(z_ref, x_ref, bias_ref, og_ref, tril_ref, wo_ref, y_ref, c_ref, n_ref, m_ref):
    @pl.when(pl.program_id(1) == 0)
    def _():
        c_ref[...] = jnp.zeros_like(c_ref)
        n_ref[...] = jnp.zeros_like(n_ref)
        m_ref[...] = jnp.zeros_like(m_ref)

    y_ref[...] = _mlstm_chunk(z_ref[...], x_ref[...], bias_ref[...], og_ref[...], tril_ref[...], wo_ref[...],
                              c_ref, n_ref, m_ref, c_ref, n_ref, m_ref, ML_CHUNK)


def _mlstm_sample_kernel(z_ref, x_ref, bias_ref, og_ref, tril_ref, wo_ref, c0_ref, n0_ref, m0_ref,
                         y_ref, c_ref, n_ref, m_ref):
    t_len = z_ref.shape[0]
    zpad = jnp.concatenate([z_ref[...], jnp.zeros((ML_CHUNK - t_len, ML_IN_PAD), F32)], axis=0)
    xpad = jnp.concatenate([x_ref[...], jnp.zeros((ML_CHUNK - t_len, D_MODEL), F32)], axis=0)
    y = _mlstm_chunk(zpad, xpad, bias_ref[...], og_ref[...], tril_ref[...], wo_ref[...],
                     c0_ref, n0_ref, m0_ref, c_ref, n_ref, m_ref, t_len)
    y_ref[...] = y[0:t_len, :]


def _mlstm_weight_specs(const):
    return [pl.BlockSpec((1, LANES), const),
            pl.BlockSpec((1, D_MODEL), const),
            pl.BlockSpec((ML_CHUNK, ML_CHUNK), const),
            pl.BlockSpec((D_MODEL, D_MODEL), const)]


def _mlstm_prompt(z, x, n_batch, seq, bias, og, tril, wo):
    nc = seq // ML_CHUNK
    tok = lambda b, j: (b * nc + j, 0)
    const = lambda b, j: (0, 0)
    return pl.pallas_call(
        _mlstm_prompt_kernel,
        grid=(n_batch, nc),
        in_specs=[pl.BlockSpec((ML_CHUNK, ML_IN_PAD), tok),
                  pl.BlockSpec((ML_CHUNK, D_MODEL), tok)] + _mlstm_weight_specs(const),
        out_specs=[pl.BlockSpec((ML_CHUNK, D_MODEL), tok),
                   pl.BlockSpec((1, ML_HEADS, ML_QK, ML_V), lambda b, j: (b, 0, 0, 0)),
                   pl.BlockSpec((1, ML_HEADS, ML_QK), lambda b, j: (b, 0, 0)),
                   pl.BlockSpec((1, 1, ML_HEADS), lambda b, j: (b, 0, 0))],
        out_shape=[jax.ShapeDtypeStruct((n_batch * seq, D_MODEL), F32),
                   jax.ShapeDtypeStruct((n_batch, ML_HEADS, ML_QK, ML_V), F32),
                   jax.ShapeDtypeStruct((n_batch, ML_HEADS, ML_QK), F32),
                   jax.ShapeDtypeStruct((n_batch, 1, ML_HEADS), F32)],
        compiler_params=_cparams(2),
        name="mlstm_prompt",
    )(z, x, bias, og, tril, wo)


def _mlstm_sample(z, x, row0, n_seq, t_len, bias, og, tril, wo, c0, n0, m0):
    blk0 = row0 // t_len
    tok = lambda i: (blk0 + i, 0)
    const = lambda i: (0, 0)
    st4 = lambda i: (i, 0, 0, 0)
    st3 = lambda i: (i, 0, 0)
    state_specs = [pl.BlockSpec((1, ML_HEADS, ML_QK, ML_V), st4),
                   pl.BlockSpec((1, ML_HEADS, ML_QK), st3),
                   pl.BlockSpec((1, 1, ML_HEADS), st3)]
    return pl.pallas_call(
        _mlstm_sample_kernel,
        grid=(n_seq,),
        in_specs=[pl.BlockSpec((t_len, ML_IN_PAD), tok),
                  pl.BlockSpec((t_len, D_MODEL), tok)] + _mlstm_weight_specs(const) + state_specs,
        out_specs=[pl.BlockSpec((t_len, D_MODEL), lambda i: (i, 0))] + state_specs,
        out_shape=[jax.ShapeDtypeStruct((n_seq * t_len, D_MODEL), F32),
                   jax.ShapeDtypeStruct((n_seq, ML_HEADS, ML_QK, ML_V), F32),
                   jax.ShapeDtypeStruct((n_seq, ML_HEADS, ML_QK), F32),
                   jax.ShapeDtypeStruct((n_seq, 1, ML_HEADS), F32)],
        compiler_params=_cparams(1),
        name="mlstm_sample",
    )(z, x, bias, og, tril, wo, c0, n0, m0)


def _topk_rows(s, k, payload=None):
    n = s.shape[0]
    rows = lax.broadcasted_iota(jnp.int32, s.shape, 0).astype(F32)
    vals, picks = [], []
    for _ in range(k):
        m = jnp.max(s, axis=0, keepdims=True)
        first = jnp.min(jnp.where(s == m, rows, float(n)), axis=0, keepdims=True)
        sel = rows == first
        vals.append(m)
        if payload is None:
            picks.append(first)
        else:
            picks.append(jnp.max(jnp.where(sel, payload, -1.0), axis=0, keepdims=True))
        s = jnp.where(sel, -jnp.inf, s)
    return jnp.concatenate(vals, axis=0), jnp.concatenate(picks, axis=0)


def _peer_select_kernel(x_ref, g_ref, wq_ref, sk_ref, xn_ref, e_ref, gate_ref, q_scr):
    h = pl.program_id(1)

    @pl.when(h == 0)
    def _():
        xn = _rmsnorm(x_ref[...], g_ref[...]).astype(BF16)
        xn_ref[...] = xn
        q = jnp.dot(xn, wq_ref[...], preferred_element_type=F32)
        for c in range(2 * PEER_HEADS):
            q_scr[c] = q[:, c * LANES:(c + 1) * LANES]

    vals, idxs = [], []
    for p in range(2):
        qhp = q_scr[2 * h + p].astype(BF16)
        st = lax.dot_general(sk_ref[0, p], qhp, NT_DIMS, preferred_element_type=F32)
        v_, i_ = _topk_rows(st, PEER_TOPK)
        vals.append(v_)
        idxs.append(i_)
    def pairs(first, second, combine):
        h8 = SUBLANES
        rows = [combine(first[0:1], second)]
        rows += [combine(first[k1:k1 + 1], second[0:h8]) for k1 in range(1, h8)]
        rows.append(combine(first[h8:PEER_TOPK], second[0:1]))
        return jnp.concatenate(rows, axis=0)

    cand = pairs(vals[0], vals[1], lambda a, b: a + b)
    expert = pairs(idxs[0], idxs[1], lambda a, b: a * N_KEYS + b)
    best, e_sel = _topk_rows(cand, PEER_TOPK, payload=expert)
    ex = jnp.exp(best - best[0:1])
    e_ref[...] = e_sel
    gate_ref[...] = ex / jnp.sum(ex, axis=0, keepdims=True)


def _peer_select(x, g, wq, sk, t=256):
    n, d = x.shape
    nslot = PEER_HEADS * PEER_TOPK
    return pl.pallas_call(
        _peer_select_kernel,
        grid=(n // t, PEER_HEADS),
        in_specs=[pl.BlockSpec((t, d), lambda i, h: (i, 0)),
                  pl.BlockSpec((1, d), lambda i, h: (0, 0)),
                  pl.BlockSpec((d, 2 * PEER_HEADS * LANES), lambda i, h: (0, 0)),
                  pl.BlockSpec((1, 2, N_KEYS, LANES), lambda i, h: (h, 0, 0, 0))],
        out_specs=[pl.BlockSpec((t, d), lambda i, h: (i, 0)),
                   pl.BlockSpec((PEER_TOPK, t), lambda i, h: (h, i)),
                   pl.BlockSpec((PEER_TOPK, t), lambda i, h: (h, i))],
        out_shape=[jax.ShapeDtypeStruct((n, d), BF16),
                   jax.ShapeDtypeStruct((nslot, n), F32),
                   jax.ShapeDtypeStruct((nslot, n), F32)],
        scratch_shapes=[pltpu.VMEM((2 * PEER_HEADS, t, LANES), F32)],
        compiler_params=_cparams(2),
        name="peer_select",
    )(x, g.reshape(1, d), wq, sk)


def _gelu(x):
    return 0.5 * x * (1.0 + lax.erf(x * 0.7071067811865476))


def _peer_main_kernel(xn_ref, e_ref, gate_ref, x_ref, u_ref, v_ref, o_ref,
                      a_scr, b_scr, g_scr, gmat, coef_scr):
    e = pl.program_id(1)
    t = xn_ref.shape[0]
    half = N_KEYS // 2

    @pl.when(e == 0)
    def _():
        ef = e_ref[...]
        af = jnp.floor(ef * (1.0 / N_KEYS))
        a_scr[...] = af.T
        b_scr[...] = (ef - af * N_KEYS).T
        g_scr[...] = gate_ref[...].T
        o_ref[...] = x_ref[...]
        coef_scr[...] = jnp.zeros_like(coef_scr)
        r = lax.broadcasted_iota(jnp.int32, (N_KEYS, LANES), 0)
        packed_shape = (N_KEYS // (2 * SUBLANES), 2 * SUBLANES, LANES)
        a_of_row = jnp.where(r < half, 2 * r, 2 * (r - half) + 1).astype(F32).astype(BF16).reshape(packed_shape)
        b_of_row = r.astype(F32).astype(BF16).reshape(packed_shape)
        one = jnp.ones(packed_shape, BF16)
        zero = jnp.zeros(packed_shape, BF16)

        def body(n8, carry):
            base = pl.multiple_of(n8 * SUBLANES, SUBLANES)
            a8 = a_scr[pl.ds(base, SUBLANES), :]
            b8 = b_scr[pl.ds(base, SUBLANES), :]
            g8 = g_scr[pl.ds(base, SUBLANES), :]
            for i in range(SUBLANES):
                arow = jnp.broadcast_to(a8[i:i + 1], (2 * SUBLANES, LANES)).astype(BF16)[None]
                brow = jnp.broadcast_to(b8[i:i + 1], (2 * SUBLANES, LANES)).astype(BF16)[None]
                grow = jnp.broadcast_to(g8[i:i + 1], (2 * SUBLANES, LANES)).astype(BF16)[None]
                pa = jnp.where(a_of_row == arow, one, zero).reshape(N_KEYS, LANES)
                qb = jnp.where(b_of_row == brow, grow, zero).reshape(N_KEYS, LANES)
                tile = lax.dot_general(pa, qb, NT_DIMS, preferred_element_type=F32)
                row0 = pl.multiple_of((base + i) * G_PITCH, SUBLANES)
                gmat[pl.ds(row0, half), :] = pltpu.pack_elementwise(
                    [tile[0:half], tile[half:N_KEYS]], packed_dtype=BF16)
            return carry

        lax.fori_loop(0, t // SUBLANES, body, 0)

    o_ref[...] += jnp.dot(coef_scr[...], v_ref[...], preferred_element_type=F32)

    act = lax.dot_general(xn_ref[...], u_ref[...], NT_DIMS, preferred_element_type=F32)
    blk = jnp.minimum(e, pl.num_programs(1) - 2)
    words_per_step = u_ref.shape[0] // (2 * N_KEYS)
    gates = []
    for i in range(words_per_step):
        word = gmat[pl.ds(blk * words_per_step + i, t, stride=G_PITCH), :]
        gates.append(lax.bitcast_convert_type(word << 16, F32))
        gates.append(lax.bitcast_convert_type(word & jnp.int32(-65536), F32))
    coef_scr[...] = (_gelu(act) * jnp.concatenate(gates, axis=1)).astype(BF16)


def _peer_main(xn, e_sel, gate, x, u, v, t=512, eb=1024):
    n, d = x.shape
    nslot = PEER_HEADS * PEER_TOPK
    n_blk = N_KEYS * N_KEYS // eb
    return pl.pallas_call(
        _peer_main_kernel,
        grid=(n // t, n_blk + 1),
        in_specs=[pl.BlockSpec((t, d), lambda i, e: (i, 0)),
                  pl.BlockSpec((nslot, t), lambda i, e: (0, i)),
                  pl.BlockSpec((nslot, t), lambda i, e: (0, i)),
                  pl.BlockSpec((t, d), lambda i, e: (i, 0)),
                  pl.BlockSpec((eb, d), lambda i, e: (jnp.minimum(e, n_blk - 1), 0)),
                  pl.BlockSpec((eb, d), lambda i, e: (jnp.maximum(e - 1, 0), 0))],
        out_specs=pl.BlockSpec((t, d), lambda i, e: (i, 0)),
        out_shape=jax.ShapeDtypeStruct((n, d), F32),
        scratch_shapes=[pltpu.VMEM((t, nslot), F32), pltpu.VMEM((t, nslot), F32), pltpu.VMEM((t, nslot), F32),
                        pltpu.VMEM((t * G_PITCH, LANES), jnp.int32), pltpu.VMEM((t, eb), BF16)],
        compiler_params=_cparams(2),
        name="peer_main",
    )(xn, e_sel, gate, x, u, v)


SEL_T = 256


def _unit_scores(q_scr, sk_ref, half, h):
    return [lax.dot_general(sk_ref[h, p], q_scr[half, 2 * h + p].astype(BF16), NT_DIMS,
                            preferred_element_type=F32) for p in range(2)]


def _unit_topk(scores):
    vals, idxs = [], []
    for st in scores:
        v_, i_ = _topk_rows(st, PEER_TOPK)
        vals.append(v_)
        idxs.append(i_)

    def pairs(first, second, combine):
        h8 = SUBLANES
        rows = [combine(first[0:1], second)]
        rows += [combine(first[k1:k1 + 1], second[0:h8]) for k1 in range(1, h8)]
        rows.append(combine(first[h8:PEER_TOPK], second[0:1]))
        return jnp.concatenate(rows, axis=0)

    cand = pairs(vals[0], vals[1], lambda a, b: a + b)
    expert = pairs(idxs[0], idxs[1], lambda a, b: a * N_KEYS + b)
    best, e_sel = _topk_rows(cand, PEER_TOPK, payload=expert)
    ex = jnp.exp(best - best[0:1])
    return e_sel, ex / jnp.sum(ex, axis=0, keepdims=True)


def _peer_fused_kernel(xs_ref, xm_ref, g_ref, wq_ref, sk_ref, u_ref, v_ref, o_ref,
                       q_scr, sel_e, sel_g, xn_scr, a_scr, b_scr, g_scr, gmat, coef_scr):
    i = pl.program_id(0)
    e = pl.program_id(1)
    n_steps = pl.num_programs(1)
    t = xm_ref.shape[0]
    n_half = t // SEL_T
    half_keys = N_KEYS // 2
    cur = i % 2

    def project_queries():
        xn = _rmsnorm(xs_ref[...], g_ref[...]).astype(BF16)
        q = jnp.dot(xn, wq_ref[...], preferred_element_type=F32)
        for hf in range(n_half):
            for c in range(2 * PEER_HEADS):
                q_scr[hf, c] = q[hf * SEL_T:(hf + 1) * SEL_T, c * LANES:(c + 1) * LANES]

    @pl.when(e == 0)
    def _():
        @pl.when(i == 0)
        def _():
            project_queries()
            sel_e[...] = jnp.zeros_like(sel_e)
            sel_g[...] = jnp.zeros_like(sel_g)

        x = xm_ref[...]
        xn_scr[...] = _rmsnorm(x, g_ref[...]).astype(BF16)
        o_ref[...] = x
        coef_scr[...] = jnp.zeros_like(coef_scr)
        prev = 1 - cur
        for hf in range(n_half):
            ef = sel_e[prev, hf]
            af = jnp.floor(ef * (1.0 / N_KEYS))
            rows = slice(hf * SEL_T, (hf + 1) * SEL_T)
            a_scr[rows, :] = af.T
            b_scr[rows, :] = (ef - af * N_KEYS).T
            g_scr[rows, :] = sel_g[prev, hf].T
        r = lax.broadcasted_iota(jnp.int32, (N_KEYS, LANES), 0)
        packed_shape = (N_KEYS // (2 * SUBLANES), 2 * SUBLANES, LANES)
        a_of_row = jnp.where(r < half_keys, 2 * r, 2 * (r - half_keys) + 1).astype(F32).astype(BF16)
        a_of_row = a_of_row.reshape(packed_shape)
        b_of_row = r.astype(F32).astype(BF16).reshape(packed_shape)
        one = jnp.ones(packed_shape, BF16)
        zero = jnp.zeros(packed_shape, BF16)

        def body(n8, carry):
            base = pl.multiple_of(n8 * SUBLANES, SUBLANES)
            a8 = a_scr[pl.ds(base, SUBLANES), :]
            b8 = b_scr[pl.ds(base, SUBLANES), :]
            g8 = g_scr[pl.ds(base, SUBLANES), :]
            for k in range(SUBLANES):
                arow = jnp.broadcast_to(a8[k:k + 1], (2 * SUBLANES, LANES)).astype(BF16)[None]
                brow = jnp.broadcast_to(b8[k:k + 1], (2 * SUBLANES, LANES)).astype(BF16)[None]
                grow = jnp.broadcast_to(g8[k:k + 1], (2 * SUBLANES, LANES)).astype(BF16)[None]
                pa = jnp.where(a_of_row == arow, one, zero).reshape(N_KEYS, LANES)
                qb = jnp.where(b_of_row == brow, grow, zero).reshape(N_KEYS, LANES)
                tile = lax.dot_general(pa, qb, NT_DIMS, preferred_element_type=F32)
                row0 = pl.multiple_of((base + k) * G_PITCH, SUBLANES)
                gmat[pl.ds(row0, half_keys), :] = pltpu.pack_elementwise(
                    [tile[0:half_keys], tile[half_keys:N_KEYS]], packed_dtype=BF16)
            return carry

        lax.fori_loop(0, t // SUBLANES, body, 0)

    unit = jnp.minimum(e, n_half * PEER_HEADS - 1)
    half = lax.shift_right_logical(unit, PEER_HEADS.bit_length() - 1)
    h = unit & (PEER_HEADS - 1)
    scores = _unit_scores(q_scr, sk_ref, half, h)

    o_ref[...] += jnp.dot(coef_scr[...], v_ref[...], preferred_element_type=F32)
    act = lax.dot_general(xn_scr[...], u_ref[...], NT_DIMS, preferred_element_type=F32)
    blk = jnp.minimum(e, n_steps - 2)
    words_per_step = u_ref.shape[0] // (2 * N_KEYS)
    gates = []
    for w in range(words_per_step):
        word = gmat[pl.ds(blk * words_per_step + w, t, stride=G_PITCH), :]
        gates.append(lax.bitcast_convert_type(word << 16, F32))
        gates.append(lax.bitcast_convert_type(word & jnp.int32(-65536), F32))
    coef_scr[...] = (_gelu(act) * jnp.concatenate(gates, axis=1)).astype(BF16)

    e_sel, gate = _unit_topk(scores)
    slot0 = pl.multiple_of(h * PEER_TOPK, PEER_TOPK)
    sel_e[cur, half, pl.ds(slot0, PEER_TOPK), :] = e_sel
    sel_g[cur, half, pl.ds(slot0, PEER_TOPK), :] = gate

    @pl.when(e == n_steps - 1)
    def _():
        project_queries()


def _peer(x, g, wq, sk, u, v, t=512, eb=1024):
    n, d = x.shape
    nslot = PEER_HEADS * PEER_TOPK
    n_tok_blk = n // t
    n_exp_blk = N_KEYS * N_KEYS // eb
    n_half = t // SEL_T
    assert n_exp_blk == n_half * PEER_HEADS
    last_tok = n_tok_blk - 1
    once = pl.Buffered(1)
    return pl.pallas_call(
        _peer_fused_kernel,
        grid=(n_tok_blk + 1, n_exp_blk + 1),
        in_specs=[pl.BlockSpec((t, d), lambda i, e: (jnp.minimum(i + e // n_exp_blk, last_tok), 0),
                               pipeline_mode=once),
                  pl.BlockSpec((t, d), lambda i, e: (jnp.maximum(i - 1, 0), 0)),
                  pl.BlockSpec((1, d), lambda i, e: (0, 0)),
                  pl.BlockSpec((d, 2 * PEER_HEADS * LANES), lambda i, e: (0, 0), pipeline_mode=once),
                  pl.BlockSpec((PEER_HEADS, 2, N_KEYS, LANES), lambda i, e: (0, 0, 0, 0), pipeline_mode=once),
                  pl.BlockSpec((eb, d), lambda i, e: (jnp.minimum(e, n_exp_blk - 1), 0)),
                  pl.BlockSpec((eb, d), lambda i, e: (jnp.maximum(e - 1, 0), 0))],
        out_specs=pl.BlockSpec((t, d), lambda i, e: (jnp.maximum(i - 1, 0), 0)),
        out_shape=jax.ShapeDtypeStruct((n, d), F32),
        scratch_shapes=[pltpu.VMEM((n_half, 2 * PEER_HEADS, SEL_T, LANES), F32),
                        pltpu.VMEM((2, n_half, nslot, SEL_T), F32),
                        pltpu.VMEM((2, n_half, nslot, SEL_T), F32),
                        pltpu.VMEM((t, d), BF16),
                        pltpu.VMEM((t, nslot), F32), pltpu.VMEM((t, nslot), F32), pltpu.VMEM((t, nslot), F32),
                        pltpu.VMEM((t * G_PITCH, LANES), jnp.int32),
                        pltpu.VMEM((t, eb), BF16)],
        compiler_params=_cparams(2),
        name="peer",
    )(x, x, g.reshape(1, d), wq, sk, u, v)


def _rope_tables(pos):
    half = HEAD_DIM // 2
    inv_freq = ROPE_THETA ** (-jnp.arange(half, dtype=F32) / half)
    ang = pos.astype(F32)[:, None] * inv_freq[None, :]
    cos, sin = jnp.cos(ang), jnp.sin(ang)
    reps = LANES // HEAD_DIM
    cos_t = jnp.tile(jnp.concatenate([cos, cos], axis=1), (1, reps))
    sin_t = jnp.tile(jnp.concatenate([-sin, sin], axis=1), (1, reps))
    return cos_t, sin_t


def kernel(x_prompt, x_sample, cache_conv, cache_win_k, cache_win_v, state_mlstm_C, state_mlstm_n,
           state_mlstm_m, norm_mix, norm_ffn, ab_w_in, ab_conv_w, ab_q_gain, ab_k_gain, ab_sinks, ab_w_out,
           ml_w_in, ml_gate_bias, ml_out_gain, ml_w_out, peer_w_q, peer_sub_keys, peer_u, peer_v):
    n_batch, seq, d = x_prompt.shape
    n_seq, t_len, _ = x_sample.shape
    n_prompt = n_batch * seq
    assert d == D_MODEL and t_len == SUBLANES and norm_mix.shape[0] == 2

    x = jnp.concatenate([x_prompt.reshape(n_prompt, d), x_sample.reshape(n_seq * t_len, d)], axis=0)

    cos_p, sin_p = _rope_tables(jnp.arange(seq, dtype=jnp.int32))
    cos_s, sin_s = _rope_tables(PAST_LEN + jnp.arange(t_len, dtype=jnp.int32))
    bt = 16
    cos_s, sin_s = jnp.tile(cos_s, (bt, 1)), jnp.tile(sin_s, (bt, 1))
    lane = jnp.arange(LANES)
    seg = (lane[:, None] // HEAD_DIM == lane[None, :] // HEAD_DIM).astype(BF16)
    reps = LANES // HEAD_DIM
    qg = jnp.tile(ab_q_gain[0], reps).reshape(1, LANES)
    kg = jnp.tile(ab_k_gain[0], reps).reshape(1, LANES)
    wo_ab = ab_w_out[0].astype(BF16)

    z = _norm_proj(x, norm_mix[0], ab_w_in[0].astype(BF16))
    y_p, k_p, v_p, c_p = _ab_prompt(z, x, n_batch, seq, cos_p, sin_p, ab_conv_w[0], qg, kg, ab_sinks[0],
                                    seg, wo_ab)
    y_s, c_s, k_s, v_s = _ab_sample(z, x, n_prompt, n_seq, t_len, cos_s, sin_s, ab_conv_w[0], qg, kg,
                                    ab_sinks[0], seg, wo_ab, cache_conv[0],
                                    cache_win_k[0].reshape(n_seq, WINDOW, LANES),
                                    cache_win_v[0].reshape(n_seq, WINDOW, LANES), bt=bt)
    x = jnp.concatenate([y_p, y_s], axis=0)
    x = _peer(x, norm_ffn[0], peer_w_q[0].astype(BF16), peer_sub_keys[0].astype(BF16),
              peer_u[0].astype(BF16), peer_v[0].astype(BF16))

    n_gate = 2 * ML_HEADS
    w_in = jnp.pad(ml_w_in[0], ((0, 0), (0, ML_IN_PAD - ml_w_in.shape[2]))).astype(BF16)
    bias = jnp.pad(ml_gate_bias[0], (0, LANES - n_gate)).reshape(1, LANES)
    og = ml_out_gain[0].reshape(1, D_MODEL)
    idx = jnp.arange(ML_CHUNK)
    tril = (idx[None, :] <= idx[:, None]).astype(BF16)
    wo_ml = ml_w_out[0].astype(BF16)

    z = _norm_proj(x, norm_mix[1], w_in)
    y_p, cm_p, nm_p, mm_p = _mlstm_prompt(z, x, n_batch, seq, bias, og, tril, wo_ml)
    y_s, cm_s, nm_s, mm_s = _mlstm_sample(z, x, n_prompt, n_seq, t_len, bias, og, tril, wo_ml,
                                          state_mlstm_C[0], state_mlstm_n[0],
                                          state_mlstm_m[0].reshape(n_seq, 1, ML_HEADS))
    x = jnp.concatenate([y_p, y_s], axis=0)
    x = _peer(x, norm_ffn[1], peer_w_q[1].astype(BF16), peer_sub_keys[1].astype(BF16),
              peer_u[1].astype(BF16), peer_v[1].astype(BF16))

    y_prompt = x[:n_prompt].reshape(n_batch, seq, d)
    y_sample = x[n_prompt:].reshape(n_seq, t_len, d)
    kv_shape_p = (1, n_batch, WINDOW, N_KV_HEADS, HEAD_DIM)
    kv_shape_s = (1, n_seq, WINDOW, N_KV_HEADS, HEAD_DIM)
    return (y_prompt, y_sample,
            c_p[:, SUBLANES - 2:, :][None], k_p.reshape(kv_shape_p), v_p.reshape(kv_shape_p),
            cm_p[None], nm_p[None], mm_p.reshape(1, n_batch, ML_HEADS),
            c_s[None], k_s.reshape(kv_shape_s), v_s.reshape(kv_shape_s),
            cm_s[None], nm_s[None], mm_s.reshape(1, n_seq, ML_HEADS))
```

```python
import functools

import jax
import jax.numpy as jnp
from jax import lax
from jax.experimental import pallas as pl
from jax.experimental.pallas import tpu as pltpu

F32 = jnp.float32
BF16 = jnp.bfloat16
EPS = 1e-6

D_MODEL = 1024
CONV_DIM = 512
N_Q_HEADS = 8
N_KV_HEADS = 2
HEAD_DIM = 64
WINDOW = 128
ROPE_THETA = 10000.0
AB_IN = 2304
ML_HEADS = 4
ML_QK = 128
ML_V = 256
ML_CHUNK = 128
ML_GATE_COL = 3072
ML_IN_PAD = ML_GATE_COL + 128
N_KEYS = 128
PEER_HEADS = 8
PEER_TOPK = 16
PAST_LEN = 16384

LANES = 128
SUBLANES = 8
G_PITCH = N_KEYS // 2 + SUBLANES
VMEM_LIMIT = 56 * 1024 * 1024

NT_DIMS = (((1,), (1,)), ((), ()))
TN_DIMS = (((0,), (0,)), ((), ()))


def _cparams(n_axes, vmem=VMEM_LIMIT):
    return pltpu.CompilerParams(dimension_semantics=("arbitrary",) * n_axes, vmem_limit_bytes=vmem)


def _rmsnorm(x, g):
    return x * lax.rsqrt(jnp.mean(x * x, axis=-1, keepdims=True) + EPS) * g


def _bf16_pieces(a, terms):
    pieces = []
    rem = a
    for _ in range(terms):
        piece = rem.astype(BF16)
        rem = rem - piece.astype(F32)
        pieces.append(piece)
    return pieces


def _split_dot(a, b_bf16, terms=2):
    return sum(jnp.dot(p, b_bf16, preferred_element_type=F32) for p in _bf16_pieces(a, terms))


def _norm_proj_kernel(x_ref, g_ref, w_ref, o_ref):
    r = _rmsnorm(x_ref[...], g_ref[...])
    o_ref[...] = jnp.dot(r.astype(BF16), w_ref[...], preferred_element_type=F32)


def _norm_proj(x, g, w_bf16, tm=512):
    n, d = x.shape
    nout = w_bf16.shape[1]
    return pl.pallas_call(
        _norm_proj_kernel,
        grid=(n // tm,),
        in_specs=[pl.BlockSpec((tm, d), lambda i: (i, 0)),
                  pl.BlockSpec((1, d), lambda i: (0, 0)),
                  pl.BlockSpec((d, nout), lambda i: (0, 0))],
        out_specs=pl.BlockSpec((tm, nout), lambda i: (i, 0)),
        out_shape=jax.ShapeDtypeStruct((n, nout), F32),
        compiler_params=_cparams(1),
        name="norm_proj",
    )(x, g.reshape(1, d), w_bf16)


def _headnorm_rope(xc, gain, cos, sin, seg, hi_half):
    ss = _split_dot(xc * xc, seg)
    xn = xc * lax.rsqrt(ss * (1.0 / HEAD_DIM) + EPS) * gain
    partner = jnp.where(hi_half, pltpu.roll(xn, 32, 1), pltpu.roll(xn, 96, 1))
    return xn * cos + partner * sin


def _softmax_sink(s, mask, sink):
    s = jnp.where(mask, s, -1e30)
    m = jnp.maximum(jnp.max(s, axis=-1, keepdims=True), sink)
    p = jnp.exp(s - m)
    denom = jnp.sum(p, axis=-1, keepdims=True) + jnp.exp(sink - m)
    return (p / denom).astype(BF16)


def _ab_prompt_kernel(sink_ref, *refs, n_batch):
    z_refs, x_refs = refs[0:n_batch], refs[n_batch:2 * n_batch]
    (cos_ref, sin_ref, cw_ref, qg_ref, kg_ref, seg_ref, wo_ref,
     y_ref, kst_ref, vst_ref, cst_ref, pk_ref, pv_ref, pu_ref) = refs[2 * n_batch:]
    j = pl.program_id(0)

    @pl.when(j == 0)
    def _():
        pk_ref[...] = jnp.zeros_like(pk_ref)
        pv_ref[...] = jnp.zeros_like(pv_ref)
        pu_ref[...] = jnp.zeros_like(pu_ref)

    for b in range(n_batch):
        _ab_prompt_block(j, sink_ref, z_refs[b], x_refs[b], cos_ref, sin_ref, cw_ref, qg_ref, kg_ref, seg_ref,
                         wo_ref, y_ref.at[b], kst_ref.at[b], vst_ref.at[b], cst_ref.at[b],
                         pk_ref.at[b], pv_ref.at[b], pu_ref.at[b])


def _ab_prompt_block(j, sink_ref, z_ref, x_ref, cos_ref, sin_ref, cw_ref, qg_ref, kg_ref, seg_ref, wo_ref,
                     y_ref, kst_ref, vst_ref, cst_ref, pk_ref, pv_ref, pu_ref):
    blk = z_ref.shape[0]
    cos = cos_ref[...]
    sin = sin_ref[...]
    seg = seg_ref[...]
    hi_half = (lax.broadcasted_iota(jnp.int32, (blk, LANES), 1) & 32) != 0

    gate_b = z_ref[:, 0:CONV_DIM]
    u = z_ref[:, CONV_DIM:2 * CONV_DIM] * z_ref[:, 2 * CONV_DIM:3 * CONV_DIM]
    ng = blk // SUBLANES
    u3 = u.reshape(ng, SUBLANES, CONV_DIM)
    ext = jnp.concatenate([pu_ref[...][None], u3], axis=0)
    t8 = lax.broadcasted_iota(jnp.int32, (ng, SUBLANES, CONV_DIM), 1)
    r1 = pltpu.roll(ext, 1, 1)
    r2 = pltpu.roll(ext, 2, 1)
    um1 = jnp.where(t8 >= 1, r1[1:], r1[:-1])
    um2 = jnp.where(t8 >= 2, r2[1:], r2[:-1])
    cw = cw_ref[...]
    conv = cw[0:1][None] * um2 + cw[1:2][None] * um1 + cw[2:3][None] * u3
    yconv = gate_b * conv.reshape(blk, CONV_DIM)

    q0 = 3 * CONV_DIM
    k0 = q0 + N_Q_HEADS * HEAD_DIM
    v0 = k0 + N_KV_HEADS * HEAD_DIM
    qg = qg_ref[...]
    qr = [_headnorm_rope(z_ref[:, q0 + c * LANES:q0 + (c + 1) * LANES], qg, cos, sin, seg, hi_half)
          for c in range(N_Q_HEADS * HEAD_DIM // LANES)]
    kr = _headnorm_rope(z_ref[:, k0:k0 + LANES], kg_ref[...], cos, sin, seg, hi_half)
    v = z_ref[:, v0:v0 + LANES]
    pk = pk_ref[...]
    pv = pv_ref[...]

    row = lax.broadcasted_iota(jnp.int32, (blk, 2 * blk), 0)
    col = lax.broadcasted_iota(jnp.int32, (blk, 2 * blk), 1)
    row_prev = row + jnp.where(j == 0, 2 * blk, 0)
    mask = ((col < blk) & (col > row_prev)) | ((col >= blk) & (col - blk <= row))

    kks, vvs = [], []
    for g in range(N_KV_HEADS):
        sl = slice(g * HEAD_DIM, (g + 1) * HEAD_DIM)
        kks.append(jnp.concatenate([pk[:, sl], kr[:, sl]], axis=0).astype(BF16))
        vvs.append(jnp.concatenate([pv[:, sl], v[:, sl]], axis=0).astype(BF16))
    outs = []
    for h in range(N_Q_HEADS):
        g = h // (N_Q_HEADS // N_KV_HEADS)
        qh = qr[h // 2][:, (h % 2) * HEAD_DIM:(h % 2 + 1) * HEAD_DIM].astype(BF16)
        s = lax.dot_general(qh, kks[g], NT_DIMS, preferred_element_type=F32) * (HEAD_DIM ** -0.5)
        p = _softmax_sink(s, mask, sink_ref[h])
        outs.append(jnp.dot(p, vvs[g], preferred_element_type=F32))
    attn = jnp.concatenate(outs, axis=1)

    y = (jnp.dot(yconv.astype(BF16), wo_ref[0:CONV_DIM, :], preferred_element_type=F32)
         + jnp.dot(attn.astype(BF16), wo_ref[CONV_DIM:2 * CONV_DIM, :], preferred_element_type=F32))
    y_ref[...] = y + x_ref[...]

    pk_ref[...] = kr
    pv_ref[...] = v
    pu_ref[...] = u3[ng - 1]
    kst_ref[...] = kr
    vst_ref[...] = v
    cst_ref[...] = u3[ng - 1]


def _ab_prompt(z, x, n_batch, seq, cos, sin, cw, qg, kg, sinks, seg, wo):
    blk = WINDOW
    nb = seq // blk
    const = lambda j, s: (0, 0)
    whole = lambda j, s: (0, 0, 0)
    tok_specs = lambda width: [pl.BlockSpec((blk, width), functools.partial(lambda j, s, b: (b * nb + j, 0), b=b))
                               for b in range(n_batch)]
    grid_spec = pltpu.PrefetchScalarGridSpec(
        num_scalar_prefetch=1,
        grid=(nb,),
        in_specs=tok_specs(AB_IN) + tok_specs(D_MODEL) + [
            pl.BlockSpec((blk, LANES), lambda j, s: (j, 0)),
            pl.BlockSpec((blk, LANES), lambda j, s: (j, 0)),
            pl.BlockSpec((3, CONV_DIM), const),
            pl.BlockSpec((1, LANES), const),
            pl.BlockSpec((1, LANES), const),
            pl.BlockSpec((LANES, LANES), const),
            pl.BlockSpec((D_MODEL, D_MODEL), const)],
        out_specs=[pl.BlockSpec((n_batch, blk, D_MODEL), lambda j, s: (0, j, 0)),
                   pl.BlockSpec((n_batch, blk, LANES), whole),
                   pl.BlockSpec((n_batch, blk, LANES), whole),
                   pl.BlockSpec((n_batch, SUBLANES, CONV_DIM), whole)],
        scratch_shapes=[pltpu.VMEM((n_batch, blk, LANES), F32), pltpu.VMEM((n_batch, blk, LANES), F32),
                        pltpu.VMEM((n_batch, SUBLANES, CONV_DIM), F32)])
    return pl.pallas_call(
        functools.partial(_ab_prompt_kernel, n_batch=n_batch),
        grid_spec=grid_spec,
        out_shape=[jax.ShapeDtypeStruct((n_batch, seq, D_MODEL), F32),
                   jax.ShapeDtypeStruct((n_batch, blk, LANES), F32),
                   jax.ShapeDtypeStruct((n_batch, blk, LANES), F32),
                   jax.ShapeDtypeStruct((n_batch, SUBLANES, CONV_DIM), F32)],
        compiler_params=_cparams(1),
        name="ab_prompt",
    )(sinks, *([z] * n_batch), *([x] * n_batch), cos, sin, cw, qg, kg, seg, wo)


def _ab_sample_kernel(sink_ref, z_ref, x_ref, cos_ref, sin_ref, cw_ref, qg_ref, kg_ref, seg_ref, wo_ref,
                      cc_ref, ck_ref, cv_ref, y_ref, cs_ref, ks_ref, vs_ref):
    rows = z_ref.shape[0]
    t_len = SUBLANES
    bt = rows // t_len
    cos = cos_ref[...]
    sin = sin_ref[...]
    seg = seg_ref[...]
    hi_half = (lax.broadcasted_iota(jnp.int32, (rows, LANES), 1) & 32) != 0

    gate_b = z_ref[:, 0:CONV_DIM]
    u = z_ref[:, CONV_DIM:2 * CONV_DIM] * z_ref[:, 2 * CONV_DIM:3 * CONV_DIM]
    u3 = u.reshape(bt, t_len, CONV_DIM)
    cc = cc_ref[...]
    c0 = cc[:, 0:1, :]
    c1 = cc[:, 1:2, :]
    t8 = lax.broadcasted_iota(jnp.int32, (bt, t_len, CONV_DIM), 1)
    r1 = pltpu.roll(u3, 1, 1)
    r2 = pltpu.roll(u3, 2, 1)
    um1 = jnp.where(t8 >= 1, r1, c1)
    um2 = jnp.where(t8 >= 2, r2, jnp.where(t8 == 1, c1, c0))
    cw = cw_ref[...]
    conv = cw[0:1][None] * um2 + cw[1:2][None] * um1 + cw[2:3][None] * u3
    yconv = gate_b * conv.reshape(rows, CONV_DIM)
    cs_ref[...] = r2[:, 0:2, :]

    q0 = 3 * CONV_DIM
    k0 = q0 + N_Q_HEADS * HEAD_DIM
    v0 = k0 + N_KV_HEADS * HEAD_DIM
    qg = qg_ref[...]
    qr = [_headnorm_rope(z_ref[:, q0 + c * LANES:q0 + (c + 1) * LANES], qg, cos, sin, seg, hi_half)
          for c in range(N_Q_HEADS * HEAD_DIM // LANES)]
    kr = _headnorm_rope(z_ref[:, k0:k0 + LANES], kg_ref[...], cos, sin, seg, hi_half)
    v = z_ref[:, v0:v0 + LANES]

    group = N_Q_HEADS // N_KV_HEADS
    nq = group * t_len
    nk = 2 * WINDOW
    qrow = lax.broadcasted_iota(jnp.int32, (nq, nk), 0)
    t_q = qrow & (t_len - 1)
    col = lax.broadcasted_iota(jnp.int32, (nq, nk), 1)
    mask = (((col < WINDOW) & (col > t_q)) | ((col >= WINDOW) & (col - WINDOW <= t_q)))[None]
    hrow = lax.broadcasted_iota(jnp.int32, (nq, 1), 0) // t_len
    pad = jnp.zeros((bt, nk - WINDOW - t_len, HEAD_DIM), F32)

    outs = [None] * N_Q_HEADS
    for g in range(N_KV_HEADS):
        sl = slice(g * HEAD_DIM, (g + 1) * HEAD_DIM)
        qs = jnp.concatenate(
            [qr[h // 2][:, (h % 2) * HEAD_DIM:(h % 2 + 1) * HEAD_DIM].reshape(bt, t_len, HEAD_DIM)
             for h in range(g * group, (g + 1) * group)], axis=1)
        kk = jnp.concatenate([ck_ref[:, :, sl], kr[:, sl].reshape(bt, t_len, HEAD_DIM), pad], axis=1)
        vv = jnp.concatenate([cv_ref[:, :, sl], v[:, sl].reshape(bt, t_len, HEAD_DIM), pad], axis=1)
        s = jnp.einsum('bqd,bkd->bqk', qs.astype(BF16), kk.astype(BF16),
                       preferred_element_type=F32) * (HEAD_DIM ** -0.5)
        sink = jnp.zeros((nq, 1), F32)
        for hh in range(group):
            sink = jnp.where(hrow == hh, sink_ref[g * group + hh], sink)
        p = _softmax_sink(s, mask, sink[None])
        o = jnp.einsum('bqk,bkd->bqd', p, vv.astype(BF16), preferred_element_type=F32)
        for hh in range(group):
            outs[g * group + hh] = o[:, hh * t_len:(hh + 1) * t_len, :].reshape(rows, HEAD_DIM)
    attn = jnp.concatenate(outs, axis=1)

    y = (jnp.dot(yconv.astype(BF16), wo_ref[0:CONV_DIM, :], preferred_element_type=F32)
         + jnp.dot(attn.astype(BF16), wo_ref[CONV_DIM:2 * CONV_DIM, :], preferred_element_type=F32))
    y_ref[...] = y + x_ref[...]

    keep = WINDOW - t_len
    ks_ref[:, 0:keep, :] = ck_ref[:, t_len:WINDOW, :]
    ks_ref[:, keep:WINDOW, :] = kr.reshape(bt, t_len, LANES)
    vs_ref[:, 0:keep, :] = cv_ref[:, t_len:WINDOW, :]
    vs_ref[:, keep:WINDOW, :] = v.reshape(bt, t_len, LANES)


def _ab_sample(z, x, row0, n_seq, t_len, cos, sin, cw, qg, kg, sinks, seg, wo, cc, ck, cv, bt=16):
    rows = bt * t_len
    blk0 = row0 // rows
    tok = lambda i, s: (blk0 + i, 0)
    const = lambda i, s: (0, 0)
    seq3 = lambda i, s: (i, 0, 0)
    grid_spec = pltpu.PrefetchScalarGridSpec(
        num_scalar_prefetch=1,
        grid=(n_seq // bt,),
        in_specs=[pl.BlockSpec((rows, AB_IN), tok),
                  pl.BlockSpec((rows, D_MODEL), tok),
                  pl.BlockSpec((rows, LANES), const),
                  pl.BlockSpec((rows, LANES), const),
                  pl.BlockSpec((3, CONV_DIM), const),
                  pl.BlockSpec((1, LANES), const),
                  pl.BlockSpec((1, LANES), const),
                  pl.BlockSpec((LANES, LANES), const),
                  pl.BlockSpec((D_MODEL, D_MODEL), const),
                  pl.BlockSpec((bt, 2, CONV_DIM), seq3),
                  pl.BlockSpec((bt, WINDOW, LANES), seq3),
                  pl.BlockSpec((bt, WINDOW, LANES), seq3)],
        out_specs=[pl.BlockSpec((rows, D_MODEL), lambda i, s: (i, 0)),
                   pl.BlockSpec((bt, 2, CONV_DIM), seq3),
                   pl.BlockSpec((bt, WINDOW, LANES), seq3),
                   pl.BlockSpec((bt, WINDOW, LANES), seq3)])
    return pl.pallas_call(
        _ab_sample_kernel,
        grid_spec=grid_spec,
        out_shape=[jax.ShapeDtypeStruct((n_seq * t_len, D_MODEL), F32),
                   jax.ShapeDtypeStruct((n_seq, 2, CONV_DIM), F32),
                   jax.ShapeDtypeStruct((n_seq, WINDOW, LANES), F32),
                   jax.ShapeDtypeStruct((n_seq, WINDOW, LANES), F32)],
        compiler_params=_cparams(1),
        name="ab_sample",
    )(sinks, z, x, cos, sin, cw, qg, kg, seg, wo, cc, ck, cv)


def _log_sigmoid(x):
    return jnp.minimum(x, 0.0) - jnp.log(1.0 + jnp.exp(-jnp.abs(x)))


def _mlstm_chunk(z, x, bias, og, tril, wo, c_src, n_src, m_src, c_dst, n_dst, m_dst, n_real):
    L = z.shape[0]
    gates = z[:, ML_GATE_COL:ML_GATE_COL + LANES] + bias
    if n_real < L:
        live = lax.broadcasted_iota(jnp.int32, (L, LANES), 0) < n_real
        li_all = jnp.where(live, gates, -1e30)
        lf_all = jnp.where(live, _log_sigmoid(gates), 0.0)
    else:
        li_all = gates
        lf_all = _log_sigmoid(gates)
    lf_pieces = _bf16_pieces(lf_all, 3)
    f_col_all = sum(jnp.dot(tril, p, preferred_element_type=F32) for p in lf_pieces)
    f_row_all = sum(lax.dot_general(p, tril, (((0,), (1,)), ((), ())), preferred_element_type=F32)
                    for p in lf_pieces)
    li_t = li_all.T
    rr = lax.broadcasted_iota(jnp.int32, (L, L), 0)
    cc = lax.broadcasted_iota(jnp.int32, (L, L), 1)
    causal = cc <= rr

    outs, m_new_all = [], []
    for h in range(ML_HEADS):
        f_col = f_col_all[:, ML_HEADS + h:ML_HEADS + h + 1]
        f_row = f_row_all[ML_HEADS + h:ML_HEADS + h + 1, :]
        li_row = li_t[h:h + 1, :]
        li_col = li_all[:, h:h + 1]
        m0 = m_src[0:1, h:h + 1]
        c0 = c_src[h]
        n0 = n_src[h:h + 1, :]
        qh = z[:, h * ML_QK:(h + 1) * ML_QK]
        kh = z[:, ML_HEADS * ML_QK + h * ML_QK:ML_HEADS * ML_QK + (h + 1) * ML_QK] * (ML_QK ** -0.5)
        v_off = 2 * ML_HEADS * ML_QK
        vh = z[:, v_off + h * ML_V:v_off + (h + 1) * ML_V]
        o_off = v_off + ML_HEADS * ML_V
        oh = z[:, o_off + h * ML_V:o_off + (h + 1) * ML_V]
        qb = qh.astype(BF16)
        vb = vh.astype(BF16)

        dmat = jnp.where(causal, f_col - f_row + li_row, -jnp.inf)
        gcar = f_col + m0
        m_t = jnp.maximum(jnp.max(dmat, axis=-1, keepdims=True), gcar)
        w = jnp.exp(dmat - m_t)
        s = lax.dot_general(qb, kh.astype(BF16), NT_DIMS, preferred_element_type=F32) * w
        carry = jnp.exp(gcar - m_t)
        num = (jnp.dot(s.astype(BF16), vb, preferred_element_type=F32)
               + jnp.dot(qb, c0.astype(BF16), preferred_element_type=F32) * carry)
        den = jnp.sum(s, axis=-1, keepdims=True) + carry * jnp.sum(qh * n0, axis=-1, keepdims=True)
        hout = num / jnp.maximum(jnp.abs(den), jnp.exp(-m_t))

        f_last = f_col[L - 1:L, :]
        w_end = f_last - f_col + li_col
        m_new = jnp.maximum(f_last + m0, jnp.max(w_end, axis=0, keepdims=True))
        a_end = jnp.exp(w_end - m_new)
        scale = jnp.exp(f_last + m0 - m_new)
        ka = kh * a_end
        c_dst[h] = scale * c0 + lax.dot_general(ka.astype(BF16), vb, TN_DIMS, preferred_element_type=F32)
        n_dst[h:h + 1, :] = scale * n0 + jnp.sum(ka, axis=0, keepdims=True)
        m_new_all.append(m_new)

        hn = _rmsnorm(hout, og[:, h * ML_V:(h + 1) * ML_V])
        outs.append(jax.nn.sigmoid(oh) * hn)
    m_dst[...] = jnp.concatenate(m_new_all, axis=1)
    out = jnp.concatenate(outs, axis=1)
    return jnp.dot(out.astype(BF16), wo, preferred_element_type=F32) + x


def _mlstm_prompt_kernel(*refs, n_batch):
    z_refs, x_refs = refs[0:n_batch], refs[n_batch:2 * n_batch]
    bias_ref, og_ref, tril_ref, wo_ref, y_ref, c_ref, n_ref, m_ref = refs[2 * n_batch:]

    @pl.when(pl.program_id(0) == 0)
    def _():
        c_ref[...] = jnp.zeros_like(c_ref)
        n_ref[...] = jnp.zeros_like(n_ref)
        m_ref[...] = jnp.zeros_like(m_ref)

    for b in range(n_batch):
        state = (c_ref.at[b], n_ref.at[b], m_ref.at[b])
        y_ref[b] = _mlstm_chunk(z_refs[b][...], x_refs[b][...], bias_ref[...], og_ref[...], tril_ref[...],
                                wo_ref[...], *state, *state, ML_CHUNK)


def _mlstm_sample_kernel(z_ref, x_ref, bias_ref, og_ref, tril_ref, wo_ref, c0_ref, n0_ref, m0_ref,
                         y_ref, c_ref, n_ref, m_ref, *, t_len):
    n_here = z_ref.shape[0] // t_len
    for s in range(n_here):
        rows = slice(s * t_len, (s + 1) * t_len)
        zpad = jnp.concatenate([z_ref[rows, :], jnp.zeros((ML_CHUNK - t_len, ML_IN_PAD), F32)], axis=0)
        xpad = jnp.concatenate([x_ref[rows, :], jnp.zeros((ML_CHUNK - t_len, D_MODEL), F32)], axis=0)
        y = _mlstm_chunk(zpad, xpad, bias_ref[...], og_ref[...], tril_ref[...], wo_ref[...],
                         c0_ref.at[s], n0_ref.at[s], m0_ref.at[s], c_ref.at[s], n_ref.at[s], m_ref.at[s], t_len)
        y_ref[rows, :] = y[0:t_len, :]


def _mlstm_weight_specs(const):
    return [pl.BlockSpec((1, LANES), const),
            pl.BlockSpec((1, D_MODEL), const),
            pl.BlockSpec((ML_CHUNK, ML_CHUNK), const),
            pl.BlockSpec((D_MODEL, D_MODEL), const)]


def _mlstm_prompt(z, x, n_batch, seq, bias, og, tril, wo):
    nc = seq // ML_CHUNK
    const = lambda j: (0, 0)
    tok_specs = lambda width: [pl.BlockSpec((ML_CHUNK, width), functools.partial(lambda j, b: (b * nc + j, 0), b=b))
                               for b in range(n_batch)]
    return pl.pallas_call(
        functools.partial(_mlstm_prompt_kernel, n_batch=n_batch),
        grid=(nc,),
        in_specs=tok_specs(ML_IN_PAD) + tok_specs(D_MODEL) + _mlstm_weight_specs(const),
        out_specs=[pl.BlockSpec((n_batch, ML_CHUNK, D_MODEL), lambda j: (0, j, 0)),
                   pl.BlockSpec((n_batch, ML_HEADS, ML_QK, ML_V), lambda j: (0, 0, 0, 0)),
                   pl.BlockSpec((n_batch, ML_HEADS, ML_QK), lambda j: (0, 0, 0)),
                   pl.BlockSpec((n_batch, 1, ML_HEADS), lambda j: (0, 0, 0))],
        out_shape=[jax.ShapeDtypeStruct((n_batch, seq, D_MODEL), F32),
                   jax.ShapeDtypeStruct((n_batch, ML_HEADS, ML_QK, ML_V), F32),
                   jax.ShapeDtypeStruct((n_batch, ML_HEADS, ML_QK), F32),
                   jax.ShapeDtypeStruct((n_batch, 1, ML_HEADS), F32)],
        compiler_params=_cparams(1),
        name="mlstm_prompt",
    )(*([z] * n_batch), *([x] * n_batch), bias, og, tril, wo)


def _mlstm_sample(z, x, row0, n_seq, t_len, bias, og, tril, wo, c0, n0, m0, seqs_per_step=4):
    rows = seqs_per_step * t_len
    blk0 = row0 // rows
    tok = lambda i: (blk0 + i, 0)
    const = lambda i: (0, 0)
    st4 = lambda i: (i, 0, 0, 0)
    st3 = lambda i: (i, 0, 0)
    state_specs = [pl.BlockSpec((seqs_per_step, ML_HEADS, ML_QK, ML_V), st4),
                   pl.BlockSpec((seqs_per_step, ML_HEADS, ML_QK), st3),
                   pl.BlockSpec((seqs_per_step, 1, ML_HEADS), st3)]
    return pl.pallas_call(
        functools.partial(_mlstm_sample_kernel, t_len=t_len),
        grid=(n_seq // seqs_per_step,),
        in_specs=[pl.BlockSpec((rows, ML_IN_PAD), tok),
                  pl.BlockSpec((rows, D_MODEL), tok)] + _mlstm_weight_specs(const) + state_specs,
        out_specs=[pl.BlockSpec((rows, D_MODEL), lambda i: (i, 0))] + state_specs,
        out_shape=[jax.ShapeDtypeStruct((n_seq * t_len, D_MODEL), F32),
                   jax.ShapeDtypeStruct((n_seq, ML_HEADS, ML_QK, ML_V), F32),
                   jax.ShapeDtypeStruct((n_seq, ML_HEADS, ML_QK), F32),
                   jax.ShapeDtypeStruct((n_seq, 1, ML_HEADS), F32)],
        compiler_params=_cparams(1),
        name="mlstm_sample",
    )(z, x, bias, og, tril, wo, c0, n0, m0)


def _gelu(x):
    return 0.5 * x * (1.0 + lax.erf(x * 0.7071067811865476))


SEL_T = 256


def _unit_scores(q_scr, sk_ref, half, h):
    return [lax.dot_general(sk_ref[h, p], q_scr[half, 2 * h + p].astype(BF16), NT_DIMS,
                            preferred_element_type=F32) for p in range(2)]


class _TopK:
    def __init__(self, s, payload=None):
        self.s = s
        self.payload = payload
        self.rows = lax.broadcasted_iota(jnp.int32, s.shape, 0).astype(F32)
        self.vals, self.picks = [], []

    def step(self, n):
        bound = float(self.s.shape[0])
        for _ in range(n):
            m = jnp.max(self.s, axis=0, keepdims=True)
            first = jnp.min(jnp.where(self.s == m, self.rows, bound), axis=0, keepdims=True)
            sel = self.rows == first
            self.vals.append(m)
            if self.payload is None:
                self.picks.append(first)
            else:
                self.picks.append(jnp.max(jnp.where(sel, self.payload, -1.0), axis=0, keepdims=True))
            self.s = jnp.where(sel, -jnp.inf, self.s)

    def result(self):
        return jnp.concatenate(self.vals, axis=0), jnp.concatenate(self.picks, axis=0)


def _pair_candidates(first, second):
    def pairs(a, b, combine):
        h8 = SUBLANES
        rows = [combine(a[0:1], b)]
        rows += [combine(a[k1:k1 + 1], b[0:h8]) for k1 in range(1, h8)]
        rows.append(combine(a[h8:PEER_TOPK], b[0:1]))
        return jnp.concatenate(rows, axis=0)

    (v1, i1), (v2, i2) = first, second
    return pairs(v1, v2, lambda a, b: a + b), pairs(i1, i2, lambda a, b: a * N_KEYS + b)


def _peer_fused_kernel(xs_ref, xm_ref, g_ref, wq_ref, sk_ref, u_ref, v_ref, o_ref,
                       q_scr, sel_e, sel_g, xn_scr, a_scr, b_scr, g_scr, gmat, coef_scr):
    i = pl.program_id(0)
    e = pl.program_id(1)
    n_steps = pl.num_programs(1)
    t = xm_ref.shape[0]
    n_half = t // SEL_T
    half_keys = N_KEYS // 2
    cur = i % 2

    def project_queries():
        xn = _rmsnorm(xs_ref[...], g_ref[...]).astype(BF16)
        q = jnp.dot(xn, wq_ref[...], preferred_element_type=F32)
        for hf in range(n_half):
            for c in range(2 * PEER_HEADS):
                q_scr[hf, c] = q[hf * SEL_T:(hf + 1) * SEL_T, c * LANES:(c + 1) * LANES]

    @pl.when(e == 0)
    def _():
        @pl.when(i == 0)
        def _():
            project_queries()
            sel_e[...] = jnp.zeros_like(sel_e)
            sel_g[...] = jnp.zeros_like(sel_g)

        x = xm_ref[...]
        xn_scr[...] = _rmsnorm(x, g_ref[...]).astype(BF16)
        o_ref[...] = x
        coef_scr[0] = jnp.zeros(coef_scr.shape[1:], BF16)
        prev = 1 - cur
        for hf in range(n_half):
            ef = sel_e[prev, hf]
            af = jnp.floor(ef * (1.0 / N_KEYS))
            rows = slice(hf * SEL_T, (hf + 1) * SEL_T)
            a_scr[rows, :] = af.T
            b_scr[rows, :] = (ef - af * N_KEYS).T
            g_scr[rows, :] = sel_g[prev, hf].T
        r = lax.broadcasted_iota(jnp.int32, (N_KEYS, LANES), 0)
        packed_shape = (N_KEYS // (2 * SUBLANES), 2 * SUBLANES, LANES)
        a_of_row = jnp.where(r < half_keys, 2 * r, 2 * (r - half_keys) + 1).astype(F32).astype(BF16)
        a_of_row = a_of_row.reshape(packed_shape)
        b_of_row = r.astype(F32).astype(BF16).reshape(packed_shape)
        one = jnp.ones(packed_shape, BF16)
        zero = jnp.zeros(packed_shape, BF16)

        def body(n8, carry):
            base = pl.multiple_of(n8 * SUBLANES, SUBLANES)
            a8 = a_scr[pl.ds(base, SUBLANES), :]
            b8 = b_scr[pl.ds(base, SUBLANES), :]
            g8 = g_scr[pl.ds(base, SUBLANES), :]
            for k in range(SUBLANES):
                arow = jnp.broadcast_to(a8[k:k + 1], (2 * SUBLANES, LANES)).astype(BF16)[None]
                brow = jnp.broadcast_to(b8[k:k + 1], (2 * SUBLANES, LANES)).astype(BF16)[None]
                grow = jnp.broadcast_to(g8[k:k + 1], (2 * SUBLANES, LANES)).astype(BF16)[None]
                pa = jnp.where(a_of_row == arow, one, zero).reshape(N_KEYS, LANES)
                qb = jnp.where(b_of_row == brow, grow, zero).reshape(N_KEYS, LANES)
                tile = lax.dot_general(pa, qb, NT_DIMS, preferred_element_type=F32)
                row0 = pl.multiple_of((base + k) * G_PITCH, SUBLANES)
                gmat[pl.ds(row0, half_keys), :] = pltpu.pack_elementwise(
                    [tile[0:half_keys], tile[half_keys:N_KEYS]], packed_dtype=BF16)
            return carry

        lax.fori_loop(0, t // SUBLANES, body, 0)

    unit = jnp.minimum(e, n_half * PEER_HEADS - 1)
    half = lax.shift_right_logical(unit, PEER_HEADS.bit_length() - 1)
    h = unit & (PEER_HEADS - 1)
    first_level = [_TopK(st) for st in _unit_scores(q_scr, sk_ref, half, h)]

    rd = e & 1
    o_ref[...] += jnp.dot(coef_scr[rd], v_ref[...], preferred_element_type=F32)
    act = lax.dot_general(xn_scr[...], u_ref[...], NT_DIMS, preferred_element_type=F32)
    blk = jnp.minimum(e, n_steps - 2)
    words_per_step = u_ref.shape[0] // (2 * N_KEYS)
    gates = []
    for w in range(words_per_step):
        word = gmat[pl.ds(blk * words_per_step + w, t, stride=G_PITCH), :]
        gates.append(lax.bitcast_convert_type(word << 16, F32))
        gates.append(lax.bitcast_convert_type(word & jnp.int32(-65536), F32))
    coef_scr[1 - rd] = (_gelu(act) * jnp.concatenate(gates, axis=1)).astype(BF16)

    for lst in first_level:
        lst.step(PEER_TOPK)
    cand, expert = _pair_candidates(first_level[0].result(), first_level[1].result())
    second_level = _TopK(cand, payload=expert)
    second_level.step(PEER_TOPK)
    best, e_sel = second_level.result()
    ex = jnp.exp(best - best[0:1])
    slot0 = pl.multiple_of(h * PEER_TOPK, PEER_TOPK)
    sel_e[cur, half, pl.ds(slot0, PEER_TOPK), :] = e_sel
    sel_g[cur, half, pl.ds(slot0, PEER_TOPK), :] = ex / jnp.sum(ex, axis=0, keepdims=True)

    @pl.when(e == n_steps - 1)
    def _():
        project_queries()


def _peer(x, g, wq, sk, u, v, t=512, eb=1024):
    n, d = x.shape
    nslot = PEER_HEADS * PEER_TOPK
    n_tok_blk = n // t
    n_exp_blk = N_KEYS * N_KEYS // eb
    n_half = t // SEL_T
    assert n_exp_blk == n_half * PEER_HEADS
    last_tok = n_tok_blk - 1
    once = pl.Buffered(1)
    return pl.pallas_call(
        _peer_fused_kernel,
        grid=(n_tok_blk + 1, n_exp_blk + 1),
        in_specs=[pl.BlockSpec((t, d), lambda i, e: (jnp.minimum(i + e // n_exp_blk, last_tok), 0),
                               pipeline_mode=once),
                  pl.BlockSpec((t, d), lambda i, e: (jnp.maximum(i - 1, 0), 0)),
                  pl.BlockSpec((1, d), lambda i, e: (0, 0)),
                  pl.BlockSpec((d, 2 * PEER_HEADS * LANES), lambda i, e: (0, 0), pipeline_mode=once),
                  pl.BlockSpec((PEER_HEADS, 2, N_KEYS, LANES), lambda i, e: (0, 0, 0, 0), pipeline_mode=once),
                  pl.BlockSpec((eb, d), lambda i, e: (jnp.minimum(e, n_exp_blk - 1), 0)),
                  pl.BlockSpec((eb, d), lambda i, e: (jnp.maximum(e - 1, 0), 0))],
        out_specs=pl.BlockSpec((t, d), lambda i, e: (jnp.maximum(i - 1, 0), 0)),
        out_shape=jax.ShapeDtypeStruct((n, d), F32),
        scratch_shapes=[pltpu.VMEM((n_half, 2 * PEER_HEADS, SEL_T, LANES), F32),
                        pltpu.VMEM((2, n_half, nslot, SEL_T), F32),
                        pltpu.VMEM((2, n_half, nslot, SEL_T), F32),
                        pltpu.VMEM((t, d), BF16),
                        pltpu.VMEM((t, nslot), F32), pltpu.VMEM((t, nslot), F32), pltpu.VMEM((t, nslot), F32),
                        pltpu.VMEM((t * G_PITCH, LANES), jnp.int32),
                        pltpu.VMEM((2, t, eb), BF16)],
        compiler_params=_cparams(2),
        name="peer",
    )(x, x, g.reshape(1, d), wq, sk, u, v)


def _rope_tables(pos):
    half = HEAD_DIM // 2
    inv_freq = ROPE_THETA ** (-jnp.arange(half, dtype=F32) / half)
    ang = pos.astype(F32)[:, None] * inv_freq[None, :]
    cos, sin = jnp.cos(ang), jnp.sin(ang)
    reps = LANES // HEAD_DIM
    cos_t = jnp.tile(jnp.concatenate([cos, cos], axis=1), (1, reps))
    sin_t = jnp.tile(jnp.concatenate([-sin, sin], axis=1), (1, reps))
    return cos_t, sin_t


def kernel(x_prompt, x_sample, cache_conv, cache_win_k, cache_win_v, state_mlstm_C, state_mlstm_n,
           state_mlstm_m, norm_mix, norm_ffn, ab_w_in, ab_conv_w, ab_q_gain, ab_k_gain, ab_sinks, ab_w_out,
           ml_w_in, ml_gate_bias, ml_out_gain, ml_w_out, peer_w_q, peer_sub_keys, peer_u, peer_v):
    n_batch, seq, d = x_prompt.shape
    n_seq, t_len, _ = x_sample.shape
    n_prompt = n_batch * seq
    assert d == D_MODEL and t_len == SUBLANES and norm_mix.shape[0] == 2

    x = jnp.concatenate([x_prompt.reshape(n_prompt, d), x_sample.reshape(n_seq * t_len, d)], axis=0)

    cos_p, sin_p = _rope_tables(jnp.arange(seq, dtype=jnp.int32))
    cos_s, sin_s = _rope_tables(PAST_LEN + jnp.arange(t_len, dtype=jnp.int32))
    bt = 16
    cos_s, sin_s = jnp.tile(cos_s, (bt, 1)), jnp.tile(sin_s, (bt, 1))
    lane = jnp.arange(LANES)
    seg = (lane[:, None] // HEAD_DIM == lane[None, :] // HEAD_DIM).astype(BF16)
    reps = LANES // HEAD_DIM
    qg = jnp.tile(ab_q_gain[0], reps).reshape(1, LANES)
    kg = jnp.tile(ab_k_gain[0], reps).reshape(1, LANES)
    wo_ab = ab_w_out[0].astype(BF16)

    z = _norm_proj(x, norm_mix[0], ab_w_in[0].astype(BF16))
    y_p, k_p, v_p, c_p = _ab_prompt(z, x, n_batch, seq, cos_p, sin_p, ab_conv_w[0], qg, kg, ab_sinks[0],
                                    seg, wo_ab)
    y_s, c_s, k_s, v_s = _ab_sample(z, x, n_prompt, n_seq, t_len, cos_s, sin_s, ab_conv_w[0], qg, kg,
                                    ab_sinks[0], seg, wo_ab, cache_conv[0],
                                    cache_win_k[0].reshape(n_seq, WINDOW, LANES),
                                    cache_win_v[0].reshape(n_seq, WINDOW, LANES), bt=bt)
    x = jnp.concatenate([y_p.reshape(n_prompt, d), y_s], axis=0)
    x = _peer(x, norm_ffn[0], peer_w_q[0].astype(BF16), peer_sub_keys[0].astype(BF16),
              peer_u[0].astype(BF16), peer_v[0].astype(BF16))

    n_gate = 2 * ML_HEADS
    w_in = jnp.pad(ml_w_in[0], ((0, 0), (0, ML_IN_PAD - ml_w_in.shape[2]))).astype(BF16)
    bias = jnp.pad(ml_gate_bias[0], (0, LANES - n_gate)).reshape(1, LANES)
    og = ml_out_gain[0].reshape(1, D_MODEL)
    idx = jnp.arange(ML_CHUNK)
    tril = (idx[None, :] <= idx[:, None]).astype(BF16)
    wo_ml = ml_w_out[0].astype(BF16)

    z = _norm_proj(x, norm_mix[1], w_in)
    y_p, cm_p, nm_p, mm_p = _mlstm_prompt(z, x, n_batch, seq, bias, og, tril, wo_ml)
    y_s, cm_s, nm_s, mm_s = _mlstm_sample(z, x, n_prompt, n_seq, t_len, bias, og, tril, wo_ml,
                                          state_mlstm_C[0], state_mlstm_n[0],
                                          state_mlstm_m[0].reshape(n_seq, 1, ML_HEADS))
    x = jnp.concatenate([y_p.reshape(n_prompt, d), y_s], axis=0)
    x = _peer(x, norm_ffn[1], peer_w_q[1].astype(BF16), peer_sub_keys[1].astype(BF16),
              peer_u[1].astype(BF16), peer_v[1].astype(BF16))

    y_prompt = x[:n_prompt].reshape(n_batch, seq, d)
    y_sample = x[n_prompt:].reshape(n_seq, t_len, d)
    kv_shape_p = (1, n_batch, WINDOW, N_KV_HEADS, HEAD_DIM)
    kv_shape_s = (1, n_seq, WINDOW, N_KV_HEADS, HEAD_DIM)
    return (y_prompt, y_sample,
            c_p[:, SUBLANES - 2:, :][None], k_p.reshape(kv_shape_p), v_p.reshape(kv_shape_p),
            cm_p[None], nm_p[None], mm_p.reshape(1, n_batch, ML_HEADS),
            c_s[None], k_s.reshape(kv_shape_s), v_s.reshape(kv_shape_s),
            cm_s[None], nm_s[None], mm_s.reshape(1, n_seq, ML_HEADS))
```

```python
import functools

import jax
import jax.numpy as jnp
from jax import lax
from jax.experimental import pallas as pl
from jax.experimental.pallas import tpu as pltpu

F32 = jnp.float32
BF16 = jnp.bfloat16
EPS = 1e-6

D_MODEL = 1024
CONV_DIM = 512
N_Q_HEADS = 8
N_KV_HEADS = 2
HEAD_DIM = 64
WINDOW = 128
ROPE_THETA = 10000.0
AB_IN = 2304
ML_HEADS = 4
ML_QK = 128
ML_V = 256
ML_CHUNK = 128
ML_GATE_COL = 3072
ML_IN_PAD = ML_GATE_COL + 128
N_KEYS = 128
PEER_HEADS = 8
PEER_TOPK = 16
PAST_LEN = 16384

LANES = 128
SUBLANES = 8
G_PITCH = N_KEYS // 2 + SUBLANES
VMEM_LIMIT = 56 * 1024 * 1024

NT_DIMS = (((1,), (1,)), ((), ()))
TN_DIMS = (((0,), (0,)), ((), ()))


def _cparams(n_axes, vmem=VMEM_LIMIT):
    return pltpu.CompilerParams(dimension_semantics=("arbitrary",) * n_axes, vmem_limit_bytes=vmem)


def _rmsnorm(x, g):
    return x * lax.rsqrt(jnp.mean(x * x, axis=-1, keepdims=True) + EPS) * g


def _bf16_pieces(a, terms):
    pieces = []
    rem = a
    for _ in range(terms):
        piece = rem.astype(BF16)
        rem = rem - piece.astype(F32)
        pieces.append(piece)
    return pieces


def _split_dot(a, b_bf16, terms=2):
    return sum(jnp.dot(p, b_bf16, preferred_element_type=F32) for p in _bf16_pieces(a, terms))


def _norm_proj_kernel(x_ref, g_ref, w_ref, o_ref):
    r = _rmsnorm(x_ref[...], g_ref[...])
    o_ref[...] = jnp.dot(r.astype(BF16), w_ref[...], preferred_element_type=F32)


def _norm_proj(x, g, w_bf16, tm=512):
    n, d = x.shape
    nout = w_bf16.shape[1]
    return pl.pallas_call(
        _norm_proj_kernel,
        grid=(n // tm,),
        in_specs=[pl.BlockSpec((tm, d), lambda i: (i, 0)),
                  pl.BlockSpec((1, d), lambda i: (0, 0)),
                  pl.BlockSpec((d, nout), lambda i: (0, 0))],
        out_specs=pl.BlockSpec((tm, nout), lambda i: (i, 0)),
        out_shape=jax.ShapeDtypeStruct((n, nout), F32),
        compiler_params=_cparams(1),
        name="norm_proj",
    )(x, g.reshape(1, d), w_bf16)


def _headnorm_rope(xc, gain, cos, sin, seg, hi_half):
    ss = _split_dot(xc * xc, seg)
    xn = xc * lax.rsqrt(ss * (1.0 / HEAD_DIM) + EPS) * gain
    partner = jnp.where(hi_half, pltpu.roll(xn, 32, 1), pltpu.roll(xn, 96, 1))
    return xn * cos + partner * sin


def _softmax_sink(s, mask, sink):
    s = jnp.where(mask, s, -1e30)
    m = jnp.maximum(jnp.max(s, axis=-1, keepdims=True), sink)
    p = jnp.exp(s - m)
    denom = jnp.sum(p, axis=-1, keepdims=True) + jnp.exp(sink - m)
    return (p / denom).astype(BF16)


def _ab_prompt_kernel(sink_ref, *refs, n_batch):
    z_refs, x_refs = refs[0:n_batch], refs[n_batch:2 * n_batch]
    (cos_ref, sin_ref, cw_ref, qg_ref, kg_ref, seg_ref, wo_ref,
     y_ref, kst_ref, vst_ref, cst_ref, pk_ref, pv_ref, pu_ref) = refs[2 * n_batch:]
    j = pl.program_id(0)

    @pl.when(j == 0)
    def _():
        pk_ref[...] = jnp.zeros_like(pk_ref)
        pv_ref[...] = jnp.zeros_like(pv_ref)
        pu_ref[...] = jnp.zeros_like(pu_ref)

    for b in range(n_batch):
        _ab_prompt_block(j, sink_ref, z_refs[b], x_refs[b], cos_ref, sin_ref, cw_ref, qg_ref, kg_ref, seg_ref,
                         wo_ref, y_ref.at[b], kst_ref.at[b], vst_ref.at[b], cst_ref.at[b],
                         pk_ref.at[b], pv_ref.at[b], pu_ref.at[b])


def _ab_prompt_block(j, sink_ref, z_ref, x_ref, cos_ref, sin_ref, cw_ref, qg_ref, kg_ref, seg_ref, wo_ref,
                     y_ref, kst_ref, vst_ref, cst_ref, pk_ref, pv_ref, pu_ref):
    blk = z_ref.shape[0]
    cos = cos_ref[...]
    sin = sin_ref[...]
    seg = seg_ref[...]
    hi_half = (lax.broadcasted_iota(jnp.int32, (blk, LANES), 1) & 32) != 0

    gate_b = z_ref[:, 0:CONV_DIM]
    u = z_ref[:, CONV_DIM:2 * CONV_DIM] * z_ref[:, 2 * CONV_DIM:3 * CONV_DIM]
    ng = blk // SUBLANES
    u3 = u.reshape(ng, SUBLANES, CONV_DIM)
    ext = jnp.concatenate([pu_ref[...][None], u3], axis=0)
    t8 = lax.broadcasted_iota(jnp.int32, (ng, SUBLANES, CONV_DIM), 1)
    r1 = pltpu.roll(ext, 1, 1)
    r2 = pltpu.roll(ext, 2, 1)
    um1 = jnp.where(t8 >= 1, r1[1:], r1[:-1])
    um2 = jnp.where(t8 >= 2, r2[1:], r2[:-1])
    cw = cw_ref[...]
    conv = cw[0:1][None] * um2 + cw[1:2][None] * um1 + cw[2:3][None] * u3
    yconv = gate_b * conv.reshape(blk, CONV_DIM)

    q0 = 3 * CONV_DIM
    k0 = q0 + N_Q_HEADS * HEAD_DIM
    v0 = k0 + N_KV_HEADS * HEAD_DIM
    qg = qg_ref[...]
    qr = [_headnorm_rope(z_ref[:, q0 + c * LANES:q0 + (c + 1) * LANES], qg, cos, sin, seg, hi_half)
          for c in range(N_Q_HEADS * HEAD_DIM // LANES)]
    kr = _headnorm_rope(z_ref[:, k0:k0 + LANES], kg_ref[...], cos, sin, seg, hi_half)
    v = z_ref[:, v0:v0 + LANES]
    pk = pk_ref[...]
    pv = pv_ref[...]

    row = lax.broadcasted_iota(jnp.int32, (blk, 2 * blk), 0)
    col = lax.broadcasted_iota(jnp.int32, (blk, 2 * blk), 1)
    row_prev = row + jnp.where(j == 0, 2 * blk, 0)
    mask = ((col < blk) & (col > row_prev)) | ((col >= blk) & (col - blk <= row))

    kks, vvs = [], []
    for g in range(N_KV_HEADS):
        sl = slice(g * HEAD_DIM, (g + 1) * HEAD_DIM)
        kks.append(jnp.concatenate([pk[:, sl], kr[:, sl]], axis=0).astype(BF16))
        vvs.append(jnp.concatenate([pv[:, sl], v[:, sl]], axis=0).astype(BF16))
    outs = []
    for h in range(N_Q_HEADS):
        g = h // (N_Q_HEADS // N_KV_HEADS)
        qh = qr[h // 2][:, (h % 2) * HEAD_DIM:(h % 2 + 1) * HEAD_DIM].astype(BF16)
        s = lax.dot_general(qh, kks[g], NT_DIMS, preferred_element_type=F32) * (HEAD_DIM ** -0.5)
        p = _softmax_sink(s, mask, sink_ref[h])
        outs.append(jnp.dot(p, vvs[g], preferred_element_type=F32))
    attn = jnp.concatenate(outs, axis=1)

    y = (jnp.dot(yconv.astype(BF16), wo_ref[0:CONV_DIM, :], preferred_element_type=F32)
         + jnp.dot(attn.astype(BF16), wo_ref[CONV_DIM:2 * CONV_DIM, :], preferred_element_type=F32))
    y_ref[...] = y + x_ref[...]

    pk_ref[...] = kr
    pv_ref[...] = v
    pu_ref[...] = u3[ng - 1]
    kst_ref[...] = kr
    vst_ref[...] = v
    cst_ref[...] = u3[ng - 1]


def _ab_prompt(z, x, n_batch, seq, cos, sin, cw, qg, kg, sinks, seg, wo):
    blk = WINDOW
    nb = seq // blk
    const = lambda j, s: (0, 0)
    whole = lambda j, s: (0, 0, 0)
    tok_specs = lambda width: [pl.BlockSpec((blk, width), functools.partial(lambda j, s, b: (b * nb + j, 0), b=b))
                               for b in range(n_batch)]
    grid_spec = pltpu.PrefetchScalarGridSpec(
        num_scalar_prefetch=1,
        grid=(nb,),
        in_specs=tok_specs(AB_IN) + tok_specs(D_MODEL) + [
            pl.BlockSpec((blk, LANES), lambda j, s: (j, 0)),
            pl.BlockSpec((blk, LANES), lambda j, s: (j, 0)),
            pl.BlockSpec((3, CONV_DIM), const),
            pl.BlockSpec((1, LANES), const),
            pl.BlockSpec((1, LANES), const),
            pl.BlockSpec((LANES, LANES), const),
            pl.BlockSpec((D_MODEL, D_MODEL), const)],
        out_specs=[pl.BlockSpec((n_batch, blk, D_MODEL), lambda j, s: (0, j, 0)),
                   pl.BlockSpec((n_batch, blk, LANES), whole),
                   pl.BlockSpec((n_batch, blk, LANES), whole),
                   pl.BlockSpec((n_batch, SUBLANES, CONV_DIM), whole)],
        scratch_shapes=[pltpu.VMEM((n_batch, blk, LANES), F32), pltpu.VMEM((n_batch, blk, LANES), F32),
                        pltpu.VMEM((n_batch, SUBLANES, CONV_DIM), F32)])
    return pl.pallas_call(
        functools.partial(_ab_prompt_kernel, n_batch=n_batch),
        grid_spec=grid_spec,
        out_shape=[jax.ShapeDtypeStruct((n_batch, seq, D_MODEL), F32),
                   jax.ShapeDtypeStruct((n_batch, blk, LANES), F32),
                   jax.ShapeDtypeStruct((n_batch, blk, LANES), F32),
                   jax.ShapeDtypeStruct((n_batch, SUBLANES, CONV_DIM), F32)],
        compiler_params=_cparams(1),
        name="ab_prompt",
    )(sinks, *([z] * n_batch), *([x] * n_batch), cos, sin, cw, qg, kg, seg, wo)


def _ab_sample_kernel(sink_ref, z_ref, x_ref, cos_ref, sin_ref, cw_ref, qg_ref, kg_ref, seg_ref, wo_ref,
                      cc_ref, ck_ref, cv_ref, y_ref, cs_ref, ks_ref, vs_ref):
    rows = z_ref.shape[0]
    t_len = SUBLANES
    bt = rows // t_len
    cos = cos_ref[...]
    sin = sin_ref[...]
    seg = seg_ref[...]
    hi_half = (lax.broadcasted_iota(jnp.int32, (rows, LANES), 1) & 32) != 0

    gate_b = z_ref[:, 0:CONV_DIM]
    u = z_ref[:, CONV_DIM:2 * CONV_DIM] * z_ref[:, 2 * CONV_DIM:3 * CONV_DIM]
    u3 = u.reshape(bt, t_len, CONV_DIM)
    cc = cc_ref[...]
    c0 = cc[:, 0:1, :]
    c1 = cc[:, 1:2, :]
    t8 = lax.broadcasted_iota(jnp.int32, (bt, t_len, CONV_DIM), 1)
    r1 = pltpu.roll(u3, 1, 1)
    r2 = pltpu.roll(u3, 2, 1)
    um1 = jnp.where(t8 >= 1, r1, c1)
    um2 = jnp.where(t8 >= 2, r2, jnp.where(t8 == 1, c1, c0))
    cw = cw_ref[...]
    conv = cw[0:1][None] * um2 + cw[1:2][None] * um1 + cw[2:3][None] * u3
    yconv = gate_b * conv.reshape(rows, CONV_DIM)
    cs_ref[...] = r2[:, 0:2, :]

    q0 = 3 * CONV_DIM
    k0 = q0 + N_Q_HEADS * HEAD_DIM
    v0 = k0 + N_KV_HEADS * HEAD_DIM
    qg = qg_ref[...]
    qr = [_headnorm_rope(z_ref[:, q0 + c * LANES:q0 + (c + 1) * LANES], qg, cos, sin, seg, hi_half)
          for c in range(N_Q_HEADS * HEAD_DIM // LANES)]
    kr = _headnorm_rope(z_ref[:, k0:k0 + LANES], kg_ref[...], cos, sin, seg, hi_half)
    v = z_ref[:, v0:v0 + LANES]

    group = N_Q_HEADS // N_KV_HEADS
    nq = group * t_len
    nk = 2 * WINDOW
    qrow = lax.broadcasted_iota(jnp.int32, (nq, nk), 0)
    t_q = qrow & (t_len - 1)
    col = lax.broadcasted_iota(jnp.int32, (nq, nk), 1)
    mask = (((col < WINDOW) & (col > t_q)) | ((col >= WINDOW) & (col - WINDOW <= t_q)))[None]
    hrow = lax.broadcasted_iota(jnp.int32, (nq, 1), 0) // t_len
    pad = jnp.zeros((bt, nk - WINDOW - t_len, HEAD_DIM), F32)

    outs = [None] * N_Q_HEADS
    for g in range(N_KV_HEADS):
        sl = slice(g * HEAD_DIM, (g + 1) * HEAD_DIM)
        qs = jnp.concatenate(
            [qr[h // 2][:, (h % 2) * HEAD_DIM:(h % 2 + 1) * HEAD_DIM].reshape(bt, t_len, HEAD_DIM)
             for h in range(g * group, (g + 1) * group)], axis=1)
        kk = jnp.concatenate([ck_ref[:, :, sl], kr[:, sl].reshape(bt, t_len, HEAD_DIM), pad], axis=1)
        vv = jnp.concatenate([cv_ref[:, :, sl], v[:, sl].reshape(bt, t_len, HEAD_DIM), pad], axis=1)
        s = jnp.einsum('bqd,bkd->bqk', qs.astype(BF16), kk.astype(BF16),
                       preferred_element_type=F32) * (HEAD_DIM ** -0.5)
        sink = jnp.zeros((nq, 1), F32)
        for hh in range(group):
            sink = jnp.where(hrow == hh, sink_ref[g * group + hh], sink)
        p = _softmax_sink(s, mask, sink[None])
        o = jnp.einsum('bqk,bkd->bqd', p, vv.astype(BF16), preferred_element_type=F32)
        for hh in range(group):
            outs[g * group + hh] = o[:, hh * t_len:(hh + 1) * t_len, :].reshape(rows, HEAD_DIM)
    attn = jnp.concatenate(outs, axis=1)

    y = (jnp.dot(yconv.astype(BF16), wo_ref[0:CONV_DIM, :], preferred_element_type=F32)
         + jnp.dot(attn.astype(BF16), wo_ref[CONV_DIM:2 * CONV_DIM, :], preferred_element_type=F32))
    y_ref[...] = y + x_ref[...]

    keep = WINDOW - t_len
    ks_ref[:, 0:keep, :] = ck_ref[:, t_len:WINDOW, :]
    ks_ref[:, keep:WINDOW, :] = kr.reshape(bt, t_len, LANES)
    vs_ref[:, 0:keep, :] = cv_ref[:, t_len:WINDOW, :]
    vs_ref[:, keep:WINDOW, :] = v.reshape(bt, t_len, LANES)


def _ab_sample(z, x, row0, n_seq, t_len, cos, sin, cw, qg, kg, sinks, seg, wo, cc, ck, cv, bt=16):
    rows = bt * t_len
    blk0 = row0 // rows
    tok = lambda i, s: (blk0 + i, 0)
    const = lambda i, s: (0, 0)
    seq3 = lambda i, s: (i, 0, 0)
    grid_spec = pltpu.PrefetchScalarGridSpec(
        num_scalar_prefetch=1,
        grid=(n_seq // bt,),
        in_specs=[pl.BlockSpec((rows, AB_IN), tok),
                  pl.BlockSpec((rows, D_MODEL), tok),
                  pl.BlockSpec((rows, LANES), const),
                  pl.BlockSpec((rows, LANES), const),
                  pl.BlockSpec((3, CONV_DIM), const),
                  pl.BlockSpec((1, LANES), const),
                  pl.BlockSpec((1, LANES), const),
                  pl.BlockSpec((LANES, LANES), const),
                  pl.BlockSpec((D_MODEL, D_MODEL), const),
                  pl.BlockSpec((bt, 2, CONV_DIM), seq3),
                  pl.BlockSpec((bt, WINDOW, LANES), seq3),
                  pl.BlockSpec((bt, WINDOW, LANES), seq3)],
        out_specs=[pl.BlockSpec((rows, D_MODEL), lambda i, s: (i, 0)),
                   pl.BlockSpec((bt, 2, CONV_DIM), seq3),
                   pl.BlockSpec((bt, WINDOW, LANES), seq3),
                   pl.BlockSpec((bt, WINDOW, LANES), seq3)])
    return pl.pallas_call(
        _ab_sample_kernel,
        grid_spec=grid_spec,
        out_shape=[jax.ShapeDtypeStruct((n_seq * t_len, D_MODEL), F32),
                   jax.ShapeDtypeStruct((n_seq, 2, CONV_DIM), F32),
                   jax.ShapeDtypeStruct((n_seq, WINDOW, LANES), F32),
                   jax.ShapeDtypeStruct((n_seq, WINDOW, LANES), F32)],
        compiler_params=_cparams(1),
        name="ab_sample",
    )(sinks, z, x, cos, sin, cw, qg, kg, seg, wo, cc, ck, cv)


def _log_sigmoid(x):
    return jnp.minimum(x, 0.0) - jnp.log(1.0 + jnp.exp(-jnp.abs(x)))


def _mlstm_chunk(z, x, bias, og, tril, wo, c_src, n_src, m_src, c_dst, n_dst, m_dst, n_real):
    L = z.shape[0]
    gates = z[:, ML_GATE_COL:ML_GATE_COL + LANES] + bias
    if n_real < L:
        live = lax.broadcasted_iota(jnp.int32, (L, LANES), 0) < n_real
        li_all = jnp.where(live, gates, -1e30)
        lf_all = jnp.where(live, _log_sigmoid(gates), 0.0)
    else:
        li_all = gates
        lf_all = _log_sigmoid(gates)
    lf_pieces = _bf16_pieces(lf_all, 3)
    f_col_all = sum(jnp.dot(tril, p, preferred_element_type=F32) for p in lf_pieces)
    f_row_all = sum(lax.dot_general(p, tril, (((0,), (1,)), ((), ())), preferred_element_type=F32)
                    for p in lf_pieces)
    li_t = li_all.T
    rr = lax.broadcasted_iota(jnp.int32, (L, L), 0)
    cc = lax.broadcasted_iota(jnp.int32, (L, L), 1)
    causal = cc <= rr

    outs, m_new_all = [], []
    for h in range(ML_HEADS):
        f_col = f_col_all[:, ML_HEADS + h:ML_HEADS + h + 1]
        f_row = f_row_all[ML_HEADS + h:ML_HEADS + h + 1, :]
        li_row = li_t[h:h + 1, :]
        li_col = li_all[:, h:h + 1]
        m0 = m_src[0:1, h:h + 1]
        c0 = c_src[h]
        n0 = n_src[h:h + 1, :]
        qh = z[:, h * ML_QK:(h + 1) * ML_QK]
        kh = z[:, ML_HEADS * ML_QK + h * ML_QK:ML_HEADS * ML_QK + (h + 1) * ML_QK] * (ML_QK ** -0.5)
        v_off = 2 * ML_HEADS * ML_QK
        vh = z[:, v_off + h * ML_V:v_off + (h + 1) * ML_V]
        o_off = v_off + ML_HEADS * ML_V
        oh = z[:, o_off + h * ML_V:o_off + (h + 1) * ML_V]
        qb = qh.astype(BF16)
        vb = vh.astype(BF16)

        dmat = jnp.where(causal, f_col - f_row + li_row, -jnp.inf)
        gcar = f_col + m0
        m_t = jnp.maximum(jnp.max(dmat, axis=-1, keepdims=True), gcar)
        w = jnp.exp(dmat - m_t)
        s = lax.dot_general(qb, kh.astype(BF16), NT_DIMS, preferred_element_type=F32) * w
        carry = jnp.exp(gcar - m_t)
        num = (jnp.dot(s.astype(BF16), vb, preferred_element_type=F32)
               + jnp.dot(qb, c0.astype(BF16), preferred_element_type=F32) * carry)
        den = jnp.sum(s, axis=-1, keepdims=True) + carry * jnp.sum(qh * n0, axis=-1, keepdims=True)
        hout = num / jnp.maximum(jnp.abs(den), jnp.exp(-m_t))

        f_last = f_col[L - 1:L, :]
        w_end = f_last - f_col + li_col
        m_new = jnp.maximum(f_last + m0, jnp.max(w_end, axis=0, keepdims=True))
        a_end = jnp.exp(w_end - m_new)
        scale = jnp.exp(f_last + m0 - m_new)
        ka = kh * a_end
        c_dst[h] = scale * c0 + lax.dot_general(ka.astype(BF16), vb, TN_DIMS, preferred_element_type=F32)
        n_dst[h:h + 1, :] = scale * n0 + jnp.sum(ka, axis=0, keepdims=True)
        m_new_all.append(m_new)

        hn = _rmsnorm(hout, og[:, h * ML_V:(h + 1) * ML_V])
        outs.append(jax.nn.sigmoid(oh) * hn)
    m_dst[...] = jnp.concatenate(m_new_all, axis=1)
    out = jnp.concatenate(outs, axis=1)
    return jnp.dot(out.astype(BF16), wo, preferred_element_type=F32) + x


def _mlstm_prompt_kernel(*refs, n_batch):
    z_refs, x_refs = refs[0:n_batch], refs[n_batch:2 * n_batch]
    bias_ref, og_ref, tril_ref, wo_ref, y_ref, c_ref, n_ref, m_ref = refs[2 * n_batch:]

    @pl.when(pl.program_id(0) == 0)
    def _():
        c_ref[...] = jnp.zeros_like(c_ref)
        n_ref[...] = jnp.zeros_like(n_ref)
        m_ref[...] = jnp.zeros_like(m_ref)

    for b in range(n_batch):
        state = (c_ref.at[b], n_ref.at[b], m_ref.at[b])
        y_ref[b] = _mlstm_chunk(z_refs[b][...], x_refs[b][...], bias_ref[...], og_ref[...], tril_ref[...],
                                wo_ref[...], *state, *state, ML_CHUNK)


def _mlstm_sample_kernel(z_ref, x_ref, bias_ref, og_ref, tril_ref, wo_ref, c0_ref, n0_ref, m0_ref,
                         y_ref, c_ref, n_ref, m_ref, *, t_len):
    n_here = z_ref.shape[0] // t_len
    for s in range(n_here):
        rows = slice(s * t_len, (s + 1) * t_len)
        zpad = jnp.concatenate([z_ref[rows, :], jnp.zeros((ML_CHUNK - t_len, ML_IN_PAD), F32)], axis=0)
        xpad = jnp.concatenate([x_ref[rows, :], jnp.zeros((ML_CHUNK - t_len, D_MODEL), F32)], axis=0)
        y = _mlstm_chunk(zpad, xpad, bias_ref[...], og_ref[...], tril_ref[...], wo_ref[...],
                         c0_ref.at[s], n0_ref.at[s], m0_ref.at[s], c_ref.at[s], n_ref.at[s], m_ref.at[s], t_len)
        y_ref[rows, :] = y[0:t_len, :]


def _mlstm_weight_specs(const):
    return [pl.BlockSpec((1, LANES), const),
            pl.BlockSpec((1, D_MODEL), const),
            pl.BlockSpec((ML_CHUNK, ML_CHUNK), const),
            pl.BlockSpec((D_MODEL, D_MODEL), const)]


def _mlstm_prompt(z, x, n_batch, seq, bias, og, tril, wo):
    nc = seq // ML_CHUNK
    const = lambda j: (0, 0)
    tok_specs = lambda width: [pl.BlockSpec((ML_CHUNK, width), functools.partial(lambda j, b: (b * nc + j, 0), b=b))
                               for b in range(n_batch)]
    return pl.pallas_call(
        functools.partial(_mlstm_prompt_kernel, n_batch=n_batch),
        grid=(nc,),
        in_specs=tok_specs(ML_IN_PAD) + tok_specs(D_MODEL) + _mlstm_weight_specs(const),
        out_specs=[pl.BlockSpec((n_batch, ML_CHUNK, D_MODEL), lambda j: (0, j, 0)),
                   pl.BlockSpec((n_batch, ML_HEADS, ML_QK, ML_V), lambda j: (0, 0, 0, 0)),
                   pl.BlockSpec((n_batch, ML_HEADS, ML_QK), lambda j: (0, 0, 0)),
                   pl.BlockSpec((n_batch, 1, ML_HEADS), lambda j: (0, 0, 0))],
        out_shape=[jax.ShapeDtypeStruct((n_batch, seq, D_MODEL), F32),
                   jax.ShapeDtypeStruct((n_batch, ML_HEADS, ML_QK, ML_V), F32),
                   jax.ShapeDtypeStruct((n_batch, ML_HEADS, ML_QK), F32),
                   jax.ShapeDtypeStruct((n_batch, 1, ML_HEADS), F32)],
        compiler_params=_cparams(1),
        name="mlstm_prompt",
    )(*([z] * n_batch), *([x] * n_batch), bias, og, tril, wo)


def _mlstm_sample(z, x, row0, n_seq, t_len, bias, og, tril, wo, c0, n0, m0, seqs_per_step=4):
    rows = seqs_per_step * t_len
    blk0 = row0 // rows
    tok = lambda i: (blk0 + i, 0)
    const = lambda i: (0, 0)
    st4 = lambda i: (i, 0, 0, 0)
    st3 = lambda i: (i, 0, 0)
    state_specs = [pl.BlockSpec((seqs_per_step, ML_HEADS, ML_QK, ML_V), st4),
                   pl.BlockSpec((seqs_per_step, ML_HEADS, ML_QK), st3),
                   pl.BlockSpec((seqs_per_step, 1, ML_HEADS), st3)]
    return pl.pallas_call(
        functools.partial(_mlstm_sample_kernel, t_len=t_len),
        grid=(n_seq // seqs_per_step,),
        in_specs=[pl.BlockSpec((rows, ML_IN_PAD), tok),
                  pl.BlockSpec((rows, D_MODEL), tok)] + _mlstm_weight_specs(const) + state_specs,
        out_specs=[pl.BlockSpec((rows, D_MODEL), lambda i: (i, 0))] + state_specs,
        out_shape=[jax.ShapeDtypeStruct((n_seq * t_len, D_MODEL), F32),
                   jax.ShapeDtypeStruct((n_seq, ML_HEADS, ML_QK, ML_V), F32),
                   jax.ShapeDtypeStruct((n_seq, ML_HEADS, ML_QK), F32),
                   jax.ShapeDtypeStruct((n_seq, 1, ML_HEADS), F32)],
        compiler_params=_cparams(1),
        name="mlstm_sample",
    )(z, x, bias, og, tril, wo, c0, n0, m0)


def _gelu(x):
    return 0.5 * x * (1.0 + lax.erf(x * 0.7071067811865476))


SEL_T = 256


def _unit_scores(q_scr, sk_ref, half, h):
    return [lax.dot_general(sk_ref[h, p], q_scr[half, 2 * h + p].astype(BF16), NT_DIMS,
                            preferred_element_type=F32) for p in range(2)]


class _TopK:
    def __init__(self, s, payload=None):
        self.s = s
        self.payload = payload
        self.rows = lax.broadcasted_iota(jnp.int32, s.shape, 0).astype(F32)
        self.vals, self.picks = [], []

    def step(self, n):
        bound = float(self.s.shape[0])
        for _ in range(n):
            m = jnp.max(self.s, axis=0, keepdims=True)
            first = jnp.min(jnp.where(self.s == m, self.rows, bound), axis=0, keepdims=True)
            sel = self.rows == first
            self.vals.append(m)
            if self.payload is None:
                self.picks.append(first)
            else:
                self.picks.append(jnp.max(jnp.where(sel, self.payload, -1.0), axis=0, keepdims=True))
            self.s = jnp.where(sel, -jnp.inf, self.s)

    def result(self):
        return jnp.concatenate(self.vals, axis=0), jnp.concatenate(self.picks, axis=0)


def _pair_candidates(first, second):
    def pairs(a, b, combine):
        h8 = SUBLANES
        rows = [combine(a[0:1], b)]
        rows += [combine(a[k1:k1 + 1], b[0:h8]) for k1 in range(1, h8)]
        rows.append(combine(a[h8:PEER_TOPK], b[0:1]))
        return jnp.concatenate(rows, axis=0)

    (v1, i1), (v2, i2) = first, second
    return pairs(v1, v2, lambda a, b: a + b), pairs(i1, i2, lambda a, b: a * N_KEYS + b)


def _peer_fused_kernel(xs_ref, xm_ref, g_ref, wq_ref, sk_ref, u_ref, v_ref, o_ref,
                       q_scr, sel_e, sel_g, xn_scr, a_scr, bg_scr, gmat, coef_scr):
    i = pl.program_id(0)
    e = pl.program_id(1)
    n_steps = pl.num_programs(1)
    t = xm_ref.shape[0]
    n_half = t // SEL_T
    half_keys = N_KEYS // 2
    cur = i % 2

    def project_queries():
        xn = _rmsnorm(xs_ref[...], g_ref[...]).astype(BF16)
        q = jnp.dot(xn, wq_ref[...], preferred_element_type=F32)
        for hf in range(n_half):
            for c in range(2 * PEER_HEADS):
                q_scr[hf, c] = q[hf * SEL_T:(hf + 1) * SEL_T, c * LANES:(c + 1) * LANES]

    @pl.when(e == 0)
    def _():
        @pl.when(i == 0)
        def _():
            project_queries()
            sel_e[...] = jnp.zeros_like(sel_e)
            sel_g[...] = jnp.zeros_like(sel_g)

        x = xm_ref[...]
        xn_scr[...] = _rmsnorm(x, g_ref[...]).astype(BF16)
        o_ref[...] = x
        coef_scr[0] = jnp.zeros(coef_scr.shape[1:], BF16)
        prev = 1 - cur
        lane_groups = SEL_T // LANES
        for hf in range(n_half):
            ef = sel_e[prev, hf]
            af = jnp.floor(ef * (1.0 / N_KEYS))
            bf = ef - af * N_KEYS
            gf = sel_g[prev, hf]
            a_scr[hf * SEL_T:(hf + 1) * SEL_T, :] = af.T
            for lg in range(lane_groups):
                bg_scr[hf * lane_groups + lg] = (bf + 0.5 * gf)[:, lg * LANES:(lg + 1) * LANES]
        r = lax.broadcasted_iota(jnp.int32, (N_KEYS, LANES), 0)
        packed_shape = (N_KEYS // (2 * SUBLANES), 2 * SUBLANES, LANES)
        a_of_row = jnp.where(r < half_keys, 2 * r, 2 * (r - half_keys) + 1).astype(F32).astype(BF16)
        a_of_row = a_of_row.reshape(packed_shape)
        one = jnp.ones(packed_shape, BF16)
        zero = jnp.zeros(packed_shape, BF16)
        b_of_lane = lax.broadcasted_iota(jnp.int32, (N_KEYS, LANES), 1).astype(F32)
        per_body = 32
        bodies_per_group = LANES // per_body

        def body(it, carry):
            grp = lax.shift_right_logical(it, bodies_per_group.bit_length() - 1)
            sub = it & (bodies_per_group - 1)
            shift = (LANES - sub * per_body) & (LANES - 1)
            bg = pltpu.roll(bg_scr[grp], shift, 1)
            tok0 = pl.multiple_of(it * per_body, per_body)
            arows = a_scr[pl.ds(tok0, per_body), :]
            for k in range(per_body):
                arow = jnp.broadcast_to(arows[k:k + 1], (2 * SUBLANES, LANES)).astype(BF16)[None]
                pa = jnp.where(a_of_row == arow, one, zero).reshape(N_KEYS, LANES)
                bgcol = jnp.broadcast_to(bg[:, k:k + 1], (N_KEYS, LANES))
                bcol = jnp.floor(bgcol)
                gcol = 2.0 * (bgcol - bcol)
                qbt = jnp.where(bcol == b_of_lane, gcol, 0.0).astype(BF16)
                tile = jnp.dot(pa, qbt, preferred_element_type=F32)
                row0 = pl.multiple_of((tok0 + k) * G_PITCH, SUBLANES)
                gmat[pl.ds(row0, half_keys), :] = pltpu.pack_elementwise(
                    [tile[0:half_keys], tile[half_keys:N_KEYS]], packed_dtype=BF16)
            return carry

        lax.fori_loop(0, t // per_body, body, 0)

    unit = jnp.minimum(e, n_half * PEER_HEADS - 1)
    half = lax.shift_right_logical(unit, PEER_HEADS.bit_length() - 1)
    h = unit & (PEER_HEADS - 1)
    first_level = [_TopK(st) for st in _unit_scores(q_scr, sk_ref, half, h)]

    rd = e & 1
    o_ref[...] += jnp.dot(coef_scr[rd], v_ref[...], preferred_element_type=F32)
    act = lax.dot_general(xn_scr[...], u_ref[...], NT_DIMS, preferred_element_type=F32)
    blk = jnp.minimum(e, n_steps - 2)
    words_per_step = u_ref.shape[0] // (2 * N_KEYS)
    gates = []
    for w in range(words_per_step):
        word = gmat[pl.ds(blk * words_per_step + w, t, stride=G_PITCH), :]
        gates.append(lax.bitcast_convert_type(word << 16, F32))
        gates.append(lax.bitcast_convert_type(word & jnp.int32(-65536), F32))
    coef_scr[1 - rd] = (_gelu(act) * jnp.concatenate(gates, axis=1)).astype(BF16)

    for lst in first_level:
        lst.step(PEER_TOPK)
    cand, expert = _pair_candidates(first_level[0].result(), first_level[1].result())
    second_level = _TopK(cand, payload=expert)
    second_level.step(PEER_TOPK)
    best, e_sel = second_level.result()
    ex = jnp.exp(best - best[0:1])
    slot0 = pl.multiple_of(h * PEER_TOPK, PEER_TOPK)
    sel_e[cur, half, pl.ds(slot0, PEER_TOPK), :] = e_sel
    sel_g[cur, half, pl.ds(slot0, PEER_TOPK), :] = ex / jnp.sum(ex, axis=0, keepdims=True)

    @pl.when(e == n_steps - 1)
    def _():
        project_queries()


def _peer(x, g, wq, sk, u, v, t=512, eb=1024):
    n, d = x.shape
    nslot = PEER_HEADS * PEER_TOPK
    n_tok_blk = n // t
    n_exp_blk = N_KEYS * N_KEYS // eb
    n_half = t // SEL_T
    assert n_exp_blk == n_half * PEER_HEADS
    last_tok = n_tok_blk - 1
    once = pl.Buffered(1)
    return pl.pallas_call(
        _peer_fused_kernel,
        grid=(n_tok_blk + 1, n_exp_blk + 1),
        in_specs=[pl.BlockSpec((t, d), lambda i, e: (jnp.minimum(i + e // n_exp_blk, last_tok), 0),
                               pipeline_mode=once),
                  pl.BlockSpec((t, d), lambda i, e: (jnp.maximum(i - 1, 0), 0)),
                  pl.BlockSpec((1, d), lambda i, e: (0, 0)),
                  pl.BlockSpec((d, 2 * PEER_HEADS * LANES), lambda i, e: (0, 0), pipeline_mode=once),
                  pl.BlockSpec((PEER_HEADS, 2, N_KEYS, LANES), lambda i, e: (0, 0, 0, 0), pipeline_mode=once),
                  pl.BlockSpec((eb, d), lambda i, e: (jnp.minimum(e, n_exp_blk - 1), 0)),
                  pl.BlockSpec((eb, d), lambda i, e: (jnp.maximum(e - 1, 0), 0))],
        out_specs=pl.BlockSpec((t, d), lambda i, e: (jnp.maximum(i - 1, 0), 0)),
        out_shape=jax.ShapeDtypeStruct((n, d), F32),
        scratch_shapes=[pltpu.VMEM((n_half, 2 * PEER_HEADS, SEL_T, LANES), F32),
                        pltpu.VMEM((2, n_half, nslot, SEL_T), F32),
                        pltpu.VMEM((2, n_half, nslot, SEL_T), F32),
                        pltpu.VMEM((t, d), BF16),
                        pltpu.VMEM((t, nslot), F32),
                        pltpu.VMEM((t // LANES, nslot, LANES), F32),
                        pltpu.VMEM((t * G_PITCH, LANES), jnp.int32),
                        pltpu.VMEM((2, t, eb), BF16)],
        compiler_params=_cparams(2),
        name="peer",
    )(x, x, g.reshape(1, d), wq, sk, u, v)


def _rope_tables(pos):
    half = HEAD_DIM // 2
    inv_freq = ROPE_THETA ** (-jnp.arange(half, dtype=F32) / half)
    ang = pos.astype(F32)[:, None] * inv_freq[None, :]
    cos, sin = jnp.cos(ang), jnp.sin(ang)
    reps = LANES // HEAD_DIM
    cos_t = jnp.tile(jnp.concatenate([cos, cos], axis=1), (1, reps))
    sin_t = jnp.tile(jnp.concatenate([-sin, sin], axis=1), (1, reps))
    return cos_t, sin_t


def kernel(x_prompt, x_sample, cache_conv, cache_win_k, cache_win_v, state_mlstm_C, state_mlstm_n,
           state_mlstm_m, norm_mix, norm_ffn, ab_w_in, ab_conv_w, ab_q_gain, ab_k_gain, ab_sinks, ab_w_out,
           ml_w_in, ml_gate_bias, ml_out_gain, ml_w_out, peer_w_q, peer_sub_keys, peer_u, peer_v):
    n_batch, seq, d = x_prompt.shape
    n_seq, t_len, _ = x_sample.shape
    n_prompt = n_batch * seq
    assert d == D_MODEL and t_len == SUBLANES and norm_mix.shape[0] == 2

    x = jnp.concatenate([x_prompt.reshape(n_prompt, d), x_sample.reshape(n_seq * t_len, d)], axis=0)

    cos_p, sin_p = _rope_tables(jnp.arange(seq, dtype=jnp.int32))
    cos_s, sin_s = _rope_tables(PAST_LEN + jnp.arange(t_len, dtype=jnp.int32))
    bt = 16
    cos_s, sin_s = jnp.tile(cos_s, (bt, 1)), jnp.tile(sin_s, (bt, 1))
    lane = jnp.arange(LANES)
    seg = (lane[:, None] // HEAD_DIM == lane[None, :] // HEAD_DIM).astype(BF16)
    reps = LANES // HEAD_DIM
    qg = jnp.tile(ab_q_gain[0], reps).reshape(1, LANES)
    kg = jnp.tile(ab_k_gain[0], reps).reshape(1, LANES)
    wo_ab = ab_w_out[0].astype(BF16)

    z = _norm_proj(x, norm_mix[0], ab_w_in[0].astype(BF16))
    y_p, k_p, v_p, c_p = _ab_prompt(z, x, n_batch, seq, cos_p, sin_p, ab_conv_w[0], qg, kg, ab_sinks[0],
                                    seg, wo_ab)
    y_s, c_s, k_s, v_s = _ab_sample(z, x, n_prompt, n_seq, t_len, cos_s, sin_s, ab_conv_w[0], qg, kg,
                                    ab_sinks[0], seg, wo_ab, cache_conv[0],
                                    cache_win_k[0].reshape(n_seq, WINDOW, LANES),
                                    cache_win_v[0].reshape(n_seq, WINDOW, LANES), bt=bt)
    x = jnp.concatenate([y_p.reshape(n_prompt, d), y_s], axis=0)
    x = _peer(x, norm_ffn[0], peer_w_q[0].astype(BF16), peer_sub_keys[0].astype(BF16),
              peer_u[0].astype(BF16), peer_v[0].astype(BF16))

    n_gate = 2 * ML_HEADS
    w_in = jnp.pad(ml_w_in[0], ((0, 0), (0, ML_IN_PAD - ml_w_in.shape[2]))).astype(BF16)
    bias = jnp.pad(ml_gate_bias[0], (0, LANES - n_gate)).reshape(1, LANES)
    og = ml_out_gain[0].reshape(1, D_MODEL)
    idx = jnp.arange(ML_CHUNK)
    tril = (idx[None, :] <= idx[:, None]).astype(BF16)
    wo_ml = ml_w_out[0].astype(BF16)

    z = _norm_proj(x, norm_mix[1], w_in)
    y_p, cm_p, nm_p, mm_p = _mlstm_prompt(z, x, n_batch, seq, bias, og, tril, wo_ml)
    y_s, cm_s, nm_s, mm_s = _mlstm_sample(z, x, n_prompt, n_seq, t_len, bias, og, tril, wo_ml,
                                          state_mlstm_C[0], state_mlstm_n[0],
                                          state_mlstm_m[0].reshape(n_seq, 1, ML_HEADS))
    x = jnp.concatenate([y_p.reshape(n_prompt, d), y_s], axis=0)
    x = _peer(x, norm_ffn[1], peer_w_q[1].astype(BF16), peer_sub_keys[1].astype(BF16),
              peer_u[1].astype(BF16), peer_v[1].astype(BF16))

    y_prompt = x[:n_prompt].reshape(n_batch, seq, d)
    y_sample = x[n_prompt:].reshape(n_seq, t_len, d)
    kv_shape_p = (1, n_batch, WINDOW, N_KV_HEADS, HEAD_DIM)
    kv_shape_s = (1, n_seq, WINDOW, N_KV_HEADS, HEAD_DIM)
    return (y_prompt, y_sample,
            c_p[:, SUBLANES - 2:, :][None], k_p.reshape(kv_shape_p), v_p.reshape(kv_shape_p),
            cm_p[None], nm_p[None], mm_p.reshape(1, n_batch, ML_HEADS),
            c_s[None], k_s.reshape(kv_shape_s), v_s.reshape(kv_shape_s),
            cm_s[None], nm_s[None], mm_s.reshape(1, n_seq, ML_HEADS))
```

```python
import functools

import jax
import jax.numpy as jnp
from jax import lax
from jax.experimental import pallas as pl
from jax.experimental.pallas import tpu as pltpu

F32 = jnp.float32
BF16 = jnp.bfloat16
EPS = 1e-6

D_MODEL = 1024
CONV_DIM = 512
N_Q_HEADS = 8
N_KV_HEADS = 2
HEAD_DIM = 64
WINDOW = 128
ROPE_THETA = 10000.0
AB_IN = 2304
ML_HEADS = 4
ML_QK = 128
ML_V = 256
ML_CHUNK = 128
ML_GATE_COL = 3072
ML_IN_PAD = ML_GATE_COL + 128
N_KEYS = 128
PEER_HEADS = 8
PEER_TOPK = 16
PAST_LEN = 16384

LANES = 128
SUBLANES = 8
G_PITCH = N_KEYS // 2 + SUBLANES
VMEM_LIMIT = 56 * 1024 * 1024

NT_DIMS = (((1,), (1,)), ((), ()))
TN_DIMS = (((0,), (0,)), ((), ()))


def _cparams(n_axes, vmem=VMEM_LIMIT):
    return pltpu.CompilerParams(dimension_semantics=("arbitrary",) * n_axes, vmem_limit_bytes=vmem)


def _rmsnorm(x, g):
    return x * lax.rsqrt(jnp.mean(x * x, axis=-1, keepdims=True) + EPS) * g


def _bf16_pieces(a, terms):
    pieces = []
    rem = a
    for _ in range(terms):
        piece = rem.astype(BF16)
        rem = rem - piece.astype(F32)
        pieces.append(piece)
    return pieces


def _split_dot(a, b_bf16, terms=2):
    return sum(jnp.dot(p, b_bf16, preferred_element_type=F32) for p in _bf16_pieces(a, terms))


def _norm_proj_kernel(x_ref, g_ref, w_ref, o_ref):
    r = _rmsnorm(x_ref[...], g_ref[...])
    o_ref[...] = jnp.dot(r.astype(BF16), w_ref[...], preferred_element_type=F32)


def _norm_proj(x, g, w_bf16, tm=512):
    n, d = x.shape
    nout = w_bf16.shape[1]
    return pl.pallas_call(
        _norm_proj_kernel,
        grid=(n // tm,),
        in_specs=[pl.BlockSpec((tm, d), lambda i: (i, 0)),
                  pl.BlockSpec((1, d), lambda i: (0, 0)),
                  pl.BlockSpec((d, nout), lambda i: (0, 0))],
        out_specs=pl.BlockSpec((tm, nout), lambda i: (i, 0)),
        out_shape=jax.ShapeDtypeStruct((n, nout), F32),
        compiler_params=_cparams(1),
        name="norm_proj",
    )(x, g.reshape(1, d), w_bf16)


def _headnorm_rope(xc, gain, cos, sin, seg, hi_half):
    ss = _split_dot(xc * xc, seg)
    xn = xc * lax.rsqrt(ss * (1.0 / HEAD_DIM) + EPS) * gain
    partner = jnp.where(hi_half, pltpu.roll(xn, 32, 1), pltpu.roll(xn, 96, 1))
    return xn * cos + partner * sin


def _softmax_sink(s, mask, sink):
    s = jnp.where(mask, s, -1e30)
    m = jnp.maximum(jnp.max(s, axis=-1, keepdims=True), sink)
    p = jnp.exp(s - m)
    denom = jnp.sum(p, axis=-1, keepdims=True) + jnp.exp(sink - m)
    return (p / denom).astype(BF16)


def _ab_prompt_kernel(sink_ref, *refs, n_batch):
    z_refs, x_refs = refs[0:n_batch], refs[n_batch:2 * n_batch]
    (cos_ref, sin_ref, cw_ref, qg_ref, kg_ref, seg_ref, wo_ref,
     y_ref, kst_ref, vst_ref, cst_ref, pk_ref, pv_ref, pu_ref) = refs[2 * n_batch:]
    j = pl.program_id(0)

    @pl.when(j == 0)
    def _():
        pk_ref[...] = jnp.zeros_like(pk_ref)
        pv_ref[...] = jnp.zeros_like(pv_ref)
        pu_ref[...] = jnp.zeros_like(pu_ref)

    for b in range(n_batch):
        _ab_prompt_block(j, sink_ref, z_refs[b], x_refs[b], cos_ref, sin_ref, cw_ref, qg_ref, kg_ref, seg_ref,
                         wo_ref, y_ref.at[b], kst_ref.at[b], vst_ref.at[b], cst_ref.at[b],
                         pk_ref.at[b], pv_ref.at[b], pu_ref.at[b])


def _ab_prompt_block(j, sink_ref, z_ref, x_ref, cos_ref, sin_ref, cw_ref, qg_ref, kg_ref, seg_ref, wo_ref,
                     y_ref, kst_ref, vst_ref, cst_ref, pk_ref, pv_ref, pu_ref):
    blk = z_ref.shape[0]
    cos = cos_ref[...]
    sin = sin_ref[...]
    seg = seg_ref[...]
    hi_half = (lax.broadcasted_iota(jnp.int32, (blk, LANES), 1) & 32) != 0

    gate_b = z_ref[:, 0:CONV_DIM]
    u = z_ref[:, CONV_DIM:2 * CONV_DIM] * z_ref[:, 2 * CONV_DIM:3 * CONV_DIM]
    ng = blk // SUBLANES
    u3 = u.reshape(ng, SUBLANES, CONV_DIM)
    ext = jnp.concatenate([pu_ref[...][None], u3], axis=0)
    t8 = lax.broadcasted_iota(jnp.int32, (ng, SUBLANES, CONV_DIM), 1)
    r1 = pltpu.roll(ext, 1, 1)
    r2 = pltpu.roll(ext, 2, 1)
    um1 = jnp.where(t8 >= 1, r1[1:], r1[:-1])
    um2 = jnp.where(t8 >= 2, r2[1:], r2[:-1])
    cw = cw_ref[...]
    conv = cw[0:1][None] * um2 + cw[1:2][None] * um1 + cw[2:3][None] * u3
    yconv = gate_b * conv.reshape(blk, CONV_DIM)

    q0 = 3 * CONV_DIM
    k0 = q0 + N_Q_HEADS * HEAD_DIM
    v0 = k0 + N_KV_HEADS * HEAD_DIM
    qg = qg_ref[...]
    qr = [_headnorm_rope(z_ref[:, q0 + c * LANES:q0 + (c + 1) * LANES], qg, cos, sin, seg, hi_half)
          for c in range(N_Q_HEADS * HEAD_DIM // LANES)]
    kr = _headnorm_rope(z_ref[:, k0:k0 + LANES], kg_ref[...], cos, sin, seg, hi_half)
    v = z_ref[:, v0:v0 + LANES]
    pk = pk_ref[...]
    pv = pv_ref[...]

    row = lax.broadcasted_iota(jnp.int32, (blk, 2 * blk), 0)
    col = lax.broadcasted_iota(jnp.int32, (blk, 2 * blk), 1)
    row_prev = row + jnp.where(j == 0, 2 * blk, 0)
    mask = ((col < blk) & (col > row_prev)) | ((col >= blk) & (col - blk <= row))

    kks, vvs = [], []
    for g in range(N_KV_HEADS):
        sl = slice(g * HEAD_DIM, (g + 1) * HEAD_DIM)
        kks.append(jnp.concatenate([pk[:, sl], kr[:, sl]], axis=0).astype(BF16))
        vvs.append(jnp.concatenate([pv[:, sl], v[:, sl]], axis=0).astype(BF16))
    outs = []
    for h in range(N_Q_HEADS):
        g = h // (N_Q_HEADS // N_KV_HEADS)
        qh = qr[h // 2][:, (h % 2) * HEAD_DIM:(h % 2 + 1) * HEAD_DIM].astype(BF16)
        s = lax.dot_general(qh, kks[g], NT_DIMS, preferred_element_type=F32) * (HEAD_DIM ** -0.5)
        p = _softmax_sink(s, mask, sink_ref[h])
        outs.append(jnp.dot(p, vvs[g], preferred_element_type=F32))
    attn = jnp.concatenate(outs, axis=1)

    y = (jnp.dot(yconv.astype(BF16), wo_ref[0:CONV_DIM, :], preferred_element_type=F32)
         + jnp.dot(attn.astype(BF16), wo_ref[CONV_DIM:2 * CONV_DIM, :], preferred_element_type=F32))
    y_ref[...] = y + x_ref[...]

    pk_ref[...] = kr
    pv_ref[...] = v
    pu_ref[...] = u3[ng - 1]
    kst_ref[...] = kr
    vst_ref[...] = v
    cst_ref[...] = u3[ng - 1]


def _ab_prompt(z, x, n_batch, seq, cos, sin, cw, qg, kg, sinks, seg, wo):
    blk = WINDOW
    nb = seq // blk
    const = lambda j, s: (0, 0)
    whole = lambda j, s: (0, 0, 0)
    tok_specs = lambda width: [pl.BlockSpec((blk, width), functools.partial(lambda j, s, b: (b * nb + j, 0), b=b))
                               for b in range(n_batch)]
    grid_spec = pltpu.PrefetchScalarGridSpec(
        num_scalar_prefetch=1,
        grid=(nb,),
        in_specs=tok_specs(AB_IN) + tok_specs(D_MODEL) + [
            pl.BlockSpec((blk, LANES), lambda j, s: (j, 0)),
            pl.BlockSpec((blk, LANES), lambda j, s: (j, 0)),
            pl.BlockSpec((3, CONV_DIM), const),
            pl.BlockSpec((1, LANES), const),
            pl.BlockSpec((1, LANES), const),
            pl.BlockSpec((LANES, LANES), const),
            pl.BlockSpec((D_MODEL, D_MODEL), const)],
        out_specs=[pl.BlockSpec((n_batch, blk, D_MODEL), lambda j, s: (0, j, 0)),
                   pl.BlockSpec((n_batch, blk, LANES), whole),
                   pl.BlockSpec((n_batch, blk, LANES), whole),
                   pl.BlockSpec((n_batch, SUBLANES, CONV_DIM), whole)],
        scratch_shapes=[pltpu.VMEM((n_batch, blk, LANES), F32), pltpu.VMEM((n_batch, blk, LANES), F32),
                        pltpu.VMEM((n_batch, SUBLANES, CONV_DIM), F32)])
    return pl.pallas_call(
        functools.partial(_ab_prompt_kernel, n_batch=n_batch),
        grid_spec=grid_spec,
        out_shape=[jax.ShapeDtypeStruct((n_batch, seq, D_MODEL), F32),
                   jax.ShapeDtypeStruct((n_batch, blk, LANES), F32),
                   jax.ShapeDtypeStruct((n_batch, blk, LANES), F32),
                   jax.ShapeDtypeStruct((n_batch, SUBLANES, CONV_DIM), F32)],
        compiler_params=_cparams(1),
        name="ab_prompt",
    )(sinks, *([z] * n_batch), *([x] * n_batch), cos, sin, cw, qg, kg, seg, wo)


def _ab_sample_kernel(sink_ref, z_ref, x_ref, cos_ref, sin_ref, cw_ref, qg_ref, kg_ref, seg_ref, wo_ref,
                      cc_ref, ck_ref, cv_ref, y_ref, cs_ref, ks_ref, vs_ref):
    rows = z_ref.shape[0]
    t_len = SUBLANES
    bt = rows // t_len
    cos = cos_ref[...]
    sin = sin_ref[...]
    seg = seg_ref[...]
    hi_half = (lax.broadcasted_iota(jnp.int32, (rows, LANES), 1) & 32) != 0

    gate_b = z_ref[:, 0:CONV_DIM]
    u = z_ref[:, CONV_DIM:2 * CONV_DIM] * z_ref[:, 2 * CONV_DIM:3 * CONV_DIM]
    u3 = u.reshape(bt, t_len, CONV_DIM)
    cc = cc_ref[...]
    c0 = cc[:, 0:1, :]
    c1 = cc[:, 1:2, :]
    t8 = lax.broadcasted_iota(jnp.int32, (bt, t_len, CONV_DIM), 1)
    r1 = pltpu.roll(u3, 1, 1)
    r2 = pltpu.roll(u3, 2, 1)
    um1 = jnp.where(t8 >= 1, r1, c1)
    um2 = jnp.where(t8 >= 2, r2, jnp.where(t8 == 1, c1, c0))
    cw = cw_ref[...]
    conv = cw[0:1][None] * um2 + cw[1:2][None] * um1 + cw[2:3][None] * u3
    yconv = gate_b * conv.reshape(rows, CONV_DIM)
    cs_ref[...] = r2[:, 0:2, :]

    q0 = 3 * CONV_DIM
    k0 = q0 + N_Q_HEADS * HEAD_DIM
    v0 = k0 + N_KV_HEADS * HEAD_DIM
    qg = qg_ref[...]
    qr = [_headnorm_rope(z_ref[:, q0 + c * LANES:q0 + (c + 1) * LANES], qg, cos, sin, seg, hi_half)
          for c in range(N_Q_HEADS * HEAD_DIM // LANES)]
    kr = _headnorm_rope(z_ref[:, k0:k0 + LANES], kg_ref[...], cos, sin, seg, hi_half)
    v = z_ref[:, v0:v0 + LANES]

    group = N_Q_HEADS // N_KV_HEADS
    nq = group * t_len
    nk = 2 * WINDOW
    qrow = lax.broadcasted_iota(jnp.int32, (nq, nk), 0)
    t_q = qrow & (t_len - 1)
    col = lax.broadcasted_iota(jnp.int32, (nq, nk), 1)
    mask = (((col < WINDOW) & (col > t_q)) | ((col >= WINDOW) & (col - WINDOW <= t_q)))[None]
    hrow = lax.broadcasted_iota(jnp.int32, (nq, 1), 0) // t_len
    pad = jnp.zeros((bt, nk - WINDOW - t_len, HEAD_DIM), F32)

    outs = [None] * N_Q_HEADS
    for g in range(N_KV_HEADS):
        sl = slice(g * HEAD_DIM, (g + 1) * HEAD_DIM)
        qs = jnp.concatenate(
            [qr[h // 2][:, (h % 2) * HEAD_DIM:(h % 2 + 1) * HEAD_DIM].reshape(bt, t_len, HEAD_DIM)
             for h in range(g * group, (g + 1) * group)], axis=1)
        kk = jnp.concatenate([ck_ref[:, :, sl], kr[:, sl].reshape(bt, t_len, HEAD_DIM), pad], axis=1)
        vv = jnp.concatenate([cv_ref[:, :, sl], v[:, sl].reshape(bt, t_len, HEAD_DIM), pad], axis=1)
        s = jnp.einsum('bqd,bkd->bqk', qs.astype(BF16), kk.astype(BF16),
                       preferred_element_type=F32) * (HEAD_DIM ** -0.5)
        sink = jnp.zeros((nq, 1), F32)
        for hh in range(group):
            sink = jnp.where(hrow == hh, sink_ref[g * group + hh], sink)
        p = _softmax_sink(s, mask, sink[None])
        o = jnp.einsum('bqk,bkd->bqd', p, vv.astype(BF16), preferred_element_type=F32)
        for hh in range(group):
            outs[g * group + hh] = o[:, hh * t_len:(hh + 1) * t_len, :].reshape(rows, HEAD_DIM)
    attn = jnp.concatenate(outs, axis=1)

    y = (jnp.dot(yconv.astype(BF16), wo_ref[0:CONV_DIM, :], preferred_element_type=F32)
         + jnp.dot(attn.astype(BF16), wo_ref[CONV_DIM:2 * CONV_DIM, :], preferred_element_type=F32))
    y_ref[...] = y + x_ref[...]

    keep = WINDOW - t_len
    ks_ref[:, 0:keep, :] = ck_ref[:, t_len:WINDOW, :]
    ks_ref[:, keep:WINDOW, :] = kr.reshape(bt, t_len, LANES)
    vs_ref[:, 0:keep, :] = cv_ref[:, t_len:WINDOW, :]
    vs_ref[:, keep:WINDOW, :] = v.reshape(bt, t_len, LANES)


def _ab_sample(z, x, row0, n_seq, t_len, cos, sin, cw, qg, kg, sinks, seg, wo, cc, ck, cv, bt=16):
    rows = bt * t_len
    blk0 = row0 // rows
    tok = lambda i, s: (blk0 + i, 0)
    const = lambda i, s: (0, 0)
    seq3 = lambda i, s: (i, 0, 0)
    grid_spec = pltpu.PrefetchScalarGridSpec(
        num_scalar_prefetch=1,
        grid=(n_seq // bt,),
        in_specs=[pl.BlockSpec((rows, AB_IN), tok),
                  pl.BlockSpec((rows, D_MODEL), tok),
                  pl.BlockSpec((rows, LANES), const),
                  pl.BlockSpec((rows, LANES), const),
                  pl.BlockSpec((3, CONV_DIM), const),
                  pl.BlockSpec((1, LANES), const),
                  pl.BlockSpec((1, LANES), const),
                  pl.BlockSpec((LANES, LANES), const),
                  pl.BlockSpec((D_MODEL, D_MODEL), const),
                  pl.BlockSpec((bt, 2, CONV_DIM), seq3),
                  pl.BlockSpec((bt, WINDOW, LANES), seq3),
                  pl.BlockSpec((bt, WINDOW, LANES), seq3)],
        out_specs=[pl.BlockSpec((rows, D_MODEL), lambda i, s: (i, 0)),
                   pl.BlockSpec((bt, 2, CONV_DIM), seq3),
                   pl.BlockSpec((bt, WINDOW, LANES), seq3),
                   pl.BlockSpec((bt, WINDOW, LANES), seq3)])
    return pl.pallas_call(
        _ab_sample_kernel,
        grid_spec=grid_spec,
        out_shape=[jax.ShapeDtypeStruct((n_seq * t_len, D_MODEL), F32),
                   jax.ShapeDtypeStruct((n_seq, 2, CONV_DIM), F32),
                   jax.ShapeDtypeStruct((n_seq, WINDOW, LANES), F32),
                   jax.ShapeDtypeStruct((n_seq, WINDOW, LANES), F32)],
        compiler_params=_cparams(1),
        name="ab_sample",
    )(sinks, z, x, cos, sin, cw, qg, kg, seg, wo, cc, ck, cv)


def _log_sigmoid(x):
    return jnp.minimum(x, 0.0) - jnp.log(1.0 + jnp.exp(-jnp.abs(x)))


def _mlstm_chunk(z, x, bias, og, tril, wo, c_src, n_src, m_src, c_dst, n_dst, m_dst, n_real):
    L = z.shape[0]
    gates = z[:, ML_GATE_COL:ML_GATE_COL + LANES] + bias
    if n_real < L:
        live = lax.broadcasted_iota(jnp.int32, (L, LANES), 0) < n_real
        li_all = jnp.where(live, gates, -1e30)
        lf_all = jnp.where(live, _log_sigmoid(gates), 0.0)
    else:
        li_all = gates
        lf_all = _log_sigmoid(gates)
    lf_pieces = _bf16_pieces(lf_all, 3)
    f_col_all = sum(jnp.dot(tril, p, preferred_element_type=F32) for p in lf_pieces)
    f_row_all = sum(lax.dot_general(p, tril, (((0,), (1,)), ((), ())), preferred_element_type=F32)
                    for p in lf_pieces)
    li_t = li_all.T
    rr = lax.broadcasted_iota(jnp.int32, (L, L), 0)
    cc = lax.broadcasted_iota(jnp.int32, (L, L), 1)
    causal = cc <= rr

    outs, m_new_all = [], []
    for h in range(ML_HEADS):
        f_col = f_col_all[:, ML_HEADS + h:ML_HEADS + h + 1]
        f_row = f_row_all[ML_HEADS + h:ML_HEADS + h + 1, :]
        li_row = li_t[h:h + 1, :]
        li_col = li_all[:, h:h + 1]
        m0 = m_src[0:1, h:h + 1]
        c0 = c_src[h]
        n0 = n_src[h:h + 1, :]
        qh = z[:, h * ML_QK:(h + 1) * ML_QK]
        kh = z[:, ML_HEADS * ML_QK + h * ML_QK:ML_HEADS * ML_QK + (h + 1) * ML_QK] * (ML_QK ** -0.5)
        v_off = 2 * ML_HEADS * ML_QK
        vh = z[:, v_off + h * ML_V:v_off + (h + 1) * ML_V]
        o_off = v_off + ML_HEADS * ML_V
        oh = z[:, o_off + h * ML_V:o_off + (h + 1) * ML_V]
        qb = qh.astype(BF16)
        vb = vh.astype(BF16)

        dmat = jnp.where(causal, f_col - f_row + li_row, -jnp.inf)
        gcar = f_col + m0
        m_t = jnp.maximum(jnp.max(dmat, axis=-1, keepdims=True), gcar)
        w = jnp.exp(dmat - m_t)
        s = lax.dot_general(qb, kh.astype(BF16), NT_DIMS, preferred_element_type=F32) * w
        carry = jnp.exp(gcar - m_t)
        num = (jnp.dot(s.astype(BF16), vb, preferred_element_type=F32)
               + jnp.dot(qb, c0.astype(BF16), preferred_element_type=F32) * carry)
        den = jnp.sum(s, axis=-1, keepdims=True) + carry * jnp.sum(qh * n0, axis=-1, keepdims=True)
        hout = num / jnp.maximum(jnp.abs(den), jnp.exp(-m_t))

        f_last = f_col[L - 1:L, :]
        w_end = f_last - f_col + li_col
        m_new = jnp.maximum(f_last + m0, jnp.max(w_end, axis=0, keepdims=True))
        a_end = jnp.exp(w_end - m_new)
        scale = jnp.exp(f_last + m0 - m_new)
        ka = kh * a_end
        c_dst[h] = scale * c0 + lax.dot_general(ka.astype(BF16), vb, TN_DIMS, preferred_element_type=F32)
        n_dst[h:h + 1, :] = scale * n0 + jnp.sum(ka, axis=0, keepdims=True)
        m_new_all.append(m_new)

        hn = _rmsnorm(hout, og[:, h * ML_V:(h + 1) * ML_V])
        outs.append(jax.nn.sigmoid(oh) * hn)
    m_dst[...] = jnp.concatenate(m_new_all, axis=1)
    out = jnp.concatenate(outs, axis=1)
    return jnp.dot(out.astype(BF16), wo, preferred_element_type=F32) + x


def _mlstm_prompt_kernel(*refs, n_batch):
    z_refs, x_refs = refs[0:n_batch], refs[n_batch:2 * n_batch]
    bias_ref, og_ref, tril_ref, wo_ref, y_ref, c_ref, n_ref, m_ref = refs[2 * n_batch:]

    @pl.when(pl.program_id(0) == 0)
    def _():
        c_ref[...] = jnp.zeros_like(c_ref)
        n_ref[...] = jnp.zeros_like(n_ref)
        m_ref[...] = jnp.zeros_like(m_ref)

    for b in range(n_batch):
        state = (c_ref.at[b], n_ref.at[b], m_ref.at[b])
        y_ref[b] = _mlstm_chunk(z_refs[b][...], x_refs[b][...], bias_ref[...], og_ref[...], tril_ref[...],
                                wo_ref[...], *state, *state, ML_CHUNK)


def _mlstm_sample_kernel(z_ref, x_ref, bias_ref, og_ref, tril_ref, wo_ref, c0_ref, n0_ref, m0_ref,
                         y_ref, c_ref, n_ref, m_ref, *, t_len):
    n_here = z_ref.shape[0] // t_len
    for s in range(n_here):
        rows = slice(s * t_len, (s + 1) * t_len)
        zpad = jnp.concatenate([z_ref[rows, :], jnp.zeros((ML_CHUNK - t_len, ML_IN_PAD), F32)], axis=0)
        xpad = jnp.concatenate([x_ref[rows, :], jnp.zeros((ML_CHUNK - t_len, D_MODEL), F32)], axis=0)
        y = _mlstm_chunk(zpad, xpad, bias_ref[...], og_ref[...], tril_ref[...], wo_ref[...],
                         c0_ref.at[s], n0_ref.at[s], m0_ref.at[s], c_ref.at[s], n_ref.at[s], m_ref.at[s], t_len)
        y_ref[rows, :] = y[0:t_len, :]


def _mlstm_weight_specs(const):
    return [pl.BlockSpec((1, LANES), const),
            pl.BlockSpec((1, D_MODEL), const),
            pl.BlockSpec((ML_CHUNK, ML_CHUNK), const),
            pl.BlockSpec((D_MODEL, D_MODEL), const)]


def _mlstm_prompt(z, x, n_batch, seq, bias, og, tril, wo):
    nc = seq // ML_CHUNK
    const = lambda j: (0, 0)
    tok_specs = lambda width: [pl.BlockSpec((ML_CHUNK, width), functools.partial(lambda j, b: (b * nc + j, 0), b=b))
                               for b in range(n_batch)]
    return pl.pallas_call(
        functools.partial(_mlstm_prompt_kernel, n_batch=n_batch),
        grid=(nc,),
        in_specs=tok_specs(ML_IN_PAD) + tok_specs(D_MODEL) + _mlstm_weight_specs(const),
        out_specs=[pl.BlockSpec((n_batch, ML_CHUNK, D_MODEL), lambda j: (0, j, 0)),
                   pl.BlockSpec((n_batch, ML_HEADS, ML_QK, ML_V), lambda j: (0, 0, 0, 0)),
                   pl.BlockSpec((n_batch, ML_HEADS, ML_QK), lambda j: (0, 0, 0)),
                   pl.BlockSpec((n_batch, 1, ML_HEADS), lambda j: (0, 0, 0))],
        out_shape=[jax.ShapeDtypeStruct((n_batch, seq, D_MODEL), F32),
                   jax.ShapeDtypeStruct((n_batch, ML_HEADS, ML_QK, ML_V), F32),
                   jax.ShapeDtypeStruct((n_batch, ML_HEADS, ML_QK), F32),
                   jax.ShapeDtypeStruct((n_batch, 1, ML_HEADS), F32)],
        compiler_params=_cparams(1),
        name="mlstm_prompt",
    )(*([z] * n_batch), *([x] * n_batch), bias, og, tril, wo)


def _mlstm_sample(z, x, row0, n_seq, t_len, bias, og, tril, wo, c0, n0, m0, seqs_per_step=4):
    rows = seqs_per_step * t_len
    blk0 = row0 // rows
    tok = lambda i: (blk0 + i, 0)
    const = lambda i: (0, 0)
    st4 = lambda i: (i, 0, 0, 0)
    st3 = lambda i: (i, 0, 0)
    state_specs = [pl.BlockSpec((seqs_per_step, ML_HEADS, ML_QK, ML_V), st4),
                   pl.BlockSpec((seqs_per_step, ML_HEADS, ML_QK), st3),
                   pl.BlockSpec((seqs_per_step, 1, ML_HEADS), st3)]
    return pl.pallas_call(
        functools.partial(_mlstm_sample_kernel, t_len=t_len),
        grid=(n_seq // seqs_per_step,),
        in_specs=[pl.BlockSpec((rows, ML_IN_PAD), tok),
                  pl.BlockSpec((rows, D_MODEL), tok)] + _mlstm_weight_specs(const) + state_specs,
        out_specs=[pl.BlockSpec((rows, D_MODEL), lambda i: (i, 0))] + state_specs,
        out_shape=[jax.ShapeDtypeStruct((n_seq * t_len, D_MODEL), F32),
                   jax.ShapeDtypeStruct((n_seq, ML_HEADS, ML_QK, ML_V), F32),
                   jax.ShapeDtypeStruct((n_seq, ML_HEADS, ML_QK), F32),
                   jax.ShapeDtypeStruct((n_seq, 1, ML_HEADS), F32)],
        compiler_params=_cparams(1),
        name="mlstm_sample",
    )(z, x, bias, og, tril, wo, c0, n0, m0)


def _gelu(x):
    return 0.5 * x * (1.0 + lax.erf(x * 0.7071067811865476))


SEL_T = 256


def _unit_scores(q_scr, sk_ref, half, h):
    return [lax.dot_general(sk_ref[h, p], q_scr[half, 2 * h + p].astype(BF16), NT_DIMS,
                            preferred_element_type=F32) for p in range(2)]


class _TopK:
    def __init__(self, s, payload=None):
        self.s = s
        self.payload = payload
        self.rows = lax.broadcasted_iota(jnp.int32, s.shape, 0).astype(F32)
        self.vals, self.picks = [], []

    def step(self, n):
        bound = float(self.s.shape[0])
        for _ in range(n):
            m = jnp.max(self.s, axis=0, keepdims=True)
            first = jnp.min(jnp.where(self.s == m, self.rows, bound), axis=0, keepdims=True)
            sel = self.rows == first
            self.vals.append(m)
            if self.payload is None:
                self.picks.append(first)
            else:
                self.picks.append(jnp.max(jnp.where(sel, self.payload, -1.0), axis=0, keepdims=True))
            self.s = jnp.where(sel, -jnp.inf, self.s)

    def result(self):
        return jnp.concatenate(self.vals, axis=0), jnp.concatenate(self.picks, axis=0)


def _pair_candidates(first, second):
    def pairs(a, b, combine):
        h8 = SUBLANES
        rows = [combine(a[0:1], b)]
        rows += [combine(a[k1:k1 + 1], b[0:h8]) for k1 in range(1, h8)]
        rows.append(combine(a[h8:PEER_TOPK], b[0:1]))
        return jnp.concatenate(rows, axis=0)

    (v1, i1), (v2, i2) = first, second
    return pairs(v1, v2, lambda a, b: a + b), pairs(i1, i2, lambda a, b: a * N_KEYS + b)


def _peer_fused_kernel(xs_ref, xm_ref, g_ref, wq_ref, sk_ref, u_ref, v_ref, o_ref,
                       q_scr, sel_e, sel_g, xn_scr, a_scr, bgt_scr, bg_scr, gmat, coef_scr):
    i = pl.program_id(0)
    e = pl.program_id(1)
    n_steps = pl.num_programs(1)
    t = xm_ref.shape[0]
    n_half = t // SEL_T
    half_keys = N_KEYS // 2
    cur = i % 2

    def project_queries():
        xn = _rmsnorm(xs_ref[...], g_ref[...]).astype(BF16)
        q = jnp.dot(xn, wq_ref[...], preferred_element_type=F32)
        for hf in range(n_half):
            for c in range(2 * PEER_HEADS):
                q_scr[hf, c] = q[hf * SEL_T:(hf + 1) * SEL_T, c * LANES:(c + 1) * LANES]

    @pl.when(e == 0)
    def _():
        @pl.when(i == 0)
        def _():
            project_queries()
            sel_e[...] = jnp.zeros_like(sel_e)
            sel_g[...] = jnp.zeros_like(sel_g)

        x = xm_ref[...]
        xn_scr[...] = _rmsnorm(x, g_ref[...]).astype(BF16)
        o_ref[...] = x
        coef_scr[0] = jnp.zeros(coef_scr.shape[1:], BF16)
        prev = 1 - cur
        lane_groups = SEL_T // LANES
        for hf in range(n_half):
            ef = sel_e[prev, hf]
            af = jnp.floor(ef * (1.0 / N_KEYS))
            bf = ef - af * N_KEYS
            gf = sel_g[prev, hf]
            a_scr[hf * SEL_T:(hf + 1) * SEL_T, :] = af.T
            bgf = bf + 0.5 * gf
            bgt_scr[hf * SEL_T:(hf + 1) * SEL_T, :] = bgf.T
            for lg in range(lane_groups):
                bg_scr[hf * lane_groups + lg] = bgf[:, lg * LANES:(lg + 1) * LANES]
        r = lax.broadcasted_iota(jnp.int32, (N_KEYS, LANES), 0)
        packed_shape = (N_KEYS // (2 * SUBLANES), 2 * SUBLANES, LANES)
        a_of_row = jnp.where(r < half_keys, 2 * r, 2 * (r - half_keys) + 1).astype(F32).astype(BF16)
        a_of_row = a_of_row.reshape(packed_shape)
        one = jnp.ones(packed_shape, BF16)
        zero = jnp.zeros(packed_shape, BF16)
        b_of_lane = lax.broadcasted_iota(jnp.int32, (N_KEYS, LANES), 1).astype(F32)
        b_of_row = r.astype(F32).astype(BF16).reshape(packed_shape)
        per_body = 32
        bodies_per_group = LANES // per_body
        row_major_every = 3

        def body(it, carry):
            grp = lax.shift_right_logical(it, bodies_per_group.bit_length() - 1)
            sub = it & (bodies_per_group - 1)
            shift = (LANES - sub * per_body) & (LANES - 1)
            bg = pltpu.roll(bg_scr[grp], shift, 1)
            tok0 = pl.multiple_of(it * per_body, per_body)
            arows = a_scr[pl.ds(tok0, per_body), :]
            bgrows = bgt_scr[pl.ds(tok0, per_body), :]
            for k in range(per_body):
                arow = jnp.broadcast_to(arows[k:k + 1], (2 * SUBLANES, LANES)).astype(BF16)[None]
                pa = jnp.where(a_of_row == arow, one, zero).reshape(N_KEYS, LANES)
                if k % row_major_every == 0:
                    bgrow = jnp.broadcast_to(bgrows[k:k + 1], (2 * SUBLANES, LANES))
                    brow = jnp.floor(bgrow)
                    grow = (2.0 * (bgrow - brow)).astype(BF16)[None]
                    qb = jnp.where(b_of_row == brow.astype(BF16)[None], grow, zero).reshape(N_KEYS, LANES)
                    tile = lax.dot_general(pa, qb, NT_DIMS, preferred_element_type=F32)
                else:
                    bgcol = jnp.broadcast_to(bg[:, k:k + 1], (N_KEYS, LANES))
                    bcol = jnp.floor(bgcol)
                    gcol = 2.0 * (bgcol - bcol)
                    qbt = jnp.where(bcol == b_of_lane, gcol, 0.0).astype(BF16)
                    tile = jnp.dot(pa, qbt, preferred_element_type=F32)
                row0 = pl.multiple_of((tok0 + k) * G_PITCH, SUBLANES)
                gmat[pl.ds(row0, half_keys), :] = pltpu.pack_elementwise(
                    [tile[0:half_keys], tile[half_keys:N_KEYS]], packed_dtype=BF16)
            return carry

        lax.fori_loop(0, t // per_body, body, 0)

    unit = jnp.minimum(e, n_half * PEER_HEADS - 1)
    half = lax.shift_right_logical(unit, PEER_HEADS.bit_length() - 1)
    h = unit & (PEER_HEADS - 1)
    first_level = [_TopK(st) for st in _unit_scores(q_scr, sk_ref, half, h)]

    rd = e & 1
    o_ref[...] += jnp.dot(coef_scr[rd], v_ref[...], preferred_element_type=F32)
    act = lax.dot_general(xn_scr[...], u_ref[...], NT_DIMS, preferred_element_type=F32)
    blk = jnp.minimum(e, n_steps - 2)
    words_per_step = u_ref.shape[0] // (2 * N_KEYS)
    gates = []
    for w in range(words_per_step):
        word = gmat[pl.ds(blk * words_per_step + w, t, stride=G_PITCH), :]
        gates.append(lax.bitcast_convert_type(word << 16, F32))
        gates.append(lax.bitcast_convert_type(word & jnp.int32(-65536), F32))
    coef_scr[1 - rd] = (_gelu(act) * jnp.concatenate(gates, axis=1)).astype(BF16)

    for lst in first_level:
        lst.step(PEER_TOPK)
    cand, expert = _pair_candidates(first_level[0].result(), first_level[1].result())
    second_level = _TopK(cand, payload=expert)
    second_level.step(PEER_TOPK)
    best, e_sel = second_level.result()
    ex = jnp.exp(best - best[0:1])
    slot0 = pl.multiple_of(h * PEER_TOPK, PEER_TOPK)
    sel_e[cur, half, pl.ds(slot0, PEER_TOPK), :] = e_sel
    sel_g[cur, half, pl.ds(slot0, PEER_TOPK), :] = ex / jnp.sum(ex, axis=0, keepdims=True)

    @pl.when(e == n_steps - 1)
    def _():
        project_queries()


def _peer(x, g, wq, sk, u, v, t=512, eb=1024):
    n, d = x.shape
    nslot = PEER_HEADS * PEER_TOPK
    n_tok_blk = n // t
    n_exp_blk = N_KEYS * N_KEYS // eb
    n_half = t // SEL_T
    assert n_exp_blk == n_half * PEER_HEADS
    last_tok = n_tok_blk - 1
    once = pl.Buffered(1)
    return pl.pallas_call(
        _peer_fused_kernel,
        grid=(n_tok_blk + 1, n_exp_blk + 1),
        in_specs=[pl.BlockSpec((t, d), lambda i, e: (jnp.minimum(i + e // n_exp_blk, last_tok), 0),
                               pipeline_mode=once),
                  pl.BlockSpec((t, d), lambda i, e: (jnp.maximum(i - 1, 0), 0)),
                  pl.BlockSpec((1, d), lambda i, e: (0, 0)),
                  pl.BlockSpec((d, 2 * PEER_HEADS * LANES), lambda i, e: (0, 0), pipeline_mode=once),
                  pl.BlockSpec((PEER_HEADS, 2, N_KEYS, LANES), lambda i, e: (0, 0, 0, 0), pipeline_mode=once),
                  pl.BlockSpec((eb, d), lambda i, e: (jnp.minimum(e, n_exp_blk - 1), 0)),
                  pl.BlockSpec((eb, d), lambda i, e: (jnp.maximum(e - 1, 0), 0))],
        out_specs=pl.BlockSpec((t, d), lambda i, e: (jnp.maximum(i - 1, 0), 0)),
        out_shape=jax.ShapeDtypeStruct((n, d), F32),
        scratch_shapes=[pltpu.VMEM((n_half, 2 * PEER_HEADS, SEL_T, LANES), F32),
                        pltpu.VMEM((2, n_half, nslot, SEL_T), F32),
                        pltpu.VMEM((2, n_half, nslot, SEL_T), F32),
                        pltpu.VMEM((t, d), BF16),
                        pltpu.VMEM((t, nslot), F32),
                        pltpu.VMEM((t, nslot), F32),
                        pltpu.VMEM((t // LANES, nslot, LANES), F32),
                        pltpu.VMEM((t * G_PITCH, LANES), jnp.int32),
                        pltpu.VMEM((2, t, eb), BF16)],
        compiler_params=_cparams(2),
        name="peer",
    )(x, x, g.reshape(1, d), wq, sk, u, v)


def _rope_tables(pos):
    half = HEAD_DIM // 2
    inv_freq = ROPE_THETA ** (-jnp.arange(half, dtype=F32) / half)
    ang = pos.astype(F32)[:, None] * inv_freq[None, :]
    cos, sin = jnp.cos(ang), jnp.sin(ang)
    reps = LANES // HEAD_DIM
    cos_t = jnp.tile(jnp.concatenate([cos, cos], axis=1), (1, reps))
    sin_t = jnp.tile(jnp.concatenate([-sin, sin], axis=1), (1, reps))
    return cos_t, sin_t


def kernel(x_prompt, x_sample, cache_conv, cache_win_k, cache_win_v, state_mlstm_C, state_mlstm_n,
           state_mlstm_m, norm_mix, norm_ffn, ab_w_in, ab_conv_w, ab_q_gain, ab_k_gain, ab_sinks, ab_w_out,
           ml_w_in, ml_gate_bias, ml_out_gain, ml_w_out, peer_w_q, peer_sub_keys, peer_u, peer_v):
    n_batch, seq, d = x_prompt.shape
    n_seq, t_len, _ = x_sample.shape
    n_prompt = n_batch * seq
    assert d == D_MODEL and t_len == SUBLANES and norm_mix.shape[0] == 2

    x = jnp.concatenate([x_prompt.reshape(n_prompt, d), x_sample.reshape(n_seq * t_len, d)], axis=0)

    cos_p, sin_p = _rope_tables(jnp.arange(seq, dtype=jnp.int32))
    cos_s, sin_s = _rope_tables(PAST_LEN + jnp.arange(t_len, dtype=jnp.int32))
    bt = 16
    cos_s, sin_s = jnp.tile(cos_s, (bt, 1)), jnp.tile(sin_s, (bt, 1))
    lane = jnp.arange(LANES)
    seg = (lane[:, None] // HEAD_DIM == lane[None, :] // HEAD_DIM).astype(BF16)
    reps = LANES // HEAD_DIM
    qg = jnp.tile(ab_q_gain[0], reps).reshape(1, LANES)
    kg = jnp.tile(ab_k_gain[0], reps).reshape(1, LANES)
    wo_ab = ab_w_out[0].astype(BF16)

    z = _norm_proj(x, norm_mix[0], ab_w_in[0].astype(BF16))
    y_p, k_p, v_p, c_p = _ab_prompt(z, x, n_batch, seq, cos_p, sin_p, ab_conv_w[0], qg, kg, ab_sinks[0],
                                    seg, wo_ab)
    y_s, c_s, k_s, v_s = _ab_sample(z, x, n_prompt, n_seq, t_len, cos_s, sin_s, ab_conv_w[0], qg, kg,
                                    ab_sinks[0], seg, wo_ab, cache_conv[0],
                                    cache_win_k[0].reshape(n_seq, WINDOW, LANES),
                                    cache_win_v[0].reshape(n_seq, WINDOW, LANES), bt=bt)
    x = jnp.concatenate([y_p.reshape(n_prompt, d), y_s], axis=0)
    x = _peer(x, norm_ffn[0], peer_w_q[0].astype(BF16), peer_sub_keys[0].astype(BF16),
              peer_u[0].astype(BF16), peer_v[0].astype(BF16))

    n_gate = 2 * ML_HEADS
    w_in = jnp.pad(ml_w_in[0], ((0, 0), (0, ML_IN_PAD - ml_w_in.shape[2]))).astype(BF16)
    bias = jnp.pad(ml_gate_bias[0], (0, LANES - n_gate)).reshape(1, LANES)
    og = ml_out_gain[0].reshape(1, D_MODEL)
    idx = jnp.arange(ML_CHUNK)
    tril = (idx[None, :] <= idx[:, None]).astype(BF16)
    wo_ml = ml_w_out[0].astype(BF16)

    z = _norm_proj(x, norm_mix[1], w_in)
    y_p, cm_p, nm_p, mm_p = _mlstm_prompt(z, x, n_batch, seq, bias, og, tril, wo_ml)
    y_s, cm_s, nm_s, mm_s = _mlstm_sample(z, x, n_prompt, n_seq, t_len, bias, og, tril, wo_ml,
                                          state_mlstm_C[0], state_mlstm_n[0],
                                          state_mlstm_m[0].reshape(n_seq, 1, ML_HEADS))
    x = jnp.concatenate([y_p.reshape(n_prompt, d), y_s], axis=0)
    x = _peer(x, norm_ffn[1], peer_w_q[1].astype(BF16), peer_sub_keys[1].astype(BF16),
              peer_u[1].astype(BF16), peer_v[1].astype(BF16))

    y_prompt = x[:n_prompt].reshape(n_batch, seq, d)
    y_sample = x[n_prompt:].reshape(n_seq, t_len, d)
    kv_shape_p = (1, n_batch, WINDOW, N_KV_HEADS, HEAD_DIM)
    kv_shape_s = (1, n_seq, WINDOW, N_KV_HEADS, HEAD_DIM)
    return (y_prompt, y_sample,
            c_p[:, SUBLANES - 2:, :][None], k_p.reshape(kv_shape_p), v_p.reshape(kv_shape_p),
            cm_p[None], nm_p[None], mm_p.reshape(1, n_batch, ML_HEADS),
            c_s[None], k_s.reshape(kv_shape_s), v_s.reshape(kv_shape_s),
            cm_s[None], nm_s[None], mm_s.reshape(1, n_seq, ML_HEADS))
```

```python
import functools

import jax
import jax.numpy as jnp
from jax import lax
from jax.experimental import pallas as pl
from jax.experimental.pallas import tpu as pltpu

F32 = jnp.float32
BF16 = jnp.bfloat16
EPS = 1e-6

D_MODEL = 1024
CONV_DIM = 512
N_Q_HEADS = 8
N_KV_HEADS = 2
HEAD_DIM = 64
WINDOW = 128
ROPE_THETA = 10000.0
AB_IN = 2304
ML_HEADS = 4
ML_QK = 128
ML_V = 256
ML_CHUNK = 128
ML_GATE_COL = 3072
ML_IN_PAD = ML_GATE_COL + 128
N_KEYS = 128
PEER_HEADS = 8
PEER_TOPK = 16
PAST_LEN = 16384

LANES = 128
SUBLANES = 8
G_PITCH = N_KEYS // 2 + SUBLANES
VMEM_LIMIT = 56 * 1024 * 1024

NT_DIMS = (((1,), (1,)), ((), ()))
TN_DIMS = (((0,), (0,)), ((), ()))


def _cparams(n_axes, vmem=VMEM_LIMIT):
    return pltpu.CompilerParams(dimension_semantics=("arbitrary",) * n_axes, vmem_limit_bytes=vmem)


def _rmsnorm(x, g):
    return x * lax.rsqrt(jnp.mean(x * x, axis=-1, keepdims=True) + EPS) * g


def _bf16_pieces(a, terms):
    pieces = []
    rem = a
    for _ in range(terms):
        piece = rem.astype(BF16)
        rem = rem - piece.astype(F32)
        pieces.append(piece)
    return pieces


def _split_dot(a, b_bf16, terms=2):
    return sum(jnp.dot(p, b_bf16, preferred_element_type=F32) for p in _bf16_pieces(a, terms))


def _norm_proj_kernel(x_ref, g_ref, w_ref, o_ref):
    r = _rmsnorm(x_ref[...], g_ref[...])
    o_ref[...] = jnp.dot(r.astype(BF16), w_ref[...], preferred_element_type=F32)


def _norm_proj(x, g, w_bf16, tm=512):
    n, d = x.shape
    nout = w_bf16.shape[1]
    return pl.pallas_call(
        _norm_proj_kernel,
        grid=(n // tm,),
        in_specs=[pl.BlockSpec((tm, d), lambda i: (i, 0)),
                  pl.BlockSpec((1, d), lambda i: (0, 0)),
                  pl.BlockSpec((d, nout), lambda i: (0, 0))],
        out_specs=pl.BlockSpec((tm, nout), lambda i: (i, 0)),
        out_shape=jax.ShapeDtypeStruct((n, nout), F32),
        compiler_params=_cparams(1),
        name="norm_proj",
    )(x, g.reshape(1, d), w_bf16)


def _headnorm_rope(xc, gain, cos, sin, seg, hi_half):
    ss = _split_dot(xc * xc, seg)
    xn = xc * lax.rsqrt(ss * (1.0 / HEAD_DIM) + EPS) * gain
    partner = jnp.where(hi_half, pltpu.roll(xn, 32, 1), pltpu.roll(xn, 96, 1))
    return xn * cos + partner * sin


def _softmax_sink(s, mask, sink):
    s = jnp.where(mask, s, -1e30)
    m = jnp.maximum(jnp.max(s, axis=-1, keepdims=True), sink)
    p = jnp.exp(s - m)
    denom = jnp.sum(p, axis=-1, keepdims=True) + jnp.exp(sink - m)
    return (p / denom).astype(BF16)


def _ab_prompt_kernel(sink_ref, *refs, n_batch):
    z_refs, x_refs = refs[0:n_batch], refs[n_batch:2 * n_batch]
    (cos_ref, sin_ref, cw_ref, qg_ref, kg_ref, seg_ref, wo_ref,
     y_ref, kst_ref, vst_ref, cst_ref, pk_ref, pv_ref, pu_ref) = refs[2 * n_batch:]
    j = pl.program_id(0)

    @pl.when(j == 0)
    def _():
        pk_ref[...] = jnp.zeros_like(pk_ref)
        pv_ref[...] = jnp.zeros_like(pv_ref)
        pu_ref[...] = jnp.zeros_like(pu_ref)

    for b in range(n_batch):
        _ab_prompt_block(j, sink_ref, z_refs[b], x_refs[b], cos_ref, sin_ref, cw_ref, qg_ref, kg_ref, seg_ref,
                         wo_ref, y_ref.at[b], kst_ref.at[b], vst_ref.at[b], cst_ref.at[b],
                         pk_ref.at[b], pv_ref.at[b], pu_ref.at[b])


def _ab_prompt_block(j, sink_ref, z_ref, x_ref, cos_ref, sin_ref, cw_ref, qg_ref, kg_ref, seg_ref, wo_ref,
                     y_ref, kst_ref, vst_ref, cst_ref, pk_ref, pv_ref, pu_ref):
    blk = z_ref.shape[0]
    cos = cos_ref[...]
    sin = sin_ref[...]
    seg = seg_ref[...]
    hi_half = (lax.broadcasted_iota(jnp.int32, (blk, LANES), 1) & 32) != 0

    gate_b = z_ref[:, 0:CONV_DIM]
    u = z_ref[:, CONV_DIM:2 * CONV_DIM] * z_ref[:, 2 * CONV_DIM:3 * CONV_DIM]
    ng = blk // SUBLANES
    u3 = u.reshape(ng, SUBLANES, CONV_DIM)
    ext = jnp.concatenate([pu_ref[...][None], u3], axis=0)
    t8 = lax.broadcasted_iota(jnp.int32, (ng, SUBLANES, CONV_DIM), 1)
    r1 = pltpu.roll(ext, 1, 1)
    r2 = pltpu.roll(ext, 2, 1)
    um1 = jnp.where(t8 >= 1, r1[1:], r1[:-1])
    um2 = jnp.where(t8 >= 2, r2[1:], r2[:-1])
    cw = cw_ref[...]
    conv = cw[0:1][None] * um2 + cw[1:2][None] * um1 + cw[2:3][None] * u3
    yconv = gate_b * conv.reshape(blk, CONV_DIM)

    q0 = 3 * CONV_DIM
    k0 = q0 + N_Q_HEADS * HEAD_DIM
    v0 = k0 + N_KV_HEADS * HEAD_DIM
    qg = qg_ref[...]
    qr = [_headnorm_rope(z_ref[:, q0 + c * LANES:q0 + (c + 1) * LANES], qg, cos, sin, seg, hi_half)
          for c in range(N_Q_HEADS * HEAD_DIM // LANES)]
    kr = _headnorm_rope(z_ref[:, k0:k0 + LANES], kg_ref[...], cos, sin, seg, hi_half)
    v = z_ref[:, v0:v0 + LANES]
    pk = pk_ref[...]
    pv = pv_ref[...]

    row = lax.broadcasted_iota(jnp.int32, (blk, 2 * blk), 0)
    col = lax.broadcasted_iota(jnp.int32, (blk, 2 * blk), 1)
    row_prev = row + jnp.where(j == 0, 2 * blk, 0)
    mask = ((col < blk) & (col > row_prev)) | ((col >= blk) & (col - blk <= row))

    kks, vvs = [], []
    for g in range(N_KV_HEADS):
        sl = slice(g * HEAD_DIM, (g + 1) * HEAD_DIM)
        kks.append(jnp.concatenate([pk[:, sl], kr[:, sl]], axis=0).astype(BF16))
        vvs.append(jnp.concatenate([pv[:, sl], v[:, sl]], axis=0).astype(BF16))
    outs = []
    for h in range(N_Q_HEADS):
        g = h // (N_Q_HEADS // N_KV_HEADS)
        qh = qr[h // 2][:, (h % 2) * HEAD_DIM:(h % 2 + 1) * HEAD_DIM].astype(BF16)
        s = lax.dot_general(qh, kks[g], NT_DIMS, preferred_element_type=F32) * (HEAD_DIM ** -0.5)
        p = _softmax_sink(s, mask, sink_ref[h])
        outs.append(jnp.dot(p, vvs[g], preferred_element_type=F32))
    attn = jnp.concatenate(outs, axis=1)

    y = (jnp.dot(yconv.astype(BF16), wo_ref[0:CONV_DIM, :], preferred_element_type=F32)
         + jnp.dot(attn.astype(BF16), wo_ref[CONV_DIM:2 * CONV_DIM, :], preferred_element_type=F32))
    y_ref[...] = y + x_ref[...]

    pk_ref[...] = kr
    pv_ref[...] = v
    pu_ref[...] = u3[ng - 1]
    kst_ref[...] = kr
    vst_ref[...] = v
    cst_ref[...] = u3[ng - 1]


def _ab_prompt(z, x, n_batch, seq, cos, sin, cw, qg, kg, sinks, seg, wo):
    blk = WINDOW
    nb = seq // blk
    const = lambda j, s: (0, 0)
    whole = lambda j, s: (0, 0, 0)
    tok_specs = lambda width: [pl.BlockSpec((blk, width), functools.partial(lambda j, s, b: (b * nb + j, 0), b=b))
                               for b in range(n_batch)]
    grid_spec = pltpu.PrefetchScalarGridSpec(
        num_scalar_prefetch=1,
        grid=(nb,),
        in_specs=tok_specs(AB_IN) + tok_specs(D_MODEL) + [
            pl.BlockSpec((blk, LANES), lambda j, s: (j, 0)),
            pl.BlockSpec((blk, LANES), lambda j, s: (j, 0)),
            pl.BlockSpec((3, CONV_DIM), const),
            pl.BlockSpec((1, LANES), const),
            pl.BlockSpec((1, LANES), const),
            pl.BlockSpec((LANES, LANES), const),
            pl.BlockSpec((D_MODEL, D_MODEL), const)],
        out_specs=[pl.BlockSpec((n_batch, blk, D_MODEL), lambda j, s: (0, j, 0)),
                   pl.BlockSpec((n_batch, blk, LANES), whole),
                   pl.BlockSpec((n_batch, blk, LANES), whole),
                   pl.BlockSpec((n_batch, SUBLANES, CONV_DIM), whole)],
        scratch_shapes=[pltpu.VMEM((n_batch, blk, LANES), F32), pltpu.VMEM((n_batch, blk, LANES), F32),
                        pltpu.VMEM((n_batch, SUBLANES, CONV_DIM), F32)])
    return pl.pallas_call(
        functools.partial(_ab_prompt_kernel, n_batch=n_batch),
        grid_spec=grid_spec,
        out_shape=[jax.ShapeDtypeStruct((n_batch, seq, D_MODEL), F32),
                   jax.ShapeDtypeStruct((n_batch, blk, LANES), F32),
                   jax.ShapeDtypeStruct((n_batch, blk, LANES), F32),
                   jax.ShapeDtypeStruct((n_batch, SUBLANES, CONV_DIM), F32)],
        compiler_params=_cparams(1),
        name="ab_prompt",
    )(sinks, *([z] * n_batch), *([x] * n_batch), cos, sin, cw, qg, kg, seg, wo)


def _ab_sample_kernel(sink_ref, z_ref, x_ref, cos_ref, sin_ref, cw_ref, qg_ref, kg_ref, seg_ref, wo_ref,
                      cc_ref, ck_ref, cv_ref, y_ref, cs_ref, ks_ref, vs_ref):
    rows = z_ref.shape[0]
    t_len = SUBLANES
    bt = rows // t_len
    cos = cos_ref[...]
    sin = sin_ref[...]
    seg = seg_ref[...]
    hi_half = (lax.broadcasted_iota(jnp.int32, (rows, LANES), 1) & 32) != 0

    gate_b = z_ref[:, 0:CONV_DIM]
    u = z_ref[:, CONV_DIM:2 * CONV_DIM] * z_ref[:, 2 * CONV_DIM:3 * CONV_DIM]
    u3 = u.reshape(bt, t_len, CONV_DIM)
    cc = cc_ref[...]
    c0 = cc[:, 0:1, :]
    c1 = cc[:, 1:2, :]
    t8 = lax.broadcasted_iota(jnp.int32, (bt, t_len, CONV_DIM), 1)
    r1 = pltpu.roll(u3, 1, 1)
    r2 = pltpu.roll(u3, 2, 1)
    um1 = jnp.where(t8 >= 1, r1, c1)
    um2 = jnp.where(t8 >= 2, r2, jnp.where(t8 == 1, c1, c0))
    cw = cw_ref[...]
    conv = cw[0:1][None] * um2 + cw[1:2][None] * um1 + cw[2:3][None] * u3
    yconv = gate_b * conv.reshape(rows, CONV_DIM)
    cs_ref[...] = r2[:, 0:2, :]

    q0 = 3 * CONV_DIM
    k0 = q0 + N_Q_HEADS * HEAD_DIM
    v0 = k0 + N_KV_HEADS * HEAD_DIM
    qg = qg_ref[...]
    qr = [_headnorm_rope(z_ref[:, q0 + c * LANES:q0 + (c + 1) * LANES], qg, cos, sin, seg, hi_half)
          for c in range(N_Q_HEADS * HEAD_DIM // LANES)]
    kr = _headnorm_rope(z_ref[:, k0:k0 + LANES], kg_ref[...], cos, sin, seg, hi_half)
    v = z_ref[:, v0:v0 + LANES]

    group = N_Q_HEADS // N_KV_HEADS
    nq = group * t_len
    nk = 2 * WINDOW
    qrow = lax.broadcasted_iota(jnp.int32, (nq, nk), 0)
    t_q = qrow & (t_len - 1)
    col = lax.broadcasted_iota(jnp.int32, (nq, nk), 1)
    mask = (((col < WINDOW) & (col > t_q)) | ((col >= WINDOW) & (col - WINDOW <= t_q)))[None]
    hrow = lax.broadcasted_iota(jnp.int32, (nq, 1), 0) // t_len
    pad = jnp.zeros((bt, nk - WINDOW - t_len, HEAD_DIM), F32)

    outs = [None] * N_Q_HEADS
    for g in range(N_KV_HEADS):
        sl = slice(g * HEAD_DIM, (g + 1) * HEAD_DIM)
        qs = jnp.concatenate(
            [qr[h // 2][:, (h % 2) * HEAD_DIM:(h % 2 + 1) * HEAD_DIM].reshape(bt, t_len, HEAD_DIM)
             for h in range(g * group, (g + 1) * group)], axis=1)
        kk = jnp.concatenate([ck_ref[:, :, sl], kr[:, sl].reshape(bt, t_len, HEAD_DIM), pad], axis=1)
        vv = jnp.concatenate([cv_ref[:, :, sl], v[:, sl].reshape(bt, t_len, HEAD_DIM), pad], axis=1)
        s = jnp.einsum('bqd,bkd->bqk', qs.astype(BF16), kk.astype(BF16),
                       preferred_element_type=F32) * (HEAD_DIM ** -0.5)
        sink = jnp.zeros((nq, 1), F32)
        for hh in range(group):
            sink = jnp.where(hrow == hh, sink_ref[g * group + hh], sink)
        p = _softmax_sink(s, mask, sink[None])
        o = jnp.einsum('bqk,bkd->bqd', p, vv.astype(BF16), preferred_element_type=F32)
        for hh in range(group):
            outs[g * group + hh] = o[:, hh * t_len:(hh + 1) * t_len, :].reshape(rows, HEAD_DIM)
    attn = jnp.concatenate(outs, axis=1)

    y = (jnp.dot(yconv.astype(BF16), wo_ref[0:CONV_DIM, :], preferred_element_type=F32)
         + jnp.dot(attn.astype(BF16), wo_ref[CONV_DIM:2 * CONV_DIM, :], preferred_element_type=F32))
    y_ref[...] = y + x_ref[...]

    keep = WINDOW - t_len
    ks_ref[:, 0:keep, :] = ck_ref[:, t_len:WINDOW, :]
    ks_ref[:, keep:WINDOW, :] = kr.reshape(bt, t_len, LANES)
    vs_ref[:, 0:keep, :] = cv_ref[:, t_len:WINDOW, :]
    vs_ref[:, keep:WINDOW, :] = v.reshape(bt, t_len, LANES)


def _ab_sample(z, x, row0, n_seq, t_len, cos, sin, cw, qg, kg, sinks, seg, wo, cc, ck, cv, bt=16):
    rows = bt * t_len
    blk0 = row0 // rows
    tok = lambda i, s: (blk0 + i, 0)
    const = lambda i, s: (0, 0)
    seq3 = lambda i, s: (i, 0, 0)
    grid_spec = pltpu.PrefetchScalarGridSpec(
        num_scalar_prefetch=1,
        grid=(n_seq // bt,),
        in_specs=[pl.BlockSpec((rows, AB_IN), tok),
                  pl.BlockSpec((rows, D_MODEL), tok),
                  pl.BlockSpec((rows, LANES), const),
                  pl.BlockSpec((rows, LANES), const),
                  pl.BlockSpec((3, CONV_DIM), const),
                  pl.BlockSpec((1, LANES), const),
                  pl.BlockSpec((1, LANES), const),
                  pl.BlockSpec((LANES, LANES), const),
                  pl.BlockSpec((D_MODEL, D_MODEL), const),
                  pl.BlockSpec((bt, 2, CONV_DIM), seq3),
                  pl.BlockSpec((bt, WINDOW, LANES), seq3),
                  pl.BlockSpec((bt, WINDOW, LANES), seq3)],
        out_specs=[pl.BlockSpec((rows, D_MODEL), lambda i, s: (i, 0)),
                   pl.BlockSpec((bt, 2, CONV_DIM), seq3),
                   pl.BlockSpec((bt, WINDOW, LANES), seq3),
                   pl.BlockSpec((bt, WINDOW, LANES), seq3)])
    return pl.pallas_call(
        _ab_sample_kernel,
        grid_spec=grid_spec,
        out_shape=[jax.ShapeDtypeStruct((n_seq * t_len, D_MODEL), F32),
                   jax.ShapeDtypeStruct((n_seq, 2, CONV_DIM), F32),
                   jax.ShapeDtypeStruct((n_seq, WINDOW, LANES), F32),
                   jax.ShapeDtypeStruct((n_seq, WINDOW, LANES), F32)],
        compiler_params=_cparams(1),
        name="ab_sample",
    )(sinks, z, x, cos, sin, cw, qg, kg, seg, wo, cc, ck, cv)


def _log_sigmoid(x):
    return jnp.minimum(x, 0.0) - jnp.log(1.0 + jnp.exp(-jnp.abs(x)))


def _mlstm_chunk(z, x, bias, og, tril, wo, c_src, n_src, m_src, c_dst, n_dst, m_dst, n_real):
    L = z.shape[0]
    gates = z[:, ML_GATE_COL:ML_GATE_COL + LANES] + bias
    if n_real < L:
        live = lax.broadcasted_iota(jnp.int32, (L, LANES), 0) < n_real
        li_all = jnp.where(live, gates, -1e30)
        lf_all = jnp.where(live, _log_sigmoid(gates), 0.0)
    else:
        li_all = gates
        lf_all = _log_sigmoid(gates)
    lf_pieces = _bf16_pieces(lf_all, 3)
    f_col_all = sum(jnp.dot(tril, p, preferred_element_type=F32) for p in lf_pieces)
    f_row_all = sum(lax.dot_general(p, tril, (((0,), (1,)), ((), ())), preferred_element_type=F32)
                    for p in lf_pieces)
    li_t = li_all.T
    rr = lax.broadcasted_iota(jnp.int32, (L, L), 0)
    cc = lax.broadcasted_iota(jnp.int32, (L, L), 1)
    causal = cc <= rr

    outs, m_new_all = [], []
    for h in range(ML_HEADS):
        f_col = f_col_all[:, ML_HEADS + h:ML_HEADS + h + 1]
        f_row = f_row_all[ML_HEADS + h:ML_HEADS + h + 1, :]
        li_row = li_t[h:h + 1, :]
        li_col = li_all[:, h:h + 1]
        m0 = m_src[0:1, h:h + 1]
        c0 = c_src[h]
        n0 = n_src[h:h + 1, :]
        qh = z[:, h * ML_QK:(h + 1) * ML_QK]
        kh = z[:, ML_HEADS * ML_QK + h * ML_QK:ML_HEADS * ML_QK + (h + 1) * ML_QK] * (ML_QK ** -0.5)
        v_off = 2 * ML_HEADS * ML_QK
        vh = z[:, v_off + h * ML_V:v_off + (h + 1) * ML_V]
        o_off = v_off + ML_HEADS * ML_V
        oh = z[:, o_off + h * ML_V:o_off + (h + 1) * ML_V]
        qb = qh.astype(BF16)
        vb = vh.astype(BF16)

        dmat = jnp.where(causal, f_col - f_row + li_row, -jnp.inf)
        gcar = f_col + m0
        m_t = jnp.maximum(jnp.max(dmat, axis=-1, keepdims=True), gcar)
        w = jnp.exp(dmat - m_t)
        s = lax.dot_general(qb, kh.astype(BF16), NT_DIMS, preferred_element_type=F32) * w
        carry = jnp.exp(gcar - m_t)
        num = (jnp.dot(s.astype(BF16), vb, preferred_element_type=F32)
               + jnp.dot(qb, c0.astype(BF16), preferred_element_type=F32) * carry)
        den = jnp.sum(s, axis=-1, keepdims=True) + carry * jnp.sum(qh * n0, axis=-1, keepdims=True)
        hout = num / jnp.maximum(jnp.abs(den), jnp.exp(-m_t))

        f_last = f_col[L - 1:L, :]
        w_end = f_last - f_col + li_col
        m_new = jnp.maximum(f_last + m0, jnp.max(w_end, axis=0, keepdims=True))
        a_end = jnp.exp(w_end - m_new)
        scale = jnp.exp(f_last + m0 - m_new)
        ka = kh * a_end
        c_dst[h] = scale * c0 + lax.dot_general(ka.astype(BF16), vb, TN_DIMS, preferred_element_type=F32)
        n_dst[h:h + 1, :] = scale * n0 + jnp.sum(ka, axis=0, keepdims=True)
        m_new_all.append(m_new)

        hn = _rmsnorm(hout, og[:, h * ML_V:(h + 1) * ML_V])
        outs.append(jax.nn.sigmoid(oh) * hn)
    m_dst[...] = jnp.concatenate(m_new_all, axis=1)
    out = jnp.concatenate(outs, axis=1)
    return jnp.dot(out.astype(BF16), wo, preferred_element_type=F32) + x


def _mlstm_prompt_kernel(*refs, n_batch):
    z_refs, x_refs = refs[0:n_batch], refs[n_batch:2 * n_batch]
    bias_ref, og_ref, tril_ref, wo_ref, y_ref, c_ref, n_ref, m_ref = refs[2 * n_batch:]

    @pl.when(pl.program_id(0) == 0)
    def _():
        c_ref[...] = jnp.zeros_like(c_ref)
        n_ref[...] = jnp.zeros_like(n_ref)
        m_ref[...] = jnp.zeros_like(m_ref)

    for b in range(n_batch):
        state = (c_ref.at[b], n_ref.at[b], m_ref.at[b])
        y_ref[b] = _mlstm_chunk(z_refs[b][...], x_refs[b][...], bias_ref[...], og_ref[...], tril_ref[...],
                                wo_ref[...], *state, *state, ML_CHUNK)


def _mlstm_sample_kernel(z_ref, x_ref, bias_ref, og_ref, tril_ref, wo_ref, c0_ref, n0_ref, m0_ref,
                         y_ref, c_ref, n_ref, m_ref, *, t_len):
    n_here = z_ref.shape[0] // t_len
    for s in range(n_here):
        rows = slice(s * t_len, (s + 1) * t_len)
        zpad = jnp.concatenate([z_ref[rows, :], jnp.zeros((ML_CHUNK - t_len, ML_IN_PAD), F32)], axis=0)
        xpad = jnp.concatenate([x_ref[rows, :], jnp.zeros((ML_CHUNK - t_len, D_MODEL), F32)], axis=0)
        y = _mlstm_chunk(zpad, xpad, bias_ref[...], og_ref[...], tril_ref[...], wo_ref[...],
                         c0_ref.at[s], n0_ref.at[s], m0_ref.at[s], c_ref.at[s], n_ref.at[s], m_ref.at[s], t_len)
        y_ref[rows, :] = y[0:t_len, :]


def _mlstm_weight_specs(const):
    return [pl.BlockSpec((1, LANES), const),
            pl.BlockSpec((1, D_MODEL), const),
            pl.BlockSpec((ML_CHUNK, ML_CHUNK), const),
            pl.BlockSpec((D_MODEL, D_MODEL), const)]


def _mlstm_prompt(z, x, n_batch, seq, bias, og, tril, wo):
    nc = seq // ML_CHUNK
    const = lambda j: (0, 0)
    tok_specs = lambda width: [pl.BlockSpec((ML_CHUNK, width), functools.partial(lambda j, b: (b * nc + j, 0), b=b))
                               for b in range(n_batch)]
    return pl.pallas_call(
        functools.partial(_mlstm_prompt_kernel, n_batch=n_batch),
        grid=(nc,),
        in_specs=tok_specs(ML_IN_PAD) + tok_specs(D_MODEL) + _mlstm_weight_specs(const),
        out_specs=[pl.BlockSpec((n_batch, ML_CHUNK, D_MODEL), lambda j: (0, j, 0)),
                   pl.BlockSpec((n_batch, ML_HEADS, ML_QK, ML_V), lambda j: (0, 0, 0, 0)),
                   pl.BlockSpec((n_batch, ML_HEADS, ML_QK), lambda j: (0, 0, 0)),
                   pl.BlockSpec((n_batch, 1, ML_HEADS), lambda j: (0, 0, 0))],
        out_shape=[jax.ShapeDtypeStruct((n_batch, seq, D_MODEL), F32),
                   jax.ShapeDtypeStruct((n_batch, ML_HEADS, ML_QK, ML_V), F32),
                   jax.ShapeDtypeStruct((n_batch, ML_HEADS, ML_QK), F32),
                   jax.ShapeDtypeStruct((n_batch, 1, ML_HEADS), F32)],
        compiler_params=_cparams(1),
        name="mlstm_prompt",
    )(*([z] * n_batch), *([x] * n_batch), bias, og, tril, wo)


def _mlstm_sample(z, x, row0, n_seq, t_len, bias, og, tril, wo, c0, n0, m0, seqs_per_step=4):
    rows = seqs_per_step * t_len
    blk0 = row0 // rows
    tok = lambda i: (blk0 + i, 0)
    const = lambda i: (0, 0)
    st4 = lambda i: (i, 0, 0, 0)
    st3 = lambda i: (i, 0, 0)
    state_specs = [pl.BlockSpec((seqs_per_step, ML_HEADS, ML_QK, ML_V), st4),
                   pl.BlockSpec((seqs_per_step, ML_HEADS, ML_QK), st3),
                   pl.BlockSpec((seqs_per_step, 1, ML_HEADS), st3)]
    return pl.pallas_call(
        functools.partial(_mlstm_sample_kernel, t_len=t_len),
        grid=(n_seq // seqs_per_step,),
        in_specs=[pl.BlockSpec((rows, ML_IN_PAD), tok),
                  pl.BlockSpec((rows, D_MODEL), tok)] + _mlstm_weight_specs(const) + state_specs,
        out_specs=[pl.BlockSpec((rows, D_MODEL), lambda i: (i, 0))] + state_specs,
        out_shape=[jax.ShapeDtypeStruct((n_seq * t_len, D_MODEL), F32),
                   jax.ShapeDtypeStruct((n_seq, ML_HEADS, ML_QK, ML_V), F32),
                   jax.ShapeDtypeStruct((n_seq, ML_HEADS, ML_QK), F32),
                   jax.ShapeDtypeStruct((n_seq, 1, ML_HEADS), F32)],
        compiler_params=_cparams(1),
        name="mlstm_sample",
    )(z, x, bias, og, tril, wo, c0, n0, m0)


def _gelu(x):
    return 0.5 * x * (1.0 + lax.erf(x * 0.7071067811865476))


SEL_T = 256


def _unit_scores(q_scr, sk_ref, half, h):
    return [lax.dot_general(sk_ref[h, p], q_scr[half, 2 * h + p].astype(BF16), NT_DIMS,
                            preferred_element_type=F32).astype(BF16) for p in range(2)]


class _TopK:
    def __init__(self, s, payload=None):
        self.s = s
        self.payload = payload
        self.rows = lax.broadcasted_iota(jnp.int32, s.shape, 0).astype(F32).astype(s.dtype)
        self.vals, self.picks = [], []

    def step(self, n):
        dt = self.s.dtype
        bound = jnp.asarray(self.s.shape[0], dt)
        for _ in range(n):
            m = jnp.max(self.s, axis=0, keepdims=True)
            first = jnp.min(jnp.where(self.s == m, self.rows, bound), axis=0, keepdims=True)
            sel = self.rows == first
            self.vals.append(m)
            if self.payload is None:
                self.picks.append(first)
            else:
                self.picks.append(jnp.max(jnp.where(sel, self.payload, -1.0), axis=0, keepdims=True))
            self.s = jnp.where(sel, jnp.asarray(-jnp.inf, dt), self.s)

    def result(self):
        return jnp.concatenate(self.vals, axis=0), jnp.concatenate(self.picks, axis=0)


def _pair_candidates(first, second):
    def pairs(a, b, combine):
        h8 = SUBLANES
        rows = [combine(a[0:1], b)]
        rows += [combine(a[k1:k1 + 1], b[0:h8]) for k1 in range(1, h8)]
        rows.append(combine(a[h8:PEER_TOPK], b[0:1]))
        return jnp.concatenate(rows, axis=0)

    (v1, i1), (v2, i2) = [tuple(a.astype(F32) for a in lst) for lst in (first, second)]
    return pairs(v1, v2, lambda a, b: a + b), pairs(i1, i2, lambda a, b: a * N_KEYS + b)


def _peer_fused_kernel(xs_ref, xm_ref, g_ref, wq_ref, sk_ref, u_ref, v_ref, o_ref,
                       q_scr, sel_e, sel_g, xn_scr, a_scr, bgt_scr, bg_scr, gmat, coef_scr):
    i = pl.program_id(0)
    e = pl.program_id(1)
    n_steps = pl.num_programs(1)
    t = xm_ref.shape[0]
    n_half = t // SEL_T
    half_keys = N_KEYS // 2
    cur = i % 2

    def project_queries():
        xn = _rmsnorm(xs_ref[...], g_ref[...]).astype(BF16)
        q = jnp.dot(xn, wq_ref[...], preferred_element_type=F32)
        for hf in range(n_half):
            for c in range(2 * PEER_HEADS):
                q_scr[hf, c] = q[hf * SEL_T:(hf + 1) * SEL_T, c * LANES:(c + 1) * LANES]

    @pl.when(e == 0)
    def _():
        @pl.when(i == 0)
        def _():
            project_queries()
            sel_e[...] = jnp.zeros_like(sel_e)
            sel_g[...] = jnp.zeros_like(sel_g)

        x = xm_ref[...]
        xn_scr[...] = _rmsnorm(x, g_ref[...]).astype(BF16)
        o_ref[...] = x
        coef_scr[0] = jnp.zeros(coef_scr.shape[1:], BF16)
        prev = 1 - cur
        lane_groups = SEL_T // LANES
        for hf in range(n_half):
            ef = sel_e[prev, hf]
            af = jnp.floor(ef * (1.0 / N_KEYS))
            bf = ef - af * N_KEYS
            gf = sel_g[prev, hf]
            a_scr[hf * SEL_T:(hf + 1) * SEL_T, :] = af.T
            bgf = bf + 0.5 * gf
            bgt_scr[hf * SEL_T:(hf + 1) * SEL_T, :] = bgf.T
            for lg in range(lane_groups):
                bg_scr[hf * lane_groups + lg] = bgf[:, lg * LANES:(lg + 1) * LANES]
        r = lax.broadcasted_iota(jnp.int32, (N_KEYS, LANES), 0)
        packed_shape = (N_KEYS // (2 * SUBLANES), 2 * SUBLANES, LANES)
        a_of_row = jnp.where(r < half_keys, 2 * r, 2 * (r - half_keys) + 1).astype(F32).astype(BF16)
        a_of_row = a_of_row.reshape(packed_shape)
        one = jnp.ones(packed_shape, BF16)
        zero = jnp.zeros(packed_shape, BF16)
        b_of_lane = lax.broadcasted_iota(jnp.int32, (N_KEYS, LANES), 1).astype(F32)
        b_of_row = r.astype(F32).astype(BF16).reshape(packed_shape)
        per_body = 32
        bodies_per_group = LANES // per_body
        row_major_every = 3

        def body(it, carry):
            grp = lax.shift_right_logical(it, bodies_per_group.bit_length() - 1)
            sub = it & (bodies_per_group - 1)
            shift = (LANES - sub * per_body) & (LANES - 1)
            bg = pltpu.roll(bg_scr[grp], shift, 1)
            tok0 = pl.multiple_of(it * per_body, per_body)
            arows = a_scr[pl.ds(tok0, per_body), :]
            bgrows = bgt_scr[pl.ds(tok0, per_body), :]
            for k in range(per_body):
                arow = jnp.broadcast_to(arows[k:k + 1], (2 * SUBLANES, LANES)).astype(BF16)[None]
                pa = jnp.where(a_of_row == arow, one, zero).reshape(N_KEYS, LANES)
                if k % row_major_every == 0:
                    bgrow = jnp.broadcast_to(bgrows[k:k + 1], (2 * SUBLANES, LANES))
                    brow = jnp.floor(bgrow)
                    grow = (2.0 * (bgrow - brow)).astype(BF16)[None]
                    qb = jnp.where(b_of_row == brow.astype(BF16)[None], grow, zero).reshape(N_KEYS, LANES)
                    tile = lax.dot_general(pa, qb, NT_DIMS, preferred_element_type=F32)
                else:
                    bgcol = jnp.broadcast_to(bg[:, k:k + 1], (N_KEYS, LANES))
                    bcol = jnp.floor(bgcol)
                    gcol = 2.0 * (bgcol - bcol)
                    qbt = jnp.where(bcol == b_of_lane, gcol, 0.0).astype(BF16)
                    tile = jnp.dot(pa, qbt, preferred_element_type=F32)
                row0 = pl.multiple_of((tok0 + k) * G_PITCH, SUBLANES)
                gmat[pl.ds(row0, half_keys), :] = pltpu.pack_elementwise(
                    [tile[0:half_keys], tile[half_keys:N_KEYS]], packed_dtype=BF16)
            return carry

        lax.fori_loop(0, t // per_body, body, 0)

    unit = jnp.minimum(e, n_half * PEER_HEADS - 1)
    half = lax.shift_right_logical(unit, PEER_HEADS.bit_length() - 1)
    h = unit & (PEER_HEADS - 1)
    first_level = [_TopK(st) for st in _unit_scores(q_scr, sk_ref, half, h)]

    rd = e & 1
    o_ref[...] += jnp.dot(coef_scr[rd], v_ref[...], preferred_element_type=F32)
    act = lax.dot_general(xn_scr[...], u_ref[...], NT_DIMS, preferred_element_type=F32)
    blk = jnp.minimum(e, n_steps - 2)
    words_per_step = u_ref.shape[0] // (2 * N_KEYS)
    gates = []
    for w in range(words_per_step):
        word = gmat[pl.ds(blk * words_per_step + w, t, stride=G_PITCH), :]
        gates.append(lax.bitcast_convert_type(word << 16, F32))
        gates.append(lax.bitcast_convert_type(word & jnp.int32(-65536), F32))
    coef_scr[1 - rd] = (_gelu(act) * jnp.concatenate(gates, axis=1)).astype(BF16)

    for lst in first_level:
        lst.step(PEER_TOPK)
    cand, expert = _pair_candidates(first_level[0].result(), first_level[1].result())
    second_level = _TopK(cand, payload=expert)
    second_level.step(PEER_TOPK)
    best, e_sel = second_level.result()
    ex = jnp.exp(best - best[0:1])
    slot0 = pl.multiple_of(h * PEER_TOPK, PEER_TOPK)
    sel_e[cur, half, pl.ds(slot0, PEER_TOPK), :] = e_sel
    sel_g[cur, half, pl.ds(slot0, PEER_TOPK), :] = ex / jnp.sum(ex, axis=0, keepdims=True)

    @pl.when(e == n_steps - 1)
    def _():
        project_queries()


def _peer(x, g, wq, sk, u, v, t=512, eb=1024):
    n, d = x.shape
    nslot = PEER_HEADS * PEER_TOPK
    n_tok_blk = n // t
    n_exp_blk = N_KEYS * N_KEYS // eb
    n_half = t // SEL_T
    assert n_exp_blk == n_half * PEER_HEADS
    last_tok = n_tok_blk - 1
    once = pl.Buffered(1)
    return pl.pallas_call(
        _peer_fused_kernel,
        grid=(n_tok_blk + 1, n_exp_blk + 1),
        in_specs=[pl.BlockSpec((t, d), lambda i, e: (jnp.minimum(i + e // n_exp_blk, last_tok), 0),
                               pipeline_mode=once),
                  pl.BlockSpec((t, d), lambda i, e: (jnp.maximum(i - 1, 0), 0)),
                  pl.BlockSpec((1, d), lambda i, e: (0, 0)),
                  pl.BlockSpec((d, 2 * PEER_HEADS * LANES), lambda i, e: (0, 0), pipeline_mode=once),
                  pl.BlockSpec((PEER_HEADS, 2, N_KEYS, LANES), lambda i, e: (0, 0, 0, 0), pipeline_mode=once),
                  pl.BlockSpec((eb, d), lambda i, e: (jnp.minimum(e, n_exp_blk - 1), 0)),
                  pl.BlockSpec((eb, d), lambda i, e: (jnp.maximum(e - 1, 0), 0))],
        out_specs=pl.BlockSpec((t, d), lambda i, e: (jnp.maximum(i - 1, 0), 0)),
        out_shape=jax.ShapeDtypeStruct((n, d), F32),
        scratch_shapes=[pltpu.VMEM((n_half, 2 * PEER_HEADS, SEL_T, LANES), F32),
                        pltpu.VMEM((2, n_half, nslot, SEL_T), F32),
                        pltpu.VMEM((2, n_half, nslot, SEL_T), F32),
                        pltpu.VMEM((t, d), BF16),
                        pltpu.VMEM((t, nslot), F32),
                        pltpu.VMEM((t, nslot), F32),
                        pltpu.VMEM((t // LANES, nslot, LANES), F32),
                        pltpu.VMEM((t * G_PITCH, LANES), jnp.int32),
                        pltpu.VMEM((2, t, eb), BF16)],
        compiler_params=_cparams(2),
        name="peer",
    )(x, x, g.reshape(1, d), wq, sk, u, v)


def _rope_tables(pos):
    half = HEAD_DIM // 2
    inv_freq = ROPE_THETA ** (-jnp.arange(half, dtype=F32) / half)
    ang = pos.astype(F32)[:, None] * inv_freq[None, :]
    cos, sin = jnp.cos(ang), jnp.sin(ang)
    reps = LANES // HEAD_DIM
    cos_t = jnp.tile(jnp.concatenate([cos, cos], axis=1), (1, reps))
    sin_t = jnp.tile(jnp.concatenate([-sin, sin], axis=1), (1, reps))
    return cos_t, sin_t


def kernel(x_prompt, x_sample, cache_conv, cache_win_k, cache_win_v, state_mlstm_C, state_mlstm_n,
           state_mlstm_m, norm_mix, norm_ffn, ab_w_in, ab_conv_w, ab_q_gain, ab_k_gain, ab_sinks, ab_w_out,
           ml_w_in, ml_gate_bias, ml_out_gain, ml_w_out, peer_w_q, peer_sub_keys, peer_u, peer_v):
    n_batch, seq, d = x_prompt.shape
    n_seq, t_len, _ = x_sample.shape
    n_prompt = n_batch * seq
    assert d == D_MODEL and t_len == SUBLANES and norm_mix.shape[0] == 2

    x = jnp.concatenate([x_prompt.reshape(n_prompt, d), x_sample.reshape(n_seq * t_len, d)], axis=0)

    cos_p, sin_p = _rope_tables(jnp.arange(seq, dtype=jnp.int32))
    cos_s, sin_s = _rope_tables(PAST_LEN + jnp.arange(t_len, dtype=jnp.int32))
    bt = 16
    cos_s, sin_s = jnp.tile(cos_s, (bt, 1)), jnp.tile(sin_s, (bt, 1))
    lane = jnp.arange(LANES)
    seg = (lane[:, None] // HEAD_DIM == lane[None, :] // HEAD_DIM).astype(BF16)
    reps = LANES // HEAD_DIM
    qg = jnp.tile(ab_q_gain[0], reps).reshape(1, LANES)
    kg = jnp.tile(ab_k_gain[0], reps).reshape(1, LANES)
    wo_ab = ab_w_out[0].astype(BF16)

    z = _norm_proj(x, norm_mix[0], ab_w_in[0].astype(BF16))
    y_p, k_p, v_p, c_p = _ab_prompt(z, x, n_batch, seq, cos_p, sin_p, ab_conv_w[0], qg, kg, ab_sinks[0],
                                    seg, wo_ab)
    y_s, c_s, k_s, v_s = _ab_sample(z, x, n_prompt, n_seq, t_len, cos_s, sin_s, ab_conv_w[0], qg, kg,
                                    ab_sinks[0], seg, wo_ab, cache_conv[0],
                                    cache_win_k[0].reshape(n_seq, WINDOW, LANES),
                                    cache_win_v[0].reshape(n_seq, WINDOW, LANES), bt=bt)
    x = jnp.concatenate([y_p.reshape(n_prompt, d), y_s], axis=0)
    x = _peer(x, norm_ffn[0], peer_w_q[0].astype(BF16), peer_sub_keys[0].astype(BF16),
              peer_u[0].astype(BF16), peer_v[0].astype(BF16))

    n_gate = 2 * ML_HEADS
    w_in = jnp.pad(ml_w_in[0], ((0, 0), (0, ML_IN_PAD - ml_w_in.shape[2]))).astype(BF16)
    bias = jnp.pad(ml_gate_bias[0], (0, LANES - n_gate)).reshape(1, LANES)
    og = ml_out_gain[0].reshape(1, D_MODEL)
    idx = jnp.arange(ML_CHUNK)
    tril = (idx[None, :] <= idx[:, None]).astype(BF16)
    wo_ml = ml_w_out[0].astype(BF16)

    z = _norm_proj(x, norm_mix[1], w_in)
    y_p, cm_p, nm_p, mm_p = _mlstm_prompt(z, x, n_batch, seq, bias, og, tril, wo_ml)
    y_s, cm_s, nm_s, mm_s = _mlstm_sample(z, x, n_prompt, n_seq, t_len, bias, og, tril, wo_ml,
                                          state_mlstm_C[0], state_mlstm_n[0],
                                          state_mlstm_m[0].reshape(n_seq, 1, ML_HEADS))
    x = jnp.concatenate([y_p.reshape(n_prompt, d), y_s], axis=0)
    x = _peer(x, norm_ffn[1], peer_w_q[1].astype(BF16), peer_sub_keys[1].astype(BF16),
              peer_u[1].astype(BF16), peer_v[1].astype(BF16))

    y_prompt = x[:n_prompt].reshape(n_batch, seq, d)
    y_sample = x[n_prompt:].reshape(n_seq, t_len, d)
    kv_shape_p = (1, n_batch, WINDOW, N_KV_HEADS, HEAD_DIM)
    kv_shape_s = (1, n_seq, WINDOW, N_KV_HEADS, HEAD_DIM)
    return (y_prompt, y_sample,
            c_p[:, SUBLANES - 2:, :][None], k_p.reshape(kv_shape_p), v_p.reshape(kv_shape_p),
            cm_p[None], nm_p[None], mm_p.reshape(1, n_batch, ML_HEADS),
            c_s[None], k_s.reshape(kv_shape_s), v_s.reshape(kv_shape_s),
            cm_s[None], nm_s[None], mm_s.reshape(1, n_seq, ML_HEADS))
```

```python
import functools

import jax
import jax.numpy as jnp
from jax import lax
from jax.experimental import pallas as pl
from jax.experimental.pallas import tpu as pltpu

F32 = jnp.float32
BF16 = jnp.bfloat16
EPS = 1e-6

D_MODEL = 1024
CONV_DIM = 512
N_Q_HEADS = 8
N_KV_HEADS = 2
HEAD_DIM = 64
WINDOW = 128
ROPE_THETA = 10000.0
AB_IN = 2304
ML_HEADS = 4
ML_QK = 128
ML_V = 256
ML_CHUNK = 128
ML_GATE_COL = 3072
ML_IN_PAD = ML_GATE_COL + 128
N_KEYS = 128
PEER_HEADS = 8
PEER_TOPK = 16
PAST_LEN = 16384

LANES = 128
SUBLANES = 8
G_PITCH = N_KEYS // 2 + SUBLANES
VMEM_LIMIT = 56 * 1024 * 1024

NT_DIMS = (((1,), (1,)), ((), ()))
TN_DIMS = (((0,), (0,)), ((), ()))


def _cparams(n_axes, vmem=VMEM_LIMIT):
    return pltpu.CompilerParams(dimension_semantics=("arbitrary",) * n_axes, vmem_limit_bytes=vmem)


def _rmsnorm(x, g):
    return x * lax.rsqrt(jnp.mean(x * x, axis=-1, keepdims=True) + EPS) * g


def _bf16_pieces(a, terms):
    pieces = []
    rem = a
    for _ in range(terms):
        piece = rem.astype(BF16)
        rem = rem - piece.astype(F32)
        pieces.append(piece)
    return pieces


def _split_dot(a, b_bf16, terms=2):
    return sum(jnp.dot(p, b_bf16, preferred_element_type=F32) for p in _bf16_pieces(a, terms))


def _norm_proj_kernel(x_ref, g_ref, w_ref, o_ref):
    r = _rmsnorm(x_ref[...], g_ref[...])
    o_ref[...] = jnp.dot(r.astype(BF16), w_ref[...], preferred_element_type=F32)


def _norm_proj(x, g, w_bf16, tm=512):
    n, d = x.shape
    nout = w_bf16.shape[1]
    return pl.pallas_call(
        _norm_proj_kernel,
        grid=(n // tm,),
        in_specs=[pl.BlockSpec((tm, d), lambda i: (i, 0)),
                  pl.BlockSpec((1, d), lambda i: (0, 0)),
                  pl.BlockSpec((d, nout), lambda i: (0, 0))],
        out_specs=pl.BlockSpec((tm, nout), lambda i: (i, 0)),
        out_shape=jax.ShapeDtypeStruct((n, nout), F32),
        compiler_params=_cparams(1),
        name="norm_proj",
    )(x, g.reshape(1, d), w_bf16)


def _headnorm_rope(xc, gain, cos, sin, seg, hi_half):
    ss = _split_dot(xc * xc, seg)
    xn = xc * lax.rsqrt(ss * (1.0 / HEAD_DIM) + EPS) * gain
    partner = jnp.where(hi_half, pltpu.roll(xn, 32, 1), pltpu.roll(xn, 96, 1))
    return xn * cos + partner * sin


def _softmax_sink(s, mask, sink):
    s = jnp.where(mask, s, -1e30)
    m = jnp.maximum(jnp.max(s, axis=-1, keepdims=True), sink)
    p = jnp.exp(s - m)
    denom = jnp.sum(p, axis=-1, keepdims=True) + jnp.exp(sink - m)
    return (p / denom).astype(BF16)


def _ab_prompt_kernel(sink_ref, *refs, n_batch):
    z_refs, x_refs = refs[0:n_batch], refs[n_batch:2 * n_batch]
    (cos_ref, sin_ref, cw_ref, qg_ref, kg_ref, seg_ref, wo_ref,
     y_ref, kst_ref, vst_ref, cst_ref, pk_ref, pv_ref, pu_ref) = refs[2 * n_batch:]
    j = pl.program_id(0)

    @pl.when(j == 0)
    def _():
        pk_ref[...] = jnp.zeros_like(pk_ref)
        pv_ref[...] = jnp.zeros_like(pv_ref)
        pu_ref[...] = jnp.zeros_like(pu_ref)

    for b in range(n_batch):
        _ab_prompt_block(j, sink_ref, z_refs[b], x_refs[b], cos_ref, sin_ref, cw_ref, qg_ref, kg_ref, seg_ref,
                         wo_ref, y_ref.at[b], kst_ref.at[b], vst_ref.at[b], cst_ref.at[b],
                         pk_ref.at[b], pv_ref.at[b], pu_ref.at[b])


def _ab_prompt_block(j, sink_ref, z_ref, x_ref, cos_ref, sin_ref, cw_ref, qg_ref, kg_ref, seg_ref, wo_ref,
                     y_ref, kst_ref, vst_ref, cst_ref, pk_ref, pv_ref, pu_ref):
    blk = z_ref.shape[0]
    cos = cos_ref[...]
    sin = sin_ref[...]
    seg = seg_ref[...]
    hi_half = (lax.broadcasted_iota(jnp.int32, (blk, LANES), 1) & 32) != 0

    gate_b = z_ref[:, 0:CONV_DIM]
    u = z_ref[:, CONV_DIM:2 * CONV_DIM] * z_ref[:, 2 * CONV_DIM:3 * CONV_DIM]
    ng = blk // SUBLANES
    u3 = u.reshape(ng, SUBLANES, CONV_DIM)
    ext = jnp.concatenate([pu_ref[...][None], u3], axis=0)
    t8 = lax.broadcasted_iota(jnp.int32, (ng, SUBLANES, CONV_DIM), 1)
    r1 = pltpu.roll(ext, 1, 1)
    r2 = pltpu.roll(ext, 2, 1)
    um1 = jnp.where(t8 >= 1, r1[1:], r1[:-1])
    um2 = jnp.where(t8 >= 2, r2[1:], r2[:-1])
    cw = cw_ref[...]
    conv = cw[0:1][None] * um2 + cw[1:2][None] * um1 + cw[2:3][None] * u3
    yconv = gate_b * conv.reshape(blk, CONV_DIM)

    q0 = 3 * CONV_DIM
    k0 = q0 + N_Q_HEADS * HEAD_DIM
    v0 = k0 + N_KV_HEADS * HEAD_DIM
    qg = qg_ref[...]
    qr = [_headnorm_rope(z_ref[:, q0 + c * LANES:q0 + (c + 1) * LANES], qg, cos, sin, seg, hi_half)
          for c in range(N_Q_HEADS * HEAD_DIM // LANES)]
    kr = _headnorm_rope(z_ref[:, k0:k0 + LANES], kg_ref[...], cos, sin, seg, hi_half)
    v = z_ref[:, v0:v0 + LANES]
    pk = pk_ref[...]
    pv = pv_ref[...]

    row = lax.broadcasted_iota(jnp.int32, (blk, 2 * blk), 0)
    col = lax.broadcasted_iota(jnp.int32, (blk, 2 * blk), 1)
    row_prev = row + jnp.where(j == 0, 2 * blk, 0)
    mask = ((col < blk) & (col > row_prev)) | ((col >= blk) & (col - blk <= row))

    kks, vvs = [], []
    for g in range(N_KV_HEADS):
        sl = slice(g * HEAD_DIM, (g + 1) * HEAD_DIM)
        kks.append(jnp.concatenate([pk[:, sl], kr[:, sl]], axis=0).astype(BF16))
        vvs.append(jnp.concatenate([pv[:, sl], v[:, sl]], axis=0).astype(BF16))
    outs = []
    for h in range(N_Q_HEADS):
        g = h // (N_Q_HEADS // N_KV_HEADS)
        qh = qr[h // 2][:, (h % 2) * HEAD_DIM:(h % 2 + 1) * HEAD_DIM].astype(BF16)
        s = lax.dot_general(qh, kks[g], NT_DIMS, preferred_element_type=F32) * (HEAD_DIM ** -0.5)
        p = _softmax_sink(s, mask, sink_ref[h])
        outs.append(jnp.dot(p, vvs[g], preferred_element_type=F32))
    attn = jnp.concatenate(outs, axis=1)

    y = (jnp.dot(yconv.astype(BF16), wo_ref[0:CONV_DIM, :], preferred_element_type=F32)
         + jnp.dot(attn.astype(BF16), wo_ref[CONV_DIM:2 * CONV_DIM, :], preferred_element_type=F32))
    y_ref[...] = y + x_ref[...]

    pk_ref[...] = kr
    pv_ref[...] = v
    pu_ref[...] = u3[ng - 1]
    kst_ref[...] = kr
    vst_ref[...] = v
    cst_ref[...] = u3[ng - 1]


def _ab_prompt(z, x, n_batch, seq, cos, sin, cw, qg, kg, sinks, seg, wo):
    blk = WINDOW
    nb = seq // blk
    const = lambda j, s: (0, 0)
    whole = lambda j, s: (0, 0, 0)
    tok_specs = lambda width: [pl.BlockSpec((blk, width), functools.partial(lambda j, s, b: (b * nb + j, 0), b=b))
                               for b in range(n_batch)]
    grid_spec = pltpu.PrefetchScalarGridSpec(
        num_scalar_prefetch=1,
        grid=(nb,),
        in_specs=tok_specs(AB_IN) + tok_specs(D_MODEL) + [
            pl.BlockSpec((blk, LANES), lambda j, s: (j, 0)),
            pl.BlockSpec((blk, LANES), lambda j, s: (j, 0)),
            pl.BlockSpec((3, CONV_DIM), const),
            pl.BlockSpec((1, LANES), const),
            pl.BlockSpec((1, LANES), const),
            pl.BlockSpec((LANES, LANES), const),
            pl.BlockSpec((D_MODEL, D_MODEL), const)],
        out_specs=[pl.BlockSpec((n_batch, blk, D_MODEL), lambda j, s: (0, j, 0)),
                   pl.BlockSpec((n_batch, blk, LANES), whole),
                   pl.BlockSpec((n_batch, blk, LANES), whole),
                   pl.BlockSpec((n_batch, SUBLANES, CONV_DIM), whole)],
        scratch_shapes=[pltpu.VMEM((n_batch, blk, LANES), F32), pltpu.VMEM((n_batch, blk, LANES), F32),
                        pltpu.VMEM((n_batch, SUBLANES, CONV_DIM), F32)])
    return pl.pallas_call(
        functools.partial(_ab_prompt_kernel, n_batch=n_batch),
        grid_spec=grid_spec,
        out_shape=[jax.ShapeDtypeStruct((n_batch, seq, D_MODEL), F32),
                   jax.ShapeDtypeStruct((n_batch, blk, LANES), F32),
                   jax.ShapeDtypeStruct((n_batch, blk, LANES), F32),
                   jax.ShapeDtypeStruct((n_batch, SUBLANES, CONV_DIM), F32)],
        compiler_params=_cparams(1),
        name="ab_prompt",
    )(sinks, *([z] * n_batch), *([x] * n_batch), cos, sin, cw, qg, kg, seg, wo)


def _ab_sample_kernel(sink_ref, z_ref, x_ref, cos_ref, sin_ref, cw_ref, qg_ref, kg_ref, seg_ref, wo_ref,
                      cc_ref, ck_ref, cv_ref, y_ref, cs_ref, ks_ref, vs_ref):
    rows = z_ref.shape[0]
    t_len = SUBLANES
    bt = rows // t_len
    cos = cos_ref[...]
    sin = sin_ref[...]
    seg = seg_ref[...]
    hi_half = (lax.broadcasted_iota(jnp.int32, (rows, LANES), 1) & 32) != 0

    gate_b = z_ref[:, 0:CONV_DIM]
    u = z_ref[:, CONV_DIM:2 * CONV_DIM] * z_ref[:, 2 * CONV_DIM:3 * CONV_DIM]
    u3 = u.reshape(bt, t_len, CONV_DIM)
    cc = cc_ref[...]
    c0 = cc[:, 0:1, :]
    c1 = cc[:, 1:2, :]
    t8 = lax.broadcasted_iota(jnp.int32, (bt, t_len, CONV_DIM), 1)
    r1 = pltpu.roll(u3, 1, 1)
    r2 = pltpu.roll(u3, 2, 1)
    um1 = jnp.where(t8 >= 1, r1, c1)
    um2 = jnp.where(t8 >= 2, r2, jnp.where(t8 == 1, c1, c0))
    cw = cw_ref[...]
    conv = cw[0:1][None] * um2 + cw[1:2][None] * um1 + cw[2:3][None] * u3
    yconv = gate_b * conv.reshape(rows, CONV_DIM)
    cs_ref[...] = r2[:, 0:2, :]

    q0 = 3 * CONV_DIM
    k0 = q0 + N_Q_HEADS * HEAD_DIM
    v0 = k0 + N_KV_HEADS * HEAD_DIM
    qg = qg_ref[...]
    qr = [_headnorm_rope(z_ref[:, q0 + c * LANES:q0 + (c + 1) * LANES], qg, cos, sin, seg, hi_half)
          for c in range(N_Q_HEADS * HEAD_DIM // LANES)]
    kr = _headnorm_rope(z_ref[:, k0:k0 + LANES], kg_ref[...], cos, sin, seg, hi_half)
    v = z_ref[:, v0:v0 + LANES]

    group = N_Q_HEADS // N_KV_HEADS
    nq = group * t_len
    nk = 2 * WINDOW
    qrow = lax.broadcasted_iota(jnp.int32, (nq, nk), 0)
    t_q = qrow & (t_len - 1)
    col = lax.broadcasted_iota(jnp.int32, (nq, nk), 1)
    mask = (((col < WINDOW) & (col > t_q)) | ((col >= WINDOW) & (col - WINDOW <= t_q)))[None]
    hrow = lax.broadcasted_iota(jnp.int32, (nq, 1), 0) // t_len
    pad = jnp.zeros((bt, nk - WINDOW - t_len, HEAD_DIM), F32)

    outs = [None] * N_Q_HEADS
    for g in range(N_KV_HEADS):
        sl = slice(g * HEAD_DIM, (g + 1) * HEAD_DIM)
        qs = jnp.concatenate(
            [qr[h // 2][:, (h % 2) * HEAD_DIM:(h % 2 + 1) * HEAD_DIM].reshape(bt, t_len, HEAD_DIM)
             for h in range(g * group, (g + 1) * group)], axis=1)
        kk = jnp.concatenate([ck_ref[:, :, sl], kr[:, sl].reshape(bt, t_len, HEAD_DIM), pad], axis=1)
        vv = jnp.concatenate([cv_ref[:, :, sl], v[:, sl].reshape(bt, t_len, HEAD_DIM), pad], axis=1)
        s = jnp.einsum('bqd,bkd->bqk', qs.astype(BF16), kk.astype(BF16),
                       preferred_element_type=F32) * (HEAD_DIM ** -0.5)
        sink = jnp.zeros((nq, 1), F32)
        for hh in range(group):
            sink = jnp.where(hrow == hh, sink_ref[g * group + hh], sink)
        p = _softmax_sink(s, mask, sink[None])
        o = jnp.einsum('bqk,bkd->bqd', p, vv.astype(BF16), preferred_element_type=F32)
        for hh in range(group):
            outs[g * group + hh] = o[:, hh * t_len:(hh + 1) * t_len, :].reshape(rows, HEAD_DIM)
    attn = jnp.concatenate(outs, axis=1)

    y = (jnp.dot(yconv.astype(BF16), wo_ref[0:CONV_DIM, :], preferred_element_type=F32)
         + jnp.dot(attn.astype(BF16), wo_ref[CONV_DIM:2 * CONV_DIM, :], preferred_element_type=F32))
    y_ref[...] = y + x_ref[...]

    keep = WINDOW - t_len
    ks_ref[:, 0:keep, :] = ck_ref[:, t_len:WINDOW, :]
    ks_ref[:, keep:WINDOW, :] = kr.reshape(bt, t_len, LANES)
    vs_ref[:, 0:keep, :] = cv_ref[:, t_len:WINDOW, :]
    vs_ref[:, keep:WINDOW, :] = v.reshape(bt, t_len, LANES)


def _ab_sample(z, x, row0, n_seq, t_len, cos, sin, cw, qg, kg, sinks, seg, wo, cc, ck, cv, bt=16):
    rows = bt * t_len
    blk0 = row0 // rows
    tok = lambda i, s: (blk0 + i, 0)
    const = lambda i, s: (0, 0)
    seq3 = lambda i, s: (i, 0, 0)
    grid_spec = pltpu.PrefetchScalarGridSpec(
        num_scalar_prefetch=1,
        grid=(n_seq // bt,),
        in_specs=[pl.BlockSpec((rows, AB_IN), tok),
                  pl.BlockSpec((rows, D_MODEL), tok),
                  pl.BlockSpec((rows, LANES), const),
                  pl.BlockSpec((rows, LANES), const),
                  pl.BlockSpec((3, CONV_DIM), const),
                  pl.BlockSpec((1, LANES), const),
                  pl.BlockSpec((1, LANES), const),
                  pl.BlockSpec((LANES, LANES), const),
                  pl.BlockSpec((D_MODEL, D_MODEL), const),
                  pl.BlockSpec((bt, 2, CONV_DIM), seq3),
                  pl.BlockSpec((bt, WINDOW, LANES), seq3),
                  pl.BlockSpec((bt, WINDOW, LANES), seq3)],
        out_specs=[pl.BlockSpec((rows, D_MODEL), lambda i, s: (i, 0)),
                   pl.BlockSpec((bt, 2, CONV_DIM), seq3),
                   pl.BlockSpec((bt, WINDOW, LANES), seq3),
                   pl.BlockSpec((bt, WINDOW, LANES), seq3)])
    return pl.pallas_call(
        _ab_sample_kernel,
        grid_spec=grid_spec,
        out_shape=[jax.ShapeDtypeStruct((n_seq * t_len, D_MODEL), F32),
                   jax.ShapeDtypeStruct((n_seq, 2, CONV_DIM), F32),
                   jax.ShapeDtypeStruct((n_seq, WINDOW, LANES), F32),
                   jax.ShapeDtypeStruct((n_seq, WINDOW, LANES), F32)],
        compiler_params=_cparams(1),
        name="ab_sample",
    )(sinks, z, x, cos, sin, cw, qg, kg, seg, wo, cc, ck, cv)


def _log_sigmoid(x):
    return jnp.minimum(x, 0.0) - jnp.log(1.0 + jnp.exp(-jnp.abs(x)))


def _mlstm_chunk(z, bias, og, tril, c_src, n_src, m_src, c_dst, n_dst, m_dst, n_real, rows):
    L = z.shape[0]
    gates = z[:, ML_GATE_COL:ML_GATE_COL + LANES] + bias
    if n_real < L:
        live = lax.broadcasted_iota(jnp.int32, (L, LANES), 0) < n_real
        li_all = jnp.where(live, gates, -1e30)
        lf_all = jnp.where(live, _log_sigmoid(gates), 0.0)
    else:
        li_all = gates
        lf_all = _log_sigmoid(gates)
    lf_pieces = _bf16_pieces(lf_all, 3)
    f_col_all = sum(jnp.dot(tril[0:rows], p, preferred_element_type=F32) for p in lf_pieces)
    f_row_all = sum(lax.dot_general(p, tril, (((0,), (1,)), ((), ())), preferred_element_type=F32)
                    for p in lf_pieces)
    li_t = li_all.T
    rr = lax.broadcasted_iota(jnp.int32, (rows, L), 0)
    cc = lax.broadcasted_iota(jnp.int32, (rows, L), 1)
    causal = cc <= rr

    outs, m_new_all = [], []
    for h in range(ML_HEADS):
        f_col = f_col_all[:, ML_HEADS + h:ML_HEADS + h + 1]
        f_row = f_row_all[ML_HEADS + h:ML_HEADS + h + 1, :]
        li_row = li_t[h:h + 1, :]
        li_col = li_all[0:rows, h:h + 1]
        m0 = m_src[0:1, h:h + 1]
        c0 = c_src[h]
        n0 = n_src[h:h + 1, :]
        qh = z[0:rows, h * ML_QK:(h + 1) * ML_QK]
        kh = z[:, ML_HEADS * ML_QK + h * ML_QK:ML_HEADS * ML_QK + (h + 1) * ML_QK] * (ML_QK ** -0.5)
        v_off = 2 * ML_HEADS * ML_QK
        vh = z[:, v_off + h * ML_V:v_off + (h + 1) * ML_V]
        o_off = v_off + ML_HEADS * ML_V
        oh = z[0:rows, o_off + h * ML_V:o_off + (h + 1) * ML_V]
        qb = qh.astype(BF16)
        vb = vh.astype(BF16)

        dmat = jnp.where(causal, f_col - f_row + li_row, -jnp.inf)
        gcar = f_col + m0
        m_t = jnp.maximum(jnp.max(dmat, axis=-1, keepdims=True), gcar)
        w = jnp.exp(dmat - m_t)
        s = lax.dot_general(qb, kh.astype(BF16), NT_DIMS, preferred_element_type=F32) * w
        carry = jnp.exp(gcar - m_t)
        num = (jnp.dot(s.astype(BF16), vb, preferred_element_type=F32)
               + jnp.dot(qb, c0.astype(BF16), preferred_element_type=F32) * carry)
        den = jnp.sum(s, axis=-1, keepdims=True) + carry * jnp.sum(qh * n0, axis=-1, keepdims=True)
        hout = num / jnp.maximum(jnp.abs(den), jnp.exp(-m_t))

        f_last = f_col[n_real - 1:n_real, :]
        w_end = f_last - f_col + li_col
        m_new = jnp.maximum(f_last + m0, jnp.max(w_end, axis=0, keepdims=True))
        a_end = jnp.exp(w_end - m_new)
        scale = jnp.exp(f_last + m0 - m_new)
        ka = kh[0:rows] * a_end
        c_dst[h] = scale * c0 + lax.dot_general(ka.astype(BF16), vb[0:rows], TN_DIMS,
                                                preferred_element_type=F32)
        n_dst[h:h + 1, :] = scale * n0 + jnp.sum(ka, axis=0, keepdims=True)
        m_new_all.append(m_new)

        hn = _rmsnorm(hout, og[:, h * ML_V:(h + 1) * ML_V])
        outs.append(jax.nn.sigmoid(oh) * hn)
    m_dst[...] = jnp.concatenate(m_new_all, axis=1)
    return jnp.concatenate(outs, axis=1)


def _mlstm_prompt_kernel(*refs, n_batch):
    z_refs, x_refs = refs[0:n_batch], refs[n_batch:2 * n_batch]
    bias_ref, og_ref, tril_ref, wo_ref, y_ref, c_ref, n_ref, m_ref = refs[2 * n_batch:]

    @pl.when(pl.program_id(0) == 0)
    def _():
        c_ref[...] = jnp.zeros_like(c_ref)
        n_ref[...] = jnp.zeros_like(n_ref)
        m_ref[...] = jnp.zeros_like(m_ref)

    for b in range(n_batch):
        state = (c_ref.at[b], n_ref.at[b], m_ref.at[b])
        out = _mlstm_chunk(z_refs[b][...], bias_ref[...], og_ref[...], tril_ref[...], *state, *state,
                           ML_CHUNK, ML_CHUNK)
        y_ref[b] = jnp.dot(out.astype(BF16), wo_ref[...], preferred_element_type=F32) + x_refs[b][...]


def _mlstm_sample_kernel(z_ref, x_ref, bias_ref, og_ref, tril_ref, wo_ref, c0_ref, n0_ref, m0_ref,
                         y_ref, c_ref, n_ref, m_ref, *, t_len):
    n_here = z_ref.shape[0] // t_len
    q_rows = 2 * SUBLANES
    outs = []
    for s in range(n_here):
        zpad = jnp.concatenate([z_ref[s * t_len:(s + 1) * t_len, :],
                                jnp.zeros((ML_CHUNK - t_len, ML_IN_PAD), F32)], axis=0)
        out = _mlstm_chunk(zpad, bias_ref[...], og_ref[...], tril_ref[...],
                           c0_ref.at[s], n0_ref.at[s], m0_ref.at[s], c_ref.at[s], n_ref.at[s], m_ref.at[s],
                           t_len, q_rows)
        outs.append(out[0:t_len])
    out_all = jnp.concatenate(outs, axis=0).astype(BF16)
    y_ref[...] = jnp.dot(out_all, wo_ref[...], preferred_element_type=F32) + x_ref[...]


def _mlstm_weight_specs(const):
    return [pl.BlockSpec((1, LANES), const),
            pl.BlockSpec((1, D_MODEL), const),
            pl.BlockSpec((ML_CHUNK, ML_CHUNK), const),
            pl.BlockSpec((D_MODEL, D_MODEL), const)]


def _mlstm_prompt(z, x, n_batch, seq, bias, og, tril, wo):
    nc = seq // ML_CHUNK
    const = lambda j: (0, 0)
    tok_specs = lambda width: [pl.BlockSpec((ML_CHUNK, width), functools.partial(lambda j, b: (b * nc + j, 0), b=b))
                               for b in range(n_batch)]
    return pl.pallas_call(
        functools.partial(_mlstm_prompt_kernel, n_batch=n_batch),
        grid=(nc,),
        in_specs=tok_specs(ML_IN_PAD) + tok_specs(D_MODEL) + _mlstm_weight_specs(const),
        out_specs=[pl.BlockSpec((n_batch, ML_CHUNK, D_MODEL), lambda j: (0, j, 0)),
                   pl.BlockSpec((n_batch, ML_HEADS, ML_QK, ML_V), lambda j: (0, 0, 0, 0)),
                   pl.BlockSpec((n_batch, ML_HEADS, ML_QK), lambda j: (0, 0, 0)),
                   pl.BlockSpec((n_batch, 1, ML_HEADS), lambda j: (0, 0, 0))],
        out_shape=[jax.ShapeDtypeStruct((n_batch, seq, D_MODEL), F32),
                   jax.ShapeDtypeStruct((n_batch, ML_HEADS, ML_QK, ML_V), F32),
                   jax.ShapeDtypeStruct((n_batch, ML_HEADS, ML_QK), F32),
                   jax.ShapeDtypeStruct((n_batch, 1, ML_HEADS), F32)],
        compiler_params=_cparams(1),
        name="mlstm_prompt",
    )(*([z] * n_batch), *([x] * n_batch), bias, og, tril, wo)


def _mlstm_sample(z, x, row0, n_seq, t_len, bias, og, tril, wo, c0, n0, m0, seqs_per_step=8):
    rows = seqs_per_step * t_len
    blk0 = row0 // rows
    tok = lambda i: (blk0 + i, 0)
    const = lambda i: (0, 0)
    st4 = lambda i: (i, 0, 0, 0)
    st3 = lambda i: (i, 0, 0)
    state_specs = [pl.BlockSpec((seqs_per_step, ML_HEADS, ML_QK, ML_V), st4),
                   pl.BlockSpec((seqs_per_step, ML_HEADS, ML_QK), st3),
                   pl.BlockSpec((seqs_per_step, 1, ML_HEADS), st3)]
    return pl.pallas_call(
        functools.partial(_mlstm_sample_kernel, t_len=t_len),
        grid=(n_seq // seqs_per_step,),
        in_specs=[pl.BlockSpec((rows, ML_IN_PAD), tok),
                  pl.BlockSpec((rows, D_MODEL), tok)] + _mlstm_weight_specs(const) + state_specs,
        out_specs=[pl.BlockSpec((rows, D_MODEL), lambda i: (i, 0))] + state_specs,
        out_shape=[jax.ShapeDtypeStruct((n_seq * t_len, D_MODEL), F32),
                   jax.ShapeDtypeStruct((n_seq, ML_HEADS, ML_QK, ML_V), F32),
                   jax.ShapeDtypeStruct((n_seq, ML_HEADS, ML_QK), F32),
                   jax.ShapeDtypeStruct((n_seq, 1, ML_HEADS), F32)],
        compiler_params=_cparams(1),
        name="mlstm_sample",
    )(z, x, bias, og, tril, wo, c0, n0, m0)


def _gelu(x):
    return 0.5 * x * (1.0 + lax.erf(x * 0.7071067811865476))


SEL_T = 256


def _unit_scores(q_scr, sk_ref, half, h):
    return [lax.dot_general(sk_ref[h, p], q_scr[half, 2 * h + p].astype(BF16), NT_DIMS,
                            preferred_element_type=F32).astype(BF16) for p in range(2)]


class _TopK:
    def __init__(self, s, payload=None):
        self.s = s
        self.payload = payload
        self.rows = lax.broadcasted_iota(jnp.int32, s.shape, 0).astype(F32).astype(s.dtype)
        self.vals, self.picks = [], []

    def step(self, n):
        dt = self.s.dtype
        bound = jnp.asarray(self.s.shape[0], dt)
        for _ in range(n):
            m = jnp.max(self.s, axis=0, keepdims=True)
            first = jnp.min(jnp.where(self.s == m, self.rows, bound), axis=0, keepdims=True)
            sel = self.rows == first
            self.vals.append(m)
            if self.payload is None:
                self.picks.append(first)
            else:
                self.picks.append(jnp.max(jnp.where(sel, self.payload, -1.0), axis=0, keepdims=True))
            self.s = jnp.where(sel, jnp.asarray(-jnp.inf, dt), self.s)

    def result(self):
        return jnp.concatenate(self.vals, axis=0), jnp.concatenate(self.picks, axis=0)


def _pair_candidates(first, second):
    def pairs(a, b, combine):
        h8 = SUBLANES
        rows = [combine(a[0:1], b)]
        rows += [combine(a[k1:k1 + 1], b[0:h8]) for k1 in range(1, h8)]
        rows.append(combine(a[h8:PEER_TOPK], b[0:1]))
        return jnp.concatenate(rows, axis=0)

    (v1, i1), (v2, i2) = [tuple(a.astype(F32) for a in lst) for lst in (first, second)]
    return pairs(v1, v2, lambda a, b: a + b), pairs(i1, i2, lambda a, b: a * N_KEYS + b)


def _peer_fused_kernel(xs_ref, xm_ref, g_ref, wq_ref, sk_ref, u_ref, v_ref, o_ref,
                       q_scr, sel_e, sel_g, xn_scr, a_scr, bgt_scr, bg_scr, gmat, coef_scr):
    i = pl.program_id(0)
    e = pl.program_id(1)
    n_steps = pl.num_programs(1)
    t = xm_ref.shape[0]
    n_half = t // SEL_T
    half_keys = N_KEYS // 2
    cur = i % 2

    def project_queries():
        xn = _rmsnorm(xs_ref[...], g_ref[...]).astype(BF16)
        q = jnp.dot(xn, wq_ref[...], preferred_element_type=F32)
        for hf in range(n_half):
            for c in range(2 * PEER_HEADS):
                q_scr[hf, c] = q[hf * SEL_T:(hf + 1) * SEL_T, c * LANES:(c + 1) * LANES]

    @pl.when(e == 0)
    def _():
        @pl.when(i == 0)
        def _():
            project_queries()
            sel_e[...] = jnp.zeros_like(sel_e)
            sel_g[...] = jnp.zeros_like(sel_g)

        x = xm_ref[...]
        xn_scr[...] = _rmsnorm(x, g_ref[...]).astype(BF16)
        o_ref[...] = x
        coef_scr[0] = jnp.zeros(coef_scr.shape[1:], BF16)
        prev = 1 - cur
        lane_groups = SEL_T // LANES
        for hf in range(n_half):
            ef = sel_e[prev, hf]
            af = jnp.floor(ef * (1.0 / N_KEYS))
            bf = ef - af * N_KEYS
            gf = sel_g[prev, hf]
            a_scr[hf * SEL_T:(hf + 1) * SEL_T, :] = af.T
            bgf = bf + 0.5 * gf
            bgt_scr[hf * SEL_T:(hf + 1) * SEL_T, :] = bgf.T
            for lg in range(lane_groups):
                bg_scr[hf * lane_groups + lg] = bgf[:, lg * LANES:(lg + 1) * LANES]
        r = lax.broadcasted_iota(jnp.int32, (N_KEYS, LANES), 0)
        packed_shape = (N_KEYS // (2 * SUBLANES), 2 * SUBLANES, LANES)
        a_of_row = jnp.where(r < half_keys, 2 * r, 2 * (r - half_keys) + 1).astype(F32).astype(BF16)
        a_of_row = a_of_row.reshape(packed_shape)
        one = jnp.ones(packed_shape, BF16)
        zero = jnp.zeros(packed_shape, BF16)
        b_of_lane = lax.broadcasted_iota(jnp.int32, (N_KEYS, LANES), 1).astype(F32)
        b_of_row = r.astype(F32).astype(BF16).reshape(packed_shape)
        per_body = 32
        bodies_per_group = LANES // per_body
        row_major_every = 3

        def body(it, carry):
            grp = lax.shift_right_logical(it, bodies_per_group.bit_length() - 1)
            sub = it & (bodies_per_group - 1)
            shift = (LANES - sub * per_body) & (LANES - 1)
            bg = pltpu.roll(bg_scr[grp], shift, 1)
            tok0 = pl.multiple_of(it * per_body, per_body)
            arows = a_scr[pl.ds(tok0, per_body), :]
            bgrows = bgt_scr[pl.ds(tok0, per_body), :]
            for k in range(per_body):
                arow = jnp.broadcast_to(arows[k:k + 1], (2 * SUBLANES, LANES)).astype(BF16)[None]
                pa = jnp.where(a_of_row == arow, one, zero).reshape(N_KEYS, LANES)
                if k % row_major_every == 0:
                    bgrow = jnp.broadcast_to(bgrows[k:k + 1], (2 * SUBLANES, LANES))
                    brow = jnp.floor(bgrow)
                    grow = (2.0 * (bgrow - brow)).astype(BF16)[None]
                    qb = jnp.where(b_of_row == brow.astype(BF16)[None], grow, zero).reshape(N_KEYS, LANES)
                    tile = lax.dot_general(pa, qb, NT_DIMS, preferred_element_type=F32)
                else:
                    bgcol = jnp.broadcast_to(bg[:, k:k + 1], (N_KEYS, LANES))
                    bcol = jnp.floor(bgcol)
                    gcol = 2.0 * (bgcol - bcol)
                    qbt = jnp.where(bcol == b_of_lane, gcol, 0.0).astype(BF16)
                    tile = jnp.dot(pa, qbt, preferred_element_type=F32)
                row0 = pl.multiple_of((tok0 + k) * G_PITCH, SUBLANES)
                gmat[pl.ds(row0, half_keys), :] = pltpu.pack_elementwise(
                    [tile[0:half_keys], tile[half_keys:N_KEYS]], packed_dtype=BF16)
            return carry

        lax.fori_loop(0, t // per_body, body, 0)

    unit = jnp.minimum(e, n_half * PEER_HEADS - 1)
    half = lax.shift_right_logical(unit, PEER_HEADS.bit_length() - 1)
    h = unit & (PEER_HEADS - 1)
    first_level = [_TopK(st) for st in _unit_scores(q_scr, sk_ref, half, h)]

    rd = e & 1
    o_ref[...] += jnp.dot(coef_scr[rd], v_ref[...], preferred_element_type=F32)
    act = lax.dot_general(xn_scr[...], u_ref[...], NT_DIMS, preferred_element_type=F32)
    blk = jnp.minimum(e, n_steps - 2)
    words_per_step = u_ref.shape[0] // (2 * N_KEYS)
    gates = []
    for w in range(words_per_step):
        word = gmat[pl.ds(blk * words_per_step + w, t, stride=G_PITCH), :]
        gates.append(lax.bitcast_convert_type(word << 16, F32))
        gates.append(lax.bitcast_convert_type(word & jnp.int32(-65536), F32))
    coef_scr[1 - rd] = (_gelu(act) * jnp.concatenate(gates, axis=1)).astype(BF16)

    for lst in first_level:
        lst.step(PEER_TOPK)
    cand, expert = _pair_candidates(first_level[0].result(), first_level[1].result())
    second_level = _TopK(cand, payload=expert)
    second_level.step(PEER_TOPK)
    best, e_sel = second_level.result()
    ex = jnp.exp(best - best[0:1])
    slot0 = pl.multiple_of(h * PEER_TOPK, PEER_TOPK)
    sel_e[cur, half, pl.ds(slot0, PEER_TOPK), :] = e_sel
    sel_g[cur, half, pl.ds(slot0, PEER_TOPK), :] = ex / jnp.sum(ex, axis=0, keepdims=True)

    @pl.when(e == n_steps - 1)
    def _():
        project_queries()


def _peer(x, g, wq, sk, u, v, t=512, eb=1024):
    n, d = x.shape
    nslot = PEER_HEADS * PEER_TOPK
    n_tok_blk = n // t
    n_exp_blk = N_KEYS * N_KEYS // eb
    n_half = t // SEL_T
    assert n_exp_blk == n_half * PEER_HEADS
    last_tok = n_tok_blk - 1
    once = pl.Buffered(1)
    return pl.pallas_call(
        _peer_fused_kernel,
        grid=(n_tok_blk + 1, n_exp_blk + 1),
        in_specs=[pl.BlockSpec((t, d), lambda i, e: (jnp.minimum(i + e // n_exp_blk, last_tok), 0),
                               pipeline_mode=once),
                  pl.BlockSpec((t, d), lambda i, e: (jnp.maximum(i - 1, 0), 0)),
                  pl.BlockSpec((1, d), lambda i, e: (0, 0)),
                  pl.BlockSpec((d, 2 * PEER_HEADS * LANES), lambda i, e: (0, 0), pipeline_mode=once),
                  pl.BlockSpec((PEER_HEADS, 2, N_KEYS, LANES), lambda i, e: (0, 0, 0, 0), pipeline_mode=once),
                  pl.BlockSpec((eb, d), lambda i, e: (jnp.minimum(e, n_exp_blk - 1), 0)),
                  pl.BlockSpec((eb, d), lambda i, e: (jnp.maximum(e - 1, 0), 0))],
        out_specs=pl.BlockSpec((t, d), lambda i, e: (jnp.maximum(i - 1, 0), 0)),
        out_shape=jax.ShapeDtypeStruct((n, d), F32),
        scratch_shapes=[pltpu.VMEM((n_half, 2 * PEER_HEADS, SEL_T, LANES), F32),
                        pltpu.VMEM((2, n_half, nslot, SEL_T), F32),
                        pltpu.VMEM((2, n_half, nslot, SEL_T), F32),
                        pltpu.VMEM((t, d), BF16),
                        pltpu.VMEM((t, nslot), F32),
                        pltpu.VMEM((t, nslot), F32),
                        pltpu.VMEM((t // LANES, nslot, LANES), F32),
                        pltpu.VMEM((t * G_PITCH, LANES), jnp.int32),
                        pltpu.VMEM((2, t, eb), BF16)],
        compiler_params=_cparams(2),
        name="peer",
    )(x, x, g.reshape(1, d), wq, sk, u, v)


def _rope_tables(pos):
    half = HEAD_DIM // 2
    inv_freq = ROPE_THETA ** (-jnp.arange(half, dtype=F32) / half)
    ang = pos.astype(F32)[:, None] * inv_freq[None, :]
    cos, sin = jnp.cos(ang), jnp.sin(ang)
    reps = LANES // HEAD_DIM
    cos_t = jnp.tile(jnp.concatenate([cos, cos], axis=1), (1, reps))
    sin_t = jnp.tile(jnp.concatenate([-sin, sin], axis=1), (1, reps))
    return cos_t, sin_t


def kernel(x_prompt, x_sample, cache_conv, cache_win_k, cache_win_v, state_mlstm_C, state_mlstm_n,
           state_mlstm_m, norm_mix, norm_ffn, ab_w_in, ab_conv_w, ab_q_gain, ab_k_gain, ab_sinks, ab_w_out,
           ml_w_in, ml_gate_bias, ml_out_gain, ml_w_out, peer_w_q, peer_sub_keys, peer_u, peer_v):
    n_batch, seq, d = x_prompt.shape
    n_seq, t_len, _ = x_sample.shape
    n_prompt = n_batch * seq
    assert d == D_MODEL and t_len == SUBLANES and norm_mix.shape[0] == 2

    x = jnp.concatenate([x_prompt.reshape(n_prompt, d), x_sample.reshape(n_seq * t_len, d)], axis=0)

    cos_p, sin_p = _rope_tables(jnp.arange(seq, dtype=jnp.int32))
    cos_s, sin_s = _rope_tables(PAST_LEN + jnp.arange(t_len, dtype=jnp.int32))
    bt = 16
    cos_s, sin_s = jnp.tile(cos_s, (bt, 1)), jnp.tile(sin_s, (bt, 1))
    lane = jnp.arange(LANES)
    seg = (lane[:, None] // HEAD_DIM == lane[None, :] // HEAD_DIM).astype(BF16)
    reps = LANES // HEAD_DIM
    qg = jnp.tile(ab_q_gain[0], reps).reshape(1, LANES)
    kg = jnp.tile(ab_k_gain[0], reps).reshape(1, LANES)
    wo_ab = ab_w_out[0].astype(BF16)

    z = _norm_proj(x, norm_mix[0], ab_w_in[0].astype(BF16))
    y_p, k_p, v_p, c_p = _ab_prompt(z, x, n_batch, seq, cos_p, sin_p, ab_conv_w[0], qg, kg, ab_sinks[0],
                                    seg, wo_ab)
    y_s, c_s, k_s, v_s = _ab_sample(z, x, n_prompt, n_seq, t_len, cos_s, sin_s, ab_conv_w[0], qg, kg,
                                    ab_sinks[0], seg, wo_ab, cache_conv[0],
                                    cache_win_k[0].reshape(n_seq, WINDOW, LANES),
                                    cache_win_v[0].reshape(n_seq, WINDOW, LANES), bt=bt)
    x = jnp.concatenate([y_p.reshape(n_prompt, d), y_s], axis=0)
    x = _peer(x, norm_ffn[0], peer_w_q[0].astype(BF16), peer_sub_keys[0].astype(BF16),
              peer_u[0].astype(BF16), peer_v[0].astype(BF16))

    n_gate = 2 * ML_HEADS
    w_in = jnp.pad(ml_w_in[0], ((0, 0), (0, ML_IN_PAD - ml_w_in.shape[2]))).astype(BF16)
    bias = jnp.pad(ml_gate_bias[0], (0, LANES - n_gate)).reshape(1, LANES)
    og = ml_out_gain[0].reshape(1, D_MODEL)
    idx = jnp.arange(ML_CHUNK)
    tril = (idx[None, :] <= idx[:, None]).astype(BF16)
    wo_ml = ml_w_out[0].astype(BF16)

    z = _norm_proj(x, norm_mix[1], w_in)
    y_p, cm_p, nm_p, mm_p = _mlstm_prompt(z, x, n_batch, seq, bias, og, tril, wo_ml)
    y_s, cm_s, nm_s, mm_s = _mlstm_sample(z, x, n_prompt, n_seq, t_len, bias, og, tril, wo_ml,
                                          state_mlstm_C[0], state_mlstm_n[0],
                                          state_mlstm_m[0].reshape(n_seq, 1, ML_HEADS))
    x = jnp.concatenate([y_p.reshape(n_prompt, d), y_s], axis=0)
    x = _peer(x, norm_ffn[1], peer_w_q[1].astype(BF16), peer_sub_keys[1].astype(BF16),
              peer_u[1].astype(BF16), peer_v[1].astype(BF16))

    y_prompt = x[:n_prompt].reshape(n_batch, seq, d)
    y_sample = x[n_prompt:].reshape(n_seq, t_len, d)
    kv_shape_p = (1, n_batch, WINDOW, N_KV_HEADS, HEAD_DIM)
    kv_shape_s = (1, n_seq, WINDOW, N_KV_HEADS, HEAD_DIM)
    return (y_prompt, y_sample,
            c_p[:, SUBLANES - 2:, :][None], k_p.reshape(kv_shape_p), v_p.reshape(kv_shape_p),
            cm_p[None], nm_p[None], mm_p.reshape(1, n_batch, ML_HEADS),
            c_s[None], k_s.reshape(kv_shape_s), v_s.reshape(kv_shape_s),
            cm_s[None], nm_s[None], mm_s.reshape(1, n_seq, ML_HEADS))
```

```python
import functools

import jax
import jax.numpy as jnp
from jax import lax
from jax.experimental import pallas as pl
from jax.experimental.pallas import tpu as pltpu

F32 = jnp.float32
BF16 = jnp.bfloat16
EPS = 1e-6

D_MODEL = 1024
CONV_DIM = 512
N_Q_HEADS = 8
N_KV_HEADS = 2
HEAD_DIM = 64
WINDOW = 128
ROPE_THETA = 10000.0
AB_IN = 2304
ML_HEADS = 4
ML_QK = 128
ML_V = 256
ML_CHUNK = 128
ML_GATE_COL = 3072
ML_IN_PAD = ML_GATE_COL + 128
N_KEYS = 128
PEER_HEADS = 8
PEER_TOPK = 16
PAST_LEN = 16384

LANES = 128
SUBLANES = 8
G_PITCH = N_KEYS // 2 + SUBLANES
VMEM_LIMIT = 56 * 1024 * 1024

NT_DIMS = (((1,), (1,)), ((), ()))
TN_DIMS = (((0,), (0,)), ((), ()))


def _cparams(n_axes, vmem=VMEM_LIMIT):
    return pltpu.CompilerParams(dimension_semantics=("arbitrary",) * n_axes, vmem_limit_bytes=vmem)


def _rmsnorm(x, g):
    return x * lax.rsqrt(jnp.mean(x * x, axis=-1, keepdims=True) + EPS) * g


def _bf16_pieces(a, terms):
    pieces = []
    rem = a
    for _ in range(terms):
        piece = rem.astype(BF16)
        rem = rem - piece.astype(F32)
        pieces.append(piece)
    return pieces


def _split_dot(a, b_bf16, terms=2):
    return sum(jnp.dot(p, b_bf16, preferred_element_type=F32) for p in _bf16_pieces(a, terms))


def _norm_proj_kernel(x_ref, g_ref, w_ref, o_ref):
    r = _rmsnorm(x_ref[...], g_ref[...])
    o_ref[...] = jnp.dot(r.astype(BF16), w_ref[...], preferred_element_type=F32)


def _norm_proj(x, g, w_bf16, tm=512):
    n, d = x.shape
    nout = w_bf16.shape[1]
    return pl.pallas_call(
        _norm_proj_kernel,
        grid=(n // tm,),
        in_specs=[pl.BlockSpec((tm, d), lambda i: (i, 0)),
                  pl.BlockSpec((1, d), lambda i: (0, 0)),
                  pl.BlockSpec((d, nout), lambda i: (0, 0))],
        out_specs=pl.BlockSpec((tm, nout), lambda i: (i, 0)),
        out_shape=jax.ShapeDtypeStruct((n, nout), F32),
        compiler_params=_cparams(1),
        name="norm_proj",
    )(x, g.reshape(1, d), w_bf16)


def _headnorm_rope(xc, gain, cos, sin, seg, hi_half):
    ss = _split_dot(xc * xc, seg)
    xn = xc * lax.rsqrt(ss * (1.0 / HEAD_DIM) + EPS) * gain
    partner = jnp.where(hi_half, pltpu.roll(xn, 32, 1), pltpu.roll(xn, 96, 1))
    return xn * cos + partner * sin


def _softmax_sink(s, mask, sink):
    s = jnp.where(mask, s, -1e30)
    m = jnp.maximum(jnp.max(s, axis=-1, keepdims=True), sink)
    p = jnp.exp(s - m)
    denom = jnp.sum(p, axis=-1, keepdims=True) + jnp.exp(sink - m)
    return (p / denom).astype(BF16)


def _ab_prompt_kernel(sink_ref, *refs, n_batch):
    z_refs, x_refs = refs[0:n_batch], refs[n_batch:2 * n_batch]
    (cos_ref, sin_ref, cw_ref, qg_ref, kg_ref, seg_ref, wo_ref,
     y_ref, kst_ref, vst_ref, cst_ref, pk_ref, pv_ref, pu_ref) = refs[2 * n_batch:]
    j = pl.program_id(0)

    @pl.when(j == 0)
    def _():
        pk_ref[...] = jnp.zeros_like(pk_ref)
        pv_ref[...] = jnp.zeros_like(pv_ref)
        pu_ref[...] = jnp.zeros_like(pu_ref)

    for b in range(n_batch):
        _ab_prompt_block(j, sink_ref, z_refs[b], x_refs[b], cos_ref, sin_ref, cw_ref, qg_ref, kg_ref, seg_ref,
                         wo_ref, y_ref.at[b], kst_ref.at[b], vst_ref.at[b], cst_ref.at[b],
                         pk_ref.at[b], pv_ref.at[b], pu_ref.at[b])


def _ab_prompt_block(j, sink_ref, z_ref, x_ref, cos_ref, sin_ref, cw_ref, qg_ref, kg_ref, seg_ref, wo_ref,
                     y_ref, kst_ref, vst_ref, cst_ref, pk_ref, pv_ref, pu_ref):
    blk = z_ref.shape[0]
    cos = cos_ref[...]
    sin = sin_ref[...]
    seg = seg_ref[...]
    hi_half = (lax.broadcasted_iota(jnp.int32, (blk, LANES), 1) & 32) != 0

    gate_b = z_ref[:, 0:CONV_DIM]
    u = z_ref[:, CONV_DIM:2 * CONV_DIM] * z_ref[:, 2 * CONV_DIM:3 * CONV_DIM]
    ng = blk // SUBLANES
    u3 = u.reshape(ng, SUBLANES, CONV_DIM)
    ext = jnp.concatenate([pu_ref[...][None], u3], axis=0)
    t8 = lax.broadcasted_iota(jnp.int32, (ng, SUBLANES, CONV_DIM), 1)
    r1 = pltpu.roll(ext, 1, 1)
    r2 = pltpu.roll(ext, 2, 1)
    um1 = jnp.where(t8 >= 1, r1[1:], r1[:-1])
    um2 = jnp.where(t8 >= 2, r2[1:], r2[:-1])
    cw = cw_ref[...]
    conv = cw[0:1][None] * um2 + cw[1:2][None] * um1 + cw[2:3][None] * u3
    yconv = gate_b * conv.reshape(blk, CONV_DIM)

    q0 = 3 * CONV_DIM
    k0 = q0 + N_Q_HEADS * HEAD_DIM
    v0 = k0 + N_KV_HEADS * HEAD_DIM
    qg = qg_ref[...]
    qr = [_headnorm_rope(z_ref[:, q0 + c * LANES:q0 + (c + 1) * LANES], qg, cos, sin, seg, hi_half)
          for c in range(N_Q_HEADS * HEAD_DIM // LANES)]
    kr = _headnorm_rope(z_ref[:, k0:k0 + LANES], kg_ref[...], cos, sin, seg, hi_half)
    v = z_ref[:, v0:v0 + LANES]
    pk = pk_ref[...]
    pv = pv_ref[...]

    row = lax.broadcasted_iota(jnp.int32, (blk, 2 * blk), 0)
    col = lax.broadcasted_iota(jnp.int32, (blk, 2 * blk), 1)
    row_prev = row + jnp.where(j == 0, 2 * blk, 0)
    mask = ((col < blk) & (col > row_prev)) | ((col >= blk) & (col - blk <= row))

    kks, vvs = [], []
    for g in range(N_KV_HEADS):
        sl = slice(g * HEAD_DIM, (g + 1) * HEAD_DIM)
        kks.append(jnp.concatenate([pk[:, sl], kr[:, sl]], axis=0).astype(BF16))
        vvs.append(jnp.concatenate([pv[:, sl], v[:, sl]], axis=0).astype(BF16))
    outs = []
    for h in range(N_Q_HEADS):
        g = h // (N_Q_HEADS // N_KV_HEADS)
        qh = qr[h // 2][:, (h % 2) * HEAD_DIM:(h % 2 + 1) * HEAD_DIM].astype(BF16)
        s = lax.dot_general(qh, kks[g], NT_DIMS, preferred_element_type=F32) * (HEAD_DIM ** -0.5)
        p = _softmax_sink(s, mask, sink_ref[h])
        outs.append(jnp.dot(p, vvs[g], preferred_element_type=F32))
    attn = jnp.concatenate(outs, axis=1)

    y = (jnp.dot(yconv.astype(BF16), wo_ref[0:CONV_DIM, :], preferred_element_type=F32)
         + jnp.dot(attn.astype(BF16), wo_ref[CONV_DIM:2 * CONV_DIM, :], preferred_element_type=F32))
    y_ref[...] = y + x_ref[...]

    pk_ref[...] = kr
    pv_ref[...] = v
    pu_ref[...] = u3[ng - 1]
    kst_ref[...] = kr
    vst_ref[...] = v
    cst_ref[...] = u3[ng - 1]


def _ab_prompt(z, x, n_batch, seq, cos, sin, cw, qg, kg, sinks, seg, wo):
    blk = WINDOW
    nb = seq // blk
    const = lambda j, s: (0, 0)
    whole = lambda j, s: (0, 0, 0)
    tok_specs = lambda width: [pl.BlockSpec((blk, width), functools.partial(lambda j, s, b: (b * nb + j, 0), b=b))
                               for b in range(n_batch)]
    grid_spec = pltpu.PrefetchScalarGridSpec(
        num_scalar_prefetch=1,
        grid=(nb,),
        in_specs=tok_specs(AB_IN) + tok_specs(D_MODEL) + [
            pl.BlockSpec((blk, LANES), lambda j, s: (j, 0)),
            pl.BlockSpec((blk, LANES), lambda j, s: (j, 0)),
            pl.BlockSpec((3, CONV_DIM), const),
            pl.BlockSpec((1, LANES), const),
            pl.BlockSpec((1, LANES), const),
            pl.BlockSpec((LANES, LANES), const),
            pl.BlockSpec((D_MODEL, D_MODEL), const)],
        out_specs=[pl.BlockSpec((n_batch, blk, D_MODEL), lambda j, s: (0, j, 0)),
                   pl.BlockSpec((n_batch, blk, LANES), whole),
                   pl.BlockSpec((n_batch, blk, LANES), whole),
                   pl.BlockSpec((n_batch, SUBLANES, CONV_DIM), whole)],
        scratch_shapes=[pltpu.VMEM((n_batch, blk, LANES), F32), pltpu.VMEM((n_batch, blk, LANES), F32),
                        pltpu.VMEM((n_batch, SUBLANES, CONV_DIM), F32)])
    return pl.pallas_call(
        functools.partial(_ab_prompt_kernel, n_batch=n_batch),
        grid_spec=grid_spec,
        out_shape=[jax.ShapeDtypeStruct((n_batch, seq, D_MODEL), F32),
                   jax.ShapeDtypeStruct((n_batch, blk, LANES), F32),
                   jax.ShapeDtypeStruct((n_batch, blk, LANES), F32),
                   jax.ShapeDtypeStruct((n_batch, SUBLANES, CONV_DIM), F32)],
        compiler_params=_cparams(1),
        name="ab_prompt",
    )(sinks, *([z] * n_batch), *([x] * n_batch), cos, sin, cw, qg, kg, seg, wo)


def _ab_sample_kernel(sink_ref, z_ref, x_ref, cos_ref, sin_ref, cw_ref, qg_ref, kg_ref, seg_ref, wo_ref,
                      cc_ref, ck_ref, cv_ref, y_ref, cs_ref, ks_ref, vs_ref):
    rows = z_ref.shape[0]
    t_len = SUBLANES
    bt = rows // t_len
    cos = cos_ref[...]
    sin = sin_ref[...]
    seg = seg_ref[...]
    hi_half = (lax.broadcasted_iota(jnp.int32, (rows, LANES), 1) & 32) != 0

    gate_b = z_ref[:, 0:CONV_DIM]
    u = z_ref[:, CONV_DIM:2 * CONV_DIM] * z_ref[:, 2 * CONV_DIM:3 * CONV_DIM]
    u3 = u.reshape(bt, t_len, CONV_DIM)
    cc = cc_ref[...]
    c0 = cc[:, 0:1, :]
    c1 = cc[:, 1:2, :]
    t8 = lax.broadcasted_iota(jnp.int32, (bt, t_len, CONV_DIM), 1)
    r1 = pltpu.roll(u3, 1, 1)
    r2 = pltpu.roll(u3, 2, 1)
    um1 = jnp.where(t8 >= 1, r1, c1)
    um2 = jnp.where(t8 >= 2, r2, jnp.where(t8 == 1, c1, c0))
    cw = cw_ref[...]
    conv = cw[0:1][None] * um2 + cw[1:2][None] * um1 + cw[2:3][None] * u3
    yconv = gate_b * conv.reshape(rows, CONV_DIM)
    cs_ref[...] = r2[:, 0:2, :]

    q0 = 3 * CONV_DIM
    k0 = q0 + N_Q_HEADS * HEAD_DIM
    v0 = k0 + N_KV_HEADS * HEAD_DIM
    qg = qg_ref[...]
    qr = [_headnorm_rope(z_ref[:, q0 + c * LANES:q0 + (c + 1) * LANES], qg, cos, sin, seg, hi_half)
          for c in range(N_Q_HEADS * HEAD_DIM // LANES)]
    kr = _headnorm_rope(z_ref[:, k0:k0 + LANES], kg_ref[...], cos, sin, seg, hi_half)
    v = z_ref[:, v0:v0 + LANES]

    group = N_Q_HEADS // N_KV_HEADS
    nq = group * t_len
    nk = 2 * WINDOW
    qrow = lax.broadcasted_iota(jnp.int32, (nq, nk), 0)
    t_q = qrow & (t_len - 1)
    col = lax.broadcasted_iota(jnp.int32, (nq, nk), 1)
    mask = (((col < WINDOW) & (col > t_q)) | ((col >= WINDOW) & (col - WINDOW <= t_q)))[None]
    hrow = lax.broadcasted_iota(jnp.int32, (nq, 1), 0) // t_len
    pad = jnp.zeros((bt, nk - WINDOW - t_len, HEAD_DIM), F32)

    outs = [None] * N_Q_HEADS
    for g in range(N_KV_HEADS):
        sl = slice(g * HEAD_DIM, (g + 1) * HEAD_DIM)
        qs = jnp.concatenate(
            [qr[h // 2][:, (h % 2) * HEAD_DIM:(h % 2 + 1) * HEAD_DIM].reshape(bt, t_len, HEAD_DIM)
             for h in range(g * group, (g + 1) * group)], axis=1)
        kk = jnp.concatenate([ck_ref[:, :, sl], kr[:, sl].reshape(bt, t_len, HEAD_DIM), pad], axis=1)
        vv = jnp.concatenate([cv_ref[:, :, sl], v[:, sl].reshape(bt, t_len, HEAD_DIM), pad], axis=1)
        s = jnp.einsum('bqd,bkd->bqk', qs.astype(BF16), kk.astype(BF16),
                       preferred_element_type=F32) * (HEAD_DIM ** -0.5)
        sink = jnp.zeros((nq, 1), F32)
        for hh in range(group):
            sink = jnp.where(hrow == hh, sink_ref[g * group + hh], sink)
        p = _softmax_sink(s, mask, sink[None])
        o = jnp.einsum('bqk,bkd->bqd', p, vv.astype(BF16), preferred_element_type=F32)
        for hh in range(group):
            outs[g * group + hh] = o[:, hh * t_len:(hh + 1) * t_len, :].reshape(rows, HEAD_DIM)
    attn = jnp.concatenate(outs, axis=1)

    y = (jnp.dot(yconv.astype(BF16), wo_ref[0:CONV_DIM, :], preferred_element_type=F32)
         + jnp.dot(attn.astype(BF16), wo_ref[CONV_DIM:2 * CONV_DIM, :], preferred_element_type=F32))
    y_ref[...] = y + x_ref[...]

    keep = WINDOW - t_len
    ks_ref[:, 0:keep, :] = ck_ref[:, t_len:WINDOW, :]
    ks_ref[:, keep:WINDOW, :] = kr.reshape(bt, t_len, LANES)
    vs_ref[:, 0:keep, :] = cv_ref[:, t_len:WINDOW, :]
    vs_ref[:, keep:WINDOW, :] = v.reshape(bt, t_len, LANES)


def _ab_sample(z, x, row0, n_seq, t_len, cos, sin, cw, qg, kg, sinks, seg, wo, cc, ck, cv, bt=16):
    rows = bt * t_len
    blk0 = row0 // rows
    tok = lambda i, s: (blk0 + i, 0)
    const = lambda i, s: (0, 0)
    seq3 = lambda i, s: (i, 0, 0)
    grid_spec = pltpu.PrefetchScalarGridSpec(
        num_scalar_prefetch=1,
        grid=(n_seq // bt,),
        in_specs=[pl.BlockSpec((rows, AB_IN), tok),
                  pl.BlockSpec((rows, D_MODEL), tok),
                  pl.BlockSpec((rows, LANES), const),
                  pl.BlockSpec((rows, LANES), const),
                  pl.BlockSpec((3, CONV_DIM), const),
                  pl.BlockSpec((1, LANES), const),
                  pl.BlockSpec((1, LANES), const),
                  pl.BlockSpec((LANES, LANES), const),
                  pl.BlockSpec((D_MODEL, D_MODEL), const),
                  pl.BlockSpec((bt, 2, CONV_DIM), seq3),
                  pl.BlockSpec((bt, WINDOW, LANES), seq3),
                  pl.BlockSpec((bt, WINDOW, LANES), seq3)],
        out_specs=[pl.BlockSpec((rows, D_MODEL), lambda i, s: (i, 0)),
                   pl.BlockSpec((bt, 2, CONV_DIM), seq3),
                   pl.BlockSpec((bt, WINDOW, LANES), seq3),
                   pl.BlockSpec((bt, WINDOW, LANES), seq3)])
    return pl.pallas_call(
        _ab_sample_kernel,
        grid_spec=grid_spec,
        out_shape=[jax.ShapeDtypeStruct((n_seq * t_len, D_MODEL), F32),
                   jax.ShapeDtypeStruct((n_seq, 2, CONV_DIM), F32),
                   jax.ShapeDtypeStruct((n_seq, WINDOW, LANES), F32),
                   jax.ShapeDtypeStruct((n_seq, WINDOW, LANES), F32)],
        compiler_params=_cparams(1),
        name="ab_sample",
    )(sinks, z, x, cos, sin, cw, qg, kg, seg, wo, cc, ck, cv)


def _log_sigmoid(x):
    return jnp.minimum(x, 0.0) - jnp.log(1.0 + jnp.exp(-jnp.abs(x)))


def _mlstm_chunk(z, bias, og, tril, c_src, n_src, m_src, c_dst, n_dst, m_dst, n_real, rows):
    L = z.shape[0]
    gates = z[:, ML_GATE_COL:ML_GATE_COL + LANES] + bias
    if n_real < L:
        live = lax.broadcasted_iota(jnp.int32, (L, LANES), 0) < n_real
        li_all = jnp.where(live, gates, -1e30)
        lf_all = jnp.where(live, _log_sigmoid(gates), 0.0)
    else:
        li_all = gates
        lf_all = _log_sigmoid(gates)
    lf_pieces = _bf16_pieces(lf_all, 3)
    f_col_all = sum(jnp.dot(tril[0:rows], p, preferred_element_type=F32) for p in lf_pieces)
    f_row_all = sum(lax.dot_general(p, tril, (((0,), (1,)), ((), ())), preferred_element_type=F32)
                    for p in lf_pieces)
    li_t = li_all.T
    rr = lax.broadcasted_iota(jnp.int32, (rows, L), 0)
    cc = lax.broadcasted_iota(jnp.int32, (rows, L), 1)
    causal = cc <= rr

    outs, m_new_all = [], []
    for h in range(ML_HEADS):
        f_col = f_col_all[:, ML_HEADS + h:ML_HEADS + h + 1]
        f_row = f_row_all[ML_HEADS + h:ML_HEADS + h + 1, :]
        li_row = li_t[h:h + 1, :]
        li_col = li_all[0:rows, h:h + 1]
        m0 = m_src[0:1, h:h + 1]
        c0 = c_src[h]
        n0 = n_src[h:h + 1, :]
        qh = z[0:rows, h * ML_QK:(h + 1) * ML_QK]
        kh = z[:, ML_HEADS * ML_QK + h * ML_QK:ML_HEADS * ML_QK + (h + 1) * ML_QK] * (ML_QK ** -0.5)
        v_off = 2 * ML_HEADS * ML_QK
        vh = z[:, v_off + h * ML_V:v_off + (h + 1) * ML_V]
        o_off = v_off + ML_HEADS * ML_V
        oh = z[0:rows, o_off + h * ML_V:o_off + (h + 1) * ML_V]
        qb = qh.astype(BF16)
        vb = vh.astype(BF16)

        dmat = jnp.where(causal, f_col - f_row + li_row, -jnp.inf)
        gcar = f_col + m0
        m_t = jnp.maximum(jnp.max(dmat, axis=-1, keepdims=True), gcar)
        w = jnp.exp(dmat - m_t)
        s = lax.dot_general(qb, kh.astype(BF16), NT_DIMS, preferred_element_type=F32) * w
        carry = jnp.exp(gcar - m_t)
        num = (jnp.dot(s.astype(BF16), vb, preferred_element_type=F32)
               + jnp.dot(qb, c0.astype(BF16), preferred_element_type=F32) * carry)
        den = jnp.sum(s, axis=-1, keepdims=True) + carry * jnp.sum(qh * n0, axis=-1, keepdims=True)
        hout = num / jnp.maximum(jnp.abs(den), jnp.exp(-m_t))

        f_last = f_col[n_real - 1:n_real, :]
        w_end = f_last - f_col + li_col
        m_new = jnp.maximum(f_last + m0, jnp.max(w_end, axis=0, keepdims=True))
        a_end = jnp.exp(w_end - m_new)
        scale = jnp.exp(f_last + m0 - m_new)
        ka = kh[0:rows] * a_end
        c_dst[h] = scale * c0 + lax.dot_general(ka.astype(BF16), vb[0:rows], TN_DIMS,
                                                preferred_element_type=F32)
        n_dst[h:h + 1, :] = scale * n0 + jnp.sum(ka, axis=0, keepdims=True)
        m_new_all.append(m_new)

        hn = _rmsnorm(hout, og[:, h * ML_V:(h + 1) * ML_V])
        outs.append(jax.nn.sigmoid(oh) * hn)
    m_dst[...] = jnp.concatenate(m_new_all, axis=1)
    return jnp.concatenate(outs, axis=1)


def _mlstm_prompt_kernel(*refs, n_batch):
    z_refs, x_refs = refs[0:n_batch], refs[n_batch:2 * n_batch]
    bias_ref, og_ref, tril_ref, wo_ref, y_ref, c_ref, n_ref, m_ref = refs[2 * n_batch:]

    @pl.when(pl.program_id(0) == 0)
    def _():
        c_ref[...] = jnp.zeros_like(c_ref)
        n_ref[...] = jnp.zeros_like(n_ref)
        m_ref[...] = jnp.zeros_like(m_ref)

    for b in range(n_batch):
        state = (c_ref.at[b], n_ref.at[b], m_ref.at[b])
        out = _mlstm_chunk(z_refs[b][...], bias_ref[...], og_ref[...], tril_ref[...], *state, *state,
                           ML_CHUNK, ML_CHUNK)
        y_ref[b] = jnp.dot(out.astype(BF16), wo_ref[...], preferred_element_type=F32) + x_refs[b][...]


def _mlstm_sample_kernel(z_ref, x_ref, bias_ref, og_ref, tril_ref, wo_ref, c0_ref, n0_ref, m0_ref,
                         y_ref, c_ref, n_ref, m_ref, *, t_len):
    n_here = z_ref.shape[0] // t_len
    q_rows = 2 * SUBLANES
    outs = []
    for s in range(n_here):
        zpad = jnp.concatenate([z_ref[s * t_len:(s + 1) * t_len, :],
                                jnp.zeros((ML_CHUNK - t_len, ML_IN_PAD), F32)], axis=0)
        out = _mlstm_chunk(zpad, bias_ref[...], og_ref[...], tril_ref[...],
                           c0_ref.at[s], n0_ref.at[s], m0_ref.at[s], c_ref.at[s], n_ref.at[s], m_ref.at[s],
                           t_len, q_rows)
        outs.append(out[0:t_len])
    out_all = jnp.concatenate(outs, axis=0).astype(BF16)
    y_ref[...] = jnp.dot(out_all, wo_ref[...], preferred_element_type=F32) + x_ref[...]


def _mlstm_weight_specs(const):
    return [pl.BlockSpec((1, LANES), const),
            pl.BlockSpec((1, D_MODEL), const),
            pl.BlockSpec((ML_CHUNK, ML_CHUNK), const),
            pl.BlockSpec((D_MODEL, D_MODEL), const)]


def _mlstm_prompt(z, x, n_batch, seq, bias, og, tril, wo):
    nc = seq // ML_CHUNK
    const = lambda j: (0, 0)
    tok_specs = lambda width: [pl.BlockSpec((ML_CHUNK, width), functools.partial(lambda j, b: (b * nc + j, 0), b=b))
                               for b in range(n_batch)]
    return pl.pallas_call(
        functools.partial(_mlstm_prompt_kernel, n_batch=n_batch),
        grid=(nc,),
        in_specs=tok_specs(ML_IN_PAD) + tok_specs(D_MODEL) + _mlstm_weight_specs(const),
        out_specs=[pl.BlockSpec((n_batch, ML_CHUNK, D_MODEL), lambda j: (0, j, 0)),
                   pl.BlockSpec((n_batch, ML_HEADS, ML_QK, ML_V), lambda j: (0, 0, 0, 0)),
                   pl.BlockSpec((n_batch, ML_HEADS, ML_QK), lambda j: (0, 0, 0)),
                   pl.BlockSpec((n_batch, 1, ML_HEADS), lambda j: (0, 0, 0))],
        out_shape=[jax.ShapeDtypeStruct((n_batch, seq, D_MODEL), F32),
                   jax.ShapeDtypeStruct((n_batch, ML_HEADS, ML_QK, ML_V), F32),
                   jax.ShapeDtypeStruct((n_batch, ML_HEADS, ML_QK), F32),
                   jax.ShapeDtypeStruct((n_batch, 1, ML_HEADS), F32)],
        compiler_params=_cparams(1),
        name="mlstm_prompt",
    )(*([z] * n_batch), *([x] * n_batch), bias, og, tril, wo)


def _mlstm_sample(z, x, row0, n_seq, t_len, bias, og, tril, wo, c0, n0, m0, seqs_per_step=8):
    rows = seqs_per_step * t_len
    blk0 = row0 // rows
    tok = lambda i: (blk0 + i, 0)
    const = lambda i: (0, 0)
    st4 = lambda i: (i, 0, 0, 0)
    st3 = lambda i: (i, 0, 0)
    state_specs = [pl.BlockSpec((seqs_per_step, ML_HEADS, ML_QK, ML_V), st4),
                   pl.BlockSpec((seqs_per_step, ML_HEADS, ML_QK), st3),
                   pl.BlockSpec((seqs_per_step, 1, ML_HEADS), st3)]
    return pl.pallas_call(
        functools.partial(_mlstm_sample_kernel, t_len=t_len),
        grid=(n_seq // seqs_per_step,),
        in_specs=[pl.BlockSpec((rows, ML_IN_PAD), tok),
                  pl.BlockSpec((rows, D_MODEL), tok)] + _mlstm_weight_specs(const) + state_specs,
        out_specs=[pl.BlockSpec((rows, D_MODEL), lambda i: (i, 0))] + state_specs,
        out_shape=[jax.ShapeDtypeStruct((n_seq * t_len, D_MODEL), F32),
                   jax.ShapeDtypeStruct((n_seq, ML_HEADS, ML_QK, ML_V), F32),
                   jax.ShapeDtypeStruct((n_seq, ML_HEADS, ML_QK), F32),
                   jax.ShapeDtypeStruct((n_seq, 1, ML_HEADS), F32)],
        compiler_params=_cparams(1),
        name="mlstm_sample",
    )(z, x, bias, og, tril, wo, c0, n0, m0)


def _gelu(x):
    return 0.5 * x * (1.0 + lax.erf(x * 0.7071067811865476))


SEL_T = 256


def _unit_scores(q_scr, sk_ref, half, h):
    return [lax.dot_general(sk_ref[h, p], q_scr[half, 2 * h + p].astype(BF16), NT_DIMS,
                            preferred_element_type=F32) for p in range(2)]


def _batcher_pairs(n):
    pairs = []
    p = 1
    while p < n:
        k = p
        while k >= 1:
            for j in range(k % p, n - k, 2 * k):
                for i in range(min(k, n - j - k)):
                    if (i + j) // (2 * p) == (i + j + k) // (2 * p):
                        pairs.append((i + j, i + j + k))
            k //= 2
        p *= 2
    return pairs


def _bitonic_merge_pairs(n):
    pairs = []
    s = n // 2
    while s >= 1:
        pairs += [(i, i + s) for i in range(n) if not i & s]
        s //= 2
    return pairs


def _top16_of_keys(s):
    n_vreg = N_KEYS // SUBLANES
    assert n_vreg == PEER_TOPK and N_KEYS == 128
    rounded = (s + 0.0).astype(BF16).astype(F32)
    b16 = lax.shift_right_logical(lax.bitcast_convert_type(rounded, jnp.int32), 16)
    code = jnp.where(b16 >= 0x8000, b16 ^ 0xFFFF, b16 | 0x8000)
    row = lax.broadcasted_iota(jnp.int32, s.shape, 0)
    keys = lax.bitcast_convert_type((code * N_KEYS + (N_KEYS - 1 - row)) | 0x4B000000, F32)

    x = [keys[v * SUBLANES:(v + 1) * SUBLANES] for v in range(n_vreg)]

    def exchange(i, j):
        x[i], x[j] = jnp.maximum(x[i], x[j]), jnp.minimum(x[i], x[j])

    for i, j in _batcher_pairs(n_vreg):
        exchange(i, j)
    for shift in (SUBLANES // 2, SUBLANES // 4, SUBLANES // 8):
        other = [pltpu.roll(a, shift, 0) for a in x]
        x = [jnp.maximum(x[i], other[n_vreg - 1 - i]) for i in range(n_vreg)]
        for i, j in _bitonic_merge_pairs(n_vreg):
            exchange(i, j)
    top = lax.bitcast_convert_type(jnp.concatenate([a[0:1] for a in x], axis=0), jnp.int32) & 0x7FFFFF
    rows = (N_KEYS - 1 - (top & (N_KEYS - 1))).astype(F32)
    code = lax.shift_right_logical(top, 7)
    b16 = jnp.where(code >= 0x8000, code ^ 0x8000, code ^ 0xFFFF)
    return lax.bitcast_convert_type(b16 << 16, F32), rows


class _TopK:
    def __init__(self, s, payload=None):
        self.s = s
        self.payload = payload
        self.rows = lax.broadcasted_iota(jnp.int32, s.shape, 0).astype(F32).astype(s.dtype)
        self.vals, self.picks = [], []

    def step(self, n):
        dt = self.s.dtype
        bound = jnp.asarray(self.s.shape[0], dt)
        for _ in range(n):
            m = jnp.max(self.s, axis=0, keepdims=True)
            first = jnp.min(jnp.where(self.s == m, self.rows, bound), axis=0, keepdims=True)
            sel = self.rows == first
            self.vals.append(m)
            if self.payload is None:
                self.picks.append(first)
            else:
                self.picks.append(jnp.max(jnp.where(sel, self.payload, -1.0), axis=0, keepdims=True))
            self.s = jnp.where(sel, jnp.asarray(-jnp.inf, dt), self.s)

    def result(self):
        return jnp.concatenate(self.vals, axis=0), jnp.concatenate(self.picks, axis=0)


def _pair_candidates(first, second):
    def pairs(a, b, combine):
        h8 = SUBLANES
        rows = [combine(a[0:1], b)]
        rows += [combine(a[k1:k1 + 1], b[0:h8]) for k1 in range(1, h8)]
        rows.append(combine(a[h8:PEER_TOPK], b[0:1]))
        return jnp.concatenate(rows, axis=0)

    (v1, i1), (v2, i2) = [tuple(a.astype(F32) for a in lst) for lst in (first, second)]
    return pairs(v1, v2, lambda a, b: a + b), pairs(i1, i2, lambda a, b: a * N_KEYS + b)


def _peer_fused_kernel(xs_ref, xm_ref, g_ref, wq_ref, sk_ref, u_ref, v_ref, o_ref,
                       q_scr, sel_e, sel_g, xn_scr, a_scr, bgt_scr, bg_scr, gmat, coef_scr):
    i = pl.program_id(0)
    e = pl.program_id(1)
    n_steps = pl.num_programs(1)
    t = xm_ref.shape[0]
    n_half = t // SEL_T
    half_keys = N_KEYS // 2
    cur = i % 2

    def project_queries():
        xn = _rmsnorm(xs_ref[...], g_ref[...]).astype(BF16)
        q = jnp.dot(xn, wq_ref[...], preferred_element_type=F32)
        for hf in range(n_half):
            for c in range(2 * PEER_HEADS):
                q_scr[hf, c] = q[hf * SEL_T:(hf + 1) * SEL_T, c * LANES:(c + 1) * LANES]

    @pl.when(e == 0)
    def _():
        @pl.when(i == 0)
        def _():
            project_queries()
            sel_e[...] = jnp.zeros_like(sel_e)
            sel_g[...] = jnp.zeros_like(sel_g)

        x = xm_ref[...]
        xn_scr[...] = _rmsnorm(x, g_ref[...]).astype(BF16)
        o_ref[...] = x
        coef_scr[0] = jnp.zeros(coef_scr.shape[1:], BF16)
        prev = 1 - cur
        lane_groups = SEL_T // LANES
        for hf in range(n_half):
            ef = sel_e[prev, hf]
            af = jnp.floor(ef * (1.0 / N_KEYS))
            bf = ef - af * N_KEYS
            gf = sel_g[prev, hf]
            a_scr[hf * SEL_T:(hf + 1) * SEL_T, :] = af.T
            bgf = bf + 0.5 * gf
            bgt_scr[hf * SEL_T:(hf + 1) * SEL_T, :] = bgf.T
            for lg in range(lane_groups):
                bg_scr[hf * lane_groups + lg] = bgf[:, lg * LANES:(lg + 1) * LANES]
        r = lax.broadcasted_iota(jnp.int32, (N_KEYS, LANES), 0)
        packed_shape = (N_KEYS // (2 * SUBLANES), 2 * SUBLANES, LANES)
        a_of_row = jnp.where(r < half_keys, 2 * r, 2 * (r - half_keys) + 1).astype(F32).astype(BF16)
        a_of_row = a_of_row.reshape(packed_shape)
        one = jnp.ones(packed_shape, BF16)
        zero = jnp.zeros(packed_shape, BF16)
        b_of_lane = lax.broadcasted_iota(jnp.int32, (N_KEYS, LANES), 1).astype(F32)
        b_of_row = r.astype(F32).astype(BF16).reshape(packed_shape)
        per_body = 32
        bodies_per_group = LANES // per_body
        row_major_every = 3

        def body(it, carry):
            grp = lax.shift_right_logical(it, bodies_per_group.bit_length() - 1)
            sub = it & (bodies_per_group - 1)
            shift = (LANES - sub * per_body) & (LANES - 1)
            bg = pltpu.roll(bg_scr[grp], shift, 1)
            tok0 = pl.multiple_of(it * per_body, per_body)
            arows = a_scr[pl.ds(tok0, per_body), :]
            bgrows = bgt_scr[pl.ds(tok0, per_body), :]
            for k in range(per_body):
                arow = jnp.broadcast_to(arows[k:k + 1], (2 * SUBLANES, LANES)).astype(BF16)[None]
                pa = jnp.where(a_of_row == arow, one, zero).reshape(N_KEYS, LANES)
                if k % row_major_every == 0:
                    bgrow = jnp.broadcast_to(bgrows[k:k + 1], (2 * SUBLANES, LANES))
                    brow = jnp.floor(bgrow)
                    grow = (2.0 * (bgrow - brow)).astype(BF16)[None]
                    qb = jnp.where(b_of_row == brow.astype(BF16)[None], grow, zero).reshape(N_KEYS, LANES)
                    tile = lax.dot_general(pa, qb, NT_DIMS, preferred_element_type=F32)
                else:
                    bgcol = jnp.broadcast_to(bg[:, k:k + 1], (N_KEYS, LANES))
                    bcol = jnp.floor(bgcol)
                    gcol = 2.0 * (bgcol - bcol)
                    qbt = jnp.where(bcol == b_of_lane, gcol, 0.0).astype(BF16)
                    tile = jnp.dot(pa, qbt, preferred_element_type=F32)
                row0 = pl.multiple_of((tok0 + k) * G_PITCH, SUBLANES)
                gmat[pl.ds(row0, half_keys), :] = pltpu.pack_elementwise(
                    [tile[0:half_keys], tile[half_keys:N_KEYS]], packed_dtype=BF16)
            return carry

        lax.fori_loop(0, t // per_body, body, 0)

    unit = jnp.minimum(e, n_half * PEER_HEADS - 1)
    half = lax.shift_right_logical(unit, PEER_HEADS.bit_length() - 1)
    h = unit & (PEER_HEADS - 1)
    rd = e & 1
    blk = jnp.minimum(e, n_steps - 2)
    eb = u_ref.shape[0]
    n_stages = 4
    exp_w = eb // n_stages
    out_w = v_ref.shape[1] // n_stages
    words_per_stage = exp_w // (2 * N_KEYS)

    def expert_chunk(c):
        ocols = slice(c * out_w, (c + 1) * out_w)
        ecols = slice(c * exp_w, (c + 1) * exp_w)
        o_ref[:, ocols] += jnp.dot(coef_scr[rd], v_ref[:, ocols], preferred_element_type=F32)
        act = lax.dot_general(xn_scr[...], u_ref[ecols, :], NT_DIMS, preferred_element_type=F32)
        gates = []
        for w in range(words_per_stage):
            word = gmat[pl.ds((blk * n_stages + c) * words_per_stage + w, t, stride=G_PITCH), :]
            gates.append(lax.bitcast_convert_type(word << 16, F32))
            gates.append(lax.bitcast_convert_type(word & jnp.int32(-65536), F32))
        coef_scr[1 - rd, :, ecols] = (_gelu(act) * jnp.concatenate(gates, axis=1)).astype(BF16)

    def first_level(p):
        st = lax.dot_general(sk_ref[h, p], q_scr[half, 2 * h + p].astype(BF16), NT_DIMS,
                             preferred_element_type=F32)
        return _top16_of_keys(st)

    expert_chunk(0)
    first = first_level(0)
    expert_chunk(1)
    second = first_level(1)
    expert_chunk(2)
    cand, expert = _pair_candidates(first, second)
    second_level = _TopK(cand, payload=expert)
    second_level.step(PEER_TOPK)
    expert_chunk(3)
    best, e_sel = second_level.result()
    ex = jnp.exp(best - best[0:1])
    slot0 = pl.multiple_of(h * PEER_TOPK, PEER_TOPK)
    sel_e[cur, half, pl.ds(slot0, PEER_TOPK), :] = e_sel
    sel_g[cur, half, pl.ds(slot0, PEER_TOPK), :] = ex / jnp.sum(ex, axis=0, keepdims=True)

    @pl.when(e == n_steps - 1)
    def _():
        project_queries()


def _peer(x, g, wq, sk, u, v, t=512, eb=1024):
    n, d = x.shape
    nslot = PEER_HEADS * PEER_TOPK
    n_tok_blk = n // t
    n_exp_blk = N_KEYS * N_KEYS // eb
    n_half = t // SEL_T
    assert n_exp_blk == n_half * PEER_HEADS
    last_tok = n_tok_blk - 1
    once = pl.Buffered(1)
    return pl.pallas_call(
        _peer_fused_kernel,
        grid=(n_tok_blk + 1, n_exp_blk + 1),
        in_specs=[pl.BlockSpec((t, d), lambda i, e: (jnp.minimum(i + e // n_exp_blk, last_tok), 0),
                               pipeline_mode=once),
                  pl.BlockSpec((t, d), lambda i, e: (jnp.maximum(i - 1, 0), 0)),
                  pl.BlockSpec((1, d), lambda i, e: (0, 0)),
                  pl.BlockSpec((d, 2 * PEER_HEADS * LANES), lambda i, e: (0, 0), pipeline_mode=once),
                  pl.BlockSpec((PEER_HEADS, 2, N_KEYS, LANES), lambda i, e: (0, 0, 0, 0), pipeline_mode=once),
                  pl.BlockSpec((eb, d), lambda i, e: (jnp.minimum(e, n_exp_blk - 1), 0)),
                  pl.BlockSpec((eb, d), lambda i, e: (jnp.maximum(e - 1, 0), 0))],
        out_specs=pl.BlockSpec((t, d), lambda i, e: (jnp.maximum(i - 1, 0), 0)),
        out_shape=jax.ShapeDtypeStruct((n, d), F32),
        scratch_shapes=[pltpu.VMEM((n_half, 2 * PEER_HEADS, SEL_T, LANES), F32),
                        pltpu.VMEM((2, n_half, nslot, SEL_T), F32),
                        pltpu.VMEM((2, n_half, nslot, SEL_T), F32),
                        pltpu.VMEM((t, d), BF16),
                        pltpu.VMEM((t, nslot), F32),
                        pltpu.VMEM((t, nslot), F32),
                        pltpu.VMEM((t // LANES, nslot, LANES), F32),
                        pltpu.VMEM((t * G_PITCH, LANES), jnp.int32),
                        pltpu.VMEM((2, t, eb), BF16)],
        compiler_params=_cparams(2),
        name="peer",
    )(x, x, g.reshape(1, d), wq, sk, u, v)


def _rope_tables(pos):
    half = HEAD_DIM // 2
    inv_freq = ROPE_THETA ** (-jnp.arange(half, dtype=F32) / half)
    ang = pos.astype(F32)[:, None] * inv_freq[None, :]
    cos, sin = jnp.cos(ang), jnp.sin(ang)
    reps = LANES // HEAD_DIM
    cos_t = jnp.tile(jnp.concatenate([cos, cos], axis=1), (1, reps))
    sin_t = jnp.tile(jnp.concatenate([-sin, sin], axis=1), (1, reps))
    return cos_t, sin_t


def kernel(x_prompt, x_sample, cache_conv, cache_win_k, cache_win_v, state_mlstm_C, state_mlstm_n,
           state_mlstm_m, norm_mix, norm_ffn, ab_w_in, ab_conv_w, ab_q_gain, ab_k_gain, ab_sinks, ab_w_out,
           ml_w_in, ml_gate_bias, ml_out_gain, ml_w_out, peer_w_q, peer_sub_keys, peer_u, peer_v):
    n_batch, seq, d = x_prompt.shape
    n_seq, t_len, _ = x_sample.shape
    n_prompt = n_batch * seq
    assert d == D_MODEL and t_len == SUBLANES and norm_mix.shape[0] == 2

    x = jnp.concatenate([x_prompt.reshape(n_prompt, d), x_sample.reshape(n_seq * t_len, d)], axis=0)

    cos_p, sin_p = _rope_tables(jnp.arange(seq, dtype=jnp.int32))
    cos_s, sin_s = _rope_tables(PAST_LEN + jnp.arange(t_len, dtype=jnp.int32))
    bt = 16
    cos_s, sin_s = jnp.tile(cos_s, (bt, 1)), jnp.tile(sin_s, (bt, 1))
    lane = jnp.arange(LANES)
    seg = (lane[:, None] // HEAD_DIM == lane[None, :] // HEAD_DIM).astype(BF16)
    reps = LANES // HEAD_DIM
    qg = jnp.tile(ab_q_gain[0], reps).reshape(1, LANES)
    kg = jnp.tile(ab_k_gain[0], reps).reshape(1, LANES)
    wo_ab = ab_w_out[0].astype(BF16)

    z = _norm_proj(x, norm_mix[0], ab_w_in[0].astype(BF16))
    y_p, k_p, v_p, c_p = _ab_prompt(z, x, n_batch, seq, cos_p, sin_p, ab_conv_w[0], qg, kg, ab_sinks[0],
                                    seg, wo_ab)
    y_s, c_s, k_s, v_s = _ab_sample(z, x, n_prompt, n_seq, t_len, cos_s, sin_s, ab_conv_w[0], qg, kg,
                                    ab_sinks[0], seg, wo_ab, cache_conv[0],
                                    cache_win_k[0].reshape(n_seq, WINDOW, LANES),
                                    cache_win_v[0].reshape(n_seq, WINDOW, LANES), bt=bt)
    x = jnp.concatenate([y_p.reshape(n_prompt, d), y_s], axis=0)
    x = _peer(x, norm_ffn[0], peer_w_q[0].astype(BF16), peer_sub_keys[0].astype(BF16),
              peer_u[0].astype(BF16), peer_v[0].astype(BF16))

    n_gate = 2 * ML_HEADS
    w_in = jnp.pad(ml_w_in[0], ((0, 0), (0, ML_IN_PAD - ml_w_in.shape[2]))).astype(BF16)
    bias = jnp.pad(ml_gate_bias[0], (0, LANES - n_gate)).reshape(1, LANES)
    og = ml_out_gain[0].reshape(1, D_MODEL)
    idx = jnp.arange(ML_CHUNK)
    tril = (idx[None, :] <= idx[:, None]).astype(BF16)
    wo_ml = ml_w_out[0].astype(BF16)

    z = _norm_proj(x, norm_mix[1], w_in)
    y_p, cm_p, nm_p, mm_p = _mlstm_prompt(z, x, n_batch, seq, bias, og, tril, wo_ml)
    y_s, cm_s, nm_s, mm_s = _mlstm_sample(z, x, n_prompt, n_seq, t_len, bias, og, tril, wo_ml,
                                          state_mlstm_C[0], state_mlstm_n[0],
                                          state_mlstm_m[0].reshape(n_seq, 1, ML_HEADS))
    x = jnp.concatenate([y_p.reshape(n_prompt, d), y_s], axis=0)
    x = _peer(x, norm_ffn[1], peer_w_q[1].astype(BF16), peer_sub_keys[1].astype(BF16),
              peer_u[1].astype(BF16), peer_v[1].astype(BF16))

    y_prompt = x[:n_prompt].reshape(n_batch, seq, d)
    y_sample = x[n_prompt:].reshape(n_seq, t_len, d)
    kv_shape_p = (1, n_batch, WINDOW, N_KV_HEADS, HEAD_DIM)
    kv_shape_s = (1, n_seq, WINDOW, N_KV_HEADS, HEAD_DIM)
    return (y_prompt, y_sample,
            c_p[:, SUBLANES - 2:, :][None], k_p.reshape(kv_shape_p), v_p.reshape(kv_shape_p),
            cm_p[None], nm_p[None], mm_p.reshape(1, n_batch, ML_HEADS),
            c_s[None], k_s.reshape(kv_shape_s), v_s.reshape(kv_shape_s),
            cm_s[None], nm_s[None], mm_s.reshape(1, n_seq, ML_HEADS))
```

```python
import functools

import jax
import jax.numpy as jnp
from jax import lax
from jax.experimental import pallas as pl
from jax.experimental.pallas import tpu as pltpu

F32 = jnp.float32
BF16 = jnp.bfloat16
EPS = 1e-6

D_MODEL = 1024
CONV_DIM = 512
N_Q_HEADS = 8
N_KV_HEADS = 2
HEAD_DIM = 64
WINDOW = 128
ROPE_THETA = 10000.0
AB_IN = 2304
ML_HEADS = 4
ML_QK = 128
ML_V = 256
ML_CHUNK = 128
ML_GATE_COL = 3072
ML_IN_PAD = ML_GATE_COL + 128
N_KEYS = 128
PEER_HEADS = 8
PEER_TOPK = 16
PAST_LEN = 16384

LANES = 128
SUBLANES = 8
G_PITCH = N_KEYS // 2 + SUBLANES
VMEM_LIMIT = 56 * 1024 * 1024

NT_DIMS = (((1,), (1,)), ((), ()))
TN_DIMS = (((0,), (0,)), ((), ()))


def _cparams(n_axes, vmem=VMEM_LIMIT):
    return pltpu.CompilerParams(dimension_semantics=("arbitrary",) * n_axes, vmem_limit_bytes=vmem)


def _rmsnorm(x, g):
    return x * lax.rsqrt(jnp.mean(x * x, axis=-1, keepdims=True) + EPS) * g


def _bf16_pieces(a, terms):
    pieces = []
    rem = a
    for _ in range(terms):
        piece = rem.astype(BF16)
        rem = rem - piece.astype(F32)
        pieces.append(piece)
    return pieces


def _split_dot(a, b_bf16, terms=2):
    return sum(jnp.dot(p, b_bf16, preferred_element_type=F32) for p in _bf16_pieces(a, terms))


def _norm_proj_kernel(*refs, starts):
    x_refs = refs[:len(starts)]
    g_ref, w_ref, o_ref = refs[len(starts):]
    i = pl.program_id(0)
    x = x_refs[0][...]
    for x_ref, start in zip(x_refs[1:], starts[1:]):
        x = jnp.where(i >= start, x_ref[...], x)
    r = _rmsnorm(x, g_ref[...])
    o_ref[...] = jnp.dot(r.astype(BF16), w_ref[...], preferred_element_type=F32)


def _norm_proj(x_parts, g, w_bf16, tm=512):
    d = x_parts[0].shape[1]
    nout = w_bf16.shape[1]
    blocks = [p.shape[0] // tm for p in x_parts]
    starts = tuple(sum(blocks[:k]) for k in range(len(blocks)))
    part_spec = lambda start, nblk: pl.BlockSpec((tm, d), lambda i: (jnp.clip(i - start, 0, nblk - 1), 0))
    return pl.pallas_call(
        functools.partial(_norm_proj_kernel, starts=starts),
        grid=(sum(blocks),),
        in_specs=[part_spec(s, nb) for s, nb in zip(starts, blocks)] + [
            pl.BlockSpec((1, d), lambda i: (0, 0)),
            pl.BlockSpec((d, nout), lambda i: (0, 0))],
        out_specs=pl.BlockSpec((tm, nout), lambda i: (i, 0)),
        out_shape=jax.ShapeDtypeStruct((sum(blocks) * tm, nout), F32),
        compiler_params=_cparams(1),
        name="norm_proj",
    )(*x_parts, g.reshape(1, d), w_bf16)


def _headnorm_rope(xc, gain, cos, sin, seg, hi_half):
    ss = _split_dot(xc * xc, seg)
    xn = xc * lax.rsqrt(ss * (1.0 / HEAD_DIM) + EPS) * gain
    partner = jnp.where(hi_half, pltpu.roll(xn, 32, 1), pltpu.roll(xn, 96, 1))
    return xn * cos + partner * sin


def _softmax_sink(s, mask, sink):
    s = jnp.where(mask, s, -1e30)
    m = jnp.maximum(jnp.max(s, axis=-1, keepdims=True), sink)
    p = jnp.exp(s - m)
    denom = jnp.sum(p, axis=-1, keepdims=True) + jnp.exp(sink - m)
    return (p / denom).astype(BF16)


def _ab_prompt_kernel(sink_ref, *refs, n_batch):
    z_refs, x_refs = refs[0:n_batch], refs[n_batch:2 * n_batch]
    (cos_ref, sin_ref, cw_ref, qg_ref, kg_ref, seg_ref, wo_ref,
     y_ref, kst_ref, vst_ref, cst_ref, pk_ref, pv_ref, pu_ref) = refs[2 * n_batch:]
    j = pl.program_id(0)

    @pl.when(j == 0)
    def _():
        pk_ref[...] = jnp.zeros_like(pk_ref)
        pv_ref[...] = jnp.zeros_like(pv_ref)
        pu_ref[...] = jnp.zeros_like(pu_ref)

    for b in range(n_batch):
        _ab_prompt_block(j, sink_ref, z_refs[b], x_refs[b], cos_ref, sin_ref, cw_ref, qg_ref, kg_ref, seg_ref,
                         wo_ref, y_ref.at[b], kst_ref.at[b], vst_ref.at[b], cst_ref.at[b],
                         pk_ref.at[b], pv_ref.at[b], pu_ref.at[b])


def _ab_prompt_block(j, sink_ref, z_ref, x_ref, cos_ref, sin_ref, cw_ref, qg_ref, kg_ref, seg_ref, wo_ref,
                     y_ref, kst_ref, vst_ref, cst_ref, pk_ref, pv_ref, pu_ref):
    blk = z_ref.shape[0]
    cos = cos_ref[...]
    sin = sin_ref[...]
    seg = seg_ref[...]
    hi_half = (lax.broadcasted_iota(jnp.int32, (blk, LANES), 1) & 32) != 0

    gate_b = z_ref[:, 0:CONV_DIM]
    u = z_ref[:, CONV_DIM:2 * CONV_DIM] * z_ref[:, 2 * CONV_DIM:3 * CONV_DIM]
    ng = blk // SUBLANES
    u3 = u.reshape(ng, SUBLANES, CONV_DIM)
    ext = jnp.concatenate([pu_ref[...][None], u3], axis=0)
    t8 = lax.broadcasted_iota(jnp.int32, (ng, SUBLANES, CONV_DIM), 1)
    r1 = pltpu.roll(ext, 1, 1)
    r2 = pltpu.roll(ext, 2, 1)
    um1 = jnp.where(t8 >= 1, r1[1:], r1[:-1])
    um2 = jnp.where(t8 >= 2, r2[1:], r2[:-1])
    cw = cw_ref[...]
    conv = cw[0:1][None] * um2 + cw[1:2][None] * um1 + cw[2:3][None] * u3
    yconv = gate_b * conv.reshape(blk, CONV_DIM)

    q0 = 3 * CONV_DIM
    k0 = q0 + N_Q_HEADS * HEAD_DIM
    v0 = k0 + N_KV_HEADS * HEAD_DIM
    qg = qg_ref[...]
    qr = [_headnorm_rope(z_ref[:, q0 + c * LANES:q0 + (c + 1) * LANES], qg, cos, sin, seg, hi_half)
          for c in range(N_Q_HEADS * HEAD_DIM // LANES)]
    kr = _headnorm_rope(z_ref[:, k0:k0 + LANES], kg_ref[...], cos, sin, seg, hi_half)
    v = z_ref[:, v0:v0 + LANES]
    pk = pk_ref[...]
    pv = pv_ref[...]

    row = lax.broadcasted_iota(jnp.int32, (blk, 2 * blk), 0)
    col = lax.broadcasted_iota(jnp.int32, (blk, 2 * blk), 1)
    row_prev = row + jnp.where(j == 0, 2 * blk, 0)
    mask = ((col < blk) & (col > row_prev)) | ((col >= blk) & (col - blk <= row))

    kks, vvs = [], []
    for g in range(N_KV_HEADS):
        sl = slice(g * HEAD_DIM, (g + 1) * HEAD_DIM)
        kks.append(jnp.concatenate([pk[:, sl], kr[:, sl]], axis=0).astype(BF16))
        vvs.append(jnp.concatenate([pv[:, sl], v[:, sl]], axis=0).astype(BF16))
    outs = []
    for h in range(N_Q_HEADS):
        g = h // (N_Q_HEADS // N_KV_HEADS)
        qh = qr[h // 2][:, (h % 2) * HEAD_DIM:(h % 2 + 1) * HEAD_DIM].astype(BF16)
        s = lax.dot_general(qh, kks[g], NT_DIMS, preferred_element_type=F32) * (HEAD_DIM ** -0.5)
        p = _softmax_sink(s, mask, sink_ref[h])
        outs.append(jnp.dot(p, vvs[g], preferred_element_type=F32))
    attn = jnp.concatenate(outs, axis=1)

    y = (jnp.dot(yconv.astype(BF16), wo_ref[0:CONV_DIM, :], preferred_element_type=F32)
         + jnp.dot(attn.astype(BF16), wo_ref[CONV_DIM:2 * CONV_DIM, :], preferred_element_type=F32))
    y_ref[...] = y + x_ref[...]

    pk_ref[...] = kr
    pv_ref[...] = v
    pu_ref[...] = u3[ng - 1]
    kst_ref[...] = kr
    vst_ref[...] = v
    cst_ref[...] = u3[ng - 1]


def _ab_prompt(z, x, n_batch, seq, cos, sin, cw, qg, kg, sinks, seg, wo):
    blk = WINDOW
    nb = seq // blk
    const = lambda j, s: (0, 0)
    whole = lambda j, s: (0, 0, 0)
    tok_specs = lambda width: [pl.BlockSpec((blk, width), functools.partial(lambda j, s, b: (b * nb + j, 0), b=b))
                               for b in range(n_batch)]
    grid_spec = pltpu.PrefetchScalarGridSpec(
        num_scalar_prefetch=1,
        grid=(nb,),
        in_specs=tok_specs(AB_IN) + tok_specs(D_MODEL) + [
            pl.BlockSpec((blk, LANES), lambda j, s: (j, 0)),
            pl.BlockSpec((blk, LANES), lambda j, s: (j, 0)),
            pl.BlockSpec((3, CONV_DIM), const),
            pl.BlockSpec((1, LANES), const),
            pl.BlockSpec((1, LANES), const),
            pl.BlockSpec((LANES, LANES), const),
            pl.BlockSpec((D_MODEL, D_MODEL), const)],
        out_specs=[pl.BlockSpec((n_batch, blk, D_MODEL), lambda j, s: (0, j, 0)),
                   pl.BlockSpec((n_batch, blk, LANES), whole),
                   pl.BlockSpec((n_batch, blk, LANES), whole),
                   pl.BlockSpec((n_batch, SUBLANES, CONV_DIM), whole)],
        scratch_shapes=[pltpu.VMEM((n_batch, blk, LANES), F32), pltpu.VMEM((n_batch, blk, LANES), F32),
                        pltpu.VMEM((n_batch, SUBLANES, CONV_DIM), F32)])
    return pl.pallas_call(
        functools.partial(_ab_prompt_kernel, n_batch=n_batch),
        grid_spec=grid_spec,
        out_shape=[jax.ShapeDtypeStruct((n_batch, seq, D_MODEL), F32),
                   jax.ShapeDtypeStruct((n_batch, blk, LANES), F32),
                   jax.ShapeDtypeStruct((n_batch, blk, LANES), F32),
                   jax.ShapeDtypeStruct((n_batch, SUBLANES, CONV_DIM), F32)],
        compiler_params=_cparams(1),
        name="ab_prompt",
    )(sinks, *([z] * n_batch), *([x] * n_batch), cos, sin, cw, qg, kg, seg, wo)


def _ab_sample_kernel(sink_ref, z_ref, x_ref, cos_ref, sin_ref, cw_ref, qg_ref, kg_ref, seg_ref, wo_ref,
                      cc_ref, ck_ref, cv_ref, y_ref, cs_ref, ks_ref, vs_ref):
    rows = z_ref.shape[0]
    t_len = SUBLANES
    bt = rows // t_len
    cos = cos_ref[...]
    sin = sin_ref[...]
    seg = seg_ref[...]
    hi_half = (lax.broadcasted_iota(jnp.int32, (rows, LANES), 1) & 32) != 0

    gate_b = z_ref[:, 0:CONV_DIM]
    u = z_ref[:, CONV_DIM:2 * CONV_DIM] * z_ref[:, 2 * CONV_DIM:3 * CONV_DIM]
    u3 = u.reshape(bt, t_len, CONV_DIM)
    cc = cc_ref[...]
    c0 = cc[:, 0:1, :]
    c1 = cc[:, 1:2, :]
    t8 = lax.broadcasted_iota(jnp.int32, (bt, t_len, CONV_DIM), 1)
    r1 = pltpu.roll(u3, 1, 1)
    r2 = pltpu.roll(u3, 2, 1)
    um1 = jnp.where(t8 >= 1, r1, c1)
    um2 = jnp.where(t8 >= 2, r2, jnp.where(t8 == 1, c1, c0))
    cw = cw_ref[...]
    conv = cw[0:1][None] * um2 + cw[1:2][None] * um1 + cw[2:3][None] * u3
    yconv = gate_b * conv.reshape(rows, CONV_DIM)
    cs_ref[...] = r2[:, 0:2, :]

    q0 = 3 * CONV_DIM
    k0 = q0 + N_Q_HEADS * HEAD_DIM
    v0 = k0 + N_KV_HEADS * HEAD_DIM
    qg = qg_ref[...]
    qr = [_headnorm_rope(z_ref[:, q0 + c * LANES:q0 + (c + 1) * LANES], qg, cos, sin, seg, hi_half)
          for c in range(N_Q_HEADS * HEAD_DIM // LANES)]
    kr = _headnorm_rope(z_ref[:, k0:k0 + LANES], kg_ref[...], cos, sin, seg, hi_half)
    v = z_ref[:, v0:v0 + LANES]

    group = N_Q_HEADS // N_KV_HEADS
    nq = group * t_len
    nk = 2 * WINDOW
    qrow = lax.broadcasted_iota(jnp.int32, (nq, nk), 0)
    t_q = qrow & (t_len - 1)
    col = lax.broadcasted_iota(jnp.int32, (nq, nk), 1)
    mask = (((col < WINDOW) & (col > t_q)) | ((col >= WINDOW) & (col - WINDOW <= t_q)))[None]
    hrow = lax.broadcasted_iota(jnp.int32, (nq, 1), 0) // t_len
    pad = jnp.zeros((bt, nk - WINDOW - t_len, HEAD_DIM), F32)

    outs = [None] * N_Q_HEADS
    for g in range(N_KV_HEADS):
        sl = slice(g * HEAD_DIM, (g + 1) * HEAD_DIM)
        qs = jnp.concatenate(
            [qr[h // 2][:, (h % 2) * HEAD_DIM:(h % 2 + 1) * HEAD_DIM].reshape(bt, t_len, HEAD_DIM)
             for h in range(g * group, (g + 1) * group)], axis=1)
        kk = jnp.concatenate([ck_ref[:, :, sl], kr[:, sl].reshape(bt, t_len, HEAD_DIM), pad], axis=1)
        vv = jnp.concatenate([cv_ref[:, :, sl], v[:, sl].reshape(bt, t_len, HEAD_DIM), pad], axis=1)
        s = jnp.einsum('bqd,bkd->bqk', qs.astype(BF16), kk.astype(BF16),
                       preferred_element_type=F32) * (HEAD_DIM ** -0.5)
        sink = jnp.zeros((nq, 1), F32)
        for hh in range(group):
            sink = jnp.where(hrow == hh, sink_ref[g * group + hh], sink)
        p = _softmax_sink(s, mask, sink[None])
        o = jnp.einsum('bqk,bkd->bqd', p, vv.astype(BF16), preferred_element_type=F32)
        for hh in range(group):
            outs[g * group + hh] = o[:, hh * t_len:(hh + 1) * t_len, :].reshape(rows, HEAD_DIM)
    attn = jnp.concatenate(outs, axis=1)

    y = (jnp.dot(yconv.astype(BF16), wo_ref[0:CONV_DIM, :], preferred_element_type=F32)
         + jnp.dot(attn.astype(BF16), wo_ref[CONV_DIM:2 * CONV_DIM, :], preferred_element_type=F32))
    y_ref[...] = y + x_ref[...]

    keep = WINDOW - t_len
    ks_ref[:, 0:keep, :] = ck_ref[:, t_len:WINDOW, :]
    ks_ref[:, keep:WINDOW, :] = kr.reshape(bt, t_len, LANES)
    vs_ref[:, 0:keep, :] = cv_ref[:, t_len:WINDOW, :]
    vs_ref[:, keep:WINDOW, :] = v.reshape(bt, t_len, LANES)


def _ab_sample(z, x, row0, n_seq, t_len, cos, sin, cw, qg, kg, sinks, seg, wo, cc, ck, cv, bt=16):
    rows = bt * t_len
    blk0 = row0 // rows
    tok = lambda i, s: (blk0 + i, 0)
    const = lambda i, s: (0, 0)
    seq3 = lambda i, s: (i, 0, 0)
    grid_spec = pltpu.PrefetchScalarGridSpec(
        num_scalar_prefetch=1,
        grid=(n_seq // bt,),
        in_specs=[pl.BlockSpec((rows, AB_IN), tok),
                  pl.BlockSpec((rows, D_MODEL), lambda i, s: (i, 0)),
                  pl.BlockSpec((rows, LANES), const),
                  pl.BlockSpec((rows, LANES), const),
                  pl.BlockSpec((3, CONV_DIM), const),
                  pl.BlockSpec((1, LANES), const),
                  pl.BlockSpec((1, LANES), const),
                  pl.BlockSpec((LANES, LANES), const),
                  pl.BlockSpec((D_MODEL, D_MODEL), const),
                  pl.BlockSpec((bt, 2, CONV_DIM), seq3),
                  pl.BlockSpec((bt, WINDOW, LANES), seq3),
                  pl.BlockSpec((bt, WINDOW, LANES), seq3)],
        out_specs=[pl.BlockSpec((rows, D_MODEL), lambda i, s: (i, 0)),
                   pl.BlockSpec((bt, 2, CONV_DIM), seq3),
                   pl.BlockSpec((bt, WINDOW, LANES), seq3),
                   pl.BlockSpec((bt, WINDOW, LANES), seq3)])
    return pl.pallas_call(
        _ab_sample_kernel,
        grid_spec=grid_spec,
        out_shape=[jax.ShapeDtypeStruct((n_seq * t_len, D_MODEL), F32),
                   jax.ShapeDtypeStruct((n_seq, 2, CONV_DIM), F32),
                   jax.ShapeDtypeStruct((n_seq, WINDOW, LANES), F32),
                   jax.ShapeDtypeStruct((n_seq, WINDOW, LANES), F32)],
        compiler_params=_cparams(1),
        name="ab_sample",
    )(sinks, z, x, cos, sin, cw, qg, kg, seg, wo, cc, ck, cv)


def _log_sigmoid(x):
    return jnp.minimum(x, 0.0) - jnp.log(1.0 + jnp.exp(-jnp.abs(x)))


def _mlstm_chunk(z, bias, og, tril, c_src, n_src, m_src, c_dst, n_dst, m_dst, n_real, rows):
    L = z.shape[0]
    gates = z[:, ML_GATE_COL:ML_GATE_COL + LANES] + bias
    if n_real < L:
        live = lax.broadcasted_iota(jnp.int32, (L, LANES), 0) < n_real
        li_all = jnp.where(live, gates, -1e30)
        lf_all = jnp.where(live, _log_sigmoid(gates), 0.0)
    else:
        li_all = gates
        lf_all = _log_sigmoid(gates)
    lf_pieces = _bf16_pieces(lf_all, 3)
    f_col_all = sum(jnp.dot(tril[0:rows], p, preferred_element_type=F32) for p in lf_pieces)
    f_row_all = sum(lax.dot_general(p, tril, (((0,), (1,)), ((), ())), preferred_element_type=F32)
                    for p in lf_pieces)
    li_t = li_all.T
    rr = lax.broadcasted_iota(jnp.int32, (rows, L), 0)
    cc = lax.broadcasted_iota(jnp.int32, (rows, L), 1)
    causal = cc <= rr

    outs, m_new_all = [], []
    for h in range(ML_HEADS):
        f_col = f_col_all[:, ML_HEADS + h:ML_HEADS + h + 1]
        f_row = f_row_all[ML_HEADS + h:ML_HEADS + h + 1, :]
        li_row = li_t[h:h + 1, :]
        li_col = li_all[0:rows, h:h + 1]
        m0 = m_src[0:1, h:h + 1]
        c0 = c_src[h]
        n0 = n_src[h:h + 1, :]
        qh = z[0:rows, h * ML_QK:(h + 1) * ML_QK]
        kh = z[:, ML_HEADS * ML_QK + h * ML_QK:ML_HEADS * ML_QK + (h + 1) * ML_QK] * (ML_QK ** -0.5)
        v_off = 2 * ML_HEADS * ML_QK
        vh = z[:, v_off + h * ML_V:v_off + (h + 1) * ML_V]
        o_off = v_off + ML_HEADS * ML_V
        oh = z[0:rows, o_off + h * ML_V:o_off + (h + 1) * ML_V]
        qb = qh.astype(BF16)
        vb = vh.astype(BF16)

        dmat = jnp.where(causal, f_col - f_row + li_row, -jnp.inf)
        gcar = f_col + m0
        m_t = jnp.maximum(jnp.max(dmat, axis=-1, keepdims=True), gcar)
        w = jnp.exp(dmat - m_t)
        s = lax.dot_general(qb, kh.astype(BF16), NT_DIMS, preferred_element_type=F32) * w
        carry = jnp.exp(gcar - m_t)
        num = (jnp.dot(s.astype(BF16), vb, preferred_element_type=F32)
               + jnp.dot(qb, c0.astype(BF16), preferred_element_type=F32) * carry)
        den = jnp.sum(s, axis=-1, keepdims=True) + carry * jnp.sum(qh * n0, axis=-1, keepdims=True)
        hout = num / jnp.maximum(jnp.abs(den), jnp.exp(-m_t))

        f_last = f_col[n_real - 1:n_real, :]
        w_end = f_last - f_col + li_col
        m_new = jnp.maximum(f_last + m0, jnp.max(w_end, axis=0, keepdims=True))
        a_end = jnp.exp(w_end - m_new)
        scale = jnp.exp(f_last + m0 - m_new)
        ka = kh[0:rows] * a_end
        c_dst[h] = scale * c0 + lax.dot_general(ka.astype(BF16), vb[0:rows], TN_DIMS,
                                                preferred_element_type=F32)
        n_dst[h:h + 1, :] = scale * n0 + jnp.sum(ka, axis=0, keepdims=True)
        m_new_all.append(m_new)

        hn = _rmsnorm(hout, og[:, h * ML_V:(h + 1) * ML_V])
        outs.append(jax.nn.sigmoid(oh) * hn)
    m_dst[...] = jnp.concatenate(m_new_all, axis=1)
    return jnp.concatenate(outs, axis=1)


def _mlstm_prompt_kernel(*refs, n_batch):
    z_refs, x_refs = refs[0:n_batch], refs[n_batch:2 * n_batch]
    bias_ref, og_ref, tril_ref, wo_ref, y_ref, c_ref, n_ref, m_ref = refs[2 * n_batch:]

    @pl.when(pl.program_id(0) == 0)
    def _():
        c_ref[...] = jnp.zeros_like(c_ref)
        n_ref[...] = jnp.zeros_like(n_ref)
        m_ref[...] = jnp.zeros_like(m_ref)

    for b in range(n_batch):
        state = (c_ref.at[b], n_ref.at[b], m_ref.at[b])
        out = _mlstm_chunk(z_refs[b][...], bias_ref[...], og_ref[...], tril_ref[...], *state, *state,
                           ML_CHUNK, ML_CHUNK)
        y_ref[b] = jnp.dot(out.astype(BF16), wo_ref[...], preferred_element_type=F32) + x_refs[b][...]


def _mlstm_sample_kernel(z_ref, x_ref, bias_ref, og_ref, tril_ref, wo_ref, c0_ref, n0_ref, m0_ref,
                         y_ref, c_ref, n_ref, m_ref, *, t_len):
    n_here = z_ref.shape[0] // t_len
    q_rows = 2 * SUBLANES
    outs = []
    for s in range(n_here):
        zpad = jnp.concatenate([z_ref[s * t_len:(s + 1) * t_len, :],
                                jnp.zeros((ML_CHUNK - t_len, ML_IN_PAD), F32)], axis=0)
        out = _mlstm_chunk(zpad, bias_ref[...], og_ref[...], tril_ref[...],
                           c0_ref.at[s], n0_ref.at[s], m0_ref.at[s], c_ref.at[s], n_ref.at[s], m_ref.at[s],
                           t_len, q_rows)
        outs.append(out[0:t_len])
    out_all = jnp.concatenate(outs, axis=0).astype(BF16)
    y_ref[...] = jnp.dot(out_all, wo_ref[...], preferred_element_type=F32) + x_ref[...]


def _mlstm_weight_specs(const):
    return [pl.BlockSpec((1, LANES), const),
            pl.BlockSpec((1, D_MODEL), const),
            pl.BlockSpec((ML_CHUNK, ML_CHUNK), const),
            pl.BlockSpec((D_MODEL, D_MODEL), const)]


def _mlstm_prompt(z, x, n_batch, seq, bias, og, tril, wo):
    nc = seq // ML_CHUNK
    const = lambda j: (0, 0)
    tok_specs = lambda width: [pl.BlockSpec((ML_CHUNK, width), functools.partial(lambda j, b: (b * nc + j, 0), b=b))
                               for b in range(n_batch)]
    return pl.pallas_call(
        functools.partial(_mlstm_prompt_kernel, n_batch=n_batch),
        grid=(nc,),
        in_specs=tok_specs(ML_IN_PAD) + tok_specs(D_MODEL) + _mlstm_weight_specs(const),
        out_specs=[pl.BlockSpec((n_batch, ML_CHUNK, D_MODEL), lambda j: (0, j, 0)),
                   pl.BlockSpec((n_batch, ML_HEADS, ML_QK, ML_V), lambda j: (0, 0, 0, 0)),
                   pl.BlockSpec((n_batch, ML_HEADS, ML_QK), lambda j: (0, 0, 0)),
                   pl.BlockSpec((n_batch, 1, ML_HEADS), lambda j: (0, 0, 0))],
        out_shape=[jax.ShapeDtypeStruct((n_batch, seq, D_MODEL), F32),
                   jax.ShapeDtypeStruct((n_batch, ML_HEADS, ML_QK, ML_V), F32),
                   jax.ShapeDtypeStruct((n_batch, ML_HEADS, ML_QK), F32),
                   jax.ShapeDtypeStruct((n_batch, 1, ML_HEADS), F32)],
        compiler_params=_cparams(1),
        name="mlstm_prompt",
    )(*([z] * n_batch), *([x] * n_batch), bias, og, tril, wo)


def _mlstm_sample(z, x, row0, n_seq, t_len, bias, og, tril, wo, c0, n0, m0, seqs_per_step=8):
    rows = seqs_per_step * t_len
    blk0 = row0 // rows
    tok = lambda i: (blk0 + i, 0)
    const = lambda i: (0, 0)
    st4 = lambda i: (i, 0, 0, 0)
    st3 = lambda i: (i, 0, 0)
    state_specs = [pl.BlockSpec((seqs_per_step, ML_HEADS, ML_QK, ML_V), st4),
                   pl.BlockSpec((seqs_per_step, ML_HEADS, ML_QK), st3),
                   pl.BlockSpec((seqs_per_step, 1, ML_HEADS), st3)]
    return pl.pallas_call(
        functools.partial(_mlstm_sample_kernel, t_len=t_len),
        grid=(n_seq // seqs_per_step,),
        in_specs=[pl.BlockSpec((rows, ML_IN_PAD), tok),
                  pl.BlockSpec((rows, D_MODEL), tok)] + _mlstm_weight_specs(const) + state_specs,
        out_specs=[pl.BlockSpec((rows, D_MODEL), lambda i: (i, 0))] + state_specs,
        out_shape=[jax.ShapeDtypeStruct((n_seq * t_len, D_MODEL), F32),
                   jax.ShapeDtypeStruct((n_seq, ML_HEADS, ML_QK, ML_V), F32),
                   jax.ShapeDtypeStruct((n_seq, ML_HEADS, ML_QK), F32),
                   jax.ShapeDtypeStruct((n_seq, 1, ML_HEADS), F32)],
        compiler_params=_cparams(1),
        name="mlstm_sample",
    )(z, x, bias, og, tril, wo, c0, n0, m0)


def _gelu(x):
    return 0.5 * x * (1.0 + lax.erf(x * 0.7071067811865476))


SEL_T = 256


def _unit_scores(q_scr, sk_ref, half, h):
    return [lax.dot_general(sk_ref[h, p], q_scr[half, 2 * h + p].astype(BF16), NT_DIMS,
                            preferred_element_type=F32) for p in range(2)]


def _batcher_pairs(n):
    pairs = []
    p = 1
    while p < n:
        k = p
        while k >= 1:
            for j in range(k % p, n - k, 2 * k):
                for i in range(min(k, n - j - k)):
                    if (i + j) // (2 * p) == (i + j + k) // (2 * p):
                        pairs.append((i + j, i + j + k))
            k //= 2
        p *= 2
    return pairs


def _bitonic_merge_pairs(n):
    pairs = []
    s = n // 2
    while s >= 1:
        pairs += [(i, i + s) for i in range(n) if not i & s]
        s //= 2
    return pairs


def _top16_of_keys(s, out):
    n_vreg = N_KEYS // SUBLANES
    assert n_vreg == PEER_TOPK and N_KEYS == 128
    rounded = (s + 0.0).astype(BF16).astype(F32)
    b16 = lax.shift_right_logical(lax.bitcast_convert_type(rounded, jnp.int32), 16)
    code = jnp.where(b16 >= 0x8000, b16 ^ 0xFFFF, b16 | 0x8000)
    row = lax.broadcasted_iota(jnp.int32, s.shape, 0)
    keys = lax.bitcast_convert_type((code * N_KEYS + (N_KEYS - 1 - row)) | 0x4B000000, F32)

    x = [keys[v * SUBLANES:(v + 1) * SUBLANES] for v in range(n_vreg)]

    def exchange(i, j):
        x[i], x[j] = jnp.maximum(x[i], x[j]), jnp.minimum(x[i], x[j])

    for i, j in _batcher_pairs(n_vreg):
        exchange(i, j)
    yield
    for shift in (SUBLANES // 2, SUBLANES // 4, SUBLANES // 8):
        other = [pltpu.roll(a, shift, 0) for a in x]
        x = [jnp.maximum(x[i], other[n_vreg - 1 - i]) for i in range(n_vreg)]
        for i, j in _bitonic_merge_pairs(n_vreg):
            exchange(i, j)
        yield
    top = lax.bitcast_convert_type(jnp.concatenate([a[0:1] for a in x], axis=0), jnp.int32) & 0x7FFFFF
    rows = (N_KEYS - 1 - (top & (N_KEYS - 1))).astype(F32)
    code = lax.shift_right_logical(top, 7)
    b16 = jnp.where(code >= 0x8000, code ^ 0x8000, code ^ 0xFFFF)
    out.append((lax.bitcast_convert_type(b16 << 16, F32), rows))


class _TopK:
    def __init__(self, s, payload=None):
        self.s = s
        self.payload = payload
        self.rows = lax.broadcasted_iota(jnp.int32, s.shape, 0).astype(F32).astype(s.dtype)
        self.vals, self.picks = [], []

    def step(self, n):
        dt = self.s.dtype
        bound = jnp.asarray(self.s.shape[0], dt)
        for _ in range(n):
            m = jnp.max(self.s, axis=0, keepdims=True)
            first = jnp.min(jnp.where(self.s == m, self.rows, bound), axis=0, keepdims=True)
            sel = self.rows == first
            self.vals.append(m)
            if self.payload is None:
                self.picks.append(first)
            else:
                self.picks.append(jnp.max(jnp.where(sel, self.payload, -1.0), axis=0, keepdims=True))
            self.s = jnp.where(sel, jnp.asarray(-jnp.inf, dt), self.s)

    def result(self):
        return jnp.concatenate(self.vals, axis=0), jnp.concatenate(self.picks, axis=0)


def _pair_candidates(first, second):
    def pairs(a, b, combine):
        h8 = SUBLANES
        rows = [combine(a[0:1], b)]
        rows += [combine(a[k1:k1 + 1], b[0:h8]) for k1 in range(1, h8)]
        rows.append(combine(a[h8:PEER_TOPK], b[0:1]))
        return jnp.concatenate(rows, axis=0)

    (v1, i1), (v2, i2) = [tuple(a.astype(F32) for a in lst) for lst in (first, second)]
    return pairs(v1, v2, lambda a, b: a + b), pairs(i1, i2, lambda a, b: a * N_KEYS + b)


def _peer_fused_kernel(xs_ref, xm_ref, g_ref, wq_ref, sk_ref, u_ref, v_ref, o_ref,
                       q_scr, sel_e, sel_g, xn_scr, a_scr, bgt_scr, bg_scr, gmat, coef_scr):
    i = pl.program_id(0)
    e = pl.program_id(1)
    n_steps = pl.num_programs(1)
    t = xm_ref.shape[0]
    n_half = t // SEL_T
    half_keys = N_KEYS // 2
    cur = i % 2

    def project_queries():
        xn = _rmsnorm(xs_ref[...], g_ref[...]).astype(BF16)
        q = jnp.dot(xn, wq_ref[...], preferred_element_type=F32)
        for hf in range(n_half):
            for c in range(2 * PEER_HEADS):
                q_scr[hf, c] = q[hf * SEL_T:(hf + 1) * SEL_T, c * LANES:(c + 1) * LANES]

    @pl.when(e == 0)
    def _():
        @pl.when(i == 0)
        def _():
            project_queries()
            sel_e[...] = jnp.zeros_like(sel_e)
            sel_g[...] = jnp.zeros_like(sel_g)

        x = xm_ref[...]
        xn_scr[...] = _rmsnorm(x, g_ref[...]).astype(BF16)
        o_ref[...] = x
        coef_scr[0] = jnp.zeros(coef_scr.shape[1:], BF16)
        prev = 1 - cur
        lane_groups = SEL_T // LANES
        for hf in range(n_half):
            ef = sel_e[prev, hf]
            af = jnp.floor(ef * (1.0 / N_KEYS))
            bf = ef - af * N_KEYS
            gf = sel_g[prev, hf]
            a_scr[hf * SEL_T:(hf + 1) * SEL_T, :] = af.T
            bgf = bf + 0.5 * gf
            bgt_scr[hf * SEL_T:(hf + 1) * SEL_T, :] = bgf.T
            for lg in range(lane_groups):
                bg_scr[hf * lane_groups + lg] = bgf[:, lg * LANES:(lg + 1) * LANES]
        r = lax.broadcasted_iota(jnp.int32, (N_KEYS, LANES), 0)
        packed_shape = (N_KEYS // (2 * SUBLANES), 2 * SUBLANES, LANES)
        a_of_row = jnp.where(r < half_keys, 2 * r, 2 * (r - half_keys) + 1).astype(F32).astype(BF16)
        a_of_row = a_of_row.reshape(packed_shape)
        one = jnp.ones(packed_shape, BF16)
        zero = jnp.zeros(packed_shape, BF16)
        b_of_lane = lax.broadcasted_iota(jnp.int32, (N_KEYS, LANES), 1).astype(F32)
        b_of_row = r.astype(F32).astype(BF16).reshape(packed_shape)
        per_body = 32
        bodies_per_group = LANES // per_body
        row_major_every = 3

        def body(it, carry):
            grp = lax.shift_right_logical(it, bodies_per_group.bit_length() - 1)
            sub = it & (bodies_per_group - 1)
            shift = (LANES - sub * per_body) & (LANES - 1)
            bg = pltpu.roll(bg_scr[grp], shift, 1)
            tok0 = pl.multiple_of(it * per_body, per_body)
            arows = a_scr[pl.ds(tok0, per_body), :]
            bgrows = bgt_scr[pl.ds(tok0, per_body), :]
            for k in range(per_body):
                arow = jnp.broadcast_to(arows[k:k + 1], (2 * SUBLANES, LANES)).astype(BF16)[None]
                pa = jnp.where(a_of_row == arow, one, zero).reshape(N_KEYS, LANES)
                if k % row_major_every == 0:
                    bgrow = jnp.broadcast_to(bgrows[k:k + 1], (2 * SUBLANES, LANES))
                    brow = jnp.floor(bgrow)
                    grow = (2.0 * (bgrow - brow)).astype(BF16)[None]
                    qb = jnp.where(b_of_row == brow.astype(BF16)[None], grow, zero).reshape(N_KEYS, LANES)
                    tile = lax.dot_general(pa, qb, NT_DIMS, preferred_element_type=F32)
                else:
                    bgcol = jnp.broadcast_to(bg[:, k:k + 1], (N_KEYS, LANES))
                    bcol = jnp.floor(bgcol)
                    gcol = 2.0 * (bgcol - bcol)
                    qbt = jnp.where(bcol == b_of_lane, gcol, 0.0).astype(BF16)
                    tile = jnp.dot(pa, qbt, preferred_element_type=F32)
                row0 = pl.multiple_of((tok0 + k) * G_PITCH, SUBLANES)
                gmat[pl.ds(row0, half_keys), :] = pltpu.pack_elementwise(
                    [tile[0:half_keys], tile[half_keys:N_KEYS]], packed_dtype=BF16)
            return carry

        lax.fori_loop(0, t // per_body, body, 0)

    unit = jnp.minimum(e, n_half * PEER_HEADS - 1)
    half = lax.shift_right_logical(unit, PEER_HEADS.bit_length() - 1)
    h = unit & (PEER_HEADS - 1)
    rd = e & 1
    blk = jnp.minimum(e, n_steps - 2)
    eb = u_ref.shape[0]
    n_stages = 4
    exp_w = eb // n_stages
    out_w = v_ref.shape[1] // n_stages
    words_per_stage = exp_w // (2 * N_KEYS)

    def value_piece(c):
        ocols = slice(c * out_w, (c + 1) * out_w)
        o_ref[:, ocols] += jnp.dot(coef_scr[rd], v_ref[:, ocols], preferred_element_type=F32)

    def act_piece(c):
        ecols = slice(c * exp_w, (c + 1) * exp_w)
        act = lax.dot_general(xn_scr[...], u_ref[ecols, :], NT_DIMS, preferred_element_type=F32)
        gates = []
        for w in range(words_per_stage):
            word = gmat[pl.ds((blk * n_stages + c) * words_per_stage + w, t, stride=G_PITCH), :]
            gates.append(lax.bitcast_convert_type(word << 16, F32))
            gates.append(lax.bitcast_convert_type(word & jnp.int32(-65536), F32))
        coef_scr[1 - rd, :, ecols] = (_gelu(act) * jnp.concatenate(gates, axis=1)).astype(BF16)

    def retrieval():
        lists = []
        for p in range(2):
            st = lax.dot_general(sk_ref[h, p], q_scr[half, 2 * h + p].astype(BF16), NT_DIMS,
                                 preferred_element_type=F32)
            yield from _top16_of_keys(st, lists)
            yield
        cand, expert = _pair_candidates(lists[0], lists[1])
        second_level = _TopK(cand, payload=expert)
        for _ in range(4):
            second_level.step(PEER_TOPK // 4)
            yield
        best, e_sel = second_level.result()
        ex = jnp.exp(best - best[0:1])
        slot0 = pl.multiple_of(h * PEER_TOPK, PEER_TOPK)
        sel_e[cur, half, pl.ds(slot0, PEER_TOPK), :] = e_sel
        sel_g[cur, half, pl.ds(slot0, PEER_TOPK), :] = ex / jnp.sum(ex, axis=0, keepdims=True)

    pieces = [functools.partial(f, c) for c in range(n_stages) for f in (value_piece, act_piece)]
    parts = retrieval()
    parts_per_piece = 2
    for piece in pieces:
        piece()
        for _ in range(parts_per_piece):
            next(parts, None)
    for _ in parts:
        pass

    @pl.when(e == n_steps - 1)
    def _():
        project_queries()


def _peer(x, g, wq, sk, u, v, t=512, eb=1024):
    n, d = x.shape
    nslot = PEER_HEADS * PEER_TOPK
    n_tok_blk = n // t
    n_exp_blk = N_KEYS * N_KEYS // eb
    n_half = t // SEL_T
    assert n_exp_blk == n_half * PEER_HEADS
    last_tok = n_tok_blk - 1
    once = pl.Buffered(1)
    return pl.pallas_call(
        _peer_fused_kernel,
        grid=(n_tok_blk + 1, n_exp_blk + 1),
        in_specs=[pl.BlockSpec((t, d), lambda i, e: (jnp.minimum(i + e // n_exp_blk, last_tok), 0),
                               pipeline_mode=once),
                  pl.BlockSpec((t, d), lambda i, e: (jnp.maximum(i - 1, 0), 0)),
                  pl.BlockSpec((1, d), lambda i, e: (0, 0)),
                  pl.BlockSpec((d, 2 * PEER_HEADS * LANES), lambda i, e: (0, 0), pipeline_mode=once),
                  pl.BlockSpec((PEER_HEADS, 2, N_KEYS, LANES), lambda i, e: (0, 0, 0, 0), pipeline_mode=once),
                  pl.BlockSpec((eb, d), lambda i, e: (jnp.minimum(e, n_exp_blk - 1), 0)),
                  pl.BlockSpec((eb, d), lambda i, e: (jnp.maximum(e - 1, 0), 0))],
        out_specs=pl.BlockSpec((t, d), lambda i, e: (jnp.maximum(i - 1, 0), 0)),
        out_shape=jax.ShapeDtypeStruct((n, d), F32),
        scratch_shapes=[pltpu.VMEM((n_half, 2 * PEER_HEADS, SEL_T, LANES), F32),
                        pltpu.VMEM((2, n_half, nslot, SEL_T), F32),
                        pltpu.VMEM((2, n_half, nslot, SEL_T), F32),
                        pltpu.VMEM((t, d), BF16),
                        pltpu.VMEM((t, nslot), F32),
                        pltpu.VMEM((t, nslot), F32),
                        pltpu.VMEM((t // LANES, nslot, LANES), F32),
                        pltpu.VMEM((t * G_PITCH, LANES), jnp.int32),
                        pltpu.VMEM((2, t, eb), BF16)],
        compiler_params=_cparams(2),
        name="peer",
    )(x, x, g.reshape(1, d), wq, sk, u, v)


def _rope_tables(pos):
    half = HEAD_DIM // 2
    inv_freq = ROPE_THETA ** (-jnp.arange(half, dtype=F32) / half)
    ang = pos.astype(F32)[:, None] * inv_freq[None, :]
    cos, sin = jnp.cos(ang), jnp.sin(ang)
    reps = LANES // HEAD_DIM
    cos_t = jnp.tile(jnp.concatenate([cos, cos], axis=1), (1, reps))
    sin_t = jnp.tile(jnp.concatenate([-sin, sin], axis=1), (1, reps))
    return cos_t, sin_t


def kernel(x_prompt, x_sample, cache_conv, cache_win_k, cache_win_v, state_mlstm_C, state_mlstm_n,
           state_mlstm_m, norm_mix, norm_ffn, ab_w_in, ab_conv_w, ab_q_gain, ab_k_gain, ab_sinks, ab_w_out,
           ml_w_in, ml_gate_bias, ml_out_gain, ml_w_out, peer_w_q, peer_sub_keys, peer_u, peer_v):
    n_batch, seq, d = x_prompt.shape
    n_seq, t_len, _ = x_sample.shape
    n_prompt = n_batch * seq
    assert d == D_MODEL and t_len == SUBLANES and norm_mix.shape[0] == 2

    xp = x_prompt.reshape(n_prompt, d)
    xs = x_sample.reshape(n_seq * t_len, d)

    cos_p, sin_p = _rope_tables(jnp.arange(seq, dtype=jnp.int32))
    cos_s, sin_s = _rope_tables(PAST_LEN + jnp.arange(t_len, dtype=jnp.int32))
    bt = 16
    cos_s, sin_s = jnp.tile(cos_s, (bt, 1)), jnp.tile(sin_s, (bt, 1))
    lane = jnp.arange(LANES)
    seg = (lane[:, None] // HEAD_DIM == lane[None, :] // HEAD_DIM).astype(BF16)
    reps = LANES // HEAD_DIM
    qg = jnp.tile(ab_q_gain[0], reps).reshape(1, LANES)
    kg = jnp.tile(ab_k_gain[0], reps).reshape(1, LANES)
    wo_ab = ab_w_out[0].astype(BF16)

    z = _norm_proj([xp, xs], norm_mix[0], ab_w_in[0].astype(BF16))
    y_p, k_p, v_p, c_p = _ab_prompt(z, xp, n_batch, seq, cos_p, sin_p, ab_conv_w[0], qg, kg, ab_sinks[0],
                                    seg, wo_ab)
    y_s, c_s, k_s, v_s = _ab_sample(z, xs, n_prompt, n_seq, t_len, cos_s, sin_s, ab_conv_w[0], qg, kg,
                                    ab_sinks[0], seg, wo_ab, cache_conv[0],
                                    cache_win_k[0].reshape(n_seq, WINDOW, LANES),
                                    cache_win_v[0].reshape(n_seq, WINDOW, LANES), bt=bt)
    x = jnp.concatenate([y_p.reshape(n_prompt, d), y_s], axis=0)
    x = _peer(x, norm_ffn[0], peer_w_q[0].astype(BF16), peer_sub_keys[0].astype(BF16),
              peer_u[0].astype(BF16), peer_v[0].astype(BF16))

    n_gate = 2 * ML_HEADS
    w_in = jnp.pad(ml_w_in[0], ((0, 0), (0, ML_IN_PAD - ml_w_in.shape[2]))).astype(BF16)
    bias = jnp.pad(ml_gate_bias[0], (0, LANES - n_gate)).reshape(1, LANES)
    og = ml_out_gain[0].reshape(1, D_MODEL)
    idx = jnp.arange(ML_CHUNK)
    tril = (idx[None, :] <= idx[:, None]).astype(BF16)
    wo_ml = ml_w_out[0].astype(BF16)

    z = _norm_proj([x], norm_mix[1], w_in)
    y_p, cm_p, nm_p, mm_p = _mlstm_prompt(z, x, n_batch, seq, bias, og, tril, wo_ml)
    y_s, cm_s, nm_s, mm_s = _mlstm_sample(z, x, n_prompt, n_seq, t_len, bias, og, tril, wo_ml,
                                          state_mlstm_C[0], state_mlstm_n[0],
                                          state_mlstm_m[0].reshape(n_seq, 1, ML_HEADS))
    x = jnp.concatenate([y_p.reshape(n_prompt, d), y_s], axis=0)
    x = _peer(x, norm_ffn[1], peer_w_q[1].astype(BF16), peer_sub_keys[1].astype(BF16),
              peer_u[1].astype(BF16), peer_v[1].astype(BF16))

    y_prompt = x[:n_prompt].reshape(n_batch, seq, d)
    y_sample = x[n_prompt:].reshape(n_seq, t_len, d)
    kv_shape_p = (1, n_batch, WINDOW, N_KV_HEADS, HEAD_DIM)
    kv_shape_s = (1, n_seq, WINDOW, N_KV_HEADS, HEAD_DIM)
    return (y_prompt, y_sample,
            c_p[:, SUBLANES - 2:, :][None], k_p.reshape(kv_shape_p), v_p.reshape(kv_shape_p),
            cm_p[None], nm_p[None], mm_p.reshape(1, n_batch, ML_HEADS),
            c_s[None], k_s.reshape(kv_shape_s), v_s.reshape(kv_shape_s),
            cm_s[None], nm_s[None], mm_s.reshape(1, n_seq, ML_HEADS))
```

```python
import functools

import jax
import jax.numpy as jnp
from jax import lax
from jax.experimental import pallas as pl
from jax.experimental.pallas import tpu as pltpu

F32 = jnp.float32
BF16 = jnp.bfloat16
EPS = 1e-6

D_MODEL = 1024
CONV_DIM = 512
N_Q_HEADS = 8
N_KV_HEADS = 2
HEAD_DIM = 64
WINDOW = 128
ROPE_THETA = 10000.0
AB_IN = 2304
ML_HEADS = 4
ML_QK = 128
ML_V = 256
ML_CHUNK = 128
ML_GATE_COL = 3072
ML_IN_PAD = ML_GATE_COL + 128
N_KEYS = 128
PEER_HEADS = 8
PEER_TOPK = 16
PAST_LEN = 16384

LANES = 128
SUBLANES = 8
G_PITCH = N_KEYS // 2 + SUBLANES
VMEM_LIMIT = 56 * 1024 * 1024

NT_DIMS = (((1,), (1,)), ((), ()))
TN_DIMS = (((0,), (0,)), ((), ()))


def _cparams(n_axes, vmem=VMEM_LIMIT):
    return pltpu.CompilerParams(dimension_semantics=("arbitrary",) * n_axes, vmem_limit_bytes=vmem)


def _rmsnorm(x, g):
    return x * lax.rsqrt(jnp.mean(x * x, axis=-1, keepdims=True) + EPS) * g


def _bf16_pieces(a, terms):
    pieces = []
    rem = a
    for _ in range(terms):
        piece = rem.astype(BF16)
        rem = rem - piece.astype(F32)
        pieces.append(piece)
    return pieces


def _split_dot(a, b_bf16, terms=2):
    return sum(jnp.dot(p, b_bf16, preferred_element_type=F32) for p in _bf16_pieces(a, terms))


def _norm_proj_kernel(*refs, starts):
    x_refs = refs[:len(starts)]
    g_ref, w_ref, o_ref = refs[len(starts):]
    i = pl.program_id(0)
    x = x_refs[0][...]
    for x_ref, start in zip(x_refs[1:], starts[1:]):
        x = jnp.where(i >= start, x_ref[...], x)
    r = _rmsnorm(x, g_ref[...])
    o_ref[...] = jnp.dot(r.astype(BF16), w_ref[...], preferred_element_type=F32)


def _norm_proj(x_parts, g, w_bf16, tm=512):
    d = x_parts[0].shape[1]
    nout = w_bf16.shape[1]
    blocks = [p.shape[0] // tm for p in x_parts]
    starts = tuple(sum(blocks[:k]) for k in range(len(blocks)))
    part_spec = lambda start, nblk: pl.BlockSpec((tm, d), lambda i: (jnp.clip(i - start, 0, nblk - 1), 0))
    return pl.pallas_call(
        functools.partial(_norm_proj_kernel, starts=starts),
        grid=(sum(blocks),),
        in_specs=[part_spec(s, nb) for s, nb in zip(starts, blocks)] + [
            pl.BlockSpec((1, d), lambda i: (0, 0)),
            pl.BlockSpec((d, nout), lambda i: (0, 0))],
        out_specs=pl.BlockSpec((tm, nout), lambda i: (i, 0)),
        out_shape=jax.ShapeDtypeStruct((sum(blocks) * tm, nout), F32),
        compiler_params=_cparams(1),
        name="norm_proj",
    )(*x_parts, g.reshape(1, d), w_bf16)


def _headnorm_rope(xc, gain, cos, sin, seg, hi_half):
    ss = _split_dot(xc * xc, seg)
    xn = xc * lax.rsqrt(ss * (1.0 / HEAD_DIM) + EPS) * gain
    partner = jnp.where(hi_half, pltpu.roll(xn, 32, 1), pltpu.roll(xn, 96, 1))
    return xn * cos + partner * sin


def _softmax_sink(s, mask, sink):
    s = jnp.where(mask, s, -1e30)
    m = jnp.maximum(jnp.max(s, axis=-1, keepdims=True), sink)
    p = jnp.exp(s - m)
    denom = jnp.sum(p, axis=-1, keepdims=True) + jnp.exp(sink - m)
    return (p / denom).astype(BF16)


def _ab_prompt_kernel(sink_ref, *refs, n_batch):
    z_refs, x_refs = refs[0:n_batch], refs[n_batch:2 * n_batch]
    (cos_ref, sin_ref, cw_ref, qg_ref, kg_ref, seg_ref, wo_ref,
     y_ref, kst_ref, vst_ref, cst_ref, pk_ref, pv_ref, pu_ref) = refs[2 * n_batch:]
    j = pl.program_id(0)

    @pl.when(j == 0)
    def _():
        pk_ref[...] = jnp.zeros_like(pk_ref)
        pv_ref[...] = jnp.zeros_like(pv_ref)
        pu_ref[...] = jnp.zeros_like(pu_ref)

    for b in range(n_batch):
        _ab_prompt_block(j, sink_ref, z_refs[b], x_refs[b], cos_ref, sin_ref, cw_ref, qg_ref, kg_ref, seg_ref,
                         wo_ref, y_ref.at[b], kst_ref.at[b], vst_ref.at[b], cst_ref.at[b],
                         pk_ref.at[b], pv_ref.at[b], pu_ref.at[b])


def _ab_prompt_block(j, sink_ref, z_ref, x_ref, cos_ref, sin_ref, cw_ref, qg_ref, kg_ref, seg_ref, wo_ref,
                     y_ref, kst_ref, vst_ref, cst_ref, pk_ref, pv_ref, pu_ref):
    blk = z_ref.shape[0]
    cos = cos_ref[...]
    sin = sin_ref[...]
    seg = seg_ref[...]
    hi_half = (lax.broadcasted_iota(jnp.int32, (blk, LANES), 1) & 32) != 0

    gate_b = z_ref[:, 0:CONV_DIM]
    u = z_ref[:, CONV_DIM:2 * CONV_DIM] * z_ref[:, 2 * CONV_DIM:3 * CONV_DIM]
    ng = blk // SUBLANES
    u3 = u.reshape(ng, SUBLANES, CONV_DIM)
    ext = jnp.concatenate([pu_ref[...][None], u3], axis=0)
    t8 = lax.broadcasted_iota(jnp.int32, (ng, SUBLANES, CONV_DIM), 1)
    r1 = pltpu.roll(ext, 1, 1)
    r2 = pltpu.roll(ext, 2, 1)
    um1 = jnp.where(t8 >= 1, r1[1:], r1[:-1])
    um2 = jnp.where(t8 >= 2, r2[1:], r2[:-1])
    cw = cw_ref[...]
    conv = cw[0:1][None] * um2 + cw[1:2][None] * um1 + cw[2:3][None] * u3
    yconv = gate_b * conv.reshape(blk, CONV_DIM)

    q0 = 3 * CONV_DIM
    k0 = q0 + N_Q_HEADS * HEAD_DIM
    v0 = k0 + N_KV_HEADS * HEAD_DIM
    qg = qg_ref[...]
    qr = [_headnorm_rope(z_ref[:, q0 + c * LANES:q0 + (c + 1) * LANES], qg, cos, sin, seg, hi_half)
          for c in range(N_Q_HEADS * HEAD_DIM // LANES)]
    kr = _headnorm_rope(z_ref[:, k0:k0 + LANES], kg_ref[...], cos, sin, seg, hi_half)
    v = z_ref[:, v0:v0 + LANES]
    pk = pk_ref[...]
    pv = pv_ref[...]

    row = lax.broadcasted_iota(jnp.int32, (blk, 2 * blk), 0)
    col = lax.broadcasted_iota(jnp.int32, (blk, 2 * blk), 1)
    row_prev = row + jnp.where(j == 0, 2 * blk, 0)
    mask = ((col < blk) & (col > row_prev)) | ((col >= blk) & (col - blk <= row))

    kks, vvs = [], []
    for g in range(N_KV_HEADS):
        sl = slice(g * HEAD_DIM, (g + 1) * HEAD_DIM)
        kks.append(jnp.concatenate([pk[:, sl], kr[:, sl]], axis=0).astype(BF16))
        vvs.append(jnp.concatenate([pv[:, sl], v[:, sl]], axis=0).astype(BF16))
    outs = []
    for h in range(N_Q_HEADS):
        g = h // (N_Q_HEADS // N_KV_HEADS)
        qh = qr[h // 2][:, (h % 2) * HEAD_DIM:(h % 2 + 1) * HEAD_DIM].astype(BF16)
        s = lax.dot_general(qh, kks[g], NT_DIMS, preferred_element_type=F32) * (HEAD_DIM ** -0.5)
        p = _softmax_sink(s, mask, sink_ref[h])
        outs.append(jnp.dot(p, vvs[g], preferred_element_type=F32))
    attn = jnp.concatenate(outs, axis=1)

    y = (jnp.dot(yconv.astype(BF16), wo_ref[0:CONV_DIM, :], preferred_element_type=F32)
         + jnp.dot(attn.astype(BF16), wo_ref[CONV_DIM:2 * CONV_DIM, :], preferred_element_type=F32))
    y_ref[...] = y + x_ref[...]

    pk_ref[...] = kr
    pv_ref[...] = v
    pu_ref[...] = u3[ng - 1]
    kst_ref[...] = kr
    vst_ref[...] = v
    cst_ref[...] = u3[ng - 1]


def _ab_prompt(z, x, n_batch, seq, cos, sin, cw, qg, kg, sinks, seg, wo):
    blk = WINDOW
    nb = seq // blk
    const = lambda j, s: (0, 0)
    whole = lambda j, s: (0, 0, 0)
    tok_specs = lambda width: [pl.BlockSpec((blk, width), functools.partial(lambda j, s, b: (b * nb + j, 0), b=b))
                               for b in range(n_batch)]
    grid_spec = pltpu.PrefetchScalarGridSpec(
        num_scalar_prefetch=1,
        grid=(nb,),
        in_specs=tok_specs(AB_IN) + tok_specs(D_MODEL) + [
            pl.BlockSpec((blk, LANES), lambda j, s: (j, 0)),
            pl.BlockSpec((blk, LANES), lambda j, s: (j, 0)),
            pl.BlockSpec((3, CONV_DIM), const),
            pl.BlockSpec((1, LANES), const),
            pl.BlockSpec((1, LANES), const),
            pl.BlockSpec((LANES, LANES), const),
            pl.BlockSpec((D_MODEL, D_MODEL), const)],
        out_specs=[pl.BlockSpec((n_batch, blk, D_MODEL), lambda j, s: (0, j, 0)),
                   pl.BlockSpec((n_batch, blk, LANES), whole),
                   pl.BlockSpec((n_batch, blk, LANES), whole),
                   pl.BlockSpec((n_batch, SUBLANES, CONV_DIM), whole)],
        scratch_shapes=[pltpu.VMEM((n_batch, blk, LANES), F32), pltpu.VMEM((n_batch, blk, LANES), F32),
                        pltpu.VMEM((n_batch, SUBLANES, CONV_DIM), F32)])
    return pl.pallas_call(
        functools.partial(_ab_prompt_kernel, n_batch=n_batch),
        grid_spec=grid_spec,
        out_shape=[jax.ShapeDtypeStruct((n_batch, seq, D_MODEL), F32),
                   jax.ShapeDtypeStruct((n_batch, blk, LANES), F32),
                   jax.ShapeDtypeStruct((n_batch, blk, LANES), F32),
                   jax.ShapeDtypeStruct((n_batch, SUBLANES, CONV_DIM), F32)],
        compiler_params=_cparams(1),
        name="ab_prompt",
    )(sinks, *([z] * n_batch), *([x] * n_batch), cos, sin, cw, qg, kg, seg, wo)


def _ab_sample_kernel(sink_ref, z_ref, x_ref, cos_ref, sin_ref, cw_ref, qg_ref, kg_ref, seg_ref, wo_ref,
                      cc_ref, ck_ref, cv_ref, y_ref, cs_ref, ks_ref, vs_ref):
    rows = z_ref.shape[0]
    t_len = SUBLANES
    bt = rows // t_len
    cos = cos_ref[...]
    sin = sin_ref[...]
    seg = seg_ref[...]
    hi_half = (lax.broadcasted_iota(jnp.int32, (rows, LANES), 1) & 32) != 0

    gate_b = z_ref[:, 0:CONV_DIM]
    u = z_ref[:, CONV_DIM:2 * CONV_DIM] * z_ref[:, 2 * CONV_DIM:3 * CONV_DIM]
    u3 = u.reshape(bt, t_len, CONV_DIM)
    cc = cc_ref[...]
    c0 = cc[:, 0:1, :]
    c1 = cc[:, 1:2, :]
    t8 = lax.broadcasted_iota(jnp.int32, (bt, t_len, CONV_DIM), 1)
    r1 = pltpu.roll(u3, 1, 1)
    r2 = pltpu.roll(u3, 2, 1)
    um1 = jnp.where(t8 >= 1, r1, c1)
    um2 = jnp.where(t8 >= 2, r2, jnp.where(t8 == 1, c1, c0))
    cw = cw_ref[...]
    conv = cw[0:1][None] * um2 + cw[1:2][None] * um1 + cw[2:3][None] * u3
    yconv = gate_b * conv.reshape(rows, CONV_DIM)
    cs_ref[...] = r2[:, 0:2, :]

    q0 = 3 * CONV_DIM
    k0 = q0 + N_Q_HEADS * HEAD_DIM
    v0 = k0 + N_KV_HEADS * HEAD_DIM
    qg = qg_ref[...]
    qr = [_headnorm_rope(z_ref[:, q0 + c * LANES:q0 + (c + 1) * LANES], qg, cos, sin, seg, hi_half)
          for c in range(N_Q_HEADS * HEAD_DIM // LANES)]
    kr = _headnorm_rope(z_ref[:, k0:k0 + LANES], kg_ref[...], cos, sin, seg, hi_half)
    v = z_ref[:, v0:v0 + LANES]

    group = N_Q_HEADS // N_KV_HEADS
    nq = group * t_len
    nk = 2 * WINDOW
    qrow = lax.broadcasted_iota(jnp.int32, (nq, nk), 0)
    t_q = qrow & (t_len - 1)
    col = lax.broadcasted_iota(jnp.int32, (nq, nk), 1)
    mask = (((col < WINDOW) & (col > t_q)) | ((col >= WINDOW) & (col - WINDOW <= t_q)))[None]
    hrow = lax.broadcasted_iota(jnp.int32, (nq, 1), 0) // t_len
    pad = jnp.zeros((bt, nk - WINDOW - t_len, HEAD_DIM), F32)

    outs = [None] * N_Q_HEADS
    for g in range(N_KV_HEADS):
        sl = slice(g * HEAD_DIM, (g + 1) * HEAD_DIM)
        qs = jnp.concatenate(
            [qr[h // 2][:, (h % 2) * HEAD_DIM:(h % 2 + 1) * HEAD_DIM].reshape(bt, t_len, HEAD_DIM)
             for h in range(g * group, (g + 1) * group)], axis=1)
        kk = jnp.concatenate([ck_ref[:, :, sl], kr[:, sl].reshape(bt, t_len, HEAD_DIM), pad], axis=1)
        vv = jnp.concatenate([cv_ref[:, :, sl], v[:, sl].reshape(bt, t_len, HEAD_DIM), pad], axis=1)
        s = jnp.einsum('bqd,bkd->bqk', qs.astype(BF16), kk.astype(BF16),
                       preferred_element_type=F32) * (HEAD_DIM ** -0.5)
        sink = jnp.zeros((nq, 1), F32)
        for hh in range(group):
            sink = jnp.where(hrow == hh, sink_ref[g * group + hh], sink)
        p = _softmax_sink(s, mask, sink[None])
        o = jnp.einsum('bqk,bkd->bqd', p, vv.astype(BF16), preferred_element_type=F32)
        for hh in range(group):
            outs[g * group + hh] = o[:, hh * t_len:(hh + 1) * t_len, :].reshape(rows, HEAD_DIM)
    attn = jnp.concatenate(outs, axis=1)

    y = (jnp.dot(yconv.astype(BF16), wo_ref[0:CONV_DIM, :], preferred_element_type=F32)
         + jnp.dot(attn.astype(BF16), wo_ref[CONV_DIM:2 * CONV_DIM, :], preferred_element_type=F32))
    y_ref[...] = y + x_ref[...]

    keep = WINDOW - t_len
    ks_ref[:, 0:keep, :] = ck_ref[:, t_len:WINDOW, :]
    ks_ref[:, keep:WINDOW, :] = kr.reshape(bt, t_len, LANES)
    vs_ref[:, 0:keep, :] = cv_ref[:, t_len:WINDOW, :]
    vs_ref[:, keep:WINDOW, :] = v.reshape(bt, t_len, LANES)


def _ab_sample(z, x, row0, n_seq, t_len, cos, sin, cw, qg, kg, sinks, seg, wo, cc, ck, cv, bt=16):
    rows = bt * t_len
    blk0 = row0 // rows
    tok = lambda i, s: (blk0 + i, 0)
    const = lambda i, s: (0, 0)
    seq3 = lambda i, s: (i, 0, 0)
    grid_spec = pltpu.PrefetchScalarGridSpec(
        num_scalar_prefetch=1,
        grid=(n_seq // bt,),
        in_specs=[pl.BlockSpec((rows, AB_IN), tok),
                  pl.BlockSpec((rows, D_MODEL), lambda i, s: (i, 0)),
                  pl.BlockSpec((rows, LANES), const),
                  pl.BlockSpec((rows, LANES), const),
                  pl.BlockSpec((3, CONV_DIM), const),
                  pl.BlockSpec((1, LANES), const),
                  pl.BlockSpec((1, LANES), const),
                  pl.BlockSpec((LANES, LANES), const),
                  pl.BlockSpec((D_MODEL, D_MODEL), const),
                  pl.BlockSpec((bt, 2, CONV_DIM), seq3),
                  pl.BlockSpec((bt, WINDOW, LANES), seq3),
                  pl.BlockSpec((bt, WINDOW, LANES), seq3)],
        out_specs=[pl.BlockSpec((rows, D_MODEL), lambda i, s: (i, 0)),
                   pl.BlockSpec((bt, 2, CONV_DIM), seq3),
                   pl.BlockSpec((bt, WINDOW, LANES), seq3),
                   pl.BlockSpec((bt, WINDOW, LANES), seq3)])
    return pl.pallas_call(
        _ab_sample_kernel,
        grid_spec=grid_spec,
        out_shape=[jax.ShapeDtypeStruct((n_seq * t_len, D_MODEL), F32),
                   jax.ShapeDtypeStruct((n_seq, 2, CONV_DIM), F32),
                   jax.ShapeDtypeStruct((n_seq, WINDOW, LANES), F32),
                   jax.ShapeDtypeStruct((n_seq, WINDOW, LANES), F32)],
        compiler_params=_cparams(1),
        name="ab_sample",
    )(sinks, z, x, cos, sin, cw, qg, kg, seg, wo, cc, ck, cv)


def _log_sigmoid(x):
    return jnp.minimum(x, 0.0) - jnp.log(1.0 + jnp.exp(-jnp.abs(x)))


def _mlstm_chunk(z, bias, og, tril, c_src, n_src, m_src, c_dst, n_dst, m_dst, n_real, rows):
    L = z.shape[0]
    gates = z[:, ML_GATE_COL:ML_GATE_COL + LANES] + bias
    if n_real < L:
        live = lax.broadcasted_iota(jnp.int32, (L, LANES), 0) < n_real
        li_all = jnp.where(live, gates, -1e30)
        lf_all = jnp.where(live, _log_sigmoid(gates), 0.0)
    else:
        li_all = gates
        lf_all = _log_sigmoid(gates)
    lf_pieces = _bf16_pieces(lf_all, 3)
    f_col_all = sum(jnp.dot(tril[0:rows], p, preferred_element_type=F32) for p in lf_pieces)
    f_row_all = sum(lax.dot_general(p, tril, (((0,), (1,)), ((), ())), preferred_element_type=F32)
                    for p in lf_pieces)
    li_t = li_all.T
    rr = lax.broadcasted_iota(jnp.int32, (rows, L), 0)
    cc = lax.broadcasted_iota(jnp.int32, (rows, L), 1)
    causal = cc <= rr

    outs, m_new_all = [], []
    for h in range(ML_HEADS):
        f_col = f_col_all[:, ML_HEADS + h:ML_HEADS + h + 1]
        f_row = f_row_all[ML_HEADS + h:ML_HEADS + h + 1, :]
        li_row = li_t[h:h + 1, :]
        li_col = li_all[0:rows, h:h + 1]
        m0 = m_src[0:1, h:h + 1]
        c0 = c_src[h]
        n0 = n_src[h:h + 1, :]
        qh = z[0:rows, h * ML_QK:(h + 1) * ML_QK]
        kh = z[:, ML_HEADS * ML_QK + h * ML_QK:ML_HEADS * ML_QK + (h + 1) * ML_QK] * (ML_QK ** -0.5)
        v_off = 2 * ML_HEADS * ML_QK
        vh = z[:, v_off + h * ML_V:v_off + (h + 1) * ML_V]
        o_off = v_off + ML_HEADS * ML_V
        oh = z[0:rows, o_off + h * ML_V:o_off + (h + 1) * ML_V]
        qb = qh.astype(BF16)
        vb = vh.astype(BF16)

        dmat = jnp.where(causal, f_col - f_row + li_row, -jnp.inf)
        gcar = f_col + m0
        m_t = jnp.maximum(jnp.max(dmat, axis=-1, keepdims=True), gcar)
        w = jnp.exp(dmat - m_t)
        s = lax.dot_general(qb, kh.astype(BF16), NT_DIMS, preferred_element_type=F32) * w
        carry = jnp.exp(gcar - m_t)
        num = (jnp.dot(s.astype(BF16), vb, preferred_element_type=F32)
               + jnp.dot(qb, c0.astype(BF16), preferred_element_type=F32) * carry)
        den = jnp.sum(s, axis=-1, keepdims=True) + carry * jnp.sum(qh * n0, axis=-1, keepdims=True)
        hout = num / jnp.maximum(jnp.abs(den), jnp.exp(-m_t))

        f_last = f_col[n_real - 1:n_real, :]
        w_end = f_last - f_col + li_col
        m_new = jnp.maximum(f_last + m0, jnp.max(w_end, axis=0, keepdims=True))
        a_end = jnp.exp(w_end - m_new)
        scale = jnp.exp(f_last + m0 - m_new)
        ka = kh[0:rows] * a_end
        c_dst[h] = scale * c0 + lax.dot_general(ka.astype(BF16), vb[0:rows], TN_DIMS,
                                                preferred_element_type=F32)
        n_dst[h:h + 1, :] = scale * n0 + jnp.sum(ka, axis=0, keepdims=True)
        m_new_all.append(m_new)

        hn = _rmsnorm(hout, og[:, h * ML_V:(h + 1) * ML_V])
        outs.append(jax.nn.sigmoid(oh) * hn)
    m_dst[...] = jnp.concatenate(m_new_all, axis=1)
    return jnp.concatenate(outs, axis=1)


def _mlstm_prompt_kernel(*refs, n_batch):
    z_refs, x_refs = refs[0:n_batch], refs[n_batch:2 * n_batch]
    bias_ref, og_ref, tril_ref, wo_ref, y_ref, c_ref, n_ref, m_ref = refs[2 * n_batch:]

    @pl.when(pl.program_id(0) == 0)
    def _():
        c_ref[...] = jnp.zeros_like(c_ref)
        n_ref[...] = jnp.zeros_like(n_ref)
        m_ref[...] = jnp.zeros_like(m_ref)

    for b in range(n_batch):
        state = (c_ref.at[b], n_ref.at[b], m_ref.at[b])
        out = _mlstm_chunk(z_refs[b][...], bias_ref[...], og_ref[...], tril_ref[...], *state, *state,
                           ML_CHUNK, ML_CHUNK)
        y_ref[b] = jnp.dot(out.astype(BF16), wo_ref[...], preferred_element_type=F32) + x_refs[b][...]


def _mlstm_sample_kernel(z_ref, x_ref, bias_ref, og_ref, tril_ref, wo_ref, c0_ref, n0_ref, m0_ref,
                         y_ref, c_ref, n_ref, m_ref, *, t_len):
    n_here = z_ref.shape[0] // t_len
    q_rows = 2 * SUBLANES
    outs = []
    for s in range(n_here):
        zpad = jnp.concatenate([z_ref[s * t_len:(s + 1) * t_len, :],
                                jnp.zeros((ML_CHUNK - t_len, ML_IN_PAD), F32)], axis=0)
        out = _mlstm_chunk(zpad, bias_ref[...], og_ref[...], tril_ref[...],
                           c0_ref.at[s], n0_ref.at[s], m0_ref.at[s], c_ref.at[s], n_ref.at[s], m_ref.at[s],
                           t_len, q_rows)
        outs.append(out[0:t_len])
    out_all = jnp.concatenate(outs, axis=0).astype(BF16)
    y_ref[...] = jnp.dot(out_all, wo_ref[...], preferred_element_type=F32) + x_ref[...]


def _mlstm_weight_specs(const):
    return [pl.BlockSpec((1, LANES), const),
            pl.BlockSpec((1, D_MODEL), const),
            pl.BlockSpec((ML_CHUNK, ML_CHUNK), const),
            pl.BlockSpec((D_MODEL, D_MODEL), const)]


def _mlstm_prompt(z, x, n_batch, seq, bias, og, tril, wo):
    nc = seq // ML_CHUNK
    const = lambda j: (0, 0)
    tok_specs = lambda width: [pl.BlockSpec((ML_CHUNK, width), functools.partial(lambda j, b: (b * nc + j, 0), b=b))
                               for b in range(n_batch)]
    return pl.pallas_call(
        functools.partial(_mlstm_prompt_kernel, n_batch=n_batch),
        grid=(nc,),
        in_specs=tok_specs(ML_IN_PAD) + tok_specs(D_MODEL) + _mlstm_weight_specs(const),
        out_specs=[pl.BlockSpec((n_batch, ML_CHUNK, D_MODEL), lambda j: (0, j, 0)),
                   pl.BlockSpec((n_batch, ML_HEADS, ML_QK, ML_V), lambda j: (0, 0, 0, 0)),
                   pl.BlockSpec((n_batch, ML_HEADS, ML_QK), lambda j: (0, 0, 0)),
                   pl.BlockSpec((n_batch, 1, ML_HEADS), lambda j: (0, 0, 0))],
        out_shape=[jax.ShapeDtypeStruct((n_batch, seq, D_MODEL), F32),
                   jax.ShapeDtypeStruct((n_batch, ML_HEADS, ML_QK, ML_V), F32),
                   jax.ShapeDtypeStruct((n_batch, ML_HEADS, ML_QK), F32),
                   jax.ShapeDtypeStruct((n_batch, 1, ML_HEADS), F32)],
        compiler_params=_cparams(1),
        name="mlstm_prompt",
    )(*([z] * n_batch), *([x] * n_batch), bias, og, tril, wo)


def _mlstm_sample(z, x, row0, n_seq, t_len, bias, og, tril, wo, c0, n0, m0, seqs_per_step=8):
    rows = seqs_per_step * t_len
    blk0 = row0 // rows
    tok = lambda i: (blk0 + i, 0)
    const = lambda i: (0, 0)
    st4 = lambda i: (i, 0, 0, 0)
    st3 = lambda i: (i, 0, 0)
    state_specs = [pl.BlockSpec((seqs_per_step, ML_HEADS, ML_QK, ML_V), st4),
                   pl.BlockSpec((seqs_per_step, ML_HEADS, ML_QK), st3),
                   pl.BlockSpec((seqs_per_step, 1, ML_HEADS), st3)]
    return pl.pallas_call(
        functools.partial(_mlstm_sample_kernel, t_len=t_len),
        grid=(n_seq // seqs_per_step,),
        in_specs=[pl.BlockSpec((rows, ML_IN_PAD), tok),
                  pl.BlockSpec((rows, D_MODEL), tok)] + _mlstm_weight_specs(const) + state_specs,
        out_specs=[pl.BlockSpec((rows, D_MODEL), lambda i: (i, 0))] + state_specs,
        out_shape=[jax.ShapeDtypeStruct((n_seq * t_len, D_MODEL), F32),
                   jax.ShapeDtypeStruct((n_seq, ML_HEADS, ML_QK, ML_V), F32),
                   jax.ShapeDtypeStruct((n_seq, ML_HEADS, ML_QK), F32),
                   jax.ShapeDtypeStruct((n_seq, 1, ML_HEADS), F32)],
        compiler_params=_cparams(1),
        name="mlstm_sample",
    )(z, x, bias, og, tril, wo, c0, n0, m0)


def _gelu(x):
    return 0.5 * x * (1.0 + lax.erf(x * 0.7071067811865476))


SEL_T = 256


def _batcher_pairs(n):
    pairs = []
    p = 1
    while p < n:
        k = p
        while k >= 1:
            for j in range(k % p, n - k, 2 * k):
                for i in range(min(k, n - j - k)):
                    if (i + j) // (2 * p) == (i + j + k) // (2 * p):
                        pairs.append((i + j, i + j + k))
            k //= 2
        p *= 2
    return pairs


def _bitonic_merge_pairs(n):
    pairs = []
    s = n // 2
    while s >= 1:
        pairs += [(i, i + s) for i in range(n) if not i & s]
        s //= 2
    return pairs


def _top16_of_keys(s, out):
    n_vreg = N_KEYS // SUBLANES
    assert n_vreg == PEER_TOPK and N_KEYS == 128
    rounded = (s + 0.0).astype(BF16).astype(F32)
    b16 = lax.shift_right_logical(lax.bitcast_convert_type(rounded, jnp.int32), 16)
    code = jnp.where(b16 >= 0x8000, b16 ^ 0xFFFF, b16 | 0x8000)
    row = lax.broadcasted_iota(jnp.int32, s.shape, 0)
    keys = lax.bitcast_convert_type((code * N_KEYS + (N_KEYS - 1 - row)) | 0x4B000000, F32)

    x = [keys[v * SUBLANES:(v + 1) * SUBLANES] for v in range(n_vreg)]

    def exchange(i, j):
        x[i], x[j] = jnp.maximum(x[i], x[j]), jnp.minimum(x[i], x[j])

    for i, j in _batcher_pairs(n_vreg):
        exchange(i, j)
    yield
    for shift in (SUBLANES // 2, SUBLANES // 4, SUBLANES // 8):
        other = [pltpu.roll(a, shift, 0) for a in x]
        x = [jnp.maximum(x[i], other[n_vreg - 1 - i]) for i in range(n_vreg)]
        for i, j in _bitonic_merge_pairs(n_vreg):
            exchange(i, j)
        yield
    top = lax.bitcast_convert_type(jnp.concatenate([a[0:1] for a in x], axis=0), jnp.int32) & 0x7FFFFF
    rows = (N_KEYS - 1 - (top & (N_KEYS - 1))).astype(F32)
    code = lax.shift_right_logical(top, 7)
    b16 = jnp.where(code >= 0x8000, code ^ 0x8000, code ^ 0xFFFF)
    out.append((lax.bitcast_convert_type(b16 << 16, F32), rows))


class _TopK:
    def __init__(self, s, payload=None):
        self.s = s
        self.payload = payload
        self.rows = lax.broadcasted_iota(jnp.int32, s.shape, 0).astype(F32).astype(s.dtype)
        self.vals, self.picks = [], []

    def step(self, n):
        dt = self.s.dtype
        bound = jnp.asarray(self.s.shape[0], dt)
        for _ in range(n):
            m = jnp.max(self.s, axis=0, keepdims=True)
            first = jnp.min(jnp.where(self.s == m, self.rows, bound), axis=0, keepdims=True)
            sel = self.rows == first
            self.vals.append(m)
            if self.payload is None:
                self.picks.append(first)
            else:
                self.picks.append(jnp.max(jnp.where(sel, self.payload, -1.0), axis=0, keepdims=True))
            self.s = jnp.where(sel, jnp.asarray(-jnp.inf, dt), self.s)

    def result(self):
        return jnp.concatenate(self.vals, axis=0), jnp.concatenate(self.picks, axis=0)


def _pair_candidates(first, second):
    def pairs(a, b, combine):
        h8 = SUBLANES
        rows = [combine(a[0:1], b)]
        rows += [combine(a[k1:k1 + 1], b[0:h8]) for k1 in range(1, h8)]
        rows.append(combine(a[h8:PEER_TOPK], b[0:1]))
        return jnp.concatenate(rows, axis=0)

    (v1, i1), (v2, i2) = [tuple(a.astype(F32) for a in lst) for lst in (first, second)]
    return pairs(v1, v2, lambda a, b: a + b), pairs(i1, i2, lambda a, b: a * N_KEYS + b)


def _peer_fused_kernel(xs_ref, xm_ref, g_ref, wq_ref, sk_ref, u_ref, v_ref, *rest, split_blk):
    if split_blk is None:
        o_ref, q_scr, sel_e, sel_g, xn_scr, a_scr, bgt_scr, bg_scr, gmat, coef_scr = rest
        out_refs = ()
    else:
        *out_refs, q_scr, sel_e, sel_g, xn_scr, a_scr, bgt_scr, bg_scr, gmat, coef_scr, o_ref = rest
    i = pl.program_id(0)
    e = pl.program_id(1)
    n_steps = pl.num_programs(1)
    t = xm_ref.shape[0]
    n_half = t // SEL_T
    half_keys = N_KEYS // 2
    cur = i % 2

    def project_queries():
        xn = _rmsnorm(xs_ref[...], g_ref[...]).astype(BF16)
        q = jnp.dot(xn, wq_ref[...], preferred_element_type=F32).astype(BF16)
        for hf in range(n_half):
            for c in range(2 * PEER_HEADS):
                q_scr[hf, c] = q[hf * SEL_T:(hf + 1) * SEL_T, c * LANES:(c + 1) * LANES]

    @pl.when(e == 0)
    def _():
        @pl.when(i == 0)
        def _():
            project_queries()
            sel_e[...] = jnp.zeros_like(sel_e)
            sel_g[...] = jnp.zeros_like(sel_g)

        x = xm_ref[...]
        xn_scr[...] = _rmsnorm(x, g_ref[...]).astype(BF16)
        o_ref[...] = x
        coef_scr[0] = jnp.zeros(coef_scr.shape[1:], BF16)
        prev = 1 - cur
        lane_groups = SEL_T // LANES
        for hf in range(n_half):
            ef = sel_e[prev, hf]
            af = jnp.floor(ef * (1.0 / N_KEYS))
            bf = ef - af * N_KEYS
            gf = sel_g[prev, hf]
            a_scr[hf * SEL_T:(hf + 1) * SEL_T, :] = af.T
            bgf = bf + 0.5 * gf
            bgt_scr[hf * SEL_T:(hf + 1) * SEL_T, :] = bgf.T
            for lg in range(lane_groups):
                bg_scr[hf * lane_groups + lg] = bgf[:, lg * LANES:(lg + 1) * LANES]
        r = lax.broadcasted_iota(jnp.int32, (N_KEYS, LANES), 0)
        packed_shape = (N_KEYS // (2 * SUBLANES), 2 * SUBLANES, LANES)
        a_of_row = jnp.where(r < half_keys, 2 * r, 2 * (r - half_keys) + 1).astype(F32).astype(BF16)
        a_of_row = a_of_row.reshape(packed_shape)
        one = jnp.ones(packed_shape, BF16)
        zero = jnp.zeros(packed_shape, BF16)
        b_of_lane = lax.broadcasted_iota(jnp.int32, (N_KEYS, LANES), 1).astype(F32)
        b_of_row = r.astype(F32).astype(BF16).reshape(packed_shape)
        per_body = 32
        bodies_per_group = LANES // per_body
        row_major_every = 3

        def body(it, carry):
            grp = lax.shift_right_logical(it, bodies_per_group.bit_length() - 1)
            sub = it & (bodies_per_group - 1)
            shift = (LANES - sub * per_body) & (LANES - 1)
            bg = pltpu.roll(bg_scr[grp], shift, 1)
            tok0 = pl.multiple_of(it * per_body, per_body)
            arows = a_scr[pl.ds(tok0, per_body), :]
            bgrows = bgt_scr[pl.ds(tok0, per_body), :]
            for k in range(per_body):
                arow = jnp.broadcast_to(arows[k:k + 1], (2 * SUBLANES, LANES)).astype(BF16)[None]
                pa = jnp.where(a_of_row == arow, one, zero).reshape(N_KEYS, LANES)
                if k % row_major_every == 0:
                    bgrow = jnp.broadcast_to(bgrows[k:k + 1], (2 * SUBLANES, LANES))
                    brow = jnp.floor(bgrow)
                    grow = (2.0 * (bgrow - brow)).astype(BF16)[None]
                    qb = jnp.where(b_of_row == brow.astype(BF16)[None], grow, zero).reshape(N_KEYS, LANES)
                    tile = lax.dot_general(pa, qb, NT_DIMS, preferred_element_type=F32)
                else:
                    bgcol = jnp.broadcast_to(bg[:, k:k + 1], (N_KEYS, LANES))
                    bcol = jnp.floor(bgcol)
                    gcol = 2.0 * (bgcol - bcol)
                    qbt = jnp.where(bcol == b_of_lane, gcol, 0.0).astype(BF16)
                    tile = jnp.dot(pa, qbt, preferred_element_type=F32)
                row0 = pl.multiple_of((tok0 + k) * G_PITCH, SUBLANES)
                gmat[pl.ds(row0, half_keys), :] = pltpu.pack_elementwise(
                    [tile[0:half_keys], tile[half_keys:N_KEYS]], packed_dtype=BF16)
            return carry

        lax.fori_loop(0, t // per_body, body, 0)

    unit = jnp.minimum(e, n_half * PEER_HEADS - 1)
    half = lax.shift_right_logical(unit, PEER_HEADS.bit_length() - 1)
    h = unit & (PEER_HEADS - 1)
    rd = e & 1
    blk = jnp.minimum(e, n_steps - 2)
    eb = u_ref.shape[0]
    n_stages = 4
    exp_w = eb // n_stages
    out_w = v_ref.shape[1] // n_stages
    words_per_stage = exp_w // (2 * N_KEYS)

    def value_piece(c):
        ocols = slice(c * out_w, (c + 1) * out_w)
        o_ref[:, ocols] += jnp.dot(coef_scr[rd], v_ref[:, ocols], preferred_element_type=F32)

    def act_piece(c):
        ecols = slice(c * exp_w, (c + 1) * exp_w)
        act = lax.dot_general(xn_scr[...], u_ref[ecols, :], NT_DIMS, preferred_element_type=F32)
        gates = []
        for w in range(words_per_stage):
            word = gmat[pl.ds((blk * n_stages + c) * words_per_stage + w, t, stride=G_PITCH), :]
            gates.append(lax.bitcast_convert_type(word << 16, F32))
            gates.append(lax.bitcast_convert_type(word & jnp.int32(-65536), F32))
        coef_scr[1 - rd, :, ecols] = (_gelu(act) * jnp.concatenate(gates, axis=1)).astype(BF16)

    def retrieval():
        lists = []
        for p in range(2):
            st = lax.dot_general(sk_ref[h, p], q_scr[half, 2 * h + p], NT_DIMS, preferred_element_type=F32)
            yield from _top16_of_keys(st, lists)
            yield
        cand, expert = _pair_candidates(lists[0], lists[1])
        second_level = _TopK(cand, payload=expert)
        for _ in range(4):
            second_level.step(PEER_TOPK // 4)
            yield
        best, e_sel = second_level.result()
        ex = jnp.exp(best - best[0:1])
        slot0 = pl.multiple_of(h * PEER_TOPK, PEER_TOPK)
        sel_e[cur, half, pl.ds(slot0, PEER_TOPK), :] = e_sel
        sel_g[cur, half, pl.ds(slot0, PEER_TOPK), :] = ex / jnp.sum(ex, axis=0, keepdims=True)

    pieces = [functools.partial(f, c) for c in range(n_stages) for f in (value_piece, act_piece)]
    parts = retrieval()
    parts_per_piece = 2
    for piece in pieces:
        piece()
        for _ in range(parts_per_piece):
            next(parts, None)
    for _ in parts:
        pass

    @pl.when(e == n_steps - 1)
    def _():
        project_queries()
        if out_refs:
            first_ref, second_ref = out_refs

            @pl.when(i - 1 < split_blk)
            def _():
                first_ref[...] = o_ref[...]

            @pl.when(i - 1 >= split_blk)
            def _():
                second_ref[...] = o_ref[...]


def _peer(x, g, wq, sk, u, v, t=512, eb=1024, split_rows=None):
    n, d = x.shape
    nslot = PEER_HEADS * PEER_TOPK
    n_tok_blk = n // t
    n_exp_blk = N_KEYS * N_KEYS // eb
    n_half = t // SEL_T
    assert n_exp_blk == n_half * PEER_HEADS
    last_tok = n_tok_blk - 1
    once = pl.Buffered(1)
    if split_rows is None:
        split_blk = None
        out_specs = pl.BlockSpec((t, d), lambda i, e: (jnp.maximum(i - 1, 0), 0))
        out_shape = jax.ShapeDtypeStruct((n, d), F32)
        acc = []
    else:
        split_blk = split_rows // t
        out_specs = [pl.BlockSpec((t, d), lambda i, e: (jnp.clip(i - 1, 0, split_blk - 1), 0)),
                     pl.BlockSpec((t, d), lambda i, e: (jnp.clip(i - 1 - split_blk, 0, last_tok - split_blk), 0))]
        out_shape = [jax.ShapeDtypeStruct((split_rows, d), F32), jax.ShapeDtypeStruct((n - split_rows, d), F32)]
        acc = [pltpu.VMEM((t, d), F32)]
    return pl.pallas_call(
        functools.partial(_peer_fused_kernel, split_blk=split_blk),
        grid=(n_tok_blk + 1, n_exp_blk + 1),
        in_specs=[pl.BlockSpec((t, d), lambda i, e: (jnp.minimum(i + e // n_exp_blk, last_tok), 0),
                               pipeline_mode=once),
                  pl.BlockSpec((t, d), lambda i, e: (jnp.maximum(i - 1, 0), 0)),
                  pl.BlockSpec((1, d), lambda i, e: (0, 0)),
                  pl.BlockSpec((d, 2 * PEER_HEADS * LANES), lambda i, e: (0, 0), pipeline_mode=once),
                  pl.BlockSpec((PEER_HEADS, 2, N_KEYS, LANES), lambda i, e: (0, 0, 0, 0), pipeline_mode=once),
                  pl.BlockSpec((eb, d), lambda i, e: (jnp.minimum(e, n_exp_blk - 1), 0)),
                  pl.BlockSpec((eb, d), lambda i, e: (jnp.maximum(e - 1, 0), 0))],
        out_specs=out_specs,
        out_shape=out_shape,
        scratch_shapes=[pltpu.VMEM((n_half, 2 * PEER_HEADS, SEL_T, LANES), BF16),
                        pltpu.VMEM((2, n_half, nslot, SEL_T), F32),
                        pltpu.VMEM((2, n_half, nslot, SEL_T), F32),
                        pltpu.VMEM((t, d), BF16),
                        pltpu.VMEM((t, nslot), F32),
                        pltpu.VMEM((t, nslot), F32),
                        pltpu.VMEM((t // LANES, nslot, LANES), F32),
                        pltpu.VMEM((t * G_PITCH, LANES), jnp.int32),
                        pltpu.VMEM((2, t, eb), BF16)] + acc,
        compiler_params=_cparams(2),
        name="peer",
    )(x, x, g.reshape(1, d), wq, sk, u, v)


def _rope_tables(pos):
    half = HEAD_DIM // 2
    inv_freq = ROPE_THETA ** (-jnp.arange(half, dtype=F32) / half)
    ang = pos.astype(F32)[:, None] * inv_freq[None, :]
    cos, sin = jnp.cos(ang), jnp.sin(ang)
    reps = LANES // HEAD_DIM
    cos_t = jnp.tile(jnp.concatenate([cos, cos], axis=1), (1, reps))
    sin_t = jnp.tile(jnp.concatenate([-sin, sin], axis=1), (1, reps))
    return cos_t, sin_t


def kernel(x_prompt, x_sample, cache_conv, cache_win_k, cache_win_v, state_mlstm_C, state_mlstm_n,
           state_mlstm_m, norm_mix, norm_ffn, ab_w_in, ab_conv_w, ab_q_gain, ab_k_gain, ab_sinks, ab_w_out,
           ml_w_in, ml_gate_bias, ml_out_gain, ml_w_out, peer_w_q, peer_sub_keys, peer_u, peer_v):
    n_batch, seq, d = x_prompt.shape
    n_seq, t_len, _ = x_sample.shape
    n_prompt = n_batch * seq
    assert d == D_MODEL and t_len == SUBLANES and norm_mix.shape[0] == 2

    xp = x_prompt.reshape(n_prompt, d)
    xs = x_sample.reshape(n_seq * t_len, d)

    cos_p, sin_p = _rope_tables(jnp.arange(seq, dtype=jnp.int32))
    cos_s, sin_s = _rope_tables(PAST_LEN + jnp.arange(t_len, dtype=jnp.int32))
    bt = 16
    cos_s, sin_s = jnp.tile(cos_s, (bt, 1)), jnp.tile(sin_s, (bt, 1))
    lane = jnp.arange(LANES)
    seg = (lane[:, None] // HEAD_DIM == lane[None, :] // HEAD_DIM).astype(BF16)
    reps = LANES // HEAD_DIM
    qg = jnp.tile(ab_q_gain[0], reps).reshape(1, LANES)
    kg = jnp.tile(ab_k_gain[0], reps).reshape(1, LANES)
    wo_ab = ab_w_out[0].astype(BF16)

    z = _norm_proj([xp, xs], norm_mix[0], ab_w_in[0].astype(BF16))
    y_p, k_p, v_p, c_p = _ab_prompt(z, xp, n_batch, seq, cos_p, sin_p, ab_conv_w[0], qg, kg, ab_sinks[0],
                                    seg, wo_ab)
    y_s, c_s, k_s, v_s = _ab_sample(z, xs, n_prompt, n_seq, t_len, cos_s, sin_s, ab_conv_w[0], qg, kg,
                                    ab_sinks[0], seg, wo_ab, cache_conv[0],
                                    cache_win_k[0].reshape(n_seq, WINDOW, LANES),
                                    cache_win_v[0].reshape(n_seq, WINDOW, LANES), bt=bt)
    x = jnp.concatenate([y_p.reshape(n_prompt, d), y_s], axis=0)
    x = _peer(x, norm_ffn[0], peer_w_q[0].astype(BF16), peer_sub_keys[0].astype(BF16),
              peer_u[0].astype(BF16), peer_v[0].astype(BF16))

    n_gate = 2 * ML_HEADS
    w_in = jnp.pad(ml_w_in[0], ((0, 0), (0, ML_IN_PAD - ml_w_in.shape[2]))).astype(BF16)
    bias = jnp.pad(ml_gate_bias[0], (0, LANES - n_gate)).reshape(1, LANES)
    og = ml_out_gain[0].reshape(1, D_MODEL)
    idx = jnp.arange(ML_CHUNK)
    tril = (idx[None, :] <= idx[:, None]).astype(BF16)
    wo_ml = ml_w_out[0].astype(BF16)

    z = _norm_proj([x], norm_mix[1], w_in)
    y_p, cm_p, nm_p, mm_p = _mlstm_prompt(z, x, n_batch, seq, bias, og, tril, wo_ml)
    y_s, cm_s, nm_s, mm_s = _mlstm_sample(z, x, n_prompt, n_seq, t_len, bias, og, tril, wo_ml,
                                          state_mlstm_C[0], state_mlstm_n[0],
                                          state_mlstm_m[0].reshape(n_seq, 1, ML_HEADS))
    x = jnp.concatenate([y_p.reshape(n_prompt, d), y_s], axis=0)
    out_p, out_s = _peer(x, norm_ffn[1], peer_w_q[1].astype(BF16), peer_sub_keys[1].astype(BF16),
                         peer_u[1].astype(BF16), peer_v[1].astype(BF16), split_rows=n_prompt)

    y_prompt = out_p.reshape(n_batch, seq, d)
    y_sample = out_s.reshape(n_seq, t_len, d)
    kv_shape_p = (1, n_batch, WINDOW, N_KV_HEADS, HEAD_DIM)
    kv_shape_s = (1, n_seq, WINDOW, N_KV_HEADS, HEAD_DIM)
    return (y_prompt, y_sample,
            c_p[:, SUBLANES - 2:, :][None], k_p.reshape(kv_shape_p), v_p.reshape(kv_shape_p),
            cm_p[None], nm_p[None], mm_p.reshape(1, n_batch, ML_HEADS),
            c_s[None], k_s.reshape(kv_shape_s), v_s.reshape(kv_shape_s),
            cm_s[None], nm_s[None], mm_s.reshape(1, n_seq, ML_HEADS))
```

```python
import functools

import jax
import jax.numpy as jnp
from jax import lax
from jax.experimental import pallas as pl
from jax.experimental.pallas import tpu as pltpu

F32 = jnp.float32
BF16 = jnp.bfloat16
EPS = 1e-6

D_MODEL = 1024
CONV_DIM = 512
N_Q_HEADS = 8
N_KV_HEADS = 2
HEAD_DIM = 64
WINDOW = 128
ROPE_THETA = 10000.0
AB_IN = 2304
ML_HEADS = 4
ML_QK = 128
ML_V = 256
ML_CHUNK = 128
ML_GATE_COL = 3072
ML_IN_PAD = ML_GATE_COL + 128
N_KEYS = 128
PEER_HEADS = 8
PEER_TOPK = 16
PAST_LEN = 16384

LANES = 128
SUBLANES = 8
G_PITCH = N_KEYS // 2 + SUBLANES
VMEM_LIMIT = 56 * 1024 * 1024

NT_DIMS = (((1,), (1,)), ((), ()))
TN_DIMS = (((0,), (0,)), ((), ()))


def _cparams(n_axes, vmem=VMEM_LIMIT):
    return pltpu.CompilerParams(dimension_semantics=("arbitrary",) * n_axes, vmem_limit_bytes=vmem)


def _rmsnorm(x, g):
    return x * lax.rsqrt(jnp.mean(x * x, axis=-1, keepdims=True) + EPS) * g


def _bf16_pieces(a, terms):
    pieces = []
    rem = a
    for _ in range(terms):
        piece = rem.astype(BF16)
        rem = rem - piece.astype(F32)
        pieces.append(piece)
    return pieces


def _split_dot(a, b_bf16, terms=2):
    return sum(jnp.dot(p, b_bf16, preferred_element_type=F32) for p in _bf16_pieces(a, terms))


def _norm_proj_kernel(*refs, starts):
    x_refs = refs[:len(starts)]
    g_ref, w_ref, o_ref = refs[len(starts):]
    i = pl.program_id(0)
    x = x_refs[0][...]
    for x_ref, start in zip(x_refs[1:], starts[1:]):
        x = jnp.where(i >= start, x_ref[...], x)
    r = _rmsnorm(x, g_ref[...])
    o_ref[...] = jnp.dot(r.astype(BF16), w_ref[...], preferred_element_type=F32)


def _norm_proj(x_parts, g, w_bf16, tm=512):
    d = x_parts[0].shape[1]
    nout = w_bf16.shape[1]
    blocks = [p.shape[0] // tm for p in x_parts]
    starts = tuple(sum(blocks[:k]) for k in range(len(blocks)))
    part_spec = lambda start, nblk: pl.BlockSpec((tm, d), lambda i: (jnp.clip(i - start, 0, nblk - 1), 0))
    return pl.pallas_call(
        functools.partial(_norm_proj_kernel, starts=starts),
        grid=(sum(blocks),),
        in_specs=[part_spec(s, nb) for s, nb in zip(starts, blocks)] + [
            pl.BlockSpec((1, d), lambda i: (0, 0)),
            pl.BlockSpec((d, nout), lambda i: (0, 0))],
        out_specs=pl.BlockSpec((tm, nout), lambda i: (i, 0)),
        out_shape=jax.ShapeDtypeStruct((sum(blocks) * tm, nout), F32),
        compiler_params=_cparams(1),
        name="norm_proj",
    )(*x_parts, g.reshape(1, d), w_bf16)


def _headnorm_rope(xc, gain, cos, sin, seg, hi_half):
    ss = _split_dot(xc * xc, seg)
    xn = xc * lax.rsqrt(ss * (1.0 / HEAD_DIM) + EPS) * gain
    partner = jnp.where(hi_half, pltpu.roll(xn, 32, 1), pltpu.roll(xn, 96, 1))
    return xn * cos + partner * sin


def _softmax_sink(s, mask, sink):
    s = jnp.where(mask, s, -1e30)
    m = jnp.maximum(jnp.max(s, axis=-1, keepdims=True), sink)
    p = jnp.exp(s - m)
    denom = jnp.sum(p, axis=-1, keepdims=True) + jnp.exp(sink - m)
    return (p / denom).astype(BF16)


def _ab_prompt_kernel(sink_ref, *refs, n_batch):
    z_refs, x_refs = refs[0:n_batch], refs[n_batch:2 * n_batch]
    (cos_ref, sin_ref, cw_ref, qg_ref, kg_ref, seg_ref, wo_ref,
     y_ref, kst_ref, vst_ref, cst_ref, pk_ref, pv_ref, pu_ref) = refs[2 * n_batch:]
    j = pl.program_id(0)

    @pl.when(j == 0)
    def _():
        pk_ref[...] = jnp.zeros_like(pk_ref)
        pv_ref[...] = jnp.zeros_like(pv_ref)
        pu_ref[...] = jnp.zeros_like(pu_ref)

    for b in range(n_batch):
        _ab_prompt_block(j, sink_ref, z_refs[b], x_refs[b], cos_ref, sin_ref, cw_ref, qg_ref, kg_ref, seg_ref,
                         wo_ref, y_ref.at[b], kst_ref.at[b], vst_ref.at[b], cst_ref.at[b],
                         pk_ref.at[b], pv_ref.at[b], pu_ref.at[b])


def _ab_prompt_block(j, sink_ref, z_ref, x_ref, cos_ref, sin_ref, cw_ref, qg_ref, kg_ref, seg_ref, wo_ref,
                     y_ref, kst_ref, vst_ref, cst_ref, pk_ref, pv_ref, pu_ref):
    blk = z_ref.shape[0]
    cos = cos_ref[...]
    sin = sin_ref[...]
    seg = seg_ref[...]
    hi_half = (lax.broadcasted_iota(jnp.int32, (blk, LANES), 1) & 32) != 0

    gate_b = z_ref[:, 0:CONV_DIM]
    u = z_ref[:, CONV_DIM:2 * CONV_DIM] * z_ref[:, 2 * CONV_DIM:3 * CONV_DIM]
    ng = blk // SUBLANES
    u3 = u.reshape(ng, SUBLANES, CONV_DIM)
    ext = jnp.concatenate([pu_ref[...][None], u3], axis=0)
    t8 = lax.broadcasted_iota(jnp.int32, (ng, SUBLANES, CONV_DIM), 1)
    r1 = pltpu.roll(ext, 1, 1)
    r2 = pltpu.roll(ext, 2, 1)
    um1 = jnp.where(t8 >= 1, r1[1:], r1[:-1])
    um2 = jnp.where(t8 >= 2, r2[1:], r2[:-1])
    cw = cw_ref[...]
    conv = cw[0:1][None] * um2 + cw[1:2][None] * um1 + cw[2:3][None] * u3
    yconv = gate_b * conv.reshape(blk, CONV_DIM)

    q0 = 3 * CONV_DIM
    k0 = q0 + N_Q_HEADS * HEAD_DIM
    v0 = k0 + N_KV_HEADS * HEAD_DIM
    qg = qg_ref[...]
    qr = [_headnorm_rope(z_ref[:, q0 + c * LANES:q0 + (c + 1) * LANES], qg, cos, sin, seg, hi_half)
          for c in range(N_Q_HEADS * HEAD_DIM // LANES)]
    kr = _headnorm_rope(z_ref[:, k0:k0 + LANES], kg_ref[...], cos, sin, seg, hi_half)
    v = z_ref[:, v0:v0 + LANES]
    pk = pk_ref[...]
    pv = pv_ref[...]

    row = lax.broadcasted_iota(jnp.int32, (blk, 2 * blk), 0)
    col = lax.broadcasted_iota(jnp.int32, (blk, 2 * blk), 1)
    row_prev = row + jnp.where(j == 0, 2 * blk, 0)
    mask = ((col < blk) & (col > row_prev)) | ((col >= blk) & (col - blk <= row))

    kks, vvs = [], []
    for g in range(N_KV_HEADS):
        sl = slice(g * HEAD_DIM, (g + 1) * HEAD_DIM)
        kks.append(jnp.concatenate([pk[:, sl], kr[:, sl]], axis=0).astype(BF16))
        vvs.append(jnp.concatenate([pv[:, sl], v[:, sl]], axis=0).astype(BF16))
    outs = []
    for h in range(N_Q_HEADS):
        g = h // (N_Q_HEADS // N_KV_HEADS)
        qh = qr[h // 2][:, (h % 2) * HEAD_DIM:(h % 2 + 1) * HEAD_DIM].astype(BF16)
        s = lax.dot_general(qh, kks[g], NT_DIMS, preferred_element_type=F32) * (HEAD_DIM ** -0.5)
        p = _softmax_sink(s, mask, sink_ref[h])
        outs.append(jnp.dot(p, vvs[g], preferred_element_type=F32))
    attn = jnp.concatenate(outs, axis=1)

    y = (jnp.dot(yconv.astype(BF16), wo_ref[0:CONV_DIM, :], preferred_element_type=F32)
         + jnp.dot(attn.astype(BF16), wo_ref[CONV_DIM:2 * CONV_DIM, :], preferred_element_type=F32))
    y_ref[...] = y + x_ref[...]

    pk_ref[...] = kr
    pv_ref[...] = v
    pu_ref[...] = u3[ng - 1]
    kst_ref[...] = kr
    vst_ref[...] = v
    cst_ref[...] = u3[ng - 1]


def _ab_prompt(z, x, n_batch, seq, cos, sin, cw, qg, kg, sinks, seg, wo):
    blk = WINDOW
    nb = seq // blk
    const = lambda j, s: (0, 0)
    whole = lambda j, s: (0, 0, 0)
    tok_specs = lambda width: [pl.BlockSpec((blk, width), functools.partial(lambda j, s, b: (b * nb + j, 0), b=b))
                               for b in range(n_batch)]
    grid_spec = pltpu.PrefetchScalarGridSpec(
        num_scalar_prefetch=1,
        grid=(nb,),
        in_specs=tok_specs(AB_IN) + tok_specs(D_MODEL) + [
            pl.BlockSpec((blk, LANES), lambda j, s: (j, 0)),
            pl.BlockSpec((blk, LANES), lambda j, s: (j, 0)),
            pl.BlockSpec((3, CONV_DIM), const),
            pl.BlockSpec((1, LANES), const),
            pl.BlockSpec((1, LANES), const),
            pl.BlockSpec((LANES, LANES), const),
            pl.BlockSpec((D_MODEL, D_MODEL), const)],
        out_specs=[pl.BlockSpec((n_batch, blk, D_MODEL), lambda j, s: (0, j, 0)),
                   pl.BlockSpec((n_batch, blk, LANES), whole),
                   pl.BlockSpec((n_batch, blk, LANES), whole),
                   pl.BlockSpec((n_batch, SUBLANES, CONV_DIM), whole)],
        scratch_shapes=[pltpu.VMEM((n_batch, blk, LANES), F32), pltpu.VMEM((n_batch, blk, LANES), F32),
                        pltpu.VMEM((n_batch, SUBLANES, CONV_DIM), F32)])
    return pl.pallas_call(
        functools.partial(_ab_prompt_kernel, n_batch=n_batch),
        grid_spec=grid_spec,
        out_shape=[jax.ShapeDtypeStruct((n_batch, seq, D_MODEL), F32),
                   jax.ShapeDtypeStruct((n_batch, blk, LANES), F32),
                   jax.ShapeDtypeStruct((n_batch, blk, LANES), F32),
                   jax.ShapeDtypeStruct((n_batch, SUBLANES, CONV_DIM), F32)],
        compiler_params=_cparams(1),
        name="ab_prompt",
    )(sinks, *([z] * n_batch), *([x] * n_batch), cos, sin, cw, qg, kg, seg, wo)


def _ab_sample_kernel(sink_ref, z_ref, x_ref, cos_ref, sin_ref, cw_ref, qg_ref, kg_ref, seg_ref, wo_ref,
                      cc_ref, ck_ref, cv_ref, y_ref, cs_ref, ks_ref, vs_ref):
    rows = z_ref.shape[0]
    t_len = SUBLANES
    bt = rows // t_len
    cos = cos_ref[...]
    sin = sin_ref[...]
    seg = seg_ref[...]
    hi_half = (lax.broadcasted_iota(jnp.int32, (rows, LANES), 1) & 32) != 0

    gate_b = z_ref[:, 0:CONV_DIM]
    u = z_ref[:, CONV_DIM:2 * CONV_DIM] * z_ref[:, 2 * CONV_DIM:3 * CONV_DIM]
    u3 = u.reshape(bt, t_len, CONV_DIM)
    cc = cc_ref[...]
    c0 = cc[:, 0:1, :]
    c1 = cc[:, 1:2, :]
    t8 = lax.broadcasted_iota(jnp.int32, (bt, t_len, CONV_DIM), 1)
    r1 = pltpu.roll(u3, 1, 1)
    r2 = pltpu.roll(u3, 2, 1)
    um1 = jnp.where(t8 >= 1, r1, c1)
    um2 = jnp.where(t8 >= 2, r2, jnp.where(t8 == 1, c1, c0))
    cw = cw_ref[...]
    conv = cw[0:1][None] * um2 + cw[1:2][None] * um1 + cw[2:3][None] * u3
    yconv = gate_b * conv.reshape(rows, CONV_DIM)
    cs_ref[...] = r2[:, 0:2, :]

    q0 = 3 * CONV_DIM
    k0 = q0 + N_Q_HEADS * HEAD_DIM
    v0 = k0 + N_KV_HEADS * HEAD_DIM
    qg = qg_ref[...]
    qr = [_headnorm_rope(z_ref[:, q0 + c * LANES:q0 + (c + 1) * LANES], qg, cos, sin, seg, hi_half)
          for c in range(N_Q_HEADS * HEAD_DIM // LANES)]
    kr = _headnorm_rope(z_ref[:, k0:k0 + LANES], kg_ref[...], cos, sin, seg, hi_half)
    v = z_ref[:, v0:v0 + LANES]

    group = N_Q_HEADS // N_KV_HEADS
    nq = group * t_len
    nk = 2 * WINDOW
    qrow = lax.broadcasted_iota(jnp.int32, (nq, nk), 0)
    t_q = qrow & (t_len - 1)
    col = lax.broadcasted_iota(jnp.int32, (nq, nk), 1)
    mask = (((col < WINDOW) & (col > t_q)) | ((col >= WINDOW) & (col - WINDOW <= t_q)))[None]
    hrow = lax.broadcasted_iota(jnp.int32, (nq, 1), 0) // t_len
    pad = jnp.zeros((bt, nk - WINDOW - t_len, HEAD_DIM), F32)

    outs = [None] * N_Q_HEADS
    for g in range(N_KV_HEADS):
        sl = slice(g * HEAD_DIM, (g + 1) * HEAD_DIM)
        qs = jnp.concatenate(
            [qr[h // 2][:, (h % 2) * HEAD_DIM:(h % 2 + 1) * HEAD_DIM].reshape(bt, t_len, HEAD_DIM)
             for h in range(g * group, (g + 1) * group)], axis=1)
        kk = jnp.concatenate([ck_ref[:, :, sl], kr[:, sl].reshape(bt, t_len, HEAD_DIM), pad], axis=1)
        vv = jnp.concatenate([cv_ref[:, :, sl], v[:, sl].reshape(bt, t_len, HEAD_DIM), pad], axis=1)
        s = jnp.einsum('bqd,bkd->bqk', qs.astype(BF16), kk.astype(BF16),
                       preferred_element_type=F32) * (HEAD_DIM ** -0.5)
        sink = jnp.zeros((nq, 1), F32)
        for hh in range(group):
            sink = jnp.where(hrow == hh, sink_ref[g * group + hh], sink)
        p = _softmax_sink(s, mask, sink[None])
        o = jnp.einsum('bqk,bkd->bqd', p, vv.astype(BF16), preferred_element_type=F32)
        for hh in range(group):
            outs[g * group + hh] = o[:, hh * t_len:(hh + 1) * t_len, :].reshape(rows, HEAD_DIM)
    attn = jnp.concatenate(outs, axis=1)

    y = (jnp.dot(yconv.astype(BF16), wo_ref[0:CONV_DIM, :], preferred_element_type=F32)
         + jnp.dot(attn.astype(BF16), wo_ref[CONV_DIM:2 * CONV_DIM, :], preferred_element_type=F32))
    y_ref[...] = y + x_ref[...]

    keep = WINDOW - t_len
    ks_ref[:, 0:keep, :] = ck_ref[:, t_len:WINDOW, :]
    ks_ref[:, keep:WINDOW, :] = kr.reshape(bt, t_len, LANES)
    vs_ref[:, 0:keep, :] = cv_ref[:, t_len:WINDOW, :]
    vs_ref[:, keep:WINDOW, :] = v.reshape(bt, t_len, LANES)


def _ab_sample(z, x, row0, n_seq, t_len, cos, sin, cw, qg, kg, sinks, seg, wo, cc, ck, cv, bt=16):
    rows = bt * t_len
    blk0 = row0 // rows
    tok = lambda i, s: (blk0 + i, 0)
    const = lambda i, s: (0, 0)
    seq3 = lambda i, s: (i, 0, 0)
    grid_spec = pltpu.PrefetchScalarGridSpec(
        num_scalar_prefetch=1,
        grid=(n_seq // bt,),
        in_specs=[pl.BlockSpec((rows, AB_IN), tok),
                  pl.BlockSpec((rows, D_MODEL), lambda i, s: (i, 0)),
                  pl.BlockSpec((rows, LANES), const),
                  pl.BlockSpec((rows, LANES), const),
                  pl.BlockSpec((3, CONV_DIM), const),
                  pl.BlockSpec((1, LANES), const),
                  pl.BlockSpec((1, LANES), const),
                  pl.BlockSpec((LANES, LANES), const),
                  pl.BlockSpec((D_MODEL, D_MODEL), const),
                  pl.BlockSpec((bt, 2, CONV_DIM), seq3),
                  pl.BlockSpec((bt, WINDOW, LANES), seq3),
                  pl.BlockSpec((bt, WINDOW, LANES), seq3)],
        out_specs=[pl.BlockSpec((rows, D_MODEL), lambda i, s: (i, 0)),
                   pl.BlockSpec((bt, 2, CONV_DIM), seq3),
                   pl.BlockSpec((bt, WINDOW, LANES), seq3),
                   pl.BlockSpec((bt, WINDOW, LANES), seq3)])
    return pl.pallas_call(
        _ab_sample_kernel,
        grid_spec=grid_spec,
        out_shape=[jax.ShapeDtypeStruct((n_seq * t_len, D_MODEL), F32),
                   jax.ShapeDtypeStruct((n_seq, 2, CONV_DIM), F32),
                   jax.ShapeDtypeStruct((n_seq, WINDOW, LANES), F32),
                   jax.ShapeDtypeStruct((n_seq, WINDOW, LANES), F32)],
        compiler_params=_cparams(1),
        name="ab_sample",
    )(sinks, z, x, cos, sin, cw, qg, kg, seg, wo, cc, ck, cv)


def _log_sigmoid(x):
    return jnp.minimum(x, 0.0) - jnp.log(1.0 + jnp.exp(-jnp.abs(x)))


def _mlstm_chunk(z, bias, og, tril, c_src, n_src, m_src, c_dst, n_dst, m_dst, n_real, rows):
    L = z.shape[0]
    gates = z[:, ML_GATE_COL:ML_GATE_COL + LANES] + bias
    if n_real < L:
        live = lax.broadcasted_iota(jnp.int32, (L, LANES), 0) < n_real
        li_all = jnp.where(live, gates, -1e30)
        lf_all = jnp.where(live, _log_sigmoid(gates), 0.0)
    else:
        li_all = gates
        lf_all = _log_sigmoid(gates)
    lf_pieces = _bf16_pieces(lf_all, 3)
    f_col_all = sum(jnp.dot(tril[0:rows], p, preferred_element_type=F32) for p in lf_pieces)
    f_row_all = sum(lax.dot_general(p, tril, (((0,), (1,)), ((), ())), preferred_element_type=F32)
                    for p in lf_pieces)
    li_t = li_all.T
    rr = lax.broadcasted_iota(jnp.int32, (rows, L), 0)
    cc = lax.broadcasted_iota(jnp.int32, (rows, L), 1)
    causal = cc <= rr

    outs, m_new_all = [], []
    for h in range(ML_HEADS):
        f_col = f_col_all[:, ML_HEADS + h:ML_HEADS + h + 1]
        f_row = f_row_all[ML_HEADS + h:ML_HEADS + h + 1, :]
        li_row = li_t[h:h + 1, :]
        li_col = li_all[0:rows, h:h + 1]
        m0 = m_src[0:1, h:h + 1]
        c0 = c_src[h]
        n0 = n_src[h:h + 1, :]
        qh = z[0:rows, h * ML_QK:(h + 1) * ML_QK]
        kh = z[:, ML_HEADS * ML_QK + h * ML_QK:ML_HEADS * ML_QK + (h + 1) * ML_QK] * (ML_QK ** -0.5)
        v_off = 2 * ML_HEADS * ML_QK
        vh = z[:, v_off + h * ML_V:v_off + (h + 1) * ML_V]
        o_off = v_off + ML_HEADS * ML_V
        oh = z[0:rows, o_off + h * ML_V:o_off + (h + 1) * ML_V]
        qb = qh.astype(BF16)
        vb = vh.astype(BF16)

        dmat = jnp.where(causal, f_col - f_row + li_row, -jnp.inf)
        gcar = f_col + m0
        m_t = jnp.maximum(jnp.max(dmat, axis=-1, keepdims=True), gcar)
        w = jnp.exp(dmat - m_t)
        s = lax.dot_general(qb, kh.astype(BF16), NT_DIMS, preferred_element_type=F32) * w
        carry = jnp.exp(gcar - m_t)
        num = (jnp.dot(s.astype(BF16), vb, preferred_element_type=F32)
               + jnp.dot(qb, c0.astype(BF16), preferred_element_type=F32) * carry)
        den = jnp.sum(s, axis=-1, keepdims=True) + carry * jnp.sum(qh * n0, axis=-1, keepdims=True)
        hout = num / jnp.maximum(jnp.abs(den), jnp.exp(-m_t))

        f_last = f_col[n_real - 1:n_real, :]
        w_end = f_last - f_col + li_col
        m_new = jnp.maximum(f_last + m0, jnp.max(w_end, axis=0, keepdims=True))
        a_end = jnp.exp(w_end - m_new)
        scale = jnp.exp(f_last + m0 - m_new)
        ka = kh[0:rows] * a_end
        c_dst[h] = scale * c0 + lax.dot_general(ka.astype(BF16), vb[0:rows], TN_DIMS,
                                                preferred_element_type=F32)
        n_dst[h:h + 1, :] = scale * n0 + jnp.sum(ka, axis=0, keepdims=True)
        m_new_all.append(m_new)

        hn = _rmsnorm(hout, og[:, h * ML_V:(h + 1) * ML_V])
        outs.append(jax.nn.sigmoid(oh) * hn)
    m_dst[...] = jnp.concatenate(m_new_all, axis=1)
    return jnp.concatenate(outs, axis=1)


def _mlstm_prompt_kernel(*refs, n_batch):
    z_refs, x_refs = refs[0:n_batch], refs[n_batch:2 * n_batch]
    bias_ref, og_ref, tril_ref, wo_ref, y_ref, c_ref, n_ref, m_ref = refs[2 * n_batch:]

    @pl.when(pl.program_id(0) == 0)
    def _():
        c_ref[...] = jnp.zeros_like(c_ref)
        n_ref[...] = jnp.zeros_like(n_ref)
        m_ref[...] = jnp.zeros_like(m_ref)

    for b in range(n_batch):
        state = (c_ref.at[b], n_ref.at[b], m_ref.at[b])
        out = _mlstm_chunk(z_refs[b][...], bias_ref[...], og_ref[...], tril_ref[...], *state, *state,
                           ML_CHUNK, ML_CHUNK)
        y_ref[b] = jnp.dot(out.astype(BF16), wo_ref[...], preferred_element_type=F32) + x_refs[b][...]


def _mlstm_sample_kernel(z_ref, x_ref, bias_ref, og_ref, tril_ref, wo_ref, c0_ref, n0_ref, m0_ref,
                         y_ref, c_ref, n_ref, m_ref, *, t_len):
    n_here = z_ref.shape[0] // t_len
    q_rows = 2 * SUBLANES
    outs = []
    for s in range(n_here):
        zpad = jnp.concatenate([z_ref[s * t_len:(s + 1) * t_len, :],
                                jnp.zeros((ML_CHUNK - t_len, ML_IN_PAD), F32)], axis=0)
        out = _mlstm_chunk(zpad, bias_ref[...], og_ref[...], tril_ref[...],
                           c0_ref.at[s], n0_ref.at[s], m0_ref.at[s], c_ref.at[s], n_ref.at[s], m_ref.at[s],
                           t_len, q_rows)
        outs.append(out[0:t_len])
    out_all = jnp.concatenate(outs, axis=0).astype(BF16)
    y_ref[...] = jnp.dot(out_all, wo_ref[...], preferred_element_type=F32) + x_ref[...]


def _mlstm_weight_specs(const):
    return [pl.BlockSpec((1, LANES), const),
            pl.BlockSpec((1, D_MODEL), const),
            pl.BlockSpec((ML_CHUNK, ML_CHUNK), const),
            pl.BlockSpec((D_MODEL, D_MODEL), const)]


def _mlstm_prompt(z, x, n_batch, seq, bias, og, tril, wo):
    nc = seq // ML_CHUNK
    const = lambda j: (0, 0)
    tok_specs = lambda width: [pl.BlockSpec((ML_CHUNK, width), functools.partial(lambda j, b: (b * nc + j, 0), b=b))
                               for b in range(n_batch)]
    return pl.pallas_call(
        functools.partial(_mlstm_prompt_kernel, n_batch=n_batch),
        grid=(nc,),
        in_specs=tok_specs(ML_IN_PAD) + tok_specs(D_MODEL) + _mlstm_weight_specs(const),
        out_specs=[pl.BlockSpec((n_batch, ML_CHUNK, D_MODEL), lambda j: (0, j, 0)),
                   pl.BlockSpec((n_batch, ML_HEADS, ML_QK, ML_V), lambda j: (0, 0, 0, 0)),
                   pl.BlockSpec((n_batch, ML_HEADS, ML_QK), lambda j: (0, 0, 0)),
                   pl.BlockSpec((n_batch, 1, ML_HEADS), lambda j: (0, 0, 0))],
        out_shape=[jax.ShapeDtypeStruct((n_batch, seq, D_MODEL), F32),
                   jax.ShapeDtypeStruct((n_batch, ML_HEADS, ML_QK, ML_V), F32),
                   jax.ShapeDtypeStruct((n_batch, ML_HEADS, ML_QK), F32),
                   jax.ShapeDtypeStruct((n_batch, 1, ML_HEADS), F32)],
        compiler_params=_cparams(1),
        name="mlstm_prompt",
    )(*([z] * n_batch), *([x] * n_batch), bias, og, tril, wo)


def _mlstm_sample(z, x, row0, n_seq, t_len, bias, og, tril, wo, c0, n0, m0, seqs_per_step=8):
    rows = seqs_per_step * t_len
    blk0 = row0 // rows
    tok = lambda i: (blk0 + i, 0)
    const = lambda i: (0, 0)
    st4 = lambda i: (i, 0, 0, 0)
    st3 = lambda i: (i, 0, 0)
    state_specs = [pl.BlockSpec((seqs_per_step, ML_HEADS, ML_QK, ML_V), st4),
                   pl.BlockSpec((seqs_per_step, ML_HEADS, ML_QK), st3),
                   pl.BlockSpec((seqs_per_step, 1, ML_HEADS), st3)]
    return pl.pallas_call(
        functools.partial(_mlstm_sample_kernel, t_len=t_len),
        grid=(n_seq // seqs_per_step,),
        in_specs=[pl.BlockSpec((rows, ML_IN_PAD), tok),
                  pl.BlockSpec((rows, D_MODEL), tok)] + _mlstm_weight_specs(const) + state_specs,
        out_specs=[pl.BlockSpec((rows, D_MODEL), lambda i: (i, 0))] + state_specs,
        out_shape=[jax.ShapeDtypeStruct((n_seq * t_len, D_MODEL), F32),
                   jax.ShapeDtypeStruct((n_seq, ML_HEADS, ML_QK, ML_V), F32),
                   jax.ShapeDtypeStruct((n_seq, ML_HEADS, ML_QK), F32),
                   jax.ShapeDtypeStruct((n_seq, 1, ML_HEADS), F32)],
        compiler_params=_cparams(1),
        name="mlstm_sample",
    )(z, x, bias, og, tril, wo, c0, n0, m0)


def _twice_gelu(x):
    return x * (1.0 + lax.erf(x * 0.7071067811865476))


SEL_T = 256


def _batcher_pairs(n):
    pairs = []
    p = 1
    while p < n:
        k = p
        while k >= 1:
            for j in range(k % p, n - k, 2 * k):
                for i in range(min(k, n - j - k)):
                    if (i + j) // (2 * p) == (i + j + k) // (2 * p):
                        pairs.append((i + j, i + j + k))
            k //= 2
        p *= 2
    return pairs


def _bitonic_merge_pairs(n):
    pairs = []
    s = n // 2
    while s >= 1:
        pairs += [(i, i + s) for i in range(n) if not i & s]
        s //= 2
    return pairs


def _top16_of_keys(s, out):
    n_vreg = N_KEYS // SUBLANES
    assert n_vreg == PEER_TOPK and N_KEYS == 128
    rounded = (s + 0.0).astype(BF16).astype(F32)
    b16 = lax.shift_right_logical(lax.bitcast_convert_type(rounded, jnp.int32), 16)
    code = jnp.where(b16 >= 0x8000, b16 ^ 0xFFFF, b16 | 0x8000)
    row = lax.broadcasted_iota(jnp.int32, s.shape, 0)
    keys = lax.bitcast_convert_type((code * N_KEYS + (N_KEYS - 1 - row)) | 0x4B000000, F32)

    x = [keys[v * SUBLANES:(v + 1) * SUBLANES] for v in range(n_vreg)]

    def exchange(i, j):
        x[i], x[j] = jnp.maximum(x[i], x[j]), jnp.minimum(x[i], x[j])

    for i, j in _batcher_pairs(n_vreg):
        exchange(i, j)
    yield
    for shift in (SUBLANES // 2, SUBLANES // 4, SUBLANES // 8):
        other = [pltpu.roll(a, shift, 0) for a in x]
        x = [jnp.maximum(x[i], other[n_vreg - 1 - i]) for i in range(n_vreg)]
        for i, j in _bitonic_merge_pairs(n_vreg):
            exchange(i, j)
        yield
    top = lax.bitcast_convert_type(jnp.concatenate([a[0:1] for a in x], axis=0), jnp.int32) & 0x7FFFFF
    rows = (N_KEYS - 1 - (top & (N_KEYS - 1))).astype(F32)
    code = lax.shift_right_logical(top, 7)
    b16 = jnp.where(code >= 0x8000, code ^ 0x8000, code ^ 0xFFFF)
    out.append((lax.bitcast_convert_type(b16 << 16, F32), rows))


class _TopK:
    def __init__(self, s, payload=None):
        self.s = s
        self.payload = payload
        self.rows = lax.broadcasted_iota(jnp.int32, s.shape, 0).astype(F32).astype(s.dtype)
        self.vals, self.picks = [], []

    def step(self, n):
        dt = self.s.dtype
        bound = jnp.asarray(self.s.shape[0], dt)
        for _ in range(n):
            m = jnp.max(self.s, axis=0, keepdims=True)
            first = jnp.min(jnp.where(self.s == m, self.rows, bound), axis=0, keepdims=True)
            sel = self.rows == first
            self.vals.append(m)
            if self.payload is None:
                self.picks.append(first)
            else:
                self.picks.append(jnp.max(jnp.where(sel, self.payload, -1.0), axis=0, keepdims=True))
            self.s = jnp.where(sel, jnp.asarray(-jnp.inf, dt), self.s)

    def result(self):
        return jnp.concatenate(self.vals, axis=0), jnp.concatenate(self.picks, axis=0)


def _pair_candidates(first, second):
    def pairs(a, b, combine):
        h8 = SUBLANES
        rows = [combine(a[0:1], b)]
        rows += [combine(a[k1:k1 + 1], b[0:h8]) for k1 in range(1, h8)]
        rows.append(combine(a[h8:PEER_TOPK], b[0:1]))
        return jnp.concatenate(rows, axis=0)

    (v1, i1), (v2, i2) = [tuple(a.astype(F32) for a in lst) for lst in (first, second)]
    return pairs(v1, v2, lambda a, b: a + b), pairs(i1, i2, lambda a, b: a * N_KEYS + b)


def _peer_fused_kernel(xs_ref, xm_ref, g_ref, wq_ref, sk_ref, u_ref, v_ref, *rest, split_blk):
    if split_blk is None:
        o_ref, q_scr, sel_e, sel_g, xn_scr, a_scr, bgt_scr, bg_scr, gmat, coef_scr = rest
        out_refs = ()
    else:
        *out_refs, q_scr, sel_e, sel_g, xn_scr, a_scr, bgt_scr, bg_scr, gmat, coef_scr, o_ref = rest
    i = pl.program_id(0)
    e = pl.program_id(1)
    n_steps = pl.num_programs(1)
    t = xm_ref.shape[0]
    n_half = t // SEL_T
    half_keys = N_KEYS // 2
    cur = i % 2

    def project_queries():
        xn = _rmsnorm(xs_ref[...], g_ref[...]).astype(BF16)
        q = jnp.dot(xn, wq_ref[...], preferred_element_type=F32).astype(BF16)
        for hf in range(n_half):
            for c in range(2 * PEER_HEADS):
                q_scr[hf, c] = q[hf * SEL_T:(hf + 1) * SEL_T, c * LANES:(c + 1) * LANES]

    @pl.when(e == 0)
    def _():
        @pl.when(i == 0)
        def _():
            project_queries()
            sel_e[...] = jnp.zeros_like(sel_e)
            sel_g[...] = jnp.zeros_like(sel_g)

        x = xm_ref[...]
        xn_scr[...] = _rmsnorm(x, g_ref[...]).astype(BF16)
        o_ref[...] = x
        coef_scr[0] = jnp.zeros(coef_scr.shape[1:], BF16)
        prev = 1 - cur
        lane_groups = SEL_T // LANES
        for hf in range(n_half):
            ef = sel_e[prev, hf]
            af = jnp.floor(ef * (1.0 / N_KEYS))
            bf = ef - af * N_KEYS
            gf = sel_g[prev, hf]
            a_scr[hf * SEL_T:(hf + 1) * SEL_T, :] = af.T
            bgf = bf + 0.5 * gf
            bgt_scr[hf * SEL_T:(hf + 1) * SEL_T, :] = bgf.T
            for lg in range(lane_groups):
                bg_scr[hf * lane_groups + lg] = bgf[:, lg * LANES:(lg + 1) * LANES]
        r = lax.broadcasted_iota(jnp.int32, (N_KEYS, LANES), 0)
        packed_shape = (N_KEYS // (2 * SUBLANES), 2 * SUBLANES, LANES)
        a_of_row = jnp.where(r < half_keys, 2 * r, 2 * (r - half_keys) + 1).astype(F32).astype(BF16)
        a_of_row = a_of_row.reshape(packed_shape)
        one = jnp.ones(packed_shape, BF16)
        zero = jnp.zeros(packed_shape, BF16)
        b_of_lane = lax.broadcasted_iota(jnp.int32, (N_KEYS, LANES), 1).astype(F32)
        b_of_row = r.astype(F32).astype(BF16).reshape(packed_shape)
        per_body = 32
        bodies_per_group = LANES // per_body
        row_major_every = 3

        def body(it, carry):
            grp = lax.shift_right_logical(it, bodies_per_group.bit_length() - 1)
            sub = it & (bodies_per_group - 1)
            shift = (LANES - sub * per_body) & (LANES - 1)
            bg = pltpu.roll(bg_scr[grp], shift, 1)
            tok0 = pl.multiple_of(it * per_body, per_body)
            arows = a_scr[pl.ds(tok0, per_body), :]
            bgrows = bgt_scr[pl.ds(tok0, per_body), :]
            for k in range(per_body):
                arow = jnp.broadcast_to(arows[k:k + 1], (2 * SUBLANES, LANES)).astype(BF16)[None]
                pa = jnp.where(a_of_row == arow, one, zero).reshape(N_KEYS, LANES)
                if k % row_major_every == 0:
                    bgrow = jnp.broadcast_to(bgrows[k:k + 1], (2 * SUBLANES, LANES))
                    brow = jnp.floor(bgrow)
                    grow = (bgrow - brow).astype(BF16)[None]
                    qb = jnp.where(b_of_row == brow.astype(BF16)[None], grow, zero).reshape(N_KEYS, LANES)
                    tile = lax.dot_general(pa, qb, NT_DIMS, preferred_element_type=F32)
                else:
                    bgcol = jnp.broadcast_to(bg[:, k:k + 1], (N_KEYS, LANES))
                    bcol = jnp.floor(bgcol)
                    gcol = bgcol - bcol
                    qbt = jnp.where(bcol == b_of_lane, gcol, 0.0).astype(BF16)
                    tile = jnp.dot(pa, qbt, preferred_element_type=F32)
                row0 = pl.multiple_of((tok0 + k) * G_PITCH, SUBLANES)
                gmat[pl.ds(row0, half_keys), :] = pltpu.pack_elementwise(
                    [tile[0:half_keys], tile[half_keys:N_KEYS]], packed_dtype=BF16)
            return carry

        lax.fori_loop(0, t // per_body, body, 0)

    unit = jnp.minimum(e, n_half * PEER_HEADS - 1)
    half = lax.shift_right_logical(unit, PEER_HEADS.bit_length() - 1)
    h = unit & (PEER_HEADS - 1)
    rd = e & 1
    blk = jnp.minimum(e, n_steps - 2)
    eb = u_ref.shape[0]
    n_stages = 4
    exp_w = eb // n_stages
    out_w = v_ref.shape[1] // n_stages
    words_per_stage = exp_w // (2 * N_KEYS)

    def value_piece(c):
        ocols = slice(c * out_w, (c + 1) * out_w)
        o_ref[:, ocols] += jnp.dot(coef_scr[rd], v_ref[:, ocols], preferred_element_type=F32)

    def act_piece(c):
        ecols = slice(c * exp_w, (c + 1) * exp_w)
        act = lax.dot_general(xn_scr[...], u_ref[ecols, :], NT_DIMS, preferred_element_type=F32)
        gates = []
        for w in range(words_per_stage):
            word = gmat[pl.ds((blk * n_stages + c) * words_per_stage + w, t, stride=G_PITCH), :]
            gates.append(lax.bitcast_convert_type(word << 16, F32))
            gates.append(lax.bitcast_convert_type(word & jnp.int32(-65536), F32))
        coef_scr[1 - rd, :, ecols] = (_twice_gelu(act) * jnp.concatenate(gates, axis=1)).astype(BF16)

    def retrieval():
        lists = []
        for p in range(2):
            st = lax.dot_general(sk_ref[h, p], q_scr[half, 2 * h + p], NT_DIMS, preferred_element_type=F32)
            yield from _top16_of_keys(st, lists)
            yield
        cand, expert = _pair_candidates(lists[0], lists[1])
        second_level = _TopK(cand, payload=expert)
        for _ in range(4):
            second_level.step(PEER_TOPK // 4)
            yield
        best, e_sel = second_level.result()
        ex = jnp.exp(best - best[0:1])
        slot0 = pl.multiple_of(h * PEER_TOPK, PEER_TOPK)
        sel_e[cur, half, pl.ds(slot0, PEER_TOPK), :] = e_sel
        sel_g[cur, half, pl.ds(slot0, PEER_TOPK), :] = ex / jnp.sum(ex, axis=0, keepdims=True)

    pieces = [functools.partial(f, c) for c in range(n_stages) for f in (value_piece, act_piece)]
    parts = retrieval()
    parts_per_piece = 2
    for piece in pieces:
        piece()
        for _ in range(parts_per_piece):
            next(parts, None)
    for _ in parts:
        pass

    @pl.when(e == n_steps - 1)
    def _():
        project_queries()
        if out_refs:
            first_ref, second_ref = out_refs

            @pl.when(i - 1 < split_blk)
            def _():
                first_ref[...] = o_ref[...]

            @pl.when(i - 1 >= split_blk)
            def _():
                second_ref[...] = o_ref[...]


def _peer(x, g, wq, sk, u, v, t=512, eb=1024, split_rows=None):
    n, d = x.shape
    nslot = PEER_HEADS * PEER_TOPK
    n_tok_blk = n // t
    n_exp_blk = N_KEYS * N_KEYS // eb
    n_half = t // SEL_T
    assert n_exp_blk == n_half * PEER_HEADS
    last_tok = n_tok_blk - 1
    once = pl.Buffered(1)
    if split_rows is None:
        split_blk = None
        out_specs = pl.BlockSpec((t, d), lambda i, e: (jnp.maximum(i - 1, 0), 0))
        out_shape = jax.ShapeDtypeStruct((n, d), F32)
        acc = []
    else:
        split_blk = split_rows // t
        out_specs = [pl.BlockSpec((t, d), lambda i, e: (jnp.clip(i - 1, 0, split_blk - 1), 0)),
                     pl.BlockSpec((t, d), lambda i, e: (jnp.clip(i - 1 - split_blk, 0, last_tok - split_blk), 0))]
        out_shape = [jax.ShapeDtypeStruct((split_rows, d), F32), jax.ShapeDtypeStruct((n - split_rows, d), F32)]
        acc = [pltpu.VMEM((t, d), F32)]
    return pl.pallas_call(
        functools.partial(_peer_fused_kernel, split_blk=split_blk),
        grid=(n_tok_blk + 1, n_exp_blk + 1),
        in_specs=[pl.BlockSpec((t, d), lambda i, e: (jnp.minimum(i + e // n_exp_blk, last_tok), 0),
                               pipeline_mode=once),
                  pl.BlockSpec((t, d), lambda i, e: (jnp.maximum(i - 1, 0), 0)),
                  pl.BlockSpec((1, d), lambda i, e: (0, 0)),
                  pl.BlockSpec((d, 2 * PEER_HEADS * LANES), lambda i, e: (0, 0), pipeline_mode=once),
                  pl.BlockSpec((PEER_HEADS, 2, N_KEYS, LANES), lambda i, e: (0, 0, 0, 0), pipeline_mode=once),
                  pl.BlockSpec((eb, d), lambda i, e: (jnp.minimum(e, n_exp_blk - 1), 0)),
                  pl.BlockSpec((eb, d), lambda i, e: (jnp.maximum(e - 1, 0), 0))],
        out_specs=out_specs,
        out_shape=out_shape,
        scratch_shapes=[pltpu.VMEM((n_half, 2 * PEER_HEADS, SEL_T, LANES), BF16),
                        pltpu.VMEM((2, n_half, nslot, SEL_T), F32),
                        pltpu.VMEM((2, n_half, nslot, SEL_T), F32),
                        pltpu.VMEM((t, d), BF16),
                        pltpu.VMEM((t, nslot), F32),
                        pltpu.VMEM((t, nslot), F32),
                        pltpu.VMEM((t // LANES, nslot, LANES), F32),
                        pltpu.VMEM((t * G_PITCH, LANES), jnp.int32),
                        pltpu.VMEM((2, t, eb), BF16)] + acc,
        compiler_params=_cparams(2),
        name="peer",
    )(x, x, g.reshape(1, d), wq, sk, u, v)


def _rope_tables(pos):
    half = HEAD_DIM // 2
    inv_freq = ROPE_THETA ** (-jnp.arange(half, dtype=F32) / half)
    ang = pos.astype(F32)[:, None] * inv_freq[None, :]
    cos, sin = jnp.cos(ang), jnp.sin(ang)
    reps = LANES // HEAD_DIM
    cos_t = jnp.tile(jnp.concatenate([cos, cos], axis=1), (1, reps))
    sin_t = jnp.tile(jnp.concatenate([-sin, sin], axis=1), (1, reps))
    return cos_t, sin_t


def kernel(x_prompt, x_sample, cache_conv, cache_win_k, cache_win_v, state_mlstm_C, state_mlstm_n,
           state_mlstm_m, norm_mix, norm_ffn, ab_w_in, ab_conv_w, ab_q_gain, ab_k_gain, ab_sinks, ab_w_out,
           ml_w_in, ml_gate_bias, ml_out_gain, ml_w_out, peer_w_q, peer_sub_keys, peer_u, peer_v):
    n_batch, seq, d = x_prompt.shape
    n_seq, t_len, _ = x_sample.shape
    n_prompt = n_batch * seq
    assert d == D_MODEL and t_len == SUBLANES and norm_mix.shape[0] == 2

    xp = x_prompt.reshape(n_prompt, d)
    xs = x_sample.reshape(n_seq * t_len, d)

    cos_p, sin_p = _rope_tables(jnp.arange(seq, dtype=jnp.int32))
    cos_s, sin_s = _rope_tables(PAST_LEN + jnp.arange(t_len, dtype=jnp.int32))
    bt = 16
    cos_s, sin_s = jnp.tile(cos_s, (bt, 1)), jnp.tile(sin_s, (bt, 1))
    lane = jnp.arange(LANES)
    seg = (lane[:, None] // HEAD_DIM == lane[None, :] // HEAD_DIM).astype(BF16)
    reps = LANES // HEAD_DIM
    qg = jnp.tile(ab_q_gain[0], reps).reshape(1, LANES)
    kg = jnp.tile(ab_k_gain[0], reps).reshape(1, LANES)
    wo_ab = ab_w_out[0].astype(BF16)

    z = _norm_proj([xp, xs], norm_mix[0], ab_w_in[0].astype(BF16))
    y_p, k_p, v_p, c_p = _ab_prompt(z, xp, n_batch, seq, cos_p, sin_p, ab_conv_w[0], qg, kg, ab_sinks[0],
                                    seg, wo_ab)
    y_s, c_s, k_s, v_s = _ab_sample(z, xs, n_prompt, n_seq, t_len, cos_s, sin_s, ab_conv_w[0], qg, kg,
                                    ab_sinks[0], seg, wo_ab, cache_conv[0],
                                    cache_win_k[0].reshape(n_seq, WINDOW, LANES),
                                    cache_win_v[0].reshape(n_seq, WINDOW, LANES), bt=bt)
    x = jnp.concatenate([y_p.reshape(n_prompt, d), y_s], axis=0)
    x = _peer(x, norm_ffn[0], peer_w_q[0].astype(BF16), peer_sub_keys[0].astype(BF16),
              peer_u[0].astype(BF16), peer_v[0].astype(BF16))

    n_gate = 2 * ML_HEADS
    w_in = jnp.pad(ml_w_in[0], ((0, 0), (0, ML_IN_PAD - ml_w_in.shape[2]))).astype(BF16)
    bias = jnp.pad(ml_gate_bias[0], (0, LANES - n_gate)).reshape(1, LANES)
    og = ml_out_gain[0].reshape(1, D_MODEL)
    idx = jnp.arange(ML_CHUNK)
    tril = (idx[None, :] <= idx[:, None]).astype(BF16)
    wo_ml = ml_w_out[0].astype(BF16)

    z = _norm_proj([x], norm_mix[1], w_in)
    y_p, cm_p, nm_p, mm_p = _mlstm_prompt(z, x, n_batch, seq, bias, og, tril, wo_ml)
    y_s, cm_s, nm_s, mm_s = _mlstm_sample(z, x, n_prompt, n_seq, t_len, bias, og, tril, wo_ml,
                                          state_mlstm_C[0], state_mlstm_n[0],
                                          state_mlstm_m[0].reshape(n_seq, 1, ML_HEADS))
    x = jnp.concatenate([y_p.reshape(n_prompt, d), y_s], axis=0)
    out_p, out_s = _peer(x, norm_ffn[1], peer_w_q[1].astype(BF16), peer_sub_keys[1].astype(BF16),
                         peer_u[1].astype(BF16), peer_v[1].astype(BF16), split_rows=n_prompt)

    y_prompt = out_p.reshape(n_batch, seq, d)
    y_sample = out_s.reshape(n_seq, t_len, d)
    kv_shape_p = (1, n_batch, WINDOW, N_KV_HEADS, HEAD_DIM)
    kv_shape_s = (1, n_seq, WINDOW, N_KV_HEADS, HEAD_DIM)
    return (y_prompt, y_sample,
            c_p[:, SUBLANES - 2:, :][None], k_p.reshape(kv_shape_p), v_p.reshape(kv_shape_p),
            cm_p[None], nm_p[None], mm_p.reshape(1, n_batch, ML_HEADS),
            c_s[None], k_s.reshape(kv_shape_s), v_s.reshape(kv_shape_s),
            cm_s[None], nm_s[None], mm_s.reshape(1, n_seq, ML_HEADS))
```

```python
import functools

import jax
import jax.numpy as jnp
from jax import lax
from jax.experimental import pallas as pl
from jax.experimental.pallas import tpu as pltpu

F32 = jnp.float32
BF16 = jnp.bfloat16
EPS = 1e-6

D_MODEL = 1024
CONV_DIM = 512
N_Q_HEADS = 8
N_KV_HEADS = 2
HEAD_DIM = 64
WINDOW = 128
ROPE_THETA = 10000.0
AB_IN = 2304
ML_HEADS = 4
ML_QK = 128
ML_V = 256
ML_CHUNK = 128
ML_GATE_COL = 3072
ML_IN_PAD = ML_GATE_COL + 128
N_KEYS = 128
PEER_HEADS = 8
PEER_TOPK = 16
PAST_LEN = 16384

LANES = 128
SUBLANES = 8
G_PITCH = N_KEYS // 2 + SUBLANES
VMEM_LIMIT = 56 * 1024 * 1024

NT_DIMS = (((1,), (1,)), ((), ()))
TN_DIMS = (((0,), (0,)), ((), ()))


def _cparams(n_axes, vmem=VMEM_LIMIT):
    return pltpu.CompilerParams(dimension_semantics=("arbitrary",) * n_axes, vmem_limit_bytes=vmem)


def _rmsnorm(x, g):
    return x * lax.rsqrt(jnp.mean(x * x, axis=-1, keepdims=True) + EPS) * g


def _bf16_pieces(a, terms):
    pieces = []
    rem = a
    for _ in range(terms):
        piece = rem.astype(BF16)
        rem = rem - piece.astype(F32)
        pieces.append(piece)
    return pieces


def _split_dot(a, b_bf16, terms=2):
    return sum(jnp.dot(p, b_bf16, preferred_element_type=F32) for p in _bf16_pieces(a, terms))


def _norm_proj_kernel(*refs, starts):
    x_refs = refs[:len(starts)]
    g_ref, w_ref, o_ref = refs[len(starts):]
    i = pl.program_id(0)
    x = x_refs[0][...]
    for x_ref, start in zip(x_refs[1:], starts[1:]):
        x = jnp.where(i >= start, x_ref[...], x)
    r = _rmsnorm(x, g_ref[...])
    o_ref[...] = jnp.dot(r.astype(BF16), w_ref[...], preferred_element_type=F32)


def _norm_proj(x_parts, g, w_bf16, tm=512):
    d = x_parts[0].shape[1]
    nout = w_bf16.shape[1]
    blocks = [p.shape[0] // tm for p in x_parts]
    starts = tuple(sum(blocks[:k]) for k in range(len(blocks)))
    part_spec = lambda start, nblk: pl.BlockSpec((tm, d), lambda i: (jnp.clip(i - start, 0, nblk - 1), 0))
    return pl.pallas_call(
        functools.partial(_norm_proj_kernel, starts=starts),
        grid=(sum(blocks),),
        in_specs=[part_spec(s, nb) for s, nb in zip(starts, blocks)] + [
            pl.BlockSpec((1, d), lambda i: (0, 0)),
            pl.BlockSpec((d, nout), lambda i: (0, 0))],
        out_specs=pl.BlockSpec((tm, nout), lambda i: (i, 0)),
        out_shape=jax.ShapeDtypeStruct((sum(blocks) * tm, nout), F32),
        compiler_params=_cparams(1),
        name="norm_proj",
    )(*x_parts, g.reshape(1, d), w_bf16)


def _headnorm_rope(xc, gain, cos, sin, seg, hi_half):
    ss = _split_dot(xc * xc, seg)
    xn = xc * lax.rsqrt(ss * (1.0 / HEAD_DIM) + EPS) * gain
    partner = jnp.where(hi_half, pltpu.roll(xn, 32, 1), pltpu.roll(xn, 96, 1))
    return xn * cos + partner * sin


def _softmax_sink(s, mask, sink):
    s = jnp.where(mask, s, -1e30)
    m = jnp.maximum(jnp.max(s, axis=-1, keepdims=True), sink)
    p = jnp.exp(s - m)
    denom = jnp.sum(p, axis=-1, keepdims=True) + jnp.exp(sink - m)
    return (p / denom).astype(BF16)


def _ab_prompt_kernel(sink_ref, *refs, n_batch):
    z_refs, x_refs = refs[0:n_batch], refs[n_batch:2 * n_batch]
    (cos_ref, sin_ref, cw_ref, qg_ref, kg_ref, seg_ref, wo_ref,
     y_ref, kst_ref, vst_ref, cst_ref, pk_ref, pv_ref, pu_ref) = refs[2 * n_batch:]
    j = pl.program_id(0)

    @pl.when(j == 0)
    def _():
        pk_ref[...] = jnp.zeros_like(pk_ref)
        pv_ref[...] = jnp.zeros_like(pv_ref)
        pu_ref[...] = jnp.zeros_like(pu_ref)

    for b in range(n_batch):
        _ab_prompt_block(j, sink_ref, z_refs[b], x_refs[b], cos_ref, sin_ref, cw_ref, qg_ref, kg_ref, seg_ref,
                         wo_ref, y_ref.at[b], kst_ref.at[b], vst_ref.at[b], cst_ref.at[b],
                         pk_ref.at[b], pv_ref.at[b], pu_ref.at[b])


def _ab_prompt_block(j, sink_ref, z_ref, x_ref, cos_ref, sin_ref, cw_ref, qg_ref, kg_ref, seg_ref, wo_ref,
                     y_ref, kst_ref, vst_ref, cst_ref, pk_ref, pv_ref, pu_ref):
    blk = z_ref.shape[0]
    cos = cos_ref[...]
    sin = sin_ref[...]
    seg = seg_ref[...]
    hi_half = (lax.broadcasted_iota(jnp.int32, (blk, LANES), 1) & 32) != 0

    gate_b = z_ref[:, 0:CONV_DIM]
    u = z_ref[:, CONV_DIM:2 * CONV_DIM] * z_ref[:, 2 * CONV_DIM:3 * CONV_DIM]
    ng = blk // SUBLANES
    u3 = u.reshape(ng, SUBLANES, CONV_DIM)
    ext = jnp.concatenate([pu_ref[...][None], u3], axis=0)
    t8 = lax.broadcasted_iota(jnp.int32, (ng, SUBLANES, CONV_DIM), 1)
    r1 = pltpu.roll(ext, 1, 1)
    r2 = pltpu.roll(ext, 2, 1)
    um1 = jnp.where(t8 >= 1, r1[1:], r1[:-1])
    um2 = jnp.where(t8 >= 2, r2[1:], r2[:-1])
    cw = cw_ref[...]
    conv = cw[0:1][None] * um2 + cw[1:2][None] * um1 + cw[2:3][None] * u3
    yconv = gate_b * conv.reshape(blk, CONV_DIM)

    q0 = 3 * CONV_DIM
    k0 = q0 + N_Q_HEADS * HEAD_DIM
    v0 = k0 + N_KV_HEADS * HEAD_DIM
    qg = qg_ref[...]
    qr = [_headnorm_rope(z_ref[:, q0 + c * LANES:q0 + (c + 1) * LANES], qg, cos, sin, seg, hi_half)
          for c in range(N_Q_HEADS * HEAD_DIM // LANES)]
    kr = _headnorm_rope(z_ref[:, k0:k0 + LANES], kg_ref[...], cos, sin, seg, hi_half)
    v = z_ref[:, v0:v0 + LANES]
    pk = pk_ref[...]
    pv = pv_ref[...]

    row = lax.broadcasted_iota(jnp.int32, (blk, 2 * blk), 0)
    col = lax.broadcasted_iota(jnp.int32, (blk, 2 * blk), 1)
    row_prev = row + jnp.where(j == 0, 2 * blk, 0)
    mask = ((col < blk) & (col > row_prev)) | ((col >= blk) & (col - blk <= row))

    kks, vvs = [], []
    for g in range(N_KV_HEADS):
        sl = slice(g * HEAD_DIM, (g + 1) * HEAD_DIM)
        kks.append(jnp.concatenate([pk[:, sl], kr[:, sl]], axis=0).astype(BF16))
        vvs.append(jnp.concatenate([pv[:, sl], v[:, sl]], axis=0).astype(BF16))
    outs = []
    for h in range(N_Q_HEADS):
        g = h // (N_Q_HEADS // N_KV_HEADS)
        qh = qr[h // 2][:, (h % 2) * HEAD_DIM:(h % 2 + 1) * HEAD_DIM].astype(BF16)
        s = lax.dot_general(qh, kks[g], NT_DIMS, preferred_element_type=F32) * (HEAD_DIM ** -0.5)
        p = _softmax_sink(s, mask, sink_ref[h])
        outs.append(jnp.dot(p, vvs[g], preferred_element_type=F32))
    attn = jnp.concatenate(outs, axis=1)

    y = (jnp.dot(yconv.astype(BF16), wo_ref[0:CONV_DIM, :], preferred_element_type=F32)
         + jnp.dot(attn.astype(BF16), wo_ref[CONV_DIM:2 * CONV_DIM, :], preferred_element_type=F32))
    y_ref[...] = y + x_ref[...]

    pk_ref[...] = kr
    pv_ref[...] = v
    pu_ref[...] = u3[ng - 1]
    kst_ref[...] = kr
    vst_ref[...] = v
    cst_ref[...] = u3[ng - 1]


def _ab_prompt(z, x, n_batch, seq, cos, sin, cw, qg, kg, sinks, seg, wo):
    blk = WINDOW
    nb = seq // blk
    const = lambda j, s: (0, 0)
    whole = lambda j, s: (0, 0, 0)
    tok_specs = lambda width: [pl.BlockSpec((blk, width), functools.partial(lambda j, s, b: (b * nb + j, 0), b=b))
                               for b in range(n_batch)]
    grid_spec = pltpu.PrefetchScalarGridSpec(
        num_scalar_prefetch=1,
        grid=(nb,),
        in_specs=tok_specs(AB_IN) + tok_specs(D_MODEL) + [
            pl.BlockSpec((blk, LANES), lambda j, s: (j, 0)),
            pl.BlockSpec((blk, LANES), lambda j, s: (j, 0)),
            pl.BlockSpec((3, CONV_DIM), const),
            pl.BlockSpec((1, LANES), const),
            pl.BlockSpec((1, LANES), const),
            pl.BlockSpec((LANES, LANES), const),
            pl.BlockSpec((D_MODEL, D_MODEL), const)],
        out_specs=[pl.BlockSpec((n_batch, blk, D_MODEL), lambda j, s: (0, j, 0)),
                   pl.BlockSpec((n_batch, blk, LANES), whole),
                   pl.BlockSpec((n_batch, blk, LANES), whole),
                   pl.BlockSpec((n_batch, SUBLANES, CONV_DIM), whole)],
        scratch_shapes=[pltpu.VMEM((n_batch, blk, LANES), F32), pltpu.VMEM((n_batch, blk, LANES), F32),
                        pltpu.VMEM((n_batch, SUBLANES, CONV_DIM), F32)])
    return pl.pallas_call(
        functools.partial(_ab_prompt_kernel, n_batch=n_batch),
        grid_spec=grid_spec,
        out_shape=[jax.ShapeDtypeStruct((n_batch, seq, D_MODEL), F32),
                   jax.ShapeDtypeStruct((n_batch, blk, LANES), F32),
                   jax.ShapeDtypeStruct((n_batch, blk, LANES), F32),
                   jax.ShapeDtypeStruct((n_batch, SUBLANES, CONV_DIM), F32)],
        compiler_params=_cparams(1),
        name="ab_prompt",
    )(sinks, *([z] * n_batch), *([x] * n_batch), cos, sin, cw, qg, kg, seg, wo)


def _ab_sample_kernel(sink_ref, z_ref, x_ref, cos_ref, sin_ref, cw_ref, qg_ref, kg_ref, seg_ref, wo_ref,
                      cc_ref, ck_ref, cv_ref, y_ref, cs_ref, ks_ref, vs_ref):
    rows = z_ref.shape[0]
    t_len = SUBLANES
    bt = rows // t_len
    cos = cos_ref[...]
    sin = sin_ref[...]
    seg = seg_ref[...]
    hi_half = (lax.broadcasted_iota(jnp.int32, (rows, LANES), 1) & 32) != 0

    gate_b = z_ref[:, 0:CONV_DIM]
    u = z_ref[:, CONV_DIM:2 * CONV_DIM] * z_ref[:, 2 * CONV_DIM:3 * CONV_DIM]
    u3 = u.reshape(bt, t_len, CONV_DIM)
    cc = cc_ref[...]
    c0 = cc[:, 0:1, :]
    c1 = cc[:, 1:2, :]
    t8 = lax.broadcasted_iota(jnp.int32, (bt, t_len, CONV_DIM), 1)
    r1 = pltpu.roll(u3, 1, 1)
    r2 = pltpu.roll(u3, 2, 1)
    um1 = jnp.where(t8 >= 1, r1, c1)
    um2 = jnp.where(t8 >= 2, r2, jnp.where(t8 == 1, c1, c0))
    cw = cw_ref[...]
    conv = cw[0:1][None] * um2 + cw[1:2][None] * um1 + cw[2:3][None] * u3
    yconv = gate_b * conv.reshape(rows, CONV_DIM)
    cs_ref[...] = r2[:, 0:2, :]

    q0 = 3 * CONV_DIM
    k0 = q0 + N_Q_HEADS * HEAD_DIM
    v0 = k0 + N_KV_HEADS * HEAD_DIM
    qg = qg_ref[...]
    qr = [_headnorm_rope(z_ref[:, q0 + c * LANES:q0 + (c + 1) * LANES], qg, cos, sin, seg, hi_half)
          for c in range(N_Q_HEADS * HEAD_DIM // LANES)]
    kr = _headnorm_rope(z_ref[:, k0:k0 + LANES], kg_ref[...], cos, sin, seg, hi_half)
    v = z_ref[:, v0:v0 + LANES]

    group = N_Q_HEADS // N_KV_HEADS
    nq = group * t_len
    nk = 2 * WINDOW
    qrow = lax.broadcasted_iota(jnp.int32, (nq, nk), 0)
    t_q = qrow & (t_len - 1)
    col = lax.broadcasted_iota(jnp.int32, (nq, nk), 1)
    mask = (((col < WINDOW) & (col > t_q)) | ((col >= WINDOW) & (col - WINDOW <= t_q)))[None]
    hrow = lax.broadcasted_iota(jnp.int32, (nq, 1), 0) // t_len
    pad = jnp.zeros((bt, nk - WINDOW - t_len, HEAD_DIM), F32)

    outs = [None] * N_Q_HEADS
    for g in range(N_KV_HEADS):
        sl = slice(g * HEAD_DIM, (g + 1) * HEAD_DIM)
        qs = jnp.concatenate(
            [qr[h // 2][:, (h % 2) * HEAD_DIM:(h % 2 + 1) * HEAD_DIM].reshape(bt, t_len, HEAD_DIM)
             for h in range(g * group, (g + 1) * group)], axis=1)
        kk = jnp.concatenate([ck_ref[:, :, sl], kr[:, sl].reshape(bt, t_len, HEAD_DIM), pad], axis=1)
        vv = jnp.concatenate([cv_ref[:, :, sl], v[:, sl].reshape(bt, t_len, HEAD_DIM), pad], axis=1)
        s = jnp.einsum('bqd,bkd->bqk', qs.astype(BF16), kk.astype(BF16),
                       preferred_element_type=F32) * (HEAD_DIM ** -0.5)
        sink = jnp.zeros((nq, 1), F32)
        for hh in range(group):
            sink = jnp.where(hrow == hh, sink_ref[g * group + hh], sink)
        p = _softmax_sink(s, mask, sink[None])
        o = jnp.einsum('bqk,bkd->bqd', p, vv.astype(BF16), preferred_element_type=F32)
        for hh in range(group):
            outs[g * group + hh] = o[:, hh * t_len:(hh + 1) * t_len, :].reshape(rows, HEAD_DIM)
    attn = jnp.concatenate(outs, axis=1)

    y = (jnp.dot(yconv.astype(BF16), wo_ref[0:CONV_DIM, :], preferred_element_type=F32)
         + jnp.dot(attn.astype(BF16), wo_ref[CONV_DIM:2 * CONV_DIM, :], preferred_element_type=F32))
    y_ref[...] = y + x_ref[...]

    keep = WINDOW - t_len
    ks_ref[:, 0:keep, :] = ck_ref[:, t_len:WINDOW, :]
    ks_ref[:, keep:WINDOW, :] = kr.reshape(bt, t_len, LANES)
    vs_ref[:, 0:keep, :] = cv_ref[:, t_len:WINDOW, :]
    vs_ref[:, keep:WINDOW, :] = v.reshape(bt, t_len, LANES)


def _ab_sample(z, x, row0, n_seq, t_len, cos, sin, cw, qg, kg, sinks, seg, wo, cc, ck, cv, bt=16):
    rows = bt * t_len
    blk0 = row0 // rows
    tok = lambda i, s: (blk0 + i, 0)
    const = lambda i, s: (0, 0)
    seq3 = lambda i, s: (i, 0, 0)
    grid_spec = pltpu.PrefetchScalarGridSpec(
        num_scalar_prefetch=1,
        grid=(n_seq // bt,),
        in_specs=[pl.BlockSpec((rows, AB_IN), tok),
                  pl.BlockSpec((rows, D_MODEL), lambda i, s: (i, 0)),
                  pl.BlockSpec((rows, LANES), const),
                  pl.BlockSpec((rows, LANES), const),
                  pl.BlockSpec((3, CONV_DIM), const),
                  pl.BlockSpec((1, LANES), const),
                  pl.BlockSpec((1, LANES), const),
                  pl.BlockSpec((LANES, LANES), const),
                  pl.BlockSpec((D_MODEL, D_MODEL), const),
                  pl.BlockSpec((bt, 2, CONV_DIM), seq3),
                  pl.BlockSpec((bt, WINDOW, LANES), seq3),
                  pl.BlockSpec((bt, WINDOW, LANES), seq3)],
        out_specs=[pl.BlockSpec((rows, D_MODEL), lambda i, s: (i, 0)),
                   pl.BlockSpec((bt, 2, CONV_DIM), seq3),
                   pl.BlockSpec((bt, WINDOW, LANES), seq3),
                   pl.BlockSpec((bt, WINDOW, LANES), seq3)])
    return pl.pallas_call(
        _ab_sample_kernel,
        grid_spec=grid_spec,
        out_shape=[jax.ShapeDtypeStruct((n_seq * t_len, D_MODEL), F32),
                   jax.ShapeDtypeStruct((n_seq, 2, CONV_DIM), F32),
                   jax.ShapeDtypeStruct((n_seq, WINDOW, LANES), F32),
                   jax.ShapeDtypeStruct((n_seq, WINDOW, LANES), F32)],
        compiler_params=_cparams(1),
        name="ab_sample",
    )(sinks, z, x, cos, sin, cw, qg, kg, seg, wo, cc, ck, cv)


def _log_sigmoid(x):
    return jnp.minimum(x, 0.0) - jnp.log(1.0 + jnp.exp(-jnp.abs(x)))


def _mlstm_chunk(z, bias, og, tril, c_src, n_src, m_src, c_dst, n_dst, m_dst, n_real, rows):
    L = z.shape[0]
    gates = z[:, ML_GATE_COL:ML_GATE_COL + LANES] + bias
    if n_real < L:
        live = lax.broadcasted_iota(jnp.int32, (L, LANES), 0) < n_real
        li_all = jnp.where(live, gates, -1e30)
        lf_all = jnp.where(live, _log_sigmoid(gates), 0.0)
    else:
        li_all = gates
        lf_all = _log_sigmoid(gates)
    lf_pieces = _bf16_pieces(lf_all, 3)
    f_col_all = sum(jnp.dot(tril[0:rows], p, preferred_element_type=F32) for p in lf_pieces)
    f_row_all = sum(lax.dot_general(p, tril, (((0,), (1,)), ((), ())), preferred_element_type=F32)
                    for p in lf_pieces)
    li_t = li_all.T
    rr = lax.broadcasted_iota(jnp.int32, (rows, L), 0)
    cc = lax.broadcasted_iota(jnp.int32, (rows, L), 1)
    causal = cc <= rr

    outs, m_new_all = [], []
    for h in range(ML_HEADS):
        f_col = f_col_all[:, ML_HEADS + h:ML_HEADS + h + 1]
        f_row = f_row_all[ML_HEADS + h:ML_HEADS + h + 1, :]
        li_row = li_t[h:h + 1, :]
        li_col = li_all[0:rows, h:h + 1]
        m0 = m_src[0:1, h:h + 1]
        c0 = c_src[h]
        n0 = n_src[h:h + 1, :]
        qh = z[0:rows, h * ML_QK:(h + 1) * ML_QK]
        kh = z[:, ML_HEADS * ML_QK + h * ML_QK:ML_HEADS * ML_QK + (h + 1) * ML_QK] * (ML_QK ** -0.5)
        v_off = 2 * ML_HEADS * ML_QK
        vh = z[:, v_off + h * ML_V:v_off + (h + 1) * ML_V]
        o_off = v_off + ML_HEADS * ML_V
        oh = z[0:rows, o_off + h * ML_V:o_off + (h + 1) * ML_V]
        qb = qh.astype(BF16)
        vb = vh.astype(BF16)

        dmat = jnp.where(causal, f_col - f_row + li_row, -jnp.inf)
        gcar = f_col + m0
        m_t = jnp.maximum(jnp.max(dmat, axis=-1, keepdims=True), gcar)
        w = jnp.exp(dmat - m_t)
        s = lax.dot_general(qb, kh.astype(BF16), NT_DIMS, preferred_element_type=F32) * w
        carry = jnp.exp(gcar - m_t)
        num = (jnp.dot(s.astype(BF16), vb, preferred_element_type=F32)
               + jnp.dot(qb, c0.astype(BF16), preferred_element_type=F32) * carry)
        den = jnp.sum(s, axis=-1, keepdims=True) + carry * jnp.sum(qh * n0, axis=-1, keepdims=True)
        hout = num / jnp.maximum(jnp.abs(den), jnp.exp(-m_t))

        f_last = f_col[n_real - 1:n_real, :]
        w_end = f_last - f_col + li_col
        m_new = jnp.maximum(f_last + m0, jnp.max(w_end, axis=0, keepdims=True))
        a_end = jnp.exp(w_end - m_new)
        scale = jnp.exp(f_last + m0 - m_new)
        ka = kh[0:rows] * a_end
        c_dst[h] = scale * c0 + lax.dot_general(ka.astype(BF16), vb[0:rows], TN_DIMS,
                                                preferred_element_type=F32)
        n_dst[h:h + 1, :] = scale * n0 + jnp.sum(ka, axis=0, keepdims=True)
        m_new_all.append(m_new)

        hn = _rmsnorm(hout, og[:, h * ML_V:(h + 1) * ML_V])
        outs.append(jax.nn.sigmoid(oh) * hn)
    m_dst[...] = jnp.concatenate(m_new_all, axis=1)
    return jnp.concatenate(outs, axis=1)


def _mlstm_prompt_kernel(*refs, n_batch):
    z_refs, x_refs = refs[0:n_batch], refs[n_batch:2 * n_batch]
    bias_ref, og_ref, tril_ref, wo_ref, y_ref, c_ref, n_ref, m_ref = refs[2 * n_batch:]

    @pl.when(pl.program_id(0) == 0)
    def _():
        c_ref[...] = jnp.zeros_like(c_ref)
        n_ref[...] = jnp.zeros_like(n_ref)
        m_ref[...] = jnp.zeros_like(m_ref)

    for b in range(n_batch):
        state = (c_ref.at[b], n_ref.at[b], m_ref.at[b])
        out = _mlstm_chunk(z_refs[b][...], bias_ref[...], og_ref[...], tril_ref[...], *state, *state,
                           ML_CHUNK, ML_CHUNK)
        y_ref[b] = jnp.dot(out.astype(BF16), wo_ref[...], preferred_element_type=F32) + x_refs[b][...]


def _mlstm_sample_kernel(z_ref, x_ref, bias_ref, og_ref, tril_ref, wo_ref, c0_ref, n0_ref, m0_ref,
                         y_ref, c_ref, n_ref, m_ref, *, t_len):
    n_here = z_ref.shape[0] // t_len
    q_rows = 2 * SUBLANES
    outs = []
    for s in range(n_here):
        zpad = jnp.concatenate([z_ref[s * t_len:(s + 1) * t_len, :],
                                jnp.zeros((ML_CHUNK - t_len, ML_IN_PAD), F32)], axis=0)
        out = _mlstm_chunk(zpad, bias_ref[...], og_ref[...], tril_ref[...],
                           c0_ref.at[s], n0_ref.at[s], m0_ref.at[s], c_ref.at[s], n_ref.at[s], m_ref.at[s],
                           t_len, q_rows)
        outs.append(out[0:t_len])
    out_all = jnp.concatenate(outs, axis=0).astype(BF16)
    y_ref[...] = jnp.dot(out_all, wo_ref[...], preferred_element_type=F32) + x_ref[...]


def _mlstm_weight_specs(const):
    return [pl.BlockSpec((1, LANES), const),
            pl.BlockSpec((1, D_MODEL), const),
            pl.BlockSpec((ML_CHUNK, ML_CHUNK), const),
            pl.BlockSpec((D_MODEL, D_MODEL), const)]


def _mlstm_prompt(z, x, n_batch, seq, bias, og, tril, wo):
    nc = seq // ML_CHUNK
    const = lambda j: (0, 0)
    tok_specs = lambda width: [pl.BlockSpec((ML_CHUNK, width), functools.partial(lambda j, b: (b * nc + j, 0), b=b))
                               for b in range(n_batch)]
    return pl.pallas_call(
        functools.partial(_mlstm_prompt_kernel, n_batch=n_batch),
        grid=(nc,),
        in_specs=tok_specs(ML_IN_PAD) + tok_specs(D_MODEL) + _mlstm_weight_specs(const),
        out_specs=[pl.BlockSpec((n_batch, ML_CHUNK, D_MODEL), lambda j: (0, j, 0)),
                   pl.BlockSpec((n_batch, ML_HEADS, ML_QK, ML_V), lambda j: (0, 0, 0, 0)),
                   pl.BlockSpec((n_batch, ML_HEADS, ML_QK), lambda j: (0, 0, 0)),
                   pl.BlockSpec((n_batch, 1, ML_HEADS), lambda j: (0, 0, 0))],
        out_shape=[jax.ShapeDtypeStruct((n_batch, seq, D_MODEL), F32),
                   jax.ShapeDtypeStruct((n_batch, ML_HEADS, ML_QK, ML_V), F32),
                   jax.ShapeDtypeStruct((n_batch, ML_HEADS, ML_QK), F32),
                   jax.ShapeDtypeStruct((n_batch, 1, ML_HEADS), F32)],
        compiler_params=_cparams(1),
        name="mlstm_prompt",
    )(*([z] * n_batch), *([x] * n_batch), bias, og, tril, wo)


def _mlstm_sample(z, x, row0, n_seq, t_len, bias, og, tril, wo, c0, n0, m0, seqs_per_step=8):
    rows = seqs_per_step * t_len
    blk0 = row0 // rows
    tok = lambda i: (blk0 + i, 0)
    const = lambda i: (0, 0)
    st4 = lambda i: (i, 0, 0, 0)
    st3 = lambda i: (i, 0, 0)
    state_specs = [pl.BlockSpec((seqs_per_step, ML_HEADS, ML_QK, ML_V), st4),
                   pl.BlockSpec((seqs_per_step, ML_HEADS, ML_QK), st3),
                   pl.BlockSpec((seqs_per_step, 1, ML_HEADS), st3)]
    return pl.pallas_call(
        functools.partial(_mlstm_sample_kernel, t_len=t_len),
        grid=(n_seq // seqs_per_step,),
        in_specs=[pl.BlockSpec((rows, ML_IN_PAD), tok),
                  pl.BlockSpec((rows, D_MODEL), tok)] + _mlstm_weight_specs(const) + state_specs,
        out_specs=[pl.BlockSpec((rows, D_MODEL), lambda i: (i, 0))] + state_specs,
        out_shape=[jax.ShapeDtypeStruct((n_seq * t_len, D_MODEL), F32),
                   jax.ShapeDtypeStruct((n_seq, ML_HEADS, ML_QK, ML_V), F32),
                   jax.ShapeDtypeStruct((n_seq, ML_HEADS, ML_QK), F32),
                   jax.ShapeDtypeStruct((n_seq, 1, ML_HEADS), F32)],
        compiler_params=_cparams(1),
        name="mlstm_sample",
    )(z, x, bias, og, tril, wo, c0, n0, m0)


RSQRT2 = 0.7071067811865476


def _scaled_gelu(xs):
    return xs * (1.0 + lax.erf(xs))


SEL_T = 256


def _batcher_pairs(n):
    pairs = []
    p = 1
    while p < n:
        k = p
        while k >= 1:
            for j in range(k % p, n - k, 2 * k):
                for i in range(min(k, n - j - k)):
                    if (i + j) // (2 * p) == (i + j + k) // (2 * p):
                        pairs.append((i + j, i + j + k))
            k //= 2
        p *= 2
    return pairs


def _bitonic_merge_pairs(n):
    pairs = []
    s = n // 2
    while s >= 1:
        pairs += [(i, i + s) for i in range(n) if not i & s]
        s //= 2
    return pairs


def _top16_of_keys(s, out):
    n_vreg = N_KEYS // SUBLANES
    assert n_vreg == PEER_TOPK and N_KEYS == 128
    rounded = (s + 0.0).astype(BF16).astype(F32)
    b16 = lax.shift_right_logical(lax.bitcast_convert_type(rounded, jnp.int32), 16)
    code = jnp.where(b16 >= 0x8000, b16 ^ 0xFFFF, b16 | 0x8000)
    row = lax.broadcasted_iota(jnp.int32, s.shape, 0)
    keys = lax.bitcast_convert_type((code * N_KEYS + (N_KEYS - 1 - row)) | 0x4B000000, F32)

    x = [keys[v * SUBLANES:(v + 1) * SUBLANES] for v in range(n_vreg)]

    def exchange(i, j):
        x[i], x[j] = jnp.maximum(x[i], x[j]), jnp.minimum(x[i], x[j])

    for i, j in _batcher_pairs(n_vreg):
        exchange(i, j)
    yield
    for shift in (SUBLANES // 2, SUBLANES // 4, SUBLANES // 8):
        other = [pltpu.roll(a, shift, 0) for a in x]
        x = [jnp.maximum(x[i], other[n_vreg - 1 - i]) for i in range(n_vreg)]
        for i, j in _bitonic_merge_pairs(n_vreg):
            exchange(i, j)
        yield
    top = lax.bitcast_convert_type(jnp.concatenate([a[0:1] for a in x], axis=0), jnp.int32) & 0x7FFFFF
    rows = (N_KEYS - 1 - (top & (N_KEYS - 1))).astype(F32)
    code = lax.shift_right_logical(top, 7)
    b16 = jnp.where(code >= 0x8000, code ^ 0x8000, code ^ 0xFFFF)
    out.append((lax.bitcast_convert_type(b16 << 16, F32), rows))


class _TopK:
    def __init__(self, s, payload=None):
        self.s = s
        self.payload = payload
        self.rows = lax.broadcasted_iota(jnp.int32, s.shape, 0).astype(F32).astype(s.dtype)
        self.vals, self.picks = [], []

    def step(self, n):
        dt = self.s.dtype
        bound = jnp.asarray(self.s.shape[0], dt)
        for _ in range(n):
            m = jnp.max(self.s, axis=0, keepdims=True)
            first = jnp.min(jnp.where(self.s == m, self.rows, bound), axis=0, keepdims=True)
            sel = self.rows == first
            self.vals.append(m)
            if self.payload is None:
                self.picks.append(first)
            else:
                self.picks.append(jnp.max(jnp.where(sel, self.payload, -1.0), axis=0, keepdims=True))
            self.s = jnp.where(sel, jnp.asarray(-jnp.inf, dt), self.s)

    def result(self):
        return jnp.concatenate(self.vals, axis=0), jnp.concatenate(self.picks, axis=0)


def _pair_candidates(first, second):
    def pairs(a, b, combine):
        h8 = SUBLANES
        rows = [combine(a[0:1], b)]
        rows += [combine(a[k1:k1 + 1], b[0:h8]) for k1 in range(1, h8)]
        rows.append(combine(a[h8:PEER_TOPK], b[0:1]))
        return jnp.concatenate(rows, axis=0)

    (v1, i1), (v2, i2) = [tuple(a.astype(F32) for a in lst) for lst in (first, second)]
    return pairs(v1, v2, lambda a, b: a + b), pairs(i1, i2, lambda a, b: a * N_KEYS + b)


def _peer_fused_kernel(xs_ref, xm_ref, g_ref, wq_ref, sk_ref, u_ref, v_ref, *rest, split_blk):
    if split_blk is None:
        o_ref, q_scr, sel_e, sel_g, xn_scr, a_scr, bgt_scr, bg_scr, gmat, coef_scr = rest
        out_refs = ()
    else:
        *out_refs, q_scr, sel_e, sel_g, xn_scr, a_scr, bgt_scr, bg_scr, gmat, coef_scr, o_ref = rest
    i = pl.program_id(0)
    e = pl.program_id(1)
    n_steps = pl.num_programs(1)
    t = xm_ref.shape[0]
    n_half = t // SEL_T
    half_keys = N_KEYS // 2
    cur = i % 2

    def project_queries():
        xn = _rmsnorm(xs_ref[...], g_ref[...]).astype(BF16)
        q = jnp.dot(xn, wq_ref[...], preferred_element_type=F32).astype(BF16)
        for hf in range(n_half):
            for c in range(2 * PEER_HEADS):
                q_scr[hf, c] = q[hf * SEL_T:(hf + 1) * SEL_T, c * LANES:(c + 1) * LANES]

    @pl.when(e == 0)
    def _():
        @pl.when(i == 0)
        def _():
            project_queries()
            sel_e[...] = jnp.zeros_like(sel_e)
            sel_g[...] = jnp.zeros_like(sel_g)

        x = xm_ref[...]
        xn_scr[...] = _rmsnorm(x, g_ref[...]).astype(BF16)
        o_ref[...] = x
        coef_scr[0] = jnp.zeros(coef_scr.shape[1:], BF16)
        prev = 1 - cur
        lane_groups = SEL_T // LANES
        for hf in range(n_half):
            ef = sel_e[prev, hf]
            af = jnp.floor(ef * (1.0 / N_KEYS))
            bf = ef - af * N_KEYS
            gf = sel_g[prev, hf]
            a_scr[hf * SEL_T:(hf + 1) * SEL_T, :] = af.T
            bgf = bf + RSQRT2 * gf
            bgt_scr[hf * SEL_T:(hf + 1) * SEL_T, :] = bgf.T
            for lg in range(lane_groups):
                bg_scr[hf * lane_groups + lg] = bgf[:, lg * LANES:(lg + 1) * LANES]
        r = lax.broadcasted_iota(jnp.int32, (N_KEYS, LANES), 0)
        packed_shape = (N_KEYS // (2 * SUBLANES), 2 * SUBLANES, LANES)
        a_of_row = jnp.where(r < half_keys, 2 * r, 2 * (r - half_keys) + 1).astype(F32).astype(BF16)
        a_of_row = a_of_row.reshape(packed_shape)
        one = jnp.ones(packed_shape, BF16)
        zero = jnp.zeros(packed_shape, BF16)
        b_of_lane = lax.broadcasted_iota(jnp.int32, (N_KEYS, LANES), 1).astype(F32)
        b_of_row = r.astype(F32).astype(BF16).reshape(packed_shape)
        per_body = 32
        bodies_per_group = LANES // per_body
        row_major_every = 3

        def body(it, carry):
            grp = lax.shift_right_logical(it, bodies_per_group.bit_length() - 1)
            sub = it & (bodies_per_group - 1)
            shift = (LANES - sub * per_body) & (LANES - 1)
            bg = pltpu.roll(bg_scr[grp], shift, 1)
            tok0 = pl.multiple_of(it * per_body, per_body)
            arows = a_scr[pl.ds(tok0, per_body), :]
            bgrows = bgt_scr[pl.ds(tok0, per_body), :]
            for k in range(per_body):
                arow = jnp.broadcast_to(arows[k:k + 1], (2 * SUBLANES, LANES)).astype(BF16)[None]
                pa = jnp.where(a_of_row == arow, one, zero).reshape(N_KEYS, LANES)
                if k % row_major_every == 0:
                    bgrow = jnp.broadcast_to(bgrows[k:k + 1], (2 * SUBLANES, LANES))
                    brow = jnp.floor(bgrow)
                    grow = (bgrow - brow).astype(BF16)[None]
                    qb = jnp.where(b_of_row == brow.astype(BF16)[None], grow, zero).reshape(N_KEYS, LANES)
                    tile = lax.dot_general(pa, qb, NT_DIMS, preferred_element_type=F32)
                else:
                    bgcol = jnp.broadcast_to(bg[:, k:k + 1], (N_KEYS, LANES))
                    bcol = jnp.floor(bgcol)
                    gcol = bgcol - bcol
                    qbt = jnp.where(bcol == b_of_lane, gcol, 0.0).astype(BF16)
                    tile = jnp.dot(pa, qbt, preferred_element_type=F32)
                row0 = pl.multiple_of((tok0 + k) * G_PITCH, SUBLANES)
                gmat[pl.ds(row0, half_keys), :] = pltpu.pack_elementwise(
                    [tile[0:half_keys], tile[half_keys:N_KEYS]], packed_dtype=BF16)
            return carry

        lax.fori_loop(0, t // per_body, body, 0)

    unit = jnp.minimum(e, n_half * PEER_HEADS - 1)
    half = lax.shift_right_logical(unit, PEER_HEADS.bit_length() - 1)
    h = unit & (PEER_HEADS - 1)
    rd = e & 1
    blk = jnp.minimum(e, n_steps - 2)
    eb = u_ref.shape[0]
    n_stages = 4
    exp_w = eb // n_stages
    out_w = v_ref.shape[1] // n_stages
    words_per_stage = exp_w // (2 * N_KEYS)

    def value_piece(c):
        ocols = slice(c * out_w, (c + 1) * out_w)
        o_ref[:, ocols] += jnp.dot(coef_scr[rd], v_ref[:, ocols], preferred_element_type=F32)

    def act_piece(c):
        ecols = slice(c * exp_w, (c + 1) * exp_w)
        act = lax.dot_general(xn_scr[...], u_ref[ecols, :], NT_DIMS, preferred_element_type=F32)
        gates = []
        for w in range(words_per_stage):
            word = gmat[pl.ds((blk * n_stages + c) * words_per_stage + w, t, stride=G_PITCH), :]
            gates.append(lax.bitcast_convert_type(word << 16, F32))
            gates.append(lax.bitcast_convert_type(word & jnp.int32(-65536), F32))
        coef_scr[1 - rd, :, ecols] = (_scaled_gelu(act) * jnp.concatenate(gates, axis=1)).astype(BF16)

    def retrieval():
        lists = []
        for p in range(2):
            st = lax.dot_general(sk_ref[h, p], q_scr[half, 2 * h + p], NT_DIMS, preferred_element_type=F32)
            yield from _top16_of_keys(st, lists)
            yield
        cand, expert = _pair_candidates(lists[0], lists[1])
        second_level = _TopK(cand, payload=expert)
        for _ in range(4):
            second_level.step(PEER_TOPK // 4)
            yield
        best, e_sel = second_level.result()
        ex = jnp.exp(best - best[0:1])
        slot0 = pl.multiple_of(h * PEER_TOPK, PEER_TOPK)
        sel_e[cur, half, pl.ds(slot0, PEER_TOPK), :] = e_sel
        sel_g[cur, half, pl.ds(slot0, PEER_TOPK), :] = ex / jnp.sum(ex, axis=0, keepdims=True)

    pieces = [functools.partial(f, c) for c in range(n_stages) for f in (value_piece, act_piece)]
    parts = retrieval()
    parts_per_piece = 2
    for piece in pieces:
        piece()
        for _ in range(parts_per_piece):
            next(parts, None)
    for _ in parts:
        pass

    @pl.when(e == n_steps - 1)
    def _():
        project_queries()
        if out_refs:
            first_ref, second_ref = out_refs

            @pl.when(i - 1 < split_blk)
            def _():
                first_ref[...] = o_ref[...]

            @pl.when(i - 1 >= split_blk)
            def _():
                second_ref[...] = o_ref[...]


def _peer(x, g, wq, sk, u, v, t=512, eb=1024, split_rows=None):
    n, d = x.shape
    nslot = PEER_HEADS * PEER_TOPK
    n_tok_blk = n // t
    n_exp_blk = N_KEYS * N_KEYS // eb
    n_half = t // SEL_T
    assert n_exp_blk == n_half * PEER_HEADS
    last_tok = n_tok_blk - 1
    once = pl.Buffered(1)
    if split_rows is None:
        split_blk = None
        out_specs = pl.BlockSpec((t, d), lambda i, e: (jnp.maximum(i - 1, 0), 0))
        out_shape = jax.ShapeDtypeStruct((n, d), F32)
        acc = []
    else:
        split_blk = split_rows // t
        out_specs = [pl.BlockSpec((t, d), lambda i, e: (jnp.clip(i - 1, 0, split_blk - 1), 0)),
                     pl.BlockSpec((t, d), lambda i, e: (jnp.clip(i - 1 - split_blk, 0, last_tok - split_blk), 0))]
        out_shape = [jax.ShapeDtypeStruct((split_rows, d), F32), jax.ShapeDtypeStruct((n - split_rows, d), F32)]
        acc = [pltpu.VMEM((t, d), F32)]
    return pl.pallas_call(
        functools.partial(_peer_fused_kernel, split_blk=split_blk),
        grid=(n_tok_blk + 1, n_exp_blk + 1),
        in_specs=[pl.BlockSpec((t, d), lambda i, e: (jnp.minimum(i + e // n_exp_blk, last_tok), 0),
                               pipeline_mode=once),
                  pl.BlockSpec((t, d), lambda i, e: (jnp.maximum(i - 1, 0), 0)),
                  pl.BlockSpec((1, d), lambda i, e: (0, 0)),
                  pl.BlockSpec((d, 2 * PEER_HEADS * LANES), lambda i, e: (0, 0), pipeline_mode=once),
                  pl.BlockSpec((PEER_HEADS, 2, N_KEYS, LANES), lambda i, e: (0, 0, 0, 0), pipeline_mode=once),
                  pl.BlockSpec((eb, d), lambda i, e: (jnp.minimum(e, n_exp_blk - 1), 0)),
                  pl.BlockSpec((eb, d), lambda i, e: (jnp.maximum(e - 1, 0), 0))],
        out_specs=out_specs,
        out_shape=out_shape,
        scratch_shapes=[pltpu.VMEM((n_half, 2 * PEER_HEADS, SEL_T, LANES), BF16),
                        pltpu.VMEM((2, n_half, nslot, SEL_T), F32),
                        pltpu.VMEM((2, n_half, nslot, SEL_T), F32),
                        pltpu.VMEM((t, d), BF16),
                        pltpu.VMEM((t, nslot), F32),
                        pltpu.VMEM((t, nslot), F32),
                        pltpu.VMEM((t // LANES, nslot, LANES), F32),
                        pltpu.VMEM((t * G_PITCH, LANES), jnp.int32),
                        pltpu.VMEM((2, t, eb), BF16)] + acc,
        compiler_params=_cparams(2),
        name="peer",
    )(x, x, g.reshape(1, d), wq, sk, u, v)


def _rope_tables(pos):
    half = HEAD_DIM // 2
    inv_freq = ROPE_THETA ** (-jnp.arange(half, dtype=F32) / half)
    ang = pos.astype(F32)[:, None] * inv_freq[None, :]
    cos, sin = jnp.cos(ang), jnp.sin(ang)
    reps = LANES // HEAD_DIM
    cos_t = jnp.tile(jnp.concatenate([cos, cos], axis=1), (1, reps))
    sin_t = jnp.tile(jnp.concatenate([-sin, sin], axis=1), (1, reps))
    return cos_t, sin_t


def kernel(x_prompt, x_sample, cache_conv, cache_win_k, cache_win_v, state_mlstm_C, state_mlstm_n,
           state_mlstm_m, norm_mix, norm_ffn, ab_w_in, ab_conv_w, ab_q_gain, ab_k_gain, ab_sinks, ab_w_out,
           ml_w_in, ml_gate_bias, ml_out_gain, ml_w_out, peer_w_q, peer_sub_keys, peer_u, peer_v):
    n_batch, seq, d = x_prompt.shape
    n_seq, t_len, _ = x_sample.shape
    n_prompt = n_batch * seq
    assert d == D_MODEL and t_len == SUBLANES and norm_mix.shape[0] == 2

    xp = x_prompt.reshape(n_prompt, d)
    xs = x_sample.reshape(n_seq * t_len, d)

    cos_p, sin_p = _rope_tables(jnp.arange(seq, dtype=jnp.int32))
    cos_s, sin_s = _rope_tables(PAST_LEN + jnp.arange(t_len, dtype=jnp.int32))
    bt = 16
    cos_s, sin_s = jnp.tile(cos_s, (bt, 1)), jnp.tile(sin_s, (bt, 1))
    lane = jnp.arange(LANES)
    seg = (lane[:, None] // HEAD_DIM == lane[None, :] // HEAD_DIM).astype(BF16)
    reps = LANES // HEAD_DIM
    qg = jnp.tile(ab_q_gain[0], reps).reshape(1, LANES)
    kg = jnp.tile(ab_k_gain[0], reps).reshape(1, LANES)
    wo_ab = ab_w_out[0].astype(BF16)

    z = _norm_proj([xp, xs], norm_mix[0], ab_w_in[0].astype(BF16))
    y_p, k_p, v_p, c_p = _ab_prompt(z, xp, n_batch, seq, cos_p, sin_p, ab_conv_w[0], qg, kg, ab_sinks[0],
                                    seg, wo_ab)
    y_s, c_s, k_s, v_s = _ab_sample(z, xs, n_prompt, n_seq, t_len, cos_s, sin_s, ab_conv_w[0], qg, kg,
                                    ab_sinks[0], seg, wo_ab, cache_conv[0],
                                    cache_win_k[0].reshape(n_seq, WINDOW, LANES),
                                    cache_win_v[0].reshape(n_seq, WINDOW, LANES), bt=bt)
    x = jnp.concatenate([y_p.reshape(n_prompt, d), y_s], axis=0)
    x = _peer(x, norm_ffn[0], peer_w_q[0].astype(BF16), peer_sub_keys[0].astype(BF16),
              (peer_u[0] * RSQRT2).astype(BF16), peer_v[0].astype(BF16))

    n_gate = 2 * ML_HEADS
    w_in = jnp.pad(ml_w_in[0], ((0, 0), (0, ML_IN_PAD - ml_w_in.shape[2]))).astype(BF16)
    bias = jnp.pad(ml_gate_bias[0], (0, LANES - n_gate)).reshape(1, LANES)
    og = ml_out_gain[0].reshape(1, D_MODEL)
    idx = jnp.arange(ML_CHUNK)
    tril = (idx[None, :] <= idx[:, None]).astype(BF16)
    wo_ml = ml_w_out[0].astype(BF16)

    z = _norm_proj([x], norm_mix[1], w_in)
    y_p, cm_p, nm_p, mm_p = _mlstm_prompt(z, x, n_batch, seq, bias, og, tril, wo_ml)
    y_s, cm_s, nm_s, mm_s = _mlstm_sample(z, x, n_prompt, n_seq, t_len, bias, og, tril, wo_ml,
                                          state_mlstm_C[0], state_mlstm_n[0],
                                          state_mlstm_m[0].reshape(n_seq, 1, ML_HEADS))
    x = jnp.concatenate([y_p.reshape(n_prompt, d), y_s], axis=0)
    out_p, out_s = _peer(x, norm_ffn[1], peer_w_q[1].astype(BF16), peer_sub_keys[1].astype(BF16),
                         (peer_u[1] * RSQRT2).astype(BF16), peer_v[1].astype(BF16), split_rows=n_prompt)

    y_prompt = out_p.reshape(n_batch, seq, d)
    y_sample = out_s.reshape(n_seq, t_len, d)
    kv_shape_p = (1, n_batch, WINDOW, N_KV_HEADS, HEAD_DIM)
    kv_shape_s = (1, n_seq, WINDOW, N_KV_HEADS, HEAD_DIM)
    return (y_prompt, y_sample,
            c_p[:, SUBLANES - 2:, :][None], k_p.reshape(kv_shape_p), v_p.reshape(kv_shape_p),
            cm_p[None], nm_p[None], mm_p.reshape(1, n_batch, ML_HEADS),
            c_s[None], k_s.reshape(kv_shape_s), v_s.reshape(kv_shape_s),
            cm_s[None], nm_s[None], mm_s.reshape(1, n_seq, ML_HEADS))
```

```python
import functools

import jax
import jax.numpy as jnp
from jax import lax
from jax.experimental import pallas as pl
from jax.experimental.pallas import tpu as pltpu

F32 = jnp.float32
BF16 = jnp.bfloat16
EPS = 1e-6

D_MODEL = 1024
CONV_DIM = 512
N_Q_HEADS = 8
N_KV_HEADS = 2
HEAD_DIM = 64
WINDOW = 128
ROPE_THETA = 10000.0
AB_IN = 2304
ML_HEADS = 4
ML_QK = 128
ML_V = 256
ML_CHUNK = 128
ML_GATE_COL = 3072
ML_IN_PAD = ML_GATE_COL + 128
N_KEYS = 128
PEER_HEADS = 8
PEER_TOPK = 16
PAST_LEN = 16384

LANES = 128
SUBLANES = 8
G_PITCH = N_KEYS // 2 + SUBLANES
VMEM_LIMIT = 56 * 1024 * 1024

NT_DIMS = (((1,), (1,)), ((), ()))
TN_DIMS = (((0,), (0,)), ((), ()))


def _cparams(n_axes, vmem=VMEM_LIMIT):
    return pltpu.CompilerParams(dimension_semantics=("arbitrary",) * n_axes, vmem_limit_bytes=vmem)


def _rmsnorm(x, g):
    return x * lax.rsqrt(jnp.mean(x * x, axis=-1, keepdims=True) + EPS) * g


def _bf16_pieces(a, terms):
    pieces = []
    rem = a
    for _ in range(terms):
        piece = rem.astype(BF16)
        rem = rem - piece.astype(F32)
        pieces.append(piece)
    return pieces


def _split_dot(a, b_bf16, terms=2):
    return sum(jnp.dot(p, b_bf16, preferred_element_type=F32) for p in _bf16_pieces(a, terms))


def _norm_proj_kernel(*refs, starts):
    x_refs = refs[:len(starts)]
    g_ref, w_ref, o_ref = refs[len(starts):]
    i = pl.program_id(0)
    x = x_refs[0][...]
    for x_ref, start in zip(x_refs[1:], starts[1:]):
        x = jnp.where(i >= start, x_ref[...], x)
    r = _rmsnorm(x, g_ref[...])
    o_ref[...] = jnp.dot(r.astype(BF16), w_ref[...], preferred_element_type=F32)


def _norm_proj(x_parts, g, w_bf16, tm=512):
    d = x_parts[0].shape[1]
    nout = w_bf16.shape[1]
    blocks = [p.shape[0] // tm for p in x_parts]
    starts = tuple(sum(blocks[:k]) for k in range(len(blocks)))
    part_spec = lambda start, nblk: pl.BlockSpec((tm, d), lambda i: (jnp.clip(i - start, 0, nblk - 1), 0))
    return pl.pallas_call(
        functools.partial(_norm_proj_kernel, starts=starts),
        grid=(sum(blocks),),
        in_specs=[part_spec(s, nb) for s, nb in zip(starts, blocks)] + [
            pl.BlockSpec((1, d), lambda i: (0, 0)),
            pl.BlockSpec((d, nout), lambda i: (0, 0))],
        out_specs=pl.BlockSpec((tm, nout), lambda i: (i, 0)),
        out_shape=jax.ShapeDtypeStruct((sum(blocks) * tm, nout), F32),
        compiler_params=_cparams(1),
        name="norm_proj",
    )(*x_parts, g.reshape(1, d), w_bf16)


def _headnorm_rope(xc, gain, cos, sin, seg, hi_half):
    ss = _split_dot(xc * xc, seg)
    xn = xc * lax.rsqrt(ss * (1.0 / HEAD_DIM) + EPS) * gain
    partner = jnp.where(hi_half, pltpu.roll(xn, 32, 1), pltpu.roll(xn, 96, 1))
    return xn * cos + partner * sin


def _softmax_sink(s, mask, sink):
    s = jnp.where(mask, s, -1e30)
    m = jnp.maximum(jnp.max(s, axis=-1, keepdims=True), sink)
    p = jnp.exp(s - m)
    denom = jnp.sum(p, axis=-1, keepdims=True) + jnp.exp(sink - m)
    return (p / denom).astype(BF16)


def _ab_prompt_kernel(sink_ref, *refs, n_batch):
    z_refs, x_refs = refs[0:n_batch], refs[n_batch:2 * n_batch]
    (cos_ref, sin_ref, cw_ref, qg_ref, kg_ref, seg_ref, wo_ref,
     y_ref, kst_ref, vst_ref, cst_ref, pk_ref, pv_ref, pu_ref) = refs[2 * n_batch:]
    j = pl.program_id(0)

    @pl.when(j == 0)
    def _():
        pk_ref[...] = jnp.zeros_like(pk_ref)
        pv_ref[...] = jnp.zeros_like(pv_ref)
        pu_ref[...] = jnp.zeros_like(pu_ref)

    for b in range(n_batch):
        _ab_prompt_block(j, sink_ref, z_refs[b], x_refs[b], cos_ref, sin_ref, cw_ref, qg_ref, kg_ref, seg_ref,
                         wo_ref, y_ref.at[b], kst_ref.at[b], vst_ref.at[b], cst_ref.at[b],
                         pk_ref.at[b], pv_ref.at[b], pu_ref.at[b])


def _ab_prompt_block(j, sink_ref, z_ref, x_ref, cos_ref, sin_ref, cw_ref, qg_ref, kg_ref, seg_ref, wo_ref,
                     y_ref, kst_ref, vst_ref, cst_ref, pk_ref, pv_ref, pu_ref):
    blk = z_ref.shape[0]
    cos = cos_ref[...]
    sin = sin_ref[...]
    seg = seg_ref[...]
    hi_half = (lax.broadcasted_iota(jnp.int32, (blk, LANES), 1) & 32) != 0

    gate_b = z_ref[:, 0:CONV_DIM]
    u = z_ref[:, CONV_DIM:2 * CONV_DIM] * z_ref[:, 2 * CONV_DIM:3 * CONV_DIM]
    ng = blk // SUBLANES
    u3 = u.reshape(ng, SUBLANES, CONV_DIM)
    ext = jnp.concatenate([pu_ref[...][None], u3], axis=0)
    t8 = lax.broadcasted_iota(jnp.int32, (ng, SUBLANES, CONV_DIM), 1)
    r1 = pltpu.roll(ext, 1, 1)
    r2 = pltpu.roll(ext, 2, 1)
    um1 = jnp.where(t8 >= 1, r1[1:], r1[:-1])
    um2 = jnp.where(t8 >= 2, r2[1:], r2[:-1])
    cw = cw_ref[...]
    conv = cw[0:1][None] * um2 + cw[1:2][None] * um1 + cw[2:3][None] * u3
    yconv = gate_b * conv.reshape(blk, CONV_DIM)

    q0 = 3 * CONV_DIM
    k0 = q0 + N_Q_HEADS * HEAD_DIM
    v0 = k0 + N_KV_HEADS * HEAD_DIM
    qg = qg_ref[...]
    qr = [_headnorm_rope(z_ref[:, q0 + c * LANES:q0 + (c + 1) * LANES], qg, cos, sin, seg, hi_half)
          for c in range(N_Q_HEADS * HEAD_DIM // LANES)]
    kr = _headnorm_rope(z_ref[:, k0:k0 + LANES], kg_ref[...], cos, sin, seg, hi_half)
    v = z_ref[:, v0:v0 + LANES]
    pk = pk_ref[...]
    pv = pv_ref[...]

    row = lax.broadcasted_iota(jnp.int32, (blk, 2 * blk), 0)
    col = lax.broadcasted_iota(jnp.int32, (blk, 2 * blk), 1)
    row_prev = row + jnp.where(j == 0, 2 * blk, 0)
    mask = ((col < blk) & (col > row_prev)) | ((col >= blk) & (col - blk <= row))

    kks, vvs = [], []
    for g in range(N_KV_HEADS):
        sl = slice(g * HEAD_DIM, (g + 1) * HEAD_DIM)
        kks.append(jnp.concatenate([pk[:, sl], kr[:, sl]], axis=0).astype(BF16))
        vvs.append(jnp.concatenate([pv[:, sl], v[:, sl]], axis=0).astype(BF16))
    outs = []
    for h in range(N_Q_HEADS):
        g = h // (N_Q_HEADS // N_KV_HEADS)
        qh = qr[h // 2][:, (h % 2) * HEAD_DIM:(h % 2 + 1) * HEAD_DIM].astype(BF16)
        s = lax.dot_general(qh, kks[g], NT_DIMS, preferred_element_type=F32) * (HEAD_DIM ** -0.5)
        p = _softmax_sink(s, mask, sink_ref[h])
        outs.append(jnp.dot(p, vvs[g], preferred_element_type=F32))
    attn = jnp.concatenate(outs, axis=1)

    y = (jnp.dot(yconv.astype(BF16), wo_ref[0:CONV_DIM, :], preferred_element_type=F32)
         + jnp.dot(attn.astype(BF16), wo_ref[CONV_DIM:2 * CONV_DIM, :], preferred_element_type=F32))
    y_ref[...] = y + x_ref[...]

    pk_ref[...] = kr
    pv_ref[...] = v
    pu_ref[...] = u3[ng - 1]
    kst_ref[...] = kr
    vst_ref[...] = v
    cst_ref[...] = u3[ng - 1]


def _ab_prompt(z, x, n_batch, seq, cos, sin, cw, qg, kg, sinks, seg, wo):
    blk = WINDOW
    nb = seq // blk
    const = lambda j, s: (0, 0)
    whole = lambda j, s: (0, 0, 0)
    tok_specs = lambda width: [pl.BlockSpec((blk, width), functools.partial(lambda j, s, b: (b * nb + j, 0), b=b))
                               for b in range(n_batch)]
    grid_spec = pltpu.PrefetchScalarGridSpec(
        num_scalar_prefetch=1,
        grid=(nb,),
        in_specs=tok_specs(AB_IN) + tok_specs(D_MODEL) + [
            pl.BlockSpec((blk, LANES), lambda j, s: (j, 0)),
            pl.BlockSpec((blk, LANES), lambda j, s: (j, 0)),
            pl.BlockSpec((3, CONV_DIM), const),
            pl.BlockSpec((1, LANES), const),
            pl.BlockSpec((1, LANES), const),
            pl.BlockSpec((LANES, LANES), const),
            pl.BlockSpec((D_MODEL, D_MODEL), const)],
        out_specs=[pl.BlockSpec((n_batch, blk, D_MODEL), lambda j, s: (0, j, 0)),
                   pl.BlockSpec((n_batch, blk, LANES), whole),
                   pl.BlockSpec((n_batch, blk, LANES), whole),
                   pl.BlockSpec((n_batch, SUBLANES, CONV_DIM), whole)],
        scratch_shapes=[pltpu.VMEM((n_batch, blk, LANES), F32), pltpu.VMEM((n_batch, blk, LANES), F32),
                        pltpu.VMEM((n_batch, SUBLANES, CONV_DIM), F32)])
    return pl.pallas_call(
        functools.partial(_ab_prompt_kernel, n_batch=n_batch),
        grid_spec=grid_spec,
        out_shape=[jax.ShapeDtypeStruct((n_batch, seq, D_MODEL), F32),
                   jax.ShapeDtypeStruct((n_batch, blk, LANES), F32),
                   jax.ShapeDtypeStruct((n_batch, blk, LANES), F32),
                   jax.ShapeDtypeStruct((n_batch, SUBLANES, CONV_DIM), F32)],
        compiler_params=_cparams(1),
        name="ab_prompt",
    )(sinks, *([z] * n_batch), *([x] * n_batch), cos, sin, cw, qg, kg, seg, wo)


def _ab_sample_kernel(sink_ref, z_ref, x_ref, cos_ref, sin_ref, cw_ref, qg_ref, kg_ref, seg_ref, wo_ref,
                      cc_ref, ck_ref, cv_ref, y_ref, cs_ref, ks_ref, vs_ref):
    rows = z_ref.shape[0]
    t_len = SUBLANES
    bt = rows // t_len
    cos = cos_ref[...]
    sin = sin_ref[...]
    seg = seg_ref[...]
    hi_half = (lax.broadcasted_iota(jnp.int32, (rows, LANES), 1) & 32) != 0

    gate_b = z_ref[:, 0:CONV_DIM]
    u = z_ref[:, CONV_DIM:2 * CONV_DIM] * z_ref[:, 2 * CONV_DIM:3 * CONV_DIM]
    u3 = u.reshape(bt, t_len, CONV_DIM)
    cc = cc_ref[...]
    c0 = cc[:, 0:1, :]
    c1 = cc[:, 1:2, :]
    t8 = lax.broadcasted_iota(jnp.int32, (bt, t_len, CONV_DIM), 1)
    r1 = pltpu.roll(u3, 1, 1)
    r2 = pltpu.roll(u3, 2, 1)
    um1 = jnp.where(t8 >= 1, r1, c1)
    um2 = jnp.where(t8 >= 2, r2, jnp.where(t8 == 1, c1, c0))
    cw = cw_ref[...]
    conv = cw[0:1][None] * um2 + cw[1:2][None] * um1 + cw[2:3][None] * u3
    yconv = gate_b * conv.reshape(rows, CONV_DIM)
    cs_ref[...] = r2[:, 0:2, :]

    q0 = 3 * CONV_DIM
    k0 = q0 + N_Q_HEADS * HEAD_DIM
    v0 = k0 + N_KV_HEADS * HEAD_DIM
    qg = qg_ref[...]
    qr = [_headnorm_rope(z_ref[:, q0 + c * LANES:q0 + (c + 1) * LANES], qg, cos, sin, seg, hi_half)
          for c in range(N_Q_HEADS * HEAD_DIM // LANES)]
    kr = _headnorm_rope(z_ref[:, k0:k0 + LANES], kg_ref[...], cos, sin, seg, hi_half)
    v = z_ref[:, v0:v0 + LANES]

    group = N_Q_HEADS // N_KV_HEADS
    nq = group * t_len
    nk = 2 * WINDOW
    qrow = lax.broadcasted_iota(jnp.int32, (nq, nk), 0)
    t_q = qrow & (t_len - 1)
    col = lax.broadcasted_iota(jnp.int32, (nq, nk), 1)
    mask = (((col < WINDOW) & (col > t_q)) | ((col >= WINDOW) & (col - WINDOW <= t_q)))[None]
    hrow = lax.broadcasted_iota(jnp.int32, (nq, 1), 0) // t_len
    pad = jnp.zeros((bt, nk - WINDOW - t_len, HEAD_DIM), F32)

    outs = [None] * N_Q_HEADS
    for g in range(N_KV_HEADS):
        sl = slice(g * HEAD_DIM, (g + 1) * HEAD_DIM)
        qs = jnp.concatenate(
            [qr[h // 2][:, (h % 2) * HEAD_DIM:(h % 2 + 1) * HEAD_DIM].reshape(bt, t_len, HEAD_DIM)
             for h in range(g * group, (g + 1) * group)], axis=1)
        kk = jnp.concatenate([ck_ref[:, :, sl], kr[:, sl].reshape(bt, t_len, HEAD_DIM), pad], axis=1)
        vv = jnp.concatenate([cv_ref[:, :, sl], v[:, sl].reshape(bt, t_len, HEAD_DIM), pad], axis=1)
        s = jnp.einsum('bqd,bkd->bqk', qs.astype(BF16), kk.astype(BF16),
                       preferred_element_type=F32) * (HEAD_DIM ** -0.5)
        sink = jnp.zeros((nq, 1), F32)
        for hh in range(group):
            sink = jnp.where(hrow == hh, sink_ref[g * group + hh], sink)
        p = _softmax_sink(s, mask, sink[None])
        o = jnp.einsum('bqk,bkd->bqd', p, vv.astype(BF16), preferred_element_type=F32)
        for hh in range(group):
            outs[g * group + hh] = o[:, hh * t_len:(hh + 1) * t_len, :].reshape(rows, HEAD_DIM)
    attn = jnp.concatenate(outs, axis=1)

    y = (jnp.dot(yconv.astype(BF16), wo_ref[0:CONV_DIM, :], preferred_element_type=F32)
         + jnp.dot(attn.astype(BF16), wo_ref[CONV_DIM:2 * CONV_DIM, :], preferred_element_type=F32))
    y_ref[...] = y + x_ref[...]

    keep = WINDOW - t_len
    ks_ref[:, 0:keep, :] = ck_ref[:, t_len:WINDOW, :]
    ks_ref[:, keep:WINDOW, :] = kr.reshape(bt, t_len, LANES)
    vs_ref[:, 0:keep, :] = cv_ref[:, t_len:WINDOW, :]
    vs_ref[:, keep:WINDOW, :] = v.reshape(bt, t_len, LANES)


def _ab_sample(z, x, row0, n_seq, t_len, cos, sin, cw, qg, kg, sinks, seg, wo, cc, ck, cv, bt=16):
    rows = bt * t_len
    blk0 = row0 // rows
    tok = lambda i, s: (blk0 + i, 0)
    const = lambda i, s: (0, 0)
    seq3 = lambda i, s: (i, 0, 0)
    grid_spec = pltpu.PrefetchScalarGridSpec(
        num_scalar_prefetch=1,
        grid=(n_seq // bt,),
        in_specs=[pl.BlockSpec((rows, AB_IN), tok),
                  pl.BlockSpec((rows, D_MODEL), lambda i, s: (i, 0)),
                  pl.BlockSpec((rows, LANES), const),
                  pl.BlockSpec((rows, LANES), const),
                  pl.BlockSpec((3, CONV_DIM), const),
                  pl.BlockSpec((1, LANES), const),
                  pl.BlockSpec((1, LANES), const),
                  pl.BlockSpec((LANES, LANES), const),
                  pl.BlockSpec((D_MODEL, D_MODEL), const),
                  pl.BlockSpec((bt, 2, CONV_DIM), seq3),
                  pl.BlockSpec((bt, WINDOW, LANES), seq3),
                  pl.BlockSpec((bt, WINDOW, LANES), seq3)],
        out_specs=[pl.BlockSpec((rows, D_MODEL), lambda i, s: (i, 0)),
                   pl.BlockSpec((bt, 2, CONV_DIM), seq3),
                   pl.BlockSpec((bt, WINDOW, LANES), seq3),
                   pl.BlockSpec((bt, WINDOW, LANES), seq3)])
    return pl.pallas_call(
        _ab_sample_kernel,
        grid_spec=grid_spec,
        out_shape=[jax.ShapeDtypeStruct((n_seq * t_len, D_MODEL), F32),
                   jax.ShapeDtypeStruct((n_seq, 2, CONV_DIM), F32),
                   jax.ShapeDtypeStruct((n_seq, WINDOW, LANES), F32),
                   jax.ShapeDtypeStruct((n_seq, WINDOW, LANES), F32)],
        compiler_params=_cparams(1),
        name="ab_sample",
    )(sinks, z, x, cos, sin, cw, qg, kg, seg, wo, cc, ck, cv)


def _log_sigmoid(x):
    return jnp.minimum(x, 0.0) - jnp.log(1.0 + jnp.exp(-jnp.abs(x)))


def _mlstm_chunk(z, bias, og, tril, c_src, n_src, m_src, c_dst, n_dst, m_dst, n_real, rows):
    L = z.shape[0]
    gates = z[:, ML_GATE_COL:ML_GATE_COL + LANES] + bias
    if n_real < L:
        live = lax.broadcasted_iota(jnp.int32, (L, LANES), 0) < n_real
        li_all = jnp.where(live, gates, -1e30)
        lf_all = jnp.where(live, _log_sigmoid(gates), 0.0)
    else:
        li_all = gates
        lf_all = _log_sigmoid(gates)
    lf_pieces = _bf16_pieces(lf_all, 3)
    f_col_all = sum(jnp.dot(tril[0:rows], p, preferred_element_type=F32) for p in lf_pieces)
    f_row_all = sum(lax.dot_general(p, tril, (((0,), (1,)), ((), ())), preferred_element_type=F32)
                    for p in lf_pieces)
    li_t = li_all.T
    rr = lax.broadcasted_iota(jnp.int32, (rows, L), 0)
    cc = lax.broadcasted_iota(jnp.int32, (rows, L), 1)
    causal = cc <= rr

    outs, m_new_all = [], []
    for h in range(ML_HEADS):
        f_col = f_col_all[:, ML_HEADS + h:ML_HEADS + h + 1]
        f_row = f_row_all[ML_HEADS + h:ML_HEADS + h + 1, :]
        li_row = li_t[h:h + 1, :]
        li_col = li_all[0:rows, h:h + 1]
        m0 = m_src[0:1, h:h + 1]
        c0 = c_src[h]
        n0 = n_src[h:h + 1, :]
        qh = z[0:rows, h * ML_QK:(h + 1) * ML_QK]
        kh = z[:, ML_HEADS * ML_QK + h * ML_QK:ML_HEADS * ML_QK + (h + 1) * ML_QK] * (ML_QK ** -0.5)
        v_off = 2 * ML_HEADS * ML_QK
        vh = z[:, v_off + h * ML_V:v_off + (h + 1) * ML_V]
        o_off = v_off + ML_HEADS * ML_V
        oh = z[0:rows, o_off + h * ML_V:o_off + (h + 1) * ML_V]
        qb = qh.astype(BF16)
        vb = vh.astype(BF16)

        dmat = jnp.where(causal, f_col - f_row + li_row, -jnp.inf)
        gcar = f_col + m0
        m_t = jnp.maximum(jnp.max(dmat, axis=-1, keepdims=True), gcar)
        w = jnp.exp(dmat - m_t)
        s = lax.dot_general(qb, kh.astype(BF16), NT_DIMS, preferred_element_type=F32) * w
        carry = jnp.exp(gcar - m_t)
        num = (jnp.dot(s.astype(BF16), vb, preferred_element_type=F32)
               + jnp.dot(qb, c0.astype(BF16), preferred_element_type=F32) * carry)
        den = jnp.sum(s, axis=-1, keepdims=True) + carry * jnp.sum(qh * n0, axis=-1, keepdims=True)
        hout = num / jnp.maximum(jnp.abs(den), jnp.exp(-m_t))

        f_last = f_col[n_real - 1:n_real, :]
        w_end = f_last - f_col + li_col
        m_new = jnp.maximum(f_last + m0, jnp.max(w_end, axis=0, keepdims=True))
        a_end = jnp.exp(w_end - m_new)
        scale = jnp.exp(f_last + m0 - m_new)
        ka = kh[0:rows] * a_end
        c_dst[h] = scale * c0 + lax.dot_general(ka.astype(BF16), vb[0:rows], TN_DIMS,
                                                preferred_element_type=F32)
        n_dst[h:h + 1, :] = scale * n0 + jnp.sum(ka, axis=0, keepdims=True)
        m_new_all.append(m_new)

        hn = _rmsnorm(hout, og[:, h * ML_V:(h + 1) * ML_V])
        outs.append(jax.nn.sigmoid(oh) * hn)
    m_dst[...] = jnp.concatenate(m_new_all, axis=1)
    return jnp.concatenate(outs, axis=1)


def _mlstm_prompt_kernel(*refs, n_batch):
    z_refs, x_refs = refs[0:n_batch], refs[n_batch:2 * n_batch]
    bias_ref, og_ref, tril_ref, wo_ref, y_ref, c_ref, n_ref, m_ref = refs[2 * n_batch:]

    @pl.when(pl.program_id(0) == 0)
    def _():
        c_ref[...] = jnp.zeros_like(c_ref)
        n_ref[...] = jnp.zeros_like(n_ref)
        m_ref[...] = jnp.zeros_like(m_ref)

    for b in range(n_batch):
        state = (c_ref.at[b], n_ref.at[b], m_ref.at[b])
        out = _mlstm_chunk(z_refs[b][...], bias_ref[...], og_ref[...], tril_ref[...], *state, *state,
                           ML_CHUNK, ML_CHUNK)
        y_ref[b] = jnp.dot(out.astype(BF16), wo_ref[...], preferred_element_type=F32) + x_refs[b][...]


def _mlstm_sample_kernel(z_ref, x_ref, bias_ref, og_ref, tril_ref, wo_ref, c0_ref, n0_ref, m0_ref,
                         y_ref, c_ref, n_ref, m_ref, *, t_len):
    n_here = z_ref.shape[0] // t_len
    q_rows = 2 * SUBLANES
    outs = []
    for s in range(n_here):
        zpad = jnp.concatenate([z_ref[s * t_len:(s + 1) * t_len, :],
                                jnp.zeros((ML_CHUNK - t_len, ML_IN_PAD), F32)], axis=0)
        out = _mlstm_chunk(zpad, bias_ref[...], og_ref[...], tril_ref[...],
                           c0_ref.at[s], n0_ref.at[s], m0_ref.at[s], c_ref.at[s], n_ref.at[s], m_ref.at[s],
                           t_len, q_rows)
        outs.append(out[0:t_len])
    out_all = jnp.concatenate(outs, axis=0).astype(BF16)
    y_ref[...] = jnp.dot(out_all, wo_ref[...], preferred_element_type=F32) + x_ref[...]


def _mlstm_weight_specs(const):
    return [pl.BlockSpec((1, LANES), const),
            pl.BlockSpec((1, D_MODEL), const),
            pl.BlockSpec((ML_CHUNK, ML_CHUNK), const),
            pl.BlockSpec((D_MODEL, D_MODEL), const)]


def _mlstm_prompt(z, x, n_batch, seq, bias, og, tril, wo):
    nc = seq // ML_CHUNK
    const = lambda j: (0, 0)
    tok_specs = lambda width: [pl.BlockSpec((ML_CHUNK, width), functools.partial(lambda j, b: (b * nc + j, 0), b=b))
                               for b in range(n_batch)]
    return pl.pallas_call(
        functools.partial(_mlstm_prompt_kernel, n_batch=n_batch),
        grid=(nc,),
        in_specs=tok_specs(ML_IN_PAD) + tok_specs(D_MODEL) + _mlstm_weight_specs(const),
        out_specs=[pl.BlockSpec((n_batch, ML_CHUNK, D_MODEL), lambda j: (0, j, 0)),
                   pl.BlockSpec((n_batch, ML_HEADS, ML_QK, ML_V), lambda j: (0, 0, 0, 0)),
                   pl.BlockSpec((n_batch, ML_HEADS, ML_QK), lambda j: (0, 0, 0)),
                   pl.BlockSpec((n_batch, 1, ML_HEADS), lambda j: (0, 0, 0))],
        out_shape=[jax.ShapeDtypeStruct((n_batch, seq, D_MODEL), F32),
                   jax.ShapeDtypeStruct((n_batch, ML_HEADS, ML_QK, ML_V), F32),
                   jax.ShapeDtypeStruct((n_batch, ML_HEADS, ML_QK), F32),
                   jax.ShapeDtypeStruct((n_batch, 1, ML_HEADS), F32)],
        compiler_params=_cparams(1),
        name="mlstm_prompt",
    )(*([z] * n_batch), *([x] * n_batch), bias, og, tril, wo)


def _mlstm_sample(z, x, row0, n_seq, t_len, bias, og, tril, wo, c0, n0, m0, seqs_per_step=8):
    rows = seqs_per_step * t_len
    blk0 = row0 // rows
    tok = lambda i: (blk0 + i, 0)
    const = lambda i: (0, 0)
    st4 = lambda i: (i, 0, 0, 0)
    st3 = lambda i: (i, 0, 0)
    state_specs = [pl.BlockSpec((seqs_per_step, ML_HEADS, ML_QK, ML_V), st4),
                   pl.BlockSpec((seqs_per_step, ML_HEADS, ML_QK), st3),
                   pl.BlockSpec((seqs_per_step, 1, ML_HEADS), st3)]
    return pl.pallas_call(
        functools.partial(_mlstm_sample_kernel, t_len=t_len),
        grid=(n_seq // seqs_per_step,),
        in_specs=[pl.BlockSpec((rows, ML_IN_PAD), tok),
                  pl.BlockSpec((rows, D_MODEL), tok)] + _mlstm_weight_specs(const) + state_specs,
        out_specs=[pl.BlockSpec((rows, D_MODEL), lambda i: (i, 0))] + state_specs,
        out_shape=[jax.ShapeDtypeStruct((n_seq * t_len, D_MODEL), F32),
                   jax.ShapeDtypeStruct((n_seq, ML_HEADS, ML_QK, ML_V), F32),
                   jax.ShapeDtypeStruct((n_seq, ML_HEADS, ML_QK), F32),
                   jax.ShapeDtypeStruct((n_seq, 1, ML_HEADS), F32)],
        compiler_params=_cparams(1),
        name="mlstm_sample",
    )(z, x, bias, og, tril, wo, c0, n0, m0)


RSQRT2 = 0.7071067811865476


def _scaled_gelu(xs):
    return xs * (1.0 + lax.erf(xs))


SEL_T = 256


def _batcher_pairs(n):
    pairs = []
    p = 1
    while p < n:
        k = p
        while k >= 1:
            for j in range(k % p, n - k, 2 * k):
                for i in range(min(k, n - j - k)):
                    if (i + j) // (2 * p) == (i + j + k) // (2 * p):
                        pairs.append((i + j, i + j + k))
            k //= 2
        p *= 2
    return pairs


def _bitonic_merge_pairs(n):
    pairs = []
    s = n // 2
    while s >= 1:
        pairs += [(i, i + s) for i in range(n) if not i & s]
        s //= 2
    return pairs


def _top16_of_keys(s, out):
    n_vreg = N_KEYS // SUBLANES
    assert n_vreg == PEER_TOPK and N_KEYS == 128
    rounded = (s + 0.0).astype(BF16).astype(F32)
    b16 = lax.shift_right_logical(lax.bitcast_convert_type(rounded, jnp.int32), 16)
    code = jnp.where(b16 >= 0x8000, b16 ^ 0xFFFF, b16 | 0x8000)
    row = lax.broadcasted_iota(jnp.int32, s.shape, 0)
    keys = lax.bitcast_convert_type((code * N_KEYS + (N_KEYS - 1 - row)) | 0x4B000000, F32)

    x = [keys[v * SUBLANES:(v + 1) * SUBLANES] for v in range(n_vreg)]

    def exchange(i, j):
        x[i], x[j] = jnp.maximum(x[i], x[j]), jnp.minimum(x[i], x[j])

    for i, j in _batcher_pairs(n_vreg):
        exchange(i, j)
    yield
    for shift in (SUBLANES // 2, SUBLANES // 4, SUBLANES // 8):
        other = [pltpu.roll(a, shift, 0) for a in x]
        x = [jnp.maximum(x[i], other[n_vreg - 1 - i]) for i in range(n_vreg)]
        for i, j in _bitonic_merge_pairs(n_vreg):
            exchange(i, j)
        yield
    top = lax.bitcast_convert_type(jnp.concatenate([a[0:1] for a in x], axis=0), jnp.int32) & 0x7FFFFF
    rows = (N_KEYS - 1 - (top & (N_KEYS - 1))).astype(F32)
    code = lax.shift_right_logical(top, 7)
    b16 = jnp.where(code >= 0x8000, code ^ 0x8000, code ^ 0xFFFF)
    out.append((lax.bitcast_convert_type(b16 << 16, F32), rows))


class _TopK:
    def __init__(self, s, payload=None):
        self.s = s
        self.payload = payload
        self.rows = lax.broadcasted_iota(jnp.int32, s.shape, 0).astype(F32).astype(s.dtype)
        self.vals, self.picks = [], []

    def step(self, n):
        dt = self.s.dtype
        bound = jnp.asarray(self.s.shape[0], dt)
        for _ in range(n):
            m = jnp.max(self.s, axis=0, keepdims=True)
            first = jnp.min(jnp.where(self.s == m, self.rows, bound), axis=0, keepdims=True)
            sel = self.rows == first
            self.vals.append(m)
            if self.payload is None:
                self.picks.append(first)
            else:
                self.picks.append(jnp.max(jnp.where(sel, self.payload, -1.0), axis=0, keepdims=True))
            self.s = jnp.where(sel, jnp.asarray(-jnp.inf, dt), self.s)

    def result(self):
        return jnp.concatenate(self.vals, axis=0), jnp.concatenate(self.picks, axis=0)


_PAIR_GROUPS = (
    ((0, 1, 0, 8),),
    ((0, 2, 0, 5), (5, 3, 0, 3)),
    ((0, 3, 3, 1), (1, 4, 0, 3), (4, 5, 0, 2), (6, 6, 0, 2)),
    ((0, 7, 0, 2), (2, 8, 0, -6)),
    ((0, 14, 0, -2),),
)


def _pair_candidates(first, second):
    h8 = SUBLANES
    (v1, i1), (v2, i2) = [tuple(a.astype(F32) for a in lst) for lst in (first, second)]
    row = lax.broadcasted_iota(jnp.int32, (h8, v1.shape[1]), 0)

    def build(a, b, combine, filler):
        groups = [combine(a[0:1], b)]
        for segments in _PAIR_GROUPS:
            group = jnp.full((h8, a.shape[1]), filler, F32)
            for r0, k1, k2, count in segments:
                if count > 0:
                    seg = combine(a[k1:k1 + 1], pltpu.roll(b[0:h8], (r0 - k2) % h8, 0))
                else:
                    seg = combine(pltpu.roll(a[h8:2 * h8], (r0 - (k1 - h8)) % h8, 0), b[0:1])
                group = jnp.where((row >= r0) & (row < r0 + abs(count)), seg, group)
            groups.append(group)
        return jnp.concatenate(groups, axis=0)

    return (build(v1, v2, lambda a, b: a + b, -jnp.inf),
            build(i1, i2, lambda a, b: a * N_KEYS + b, 0.0))


def _peer_fused_kernel(xs_ref, xm_ref, g_ref, wq_ref, sk_ref, u_ref, v_ref, *rest, split_blk):
    if split_blk is None:
        o_ref, q_scr, sel_e, sel_g, xn_scr, a_scr, bgt_scr, bg_scr, gmat, coef_scr = rest
        out_refs = ()
    else:
        *out_refs, q_scr, sel_e, sel_g, xn_scr, a_scr, bgt_scr, bg_scr, gmat, coef_scr, o_ref = rest
    i = pl.program_id(0)
    e = pl.program_id(1)
    n_steps = pl.num_programs(1)
    t = xm_ref.shape[0]
    n_half = t // SEL_T
    half_keys = N_KEYS // 2
    cur = i % 2

    def project_queries():
        xn = _rmsnorm(xs_ref[...], g_ref[...]).astype(BF16)
        q = jnp.dot(xn, wq_ref[...], preferred_element_type=F32).astype(BF16)
        for hf in range(n_half):
            for c in range(2 * PEER_HEADS):
                q_scr[hf, c] = q[hf * SEL_T:(hf + 1) * SEL_T, c * LANES:(c + 1) * LANES]

    @pl.when(e == 0)
    def _():
        @pl.when(i == 0)
        def _():
            project_queries()
            sel_e[...] = jnp.zeros_like(sel_e)
            sel_g[...] = jnp.zeros_like(sel_g)

        x = xm_ref[...]
        xn_scr[...] = _rmsnorm(x, g_ref[...]).astype(BF16)
        o_ref[...] = x
        coef_scr[0] = jnp.zeros(coef_scr.shape[1:], BF16)
        prev = 1 - cur
        lane_groups = SEL_T // LANES
        for hf in range(n_half):
            ef = sel_e[prev, hf]
            af = jnp.floor(ef * (1.0 / N_KEYS))
            bf = ef - af * N_KEYS
            gf = sel_g[prev, hf]
            a_scr[hf * SEL_T:(hf + 1) * SEL_T, :] = af.T
            bgf = bf + RSQRT2 * gf
            bgt_scr[hf * SEL_T:(hf + 1) * SEL_T, :] = bgf.T
            for lg in range(lane_groups):
                bg_scr[hf * lane_groups + lg] = bgf[:, lg * LANES:(lg + 1) * LANES]
        r = lax.broadcasted_iota(jnp.int32, (N_KEYS, LANES), 0)
        packed_shape = (N_KEYS // (2 * SUBLANES), 2 * SUBLANES, LANES)
        a_of_row = jnp.where(r < half_keys, 2 * r, 2 * (r - half_keys) + 1).astype(F32).astype(BF16)
        a_of_row = a_of_row.reshape(packed_shape)
        one = jnp.ones(packed_shape, BF16)
        zero = jnp.zeros(packed_shape, BF16)
        b_of_lane = lax.broadcasted_iota(jnp.int32, (N_KEYS, LANES), 1).astype(F32)
        b_of_row = r.astype(F32).astype(BF16).reshape(packed_shape)
        per_body = 32
        bodies_per_group = LANES // per_body
        row_major_every = 3

        def body(it, carry):
            grp = lax.shift_right_logical(it, bodies_per_group.bit_length() - 1)
            sub = it & (bodies_per_group - 1)
            shift = (LANES - sub * per_body) & (LANES - 1)
            bg = pltpu.roll(bg_scr[grp], shift, 1)
            tok0 = pl.multiple_of(it * per_body, per_body)
            arows = a_scr[pl.ds(tok0, per_body), :]
            bgrows = bgt_scr[pl.ds(tok0, per_body), :]
            for k in range(per_body):
                arow = jnp.broadcast_to(arows[k:k + 1], (2 * SUBLANES, LANES)).astype(BF16)[None]
                pa = jnp.where(a_of_row == arow, one, zero).reshape(N_KEYS, LANES)
                if k % row_major_every == 0:
                    bgrow = jnp.broadcast_to(bgrows[k:k + 1], (2 * SUBLANES, LANES))
                    brow = jnp.floor(bgrow)
                    grow = (bgrow - brow).astype(BF16)[None]
                    qb = jnp.where(b_of_row == brow.astype(BF16)[None], grow, zero).reshape(N_KEYS, LANES)
                    tile = lax.dot_general(pa, qb, NT_DIMS, preferred_element_type=F32)
                else:
                    bgcol = jnp.broadcast_to(bg[:, k:k + 1], (N_KEYS, LANES))
                    bcol = jnp.floor(bgcol)
                    gcol = bgcol - bcol
                    qbt = jnp.where(bcol == b_of_lane, gcol, 0.0).astype(BF16)
                    tile = jnp.dot(pa, qbt, preferred_element_type=F32)
                row0 = pl.multiple_of((tok0 + k) * G_PITCH, SUBLANES)
                gmat[pl.ds(row0, half_keys), :] = pltpu.pack_elementwise(
                    [tile[0:half_keys], tile[half_keys:N_KEYS]], packed_dtype=BF16)
            return carry

        lax.fori_loop(0, t // per_body, body, 0)

    unit = jnp.minimum(e, n_half * PEER_HEADS - 1)
    half = lax.shift_right_logical(unit, PEER_HEADS.bit_length() - 1)
    h = unit & (PEER_HEADS - 1)
    rd = e & 1
    blk = jnp.minimum(e, n_steps - 2)
    eb = u_ref.shape[0]
    n_stages = 4
    exp_w = eb // n_stages
    out_w = v_ref.shape[1] // n_stages
    words_per_stage = exp_w // (2 * N_KEYS)

    def value_piece(c):
        ocols = slice(c * out_w, (c + 1) * out_w)
        o_ref[:, ocols] += jnp.dot(coef_scr[rd], v_ref[:, ocols], preferred_element_type=F32)

    def act_piece(c):
        ecols = slice(c * exp_w, (c + 1) * exp_w)
        act = lax.dot_general(xn_scr[...], u_ref[ecols, :], NT_DIMS, preferred_element_type=F32)
        gates = []
        for w in range(words_per_stage):
            word = gmat[pl.ds((blk * n_stages + c) * words_per_stage + w, t, stride=G_PITCH), :]
            gates.append(lax.bitcast_convert_type(word << 16, F32))
            gates.append(lax.bitcast_convert_type(word & jnp.int32(-65536), F32))
        coef_scr[1 - rd, :, ecols] = (_scaled_gelu(act) * jnp.concatenate(gates, axis=1)).astype(BF16)

    def retrieval():
        lists = []
        for p in range(2):
            st = lax.dot_general(sk_ref[h, p], q_scr[half, 2 * h + p], NT_DIMS, preferred_element_type=F32)
            yield from _top16_of_keys(st, lists)
            yield
        cand, expert = _pair_candidates(lists[0], lists[1])
        second_level = _TopK(cand, payload=expert)
        for _ in range(4):
            second_level.step(PEER_TOPK // 4)
            yield
        best, e_sel = second_level.result()
        ex = jnp.exp(best - best[0:1])
        slot0 = pl.multiple_of(h * PEER_TOPK, PEER_TOPK)
        sel_e[cur, half, pl.ds(slot0, PEER_TOPK), :] = e_sel
        sel_g[cur, half, pl.ds(slot0, PEER_TOPK), :] = ex / jnp.sum(ex, axis=0, keepdims=True)

    pieces = [functools.partial(f, c) for c in range(n_stages) for f in (value_piece, act_piece)]
    parts = retrieval()
    parts_per_piece = 2
    for piece in pieces:
        piece()
        for _ in range(parts_per_piece):
            next(parts, None)
    for _ in parts:
        pass

    @pl.when(e == n_steps - 1)
    def _():
        project_queries()
        if out_refs:
            first_ref, second_ref = out_refs

            @pl.when(i - 1 < split_blk)
            def _():
                first_ref[...] = o_ref[...]

            @pl.when(i - 1 >= split_blk)
            def _():
                second_ref[...] = o_ref[...]


def _peer(x, g, wq, sk, u, v, t=512, eb=1024, split_rows=None):
    n, d = x.shape
    nslot = PEER_HEADS * PEER_TOPK
    n_tok_blk = n // t
    n_exp_blk = N_KEYS * N_KEYS // eb
    n_half = t // SEL_T
    assert n_exp_blk == n_half * PEER_HEADS
    last_tok = n_tok_blk - 1
    once = pl.Buffered(1)
    if split_rows is None:
        split_blk = None
        out_specs = pl.BlockSpec((t, d), lambda i, e: (jnp.maximum(i - 1, 0), 0))
        out_shape = jax.ShapeDtypeStruct((n, d), F32)
        acc = []
    else:
        split_blk = split_rows // t
        out_specs = [pl.BlockSpec((t, d), lambda i, e: (jnp.clip(i - 1, 0, split_blk - 1), 0)),
                     pl.BlockSpec((t, d), lambda i, e: (jnp.clip(i - 1 - split_blk, 0, last_tok - split_blk), 0))]
        out_shape = [jax.ShapeDtypeStruct((split_rows, d), F32), jax.ShapeDtypeStruct((n - split_rows, d), F32)]
        acc = [pltpu.VMEM((t, d), F32)]
    return pl.pallas_call(
        functools.partial(_peer_fused_kernel, split_blk=split_blk),
        grid=(n_tok_blk + 1, n_exp_blk + 1),
        in_specs=[pl.BlockSpec((t, d), lambda i, e: (jnp.minimum(i + e // n_exp_blk, last_tok), 0),
                               pipeline_mode=once),
                  pl.BlockSpec((t, d), lambda i, e: (jnp.maximum(i - 1, 0), 0)),
                  pl.BlockSpec((1, d), lambda i, e: (0, 0)),
                  pl.BlockSpec((d, 2 * PEER_HEADS * LANES), lambda i, e: (0, 0), pipeline_mode=once),
                  pl.BlockSpec((PEER_HEADS, 2, N_KEYS, LANES), lambda i, e: (0, 0, 0, 0), pipeline_mode=once),
                  pl.BlockSpec((eb, d), lambda i, e: (jnp.minimum(e, n_exp_blk - 1), 0)),
                  pl.BlockSpec((eb, d), lambda i, e: (jnp.maximum(e - 1, 0), 0))],
        out_specs=out_specs,
        out_shape=out_shape,
        scratch_shapes=[pltpu.VMEM((n_half, 2 * PEER_HEADS, SEL_T, LANES), BF16),
                        pltpu.VMEM((2, n_half, nslot, SEL_T), F32),
                        pltpu.VMEM((2, n_half, nslot, SEL_T), F32),
                        pltpu.VMEM((t, d), BF16),
                        pltpu.VMEM((t, nslot), F32),
                        pltpu.VMEM((t, nslot), F32),
                        pltpu.VMEM((t // LANES, nslot, LANES), F32),
                        pltpu.VMEM((t * G_PITCH, LANES), jnp.int32),
                        pltpu.VMEM((2, t, eb), BF16)] + acc,
        compiler_params=_cparams(2),
        name="peer",
    )(x, x, g.reshape(1, d), wq, sk, u, v)


def _rope_tables(pos):
    half = HEAD_DIM // 2
    inv_freq = ROPE_THETA ** (-jnp.arange(half, dtype=F32) / half)
    ang = pos.astype(F32)[:, None] * inv_freq[None, :]
    cos, sin = jnp.cos(ang), jnp.sin(ang)
    reps = LANES // HEAD_DIM
    cos_t = jnp.tile(jnp.concatenate([cos, cos], axis=1), (1, reps))
    sin_t = jnp.tile(jnp.concatenate([-sin, sin], axis=1), (1, reps))
    return cos_t, sin_t


def kernel(x_prompt, x_sample, cache_conv, cache_win_k, cache_win_v, state_mlstm_C, state_mlstm_n,
           state_mlstm_m, norm_mix, norm_ffn, ab_w_in, ab_conv_w, ab_q_gain, ab_k_gain, ab_sinks, ab_w_out,
           ml_w_in, ml_gate_bias, ml_out_gain, ml_w_out, peer_w_q, peer_sub_keys, peer_u, peer_v):
    n_batch, seq, d = x_prompt.shape
    n_seq, t_len, _ = x_sample.shape
    n_prompt = n_batch * seq
    assert d == D_MODEL and t_len == SUBLANES and norm_mix.shape[0] == 2

    xp = x_prompt.reshape(n_prompt, d)
    xs = x_sample.reshape(n_seq * t_len, d)

    cos_p, sin_p = _rope_tables(jnp.arange(seq, dtype=jnp.int32))
    cos_s, sin_s = _rope_tables(PAST_LEN + jnp.arange(t_len, dtype=jnp.int32))
    bt = 16
    cos_s, sin_s = jnp.tile(cos_s, (bt, 1)), jnp.tile(sin_s, (bt, 1))
    lane = jnp.arange(LANES)
    seg = (lane[:, None] // HEAD_DIM == lane[None, :] // HEAD_DIM).astype(BF16)
    reps = LANES // HEAD_DIM
    qg = jnp.tile(ab_q_gain[0], reps).reshape(1, LANES)
    kg = jnp.tile(ab_k_gain[0], reps).reshape(1, LANES)
    wo_ab = ab_w_out[0].astype(BF16)

    z = _norm_proj([xp, xs], norm_mix[0], ab_w_in[0].astype(BF16))
    y_p, k_p, v_p, c_p = _ab_prompt(z, xp, n_batch, seq, cos_p, sin_p, ab_conv_w[0], qg, kg, ab_sinks[0],
                                    seg, wo_ab)
    y_s, c_s, k_s, v_s = _ab_sample(z, xs, n_prompt, n_seq, t_len, cos_s, sin_s, ab_conv_w[0], qg, kg,
                                    ab_sinks[0], seg, wo_ab, cache_conv[0],
                                    cache_win_k[0].reshape(n_seq, WINDOW, LANES),
                                    cache_win_v[0].reshape(n_seq, WINDOW, LANES), bt=bt)
    x = jnp.concatenate([y_p.reshape(n_prompt, d), y_s], axis=0)
    x = _peer(x, norm_ffn[0], peer_w_q[0].astype(BF16), peer_sub_keys[0].astype(BF16),
              (peer_u[0] * RSQRT2).astype(BF16), peer_v[0].astype(BF16))

    n_gate = 2 * ML_HEADS
    w_in = jnp.pad(ml_w_in[0], ((0, 0), (0, ML_IN_PAD - ml_w_in.shape[2]))).astype(BF16)
    bias = jnp.pad(ml_gate_bias[0], (0, LANES - n_gate)).reshape(1, LANES)
    og = ml_out_gain[0].reshape(1, D_MODEL)
    idx = jnp.arange(ML_CHUNK)
    tril = (idx[None, :] <= idx[:, None]).astype(BF16)
    wo_ml = ml_w_out[0].astype(BF16)

    z = _norm_proj([x], norm_mix[1], w_in)
    y_p, cm_p, nm_p, mm_p = _mlstm_prompt(z, x, n_batch, seq, bias, og, tril, wo_ml)
    y_s, cm_s, nm_s, mm_s = _mlstm_sample(z, x, n_prompt, n_seq, t_len, bias, og, tril, wo_ml,
                                          state_mlstm_C[0], state_mlstm_n[0],
                                          state_mlstm_m[0].reshape(n_seq, 1, ML_HEADS))
    x = jnp.concatenate([y_p.reshape(n_prompt, d), y_s], axis=0)
    out_p, out_s = _peer(x, norm_ffn[1], peer_w_q[1].astype(BF16), peer_sub_keys[1].astype(BF16),
                         (peer_u[1] * RSQRT2).astype(BF16), peer_v[1].astype(BF16), split_rows=n_prompt)

    y_prompt = out_p.reshape(n_batch, seq, d)
    y_sample = out_s.reshape(n_seq, t_len, d)
    kv_shape_p = (1, n_batch, WINDOW, N_KV_HEADS, HEAD_DIM)
    kv_shape_s = (1, n_seq, WINDOW, N_KV_HEADS, HEAD_DIM)
    return (y_prompt, y_sample,
            c_p[:, SUBLANES - 2:, :][None], k_p.reshape(kv_shape_p), v_p.reshape(kv_shape_p),
            cm_p[None], nm_p[None], mm_p.reshape(1, n_batch, ML_HEADS),
            c_s[None], k_s.reshape(kv_shape_s), v_s.reshape(kv_shape_s),
            cm_s[None], nm_s[None], mm_s.reshape(1, n_seq, ML_HEADS))
```

```python
import functools

import jax
import jax.numpy as jnp
from jax import lax
from jax.experimental import pallas as pl
from jax.experimental.pallas import tpu as pltpu

F32 = jnp.float32
BF16 = jnp.bfloat16
EPS = 1e-6

D_MODEL = 1024
CONV_DIM = 512
N_Q_HEADS = 8
N_KV_HEADS = 2
HEAD_DIM = 64
WINDOW = 128
ROPE_THETA = 10000.0
AB_IN = 2304
ML_HEADS = 4
ML_QK = 128
ML_V = 256
ML_CHUNK = 128
ML_GATE_COL = 3072
ML_IN_PAD = ML_GATE_COL + 128
N_KEYS = 128
PEER_HEADS = 8
PEER_TOPK = 16
PAST_LEN = 16384

LANES = 128
SUBLANES = 8
G_PITCH = N_KEYS // 2 + SUBLANES
VMEM_LIMIT = 56 * 1024 * 1024

NT_DIMS = (((1,), (1,)), ((), ()))
TN_DIMS = (((0,), (0,)), ((), ()))


def _cparams(n_axes, vmem=VMEM_LIMIT):
    return pltpu.CompilerParams(dimension_semantics=("arbitrary",) * n_axes, vmem_limit_bytes=vmem)


def _rmsnorm(x, g):
    return x * lax.rsqrt(jnp.mean(x * x, axis=-1, keepdims=True) + EPS) * g


def _bf16_pieces(a, terms):
    pieces = []
    rem = a
    for _ in range(terms):
        piece = rem.astype(BF16)
        rem = rem - piece.astype(F32)
        pieces.append(piece)
    return pieces


def _split_dot(a, b_bf16, terms=2):
    return sum(jnp.dot(p, b_bf16, preferred_element_type=F32) for p in _bf16_pieces(a, terms))


def _norm_proj_kernel(*refs, starts):
    x_refs = refs[:len(starts)]
    g_ref, w_ref, o_ref = refs[len(starts):]
    i = pl.program_id(0)
    x = x_refs[0][...]
    for x_ref, start in zip(x_refs[1:], starts[1:]):
        x = jnp.where(i >= start, x_ref[...], x)
    r = _rmsnorm(x, g_ref[...])
    o_ref[...] = jnp.dot(r.astype(BF16), w_ref[...], preferred_element_type=F32)


def _norm_proj(x_parts, g, w_bf16, tm=512):
    d = x_parts[0].shape[1]
    nout = w_bf16.shape[1]
    blocks = [p.shape[0] // tm for p in x_parts]
    starts = tuple(sum(blocks[:k]) for k in range(len(blocks)))
    part_spec = lambda start, nblk: pl.BlockSpec((tm, d), lambda i: (jnp.clip(i - start, 0, nblk - 1), 0))
    return pl.pallas_call(
        functools.partial(_norm_proj_kernel, starts=starts),
        grid=(sum(blocks),),
        in_specs=[part_spec(s, nb) for s, nb in zip(starts, blocks)] + [
            pl.BlockSpec((1, d), lambda i: (0, 0)),
            pl.BlockSpec((d, nout), lambda i: (0, 0))],
        out_specs=pl.BlockSpec((tm, nout), lambda i: (i, 0)),
        out_shape=jax.ShapeDtypeStruct((sum(blocks) * tm, nout), F32),
        compiler_params=_cparams(1),
        name="norm_proj",
    )(*x_parts, g.reshape(1, d), w_bf16)


def _headnorm_rope(xc, gain, cos, sin, seg, hi_half):
    ss = _split_dot(xc * xc, seg)
    xn = xc * lax.rsqrt(ss * (1.0 / HEAD_DIM) + EPS) * gain
    partner = jnp.where(hi_half, pltpu.roll(xn, 32, 1), pltpu.roll(xn, 96, 1))
    return xn * cos + partner * sin


def _softmax_sink(s, mask, sink):
    s = jnp.where(mask, s, -1e30)
    m = jnp.maximum(jnp.max(s, axis=-1, keepdims=True), sink)
    p = jnp.exp(s - m)
    denom = jnp.sum(p, axis=-1, keepdims=True) + jnp.exp(sink - m)
    return (p / denom).astype(BF16)


def _ab_prompt_kernel(sink_ref, *refs, n_batch):
    z_refs, x_refs = refs[0:n_batch], refs[n_batch:2 * n_batch]
    (cos_ref, sin_ref, cw_ref, qg_ref, kg_ref, seg_ref, wo_ref,
     y_ref, kst_ref, vst_ref, cst_ref, pk_ref, pv_ref, pu_ref) = refs[2 * n_batch:]
    j = pl.program_id(0)

    @pl.when(j == 0)
    def _():
        pk_ref[...] = jnp.zeros_like(pk_ref)
        pv_ref[...] = jnp.zeros_like(pv_ref)
        pu_ref[...] = jnp.zeros_like(pu_ref)

    for b in range(n_batch):
        _ab_prompt_block(j, sink_ref, z_refs[b], x_refs[b], cos_ref, sin_ref, cw_ref, qg_ref, kg_ref, seg_ref,
                         wo_ref, y_ref.at[b], kst_ref.at[b], vst_ref.at[b], cst_ref.at[b],
                         pk_ref.at[b], pv_ref.at[b], pu_ref.at[b])


def _ab_prompt_block(j, sink_ref, z_ref, x_ref, cos_ref, sin_ref, cw_ref, qg_ref, kg_ref, seg_ref, wo_ref,
                     y_ref, kst_ref, vst_ref, cst_ref, pk_ref, pv_ref, pu_ref):
    blk = z_ref.shape[0]
    cos = cos_ref[...]
    sin = sin_ref[...]
    seg = seg_ref[...]
    hi_half = (lax.broadcasted_iota(jnp.int32, (blk, LANES), 1) & 32) != 0

    gate_b = z_ref[:, 0:CONV_DIM]
    u = z_ref[:, CONV_DIM:2 * CONV_DIM] * z_ref[:, 2 * CONV_DIM:3 * CONV_DIM]
    ng = blk // SUBLANES
    u3 = u.reshape(ng, SUBLANES, CONV_DIM)
    ext = jnp.concatenate([pu_ref[...][None], u3], axis=0)
    t8 = lax.broadcasted_iota(jnp.int32, (ng, SUBLANES, CONV_DIM), 1)
    r1 = pltpu.roll(ext, 1, 1)
    r2 = pltpu.roll(ext, 2, 1)
    um1 = jnp.where(t8 >= 1, r1[1:], r1[:-1])
    um2 = jnp.where(t8 >= 2, r2[1:], r2[:-1])
    cw = cw_ref[...]
    conv = cw[0:1][None] * um2 + cw[1:2][None] * um1 + cw[2:3][None] * u3
    yconv = gate_b * conv.reshape(blk, CONV_DIM)

    q0 = 3 * CONV_DIM
    k0 = q0 + N_Q_HEADS * HEAD_DIM
    v0 = k0 + N_KV_HEADS * HEAD_DIM
    qg = qg_ref[...]
    qr = [_headnorm_rope(z_ref[:, q0 + c * LANES:q0 + (c + 1) * LANES], qg, cos, sin, seg, hi_half)
          for c in range(N_Q_HEADS * HEAD_DIM // LANES)]
    kr = _headnorm_rope(z_ref[:, k0:k0 + LANES], kg_ref[...], cos, sin, seg, hi_half)
    v = z_ref[:, v0:v0 + LANES]
    pk = pk_ref[...]
    pv = pv_ref[...]

    row = lax.broadcasted_iota(jnp.int32, (blk, 2 * blk), 0)
    col = lax.broadcasted_iota(jnp.int32, (blk, 2 * blk), 1)
    row_prev = row + jnp.where(j == 0, 2 * blk, 0)
    mask = ((col < blk) & (col > row_prev)) | ((col >= blk) & (col - blk <= row))

    kks, vvs = [], []
    for g in range(N_KV_HEADS):
        sl = slice(g * HEAD_DIM, (g + 1) * HEAD_DIM)
        kks.append(jnp.concatenate([pk[:, sl], kr[:, sl]], axis=0).astype(BF16))
        vvs.append(jnp.concatenate([pv[:, sl], v[:, sl]], axis=0).astype(BF16))
    outs = []
    for h in range(N_Q_HEADS):
        g = h // (N_Q_HEADS // N_KV_HEADS)
        qh = qr[h // 2][:, (h % 2) * HEAD_DIM:(h % 2 + 1) * HEAD_DIM].astype(BF16)
        s = lax.dot_general(qh, kks[g], NT_DIMS, preferred_element_type=F32) * (HEAD_DIM ** -0.5)
        p = _softmax_sink(s, mask, sink_ref[h])
        outs.append(jnp.dot(p, vvs[g], preferred_element_type=F32))
    attn = jnp.concatenate(outs, axis=1)

    y = (jnp.dot(yconv.astype(BF16), wo_ref[0:CONV_DIM, :], preferred_element_type=F32)
         + jnp.dot(attn.astype(BF16), wo_ref[CONV_DIM:2 * CONV_DIM, :], preferred_element_type=F32))
    y_ref[...] = y + x_ref[...]

    pk_ref[...] = kr
    pv_ref[...] = v
    pu_ref[...] = u3[ng - 1]
    kst_ref[...] = kr
    vst_ref[...] = v
    cst_ref[...] = u3[ng - 1]


def _ab_prompt(z, x, n_batch, seq, cos, sin, cw, qg, kg, sinks, seg, wo):
    blk = WINDOW
    nb = seq // blk
    const = lambda j, s: (0, 0)
    whole = lambda j, s: (0, 0, 0)
    tok_specs = lambda width: [pl.BlockSpec((blk, width), functools.partial(lambda j, s, b: (b * nb + j, 0), b=b))
                               for b in range(n_batch)]
    grid_spec = pltpu.PrefetchScalarGridSpec(
        num_scalar_prefetch=1,
        grid=(nb,),
        in_specs=tok_specs(AB_IN) + tok_specs(D_MODEL) + [
            pl.BlockSpec((blk, LANES), lambda j, s: (j, 0)),
            pl.BlockSpec((blk, LANES), lambda j, s: (j, 0)),
            pl.BlockSpec((3, CONV_DIM), const),
            pl.BlockSpec((1, LANES), const),
            pl.BlockSpec((1, LANES), const),
            pl.BlockSpec((LANES, LANES), const),
            pl.BlockSpec((D_MODEL, D_MODEL), const)],
        out_specs=[pl.BlockSpec((n_batch, blk, D_MODEL), lambda j, s: (0, j, 0)),
                   pl.BlockSpec((n_batch, blk, LANES), whole),
                   pl.BlockSpec((n_batch, blk, LANES), whole),
                   pl.BlockSpec((n_batch, SUBLANES, CONV_DIM), whole)],
        scratch_shapes=[pltpu.VMEM((n_batch, blk, LANES), F32), pltpu.VMEM((n_batch, blk, LANES), F32),
                        pltpu.VMEM((n_batch, SUBLANES, CONV_DIM), F32)])
    return pl.pallas_call(
        functools.partial(_ab_prompt_kernel, n_batch=n_batch),
        grid_spec=grid_spec,
        out_shape=[jax.ShapeDtypeStruct((n_batch, seq, D_MODEL), F32),
                   jax.ShapeDtypeStruct((n_batch, blk, LANES), F32),
                   jax.ShapeDtypeStruct((n_batch, blk, LANES), F32),
                   jax.ShapeDtypeStruct((n_batch, SUBLANES, CONV_DIM), F32)],
        compiler_params=_cparams(1),
        name="ab_prompt",
    )(sinks, *([z] * n_batch), *([x] * n_batch), cos, sin, cw, qg, kg, seg, wo)


def _ab_sample_kernel(sink_ref, z_ref, x_ref, cos_ref, sin_ref, cw_ref, qg_ref, kg_ref, seg_ref, wo_ref,
                      cc_ref, ck_ref, cv_ref, y_ref, cs_ref, ks_ref, vs_ref):
    rows = z_ref.shape[0]
    t_len = SUBLANES
    bt = rows // t_len
    cos = cos_ref[...]
    sin = sin_ref[...]
    seg = seg_ref[...]
    hi_half = (lax.broadcasted_iota(jnp.int32, (rows, LANES), 1) & 32) != 0

    gate_b = z_ref[:, 0:CONV_DIM]
    u = z_ref[:, CONV_DIM:2 * CONV_DIM] * z_ref[:, 2 * CONV_DIM:3 * CONV_DIM]
    u3 = u.reshape(bt, t_len, CONV_DIM)
    cc = cc_ref[...]
    c0 = cc[:, 0:1, :]
    c1 = cc[:, 1:2, :]
    t8 = lax.broadcasted_iota(jnp.int32, (bt, t_len, CONV_DIM), 1)
    r1 = pltpu.roll(u3, 1, 1)
    r2 = pltpu.roll(u3, 2, 1)
    um1 = jnp.where(t8 >= 1, r1, c1)
    um2 = jnp.where(t8 >= 2, r2, jnp.where(t8 == 1, c1, c0))
    cw = cw_ref[...]
    conv = cw[0:1][None] * um2 + cw[1:2][None] * um1 + cw[2:3][None] * u3
    yconv = gate_b * conv.reshape(rows, CONV_DIM)
    cs_ref[...] = r2[:, 0:2, :]

    q0 = 3 * CONV_DIM
    k0 = q0 + N_Q_HEADS * HEAD_DIM
    v0 = k0 + N_KV_HEADS * HEAD_DIM
    qg = qg_ref[...]
    qr = [_headnorm_rope(z_ref[:, q0 + c * LANES:q0 + (c + 1) * LANES], qg, cos, sin, seg, hi_half)
          for c in range(N_Q_HEADS * HEAD_DIM // LANES)]
    kr = _headnorm_rope(z_ref[:, k0:k0 + LANES], kg_ref[...], cos, sin, seg, hi_half)
    v = z_ref[:, v0:v0 + LANES]

    group = N_Q_HEADS // N_KV_HEADS
    nq = group * t_len
    nk = 2 * WINDOW
    qrow = lax.broadcasted_iota(jnp.int32, (nq, nk), 0)
    t_q = qrow & (t_len - 1)
    col = lax.broadcasted_iota(jnp.int32, (nq, nk), 1)
    mask = (((col < WINDOW) & (col > t_q)) | ((col >= WINDOW) & (col - WINDOW <= t_q)))[None]
    hrow = lax.broadcasted_iota(jnp.int32, (nq, 1), 0) // t_len
    pad = jnp.zeros((bt, nk - WINDOW - t_len, HEAD_DIM), F32)

    outs = [None] * N_Q_HEADS
    for g in range(N_KV_HEADS):
        sl = slice(g * HEAD_DIM, (g + 1) * HEAD_DIM)
        qs = jnp.concatenate(
            [qr[h // 2][:, (h % 2) * HEAD_DIM:(h % 2 + 1) * HEAD_DIM].reshape(bt, t_len, HEAD_DIM)
             for h in range(g * group, (g + 1) * group)], axis=1)
        kk = jnp.concatenate([ck_ref[:, :, sl], kr[:, sl].reshape(bt, t_len, HEAD_DIM), pad], axis=1)
        vv = jnp.concatenate([cv_ref[:, :, sl], v[:, sl].reshape(bt, t_len, HEAD_DIM), pad], axis=1)
        s = jnp.einsum('bqd,bkd->bqk', qs.astype(BF16), kk.astype(BF16),
                       preferred_element_type=F32) * (HEAD_DIM ** -0.5)
        sink = jnp.zeros((nq, 1), F32)
        for hh in range(group):
            sink = jnp.where(hrow == hh, sink_ref[g * group + hh], sink)
        p = _softmax_sink(s, mask, sink[None])
        o = jnp.einsum('bqk,bkd->bqd', p, vv.astype(BF16), preferred_element_type=F32)
        for hh in range(group):
            outs[g * group + hh] = o[:, hh * t_len:(hh + 1) * t_len, :].reshape(rows, HEAD_DIM)
    attn = jnp.concatenate(outs, axis=1)

    y = (jnp.dot(yconv.astype(BF16), wo_ref[0:CONV_DIM, :], preferred_element_type=F32)
         + jnp.dot(attn.astype(BF16), wo_ref[CONV_DIM:2 * CONV_DIM, :], preferred_element_type=F32))
    y_ref[...] = y + x_ref[...]

    keep = WINDOW - t_len
    ks_ref[:, 0:keep, :] = ck_ref[:, t_len:WINDOW, :]
    ks_ref[:, keep:WINDOW, :] = kr.reshape(bt, t_len, LANES)
    vs_ref[:, 0:keep, :] = cv_ref[:, t_len:WINDOW, :]
    vs_ref[:, keep:WINDOW, :] = v.reshape(bt, t_len, LANES)


def _ab_sample(z, x, row0, n_seq, t_len, cos, sin, cw, qg, kg, sinks, seg, wo, cc, ck, cv, bt=16):
    rows = bt * t_len
    blk0 = row0 // rows
    tok = lambda i, s: (blk0 + i, 0)
    const = lambda i, s: (0, 0)
    seq3 = lambda i, s: (i, 0, 0)
    grid_spec = pltpu.PrefetchScalarGridSpec(
        num_scalar_prefetch=1,
        grid=(n_seq // bt,),
        in_specs=[pl.BlockSpec((rows, AB_IN), tok),
                  pl.BlockSpec((rows, D_MODEL), lambda i, s: (i, 0)),
                  pl.BlockSpec((rows, LANES), const),
                  pl.BlockSpec((rows, LANES), const),
                  pl.BlockSpec((3, CONV_DIM), const),
                  pl.BlockSpec((1, LANES), const),
                  pl.BlockSpec((1, LANES), const),
                  pl.BlockSpec((LANES, LANES), const),
                  pl.BlockSpec((D_MODEL, D_MODEL), const),
                  pl.BlockSpec((bt, 2, CONV_DIM), seq3),
                  pl.BlockSpec((bt, WINDOW, LANES), seq3),
                  pl.BlockSpec((bt, WINDOW, LANES), seq3)],
        out_specs=[pl.BlockSpec((rows, D_MODEL), lambda i, s: (i, 0)),
                   pl.BlockSpec((bt, 2, CONV_DIM), seq3),
                   pl.BlockSpec((bt, WINDOW, LANES), seq3),
                   pl.BlockSpec((bt, WINDOW, LANES), seq3)])
    return pl.pallas_call(
        _ab_sample_kernel,
        grid_spec=grid_spec,
        out_shape=[jax.ShapeDtypeStruct((n_seq * t_len, D_MODEL), F32),
                   jax.ShapeDtypeStruct((n_seq, 2, CONV_DIM), F32),
                   jax.ShapeDtypeStruct((n_seq, WINDOW, LANES), F32),
                   jax.ShapeDtypeStruct((n_seq, WINDOW, LANES), F32)],
        compiler_params=_cparams(1),
        name="ab_sample",
    )(sinks, z, x, cos, sin, cw, qg, kg, seg, wo, cc, ck, cv)


def _log_sigmoid(x):
    return jnp.minimum(x, 0.0) - jnp.log(1.0 + jnp.exp(-jnp.abs(x)))


def _mlstm_chunk(z, bias, og, tril, c_src, n_src, m_src, c_dst, n_dst, m_dst, n_real, rows):
    L = z.shape[0]
    gates = z[:, ML_GATE_COL:ML_GATE_COL + LANES] + bias
    if n_real < L:
        live = lax.broadcasted_iota(jnp.int32, (L, LANES), 0) < n_real
        li_all = jnp.where(live, gates, -1e30)
        lf_all = jnp.where(live, _log_sigmoid(gates), 0.0)
    else:
        li_all = gates
        lf_all = _log_sigmoid(gates)
    lf_pieces = _bf16_pieces(lf_all, 3)
    f_col_all = sum(jnp.dot(tril[0:rows], p, preferred_element_type=F32) for p in lf_pieces)
    f_row_all = sum(lax.dot_general(p, tril, (((0,), (1,)), ((), ())), preferred_element_type=F32)
                    for p in lf_pieces)
    li_t = li_all.T
    rr = lax.broadcasted_iota(jnp.int32, (rows, L), 0)
    cc = lax.broadcasted_iota(jnp.int32, (rows, L), 1)
    causal = cc <= rr

    outs, m_new_all = [], []
    for h in range(ML_HEADS):
        f_col = f_col_all[:, ML_HEADS + h:ML_HEADS + h + 1]
        f_row = f_row_all[ML_HEADS + h:ML_HEADS + h + 1, :]
        li_row = li_t[h:h + 1, :]
        li_col = li_all[0:rows, h:h + 1]
        m0 = m_src[0:1, h:h + 1]
        c0 = c_src[h]
        n0 = n_src[h:h + 1, :]
        qh = z[0:rows, h * ML_QK:(h + 1) * ML_QK]
        kh = z[:, ML_HEADS * ML_QK + h * ML_QK:ML_HEADS * ML_QK + (h + 1) * ML_QK] * (ML_QK ** -0.5)
        v_off = 2 * ML_HEADS * ML_QK
        vh = z[:, v_off + h * ML_V:v_off + (h + 1) * ML_V]
        o_off = v_off + ML_HEADS * ML_V
        oh = z[0:rows, o_off + h * ML_V:o_off + (h + 1) * ML_V]
        qb = qh.astype(BF16)
        vb = vh.astype(BF16)

        dmat = jnp.where(causal, f_col - f_row + li_row, -jnp.inf)
        gcar = f_col + m0
        m_t = jnp.maximum(jnp.max(dmat, axis=-1, keepdims=True), gcar)
        w = jnp.exp(dmat - m_t)
        s = lax.dot_general(qb, kh.astype(BF16), NT_DIMS, preferred_element_type=F32) * w
        carry = jnp.exp(gcar - m_t)
        num = (jnp.dot(s.astype(BF16), vb, preferred_element_type=F32)
               + jnp.dot(qb, c0.astype(BF16), preferred_element_type=F32) * carry)
        den = jnp.sum(s, axis=-1, keepdims=True) + carry * jnp.sum(qh * n0, axis=-1, keepdims=True)
        hout = num / jnp.maximum(jnp.abs(den), jnp.exp(-m_t))

        f_last = f_col[n_real - 1:n_real, :]
        w_end = f_last - f_col + li_col
        m_new = jnp.maximum(f_last + m0, jnp.max(w_end, axis=0, keepdims=True))
        a_end = jnp.exp(w_end - m_new)
        scale = jnp.exp(f_last + m0 - m_new)
        ka = kh[0:rows] * a_end
        c_dst[h] = scale * c0 + lax.dot_general(ka.astype(BF16), vb[0:rows], TN_DIMS,
                                                preferred_element_type=F32)
        n_dst[h:h + 1, :] = scale * n0 + jnp.sum(ka, axis=0, keepdims=True)
        m_new_all.append(m_new)

        hn = _rmsnorm(hout, og[:, h * ML_V:(h + 1) * ML_V])
        outs.append(jax.nn.sigmoid(oh) * hn)
    m_dst[...] = jnp.concatenate(m_new_all, axis=1)
    return jnp.concatenate(outs, axis=1)


def _mlstm_prompt_kernel(*refs, n_batch):
    z_refs, x_refs = refs[0:n_batch], refs[n_batch:2 * n_batch]
    bias_ref, og_ref, tril_ref, wo_ref, y_ref, c_ref, n_ref, m_ref = refs[2 * n_batch:]

    @pl.when(pl.program_id(0) == 0)
    def _():
        c_ref[...] = jnp.zeros_like(c_ref)
        n_ref[...] = jnp.zeros_like(n_ref)
        m_ref[...] = jnp.zeros_like(m_ref)

    for b in range(n_batch):
        state = (c_ref.at[b], n_ref.at[b], m_ref.at[b])
        out = _mlstm_chunk(z_refs[b][...], bias_ref[...], og_ref[...], tril_ref[...], *state, *state,
                           ML_CHUNK, ML_CHUNK)
        y_ref[b] = jnp.dot(out.astype(BF16), wo_ref[...], preferred_element_type=F32) + x_refs[b][...]


def _mlstm_sample_kernel(z_ref, x_ref, bias_ref, og_ref, tril_ref, wo_ref, c0_ref, n0_ref, m0_ref,
                         y_ref, c_ref, n_ref, m_ref, *, t_len):
    n_here = z_ref.shape[0] // t_len
    q_rows = 2 * SUBLANES
    outs = []
    for s in range(n_here):
        zpad = jnp.concatenate([z_ref[s * t_len:(s + 1) * t_len, :],
                                jnp.zeros((ML_CHUNK - t_len, ML_IN_PAD), F32)], axis=0)
        out = _mlstm_chunk(zpad, bias_ref[...], og_ref[...], tril_ref[...],
                           c0_ref.at[s], n0_ref.at[s], m0_ref.at[s], c_ref.at[s], n_ref.at[s], m_ref.at[s],
                           t_len, q_rows)
        outs.append(out[0:t_len])
    out_all = jnp.concatenate(outs, axis=0).astype(BF16)
    y_ref[...] = jnp.dot(out_all, wo_ref[...], preferred_element_type=F32) + x_ref[...]


def _mlstm_weight_specs(const):
    return [pl.BlockSpec((1, LANES), const),
            pl.BlockSpec((1, D_MODEL), const),
            pl.BlockSpec((ML_CHUNK, ML_CHUNK), const),
            pl.BlockSpec((D_MODEL, D_MODEL), const)]


def _mlstm_prompt(z, x, n_batch, seq, bias, og, tril, wo):
    nc = seq // ML_CHUNK
    const = lambda j: (0, 0)
    tok_specs = lambda width: [pl.BlockSpec((ML_CHUNK, width), functools.partial(lambda j, b: (b * nc + j, 0), b=b))
                               for b in range(n_batch)]
    return pl.pallas_call(
        functools.partial(_mlstm_prompt_kernel, n_batch=n_batch),
        grid=(nc,),
        in_specs=tok_specs(ML_IN_PAD) + tok_specs(D_MODEL) + _mlstm_weight_specs(const),
        out_specs=[pl.BlockSpec((n_batch, ML_CHUNK, D_MODEL), lambda j: (0, j, 0)),
                   pl.BlockSpec((n_batch, ML_HEADS, ML_QK, ML_V), lambda j: (0, 0, 0, 0)),
                   pl.BlockSpec((n_batch, ML_HEADS, ML_QK), lambda j: (0, 0, 0)),
                   pl.BlockSpec((n_batch, 1, ML_HEADS), lambda j: (0, 0, 0))],
        out_shape=[jax.ShapeDtypeStruct((n_batch, seq, D_MODEL), F32),
                   jax.ShapeDtypeStruct((n_batch, ML_HEADS, ML_QK, ML_V), F32),
                   jax.ShapeDtypeStruct((n_batch, ML_HEADS, ML_QK), F32),
                   jax.ShapeDtypeStruct((n_batch, 1, ML_HEADS), F32)],
        compiler_params=_cparams(1),
        name="mlstm_prompt",
    )(*([z] * n_batch), *([x] * n_batch), bias, og, tril, wo)


def _mlstm_sample(z, x, row0, n_seq, t_len, bias, og, tril, wo, c0, n0, m0, seqs_per_step=8):
    rows = seqs_per_step * t_len
    blk0 = row0 // rows
    tok = lambda i: (blk0 + i, 0)
    const = lambda i: (0, 0)
    st4 = lambda i: (i, 0, 0, 0)
    st3 = lambda i: (i, 0, 0)
    state_specs = [pl.BlockSpec((seqs_per_step, ML_HEADS, ML_QK, ML_V), st4),
                   pl.BlockSpec((seqs_per_step, ML_HEADS, ML_QK), st3),
                   pl.BlockSpec((seqs_per_step, 1, ML_HEADS), st3)]
    return pl.pallas_call(
        functools.partial(_mlstm_sample_kernel, t_len=t_len),
        grid=(n_seq // seqs_per_step,),
        in_specs=[pl.BlockSpec((rows, ML_IN_PAD), tok),
                  pl.BlockSpec((rows, D_MODEL), tok)] + _mlstm_weight_specs(const) + state_specs,
        out_specs=[pl.BlockSpec((rows, D_MODEL), lambda i: (i, 0))] + state_specs,
        out_shape=[jax.ShapeDtypeStruct((n_seq * t_len, D_MODEL), F32),
                   jax.ShapeDtypeStruct((n_seq, ML_HEADS, ML_QK, ML_V), F32),
                   jax.ShapeDtypeStruct((n_seq, ML_HEADS, ML_QK), F32),
                   jax.ShapeDtypeStruct((n_seq, 1, ML_HEADS), F32)],
        compiler_params=_cparams(1),
        name="mlstm_sample",
    )(z, x, bias, og, tril, wo, c0, n0, m0)


RSQRT2 = 0.7071067811865476


def _scaled_gelu(xs):
    return xs * (1.0 + lax.erf(xs))


SEL_T = 256


def _batcher_pairs(n):
    pairs = []
    p = 1
    while p < n:
        k = p
        while k >= 1:
            for j in range(k % p, n - k, 2 * k):
                for i in range(min(k, n - j - k)):
                    if (i + j) // (2 * p) == (i + j + k) // (2 * p):
                        pairs.append((i + j, i + j + k))
            k //= 2
        p *= 2
    return pairs


def _bitonic_merge_pairs(n):
    pairs = []
    s = n // 2
    while s >= 1:
        pairs += [(i, i + s) for i in range(n) if not i & s]
        s //= 2
    return pairs


def _top16_of_keys(s, out):
    n_vreg = N_KEYS // SUBLANES
    assert n_vreg == PEER_TOPK and N_KEYS == 128
    rounded = (s + 0.0).astype(BF16).astype(F32)
    b16 = lax.shift_right_logical(lax.bitcast_convert_type(rounded, jnp.int32), 16)
    code = jnp.where(b16 >= 0x8000, b16 ^ 0xFFFF, b16 | 0x8000)
    row = lax.broadcasted_iota(jnp.int32, s.shape, 0)
    keys = lax.bitcast_convert_type((code * N_KEYS + (N_KEYS - 1 - row)) | 0x4B000000, F32)

    x = [keys[v * SUBLANES:(v + 1) * SUBLANES] for v in range(n_vreg)]

    def exchange(i, j):
        x[i], x[j] = jnp.maximum(x[i], x[j]), jnp.minimum(x[i], x[j])

    for i, j in _batcher_pairs(n_vreg):
        exchange(i, j)
    yield
    for shift in (SUBLANES // 2, SUBLANES // 4, SUBLANES // 8):
        other = [pltpu.roll(a, shift, 0) for a in x]
        x = [jnp.maximum(x[i], other[n_vreg - 1 - i]) for i in range(n_vreg)]
        for i, j in _bitonic_merge_pairs(n_vreg):
            exchange(i, j)
        yield
    top = lax.bitcast_convert_type(jnp.concatenate([a[0:1] for a in x], axis=0), jnp.int32) & 0x7FFFFF
    rows = (N_KEYS - 1 - (top & (N_KEYS - 1))).astype(F32)
    code = lax.shift_right_logical(top, 7)
    b16 = jnp.where(code >= 0x8000, code ^ 0x8000, code ^ 0xFFFF)
    out.append((lax.bitcast_convert_type(b16 << 16, F32), rows))


class _TopK:
    PAYLOAD_RANGE = N_KEYS * N_KEYS

    def __init__(self, s, payload):
        assert s.shape[0] * self.PAYLOAD_RANGE < 2 ** 24
        self.s = s
        rows = lax.broadcasted_iota(jnp.int32, s.shape, 0).astype(F32)
        self.code = rows * self.PAYLOAD_RANGE + payload
        self.vals, self.codes = [], []

    def step(self, n):
        bound = float(self.s.shape[0] * self.PAYLOAD_RANGE)
        for _ in range(n):
            m = jnp.max(self.s, axis=0, keepdims=True)
            first = jnp.min(jnp.where(self.s == m, self.code, bound), axis=0, keepdims=True)
            self.vals.append(m)
            self.codes.append(first)
            self.s = jnp.where(self.code == first, -jnp.inf, self.s)

    def result(self):
        codes = jnp.concatenate(self.codes, axis=0)
        payload = codes - jnp.floor(codes * (1.0 / self.PAYLOAD_RANGE)) * self.PAYLOAD_RANGE
        return jnp.concatenate(self.vals, axis=0), payload


_PAIR_GROUPS = (
    ((0, 1, 0, 8),),
    ((0, 2, 0, 5), (5, 3, 0, 3)),
    ((0, 3, 3, 1), (1, 4, 0, 3), (4, 5, 0, 2), (6, 6, 0, 2)),
    ((0, 7, 0, 2), (2, 8, 0, -6)),
    ((0, 14, 0, -2),),
)


def _pair_candidates(first, second):
    h8 = SUBLANES
    (v1, i1), (v2, i2) = [tuple(a.astype(F32) for a in lst) for lst in (first, second)]
    row = lax.broadcasted_iota(jnp.int32, (h8, v1.shape[1]), 0)

    def build(a, b, combine, filler):
        groups = [combine(a[0:1], b)]
        for segments in _PAIR_GROUPS:
            group = jnp.full((h8, a.shape[1]), filler, F32)
            for r0, k1, k2, count in segments:
                if count > 0:
                    seg = combine(a[k1:k1 + 1], pltpu.roll(b[0:h8], (r0 - k2) % h8, 0))
                else:
                    seg = combine(pltpu.roll(a[h8:2 * h8], (r0 - (k1 - h8)) % h8, 0), b[0:1])
                group = jnp.where((row >= r0) & (row < r0 + abs(count)), seg, group)
            groups.append(group)
        return jnp.concatenate(groups, axis=0)

    return (build(v1, v2, lambda a, b: a + b, -jnp.inf),
            build(i1, i2, lambda a, b: a * N_KEYS + b, 0.0))


def _peer_fused_kernel(xs_ref, xm_ref, g_ref, wq_ref, sk_ref, u_ref, v_ref, *rest, split_blk):
    if split_blk is None:
        o_ref, q_scr, sel_e, sel_g, xn_scr, a_scr, bgt_scr, bg_scr, gmat, coef_scr = rest
        out_refs = ()
    else:
        *out_refs, q_scr, sel_e, sel_g, xn_scr, a_scr, bgt_scr, bg_scr, gmat, coef_scr, o_ref = rest
    i = pl.program_id(0)
    e = pl.program_id(1)
    n_steps = pl.num_programs(1)
    t = xm_ref.shape[0]
    n_half = t // SEL_T
    half_keys = N_KEYS // 2
    cur = i % 2

    def project_queries():
        xn = _rmsnorm(xs_ref[...], g_ref[...]).astype(BF16)
        q = jnp.dot(xn, wq_ref[...], preferred_element_type=F32).astype(BF16)
        for hf in range(n_half):
            for c in range(2 * PEER_HEADS):
                q_scr[hf, c] = q[hf * SEL_T:(hf + 1) * SEL_T, c * LANES:(c + 1) * LANES]

    @pl.when(e == 0)
    def _():
        @pl.when(i == 0)
        def _():
            project_queries()
            sel_e[...] = jnp.zeros_like(sel_e)
            sel_g[...] = jnp.zeros_like(sel_g)

        x = xm_ref[...]
        xn_scr[...] = _rmsnorm(x, g_ref[...]).astype(BF16)
        o_ref[...] = x
        coef_scr[0] = jnp.zeros(coef_scr.shape[1:], BF16)
        prev = 1 - cur
        lane_groups = SEL_T // LANES
        for hf in range(n_half):
            ef = sel_e[prev, hf]
            af = jnp.floor(ef * (1.0 / N_KEYS))
            bf = ef - af * N_KEYS
            gf = sel_g[prev, hf]
            a_scr[hf * SEL_T:(hf + 1) * SEL_T, :] = af.T
            bgf = bf + RSQRT2 * gf
            bgt_scr[hf * SEL_T:(hf + 1) * SEL_T, :] = bgf.T
            for lg in range(lane_groups):
                bg_scr[hf * lane_groups + lg] = bgf[:, lg * LANES:(lg + 1) * LANES]
        r = lax.broadcasted_iota(jnp.int32, (N_KEYS, LANES), 0)
        packed_shape = (N_KEYS // (2 * SUBLANES), 2 * SUBLANES, LANES)
        a_of_row = jnp.where(r < half_keys, 2 * r, 2 * (r - half_keys) + 1).astype(F32).astype(BF16)
        a_of_row = a_of_row.reshape(packed_shape)
        one = jnp.ones(packed_shape, BF16)
        zero = jnp.zeros(packed_shape, BF16)
        b_of_lane = lax.broadcasted_iota(jnp.int32, (N_KEYS, LANES), 1).astype(F32)
        b_of_row = r.astype(F32).astype(BF16).reshape(packed_shape)
        per_body = 32
        bodies_per_group = LANES // per_body
        row_major_every = 3

        def body(it, carry):
            grp = lax.shift_right_logical(it, bodies_per_group.bit_length() - 1)
            sub = it & (bodies_per_group - 1)
            shift = (LANES - sub * per_body) & (LANES - 1)
            bg = pltpu.roll(bg_scr[grp], shift, 1)
            tok0 = pl.multiple_of(it * per_body, per_body)
            arows = a_scr[pl.ds(tok0, per_body), :]
            bgrows = bgt_scr[pl.ds(tok0, per_body), :]
            for k in range(per_body):
                arow = jnp.broadcast_to(arows[k:k + 1], (2 * SUBLANES, LANES)).astype(BF16)[None]
                pa = jnp.where(a_of_row == arow, one, zero).reshape(N_KEYS, LANES)
                if k % row_major_every == 0:
                    bgrow = jnp.broadcast_to(bgrows[k:k + 1], (2 * SUBLANES, LANES))
                    brow = jnp.floor(bgrow)
                    grow = (bgrow - brow).astype(BF16)[None]
                    qb = jnp.where(b_of_row == brow.astype(BF16)[None], grow, zero).reshape(N_KEYS, LANES)
                    tile = lax.dot_general(pa, qb, NT_DIMS, preferred_element_type=F32)
                else:
                    bgcol = jnp.broadcast_to(bg[:, k:k + 1], (N_KEYS, LANES))
                    bcol = jnp.floor(bgcol)
                    gcol = bgcol - bcol
                    qbt = jnp.where(bcol == b_of_lane, gcol, 0.0).astype(BF16)
                    tile = jnp.dot(pa, qbt, preferred_element_type=F32)
                row0 = pl.multiple_of((tok0 + k) * G_PITCH, SUBLANES)
                gmat[pl.ds(row0, half_keys), :] = pltpu.pack_elementwise(
                    [tile[0:half_keys], tile[half_keys:N_KEYS]], packed_dtype=BF16)
            return carry

        lax.fori_loop(0, t // per_body, body, 0)

    unit = jnp.minimum(e, n_half * PEER_HEADS - 1)
    half = lax.shift_right_logical(unit, PEER_HEADS.bit_length() - 1)
    h = unit & (PEER_HEADS - 1)
    rd = e & 1
    blk = jnp.minimum(e, n_steps - 2)
    eb = u_ref.shape[0]
    n_stages = 4
    exp_w = eb // n_stages
    out_w = v_ref.shape[1] // n_stages
    words_per_stage = exp_w // (2 * N_KEYS)

    def value_piece(c):
        ocols = slice(c * out_w, (c + 1) * out_w)
        o_ref[:, ocols] += jnp.dot(coef_scr[rd], v_ref[:, ocols], preferred_element_type=F32)

    def act_piece(c):
        ecols = slice(c * exp_w, (c + 1) * exp_w)
        act = lax.dot_general(xn_scr[...], u_ref[ecols, :], NT_DIMS, preferred_element_type=F32)
        gates = []
        for w in range(words_per_stage):
            word = gmat[pl.ds((blk * n_stages + c) * words_per_stage + w, t, stride=G_PITCH), :]
            gates.append(lax.bitcast_convert_type(word << 16, F32))
            gates.append(lax.bitcast_convert_type(word & jnp.int32(-65536), F32))
        coef_scr[1 - rd, :, ecols] = (_scaled_gelu(act) * jnp.concatenate(gates, axis=1)).astype(BF16)

    def retrieval():
        lists = []
        for p in range(2):
            st = lax.dot_general(sk_ref[h, p], q_scr[half, 2 * h + p], NT_DIMS, preferred_element_type=F32)
            yield from _top16_of_keys(st, lists)
            yield
        cand, expert = _pair_candidates(lists[0], lists[1])
        second_level = _TopK(cand, payload=expert)
        for _ in range(4):
            second_level.step(PEER_TOPK // 4)
            yield
        best, e_sel = second_level.result()
        ex = jnp.exp(best - best[0:1])
        slot0 = pl.multiple_of(h * PEER_TOPK, PEER_TOPK)
        sel_e[cur, half, pl.ds(slot0, PEER_TOPK), :] = e_sel
        sel_g[cur, half, pl.ds(slot0, PEER_TOPK), :] = ex / jnp.sum(ex, axis=0, keepdims=True)

    pieces = [functools.partial(f, c) for c in range(n_stages) for f in (value_piece, act_piece)]
    parts = retrieval()
    parts_per_piece = 2
    for piece in pieces:
        piece()
        for _ in range(parts_per_piece):
            next(parts, None)
    for _ in parts:
        pass

    @pl.when(e == n_steps - 1)
    def _():
        project_queries()
        if out_refs:
            first_ref, second_ref = out_refs

            @pl.when(i - 1 < split_blk)
            def _():
                first_ref[...] = o_ref[...]

            @pl.when(i - 1 >= split_blk)
            def _():
                second_ref[...] = o_ref[...]


def _peer(x, g, wq, sk, u, v, t=512, eb=1024, split_rows=None):
    n, d = x.shape
    nslot = PEER_HEADS * PEER_TOPK
    n_tok_blk = n // t
    n_exp_blk = N_KEYS * N_KEYS // eb
    n_half = t // SEL_T
    assert n_exp_blk == n_half * PEER_HEADS
    last_tok = n_tok_blk - 1
    once = pl.Buffered(1)
    if split_rows is None:
        split_blk = None
        out_specs = pl.BlockSpec((t, d), lambda i, e: (jnp.maximum(i - 1, 0), 0))
        out_shape = jax.ShapeDtypeStruct((n, d), F32)
        acc = []
    else:
        split_blk = split_rows // t
        out_specs = [pl.BlockSpec((t, d), lambda i, e: (jnp.clip(i - 1, 0, split_blk - 1), 0)),
                     pl.BlockSpec((t, d), lambda i, e: (jnp.clip(i - 1 - split_blk, 0, last_tok - split_blk), 0))]
        out_shape = [jax.ShapeDtypeStruct((split_rows, d), F32), jax.ShapeDtypeStruct((n - split_rows, d), F32)]
        acc = [pltpu.VMEM((t, d), F32)]
    return pl.pallas_call(
        functools.partial(_peer_fused_kernel, split_blk=split_blk),
        grid=(n_tok_blk + 1, n_exp_blk + 1),
        in_specs=[pl.BlockSpec((t, d), lambda i, e: (jnp.minimum(i + e // n_exp_blk, last_tok), 0),
                               pipeline_mode=once),
                  pl.BlockSpec((t, d), lambda i, e: (jnp.maximum(i - 1, 0), 0)),
                  pl.BlockSpec((1, d), lambda i, e: (0, 0)),
                  pl.BlockSpec((d, 2 * PEER_HEADS * LANES), lambda i, e: (0, 0), pipeline_mode=once),
                  pl.BlockSpec((PEER_HEADS, 2, N_KEYS, LANES), lambda i, e: (0, 0, 0, 0), pipeline_mode=once),
                  pl.BlockSpec((eb, d), lambda i, e: (jnp.minimum(e, n_exp_blk - 1), 0)),
                  pl.BlockSpec((eb, d), lambda i, e: (jnp.maximum(e - 1, 0), 0))],
        out_specs=out_specs,
        out_shape=out_shape,
        scratch_shapes=[pltpu.VMEM((n_half, 2 * PEER_HEADS, SEL_T, LANES), BF16),
                        pltpu.VMEM((2, n_half, nslot, SEL_T), F32),
                        pltpu.VMEM((2, n_half, nslot, SEL_T), F32),
                        pltpu.VMEM((t, d), BF16),
                        pltpu.VMEM((t, nslot), F32),
                        pltpu.VMEM((t, nslot), F32),
                        pltpu.VMEM((t // LANES, nslot, LANES), F32),
                        pltpu.VMEM((t * G_PITCH, LANES), jnp.int32),
                        pltpu.VMEM((2, t, eb), BF16)] + acc,
        compiler_params=_cparams(2),
        name="peer",
    )(x, x, g.reshape(1, d), wq, sk, u, v)


def _rope_tables(pos):
    half = HEAD_DIM // 2
    inv_freq = ROPE_THETA ** (-jnp.arange(half, dtype=F32) / half)
    ang = pos.astype(F32)[:, None] * inv_freq[None, :]
    cos, sin = jnp.cos(ang), jnp.sin(ang)
    reps = LANES // HEAD_DIM
    cos_t = jnp.tile(jnp.concatenate([cos, cos], axis=1), (1, reps))
    sin_t = jnp.tile(jnp.concatenate([-sin, sin], axis=1), (1, reps))
    return cos_t, sin_t


def kernel(x_prompt, x_sample, cache_conv, cache_win_k, cache_win_v, state_mlstm_C, state_mlstm_n,
           state_mlstm_m, norm_mix, norm_ffn, ab_w_in, ab_conv_w, ab_q_gain, ab_k_gain, ab_sinks, ab_w_out,
           ml_w_in, ml_gate_bias, ml_out_gain, ml_w_out, peer_w_q, peer_sub_keys, peer_u, peer_v):
    n_batch, seq, d = x_prompt.shape
    n_seq, t_len, _ = x_sample.shape
    n_prompt = n_batch * seq
    assert d == D_MODEL and t_len == SUBLANES and norm_mix.shape[0] == 2

    xp = x_prompt.reshape(n_prompt, d)
    xs = x_sample.reshape(n_seq * t_len, d)

    cos_p, sin_p = _rope_tables(jnp.arange(seq, dtype=jnp.int32))
    cos_s, sin_s = _rope_tables(PAST_LEN + jnp.arange(t_len, dtype=jnp.int32))
    bt = 16
    cos_s, sin_s = jnp.tile(cos_s, (bt, 1)), jnp.tile(sin_s, (bt, 1))
    lane = jnp.arange(LANES)
    seg = (lane[:, None] // HEAD_DIM == lane[None, :] // HEAD_DIM).astype(BF16)
    reps = LANES // HEAD_DIM
    qg = jnp.tile(ab_q_gain[0], reps).reshape(1, LANES)
    kg = jnp.tile(ab_k_gain[0], reps).reshape(1, LANES)
    wo_ab = ab_w_out[0].astype(BF16)

    z = _norm_proj([xp, xs], norm_mix[0], ab_w_in[0].astype(BF16))
    y_p, k_p, v_p, c_p = _ab_prompt(z, xp, n_batch, seq, cos_p, sin_p, ab_conv_w[0], qg, kg, ab_sinks[0],
                                    seg, wo_ab)
    y_s, c_s, k_s, v_s = _ab_sample(z, xs, n_prompt, n_seq, t_len, cos_s, sin_s, ab_conv_w[0], qg, kg,
                                    ab_sinks[0], seg, wo_ab, cache_conv[0],
                                    cache_win_k[0].reshape(n_seq, WINDOW, LANES),
                                    cache_win_v[0].reshape(n_seq, WINDOW, LANES), bt=bt)
    x = jnp.concatenate([y_p.reshape(n_prompt, d), y_s], axis=0)
    x = _peer(x, norm_ffn[0], peer_w_q[0].astype(BF16), peer_sub_keys[0].astype(BF16),
              (peer_u[0] * RSQRT2).astype(BF16), peer_v[0].astype(BF16))

    n_gate = 2 * ML_HEADS
    w_in = jnp.pad(ml_w_in[0], ((0, 0), (0, ML_IN_PAD - ml_w_in.shape[2]))).astype(BF16)
    bias = jnp.pad(ml_gate_bias[0], (0, LANES - n_gate)).reshape(1, LANES)
    og = ml_out_gain[0].reshape(1, D_MODEL)
    idx = jnp.arange(ML_CHUNK)
    tril = (idx[None, :] <= idx[:, None]).astype(BF16)
    wo_ml = ml_w_out[0].astype(BF16)

    z = _norm_proj([x], norm_mix[1], w_in)
    y_p, cm_p, nm_p, mm_p = _mlstm_prompt(z, x, n_batch, seq, bias, og, tril, wo_ml)
    y_s, cm_s, nm_s, mm_s = _mlstm_sample(z, x, n_prompt, n_seq, t_len, bias, og, tril, wo_ml,
                                          state_mlstm_C[0], state_mlstm_n[0],
                                          state_mlstm_m[0].reshape(n_seq, 1, ML_HEADS))
    x = jnp.concatenate([y_p.reshape(n_prompt, d), y_s], axis=0)
    out_p, out_s = _peer(x, norm_ffn[1], peer_w_q[1].astype(BF16), peer_sub_keys[1].astype(BF16),
                         (peer_u[1] * RSQRT2).astype(BF16), peer_v[1].astype(BF16), split_rows=n_prompt)

    y_prompt = out_p.reshape(n_batch, seq, d)
    y_sample = out_s.reshape(n_seq, t_len, d)
    kv_shape_p = (1, n_batch, WINDOW, N_KV_HEADS, HEAD_DIM)
    kv_shape_s = (1, n_seq, WINDOW, N_KV_HEADS, HEAD_DIM)
    return (y_prompt, y_sample,
            c_p[:, SUBLANES - 2:, :][None], k_p.reshape(kv_shape_p), v_p.reshape(kv_shape_p),
            cm_p[None], nm_p[None], mm_p.reshape(1, n_batch, ML_HEADS),
            c_s[None], k_s.reshape(kv_shape_s), v_s.reshape(kv_shape_s),
            cm_s[None], nm_s[None], mm_s.reshape(1, n_seq, ML_HEADS))
```

```python
import functools

import jax
import jax.numpy as jnp
from jax import lax
from jax.experimental import pallas as pl
from jax.experimental.pallas import tpu as pltpu

F32 = jnp.float32
BF16 = jnp.bfloat16
EPS = 1e-6

D_MODEL = 1024
CONV_DIM = 512
N_Q_HEADS = 8
N_KV_HEADS = 2
HEAD_DIM = 64
WINDOW = 128
ROPE_THETA = 10000.0
AB_IN = 2304
ML_HEADS = 4
ML_QK = 128
ML_V = 256
ML_CHUNK = 128
ML_GATE_COL = 3072
ML_IN_PAD = ML_GATE_COL + 128
N_KEYS = 128
PEER_HEADS = 8
PEER_TOPK = 16
PAST_LEN = 16384

LANES = 128
SUBLANES = 8
G_PITCH = N_KEYS // 2 + SUBLANES
VMEM_LIMIT = 56 * 1024 * 1024

NT_DIMS = (((1,), (1,)), ((), ()))
TN_DIMS = (((0,), (0,)), ((), ()))


def _cparams(n_axes, vmem=VMEM_LIMIT):
    return pltpu.CompilerParams(dimension_semantics=("arbitrary",) * n_axes, vmem_limit_bytes=vmem)


def _rmsnorm(x, g):
    return x * lax.rsqrt(jnp.mean(x * x, axis=-1, keepdims=True) + EPS) * g


def _bf16_pieces(a, terms):
    pieces = []
    rem = a
    for _ in range(terms):
        piece = rem.astype(BF16)
        rem = rem - piece.astype(F32)
        pieces.append(piece)
    return pieces


def _split_dot(a, b_bf16, terms=2):
    return sum(jnp.dot(p, b_bf16, preferred_element_type=F32) for p in _bf16_pieces(a, terms))


def _norm_proj_kernel(*refs, starts):
    x_refs = refs[:len(starts)]
    g_ref, w_ref, o_ref = refs[len(starts):]
    i = pl.program_id(0)
    x = x_refs[0][...]
    for x_ref, start in zip(x_refs[1:], starts[1:]):
        x = jnp.where(i >= start, x_ref[...], x)
    r = _rmsnorm(x, g_ref[...])
    o_ref[...] = jnp.dot(r.astype(BF16), w_ref[...], preferred_element_type=F32)


def _norm_proj(x_parts, g, w_bf16, tm=512):
    d = x_parts[0].shape[1]
    nout = w_bf16.shape[1]
    blocks = [p.shape[0] // tm for p in x_parts]
    starts = tuple(sum(blocks[:k]) for k in range(len(blocks)))
    part_spec = lambda start, nblk: pl.BlockSpec((tm, d), lambda i: (jnp.clip(i - start, 0, nblk - 1), 0))
    return pl.pallas_call(
        functools.partial(_norm_proj_kernel, starts=starts),
        grid=(sum(blocks),),
        in_specs=[part_spec(s, nb) for s, nb in zip(starts, blocks)] + [
            pl.BlockSpec((1, d), lambda i: (0, 0)),
            pl.BlockSpec((d, nout), lambda i: (0, 0))],
        out_specs=pl.BlockSpec((tm, nout), lambda i: (i, 0)),
        out_shape=jax.ShapeDtypeStruct((sum(blocks) * tm, nout), F32),
        compiler_params=_cparams(1),
        name="norm_proj",
    )(*x_parts, g.reshape(1, d), w_bf16)


def _headnorm_rope(xc, gain, cos, sin, seg, hi_half):
    ss = _split_dot(xc * xc, seg)
    xn = xc * lax.rsqrt(ss * (1.0 / HEAD_DIM) + EPS) * gain
    partner = jnp.where(hi_half, pltpu.roll(xn, 32, 1), pltpu.roll(xn, 96, 1))
    return xn * cos + partner * sin


def _softmax_sink(s, mask, sink):
    s = jnp.where(mask, s, -1e30)
    m = jnp.maximum(jnp.max(s, axis=-1, keepdims=True), sink)
    p = jnp.exp(s - m)
    denom = jnp.sum(p, axis=-1, keepdims=True) + jnp.exp(sink - m)
    return (p / denom).astype(BF16)


def _ab_prompt_kernel(sink_ref, *refs, n_batch):
    z_refs, x_refs = refs[0:n_batch], refs[n_batch:2 * n_batch]
    (cos_ref, sin_ref, cw_ref, qg_ref, kg_ref, seg_ref, wo_ref,
     y_ref, kst_ref, vst_ref, cst_ref, pk_ref, pv_ref, pu_ref) = refs[2 * n_batch:]
    j = pl.program_id(0)

    @pl.when(j == 0)
    def _():
        pk_ref[...] = jnp.zeros_like(pk_ref)
        pv_ref[...] = jnp.zeros_like(pv_ref)
        pu_ref[...] = jnp.zeros_like(pu_ref)

    for b in range(n_batch):
        _ab_prompt_block(j, sink_ref, z_refs[b], x_refs[b], cos_ref, sin_ref, cw_ref, qg_ref, kg_ref, seg_ref,
                         wo_ref, y_ref.at[b], kst_ref.at[b], vst_ref.at[b], cst_ref.at[b],
                         pk_ref.at[b], pv_ref.at[b], pu_ref.at[b])


def _ab_prompt_block(j, sink_ref, z_ref, x_ref, cos_ref, sin_ref, cw_ref, qg_ref, kg_ref, seg_ref, wo_ref,
                     y_ref, kst_ref, vst_ref, cst_ref, pk_ref, pv_ref, pu_ref):
    blk = z_ref.shape[0]
    cos = cos_ref[...]
    sin = sin_ref[...]
    seg = seg_ref[...]
    hi_half = (lax.broadcasted_iota(jnp.int32, (blk, LANES), 1) & 32) != 0

    gate_b = z_ref[:, 0:CONV_DIM]
    u = z_ref[:, CONV_DIM:2 * CONV_DIM] * z_ref[:, 2 * CONV_DIM:3 * CONV_DIM]
    ng = blk // SUBLANES
    u3 = u.reshape(ng, SUBLANES, CONV_DIM)
    ext = jnp.concatenate([pu_ref[...][None], u3], axis=0)
    t8 = lax.broadcasted_iota(jnp.int32, (ng, SUBLANES, CONV_DIM), 1)
    r1 = pltpu.roll(ext, 1, 1)
    r2 = pltpu.roll(ext, 2, 1)
    um1 = jnp.where(t8 >= 1, r1[1:], r1[:-1])
    um2 = jnp.where(t8 >= 2, r2[1:], r2[:-1])
    cw = cw_ref[...]
    conv = cw[0:1][None] * um2 + cw[1:2][None] * um1 + cw[2:3][None] * u3
    yconv = gate_b * conv.reshape(blk, CONV_DIM)

    q0 = 3 * CONV_DIM
    k0 = q0 + N_Q_HEADS * HEAD_DIM
    v0 = k0 + N_KV_HEADS * HEAD_DIM
    qg = qg_ref[...]
    qr = [_headnorm_rope(z_ref[:, q0 + c * LANES:q0 + (c + 1) * LANES], qg, cos, sin, seg, hi_half)
          for c in range(N_Q_HEADS * HEAD_DIM // LANES)]
    kr = _headnorm_rope(z_ref[:, k0:k0 + LANES], kg_ref[...], cos, sin, seg, hi_half)
    v = z_ref[:, v0:v0 + LANES]
    pk = pk_ref[...]
    pv = pv_ref[...]

    row = lax.broadcasted_iota(jnp.int32, (blk, 2 * blk), 0)
    col = lax.broadcasted_iota(jnp.int32, (blk, 2 * blk), 1)
    row_prev = row + jnp.where(j == 0, 2 * blk, 0)
    mask = ((col < blk) & (col > row_prev)) | ((col >= blk) & (col - blk <= row))

    kks, vvs = [], []
    for g in range(N_KV_HEADS):
        sl = slice(g * HEAD_DIM, (g + 1) * HEAD_DIM)
        kks.append(jnp.concatenate([pk[:, sl], kr[:, sl]], axis=0).astype(BF16))
        vvs.append(jnp.concatenate([pv[:, sl], v[:, sl]], axis=0).astype(BF16))
    outs = []
    for h in range(N_Q_HEADS):
        g = h // (N_Q_HEADS // N_KV_HEADS)
        qh = qr[h // 2][:, (h % 2) * HEAD_DIM:(h % 2 + 1) * HEAD_DIM].astype(BF16)
        s = lax.dot_general(qh, kks[g], NT_DIMS, preferred_element_type=F32) * (HEAD_DIM ** -0.5)
        p = _softmax_sink(s, mask, sink_ref[h])
        outs.append(jnp.dot(p, vvs[g], preferred_element_type=F32))
    attn = jnp.concatenate(outs, axis=1)

    y = (jnp.dot(yconv.astype(BF16), wo_ref[0:CONV_DIM, :], preferred_element_type=F32)
         + jnp.dot(attn.astype(BF16), wo_ref[CONV_DIM:2 * CONV_DIM, :], preferred_element_type=F32))
    y_ref[...] = y + x_ref[...]

    pk_ref[...] = kr
    pv_ref[...] = v
    pu_ref[...] = u3[ng - 1]
    kst_ref[...] = kr
    vst_ref[...] = v
    cst_ref[...] = u3[ng - 1]


def _ab_prompt(z, x, n_batch, seq, cos, sin, cw, qg, kg, sinks, seg, wo):
    blk = WINDOW
    nb = seq // blk
    const = lambda j, s: (0, 0)
    whole = lambda j, s: (0, 0, 0)
    tok_specs = lambda width: [pl.BlockSpec((blk, width), functools.partial(lambda j, s, b: (b * nb + j, 0), b=b))
                               for b in range(n_batch)]
    grid_spec = pltpu.PrefetchScalarGridSpec(
        num_scalar_prefetch=1,
        grid=(nb,),
        in_specs=tok_specs(AB_IN) + tok_specs(D_MODEL) + [
            pl.BlockSpec((blk, LANES), lambda j, s: (j, 0)),
            pl.BlockSpec((blk, LANES), lambda j, s: (j, 0)),
            pl.BlockSpec((3, CONV_DIM), const),
            pl.BlockSpec((1, LANES), const),
            pl.BlockSpec((1, LANES), const),
            pl.BlockSpec((LANES, LANES), const),
            pl.BlockSpec((D_MODEL, D_MODEL), const)],
        out_specs=[pl.BlockSpec((n_batch, blk, D_MODEL), lambda j, s: (0, j, 0)),
                   pl.BlockSpec((n_batch, blk, LANES), whole),
                   pl.BlockSpec((n_batch, blk, LANES), whole),
                   pl.BlockSpec((n_batch, SUBLANES, CONV_DIM), whole)],
        scratch_shapes=[pltpu.VMEM((n_batch, blk, LANES), F32), pltpu.VMEM((n_batch, blk, LANES), F32),
                        pltpu.VMEM((n_batch, SUBLANES, CONV_DIM), F32)])
    return pl.pallas_call(
        functools.partial(_ab_prompt_kernel, n_batch=n_batch),
        grid_spec=grid_spec,
        out_shape=[jax.ShapeDtypeStruct((n_batch, seq, D_MODEL), F32),
                   jax.ShapeDtypeStruct((n_batch, blk, LANES), F32),
                   jax.ShapeDtypeStruct((n_batch, blk, LANES), F32),
                   jax.ShapeDtypeStruct((n_batch, SUBLANES, CONV_DIM), F32)],
        compiler_params=_cparams(1),
        name="ab_prompt",
    )(sinks, *([z] * n_batch), *([x] * n_batch), cos, sin, cw, qg, kg, seg, wo)


def _ab_sample_kernel(sink_ref, z_ref, x_ref, cos_ref, sin_ref, cw_ref, qg_ref, kg_ref, seg_ref, wo_ref,
                      cc_ref, ck_ref, cv_ref, y_ref, cs_ref, ks_ref, vs_ref):
    rows = z_ref.shape[0]
    t_len = SUBLANES
    bt = rows // t_len
    cos = cos_ref[...]
    sin = sin_ref[...]
    seg = seg_ref[...]
    hi_half = (lax.broadcasted_iota(jnp.int32, (rows, LANES), 1) & 32) != 0

    gate_b = z_ref[:, 0:CONV_DIM]
    u = z_ref[:, CONV_DIM:2 * CONV_DIM] * z_ref[:, 2 * CONV_DIM:3 * CONV_DIM]
    u3 = u.reshape(bt, t_len, CONV_DIM)
    cc = cc_ref[...]
    c0 = cc[:, 0:1, :]
    c1 = cc[:, 1:2, :]
    t8 = lax.broadcasted_iota(jnp.int32, (bt, t_len, CONV_DIM), 1)
    r1 = pltpu.roll(u3, 1, 1)
    r2 = pltpu.roll(u3, 2, 1)
    um1 = jnp.where(t8 >= 1, r1, c1)
    um2 = jnp.where(t8 >= 2, r2, jnp.where(t8 == 1, c1, c0))
    cw = cw_ref[...]
    conv = cw[0:1][None] * um2 + cw[1:2][None] * um1 + cw[2:3][None] * u3
    yconv = gate_b * conv.reshape(rows, CONV_DIM)
    cs_ref[...] = r2[:, 0:2, :]

    q0 = 3 * CONV_DIM
    k0 = q0 + N_Q_HEADS * HEAD_DIM
    v0 = k0 + N_KV_HEADS * HEAD_DIM
    qg = qg_ref[...]
    qr = [_headnorm_rope(z_ref[:, q0 + c * LANES:q0 + (c + 1) * LANES], qg, cos, sin, seg, hi_half)
          for c in range(N_Q_HEADS * HEAD_DIM // LANES)]
    kr = _headnorm_rope(z_ref[:, k0:k0 + LANES], kg_ref[...], cos, sin, seg, hi_half)
    v = z_ref[:, v0:v0 + LANES]

    group = N_Q_HEADS // N_KV_HEADS
    nq = group * t_len
    nk = 2 * WINDOW
    qrow = lax.broadcasted_iota(jnp.int32, (nq, nk), 0)
    t_q = qrow & (t_len - 1)
    col = lax.broadcasted_iota(jnp.int32, (nq, nk), 1)
    mask = (((col < WINDOW) & (col > t_q)) | ((col >= WINDOW) & (col - WINDOW <= t_q)))[None]
    hrow = lax.broadcasted_iota(jnp.int32, (nq, 1), 0) // t_len
    pad = jnp.zeros((bt, nk - WINDOW - t_len, HEAD_DIM), F32)

    outs = [None] * N_Q_HEADS
    for g in range(N_KV_HEADS):
        sl = slice(g * HEAD_DIM, (g + 1) * HEAD_DIM)
        qs = jnp.concatenate(
            [qr[h // 2][:, (h % 2) * HEAD_DIM:(h % 2 + 1) * HEAD_DIM].reshape(bt, t_len, HEAD_DIM)
             for h in range(g * group, (g + 1) * group)], axis=1)
        kk = jnp.concatenate([ck_ref[:, :, sl], kr[:, sl].reshape(bt, t_len, HEAD_DIM), pad], axis=1)
        vv = jnp.concatenate([cv_ref[:, :, sl], v[:, sl].reshape(bt, t_len, HEAD_DIM), pad], axis=1)
        s = jnp.einsum('bqd,bkd->bqk', qs.astype(BF16), kk.astype(BF16),
                       preferred_element_type=F32) * (HEAD_DIM ** -0.5)
        sink = jnp.zeros((nq, 1), F32)
        for hh in range(group):
            sink = jnp.where(hrow == hh, sink_ref[g * group + hh], sink)
        p = _softmax_sink(s, mask, sink[None])
        o = jnp.einsum('bqk,bkd->bqd', p, vv.astype(BF16), preferred_element_type=F32)
        for hh in range(group):
            outs[g * group + hh] = o[:, hh * t_len:(hh + 1) * t_len, :].reshape(rows, HEAD_DIM)
    attn = jnp.concatenate(outs, axis=1)

    y = (jnp.dot(yconv.astype(BF16), wo_ref[0:CONV_DIM, :], preferred_element_type=F32)
         + jnp.dot(attn.astype(BF16), wo_ref[CONV_DIM:2 * CONV_DIM, :], preferred_element_type=F32))
    y_ref[...] = y + x_ref[...]

    keep = WINDOW - t_len
    ks_ref[:, 0:keep, :] = ck_ref[:, t_len:WINDOW, :]
    ks_ref[:, keep:WINDOW, :] = kr.reshape(bt, t_len, LANES)
    vs_ref[:, 0:keep, :] = cv_ref[:, t_len:WINDOW, :]
    vs_ref[:, keep:WINDOW, :] = v.reshape(bt, t_len, LANES)


def _ab_sample(z, x, row0, n_seq, t_len, cos, sin, cw, qg, kg, sinks, seg, wo, cc, ck, cv, bt=16):
    rows = bt * t_len
    blk0 = row0 // rows
    tok = lambda i, s: (blk0 + i, 0)
    const = lambda i, s: (0, 0)
    seq3 = lambda i, s: (i, 0, 0)
    grid_spec = pltpu.PrefetchScalarGridSpec(
        num_scalar_prefetch=1,
        grid=(n_seq // bt,),
        in_specs=[pl.BlockSpec((rows, AB_IN), tok),
                  pl.BlockSpec((rows, D_MODEL), lambda i, s: (i, 0)),
                  pl.BlockSpec((rows, LANES), const),
                  pl.BlockSpec((rows, LANES), const),
                  pl.BlockSpec((3, CONV_DIM), const),
                  pl.BlockSpec((1, LANES), const),
                  pl.BlockSpec((1, LANES), const),
                  pl.BlockSpec((LANES, LANES), const),
                  pl.BlockSpec((D_MODEL, D_MODEL), const),
                  pl.BlockSpec((bt, 2, CONV_DIM), seq3),
                  pl.BlockSpec((bt, WINDOW, LANES), seq3),
                  pl.BlockSpec((bt, WINDOW, LANES), seq3)],
        out_specs=[pl.BlockSpec((rows, D_MODEL), lambda i, s: (i, 0)),
                   pl.BlockSpec((bt, 2, CONV_DIM), seq3),
                   pl.BlockSpec((bt, WINDOW, LANES), seq3),
                   pl.BlockSpec((bt, WINDOW, LANES), seq3)])
    return pl.pallas_call(
        _ab_sample_kernel,
        grid_spec=grid_spec,
        out_shape=[jax.ShapeDtypeStruct((n_seq * t_len, D_MODEL), F32),
                   jax.ShapeDtypeStruct((n_seq, 2, CONV_DIM), F32),
                   jax.ShapeDtypeStruct((n_seq, WINDOW, LANES), F32),
                   jax.ShapeDtypeStruct((n_seq, WINDOW, LANES), F32)],
        compiler_params=_cparams(1),
        name="ab_sample",
    )(sinks, z, x, cos, sin, cw, qg, kg, seg, wo, cc, ck, cv)


def _log_sigmoid(x):
    return jnp.minimum(x, 0.0) - jnp.log(1.0 + jnp.exp(-jnp.abs(x)))


def _mlstm_chunk(z, bias, og, tril, c_src, n_src, m_src, c_dst, n_dst, m_dst, n_real, rows):
    L = z.shape[0]
    gates = z[:, ML_GATE_COL:ML_GATE_COL + LANES] + bias
    if n_real < L:
        live = lax.broadcasted_iota(jnp.int32, (L, LANES), 0) < n_real
        li_all = jnp.where(live, gates, -1e30)
        lf_all = jnp.where(live, _log_sigmoid(gates), 0.0)
    else:
        li_all = gates
        lf_all = _log_sigmoid(gates)
    lf_pieces = _bf16_pieces(lf_all, 3)
    f_col_all = sum(jnp.dot(tril[0:rows], p, preferred_element_type=F32) for p in lf_pieces)
    f_row_all = sum(lax.dot_general(p, tril, (((0,), (1,)), ((), ())), preferred_element_type=F32)
                    for p in lf_pieces)
    li_t = li_all.T
    rr = lax.broadcasted_iota(jnp.int32, (rows, L), 0)
    cc = lax.broadcasted_iota(jnp.int32, (rows, L), 1)
    causal = cc <= rr

    outs, m_new_all = [], []
    for h in range(ML_HEADS):
        f_col = f_col_all[:, ML_HEADS + h:ML_HEADS + h + 1]
        f_row = f_row_all[ML_HEADS + h:ML_HEADS + h + 1, :]
        li_row = li_t[h:h + 1, :]
        li_col = li_all[0:rows, h:h + 1]
        m0 = m_src[0:1, h:h + 1]
        c0 = c_src[h]
        n0 = n_src[h:h + 1, :]
        qh = z[0:rows, h * ML_QK:(h + 1) * ML_QK]
        kh = z[:, ML_HEADS * ML_QK + h * ML_QK:ML_HEADS * ML_QK + (h + 1) * ML_QK] * (ML_QK ** -0.5)
        v_off = 2 * ML_HEADS * ML_QK
        vh = z[:, v_off + h * ML_V:v_off + (h + 1) * ML_V]
        o_off = v_off + ML_HEADS * ML_V
        oh = z[0:rows, o_off + h * ML_V:o_off + (h + 1) * ML_V]
        qb = qh.astype(BF16)
        vb = vh.astype(BF16)

        dmat = jnp.where(causal, f_col - f_row + li_row, -jnp.inf)
        gcar = f_col + m0
        m_t = jnp.maximum(jnp.max(dmat, axis=-1, keepdims=True), gcar)
        w = jnp.exp(dmat - m_t)
        s = lax.dot_general(qb, kh.astype(BF16), NT_DIMS, preferred_element_type=F32) * w
        carry = jnp.exp(gcar - m_t)
        num = (jnp.dot(s.astype(BF16), vb, preferred_element_type=F32)
               + jnp.dot(qb, c0.astype(BF16), preferred_element_type=F32) * carry)
        den = jnp.sum(s, axis=-1, keepdims=True) + carry * jnp.sum(qh * n0, axis=-1, keepdims=True)
        hout = num / jnp.maximum(jnp.abs(den), jnp.exp(-m_t))

        f_last = f_col[n_real - 1:n_real, :]
        w_end = f_last - f_col + li_col
        m_new = jnp.maximum(f_last + m0, jnp.max(w_end, axis=0, keepdims=True))
        a_end = jnp.exp(w_end - m_new)
        scale = jnp.exp(f_last + m0 - m_new)
        ka = kh[0:rows] * a_end
        c_dst[h] = scale * c0 + lax.dot_general(ka.astype(BF16), vb[0:rows], TN_DIMS,
                                                preferred_element_type=F32)
        n_dst[h:h + 1, :] = scale * n0 + jnp.sum(ka, axis=0, keepdims=True)
        m_new_all.append(m_new)

        hn = _rmsnorm(hout, og[:, h * ML_V:(h + 1) * ML_V])
        outs.append(jax.nn.sigmoid(oh) * hn)
    m_dst[...] = jnp.concatenate(m_new_all, axis=1)
    return jnp.concatenate(outs, axis=1)


def _mlstm_prompt_kernel(*refs, n_batch):
    z_refs, x_refs = refs[0:n_batch], refs[n_batch:2 * n_batch]
    bias_ref, og_ref, tril_ref, wo_ref, y_ref, c_ref, n_ref, m_ref = refs[2 * n_batch:]

    @pl.when(pl.program_id(0) == 0)
    def _():
        c_ref[...] = jnp.zeros_like(c_ref)
        n_ref[...] = jnp.zeros_like(n_ref)
        m_ref[...] = jnp.zeros_like(m_ref)

    for b in range(n_batch):
        state = (c_ref.at[b], n_ref.at[b], m_ref.at[b])
        out = _mlstm_chunk(z_refs[b][...], bias_ref[...], og_ref[...], tril_ref[...], *state, *state,
                           ML_CHUNK, ML_CHUNK)
        y_ref[b] = jnp.dot(out.astype(BF16), wo_ref[...], preferred_element_type=F32) + x_refs[b][...]


def _mlstm_sample_kernel(z_ref, x_ref, bias_ref, og_ref, tril_ref, wo_ref, c0_ref, n0_ref, m0_ref,
                         y_ref, c_ref, n_ref, m_ref, *, t_len):
    n_here = z_ref.shape[0] // t_len
    q_rows = 2 * SUBLANES
    outs = []
    for s in range(n_here):
        zpad = jnp.concatenate([z_ref[s * t_len:(s + 1) * t_len, :],
                                jnp.zeros((ML_CHUNK - t_len, ML_IN_PAD), F32)], axis=0)
        out = _mlstm_chunk(zpad, bias_ref[...], og_ref[...], tril_ref[...],
                           c0_ref.at[s], n0_ref.at[s], m0_ref.at[s], c_ref.at[s], n_ref.at[s], m_ref.at[s],
                           t_len, q_rows)
        outs.append(out[0:t_len])
    out_all = jnp.concatenate(outs, axis=0).astype(BF16)
    y_ref[...] = jnp.dot(out_all, wo_ref[...], preferred_element_type=F32) + x_ref[...]


def _mlstm_weight_specs(const):
    return [pl.BlockSpec((1, LANES), const),
            pl.BlockSpec((1, D_MODEL), const),
            pl.BlockSpec((ML_CHUNK, ML_CHUNK), const),
            pl.BlockSpec((D_MODEL, D_MODEL), const)]


def _mlstm_prompt(z, x, n_batch, seq, bias, og, tril, wo):
    nc = seq // ML_CHUNK
    const = lambda j: (0, 0)
    tok_specs = lambda width: [pl.BlockSpec((ML_CHUNK, width), functools.partial(lambda j, b: (b * nc + j, 0), b=b))
                               for b in range(n_batch)]
    return pl.pallas_call(
        functools.partial(_mlstm_prompt_kernel, n_batch=n_batch),
        grid=(nc,),
        in_specs=tok_specs(ML_IN_PAD) + tok_specs(D_MODEL) + _mlstm_weight_specs(const),
        out_specs=[pl.BlockSpec((n_batch, ML_CHUNK, D_MODEL), lambda j: (0, j, 0)),
                   pl.BlockSpec((n_batch, ML_HEADS, ML_QK, ML_V), lambda j: (0, 0, 0, 0)),
                   pl.BlockSpec((n_batch, ML_HEADS, ML_QK), lambda j: (0, 0, 0)),
                   pl.BlockSpec((n_batch, 1, ML_HEADS), lambda j: (0, 0, 0))],
        out_shape=[jax.ShapeDtypeStruct((n_batch, seq, D_MODEL), F32),
                   jax.ShapeDtypeStruct((n_batch, ML_HEADS, ML_QK, ML_V), F32),
                   jax.ShapeDtypeStruct((n_batch, ML_HEADS, ML_QK), F32),
                   jax.ShapeDtypeStruct((n_batch, 1, ML_HEADS), F32)],
        compiler_params=_cparams(1),
        name="mlstm_prompt",
    )(*([z] * n_batch), *([x] * n_batch), bias, og, tril, wo)


def _mlstm_sample(z, x, row0, n_seq, t_len, bias, og, tril, wo, c0, n0, m0, seqs_per_step=8):
    rows = seqs_per_step * t_len
    blk0 = row0 // rows
    tok = lambda i: (blk0 + i, 0)
    const = lambda i: (0, 0)
    st4 = lambda i: (i, 0, 0, 0)
    st3 = lambda i: (i, 0, 0)
    state_specs = [pl.BlockSpec((seqs_per_step, ML_HEADS, ML_QK, ML_V), st4),
                   pl.BlockSpec((seqs_per_step, ML_HEADS, ML_QK), st3),
                   pl.BlockSpec((seqs_per_step, 1, ML_HEADS), st3)]
    return pl.pallas_call(
        functools.partial(_mlstm_sample_kernel, t_len=t_len),
        grid=(n_seq // seqs_per_step,),
        in_specs=[pl.BlockSpec((rows, ML_IN_PAD), tok),
                  pl.BlockSpec((rows, D_MODEL), tok)] + _mlstm_weight_specs(const) + state_specs,
        out_specs=[pl.BlockSpec((rows, D_MODEL), lambda i: (i, 0))] + state_specs,
        out_shape=[jax.ShapeDtypeStruct((n_seq * t_len, D_MODEL), F32),
                   jax.ShapeDtypeStruct((n_seq, ML_HEADS, ML_QK, ML_V), F32),
                   jax.ShapeDtypeStruct((n_seq, ML_HEADS, ML_QK), F32),
                   jax.ShapeDtypeStruct((n_seq, 1, ML_HEADS), F32)],
        compiler_params=_cparams(1),
        name="mlstm_sample",
    )(z, x, bias, og, tril, wo, c0, n0, m0)


RSQRT2 = 0.7071067811865476


def _scaled_gelu(xs):
    return xs * (1.0 + lax.erf(xs))


SEL_T = 256


def _batcher_pairs(n):
    pairs = []
    p = 1
    while p < n:
        k = p
        while k >= 1:
            for j in range(k % p, n - k, 2 * k):
                for i in range(min(k, n - j - k)):
                    if (i + j) // (2 * p) == (i + j + k) // (2 * p):
                        pairs.append((i + j, i + j + k))
            k //= 2
        p *= 2
    return pairs


def _bitonic_merge_pairs(n):
    pairs = []
    s = n // 2
    while s >= 1:
        pairs += [(i, i + s) for i in range(n) if not i & s]
        s //= 2
    return pairs


def _top16_of_keys(s, out):
    n_vreg = N_KEYS // SUBLANES
    assert n_vreg == PEER_TOPK and N_KEYS == 128
    rounded = (s + 0.0).astype(BF16).astype(F32)
    b16 = lax.shift_right_logical(lax.bitcast_convert_type(rounded, jnp.int32), 16)
    code = jnp.where(b16 >= 0x8000, b16 ^ 0xFFFF, b16 | 0x8000)
    row = lax.broadcasted_iota(jnp.int32, s.shape, 0)
    keys = lax.bitcast_convert_type((code * N_KEYS + (N_KEYS - 1 - row)) | 0x4B000000, F32)

    x = [keys[v * SUBLANES:(v + 1) * SUBLANES] for v in range(n_vreg)]

    def exchange(i, j):
        x[i], x[j] = jnp.maximum(x[i], x[j]), jnp.minimum(x[i], x[j])

    for i, j in _batcher_pairs(n_vreg):
        exchange(i, j)
    yield
    for shift in (SUBLANES // 2, SUBLANES // 4, SUBLANES // 8):
        other = [pltpu.roll(a, shift, 0) for a in x]
        x = [jnp.maximum(x[i], other[n_vreg - 1 - i]) for i in range(n_vreg)]
        for i, j in _bitonic_merge_pairs(n_vreg):
            exchange(i, j)
        yield
    top = lax.bitcast_convert_type(jnp.concatenate([a[0:1] for a in x], axis=0), jnp.int32) & 0x7FFFFF
    rows = (N_KEYS - 1 - (top & (N_KEYS - 1))).astype(F32)
    code = lax.shift_right_logical(top, 7)
    b16 = jnp.where(code >= 0x8000, code ^ 0x8000, code ^ 0xFFFF)
    out.append((lax.bitcast_convert_type(b16 << 16, F32), rows))


class _TopK:
    def __init__(self, s, payload=None):
        self.s = s
        self.payload = payload
        self.rows = lax.broadcasted_iota(jnp.int32, s.shape, 0).astype(F32).astype(s.dtype)
        self.vals, self.picks = [], []

    def step(self, n):
        dt = self.s.dtype
        bound = jnp.asarray(self.s.shape[0], dt)
        for _ in range(n):
            m = jnp.max(self.s, axis=0, keepdims=True)
            first = jnp.min(jnp.where(self.s == m, self.rows, bound), axis=0, keepdims=True)
            sel = self.rows == first
            self.vals.append(m)
            if self.payload is None:
                self.picks.append(first)
            else:
                self.picks.append(jnp.max(jnp.where(sel, self.payload, -1.0), axis=0, keepdims=True))
            self.s = jnp.where(sel, jnp.asarray(-jnp.inf, dt), self.s)

    def result(self):
        return jnp.concatenate(self.vals, axis=0), jnp.concatenate(self.picks, axis=0)


_PAIR_GROUPS = (
    ((0, 1, 0, 8),),
    ((0, 2, 0, 5), (5, 3, 0, 3)),
    ((0, 3, 3, 1), (1, 4, 0, 3), (4, 5, 0, 2), (6, 6, 0, 2)),
    ((0, 7, 0, 2), (2, 8, 0, -6)),
    ((0, 14, 0, -2),),
)


def _pair_candidates(first, second):
    h8 = SUBLANES
    (v1, i1), (v2, i2) = [tuple(a.astype(F32) for a in lst) for lst in (first, second)]
    row = lax.broadcasted_iota(jnp.int32, (h8, v1.shape[1]), 0)

    def build(a, b, combine, filler):
        groups = [combine(a[0:1], b)]
        for segments in _PAIR_GROUPS:
            group = jnp.full((h8, a.shape[1]), filler, F32)
            for r0, k1, k2, count in segments:
                if count > 0:
                    seg = combine(a[k1:k1 + 1], pltpu.roll(b[0:h8], (r0 - k2) % h8, 0))
                else:
                    seg = combine(pltpu.roll(a[h8:2 * h8], (r0 - (k1 - h8)) % h8, 0), b[0:1])
                group = jnp.where((row >= r0) & (row < r0 + abs(count)), seg, group)
            groups.append(group)
        return jnp.concatenate(groups, axis=0)

    return (build(v1, v2, lambda a, b: a + b, -jnp.inf),
            build(i1, i2, lambda a, b: a * N_KEYS + b, 0.0))


def _peer_fused_kernel(xs_ref, xm_ref, g_ref, wq_ref, sk_ref, u_ref, v_ref, *rest, split_blk):
    if split_blk is None:
        o_ref, q_scr, sel_e, sel_g, xn_scr, a_scr, bgt_scr, bg_scr, gmat, coef_scr = rest
        out_refs = ()
    else:
        *out_refs, q_scr, sel_e, sel_g, xn_scr, a_scr, bgt_scr, bg_scr, gmat, coef_scr, o_ref = rest
    i = pl.program_id(0)
    e = pl.program_id(1)
    n_steps = pl.num_programs(1)
    t = xm_ref.shape[0]
    n_half = t // SEL_T
    half_keys = N_KEYS // 2
    cur = i % 2

    def project_queries():
        xn = _rmsnorm(xs_ref[...], g_ref[...]).astype(BF16)
        q = jnp.dot(xn, wq_ref[...], preferred_element_type=F32).astype(BF16)
        for hf in range(n_half):
            for c in range(2 * PEER_HEADS):
                q_scr[hf, c] = q[hf * SEL_T:(hf + 1) * SEL_T, c * LANES:(c + 1) * LANES]

    @pl.when(e == 0)
    def _():
        @pl.when(i == 0)
        def _():
            project_queries()
            sel_e[...] = jnp.zeros_like(sel_e)
            sel_g[...] = jnp.zeros_like(sel_g)

        x = xm_ref[...]
        xn_scr[...] = _rmsnorm(x, g_ref[...]).astype(BF16)
        o_ref[...] = x
        coef_scr[0] = jnp.zeros(coef_scr.shape[1:], BF16)
        prev = 1 - cur
        lane_groups = SEL_T // LANES
        for hf in range(n_half):
            ef = sel_e[prev, hf]
            af = jnp.floor(ef * (1.0 / N_KEYS))
            bf = ef - af * N_KEYS
            gf = sel_g[prev, hf]
            a_scr[hf * SEL_T:(hf + 1) * SEL_T, :] = af.T
            bgf = bf + RSQRT2 * gf
            bgt_scr[hf * SEL_T:(hf + 1) * SEL_T, :] = bgf.T
            for lg in range(lane_groups):
                bg_scr[hf * lane_groups + lg] = bgf[:, lg * LANES:(lg + 1) * LANES]
        r = lax.broadcasted_iota(jnp.int32, (N_KEYS, LANES), 0)
        packed_shape = (N_KEYS // (2 * SUBLANES), 2 * SUBLANES, LANES)
        a_of_row = jnp.where(r < half_keys, 2 * r, 2 * (r - half_keys) + 1).astype(F32).astype(BF16)
        a_of_row = a_of_row.reshape(packed_shape)
        one = jnp.ones(packed_shape, BF16)
        zero = jnp.zeros(packed_shape, BF16)
        b_of_lane = lax.broadcasted_iota(jnp.int32, (N_KEYS, LANES), 1).astype(F32)
        b_of_row = r.astype(F32).astype(BF16).reshape(packed_shape)
        per_body = 32
        bodies_per_group = LANES // per_body
        row_major_every = 3

        def body(it, carry):
            grp = lax.shift_right_logical(it, bodies_per_group.bit_length() - 1)
            sub = it & (bodies_per_group - 1)
            shift = (LANES - sub * per_body) & (LANES - 1)
            bg = pltpu.roll(bg_scr[grp], shift, 1)
            tok0 = pl.multiple_of(it * per_body, per_body)
            arows = a_scr[pl.ds(tok0, per_body), :]
            bgrows = bgt_scr[pl.ds(tok0, per_body), :]
            for k in range(per_body):
                arow = jnp.broadcast_to(arows[k:k + 1], (2 * SUBLANES, LANES)).astype(BF16)[None]
                pa = jnp.where(a_of_row == arow, one, zero).reshape(N_KEYS, LANES)
                if k % row_major_every == 0:
                    bgrow = jnp.broadcast_to(bgrows[k:k + 1], (2 * SUBLANES, LANES))
                    brow = jnp.floor(bgrow)
                    grow = (bgrow - brow).astype(BF16)[None]
                    qb = jnp.where(b_of_row == brow.astype(BF16)[None], grow, zero).reshape(N_KEYS, LANES)
                    tile = lax.dot_general(pa, qb, NT_DIMS, preferred_element_type=F32)
                else:
                    bgcol = jnp.broadcast_to(bg[:, k:k + 1], (N_KEYS, LANES))
                    bcol = jnp.floor(bgcol)
                    gcol = bgcol - bcol
                    qbt = jnp.where(bcol == b_of_lane, gcol, 0.0).astype(BF16)
                    tile = jnp.dot(pa, qbt, preferred_element_type=F32)
                row0 = pl.multiple_of((tok0 + k) * G_PITCH, SUBLANES)
                gmat[pl.ds(row0, half_keys), :] = pltpu.pack_elementwise(
                    [tile[0:half_keys], tile[half_keys:N_KEYS]], packed_dtype=BF16)
            return carry

        lax.fori_loop(0, t // per_body, body, 0)

    unit = jnp.minimum(e, n_half * PEER_HEADS - 1)
    half = lax.shift_right_logical(unit, PEER_HEADS.bit_length() - 1)
    h = unit & (PEER_HEADS - 1)
    rd = e & 1
    blk = jnp.minimum(e, n_steps - 2)
    eb = u_ref.shape[0]
    n_stages = 4
    exp_w = eb // n_stages
    out_w = v_ref.shape[1] // n_stages
    words_per_stage = exp_w // (2 * N_KEYS)

    def value_piece(c):
        ocols = slice(c * out_w, (c + 1) * out_w)
        o_ref[:, ocols] += jnp.dot(coef_scr[rd], v_ref[:, ocols], preferred_element_type=F32)

    def act_piece(c):
        ecols = slice(c * exp_w, (c + 1) * exp_w)
        act = lax.dot_general(xn_scr[...], u_ref[ecols, :], NT_DIMS, preferred_element_type=F32)
        gates = []
        for w in range(words_per_stage):
            word = gmat[pl.ds((blk * n_stages + c) * words_per_stage + w, t, stride=G_PITCH), :]
            gates.append(lax.bitcast_convert_type(word << 16, F32))
            gates.append(lax.bitcast_convert_type(word & jnp.int32(-65536), F32))
        coef_scr[1 - rd, :, ecols] = (_scaled_gelu(act) * jnp.concatenate(gates, axis=1)).astype(BF16)

    def retrieval():
        lists = []
        for p in range(2):
            st = lax.dot_general(sk_ref[h, p], q_scr[half, 2 * h + p], NT_DIMS, preferred_element_type=F32)
            yield from _top16_of_keys(st, lists)
            yield
        cand, expert = _pair_candidates(lists[0], lists[1])
        second_level = _TopK(cand, payload=expert)
        for _ in range(4):
            second_level.step(PEER_TOPK // 4)
            yield
        best, e_sel = second_level.result()
        ex = jnp.exp(best - best[0:1])
        slot0 = pl.multiple_of(h * PEER_TOPK, PEER_TOPK)
        sel_e[cur, half, pl.ds(slot0, PEER_TOPK), :] = e_sel
        sel_g[cur, half, pl.ds(slot0, PEER_TOPK), :] = ex / jnp.sum(ex, axis=0, keepdims=True)

    pieces = [functools.partial(f, c) for c in range(n_stages) for f in (value_piece, act_piece)]
    parts = retrieval()
    parts_per_piece = 2
    for piece in pieces:
        piece()
        for _ in range(parts_per_piece):
            next(parts, None)
    for _ in parts:
        pass

    @pl.when(e == n_steps - 1)
    def _():
        project_queries()
        if out_refs:
            first_ref, second_ref = out_refs

            @pl.when(i - 1 < split_blk)
            def _():
                first_ref[...] = o_ref[...]

            @pl.when(i - 1 >= split_blk)
            def _():
                second_ref[...] = o_ref[...]


def _peer(x, g, wq, sk, u_all, v_all, layer, t=512, eb=1024, split_rows=None):
    n, d = x.shape
    nslot = PEER_HEADS * PEER_TOPK
    n_tok_blk = n // t
    n_exp_blk = N_KEYS * N_KEYS // eb
    n_half = t // SEL_T
    assert n_exp_blk == n_half * PEER_HEADS
    last_tok = n_tok_blk - 1
    once = pl.Buffered(1)
    if split_rows is None:
        split_blk = None
        out_specs = pl.BlockSpec((t, d), lambda i, e: (jnp.maximum(i - 1, 0), 0))
        out_shape = jax.ShapeDtypeStruct((n, d), F32)
        acc = []
    else:
        split_blk = split_rows // t
        out_specs = [pl.BlockSpec((t, d), lambda i, e: (jnp.clip(i - 1, 0, split_blk - 1), 0)),
                     pl.BlockSpec((t, d), lambda i, e: (jnp.clip(i - 1 - split_blk, 0, last_tok - split_blk), 0))]
        out_shape = [jax.ShapeDtypeStruct((split_rows, d), F32), jax.ShapeDtypeStruct((n - split_rows, d), F32)]
        acc = [pltpu.VMEM((t, d), F32)]
    return pl.pallas_call(
        functools.partial(_peer_fused_kernel, split_blk=split_blk),
        grid=(n_tok_blk + 1, n_exp_blk + 1),
        in_specs=[pl.BlockSpec((t, d), lambda i, e: (jnp.minimum(i + e // n_exp_blk, last_tok), 0),
                               pipeline_mode=once),
                  pl.BlockSpec((t, d), lambda i, e: (jnp.maximum(i - 1, 0), 0)),
                  pl.BlockSpec((1, d), lambda i, e: (0, 0)),
                  pl.BlockSpec((d, 2 * PEER_HEADS * LANES), lambda i, e: (0, 0), pipeline_mode=once),
                  pl.BlockSpec((PEER_HEADS, 2, N_KEYS, LANES), lambda i, e: (0, 0, 0, 0), pipeline_mode=once),
                  pl.BlockSpec((None, eb, d), lambda i, e: (layer, jnp.minimum(e, n_exp_blk - 1), 0)),
                  pl.BlockSpec((None, eb, d), lambda i, e: (layer, jnp.maximum(e - 1, 0), 0))],
        out_specs=out_specs,
        out_shape=out_shape,
        scratch_shapes=[pltpu.VMEM((n_half, 2 * PEER_HEADS, SEL_T, LANES), BF16),
                        pltpu.VMEM((2, n_half, nslot, SEL_T), F32),
                        pltpu.VMEM((2, n_half, nslot, SEL_T), F32),
                        pltpu.VMEM((t, d), BF16),
                        pltpu.VMEM((t, nslot), F32),
                        pltpu.VMEM((t, nslot), F32),
                        pltpu.VMEM((t // LANES, nslot, LANES), F32),
                        pltpu.VMEM((t * G_PITCH, LANES), jnp.int32),
                        pltpu.VMEM((2, t, eb), BF16)] + acc,
        compiler_params=_cparams(2),
        name="peer",
    )(x, x, g.reshape(1, d), wq, sk, u_all, v_all)


def _rope_tables(pos):
    half = HEAD_DIM // 2
    inv_freq = ROPE_THETA ** (-jnp.arange(half, dtype=F32) / half)
    ang = pos.astype(F32)[:, None] * inv_freq[None, :]
    cos, sin = jnp.cos(ang), jnp.sin(ang)
    reps = LANES // HEAD_DIM
    cos_t = jnp.tile(jnp.concatenate([cos, cos], axis=1), (1, reps))
    sin_t = jnp.tile(jnp.concatenate([-sin, sin], axis=1), (1, reps))
    return cos_t, sin_t


def kernel(x_prompt, x_sample, cache_conv, cache_win_k, cache_win_v, state_mlstm_C, state_mlstm_n,
           state_mlstm_m, norm_mix, norm_ffn, ab_w_in, ab_conv_w, ab_q_gain, ab_k_gain, ab_sinks, ab_w_out,
           ml_w_in, ml_gate_bias, ml_out_gain, ml_w_out, peer_w_q, peer_sub_keys, peer_u, peer_v):
    n_batch, seq, d = x_prompt.shape
    n_seq, t_len, _ = x_sample.shape
    n_prompt = n_batch * seq
    assert d == D_MODEL and t_len == SUBLANES and norm_mix.shape[0] == 2

    xp = x_prompt.reshape(n_prompt, d)
    xs = x_sample.reshape(n_seq * t_len, d)

    cos_p, sin_p = _rope_tables(jnp.arange(seq, dtype=jnp.int32))
    cos_s, sin_s = _rope_tables(PAST_LEN + jnp.arange(t_len, dtype=jnp.int32))
    bt = 16
    cos_s, sin_s = jnp.tile(cos_s, (bt, 1)), jnp.tile(sin_s, (bt, 1))
    lane = jnp.arange(LANES)
    seg = (lane[:, None] // HEAD_DIM == lane[None, :] // HEAD_DIM).astype(BF16)
    reps = LANES // HEAD_DIM
    qg = jnp.tile(ab_q_gain[0], reps).reshape(1, LANES)
    kg = jnp.tile(ab_k_gain[0], reps).reshape(1, LANES)
    wo_ab = ab_w_out[0].astype(BF16)

    z = _norm_proj([xp, xs], norm_mix[0], ab_w_in[0].astype(BF16))
    y_p, k_p, v_p, c_p = _ab_prompt(z, xp, n_batch, seq, cos_p, sin_p, ab_conv_w[0], qg, kg, ab_sinks[0],
                                    seg, wo_ab)
    y_s, c_s, k_s, v_s = _ab_sample(z, xs, n_prompt, n_seq, t_len, cos_s, sin_s, ab_conv_w[0], qg, kg,
                                    ab_sinks[0], seg, wo_ab, cache_conv[0],
                                    cache_win_k[0].reshape(n_seq, WINDOW, LANES),
                                    cache_win_v[0].reshape(n_seq, WINDOW, LANES), bt=bt)
    x = jnp.concatenate([y_p.reshape(n_prompt, d), y_s], axis=0)
    u_all = (peer_u * RSQRT2).astype(BF16)
    v_all = peer_v.astype(BF16)
    x = _peer(x, norm_ffn[0], peer_w_q[0].astype(BF16), peer_sub_keys[0].astype(BF16), u_all, v_all, 0)

    n_gate = 2 * ML_HEADS
    w_in = jnp.pad(ml_w_in[0], ((0, 0), (0, ML_IN_PAD - ml_w_in.shape[2]))).astype(BF16)
    bias = jnp.pad(ml_gate_bias[0], (0, LANES - n_gate)).reshape(1, LANES)
    og = ml_out_gain[0].reshape(1, D_MODEL)
    idx = jnp.arange(ML_CHUNK)
    tril = (idx[None, :] <= idx[:, None]).astype(BF16)
    wo_ml = ml_w_out[0].astype(BF16)

    z = _norm_proj([x], norm_mix[1], w_in)
    y_p, cm_p, nm_p, mm_p = _mlstm_prompt(z, x, n_batch, seq, bias, og, tril, wo_ml)
    y_s, cm_s, nm_s, mm_s = _mlstm_sample(z, x, n_prompt, n_seq, t_len, bias, og, tril, wo_ml,
                                          state_mlstm_C[0], state_mlstm_n[0],
                                          state_mlstm_m[0].reshape(n_seq, 1, ML_HEADS))
    x = jnp.concatenate([y_p.reshape(n_prompt, d), y_s], axis=0)
    out_p, out_s = _peer(x, norm_ffn[1], peer_w_q[1].astype(BF16), peer_sub_keys[1].astype(BF16), u_all, v_all, 1,
                         split_rows=n_prompt)

    y_prompt = out_p.reshape(n_batch, seq, d)
    y_sample = out_s.reshape(n_seq, t_len, d)
    kv_shape_p = (1, n_batch, WINDOW, N_KV_HEADS, HEAD_DIM)
    kv_shape_s = (1, n_seq, WINDOW, N_KV_HEADS, HEAD_DIM)
    return (y_prompt, y_sample,
            c_p[:, SUBLANES - 2:, :][None], k_p.reshape(kv_shape_p), v_p.reshape(kv_shape_p),
            cm_p[None], nm_p[None], mm_p.reshape(1, n_batch, ML_HEADS),
            c_s[None], k_s.reshape(kv_shape_s), v_s.reshape(kv_shape_s),
            cm_s[None], nm_s[None], mm_s.reshape(1, n_seq, ML_HEADS))
```

```python
import functools

import jax
import jax.numpy as jnp
from jax import lax
from jax.experimental import pallas as pl
from jax.experimental.pallas import tpu as pltpu

F32 = jnp.float32
BF16 = jnp.bfloat16
EPS = 1e-6

D_MODEL = 1024
CONV_DIM = 512
N_Q_HEADS = 8
N_KV_HEADS = 2
HEAD_DIM = 64
WINDOW = 128
ROPE_THETA = 10000.0
AB_IN = 2304
ML_HEADS = 4
ML_QK = 128
ML_V = 256
ML_CHUNK = 128
ML_GATE_COL = 3072
ML_IN_PAD = ML_GATE_COL + 128
N_KEYS = 128
PEER_HEADS = 8
PEER_TOPK = 16
PAST_LEN = 16384

LANES = 128
SUBLANES = 8
G_PITCH = N_KEYS // 2 + SUBLANES
VMEM_LIMIT = 56 * 1024 * 1024

NT_DIMS = (((1,), (1,)), ((), ()))
TN_DIMS = (((0,), (0,)), ((), ()))


def _cparams(n_axes, vmem=VMEM_LIMIT):
    return pltpu.CompilerParams(dimension_semantics=("arbitrary",) * n_axes, vmem_limit_bytes=vmem)


def _rmsnorm(x, g):
    return x * lax.rsqrt(jnp.mean(x * x, axis=-1, keepdims=True) + EPS) * g


def _bf16_pieces(a, terms):
    pieces = []
    rem = a
    for _ in range(terms):
        piece = rem.astype(BF16)
        rem = rem - piece.astype(F32)
        pieces.append(piece)
    return pieces


def _split_dot(a, b_bf16, terms=2):
    return sum(jnp.dot(p, b_bf16, preferred_element_type=F32) for p in _bf16_pieces(a, terms))


def _norm_proj_kernel(*refs, starts):
    x_refs = refs[:len(starts)]
    g_ref, w_ref, o_ref = refs[len(starts):]
    i = pl.program_id(0)
    x = x_refs[0][...]
    for x_ref, start in zip(x_refs[1:], starts[1:]):
        x = jnp.where(i >= start, x_ref[...], x)
    r = _rmsnorm(x, g_ref[...])
    o_ref[...] = jnp.dot(r.astype(BF16), w_ref[...], preferred_element_type=F32)


def _norm_proj(x_parts, g, w_bf16, tm=512):
    d = x_parts[0].shape[1]
    nout = w_bf16.shape[1]
    blocks = [p.shape[0] // tm for p in x_parts]
    starts = tuple(sum(blocks[:k]) for k in range(len(blocks)))
    part_spec = lambda start, nblk: pl.BlockSpec((tm, d), lambda i: (jnp.clip(i - start, 0, nblk - 1), 0))
    return pl.pallas_call(
        functools.partial(_norm_proj_kernel, starts=starts),
        grid=(sum(blocks),),
        in_specs=[part_spec(s, nb) for s, nb in zip(starts, blocks)] + [
            pl.BlockSpec((1, d), lambda i: (0, 0)),
            pl.BlockSpec((d, nout), lambda i: (0, 0))],
        out_specs=pl.BlockSpec((tm, nout), lambda i: (i, 0)),
        out_shape=jax.ShapeDtypeStruct((sum(blocks) * tm, nout), F32),
        compiler_params=_cparams(1),
        name="norm_proj",
    )(*x_parts, g.reshape(1, d), w_bf16)


def _headnorm_rope(xc, gain, cos, sin, seg, hi_half):
    ss = _split_dot(xc * xc, seg)
    xn = xc * lax.rsqrt(ss * (1.0 / HEAD_DIM) + EPS) * gain
    partner = jnp.where(hi_half, pltpu.roll(xn, 32, 1), pltpu.roll(xn, 96, 1))
    return xn * cos + partner * sin


def _softmax_sink(s, mask, sink):
    s = jnp.where(mask, s, -1e30)
    m = jnp.maximum(jnp.max(s, axis=-1, keepdims=True), sink)
    p = jnp.exp(s - m)
    denom = jnp.sum(p, axis=-1, keepdims=True) + jnp.exp(sink - m)
    return (p / denom).astype(BF16)


def _ab_prompt_kernel(sink_ref, *refs, n_batch):
    z_refs, x_refs = refs[0:n_batch], refs[n_batch:2 * n_batch]
    (cos_ref, sin_ref, cw_ref, qg_ref, kg_ref, seg_ref, wo_ref,
     y_ref, kst_ref, vst_ref, cst_ref, pk_ref, pv_ref, pu_ref) = refs[2 * n_batch:]
    j = pl.program_id(0)

    @pl.when(j == 0)
    def _():
        pk_ref[...] = jnp.zeros_like(pk_ref)
        pv_ref[...] = jnp.zeros_like(pv_ref)
        pu_ref[...] = jnp.zeros_like(pu_ref)

    for b in range(n_batch):
        _ab_prompt_block(j, sink_ref, z_refs[b], x_refs[b], cos_ref, sin_ref, cw_ref, qg_ref, kg_ref, seg_ref,
                         wo_ref, y_ref.at[b], kst_ref.at[b], vst_ref.at[b], cst_ref.at[b],
                         pk_ref.at[b], pv_ref.at[b], pu_ref.at[b])


def _ab_prompt_block(j, sink_ref, z_ref, x_ref, cos_ref, sin_ref, cw_ref, qg_ref, kg_ref, seg_ref, wo_ref,
                     y_ref, kst_ref, vst_ref, cst_ref, pk_ref, pv_ref, pu_ref):
    blk = z_ref.shape[0]
    cos = cos_ref[...]
    sin = sin_ref[...]
    seg = seg_ref[...]
    hi_half = (lax.broadcasted_iota(jnp.int32, (blk, LANES), 1) & 32) != 0

    gate_b = z_ref[:, 0:CONV_DIM]
    u = z_ref[:, CONV_DIM:2 * CONV_DIM] * z_ref[:, 2 * CONV_DIM:3 * CONV_DIM]
    ng = blk // SUBLANES
    u3 = u.reshape(ng, SUBLANES, CONV_DIM)
    ext = jnp.concatenate([pu_ref[...][None], u3], axis=0)
    t8 = lax.broadcasted_iota(jnp.int32, (ng, SUBLANES, CONV_DIM), 1)
    r1 = pltpu.roll(ext, 1, 1)
    r2 = pltpu.roll(ext, 2, 1)
    um1 = jnp.where(t8 >= 1, r1[1:], r1[:-1])
    um2 = jnp.where(t8 >= 2, r2[1:], r2[:-1])
    cw = cw_ref[...]
    conv = cw[0:1][None] * um2 + cw[1:2][None] * um1 + cw[2:3][None] * u3
    yconv = gate_b * conv.reshape(blk, CONV_DIM)

    q0 = 3 * CONV_DIM
    k0 = q0 + N_Q_HEADS * HEAD_DIM
    v0 = k0 + N_KV_HEADS * HEAD_DIM
    qg = qg_ref[...]
    qr = [_headnorm_rope(z_ref[:, q0 + c * LANES:q0 + (c + 1) * LANES], qg, cos, sin, seg, hi_half)
          for c in range(N_Q_HEADS * HEAD_DIM // LANES)]
    kr = _headnorm_rope(z_ref[:, k0:k0 + LANES], kg_ref[...], cos, sin, seg, hi_half)
    v = z_ref[:, v0:v0 + LANES]
    pk = pk_ref[...]
    pv = pv_ref[...]

    row = lax.broadcasted_iota(jnp.int32, (blk, 2 * blk), 0)
    col = lax.broadcasted_iota(jnp.int32, (blk, 2 * blk), 1)
    row_prev = row + jnp.where(j == 0, 2 * blk, 0)
    mask = ((col < blk) & (col > row_prev)) | ((col >= blk) & (col - blk <= row))

    kks, vvs = [], []
    for g in range(N_KV_HEADS):
        sl = slice(g * HEAD_DIM, (g + 1) * HEAD_DIM)
        kks.append(jnp.concatenate([pk[:, sl], kr[:, sl]], axis=0).astype(BF16))
        vvs.append(jnp.concatenate([pv[:, sl], v[:, sl]], axis=0).astype(BF16))
    outs = []
    for h in range(N_Q_HEADS):
        g = h // (N_Q_HEADS // N_KV_HEADS)
        qh = qr[h // 2][:, (h % 2) * HEAD_DIM:(h % 2 + 1) * HEAD_DIM].astype(BF16)
        s = lax.dot_general(qh, kks[g], NT_DIMS, preferred_element_type=F32) * (HEAD_DIM ** -0.5)
        p = _softmax_sink(s, mask, sink_ref[h])
        outs.append(jnp.dot(p, vvs[g], preferred_element_type=F32))
    attn = jnp.concatenate(outs, axis=1)

    y = (jnp.dot(yconv.astype(BF16), wo_ref[0:CONV_DIM, :], preferred_element_type=F32)
         + jnp.dot(attn.astype(BF16), wo_ref[CONV_DIM:2 * CONV_DIM, :], preferred_element_type=F32))
    y_ref[...] = y + x_ref[...]

    pk_ref[...] = kr
    pv_ref[...] = v
    pu_ref[...] = u3[ng - 1]
    kst_ref[...] = kr
    vst_ref[...] = v
    cst_ref[...] = u3[ng - 1]


def _ab_prompt(z, x, n_batch, seq, cos, sin, cw, qg, kg, sinks, seg, wo):
    blk = WINDOW
    nb = seq // blk
    const = lambda j, s: (0, 0)
    whole = lambda j, s: (0, 0, 0)
    tok_specs = lambda width: [pl.BlockSpec((blk, width), functools.partial(lambda j, s, b: (b * nb + j, 0), b=b))
                               for b in range(n_batch)]
    grid_spec = pltpu.PrefetchScalarGridSpec(
        num_scalar_prefetch=1,
        grid=(nb,),
        in_specs=tok_specs(AB_IN) + tok_specs(D_MODEL) + [
            pl.BlockSpec((blk, LANES), lambda j, s: (j, 0)),
            pl.BlockSpec((blk, LANES), lambda j, s: (j, 0)),
            pl.BlockSpec((3, CONV_DIM), const),
            pl.BlockSpec((1, LANES), const),
            pl.BlockSpec((1, LANES), const),
            pl.BlockSpec((LANES, LANES), const),
            pl.BlockSpec((D_MODEL, D_MODEL), const)],
        out_specs=[pl.BlockSpec((n_batch, blk, D_MODEL), lambda j, s: (0, j, 0)),
                   pl.BlockSpec((n_batch, blk, LANES), whole),
                   pl.BlockSpec((n_batch, blk, LANES), whole),
                   pl.BlockSpec((n_batch, SUBLANES, CONV_DIM), whole)],
        scratch_shapes=[pltpu.VMEM((n_batch, blk, LANES), F32), pltpu.VMEM((n_batch, blk, LANES), F32),
                        pltpu.VMEM((n_batch, SUBLANES, CONV_DIM), F32)])
    return pl.pallas_call(
        functools.partial(_ab_prompt_kernel, n_batch=n_batch),
        grid_spec=grid_spec,
        out_shape=[jax.ShapeDtypeStruct((n_batch, seq, D_MODEL), F32),
                   jax.ShapeDtypeStruct((n_batch, blk, LANES), F32),
                   jax.ShapeDtypeStruct((n_batch, blk, LANES), F32),
                   jax.ShapeDtypeStruct((n_batch, SUBLANES, CONV_DIM), F32)],
        compiler_params=_cparams(1),
        name="ab_prompt",
    )(sinks, *([z] * n_batch), *([x] * n_batch), cos, sin, cw, qg, kg, seg, wo)


def _ab_sample_kernel(sink_ref, z_ref, x_ref, cos_ref, sin_ref, cw_ref, qg_ref, kg_ref, seg_ref, wo_ref,
                      cc_ref, ck_ref, cv_ref, y_ref, cs_ref, ks_ref, vs_ref):
    rows = z_ref.shape[0]
    t_len = SUBLANES
    bt = rows // t_len
    cos = cos_ref[...]
    sin = sin_ref[...]
    seg = seg_ref[...]
    hi_half = (lax.broadcasted_iota(jnp.int32, (rows, LANES), 1) & 32) != 0

    gate_b = z_ref[:, 0:CONV_DIM]
    u = z_ref[:, CONV_DIM:2 * CONV_DIM] * z_ref[:, 2 * CONV_DIM:3 * CONV_DIM]
    u3 = u.reshape(bt, t_len, CONV_DIM)
    cc = cc_ref[...]
    c0 = cc[:, 0:1, :]
    c1 = cc[:, 1:2, :]
    t8 = lax.broadcasted_iota(jnp.int32, (bt, t_len, CONV_DIM), 1)
    r1 = pltpu.roll(u3, 1, 1)
    r2 = pltpu.roll(u3, 2, 1)
    um1 = jnp.where(t8 >= 1, r1, c1)
    um2 = jnp.where(t8 >= 2, r2, jnp.where(t8 == 1, c1, c0))
    cw = cw_ref[...]
    conv = cw[0:1][None] * um2 + cw[1:2][None] * um1 + cw[2:3][None] * u3
    yconv = gate_b * conv.reshape(rows, CONV_DIM)
    cs_ref[...] = r2[:, 0:2, :]

    q0 = 3 * CONV_DIM
    k0 = q0 + N_Q_HEADS * HEAD_DIM
    v0 = k0 + N_KV_HEADS * HEAD_DIM
    qg = qg_ref[...]
    qr = [_headnorm_rope(z_ref[:, q0 + c * LANES:q0 + (c + 1) * LANES], qg, cos, sin, seg, hi_half)
          for c in range(N_Q_HEADS * HEAD_DIM // LANES)]
    kr = _headnorm_rope(z_ref[:, k0:k0 + LANES], kg_ref[...], cos, sin, seg, hi_half)
    v = z_ref[:, v0:v0 + LANES]

    group = N_Q_HEADS // N_KV_HEADS
    nq = group * t_len
    nk = 2 * WINDOW
    qrow = lax.broadcasted_iota(jnp.int32, (nq, nk), 0)
    t_q = qrow & (t_len - 1)
    col = lax.broadcasted_iota(jnp.int32, (nq, nk), 1)
    mask = (((col < WINDOW) & (col > t_q)) | ((col >= WINDOW) & (col - WINDOW <= t_q)))[None]
    hrow = lax.broadcasted_iota(jnp.int32, (nq, 1), 0) // t_len
    pad = jnp.zeros((bt, nk - WINDOW - t_len, HEAD_DIM), F32)

    outs = [None] * N_Q_HEADS
    for g in range(N_KV_HEADS):
        sl = slice(g * HEAD_DIM, (g + 1) * HEAD_DIM)
        qs = jnp.concatenate(
            [qr[h // 2][:, (h % 2) * HEAD_DIM:(h % 2 + 1) * HEAD_DIM].reshape(bt, t_len, HEAD_DIM)
             for h in range(g * group, (g + 1) * group)], axis=1)
        kk = jnp.concatenate([ck_ref[:, :, sl], kr[:, sl].reshape(bt, t_len, HEAD_DIM), pad], axis=1)
        vv = jnp.concatenate([cv_ref[:, :, sl], v[:, sl].reshape(bt, t_len, HEAD_DIM), pad], axis=1)
        s = jnp.einsum('bqd,bkd->bqk', qs.astype(BF16), kk.astype(BF16),
                       preferred_element_type=F32) * (HEAD_DIM ** -0.5)
        sink = jnp.zeros((nq, 1), F32)
        for hh in range(group):
            sink = jnp.where(hrow == hh, sink_ref[g * group + hh], sink)
        p = _softmax_sink(s, mask, sink[None])
        o = jnp.einsum('bqk,bkd->bqd', p, vv.astype(BF16), preferred_element_type=F32)
        for hh in range(group):
            outs[g * group + hh] = o[:, hh * t_len:(hh + 1) * t_len, :].reshape(rows, HEAD_DIM)
    attn = jnp.concatenate(outs, axis=1)

    y = (jnp.dot(yconv.astype(BF16), wo_ref[0:CONV_DIM, :], preferred_element_type=F32)
         + jnp.dot(attn.astype(BF16), wo_ref[CONV_DIM:2 * CONV_DIM, :], preferred_element_type=F32))
    y_ref[...] = y + x_ref[...]

    keep = WINDOW - t_len
    ks_ref[:, 0:keep, :] = ck_ref[:, t_len:WINDOW, :]
    ks_ref[:, keep:WINDOW, :] = kr.reshape(bt, t_len, LANES)
    vs_ref[:, 0:keep, :] = cv_ref[:, t_len:WINDOW, :]
    vs_ref[:, keep:WINDOW, :] = v.reshape(bt, t_len, LANES)


def _ab_sample(z, x, row0, n_seq, t_len, cos, sin, cw, qg, kg, sinks, seg, wo, cc, ck, cv, bt=16):
    rows = bt * t_len
    blk0 = row0 // rows
    tok = lambda i, s: (blk0 + i, 0)
    const = lambda i, s: (0, 0)
    seq3 = lambda i, s: (i, 0, 0)
    grid_spec = pltpu.PrefetchScalarGridSpec(
        num_scalar_prefetch=1,
        grid=(n_seq // bt,),
        in_specs=[pl.BlockSpec((rows, AB_IN), tok),
                  pl.BlockSpec((rows, D_MODEL), lambda i, s: (i, 0)),
                  pl.BlockSpec((rows, LANES), const),
                  pl.BlockSpec((rows, LANES), const),
                  pl.BlockSpec((3, CONV_DIM), const),
                  pl.BlockSpec((1, LANES), const),
                  pl.BlockSpec((1, LANES), const),
                  pl.BlockSpec((LANES, LANES), const),
                  pl.BlockSpec((D_MODEL, D_MODEL), const),
                  pl.BlockSpec((bt, 2, CONV_DIM), seq3),
                  pl.BlockSpec((bt, WINDOW, LANES), seq3),
                  pl.BlockSpec((bt, WINDOW, LANES), seq3)],
        out_specs=[pl.BlockSpec((rows, D_MODEL), lambda i, s: (i, 0)),
                   pl.BlockSpec((bt, 2, CONV_DIM), seq3),
                   pl.BlockSpec((bt, WINDOW, LANES), seq3),
                   pl.BlockSpec((bt, WINDOW, LANES), seq3)])
    return pl.pallas_call(
        _ab_sample_kernel,
        grid_spec=grid_spec,
        out_shape=[jax.ShapeDtypeStruct((n_seq * t_len, D_MODEL), F32),
                   jax.ShapeDtypeStruct((n_seq, 2, CONV_DIM), F32),
                   jax.ShapeDtypeStruct((n_seq, WINDOW, LANES), F32),
                   jax.ShapeDtypeStruct((n_seq, WINDOW, LANES), F32)],
        compiler_params=_cparams(1),
        name="ab_sample",
    )(sinks, z, x, cos, sin, cw, qg, kg, seg, wo, cc, ck, cv)


def _log_sigmoid(x):
    return jnp.minimum(x, 0.0) - jnp.log(1.0 + jnp.exp(-jnp.abs(x)))


def _mlstm_chunk(z, bias, og, tril, c_src, n_src, m_src, c_dst, n_dst, m_dst, n_real, rows):
    L = z.shape[0]
    gates = z[:, ML_GATE_COL:ML_GATE_COL + LANES] + bias
    if n_real < L:
        live = lax.broadcasted_iota(jnp.int32, (L, LANES), 0) < n_real
        li_all = jnp.where(live, gates, -1e30)
        lf_all = jnp.where(live, _log_sigmoid(gates), 0.0)
    else:
        li_all = gates
        lf_all = _log_sigmoid(gates)
    lf_pieces = _bf16_pieces(lf_all, 3)
    f_col_all = sum(jnp.dot(tril[0:rows], p, preferred_element_type=F32) for p in lf_pieces)
    f_row_all = sum(lax.dot_general(p, tril, (((0,), (1,)), ((), ())), preferred_element_type=F32)
                    for p in lf_pieces)
    li_t = li_all.T
    rr = lax.broadcasted_iota(jnp.int32, (rows, L), 0)
    cc = lax.broadcasted_iota(jnp.int32, (rows, L), 1)
    causal = cc <= rr

    outs, m_new_all = [], []
    for h in range(ML_HEADS):
        f_col = f_col_all[:, ML_HEADS + h:ML_HEADS + h + 1]
        f_row = f_row_all[ML_HEADS + h:ML_HEADS + h + 1, :]
        li_row = li_t[h:h + 1, :]
        li_col = li_all[0:rows, h:h + 1]
        m0 = m_src[0:1, h:h + 1]
        c0 = c_src[h]
        n0 = n_src[h:h + 1, :]
        qh = z[0:rows, h * ML_QK:(h + 1) * ML_QK]
        kh = z[:, ML_HEADS * ML_QK + h * ML_QK:ML_HEADS * ML_QK + (h + 1) * ML_QK] * (ML_QK ** -0.5)
        v_off = 2 * ML_HEADS * ML_QK
        vh = z[:, v_off + h * ML_V:v_off + (h + 1) * ML_V]
        o_off = v_off + ML_HEADS * ML_V
        oh = z[0:rows, o_off + h * ML_V:o_off + (h + 1) * ML_V]
        qb = qh.astype(BF16)
        vb = vh.astype(BF16)

        dmat = jnp.where(causal, f_col - f_row + li_row, -jnp.inf)
        gcar = f_col + m0
        m_t = jnp.maximum(jnp.max(dmat, axis=-1, keepdims=True), gcar)
        w = jnp.exp(dmat - m_t)
        s = lax.dot_general(qb, kh.astype(BF16), NT_DIMS, preferred_element_type=F32) * w
        carry = jnp.exp(gcar - m_t)
        num = (jnp.dot(s.astype(BF16), vb, preferred_element_type=F32)
               + jnp.dot(qb, c0.astype(BF16), preferred_element_type=F32) * carry)
        den = jnp.sum(s, axis=-1, keepdims=True) + carry * jnp.sum(qh * n0, axis=-1, keepdims=True)
        hout = num / jnp.maximum(jnp.abs(den), jnp.exp(-m_t))

        f_last = f_col[n_real - 1:n_real, :]
        w_end = f_last - f_col + li_col
        m_new = jnp.maximum(f_last + m0, jnp.max(w_end, axis=0, keepdims=True))
        a_end = jnp.exp(w_end - m_new)
        scale = jnp.exp(f_last + m0 - m_new)
        ka = kh[0:rows] * a_end
        c_dst[h] = scale * c0 + lax.dot_general(ka.astype(BF16), vb[0:rows], TN_DIMS,
                                                preferred_element_type=F32)
        n_dst[h:h + 1, :] = scale * n0 + jnp.sum(ka, axis=0, keepdims=True)
        m_new_all.append(m_new)

        hn = _rmsnorm(hout, og[:, h * ML_V:(h + 1) * ML_V])
        outs.append(jax.nn.sigmoid(oh) * hn)
    m_dst[...] = jnp.concatenate(m_new_all, axis=1)
    return jnp.concatenate(outs, axis=1)


def _mlstm_prompt_kernel(*refs, n_batch):
    z_refs, x_refs = refs[0:n_batch], refs[n_batch:2 * n_batch]
    bias_ref, og_ref, tril_ref, wo_ref, y_ref, c_ref, n_ref, m_ref = refs[2 * n_batch:]

    @pl.when(pl.program_id(0) == 0)
    def _():
        c_ref[...] = jnp.zeros_like(c_ref)
        n_ref[...] = jnp.zeros_like(n_ref)
        m_ref[...] = jnp.zeros_like(m_ref)

    for b in range(n_batch):
        state = (c_ref.at[b], n_ref.at[b], m_ref.at[b])
        out = _mlstm_chunk(z_refs[b][...], bias_ref[...], og_ref[...], tril_ref[...], *state, *state,
                           ML_CHUNK, ML_CHUNK)
        y_ref[b] = jnp.dot(out.astype(BF16), wo_ref[...], preferred_element_type=F32) + x_refs[b][...]


def _mlstm_sample_kernel(z_ref, x_ref, bias_ref, og_ref, tril_ref, wo_ref, c0_ref, n0_ref, m0_ref,
                         y_ref, c_ref, n_ref, m_ref, *, t_len):
    n_here = z_ref.shape[0] // t_len
    q_rows = 2 * SUBLANES
    outs = []
    for s in range(n_here):
        zpad = jnp.concatenate([z_ref[s * t_len:(s + 1) * t_len, :],
                                jnp.zeros((ML_CHUNK - t_len, ML_IN_PAD), F32)], axis=0)
        out = _mlstm_chunk(zpad, bias_ref[...], og_ref[...], tril_ref[...],
                           c0_ref.at[s], n0_ref.at[s], m0_ref.at[s], c_ref.at[s], n_ref.at[s], m_ref.at[s],
                           t_len, q_rows)
        outs.append(out[0:t_len])
    out_all = jnp.concatenate(outs, axis=0).astype(BF16)
    y_ref[...] = jnp.dot(out_all, wo_ref[...], preferred_element_type=F32) + x_ref[...]


def _mlstm_weight_specs(const):
    return [pl.BlockSpec((1, LANES), const),
            pl.BlockSpec((1, D_MODEL), const),
            pl.BlockSpec((ML_CHUNK, ML_CHUNK), const),
            pl.BlockSpec((D_MODEL, D_MODEL), const)]


def _mlstm_prompt(z, x, n_batch, seq, bias, og, tril, wo):
    nc = seq // ML_CHUNK
    const = lambda j: (0, 0)
    tok_specs = lambda width: [pl.BlockSpec((ML_CHUNK, width), functools.partial(lambda j, b: (b * nc + j, 0), b=b))
                               for b in range(n_batch)]
    return pl.pallas_call(
        functools.partial(_mlstm_prompt_kernel, n_batch=n_batch),
        grid=(nc,),
        in_specs=tok_specs(ML_IN_PAD) + tok_specs(D_MODEL) + _mlstm_weight_specs(const),
        out_specs=[pl.BlockSpec((n_batch, ML_CHUNK, D_MODEL), lambda j: (0, j, 0)),
                   pl.BlockSpec((n_batch, ML_HEADS, ML_QK, ML_V), lambda j: (0, 0, 0, 0)),
                   pl.BlockSpec((n_batch, ML_HEADS, ML_QK), lambda j: (0, 0, 0)),
                   pl.BlockSpec((n_batch, 1, ML_HEADS), lambda j: (0, 0, 0))],
        out_shape=[jax.ShapeDtypeStruct((n_batch, seq, D_MODEL), F32),
                   jax.ShapeDtypeStruct((n_batch, ML_HEADS, ML_QK, ML_V), F32),
                   jax.ShapeDtypeStruct((n_batch, ML_HEADS, ML_QK), F32),
                   jax.ShapeDtypeStruct((n_batch, 1, ML_HEADS), F32)],
        compiler_params=_cparams(1),
        name="mlstm_prompt",
    )(*([z] * n_batch), *([x] * n_batch), bias, og, tril, wo)


def _mlstm_sample(z, x, row0, n_seq, t_len, bias, og, tril, wo, c0, n0, m0, seqs_per_step=8):
    rows = seqs_per_step * t_len
    blk0 = row0 // rows
    tok = lambda i: (blk0 + i, 0)
    const = lambda i: (0, 0)
    st4 = lambda i: (i, 0, 0, 0)
    st3 = lambda i: (i, 0, 0)
    state_specs = [pl.BlockSpec((seqs_per_step, ML_HEADS, ML_QK, ML_V), st4),
                   pl.BlockSpec((seqs_per_step, ML_HEADS, ML_QK), st3),
                   pl.BlockSpec((seqs_per_step, 1, ML_HEADS), st3)]
    return pl.pallas_call(
        functools.partial(_mlstm_sample_kernel, t_len=t_len),
        grid=(n_seq // seqs_per_step,),
        in_specs=[pl.BlockSpec((rows, ML_IN_PAD), tok),
                  pl.BlockSpec((rows, D_MODEL), tok)] + _mlstm_weight_specs(const) + state_specs,
        out_specs=[pl.BlockSpec((rows, D_MODEL), lambda i: (i, 0))] + state_specs,
        out_shape=[jax.ShapeDtypeStruct((n_seq * t_len, D_MODEL), F32),
                   jax.ShapeDtypeStruct((n_seq, ML_HEADS, ML_QK, ML_V), F32),
                   jax.ShapeDtypeStruct((n_seq, ML_HEADS, ML_QK), F32),
                   jax.ShapeDtypeStruct((n_seq, 1, ML_HEADS), F32)],
        compiler_params=_cparams(1),
        name="mlstm_sample",
    )(z, x, bias, og, tril, wo, c0, n0, m0)


RSQRT2 = 0.7071067811865476


def _scaled_gelu(xs):
    return xs * (1.0 + lax.erf(xs))


SEL_T = 256


def _batcher_pairs(n):
    pairs = []
    p = 1
    while p < n:
        k = p
        while k >= 1:
            for j in range(k % p, n - k, 2 * k):
                for i in range(min(k, n - j - k)):
                    if (i + j) // (2 * p) == (i + j + k) // (2 * p):
                        pairs.append((i + j, i + j + k))
            k //= 2
        p *= 2
    return pairs


def _bitonic_merge_pairs(n):
    pairs = []
    s = n // 2
    while s >= 1:
        pairs += [(i, i + s) for i in range(n) if not i & s]
        s //= 2
    return pairs


def _top16_of_keys(s, out):
    n_vreg = N_KEYS // SUBLANES
    assert n_vreg == PEER_TOPK and N_KEYS == 128
    rounded = (s + 0.0).astype(BF16).astype(F32)
    b16 = lax.shift_right_logical(lax.bitcast_convert_type(rounded, jnp.int32), 16)
    code = jnp.where(b16 >= 0x8000, b16 ^ 0xFFFF, b16 | 0x8000)
    row = lax.broadcasted_iota(jnp.int32, s.shape, 0)
    keys = lax.bitcast_convert_type((code * N_KEYS + (N_KEYS - 1 - row)) | 0x4B000000, F32)

    x = [keys[v * SUBLANES:(v + 1) * SUBLANES] for v in range(n_vreg)]

    def exchange(i, j):
        x[i], x[j] = jnp.maximum(x[i], x[j]), jnp.minimum(x[i], x[j])

    for i, j in _batcher_pairs(n_vreg):
        exchange(i, j)
    yield
    for shift in (SUBLANES // 2, SUBLANES // 4, SUBLANES // 8):
        other = [pltpu.roll(a, shift, 0) for a in x]
        x = [jnp.maximum(x[i], other[n_vreg - 1 - i]) for i in range(n_vreg)]
        for i, j in _bitonic_merge_pairs(n_vreg):
            exchange(i, j)
        yield
    top = lax.bitcast_convert_type(jnp.concatenate([a[0:1] for a in x], axis=0), jnp.int32) & 0x7FFFFF
    rows = (N_KEYS - 1 - (top & (N_KEYS - 1))).astype(F32)
    code = lax.shift_right_logical(top, 7)
    b16 = jnp.where(code >= 0x8000, code ^ 0x8000, code ^ 0xFFFF)
    out.append((lax.bitcast_convert_type(b16 << 16, F32), rows))


class _TopK:
    def __init__(self, s, payload=None):
        self.s = s
        self.payload = payload
        self.rows = lax.broadcasted_iota(jnp.int32, s.shape, 0).astype(F32).astype(s.dtype)
        self.vals, self.picks = [], []

    def step(self, n):
        dt = self.s.dtype
        bound = jnp.asarray(self.s.shape[0], dt)
        for _ in range(n):
            m = jnp.max(self.s, axis=0, keepdims=True)
            first = jnp.min(jnp.where(self.s == m, self.rows, bound), axis=0, keepdims=True)
            sel = self.rows == first
            self.vals.append(m)
            if self.payload is None:
                self.picks.append(first)
            else:
                self.picks.append(jnp.max(jnp.where(sel, self.payload, -1.0), axis=0, keepdims=True))
            self.s = jnp.where(sel, jnp.asarray(-jnp.inf, dt), self.s)

    def result(self):
        return jnp.concatenate(self.vals, axis=0), jnp.concatenate(self.picks, axis=0)


_PAIR_GROUPS = (
    ((0, 1, 0, 8),),
    ((0, 2, 0, 5), (5, 3, 0, 3)),
    ((0, 3, 3, 1), (1, 4, 0, 3), (4, 5, 0, 2), (6, 6, 0, 2)),
    ((0, 7, 0, 2), (2, 8, 0, -6)),
    ((0, 14, 0, -2),),
)


def _pair_candidates(first, second):
    h8 = SUBLANES
    (v1, i1), (v2, i2) = [tuple(a.astype(F32) for a in lst) for lst in (first, second)]
    row = lax.broadcasted_iota(jnp.int32, (h8, v1.shape[1]), 0)

    def build(a, b, combine, filler):
        groups = [combine(a[0:1], b)]
        for segments in _PAIR_GROUPS:
            group = jnp.full((h8, a.shape[1]), filler, F32)
            for r0, k1, k2, count in segments:
                if count > 0:
                    seg = combine(a[k1:k1 + 1], pltpu.roll(b[0:h8], (r0 - k2) % h8, 0))
                else:
                    seg = combine(pltpu.roll(a[h8:2 * h8], (r0 - (k1 - h8)) % h8, 0), b[0:1])
                group = jnp.where((row >= r0) & (row < r0 + abs(count)), seg, group)
            groups.append(group)
        return jnp.concatenate(groups, axis=0)

    return (build(v1, v2, lambda a, b: a + b, -jnp.inf),
            build(i1, i2, lambda a, b: a * N_KEYS + b, 0.0))


def _peer_fused_kernel(xs_ref, xm_ref, g_ref, wq_ref, sk_ref, u_ref, v_ref, *rest, split_blk):
    if split_blk is None:
        o_ref, q_scr, sel_e, sel_g, xn_scr, a_scr, bgt_scr, bg_scr, gmat, coef_scr = rest
        out_refs = ()
    else:
        *out_refs, q_scr, sel_e, sel_g, xn_scr, a_scr, bgt_scr, bg_scr, gmat, coef_scr, o_ref = rest
    i = pl.program_id(0)
    e = pl.program_id(1)
    n_steps = pl.num_programs(1)
    t = xm_ref.shape[0]
    n_half = t // SEL_T
    half_keys = N_KEYS // 2
    cur = i % 2

    def project_queries():
        xn = _rmsnorm(xs_ref[...], g_ref[...]).astype(BF16)
        q = jnp.dot(xn, wq_ref[...], preferred_element_type=F32).astype(BF16)
        for hf in range(n_half):
            for c in range(2 * PEER_HEADS):
                q_scr[hf, c] = q[hf * SEL_T:(hf + 1) * SEL_T, c * LANES:(c + 1) * LANES]

    @pl.when(e == 0)
    def _():
        @pl.when(i == 0)
        def _():
            project_queries()
            sel_e[...] = jnp.zeros_like(sel_e)
            sel_g[...] = jnp.zeros_like(sel_g)

        x = xm_ref[...]
        xn_scr[...] = _rmsnorm(x, g_ref[...]).astype(BF16)
        o_ref[...] = x
        coef_scr[0] = jnp.zeros(coef_scr.shape[1:], BF16)
        prev = 1 - cur
        lane_groups = SEL_T // LANES
        for hf in range(n_half):
            ef = sel_e[prev, hf]
            af = jnp.floor(ef * (1.0 / N_KEYS))
            bf = ef - af * N_KEYS
            gf = sel_g[prev, hf]
            a_scr[hf * SEL_T:(hf + 1) * SEL_T, :] = af.T
            bgf = bf + RSQRT2 * gf
            bgt_scr[hf * SEL_T:(hf + 1) * SEL_T, :] = bgf.T
            for lg in range(lane_groups):
                bg_scr[hf * lane_groups + lg] = bgf[:, lg * LANES:(lg + 1) * LANES]
        r = lax.broadcasted_iota(jnp.int32, (N_KEYS, LANES), 0)
        packed_shape = (N_KEYS // (2 * SUBLANES), 2 * SUBLANES, LANES)
        a_of_row = jnp.where(r < half_keys, 2 * r, 2 * (r - half_keys) + 1).astype(F32).astype(BF16)
        a_of_row = a_of_row.reshape(packed_shape)
        one = jnp.ones(packed_shape, BF16)
        zero = jnp.zeros(packed_shape, BF16)
        b_of_lane = lax.broadcasted_iota(jnp.int32, (N_KEYS, LANES), 1).astype(F32)
        b_of_row = r.astype(F32).astype(BF16).reshape(packed_shape)
        per_body = 32
        bodies_per_group = LANES // per_body
        row_major_every = 3

        def body(it, carry):
            grp = lax.shift_right_logical(it, bodies_per_group.bit_length() - 1)
            sub = it & (bodies_per_group - 1)
            shift = (LANES - sub * per_body) & (LANES - 1)
            bg = pltpu.roll(bg_scr[grp], shift, 1)
            tok0 = pl.multiple_of(it * per_body, per_body)
            arows = a_scr[pl.ds(tok0, per_body), :]
            bgrows = bgt_scr[pl.ds(tok0, per_body), :]
            for k in range(per_body):
                arow = jnp.broadcast_to(arows[k:k + 1], (2 * SUBLANES, LANES)).astype(BF16)[None]
                pa = jnp.where(a_of_row == arow, one, zero).reshape(N_KEYS, LANES)
                if k % row_major_every == 0:
                    bgrow = jnp.broadcast_to(bgrows[k:k + 1], (2 * SUBLANES, LANES))
                    brow = jnp.floor(bgrow)
                    grow = (bgrow - brow).astype(BF16)[None]
                    qb = jnp.where(b_of_row == brow.astype(BF16)[None], grow, zero).reshape(N_KEYS, LANES)
                    tile = lax.dot_general(pa, qb, NT_DIMS, preferred_element_type=F32)
                else:
                    bgcol = jnp.broadcast_to(bg[:, k:k + 1], (N_KEYS, LANES))
                    bcol = jnp.floor(bgcol)
                    gcol = bgcol - bcol
                    qbt = jnp.where(bcol == b_of_lane, gcol, 0.0).astype(BF16)
                    tile = jnp.dot(pa, qbt, preferred_element_type=F32)
                row0 = pl.multiple_of((tok0 + k) * G_PITCH, SUBLANES)
                gmat[pl.ds(row0, half_keys), :] = pltpu.pack_elementwise(
                    [tile[0:half_keys], tile[half_keys:N_KEYS]], packed_dtype=BF16)
            return carry

        lax.fori_loop(0, t // per_body, body, 0)

    unit = jnp.minimum(e, n_half * PEER_HEADS - 1)
    half = lax.shift_right_logical(unit, PEER_HEADS.bit_length() - 1)
    h = unit & (PEER_HEADS - 1)
    rd = e & 1
    blk = jnp.minimum(e, n_steps - 2)
    eb = u_ref.shape[0]
    n_stages = 4
    exp_w = eb // n_stages
    out_w = v_ref.shape[1] // n_stages
    words_per_stage = exp_w // (2 * N_KEYS)

    def value_piece(c):
        ocols = slice(c * out_w, (c + 1) * out_w)
        o_ref[:, ocols] += jnp.dot(coef_scr[rd], v_ref[:, ocols], preferred_element_type=F32)

    def act_piece(c):
        ecols = slice(c * exp_w, (c + 1) * exp_w)
        act = lax.dot_general(xn_scr[...], u_ref[ecols, :], NT_DIMS, preferred_element_type=F32)
        gates = []
        for w in range(words_per_stage):
            word = gmat[pl.ds((blk * n_stages + c) * words_per_stage + w, t, stride=G_PITCH), :]
            gates.append(lax.bitcast_convert_type(word << 16, F32))
            gates.append(lax.bitcast_convert_type(word & jnp.int32(-65536), F32))
        coef_scr[1 - rd, :, ecols] = (_scaled_gelu(act) * jnp.concatenate(gates, axis=1)).astype(BF16)

    def retrieval():
        lists = []
        for p in range(2):
            st = lax.dot_general(sk_ref[h, p], q_scr[half, 2 * h + p], NT_DIMS, preferred_element_type=F32)
            yield from _top16_of_keys(st, lists)
            yield
        cand, expert = _pair_candidates(lists[0], lists[1])
        second_level = _TopK(cand, payload=expert)
        for _ in range(4):
            second_level.step(PEER_TOPK // 4)
            yield
        best, e_sel = second_level.result()
        ex = jnp.exp(best - best[0:1])
        slot0 = pl.multiple_of(h * PEER_TOPK, PEER_TOPK)
        sel_e[cur, half, pl.ds(slot0, PEER_TOPK), :] = e_sel
        sel_g[cur, half, pl.ds(slot0, PEER_TOPK), :] = ex / jnp.sum(ex, axis=0, keepdims=True)

    pieces = [functools.partial(f, c) for f in (value_piece, act_piece) for c in range(n_stages)]
    parts = retrieval()
    parts_per_piece = 2
    for piece in pieces:
        piece()
        for _ in range(parts_per_piece):
            next(parts, None)
    for _ in parts:
        pass

    @pl.when(e == n_steps - 1)
    def _():
        project_queries()
        if out_refs:
            first_ref, second_ref = out_refs

            @pl.when(i - 1 < split_blk)
            def _():
                first_ref[...] = o_ref[...]

            @pl.when(i - 1 >= split_blk)
            def _():
                second_ref[...] = o_ref[...]


def _peer(x, g, wq, sk, u_all, v_all, layer, t=512, eb=1024, split_rows=None):
    n, d = x.shape
    nslot = PEER_HEADS * PEER_TOPK
    n_tok_blk = n // t
    n_exp_blk = N_KEYS * N_KEYS // eb
    n_half = t // SEL_T
    assert n_exp_blk == n_half * PEER_HEADS
    last_tok = n_tok_blk - 1
    once = pl.Buffered(1)
    if split_rows is None:
        split_blk = None
        out_specs = pl.BlockSpec((t, d), lambda i, e: (jnp.maximum(i - 1, 0), 0))
        out_shape = jax.ShapeDtypeStruct((n, d), F32)
        acc = []
    else:
        split_blk = split_rows // t
        out_specs = [pl.BlockSpec((t, d), lambda i, e: (jnp.clip(i - 1, 0, split_blk - 1), 0)),
                     pl.BlockSpec((t, d), lambda i, e: (jnp.clip(i - 1 - split_blk, 0, last_tok - split_blk), 0))]
        out_shape = [jax.ShapeDtypeStruct((split_rows, d), F32), jax.ShapeDtypeStruct((n - split_rows, d), F32)]
        acc = [pltpu.VMEM((t, d), F32)]
    return pl.pallas_call(
        functools.partial(_peer_fused_kernel, split_blk=split_blk),
        grid=(n_tok_blk + 1, n_exp_blk + 1),
        in_specs=[pl.BlockSpec((t, d), lambda i, e: (jnp.minimum(i + e // n_exp_blk, last_tok), 0),
                               pipeline_mode=once),
                  pl.BlockSpec((t, d), lambda i, e: (jnp.maximum(i - 1, 0), 0)),
                  pl.BlockSpec((1, d), lambda i, e: (0, 0)),
                  pl.BlockSpec((d, 2 * PEER_HEADS * LANES), lambda i, e: (0, 0), pipeline_mode=once),
                  pl.BlockSpec((PEER_HEADS, 2, N_KEYS, LANES), lambda i, e: (0, 0, 0, 0), pipeline_mode=once),
                  pl.BlockSpec((None, eb, d), lambda i, e: (layer, jnp.minimum(e, n_exp_blk - 1), 0)),
                  pl.BlockSpec((None, eb, d), lambda i, e: (layer, jnp.maximum(e - 1, 0), 0))],
        out_specs=out_specs,
        out_shape=out_shape,
        scratch_shapes=[pltpu.VMEM((n_half, 2 * PEER_HEADS, SEL_T, LANES), BF16),
                        pltpu.VMEM((2, n_half, nslot, SEL_T), F32),
                        pltpu.VMEM((2, n_half, nslot, SEL_T), F32),
                        pltpu.VMEM((t, d), BF16),
                        pltpu.VMEM((t, nslot), F32),
                        pltpu.VMEM((t, nslot), F32),
                        pltpu.VMEM((t // LANES, nslot, LANES), F32),
                        pltpu.VMEM((t * G_PITCH, LANES), jnp.int32),
                        pltpu.VMEM((2, t, eb), BF16)] + acc,
        compiler_params=_cparams(2),
        name="peer",
    )(x, x, g.reshape(1, d), wq, sk, u_all, v_all)


def _rope_tables(pos):
    half = HEAD_DIM // 2
    inv_freq = ROPE_THETA ** (-jnp.arange(half, dtype=F32) / half)
    ang = pos.astype(F32)[:, None] * inv_freq[None, :]
    cos, sin = jnp.cos(ang), jnp.sin(ang)
    reps = LANES // HEAD_DIM
    cos_t = jnp.tile(jnp.concatenate([cos, cos], axis=1), (1, reps))
    sin_t = jnp.tile(jnp.concatenate([-sin, sin], axis=1), (1, reps))
    return cos_t, sin_t


def kernel(x_prompt, x_sample, cache_conv, cache_win_k, cache_win_v, state_mlstm_C, state_mlstm_n,
           state_mlstm_m, norm_mix, norm_ffn, ab_w_in, ab_conv_w, ab_q_gain, ab_k_gain, ab_sinks, ab_w_out,
           ml_w_in, ml_gate_bias, ml_out_gain, ml_w_out, peer_w_q, peer_sub_keys, peer_u, peer_v):
    n_batch, seq, d = x_prompt.shape
    n_seq, t_len, _ = x_sample.shape
    n_prompt = n_batch * seq
    assert d == D_MODEL and t_len == SUBLANES and norm_mix.shape[0] == 2

    xp = x_prompt.reshape(n_prompt, d)
    xs = x_sample.reshape(n_seq * t_len, d)

    cos_p, sin_p = _rope_tables(jnp.arange(seq, dtype=jnp.int32))
    cos_s, sin_s = _rope_tables(PAST_LEN + jnp.arange(t_len, dtype=jnp.int32))
    bt = 16
    cos_s, sin_s = jnp.tile(cos_s, (bt, 1)), jnp.tile(sin_s, (bt, 1))
    lane = jnp.arange(LANES)
    seg = (lane[:, None] // HEAD_DIM == lane[None, :] // HEAD_DIM).astype(BF16)
    reps = LANES // HEAD_DIM
    qg = jnp.tile(ab_q_gain[0], reps).reshape(1, LANES)
    kg = jnp.tile(ab_k_gain[0], reps).reshape(1, LANES)
    wo_ab = ab_w_out[0].astype(BF16)

    z = _norm_proj([xp, xs], norm_mix[0], ab_w_in[0].astype(BF16))
    y_p, k_p, v_p, c_p = _ab_prompt(z, xp, n_batch, seq, cos_p, sin_p, ab_conv_w[0], qg, kg, ab_sinks[0],
                                    seg, wo_ab)
    y_s, c_s, k_s, v_s = _ab_sample(z, xs, n_prompt, n_seq, t_len, cos_s, sin_s, ab_conv_w[0], qg, kg,
                                    ab_sinks[0], seg, wo_ab, cache_conv[0],
                                    cache_win_k[0].reshape(n_seq, WINDOW, LANES),
                                    cache_win_v[0].reshape(n_seq, WINDOW, LANES), bt=bt)
    x = jnp.concatenate([y_p.reshape(n_prompt, d), y_s], axis=0)
    u_all = (peer_u * RSQRT2).astype(BF16)
    v_all = peer_v.astype(BF16)
    x = _peer(x, norm_ffn[0], peer_w_q[0].astype(BF16), peer_sub_keys[0].astype(BF16), u_all, v_all, 0)

    n_gate = 2 * ML_HEADS
    w_in = jnp.pad(ml_w_in[0], ((0, 0), (0, ML_IN_PAD - ml_w_in.shape[2]))).astype(BF16)
    bias = jnp.pad(ml_gate_bias[0], (0, LANES - n_gate)).reshape(1, LANES)
    og = ml_out_gain[0].reshape(1, D_MODEL)
    idx = jnp.arange(ML_CHUNK)
    tril = (idx[None, :] <= idx[:, None]).astype(BF16)
    wo_ml = ml_w_out[0].astype(BF16)

    z = _norm_proj([x], norm_mix[1], w_in)
    y_p, cm_p, nm_p, mm_p = _mlstm_prompt(z, x, n_batch, seq, bias, og, tril, wo_ml)
    y_s, cm_s, nm_s, mm_s = _mlstm_sample(z, x, n_prompt, n_seq, t_len, bias, og, tril, wo_ml,
                                          state_mlstm_C[0], state_mlstm_n[0],
                                          state_mlstm_m[0].reshape(n_seq, 1, ML_HEADS))
    x = jnp.concatenate([y_p.reshape(n_prompt, d), y_s], axis=0)
    out_p, out_s = _peer(x, norm_ffn[1], peer_w_q[1].astype(BF16), peer_sub_keys[1].astype(BF16), u_all, v_all, 1,
                         split_rows=n_prompt)

    y_prompt = out_p.reshape(n_batch, seq, d)
    y_sample = out_s.reshape(n_seq, t_len, d)
    kv_shape_p = (1, n_batch, WINDOW, N_KV_HEADS, HEAD_DIM)
    kv_shape_s = (1, n_seq, WINDOW, N_KV_HEADS, HEAD_DIM)
    return (y_prompt, y_sample,
            c_p[:, SUBLANES - 2:, :][None], k_p.reshape(kv_shape_p), v_p.reshape(kv_shape_p),
            cm_p[None], nm_p[None], mm_p.reshape(1, n_batch, ML_HEADS),
            c_s[None], k_s.reshape(kv_shape_s), v_s.reshape(kv_shape_s),
            cm_s[None], nm_s[None], mm_s.reshape(1, n_seq, ML_HEADS))
```

```python
import functools

import jax
import jax.numpy as jnp
from jax import lax
from jax.experimental import pallas as pl
from jax.experimental.pallas import tpu as pltpu

F32 = jnp.float32
BF16 = jnp.bfloat16
EPS = 1e-6

D_MODEL = 1024
CONV_DIM = 512
N_Q_HEADS = 8
N_KV_HEADS = 2
HEAD_DIM = 64
WINDOW = 128
ROPE_THETA = 10000.0
AB_IN = 2304
ML_HEADS = 4
ML_QK = 128
ML_V = 256
ML_CHUNK = 128
ML_GATE_COL = 3072
ML_IN_PAD = ML_GATE_COL + 128
N_KEYS = 128
PEER_HEADS = 8
PEER_TOPK = 16
PAST_LEN = 16384

LANES = 128
SUBLANES = 8
G_PITCH = N_KEYS // 2 + SUBLANES
VMEM_LIMIT = 56 * 1024 * 1024

NT_DIMS = (((1,), (1,)), ((), ()))
TN_DIMS = (((0,), (0,)), ((), ()))


def _cparams(n_axes, vmem=VMEM_LIMIT):
    return pltpu.CompilerParams(dimension_semantics=("arbitrary",) * n_axes, vmem_limit_bytes=vmem)


def _rmsnorm(x, g):
    return x * lax.rsqrt(jnp.mean(x * x, axis=-1, keepdims=True) + EPS) * g


def _bf16_pieces(a, terms):
    pieces = []
    rem = a
    for _ in range(terms):
        piece = rem.astype(BF16)
        rem = rem - piece.astype(F32)
        pieces.append(piece)
    return pieces


def _split_dot(a, b_bf16, terms=2):
    return sum(jnp.dot(p, b_bf16, preferred_element_type=F32) for p in _bf16_pieces(a, terms))


def _norm_proj_kernel(*refs, starts):
    x_refs = refs[:len(starts)]
    g_ref, w_ref, o_ref = refs[len(starts):]
    i = pl.program_id(0)
    x = x_refs[0][...]
    for x_ref, start in zip(x_refs[1:], starts[1:]):
        x = jnp.where(i >= start, x_ref[...], x)
    r = _rmsnorm(x, g_ref[...])
    o_ref[...] = jnp.dot(r.astype(BF16), w_ref[...], preferred_element_type=F32)


def _norm_proj(x_parts, g, w_bf16, tm=512):
    d = x_parts[0].shape[1]
    nout = w_bf16.shape[1]
    blocks = [p.shape[0] // tm for p in x_parts]
    starts = tuple(sum(blocks[:k]) for k in range(len(blocks)))
    part_spec = lambda start, nblk: pl.BlockSpec((tm, d), lambda i: (jnp.clip(i - start, 0, nblk - 1), 0))
    return pl.pallas_call(
        functools.partial(_norm_proj_kernel, starts=starts),
        grid=(sum(blocks),),
        in_specs=[part_spec(s, nb) for s, nb in zip(starts, blocks)] + [
            pl.BlockSpec((1, d), lambda i: (0, 0)),
            pl.BlockSpec((d, nout), lambda i: (0, 0))],
        out_specs=pl.BlockSpec((tm, nout), lambda i: (i, 0)),
        out_shape=jax.ShapeDtypeStruct((sum(blocks) * tm, nout), F32),
        compiler_params=_cparams(1),
        name="norm_proj",
    )(*x_parts, g.reshape(1, d), w_bf16)


def _headnorm_rope(xc, gain, cos, sin, seg, hi_half):
    ss = _split_dot(xc * xc, seg)
    xn = xc * lax.rsqrt(ss * (1.0 / HEAD_DIM) + EPS) * gain
    partner = jnp.where(hi_half, pltpu.roll(xn, 32, 1), pltpu.roll(xn, 96, 1))
    return xn * cos + partner * sin


def _softmax_sink(s, mask, sink):
    s = jnp.where(mask, s, -1e30)
    m = jnp.maximum(jnp.max(s, axis=-1, keepdims=True), sink)
    p = jnp.exp(s - m)
    denom = jnp.sum(p, axis=-1, keepdims=True) + jnp.exp(sink - m)
    return (p / denom).astype(BF16)


def _ab_prompt_kernel(sink_ref, *refs, n_batch):
    z_refs, x_refs = refs[0:n_batch], refs[n_batch:2 * n_batch]
    (cos_ref, sin_ref, cw_ref, qg_ref, kg_ref, seg_ref, wo_ref,
     y_ref, kst_ref, vst_ref, cst_ref, pk_ref, pv_ref, pu_ref) = refs[2 * n_batch:]
    j = pl.program_id(0)

    @pl.when(j == 0)
    def _():
        pk_ref[...] = jnp.zeros_like(pk_ref)
        pv_ref[...] = jnp.zeros_like(pv_ref)
        pu_ref[...] = jnp.zeros_like(pu_ref)

    for b in range(n_batch):
        _ab_prompt_block(j, sink_ref, z_refs[b], x_refs[b], cos_ref, sin_ref, cw_ref, qg_ref, kg_ref, seg_ref,
                         wo_ref, y_ref.at[b], kst_ref.at[b], vst_ref.at[b], cst_ref.at[b],
                         pk_ref.at[b], pv_ref.at[b], pu_ref.at[b])


def _ab_prompt_block(j, sink_ref, z_ref, x_ref, cos_ref, sin_ref, cw_ref, qg_ref, kg_ref, seg_ref, wo_ref,
                     y_ref, kst_ref, vst_ref, cst_ref, pk_ref, pv_ref, pu_ref):
    blk = z_ref.shape[0]
    cos = cos_ref[...]
    sin = sin_ref[...]
    seg = seg_ref[...]
    hi_half = (lax.broadcasted_iota(jnp.int32, (blk, LANES), 1) & 32) != 0

    gate_b = z_ref[:, 0:CONV_DIM]
    u = z_ref[:, CONV_DIM:2 * CONV_DIM] * z_ref[:, 2 * CONV_DIM:3 * CONV_DIM]
    ng = blk // SUBLANES
    u3 = u.reshape(ng, SUBLANES, CONV_DIM)
    ext = jnp.concatenate([pu_ref[...][None], u3], axis=0)
    t8 = lax.broadcasted_iota(jnp.int32, (ng, SUBLANES, CONV_DIM), 1)
    r1 = pltpu.roll(ext, 1, 1)
    r2 = pltpu.roll(ext, 2, 1)
    um1 = jnp.where(t8 >= 1, r1[1:], r1[:-1])
    um2 = jnp.where(t8 >= 2, r2[1:], r2[:-1])
    cw = cw_ref[...]
    conv = cw[0:1][None] * um2 + cw[1:2][None] * um1 + cw[2:3][None] * u3
    yconv = gate_b * conv.reshape(blk, CONV_DIM)

    q0 = 3 * CONV_DIM
    k0 = q0 + N_Q_HEADS * HEAD_DIM
    v0 = k0 + N_KV_HEADS * HEAD_DIM
    qg = qg_ref[...]
    qr = [_headnorm_rope(z_ref[:, q0 + c * LANES:q0 + (c + 1) * LANES], qg, cos, sin, seg, hi_half)
          for c in range(N_Q_HEADS * HEAD_DIM // LANES)]
    kr = _headnorm_rope(z_ref[:, k0:k0 + LANES], kg_ref[...], cos, sin, seg, hi_half)
    v = z_ref[:, v0:v0 + LANES]
    pk = pk_ref[...]
    pv = pv_ref[...]

    row = lax.broadcasted_iota(jnp.int32, (blk, 2 * blk), 0)
    col = lax.broadcasted_iota(jnp.int32, (blk, 2 * blk), 1)
    row_prev = row + jnp.where(j == 0, 2 * blk, 0)
    mask = ((col < blk) & (col > row_prev)) | ((col >= blk) & (col - blk <= row))

    kks, vvs = [], []
    for g in range(N_KV_HEADS):
        sl = slice(g * HEAD_DIM, (g + 1) * HEAD_DIM)
        kks.append(jnp.concatenate([pk[:, sl], kr[:, sl]], axis=0).astype(BF16))
        vvs.append(jnp.concatenate([pv[:, sl], v[:, sl]], axis=0).astype(BF16))
    outs = []
    for h in range(N_Q_HEADS):
        g = h // (N_Q_HEADS // N_KV_HEADS)
        qh = qr[h // 2][:, (h % 2) * HEAD_DIM:(h % 2 + 1) * HEAD_DIM].astype(BF16)
        s = lax.dot_general(qh, kks[g], NT_DIMS, preferred_element_type=F32) * (HEAD_DIM ** -0.5)
        p = _softmax_sink(s, mask, sink_ref[h])
        outs.append(jnp.dot(p, vvs[g], preferred_element_type=F32))
    attn = jnp.concatenate(outs, axis=1)

    y = (jnp.dot(yconv.astype(BF16), wo_ref[0:CONV_DIM, :], preferred_element_type=F32)
         + jnp.dot(attn.astype(BF16), wo_ref[CONV_DIM:2 * CONV_DIM, :], preferred_element_type=F32))
    y_ref[...] = y + x_ref[...]

    pk_ref[...] = kr
    pv_ref[...] = v
    pu_ref[...] = u3[ng - 1]
    kst_ref[...] = kr
    vst_ref[...] = v
    cst_ref[...] = u3[ng - 1]


def _ab_prompt(z, x, n_batch, seq, cos, sin, cw, qg, kg, sinks, seg, wo):
    blk = WINDOW
    nb = seq // blk
    const = lambda j, s: (0, 0)
    whole = lambda j, s: (0, 0, 0)
    tok_specs = lambda width: [pl.BlockSpec((blk, width), functools.partial(lambda j, s, b: (b * nb + j, 0), b=b))
                               for b in range(n_batch)]
    grid_spec = pltpu.PrefetchScalarGridSpec(
        num_scalar_prefetch=1,
        grid=(nb,),
        in_specs=tok_specs(AB_IN) + tok_specs(D_MODEL) + [
            pl.BlockSpec((blk, LANES), lambda j, s: (j, 0)),
            pl.BlockSpec((blk, LANES), lambda j, s: (j, 0)),
            pl.BlockSpec((3, CONV_DIM), const),
            pl.BlockSpec((1, LANES), const),
            pl.BlockSpec((1, LANES), const),
            pl.BlockSpec((LANES, LANES), const),
            pl.BlockSpec((D_MODEL, D_MODEL), const)],
        out_specs=[pl.BlockSpec((n_batch, blk, D_MODEL), lambda j, s: (0, j, 0)),
                   pl.BlockSpec((n_batch, blk, LANES), whole),
                   pl.BlockSpec((n_batch, blk, LANES), whole),
                   pl.BlockSpec((n_batch, SUBLANES, CONV_DIM), whole)],
        scratch_shapes=[pltpu.VMEM((n_batch, blk, LANES), F32), pltpu.VMEM((n_batch, blk, LANES), F32),
                        pltpu.VMEM((n_batch, SUBLANES, CONV_DIM), F32)])
    return pl.pallas_call(
        functools.partial(_ab_prompt_kernel, n_batch=n_batch),
        grid_spec=grid_spec,
        out_shape=[jax.ShapeDtypeStruct((n_batch, seq, D_MODEL), F32),
                   jax.ShapeDtypeStruct((n_batch, blk, LANES), F32),
                   jax.ShapeDtypeStruct((n_batch, blk, LANES), F32),
                   jax.ShapeDtypeStruct((n_batch, SUBLANES, CONV_DIM), F32)],
        compiler_params=_cparams(1),
        name="ab_prompt",
    )(sinks, *([z] * n_batch), *([x] * n_batch), cos, sin, cw, qg, kg, seg, wo)


def _ab_sample_kernel(sink_ref, z_ref, x_ref, cos_ref, sin_ref, cw_ref, qg_ref, kg_ref, seg_ref, wo_ref,
                      cc_ref, ck_ref, cv_ref, y_ref, cs_ref, ks_ref, vs_ref):
    rows = z_ref.shape[0]
    t_len = SUBLANES
    bt = rows // t_len
    cos = cos_ref[...]
    sin = sin_ref[...]
    seg = seg_ref[...]
    hi_half = (lax.broadcasted_iota(jnp.int32, (rows, LANES), 1) & 32) != 0

    gate_b = z_ref[:, 0:CONV_DIM]
    u = z_ref[:, CONV_DIM:2 * CONV_DIM] * z_ref[:, 2 * CONV_DIM:3 * CONV_DIM]
    u3 = u.reshape(bt, t_len, CONV_DIM)
    cc = cc_ref[...]
    c0 = cc[:, 0:1, :]
    c1 = cc[:, 1:2, :]
    t8 = lax.broadcasted_iota(jnp.int32, (bt, t_len, CONV_DIM), 1)
    r1 = pltpu.roll(u3, 1, 1)
    r2 = pltpu.roll(u3, 2, 1)
    um1 = jnp.where(t8 >= 1, r1, c1)
    um2 = jnp.where(t8 >= 2, r2, jnp.where(t8 == 1, c1, c0))
    cw = cw_ref[...]
    conv = cw[0:1][None] * um2 + cw[1:2][None] * um1 + cw[2:3][None] * u3
    yconv = gate_b * conv.reshape(rows, CONV_DIM)
    cs_ref[...] = r2[:, 0:2, :]

    q0 = 3 * CONV_DIM
    k0 = q0 + N_Q_HEADS * HEAD_DIM
    v0 = k0 + N_KV_HEADS * HEAD_DIM
    qg = qg_ref[...]
    qr = [_headnorm_rope(z_ref[:, q0 + c * LANES:q0 + (c + 1) * LANES], qg, cos, sin, seg, hi_half)
          for c in range(N_Q_HEADS * HEAD_DIM // LANES)]
    kr = _headnorm_rope(z_ref[:, k0:k0 + LANES], kg_ref[...], cos, sin, seg, hi_half)
    v = z_ref[:, v0:v0 + LANES]

    group = N_Q_HEADS // N_KV_HEADS
    nq = group * t_len
    nk = 2 * WINDOW
    qrow = lax.broadcasted_iota(jnp.int32, (nq, nk), 0)
    t_q = qrow & (t_len - 1)
    col = lax.broadcasted_iota(jnp.int32, (nq, nk), 1)
    mask = (((col < WINDOW) & (col > t_q)) | ((col >= WINDOW) & (col - WINDOW <= t_q)))[None]
    hrow = lax.broadcasted_iota(jnp.int32, (nq, 1), 0) // t_len
    pad = jnp.zeros((bt, nk - WINDOW - t_len, HEAD_DIM), F32)

    outs = [None] * N_Q_HEADS
    for g in range(N_KV_HEADS):
        sl = slice(g * HEAD_DIM, (g + 1) * HEAD_DIM)
        qs = jnp.concatenate(
            [qr[h // 2][:, (h % 2) * HEAD_DIM:(h % 2 + 1) * HEAD_DIM].reshape(bt, t_len, HEAD_DIM)
             for h in range(g * group, (g + 1) * group)], axis=1)
        kk = jnp.concatenate([ck_ref[:, :, sl], kr[:, sl].reshape(bt, t_len, HEAD_DIM), pad], axis=1)
        vv = jnp.concatenate([cv_ref[:, :, sl], v[:, sl].reshape(bt, t_len, HEAD_DIM), pad], axis=1)
        s = jnp.einsum('bqd,bkd->bqk', qs.astype(BF16), kk.astype(BF16),
                       preferred_element_type=F32) * (HEAD_DIM ** -0.5)
        sink = jnp.zeros((nq, 1), F32)
        for hh in range(group):
            sink = jnp.where(hrow == hh, sink_ref[g * group + hh], sink)
        p = _softmax_sink(s, mask, sink[None])
        o = jnp.einsum('bqk,bkd->bqd', p, vv.astype(BF16), preferred_element_type=F32)
        for hh in range(group):
            outs[g * group + hh] = o[:, hh * t_len:(hh + 1) * t_len, :].reshape(rows, HEAD_DIM)
    attn = jnp.concatenate(outs, axis=1)

    y = (jnp.dot(yconv.astype(BF16), wo_ref[0:CONV_DIM, :], preferred_element_type=F32)
         + jnp.dot(attn.astype(BF16), wo_ref[CONV_DIM:2 * CONV_DIM, :], preferred_element_type=F32))
    y_ref[...] = y + x_ref[...]

    keep = WINDOW - t_len
    ks_ref[:, 0:keep, :] = ck_ref[:, t_len:WINDOW, :]
    ks_ref[:, keep:WINDOW, :] = kr.reshape(bt, t_len, LANES)
    vs_ref[:, 0:keep, :] = cv_ref[:, t_len:WINDOW, :]
    vs_ref[:, keep:WINDOW, :] = v.reshape(bt, t_len, LANES)


def _ab_sample(z, x, row0, n_seq, t_len, cos, sin, cw, qg, kg, sinks, seg, wo, cc, ck, cv, bt=16):
    rows = bt * t_len
    blk0 = row0 // rows
    tok = lambda i, s: (blk0 + i, 0)
    const = lambda i, s: (0, 0)
    seq3 = lambda i, s: (i, 0, 0)
    grid_spec = pltpu.PrefetchScalarGridSpec(
        num_scalar_prefetch=1,
        grid=(n_seq // bt,),
        in_specs=[pl.BlockSpec((rows, AB_IN), tok),
                  pl.BlockSpec((rows, D_MODEL), lambda i, s: (i, 0)),
                  pl.BlockSpec((rows, LANES), const),
                  pl.BlockSpec((rows, LANES), const),
                  pl.BlockSpec((3, CONV_DIM), const),
                  pl.BlockSpec((1, LANES), const),
                  pl.BlockSpec((1, LANES), const),
                  pl.BlockSpec((LANES, LANES), const),
                  pl.BlockSpec((D_MODEL, D_MODEL), const),
                  pl.BlockSpec((bt, 2, CONV_DIM), seq3),
                  pl.BlockSpec((bt, WINDOW, LANES), seq3),
                  pl.BlockSpec((bt, WINDOW, LANES), seq3)],
        out_specs=[pl.BlockSpec((rows, D_MODEL), lambda i, s: (i, 0)),
                   pl.BlockSpec((bt, 2, CONV_DIM), seq3),
                   pl.BlockSpec((bt, WINDOW, LANES), seq3),
                   pl.BlockSpec((bt, WINDOW, LANES), seq3)])
    return pl.pallas_call(
        _ab_sample_kernel,
        grid_spec=grid_spec,
        out_shape=[jax.ShapeDtypeStruct((n_seq * t_len, D_MODEL), F32),
                   jax.ShapeDtypeStruct((n_seq, 2, CONV_DIM), F32),
                   jax.ShapeDtypeStruct((n_seq, WINDOW, LANES), F32),
                   jax.ShapeDtypeStruct((n_seq, WINDOW, LANES), F32)],
        compiler_params=_cparams(1),
        name="ab_sample",
    )(sinks, z, x, cos, sin, cw, qg, kg, seg, wo, cc, ck, cv)


def _log_sigmoid(x):
    return jnp.minimum(x, 0.0) - jnp.log(1.0 + jnp.exp(-jnp.abs(x)))


def _mlstm_chunk(z, bias, og, tril, c_src, n_src, m_src, c_dst, n_dst, m_dst, n_real, rows):
    L = z.shape[0]
    gates = z[:, ML_GATE_COL:ML_GATE_COL + LANES] + bias
    if n_real < L:
        live = lax.broadcasted_iota(jnp.int32, (L, LANES), 0) < n_real
        li_all = jnp.where(live, gates, -1e30)
        lf_all = jnp.where(live, _log_sigmoid(gates), 0.0)
    else:
        li_all = gates
        lf_all = _log_sigmoid(gates)
    lf_pieces = _bf16_pieces(lf_all, 3)
    f_col_all = sum(jnp.dot(tril[0:rows], p, preferred_element_type=F32) for p in lf_pieces)
    f_row_all = sum(lax.dot_general(p, tril, (((0,), (1,)), ((), ())), preferred_element_type=F32)
                    for p in lf_pieces)
    li_t = li_all.T
    rr = lax.broadcasted_iota(jnp.int32, (rows, L), 0)
    cc = lax.broadcasted_iota(jnp.int32, (rows, L), 1)
    causal = cc <= rr

    outs, m_new_all = [], []
    for h in range(ML_HEADS):
        f_col = f_col_all[:, ML_HEADS + h:ML_HEADS + h + 1]
        f_row = f_row_all[ML_HEADS + h:ML_HEADS + h + 1, :]
        li_row = li_t[h:h + 1, :]
        li_col = li_all[0:rows, h:h + 1]
        m0 = m_src[0:1, h:h + 1]
        c0 = c_src[h]
        n0 = n_src[h:h + 1, :]
        qh = z[0:rows, h * ML_QK:(h + 1) * ML_QK]
        kh = z[:, ML_HEADS * ML_QK + h * ML_QK:ML_HEADS * ML_QK + (h + 1) * ML_QK] * (ML_QK ** -0.5)
        v_off = 2 * ML_HEADS * ML_QK
        vh = z[:, v_off + h * ML_V:v_off + (h + 1) * ML_V]
        o_off = v_off + ML_HEADS * ML_V
        oh = z[0:rows, o_off + h * ML_V:o_off + (h + 1) * ML_V]
        qb = qh.astype(BF16)
        vb = vh.astype(BF16)

        dmat = jnp.where(causal, f_col - f_row + li_row, -jnp.inf)
        gcar = f_col + m0
        m_t = jnp.maximum(jnp.max(dmat, axis=-1, keepdims=True), gcar)
        w = jnp.exp(dmat - m_t)
        s = lax.dot_general(qb, kh.astype(BF16), NT_DIMS, preferred_element_type=F32) * w
        carry = jnp.exp(gcar - m_t)
        num = (jnp.dot(s.astype(BF16), vb, preferred_element_type=F32)
               + jnp.dot(qb, c0.astype(BF16), preferred_element_type=F32) * carry)
        den = jnp.sum(s, axis=-1, keepdims=True) + carry * jnp.sum(qh * n0, axis=-1, keepdims=True)
        hout = num / jnp.maximum(jnp.abs(den), jnp.exp(-m_t))

        f_last = f_col[n_real - 1:n_real, :]
        w_end = f_last - f_col + li_col
        m_new = jnp.maximum(f_last + m0, jnp.max(w_end, axis=0, keepdims=True))
        a_end = jnp.exp(w_end - m_new)
        scale = jnp.exp(f_last + m0 - m_new)
        ka = kh[0:rows] * a_end
        c_dst[h] = scale * c0 + lax.dot_general(ka.astype(BF16), vb[0:rows], TN_DIMS,
                                                preferred_element_type=F32)
        n_dst[h:h + 1, :] = scale * n0 + jnp.sum(ka, axis=0, keepdims=True)
        m_new_all.append(m_new)

        hn = _rmsnorm(hout, og[:, h * ML_V:(h + 1) * ML_V])
        outs.append(jax.nn.sigmoid(oh) * hn)
    m_dst[...] = jnp.concatenate(m_new_all, axis=1)
    return jnp.concatenate(outs, axis=1)


def _mlstm_prompt_kernel(*refs, n_batch):
    z_refs, x_refs = refs[0:n_batch], refs[n_batch:2 * n_batch]
    bias_ref, og_ref, tril_ref, wo_ref, y_ref, c_ref, n_ref, m_ref = refs[2 * n_batch:]

    @pl.when(pl.program_id(0) == 0)
    def _():
        c_ref[...] = jnp.zeros_like(c_ref)
        n_ref[...] = jnp.zeros_like(n_ref)
        m_ref[...] = jnp.zeros_like(m_ref)

    for b in range(n_batch):
        state = (c_ref.at[b], n_ref.at[b], m_ref.at[b])
        out = _mlstm_chunk(z_refs[b][...], bias_ref[...], og_ref[...], tril_ref[...], *state, *state,
                           ML_CHUNK, ML_CHUNK)
        y_ref[b] = jnp.dot(out.astype(BF16), wo_ref[...], preferred_element_type=F32) + x_refs[b][...]


def _mlstm_sample_kernel(z_ref, x_ref, bias_ref, og_ref, tril_ref, wo_ref, c0_ref, n0_ref, m0_ref,
                         y_ref, c_ref, n_ref, m_ref, *, t_len):
    n_here = z_ref.shape[0] // t_len
    q_rows = 2 * SUBLANES
    outs = []
    for s in range(n_here):
        zpad = jnp.concatenate([z_ref[s * t_len:(s + 1) * t_len, :],
                                jnp.zeros((ML_CHUNK - t_len, ML_IN_PAD), F32)], axis=0)
        out = _mlstm_chunk(zpad, bias_ref[...], og_ref[...], tril_ref[...],
                           c0_ref.at[s], n0_ref.at[s], m0_ref.at[s], c_ref.at[s], n_ref.at[s], m_ref.at[s],
                           t_len, q_rows)
        outs.append(out[0:t_len])
    out_all = jnp.concatenate(outs, axis=0).astype(BF16)
    y_ref[...] = jnp.dot(out_all, wo_ref[...], preferred_element_type=F32) + x_ref[...]


def _mlstm_weight_specs(const):
    return [pl.BlockSpec((1, LANES), const),
            pl.BlockSpec((1, D_MODEL), const),
            pl.BlockSpec((ML_CHUNK, ML_CHUNK), const),
            pl.BlockSpec((D_MODEL, D_MODEL), const)]


def _mlstm_prompt(z, x, n_batch, seq, bias, og, tril, wo):
    nc = seq // ML_CHUNK
    const = lambda j: (0, 0)
    tok_specs = lambda width: [pl.BlockSpec((ML_CHUNK, width), functools.partial(lambda j, b: (b * nc + j, 0), b=b))
                               for b in range(n_batch)]
    return pl.pallas_call(
        functools.partial(_mlstm_prompt_kernel, n_batch=n_batch),
        grid=(nc,),
        in_specs=tok_specs(ML_IN_PAD) + tok_specs(D_MODEL) + _mlstm_weight_specs(const),
        out_specs=[pl.BlockSpec((n_batch, ML_CHUNK, D_MODEL), lambda j: (0, j, 0)),
                   pl.BlockSpec((n_batch, ML_HEADS, ML_QK, ML_V), lambda j: (0, 0, 0, 0)),
                   pl.BlockSpec((n_batch, ML_HEADS, ML_QK), lambda j: (0, 0, 0)),
                   pl.BlockSpec((n_batch, 1, ML_HEADS), lambda j: (0, 0, 0))],
        out_shape=[jax.ShapeDtypeStruct((n_batch, seq, D_MODEL), F32),
                   jax.ShapeDtypeStruct((n_batch, ML_HEADS, ML_QK, ML_V), F32),
                   jax.ShapeDtypeStruct((n_batch, ML_HEADS, ML_QK), F32),
                   jax.ShapeDtypeStruct((n_batch, 1, ML_HEADS), F32)],
        compiler_params=_cparams(1),
        name="mlstm_prompt",
    )(*([z] * n_batch), *([x] * n_batch), bias, og, tril, wo)


def _mlstm_sample(z, x, row0, n_seq, t_len, bias, og, tril, wo, c0, n0, m0, seqs_per_step=8):
    rows = seqs_per_step * t_len
    blk0 = row0 // rows
    tok = lambda i: (blk0 + i, 0)
    const = lambda i: (0, 0)
    st4 = lambda i: (i, 0, 0, 0)
    st3 = lambda i: (i, 0, 0)
    state_specs = [pl.BlockSpec((seqs_per_step, ML_HEADS, ML_QK, ML_V), st4),
                   pl.BlockSpec((seqs_per_step, ML_HEADS, ML_QK), st3),
                   pl.BlockSpec((seqs_per_step, 1, ML_HEADS), st3)]
    return pl.pallas_call(
        functools.partial(_mlstm_sample_kernel, t_len=t_len),
        grid=(n_seq // seqs_per_step,),
        in_specs=[pl.BlockSpec((rows, ML_IN_PAD), tok),
                  pl.BlockSpec((rows, D_MODEL), tok)] + _mlstm_weight_specs(const) + state_specs,
        out_specs=[pl.BlockSpec((rows, D_MODEL), lambda i: (i, 0))] + state_specs,
        out_shape=[jax.ShapeDtypeStruct((n_seq * t_len, D_MODEL), F32),
                   jax.ShapeDtypeStruct((n_seq, ML_HEADS, ML_QK, ML_V), F32),
                   jax.ShapeDtypeStruct((n_seq, ML_HEADS, ML_QK), F32),
                   jax.ShapeDtypeStruct((n_seq, 1, ML_HEADS), F32)],
        compiler_params=_cparams(1),
        name="mlstm_sample",
    )(z, x, bias, og, tril, wo, c0, n0, m0)


RSQRT2 = 0.7071067811865476


def _scaled_gelu(xs):
    return xs * (1.0 + lax.erf(xs))


SEL_T = 256


def _batcher_pairs(n):
    pairs = []
    p = 1
    while p < n:
        k = p
        while k >= 1:
            for j in range(k % p, n - k, 2 * k):
                for i in range(min(k, n - j - k)):
                    if (i + j) // (2 * p) == (i + j + k) // (2 * p):
                        pairs.append((i + j, i + j + k))
            k //= 2
        p *= 2
    return pairs


def _bitonic_merge_pairs(n):
    pairs = []
    s = n // 2
    while s >= 1:
        pairs += [(i, i + s) for i in range(n) if not i & s]
        s //= 2
    return pairs


def _top16_of_keys(s, out):
    n_vreg = N_KEYS // SUBLANES
    assert n_vreg == PEER_TOPK and N_KEYS == 128
    rounded = (s + 0.0).astype(BF16).astype(F32)
    b16 = lax.shift_right_logical(lax.bitcast_convert_type(rounded, jnp.int32), 16)
    code = jnp.where(b16 >= 0x8000, b16 ^ 0xFFFF, b16 | 0x8000)
    row = lax.broadcasted_iota(jnp.int32, s.shape, 0)
    keys = lax.bitcast_convert_type((code * N_KEYS + (N_KEYS - 1 - row)) | 0x4B000000, F32)

    x = [keys[v * SUBLANES:(v + 1) * SUBLANES] for v in range(n_vreg)]

    def exchange(i, j):
        x[i], x[j] = jnp.maximum(x[i], x[j]), jnp.minimum(x[i], x[j])

    for i, j in _batcher_pairs(n_vreg):
        exchange(i, j)
    yield
    for shift in (SUBLANES // 2, SUBLANES // 4, SUBLANES // 8):
        other = [pltpu.roll(a, shift, 0) for a in x]
        x = [jnp.maximum(x[i], other[n_vreg - 1 - i]) for i in range(n_vreg)]
        for i, j in _bitonic_merge_pairs(n_vreg):
            exchange(i, j)
        yield
    top = lax.bitcast_convert_type(jnp.concatenate([a[0:1] for a in x], axis=0), jnp.int32) & 0x7FFFFF
    rows = (N_KEYS - 1 - (top & (N_KEYS - 1))).astype(F32)
    code = lax.shift_right_logical(top, 7)
    b16 = jnp.where(code >= 0x8000, code ^ 0x8000, code ^ 0xFFFF)
    out.append((lax.bitcast_convert_type(b16 << 16, F32), rows))


class _TopK:
    def __init__(self, s, payload=None):
        self.s = s
        self.payload = payload
        self.rows = lax.broadcasted_iota(jnp.int32, s.shape, 0).astype(F32).astype(s.dtype)
        self.vals, self.picks = [], []

    def step(self, n):
        dt = self.s.dtype
        bound = jnp.asarray(self.s.shape[0], dt)
        for _ in range(n):
            m = jnp.max(self.s, axis=0, keepdims=True)
            first = jnp.min(jnp.where(self.s == m, self.rows, bound), axis=0, keepdims=True)
            sel = self.rows == first
            self.vals.append(m)
            if self.payload is None:
                self.picks.append(first)
            else:
                self.picks.append(jnp.max(jnp.where(sel, self.payload, -1.0), axis=0, keepdims=True))
            self.s = jnp.where(sel, jnp.asarray(-jnp.inf, dt), self.s)

    def result(self):
        return jnp.concatenate(self.vals, axis=0), jnp.concatenate(self.picks, axis=0)


_PAIR_GROUPS = (
    ((0, 1, 0, 8),),
    ((0, 2, 0, 5), (5, 3, 0, 3)),
    ((0, 3, 3, 1), (1, 4, 0, 3), (4, 5, 0, 2), (6, 6, 0, 2)),
    ((0, 7, 0, 2), (2, 8, 0, -6)),
    ((0, 14, 0, -2),),
)


def _pair_candidates(first, second):
    h8 = SUBLANES
    (v1, i1), (v2, i2) = [tuple(a.astype(F32) for a in lst) for lst in (first, second)]
    row = lax.broadcasted_iota(jnp.int32, (h8, v1.shape[1]), 0)

    def build(a, b, combine, filler):
        groups = [combine(a[0:1], b)]
        for segments in _PAIR_GROUPS:
            group = jnp.full((h8, a.shape[1]), filler, F32)
            for r0, k1, k2, count in segments:
                if count > 0:
                    seg = combine(a[k1:k1 + 1], pltpu.roll(b[0:h8], (r0 - k2) % h8, 0))
                else:
                    seg = combine(pltpu.roll(a[h8:2 * h8], (r0 - (k1 - h8)) % h8, 0), b[0:1])
                group = jnp.where((row >= r0) & (row < r0 + abs(count)), seg, group)
            groups.append(group)
        return jnp.concatenate(groups, axis=0)

    return (build(v1, v2, lambda a, b: a + b, -jnp.inf),
            build(i1, i2, lambda a, b: a * N_KEYS + b, 0.0))


def _peer_fused_kernel(xs_ref, xm_ref, g_ref, wq_ref, sk_ref, u_ref, v_ref, *rest, split_blk):
    if split_blk is None:
        o_ref, q_scr, sel_e, sel_g, xn_scr, a_scr, bgt_scr, bg_scr, gmat, coef_scr = rest
        out_refs = ()
    else:
        *out_refs, q_scr, sel_e, sel_g, xn_scr, a_scr, bgt_scr, bg_scr, gmat, coef_scr, o_ref = rest
    i = pl.program_id(0)
    e = pl.program_id(1)
    n_steps = pl.num_programs(1)
    t = xm_ref.shape[0]
    n_half = t // SEL_T
    half_keys = N_KEYS // 2
    cur = i % 2

    def project_queries():
        xn = _rmsnorm(xs_ref[...], g_ref[...]).astype(BF16)
        q = jnp.dot(xn, wq_ref[...], preferred_element_type=F32).astype(BF16)
        for hf in range(n_half):
            for c in range(2 * PEER_HEADS):
                q_scr[hf, c] = q[hf * SEL_T:(hf + 1) * SEL_T, c * LANES:(c + 1) * LANES]

    @pl.when(e == 0)
    def _():
        @pl.when(i == 0)
        def _():
            project_queries()
            sel_e[...] = jnp.zeros_like(sel_e)
            sel_g[...] = jnp.zeros_like(sel_g)

        x = xm_ref[...]
        xn_scr[...] = _rmsnorm(x, g_ref[...]).astype(BF16)
        o_ref[...] = x
        coef_scr[0] = jnp.zeros(coef_scr.shape[1:], BF16)
        prev = 1 - cur
        lane_groups = SEL_T // LANES
        for hf in range(n_half):
            ef = sel_e[prev, hf]
            af = jnp.floor(ef * (1.0 / N_KEYS))
            bf = ef - af * N_KEYS
            gf = sel_g[prev, hf]
            a_scr[hf * SEL_T:(hf + 1) * SEL_T, :] = af.T
            bgf = bf + RSQRT2 * gf
            bgt_scr[hf * SEL_T:(hf + 1) * SEL_T, :] = bgf.T
            for lg in range(lane_groups):
                bg_scr[hf * lane_groups + lg] = bgf[:, lg * LANES:(lg + 1) * LANES]
        r = lax.broadcasted_iota(jnp.int32, (N_KEYS, LANES), 0)
        packed_shape = (N_KEYS // (2 * SUBLANES), 2 * SUBLANES, LANES)
        a_of_row = jnp.where(r < half_keys, 2 * r, 2 * (r - half_keys) + 1).astype(F32).astype(BF16)
        a_of_row = a_of_row.reshape(packed_shape)
        one = jnp.ones(packed_shape, BF16)
        zero = jnp.zeros(packed_shape, BF16)
        b_of_lane = lax.broadcasted_iota(jnp.int32, (N_KEYS, LANES), 1).astype(F32)
        b_of_row = r.astype(F32).astype(BF16).reshape(packed_shape)
        per_body = 32
        bodies_per_group = LANES // per_body
        row_major_every = 3

        def body(it, carry):
            grp = lax.shift_right_logical(it, bodies_per_group.bit_length() - 1)
            sub = it & (bodies_per_group - 1)
            shift = (LANES - sub * per_body) & (LANES - 1)
            bg = pltpu.roll(bg_scr[grp], shift, 1)
            tok0 = pl.multiple_of(it * per_body, per_body)
            arows = a_scr[pl.ds(tok0, per_body), :]
            bgrows = bgt_scr[pl.ds(tok0, per_body), :]
            for k in range(per_body):
                arow = jnp.broadcast_to(arows[k:k + 1], (2 * SUBLANES, LANES)).astype(BF16)[None]
                pa = jnp.where(a_of_row == arow, one, zero).reshape(N_KEYS, LANES)
                if k % row_major_every == 0:
                    bgrow = jnp.broadcast_to(bgrows[k:k + 1], (2 * SUBLANES, LANES))
                    brow = jnp.floor(bgrow)
                    grow = (bgrow - brow).astype(BF16)[None]
                    qb = jnp.where(b_of_row == brow.astype(BF16)[None], grow, zero).reshape(N_KEYS, LANES)
                    tile = lax.dot_general(pa, qb, NT_DIMS, preferred_element_type=F32)
                else:
                    bgcol = jnp.broadcast_to(bg[:, k:k + 1], (N_KEYS, LANES))
                    bcol = jnp.floor(bgcol)
                    gcol = bgcol - bcol
                    qbt = jnp.where(bcol == b_of_lane, gcol, 0.0).astype(BF16)
                    tile = jnp.dot(pa, qbt, preferred_element_type=F32)
                row0 = pl.multiple_of((tok0 + k) * G_PITCH, SUBLANES)
                gmat[pl.ds(row0, half_keys), :] = pltpu.pack_elementwise(
                    [tile[0:half_keys], tile[half_keys:N_KEYS]], packed_dtype=BF16)
            return carry

        lax.fori_loop(0, t // per_body, body, 0)

    unit = jnp.minimum(e, n_half * PEER_HEADS - 1)
    half = lax.shift_right_logical(unit, PEER_HEADS.bit_length() - 1)
    h = unit & (PEER_HEADS - 1)
    rd = e & 1
    blk = jnp.minimum(e, n_steps - 2)
    eb = u_ref.shape[0]
    n_stages = 4
    exp_w = eb // n_stages
    out_w = v_ref.shape[1] // n_stages
    words_per_stage = exp_w // (2 * N_KEYS)

    def value_piece(c):
        ocols = slice(c * out_w, (c + 1) * out_w)
        o_ref[:, ocols] += jnp.dot(coef_scr[rd], v_ref[:, ocols], preferred_element_type=F32)

    def act_piece(c):
        ecols = slice(c * exp_w, (c + 1) * exp_w)
        act = lax.dot_general(xn_scr[...], u_ref[ecols, :], NT_DIMS, preferred_element_type=F32)
        gates = []
        for w in range(words_per_stage):
            word = gmat[pl.ds((blk * n_stages + c) * words_per_stage + w, t, stride=G_PITCH), :]
            gates.append(lax.bitcast_convert_type(word << 16, F32))
            gates.append(lax.bitcast_convert_type(word & jnp.int32(-65536), F32))
        coef_scr[1 - rd, :, ecols] = (_scaled_gelu(act) * jnp.concatenate(gates, axis=1)).astype(BF16)

    def retrieval():
        lists = []
        for p in range(2):
            st = lax.dot_general(sk_ref[h, p], q_scr[half, 2 * h + p], NT_DIMS, preferred_element_type=F32)
            yield from _top16_of_keys(st, lists)
            yield
        cand, expert = _pair_candidates(lists[0], lists[1])
        second_level = _TopK(cand, payload=expert)
        for _ in range(4):
            second_level.step(PEER_TOPK // 4)
            yield
        best, e_sel = second_level.result()
        ex = jnp.exp(best - best[0:1])
        slot0 = pl.multiple_of(h * PEER_TOPK, PEER_TOPK)
        sel_e[cur, half, pl.ds(slot0, PEER_TOPK), :] = e_sel
        sel_g[cur, half, pl.ds(slot0, PEER_TOPK), :] = ex / jnp.sum(ex, axis=0, keepdims=True)

    pieces = [functools.partial(f, c) for f in (value_piece, act_piece) for c in range(n_stages)]
    parts = retrieval()
    parts_per_piece = 3
    for piece in pieces:
        piece()
        for _ in range(parts_per_piece):
            next(parts, None)
    for _ in parts:
        pass

    @pl.when(e == n_steps - 1)
    def _():
        project_queries()
        if out_refs:
            first_ref, second_ref = out_refs

            @pl.when(i - 1 < split_blk)
            def _():
                first_ref[...] = o_ref[...]

            @pl.when(i - 1 >= split_blk)
            def _():
                second_ref[...] = o_ref[...]


def _peer(x, g, wq, sk, u_all, v_all, layer, t=512, eb=1024, split_rows=None):
    n, d = x.shape
    nslot = PEER_HEADS * PEER_TOPK
    n_tok_blk = n // t
    n_exp_blk = N_KEYS * N_KEYS // eb
    n_half = t // SEL_T
    assert n_exp_blk == n_half * PEER_HEADS
    last_tok = n_tok_blk - 1
    once = pl.Buffered(1)
    if split_rows is None:
        split_blk = None
        out_specs = pl.BlockSpec((t, d), lambda i, e: (jnp.maximum(i - 1, 0), 0))
        out_shape = jax.ShapeDtypeStruct((n, d), F32)
        acc = []
    else:
        split_blk = split_rows // t
        out_specs = [pl.BlockSpec((t, d), lambda i, e: (jnp.clip(i - 1, 0, split_blk - 1), 0)),
                     pl.BlockSpec((t, d), lambda i, e: (jnp.clip(i - 1 - split_blk, 0, last_tok - split_blk), 0))]
        out_shape = [jax.ShapeDtypeStruct((split_rows, d), F32), jax.ShapeDtypeStruct((n - split_rows, d), F32)]
        acc = [pltpu.VMEM((t, d), F32)]
    return pl.pallas_call(
        functools.partial(_peer_fused_kernel, split_blk=split_blk),
        grid=(n_tok_blk + 1, n_exp_blk + 1),
        in_specs=[pl.BlockSpec((t, d), lambda i, e: (jnp.minimum(i + e // n_exp_blk, last_tok), 0),
                               pipeline_mode=once),
                  pl.BlockSpec((t, d), lambda i, e: (jnp.maximum(i - 1, 0), 0)),
                  pl.BlockSpec((1, d), lambda i, e: (0, 0)),
                  pl.BlockSpec((d, 2 * PEER_HEADS * LANES), lambda i, e: (0, 0), pipeline_mode=once),
                  pl.BlockSpec((PEER_HEADS, 2, N_KEYS, LANES), lambda i, e: (0, 0, 0, 0), pipeline_mode=once),
                  pl.BlockSpec((None, eb, d), lambda i, e: (layer, jnp.minimum(e, n_exp_blk - 1), 0)),
                  pl.BlockSpec((None, eb, d), lambda i, e: (layer, jnp.maximum(e - 1, 0), 0))],
        out_specs=out_specs,
        out_shape=out_shape,
        scratch_shapes=[pltpu.VMEM((n_half, 2 * PEER_HEADS, SEL_T, LANES), BF16),
                        pltpu.VMEM((2, n_half, nslot, SEL_T), F32),
                        pltpu.VMEM((2, n_half, nslot, SEL_T), F32),
                        pltpu.VMEM((t, d), BF16),
                        pltpu.VMEM((t, nslot), F32),
                        pltpu.VMEM((t, nslot), F32),
                        pltpu.VMEM((t // LANES, nslot, LANES), F32),
                        pltpu.VMEM((t * G_PITCH, LANES), jnp.int32),
                        pltpu.VMEM((2, t, eb), BF16)] + acc,
        compiler_params=_cparams(2),
        name="peer",
    )(x, x, g.reshape(1, d), wq, sk, u_all, v_all)


def _rope_tables(pos):
    half = HEAD_DIM // 2
    inv_freq = ROPE_THETA ** (-jnp.arange(half, dtype=F32) / half)
    ang = pos.astype(F32)[:, None] * inv_freq[None, :]
    cos, sin = jnp.cos(ang), jnp.sin(ang)
    reps = LANES // HEAD_DIM
    cos_t = jnp.tile(jnp.concatenate([cos, cos], axis=1), (1, reps))
    sin_t = jnp.tile(jnp.concatenate([-sin, sin], axis=1), (1, reps))
    return cos_t, sin_t


def kernel(x_prompt, x_sample, cache_conv, cache_win_k, cache_win_v, state_mlstm_C, state_mlstm_n,
           state_mlstm_m, norm_mix, norm_ffn, ab_w_in, ab_conv_w, ab_q_gain, ab_k_gain, ab_sinks, ab_w_out,
           ml_w_in, ml_gate_bias, ml_out_gain, ml_w_out, peer_w_q, peer_sub_keys, peer_u, peer_v):
    n_batch, seq, d = x_prompt.shape
    n_seq, t_len, _ = x_sample.shape
    n_prompt = n_batch * seq
    assert d == D_MODEL and t_len == SUBLANES and norm_mix.shape[0] == 2

    xp = x_prompt.reshape(n_prompt, d)
    xs = x_sample.reshape(n_seq * t_len, d)

    cos_p, sin_p = _rope_tables(jnp.arange(seq, dtype=jnp.int32))
    cos_s, sin_s = _rope_tables(PAST_LEN + jnp.arange(t_len, dtype=jnp.int32))
    bt = 16
    cos_s, sin_s = jnp.tile(cos_s, (bt, 1)), jnp.tile(sin_s, (bt, 1))
    lane = jnp.arange(LANES)
    seg = (lane[:, None] // HEAD_DIM == lane[None, :] // HEAD_DIM).astype(BF16)
    reps = LANES // HEAD_DIM
    qg = jnp.tile(ab_q_gain[0], reps).reshape(1, LANES)
    kg = jnp.tile(ab_k_gain[0], reps).reshape(1, LANES)
    wo_ab = ab_w_out[0].astype(BF16)

    z = _norm_proj([xp, xs], norm_mix[0], ab_w_in[0].astype(BF16))
    y_p, k_p, v_p, c_p = _ab_prompt(z, xp, n_batch, seq, cos_p, sin_p, ab_conv_w[0], qg, kg, ab_sinks[0],
                                    seg, wo_ab)
    y_s, c_s, k_s, v_s = _ab_sample(z, xs, n_prompt, n_seq, t_len, cos_s, sin_s, ab_conv_w[0], qg, kg,
                                    ab_sinks[0], seg, wo_ab, cache_conv[0],
                                    cache_win_k[0].reshape(n_seq, WINDOW, LANES),
                                    cache_win_v[0].reshape(n_seq, WINDOW, LANES), bt=bt)
    x = jnp.concatenate([y_p.reshape(n_prompt, d), y_s], axis=0)
    u_all = (peer_u * RSQRT2).astype(BF16)
    v_all = peer_v.astype(BF16)
    x = _peer(x, norm_ffn[0], peer_w_q[0].astype(BF16), peer_sub_keys[0].astype(BF16), u_all, v_all, 0)

    n_gate = 2 * ML_HEADS
    w_in = jnp.pad(ml_w_in[0], ((0, 0), (0, ML_IN_PAD - ml_w_in.shape[2]))).astype(BF16)
    bias = jnp.pad(ml_gate_bias[0], (0, LANES - n_gate)).reshape(1, LANES)
    og = ml_out_gain[0].reshape(1, D_MODEL)
    idx = jnp.arange(ML_CHUNK)
    tril = (idx[None, :] <= idx[:, None]).astype(BF16)
    wo_ml = ml_w_out[0].astype(BF16)

    z = _norm_proj([x], norm_mix[1], w_in)
    y_p, cm_p, nm_p, mm_p = _mlstm_prompt(z, x, n_batch, seq, bias, og, tril, wo_ml)
    y_s, cm_s, nm_s, mm_s = _mlstm_sample(z, x, n_prompt, n_seq, t_len, bias, og, tril, wo_ml,
                                          state_mlstm_C[0], state_mlstm_n[0],
                                          state_mlstm_m[0].reshape(n_seq, 1, ML_HEADS))
    x = jnp.concatenate([y_p.reshape(n_prompt, d), y_s], axis=0)
    out_p, out_s = _peer(x, norm_ffn[1], peer_w_q[1].astype(BF16), peer_sub_keys[1].astype(BF16), u_all, v_all, 1,
                         split_rows=n_prompt)

    y_prompt = out_p.reshape(n_batch, seq, d)
    y_sample = out_s.reshape(n_seq, t_len, d)
    kv_shape_p = (1, n_batch, WINDOW, N_KV_HEADS, HEAD_DIM)
    kv_shape_s = (1, n_seq, WINDOW, N_KV_HEADS, HEAD_DIM)
    return (y_prompt, y_sample,
            c_p[:, SUBLANES - 2:, :][None], k_p.reshape(kv_shape_p), v_p.reshape(kv_shape_p),
            cm_p[None], nm_p[None], mm_p.reshape(1, n_batch, ML_HEADS),
            c_s[None], k_s.reshape(kv_shape_s), v_s.reshape(kv_shape_s),
            cm_s[None], nm_s[None], mm_s.reshape(1, n_seq, ML_HEADS))
```

```python
import functools

import jax
import jax.numpy as jnp
from jax import lax
from jax.experimental import pallas as pl
from jax.experimental.pallas import tpu as pltpu

F32 = jnp.float32
BF16 = jnp.bfloat16
EPS = 1e-6

D_MODEL = 1024
CONV_DIM = 512
N_Q_HEADS = 8
N_KV_HEADS = 2
HEAD_DIM = 64
WINDOW = 128
ROPE_THETA = 10000.0
AB_IN = 2304
ML_HEADS = 4
ML_QK = 128
ML_V = 256
ML_CHUNK = 128
ML_GATE_COL = 3072
ML_IN_PAD = ML_GATE_COL + 128
N_KEYS = 128
PEER_HEADS = 8
PEER_TOPK = 16
PAST_LEN = 16384

LANES = 128
SUBLANES = 8
G_PITCH = N_KEYS // 2 + SUBLANES
VMEM_LIMIT = 56 * 1024 * 1024

NT_DIMS = (((1,), (1,)), ((), ()))
TN_DIMS = (((0,), (0,)), ((), ()))


def _cparams(n_axes, vmem=VMEM_LIMIT):
    return pltpu.CompilerParams(dimension_semantics=("arbitrary",) * n_axes, vmem_limit_bytes=vmem)


def _rmsnorm(x, g):
    return x * lax.rsqrt(jnp.mean(x * x, axis=-1, keepdims=True) + EPS) * g


def _bf16_pieces(a, terms):
    pieces = []
    rem = a
    for _ in range(terms):
        piece = rem.astype(BF16)
        rem = rem - piece.astype(F32)
        pieces.append(piece)
    return pieces


def _split_dot(a, b_bf16, terms=2):
    return sum(jnp.dot(p, b_bf16, preferred_element_type=F32) for p in _bf16_pieces(a, terms))


def _norm_proj_kernel(*refs, starts):
    x_refs = refs[:len(starts)]
    g_ref, w_ref, o_ref = refs[len(starts):]
    i = pl.program_id(0)
    x = x_refs[0][...]
    for x_ref, start in zip(x_refs[1:], starts[1:]):
        x = jnp.where(i >= start, x_ref[...], x)
    r = _rmsnorm(x, g_ref[...])
    o_ref[...] = jnp.dot(r.astype(BF16), w_ref[...], preferred_element_type=F32)


def _norm_proj(x_parts, g, w_bf16, tm=512):
    d = x_parts[0].shape[1]
    nout = w_bf16.shape[1]
    blocks = [p.shape[0] // tm for p in x_parts]
    starts = tuple(sum(blocks[:k]) for k in range(len(blocks)))
    part_spec = lambda start, nblk: pl.BlockSpec((tm, d), lambda i: (jnp.clip(i - start, 0, nblk - 1), 0))
    return pl.pallas_call(
        functools.partial(_norm_proj_kernel, starts=starts),
        grid=(sum(blocks),),
        in_specs=[part_spec(s, nb) for s, nb in zip(starts, blocks)] + [
            pl.BlockSpec((1, d), lambda i: (0, 0)),
            pl.BlockSpec((d, nout), lambda i: (0, 0))],
        out_specs=pl.BlockSpec((tm, nout), lambda i: (i, 0)),
        out_shape=jax.ShapeDtypeStruct((sum(blocks) * tm, nout), F32),
        compiler_params=_cparams(1),
        name="norm_proj",
    )(*x_parts, g.reshape(1, d), w_bf16)


def _headnorm_rope(xc, gain, cos, sin, seg, hi_half):
    ss = _split_dot(xc * xc, seg)
    xn = xc * lax.rsqrt(ss * (1.0 / HEAD_DIM) + EPS) * gain
    partner = jnp.where(hi_half, pltpu.roll(xn, 32, 1), pltpu.roll(xn, 96, 1))
    return xn * cos + partner * sin


def _softmax_sink(s, mask, sink):
    s = jnp.where(mask, s, -1e30)
    m = jnp.maximum(jnp.max(s, axis=-1, keepdims=True), sink)
    p = jnp.exp(s - m)
    denom = jnp.sum(p, axis=-1, keepdims=True) + jnp.exp(sink - m)
    return (p / denom).astype(BF16)


def _ab_prompt_kernel(sink_ref, *refs, n_batch):
    z_refs, x_refs = refs[0:n_batch], refs[n_batch:2 * n_batch]
    (cos_ref, sin_ref, cw_ref, qg_ref, kg_ref, seg_ref, wo_ref,
     y_ref, kst_ref, vst_ref, cst_ref, pk_ref, pv_ref, pu_ref) = refs[2 * n_batch:]
    j = pl.program_id(0)

    @pl.when(j == 0)
    def _():
        pk_ref[...] = jnp.zeros_like(pk_ref)
        pv_ref[...] = jnp.zeros_like(pv_ref)
        pu_ref[...] = jnp.zeros_like(pu_ref)

    for b in range(n_batch):
        _ab_prompt_block(j, sink_ref, z_refs[b], x_refs[b], cos_ref, sin_ref, cw_ref, qg_ref, kg_ref, seg_ref,
                         wo_ref, y_ref.at[b], kst_ref.at[b], vst_ref.at[b], cst_ref.at[b],
                         pk_ref.at[b], pv_ref.at[b], pu_ref.at[b])


def _ab_prompt_block(j, sink_ref, z_ref, x_ref, cos_ref, sin_ref, cw_ref, qg_ref, kg_ref, seg_ref, wo_ref,
                     y_ref, kst_ref, vst_ref, cst_ref, pk_ref, pv_ref, pu_ref):
    blk = z_ref.shape[0]
    cos = cos_ref[...]
    sin = sin_ref[...]
    seg = seg_ref[...]
    hi_half = (lax.broadcasted_iota(jnp.int32, (blk, LANES), 1) & 32) != 0

    gate_b = z_ref[:, 0:CONV_DIM]
    u = z_ref[:, CONV_DIM:2 * CONV_DIM] * z_ref[:, 2 * CONV_DIM:3 * CONV_DIM]
    ng = blk // SUBLANES
    u3 = u.reshape(ng, SUBLANES, CONV_DIM)
    ext = jnp.concatenate([pu_ref[...][None], u3], axis=0)
    t8 = lax.broadcasted_iota(jnp.int32, (ng, SUBLANES, CONV_DIM), 1)
    r1 = pltpu.roll(ext, 1, 1)
    r2 = pltpu.roll(ext, 2, 1)
    um1 = jnp.where(t8 >= 1, r1[1:], r1[:-1])
    um2 = jnp.where(t8 >= 2, r2[1:], r2[:-1])
    cw = cw_ref[...]
    conv = cw[0:1][None] * um2 + cw[1:2][None] * um1 + cw[2:3][None] * u3
    yconv = gate_b * conv.reshape(blk, CONV_DIM)

    q0 = 3 * CONV_DIM
    k0 = q0 + N_Q_HEADS * HEAD_DIM
    v0 = k0 + N_KV_HEADS * HEAD_DIM
    qg = qg_ref[...]
    qr = [_headnorm_rope(z_ref[:, q0 + c * LANES:q0 + (c + 1) * LANES], qg, cos, sin, seg, hi_half)
          for c in range(N_Q_HEADS * HEAD_DIM // LANES)]
    kr = _headnorm_rope(z_ref[:, k0:k0 + LANES], kg_ref[...], cos, sin, seg, hi_half)
    v = z_ref[:, v0:v0 + LANES]
    pk = pk_ref[...]
    pv = pv_ref[...]

    row = lax.broadcasted_iota(jnp.int32, (blk, 2 * blk), 0)
    col = lax.broadcasted_iota(jnp.int32, (blk, 2 * blk), 1)
    row_prev = row + jnp.where(j == 0, 2 * blk, 0)
    mask = ((col < blk) & (col > row_prev)) | ((col >= blk) & (col - blk <= row))

    kks, vvs = [], []
    for g in range(N_KV_HEADS):
        sl = slice(g * HEAD_DIM, (g + 1) * HEAD_DIM)
        kks.append(jnp.concatenate([pk[:, sl], kr[:, sl]], axis=0).astype(BF16))
        vvs.append(jnp.concatenate([pv[:, sl], v[:, sl]], axis=0).astype(BF16))
    outs = []
    for h in range(N_Q_HEADS):
        g = h // (N_Q_HEADS // N_KV_HEADS)
        qh = qr[h // 2][:, (h % 2) * HEAD_DIM:(h % 2 + 1) * HEAD_DIM].astype(BF16)
        s = lax.dot_general(qh, kks[g], NT_DIMS, preferred_element_type=F32) * (HEAD_DIM ** -0.5)
        p = _softmax_sink(s, mask, sink_ref[h])
        outs.append(jnp.dot(p, vvs[g], preferred_element_type=F32))
    attn = jnp.concatenate(outs, axis=1)

    y = (jnp.dot(yconv.astype(BF16), wo_ref[0:CONV_DIM, :], preferred_element_type=F32)
         + jnp.dot(attn.astype(BF16), wo_ref[CONV_DIM:2 * CONV_DIM, :], preferred_element_type=F32))
    y_ref[...] = y + x_ref[...]

    pk_ref[...] = kr
    pv_ref[...] = v
    pu_ref[...] = u3[ng - 1]
    kst_ref[...] = kr
    vst_ref[...] = v
    cst_ref[...] = u3[ng - 1]


def _ab_prompt(z, x, n_batch, seq, cos, sin, cw, qg, kg, sinks, seg, wo):
    blk = WINDOW
    nb = seq // blk
    const = lambda j, s: (0, 0)
    whole = lambda j, s: (0, 0, 0)
    tok_specs = lambda width: [pl.BlockSpec((blk, width), functools.partial(lambda j, s, b: (b * nb + j, 0), b=b))
                               for b in range(n_batch)]
    grid_spec = pltpu.PrefetchScalarGridSpec(
        num_scalar_prefetch=1,
        grid=(nb,),
        in_specs=tok_specs(AB_IN) + tok_specs(D_MODEL) + [
            pl.BlockSpec((blk, LANES), lambda j, s: (j, 0)),
            pl.BlockSpec((blk, LANES), lambda j, s: (j, 0)),
            pl.BlockSpec((3, CONV_DIM), const),
            pl.BlockSpec((1, LANES), const),
            pl.BlockSpec((1, LANES), const),
            pl.BlockSpec((LANES, LANES), const),
            pl.BlockSpec((D_MODEL, D_MODEL), const)],
        out_specs=[pl.BlockSpec((n_batch, blk, D_MODEL), lambda j, s: (0, j, 0)),
                   pl.BlockSpec((n_batch, blk, LANES), whole),
                   pl.BlockSpec((n_batch, blk, LANES), whole),
                   pl.BlockSpec((n_batch, SUBLANES, CONV_DIM), whole)],
        scratch_shapes=[pltpu.VMEM((n_batch, blk, LANES), F32), pltpu.VMEM((n_batch, blk, LANES), F32),
                        pltpu.VMEM((n_batch, SUBLANES, CONV_DIM), F32)])
    return pl.pallas_call(
        functools.partial(_ab_prompt_kernel, n_batch=n_batch),
        grid_spec=grid_spec,
        out_shape=[jax.ShapeDtypeStruct((n_batch, seq, D_MODEL), F32),
                   jax.ShapeDtypeStruct((n_batch, blk, LANES), F32),
                   jax.ShapeDtypeStruct((n_batch, blk, LANES), F32),
                   jax.ShapeDtypeStruct((n_batch, SUBLANES, CONV_DIM), F32)],
        compiler_params=_cparams(1),
        name="ab_prompt",
    )(sinks, *([z] * n_batch), *([x] * n_batch), cos, sin, cw, qg, kg, seg, wo)


def _ab_sample_kernel(sink_ref, z_ref, x_ref, cos_ref, sin_ref, cw_ref, qg_ref, kg_ref, seg_ref, wo_ref,
                      cc_ref, ck_ref, cv_ref, y_ref, cs_ref, ks_ref, vs_ref):
    rows = z_ref.shape[0]
    t_len = SUBLANES
    bt = rows // t_len
    cos = cos_ref[...]
    sin = sin_ref[...]
    seg = seg_ref[...]
    hi_half = (lax.broadcasted_iota(jnp.int32, (rows, LANES), 1) & 32) != 0

    gate_b = z_ref[:, 0:CONV_DIM]
    u = z_ref[:, CONV_DIM:2 * CONV_DIM] * z_ref[:, 2 * CONV_DIM:3 * CONV_DIM]
    u3 = u.reshape(bt, t_len, CONV_DIM)
    cc = cc_ref[...]
    c0 = cc[:, 0:1, :]
    c1 = cc[:, 1:2, :]
    t8 = lax.broadcasted_iota(jnp.int32, (bt, t_len, CONV_DIM), 1)
    r1 = pltpu.roll(u3, 1, 1)
    r2 = pltpu.roll(u3, 2, 1)
    um1 = jnp.where(t8 >= 1, r1, c1)
    um2 = jnp.where(t8 >= 2, r2, jnp.where(t8 == 1, c1, c0))
    cw = cw_ref[...]
    conv = cw[0:1][None] * um2 + cw[1:2][None] * um1 + cw[2:3][None] * u3
    yconv = gate_b * conv.reshape(rows, CONV_DIM)
    cs_ref[...] = r2[:, 0:2, :]

    q0 = 3 * CONV_DIM
    k0 = q0 + N_Q_HEADS * HEAD_DIM
    v0 = k0 + N_KV_HEADS * HEAD_DIM
    qg = qg_ref[...]
    qr = [_headnorm_rope(z_ref[:, q0 + c * LANES:q0 + (c + 1) * LANES], qg, cos, sin, seg, hi_half)
          for c in range(N_Q_HEADS * HEAD_DIM // LANES)]
    kr = _headnorm_rope(z_ref[:, k0:k0 + LANES], kg_ref[...], cos, sin, seg, hi_half)
    v = z_ref[:, v0:v0 + LANES]

    group = N_Q_HEADS // N_KV_HEADS
    nq = group * t_len
    nk = 2 * WINDOW
    qrow = lax.broadcasted_iota(jnp.int32, (nq, nk), 0)
    t_q = qrow & (t_len - 1)
    col = lax.broadcasted_iota(jnp.int32, (nq, nk), 1)
    mask = (((col < WINDOW) & (col > t_q)) | ((col >= WINDOW) & (col - WINDOW <= t_q)))[None]
    hrow = lax.broadcasted_iota(jnp.int32, (nq, 1), 0) // t_len
    pad = jnp.zeros((bt, nk - WINDOW - t_len, HEAD_DIM), F32)

    outs = [None] * N_Q_HEADS
    for g in range(N_KV_HEADS):
        sl = slice(g * HEAD_DIM, (g + 1) * HEAD_DIM)
        qs = jnp.concatenate(
            [qr[h // 2][:, (h % 2) * HEAD_DIM:(h % 2 + 1) * HEAD_DIM].reshape(bt, t_len, HEAD_DIM)
             for h in range(g * group, (g + 1) * group)], axis=1)
        kk = jnp.concatenate([ck_ref[:, :, sl], kr[:, sl].reshape(bt, t_len, HEAD_DIM), pad], axis=1)
        vv = jnp.concatenate([cv_ref[:, :, sl], v[:, sl].reshape(bt, t_len, HEAD_DIM), pad], axis=1)
        s = jnp.einsum('bqd,bkd->bqk', qs.astype(BF16), kk.astype(BF16),
                       preferred_element_type=F32) * (HEAD_DIM ** -0.5)
        sink = jnp.zeros((nq, 1), F32)
        for hh in range(group):
            sink = jnp.where(hrow == hh, sink_ref[g * group + hh], sink)
        p = _softmax_sink(s, mask, sink[None])
        o = jnp.einsum('bqk,bkd->bqd', p, vv.astype(BF16), preferred_element_type=F32)
        for hh in range(group):
            outs[g * group + hh] = o[:, hh * t_len:(hh + 1) * t_len, :].reshape(rows, HEAD_DIM)
    attn = jnp.concatenate(outs, axis=1)

    y = (jnp.dot(yconv.astype(BF16), wo_ref[0:CONV_DIM, :], preferred_element_type=F32)
         + jnp.dot(attn.astype(BF16), wo_ref[CONV_DIM:2 * CONV_DIM, :], preferred_element_type=F32))
    y_ref[...] = y + x_ref[...]

    keep = WINDOW - t_len
    ks_ref[:, 0:keep, :] = ck_ref[:, t_len:WINDOW, :]
    ks_ref[:, keep:WINDOW, :] = kr.reshape(bt, t_len, LANES)
    vs_ref[:, 0:keep, :] = cv_ref[:, t_len:WINDOW, :]
    vs_ref[:, keep:WINDOW, :] = v.reshape(bt, t_len, LANES)


def _ab_sample(z, x, row0, n_seq, t_len, cos, sin, cw, qg, kg, sinks, seg, wo, cc, ck, cv, bt=16):
    rows = bt * t_len
    blk0 = row0 // rows
    tok = lambda i, s: (blk0 + i, 0)
    const = lambda i, s: (0, 0)
    seq3 = lambda i, s: (i, 0, 0)
    grid_spec = pltpu.PrefetchScalarGridSpec(
        num_scalar_prefetch=1,
        grid=(n_seq // bt,),
        in_specs=[pl.BlockSpec((rows, AB_IN), tok),
                  pl.BlockSpec((rows, D_MODEL), lambda i, s: (i, 0)),
                  pl.BlockSpec((rows, LANES), const),
                  pl.BlockSpec((rows, LANES), const),
                  pl.BlockSpec((3, CONV_DIM), const),
                  pl.BlockSpec((1, LANES), const),
                  pl.BlockSpec((1, LANES), const),
                  pl.BlockSpec((LANES, LANES), const),
                  pl.BlockSpec((D_MODEL, D_MODEL), const),
                  pl.BlockSpec((bt, 2, CONV_DIM), seq3),
                  pl.BlockSpec((bt, WINDOW, LANES), seq3),
                  pl.BlockSpec((bt, WINDOW, LANES), seq3)],
        out_specs=[pl.BlockSpec((rows, D_MODEL), lambda i, s: (i, 0)),
                   pl.BlockSpec((bt, 2, CONV_DIM), seq3),
                   pl.BlockSpec((bt, WINDOW, LANES), seq3),
                   pl.BlockSpec((bt, WINDOW, LANES), seq3)])
    return pl.pallas_call(
        _ab_sample_kernel,
        grid_spec=grid_spec,
        out_shape=[jax.ShapeDtypeStruct((n_seq * t_len, D_MODEL), F32),
                   jax.ShapeDtypeStruct((n_seq, 2, CONV_DIM), F32),
                   jax.ShapeDtypeStruct((n_seq, WINDOW, LANES), F32),
                   jax.ShapeDtypeStruct((n_seq, WINDOW, LANES), F32)],
        compiler_params=_cparams(1),
        name="ab_sample",
    )(sinks, z, x, cos, sin, cw, qg, kg, seg, wo, cc, ck, cv)


def _log_sigmoid(x):
    return jnp.minimum(x, 0.0) - jnp.log(1.0 + jnp.exp(-jnp.abs(x)))


def _mlstm_chunk(z, bias, og, tril, c_src, n_src, m_src, c_dst, n_dst, m_dst, n_real, rows):
    L = z.shape[0]
    gates = z[:, ML_GATE_COL:ML_GATE_COL + LANES] + bias
    if n_real < L:
        live = lax.broadcasted_iota(jnp.int32, (L, LANES), 0) < n_real
        li_all = jnp.where(live, gates, -1e30)
        lf_all = jnp.where(live, _log_sigmoid(gates), 0.0)
    else:
        li_all = gates
        lf_all = _log_sigmoid(gates)
    lf_pieces = _bf16_pieces(lf_all, 3)
    f_col_all = sum(jnp.dot(tril[0:rows], p, preferred_element_type=F32) for p in lf_pieces)
    f_row_all = sum(lax.dot_general(p, tril, (((0,), (1,)), ((), ())), preferred_element_type=F32)
                    for p in lf_pieces)
    li_t = li_all.T
    rr = lax.broadcasted_iota(jnp.int32, (rows, L), 0)
    cc = lax.broadcasted_iota(jnp.int32, (rows, L), 1)
    causal = cc <= rr

    outs, m_new_all = [], []
    for h in range(ML_HEADS):
        f_col = f_col_all[:, ML_HEADS + h:ML_HEADS + h + 1]
        f_row = f_row_all[ML_HEADS + h:ML_HEADS + h + 1, :]
        li_row = li_t[h:h + 1, :]
        li_col = li_all[0:rows, h:h + 1]
        m0 = m_src[0:1, h:h + 1]
        c0 = c_src[h]
        n0 = n_src[h:h + 1, :]
        qh = z[0:rows, h * ML_QK:(h + 1) * ML_QK]
        kh = z[:, ML_HEADS * ML_QK + h * ML_QK:ML_HEADS * ML_QK + (h + 1) * ML_QK] * (ML_QK ** -0.5)
        v_off = 2 * ML_HEADS * ML_QK
        vh = z[:, v_off + h * ML_V:v_off + (h + 1) * ML_V]
        o_off = v_off + ML_HEADS * ML_V
        oh = z[0:rows, o_off + h * ML_V:o_off + (h + 1) * ML_V]
        qb = qh.astype(BF16)
        vb = vh.astype(BF16)

        dmat = jnp.where(causal, f_col - f_row + li_row, -jnp.inf)
        gcar = f_col + m0
        m_t = jnp.maximum(jnp.max(dmat, axis=-1, keepdims=True), gcar)
        w = jnp.exp(dmat - m_t)
        s = lax.dot_general(qb, kh.astype(BF16), NT_DIMS, preferred_element_type=F32) * w
        carry = jnp.exp(gcar - m_t)
        num = (jnp.dot(s.astype(BF16), vb, preferred_element_type=F32)
               + jnp.dot(qb, c0.astype(BF16), preferred_element_type=F32) * carry)
        den = jnp.sum(s, axis=-1, keepdims=True) + carry * jnp.sum(qh * n0, axis=-1, keepdims=True)
        hout = num / jnp.maximum(jnp.abs(den), jnp.exp(-m_t))

        f_last = f_col[n_real - 1:n_real, :]
        w_end = f_last - f_col + li_col
        m_new = jnp.maximum(f_last + m0, jnp.max(w_end, axis=0, keepdims=True))
        a_end = jnp.exp(w_end - m_new)
        scale = jnp.exp(f_last + m0 - m_new)
        ka = kh[0:rows] * a_end
        c_dst[h] = scale * c0 + lax.dot_general(ka.astype(BF16), vb[0:rows], TN_DIMS,
                                                preferred_element_type=F32)
        n_dst[h:h + 1, :] = scale * n0 + jnp.sum(ka, axis=0, keepdims=True)
        m_new_all.append(m_new)

        hn = _rmsnorm(hout, og[:, h * ML_V:(h + 1) * ML_V])
        outs.append(jax.nn.sigmoid(oh) * hn)
    m_dst[...] = jnp.concatenate(m_new_all, axis=1)
    return jnp.concatenate(outs, axis=1)


def _mlstm_prompt_kernel(*refs, n_batch):
    z_refs, x_refs = refs[0:n_batch], refs[n_batch:2 * n_batch]
    bias_ref, og_ref, tril_ref, wo_ref, y_ref, c_ref, n_ref, m_ref = refs[2 * n_batch:]

    @pl.when(pl.program_id(0) == 0)
    def _():
        c_ref[...] = jnp.zeros_like(c_ref)
        n_ref[...] = jnp.zeros_like(n_ref)
        m_ref[...] = jnp.zeros_like(m_ref)

    for b in range(n_batch):
        state = (c_ref.at[b], n_ref.at[b], m_ref.at[b])
        out = _mlstm_chunk(z_refs[b][...], bias_ref[...], og_ref[...], tril_ref[...], *state, *state,
                           ML_CHUNK, ML_CHUNK)
        y_ref[b] = jnp.dot(out.astype(BF16), wo_ref[...], preferred_element_type=F32) + x_refs[b][...]


def _mlstm_sample_kernel(z_ref, x_ref, bias_ref, og_ref, tril_ref, wo_ref, c0_ref, n0_ref, m0_ref,
                         y_ref, c_ref, n_ref, m_ref, *, t_len):
    n_here = z_ref.shape[0] // t_len
    q_rows = 2 * SUBLANES
    outs = []
    for s in range(n_here):
        zpad = jnp.concatenate([z_ref[s * t_len:(s + 1) * t_len, :],
                                jnp.zeros((ML_CHUNK - t_len, ML_IN_PAD), F32)], axis=0)
        out = _mlstm_chunk(zpad, bias_ref[...], og_ref[...], tril_ref[...],
                           c0_ref.at[s], n0_ref.at[s], m0_ref.at[s], c_ref.at[s], n_ref.at[s], m_ref.at[s],
                           t_len, q_rows)
        outs.append(out[0:t_len])
    out_all = jnp.concatenate(outs, axis=0).astype(BF16)
    y_ref[...] = jnp.dot(out_all, wo_ref[...], preferred_element_type=F32) + x_ref[...]


def _mlstm_weight_specs(const):
    return [pl.BlockSpec((1, LANES), const),
            pl.BlockSpec((1, D_MODEL), const),
            pl.BlockSpec((ML_CHUNK, ML_CHUNK), const),
            pl.BlockSpec((D_MODEL, D_MODEL), const)]


def _mlstm_prompt(z, x, n_batch, seq, bias, og, tril, wo):
    nc = seq // ML_CHUNK
    const = lambda j: (0, 0)
    tok_specs = lambda width: [pl.BlockSpec((ML_CHUNK, width), functools.partial(lambda j, b: (b * nc + j, 0), b=b))
                               for b in range(n_batch)]
    return pl.pallas_call(
        functools.partial(_mlstm_prompt_kernel, n_batch=n_batch),
        grid=(nc,),
        in_specs=tok_specs(ML_IN_PAD) + tok_specs(D_MODEL) + _mlstm_weight_specs(const),
        out_specs=[pl.BlockSpec((n_batch, ML_CHUNK, D_MODEL), lambda j: (0, j, 0)),
                   pl.BlockSpec((n_batch, ML_HEADS, ML_QK, ML_V), lambda j: (0, 0, 0, 0)),
                   pl.BlockSpec((n_batch, ML_HEADS, ML_QK), lambda j: (0, 0, 0)),
                   pl.BlockSpec((n_batch, 1, ML_HEADS), lambda j: (0, 0, 0))],
        out_shape=[jax.ShapeDtypeStruct((n_batch, seq, D_MODEL), F32),
                   jax.ShapeDtypeStruct((n_batch, ML_HEADS, ML_QK, ML_V), F32),
                   jax.ShapeDtypeStruct((n_batch, ML_HEADS, ML_QK), F32),
                   jax.ShapeDtypeStruct((n_batch, 1, ML_HEADS), F32)],
        compiler_params=_cparams(1),
        name="mlstm_prompt",
    )(*([z] * n_batch), *([x] * n_batch), bias, og, tril, wo)


def _mlstm_sample(z, x, row0, n_seq, t_len, bias, og, tril, wo, c0, n0, m0, seqs_per_step=8):
    rows = seqs_per_step * t_len
    blk0 = row0 // rows
    tok = lambda i: (blk0 + i, 0)
    const = lambda i: (0, 0)
    st4 = lambda i: (i, 0, 0, 0)
    st3 = lambda i: (i, 0, 0)
    state_specs = [pl.BlockSpec((seqs_per_step, ML_HEADS, ML_QK, ML_V), st4),
                   pl.BlockSpec((seqs_per_step, ML_HEADS, ML_QK), st3),
                   pl.BlockSpec((seqs_per_step, 1, ML_HEADS), st3)]
    return pl.pallas_call(
        functools.partial(_mlstm_sample_kernel, t_len=t_len),
        grid=(n_seq // seqs_per_step,),
        in_specs=[pl.BlockSpec((rows, ML_IN_PAD), tok),
                  pl.BlockSpec((rows, D_MODEL), tok)] + _mlstm_weight_specs(const) + state_specs,
        out_specs=[pl.BlockSpec((rows, D_MODEL), lambda i: (i, 0))] + state_specs,
        out_shape=[jax.ShapeDtypeStruct((n_seq * t_len, D_MODEL), F32),
                   jax.ShapeDtypeStruct((n_seq, ML_HEADS, ML_QK, ML_V), F32),
                   jax.ShapeDtypeStruct((n_seq, ML_HEADS, ML_QK), F32),
                   jax.ShapeDtypeStruct((n_seq, 1, ML_HEADS), F32)],
        compiler_params=_cparams(1),
        name="mlstm_sample",
    )(z, x, bias, og, tril, wo, c0, n0, m0)


RSQRT2 = 0.7071067811865476


def _scaled_gelu(xs):
    return xs * (1.0 + lax.erf(xs))


SEL_T = 256


def _batcher_pairs(n):
    pairs = []
    p = 1
    while p < n:
        k = p
        while k >= 1:
            for j in range(k % p, n - k, 2 * k):
                for i in range(min(k, n - j - k)):
                    if (i + j) // (2 * p) == (i + j + k) // (2 * p):
                        pairs.append((i + j, i + j + k))
            k //= 2
        p *= 2
    return pairs


def _bitonic_merge_pairs(n):
    pairs = []
    s = n // 2
    while s >= 1:
        pairs += [(i, i + s) for i in range(n) if not i & s]
        s //= 2
    return pairs


def _top16_of_keys(s, out):
    n_vreg = N_KEYS // SUBLANES
    assert n_vreg == PEER_TOPK and N_KEYS == 128
    rounded = (s + 0.0).astype(BF16).astype(F32)
    b16 = lax.shift_right_logical(lax.bitcast_convert_type(rounded, jnp.int32), 16)
    code = jnp.where(b16 >= 0x8000, b16 ^ 0xFFFF, b16 | 0x8000)
    row = lax.broadcasted_iota(jnp.int32, s.shape, 0)
    keys = lax.bitcast_convert_type((code * N_KEYS + (N_KEYS - 1 - row)) | 0x4B000000, F32)

    x = [keys[v * SUBLANES:(v + 1) * SUBLANES] for v in range(n_vreg)]

    def exchange(i, j):
        x[i], x[j] = jnp.maximum(x[i], x[j]), jnp.minimum(x[i], x[j])

    for i, j in _batcher_pairs(n_vreg):
        exchange(i, j)
    yield
    for shift in (SUBLANES // 2, SUBLANES // 4, SUBLANES // 8):
        other = [pltpu.roll(a, shift, 0) for a in x]
        x = [jnp.maximum(x[i], other[n_vreg - 1 - i]) for i in range(n_vreg)]
        for i, j in _bitonic_merge_pairs(n_vreg):
            exchange(i, j)
        yield
    top = lax.bitcast_convert_type(jnp.concatenate([a[0:1] for a in x], axis=0), jnp.int32) & 0x7FFFFF
    rows = (N_KEYS - 1 - (top & (N_KEYS - 1))).astype(F32)
    code = lax.shift_right_logical(top, 7)
    b16 = jnp.where(code >= 0x8000, code ^ 0x8000, code ^ 0xFFFF)
    out.append((lax.bitcast_convert_type(b16 << 16, F32), rows))


class _TopK:
    PAYLOAD_RANGE = N_KEYS * N_KEYS

    def __init__(self, s, payload):
        assert s.shape[0] * self.PAYLOAD_RANGE < 2 ** 24
        self.s = s
        rows = lax.broadcasted_iota(jnp.int32, s.shape, 0).astype(F32)
        self.code = rows * self.PAYLOAD_RANGE + payload
        self.vals, self.codes = [], []

    def step(self, n):
        bound = float(self.s.shape[0] * self.PAYLOAD_RANGE)
        for _ in range(n):
            m = jnp.max(self.s, axis=0, keepdims=True)
            first = jnp.min(jnp.where(self.s == m, self.code, bound), axis=0, keepdims=True)
            self.vals.append(m)
            self.codes.append(first)
            self.s = jnp.where(self.code == first, -jnp.inf, self.s)

    def result(self):
        codes = jnp.concatenate(self.codes, axis=0)
        payload = codes - jnp.floor(codes * (1.0 / self.PAYLOAD_RANGE)) * self.PAYLOAD_RANGE
        return jnp.concatenate(self.vals, axis=0), payload


_PAIR_GROUPS = (
    ((0, 1, 0, 8),),
    ((0, 2, 0, 5), (5, 3, 0, 3)),
    ((0, 3, 3, 1), (1, 4, 0, 3), (4, 5, 0, 2), (6, 6, 0, 2)),
    ((0, 7, 0, 2), (2, 8, 0, -6)),
    ((0, 14, 0, -2),),
)


def _pair_candidates(first, second):
    h8 = SUBLANES
    (v1, i1), (v2, i2) = [tuple(a.astype(F32) for a in lst) for lst in (first, second)]
    row = lax.broadcasted_iota(jnp.int32, (h8, v1.shape[1]), 0)

    def build(a, b, combine, filler):
        groups = [combine(a[0:1], b)]
        for segments in _PAIR_GROUPS:
            group = jnp.full((h8, a.shape[1]), filler, F32)
            for r0, k1, k2, count in segments:
                if count > 0:
                    seg = combine(a[k1:k1 + 1], pltpu.roll(b[0:h8], (r0 - k2) % h8, 0))
                else:
                    seg = combine(pltpu.roll(a[h8:2 * h8], (r0 - (k1 - h8)) % h8, 0), b[0:1])
                group = jnp.where((row >= r0) & (row < r0 + abs(count)), seg, group)
            groups.append(group)
        return jnp.concatenate(groups, axis=0)

    return (build(v1, v2, lambda a, b: a + b, -jnp.inf),
            build(i1, i2, lambda a, b: a * N_KEYS + b, 0.0))


def _peer_fused_kernel(xs_ref, xm_ref, g_ref, wq_ref, sk_ref, u_ref, v_ref, *rest, split_blk):
    if split_blk is None:
        o_ref, q_scr, sel_e, sel_g, xn_scr, a_scr, bgt_scr, bg_scr, gmat, coef_scr = rest
        out_refs = ()
    else:
        *out_refs, q_scr, sel_e, sel_g, xn_scr, a_scr, bgt_scr, bg_scr, gmat, coef_scr, o_ref = rest
    i = pl.program_id(0)
    e = pl.program_id(1)
    n_steps = pl.num_programs(1)
    t = xm_ref.shape[0]
    n_half = t // SEL_T
    half_keys = N_KEYS // 2
    cur = i % 2

    def project_queries():
        xn = _rmsnorm(xs_ref[...], g_ref[...]).astype(BF16)
        q = jnp.dot(xn, wq_ref[...], preferred_element_type=F32).astype(BF16)
        for hf in range(n_half):
            for c in range(2 * PEER_HEADS):
                q_scr[hf, c] = q[hf * SEL_T:(hf + 1) * SEL_T, c * LANES:(c + 1) * LANES]

    @pl.when(e == 0)
    def _():
        @pl.when(i == 0)
        def _():
            project_queries()
            sel_e[...] = jnp.zeros_like(sel_e)
            sel_g[...] = jnp.zeros_like(sel_g)

        x = xm_ref[...]
        xn_scr[...] = _rmsnorm(x, g_ref[...]).astype(BF16)
        o_ref[...] = x
        coef_scr[0] = jnp.zeros(coef_scr.shape[1:], BF16)
        prev = 1 - cur
        lane_groups = SEL_T // LANES
        for hf in range(n_half):
            ef = sel_e[prev, hf]
            af = jnp.floor(ef * (1.0 / N_KEYS))
            bf = ef - af * N_KEYS
            gf = sel_g[prev, hf]
            a_scr[hf * SEL_T:(hf + 1) * SEL_T, :] = af.T
            bgf = bf + RSQRT2 * gf
            bgt_scr[hf * SEL_T:(hf + 1) * SEL_T, :] = bgf.T
            for lg in range(lane_groups):
                bg_scr[hf * lane_groups + lg] = bgf[:, lg * LANES:(lg + 1) * LANES]
        r = lax.broadcasted_iota(jnp.int32, (N_KEYS, LANES), 0)
        packed_shape = (N_KEYS // (2 * SUBLANES), 2 * SUBLANES, LANES)
        a_of_row = jnp.where(r < half_keys, 2 * r, 2 * (r - half_keys) + 1).astype(F32).astype(BF16)
        a_of_row = a_of_row.reshape(packed_shape)
        one = jnp.ones(packed_shape, BF16)
        zero = jnp.zeros(packed_shape, BF16)
        b_of_lane = lax.broadcasted_iota(jnp.int32, (N_KEYS, LANES), 1).astype(F32)
        b_of_row = r.astype(F32).astype(BF16).reshape(packed_shape)
        per_body = 32
        bodies_per_group = LANES // per_body
        row_major_every = 3

        def body(it, carry):
            grp = lax.shift_right_logical(it, bodies_per_group.bit_length() - 1)
            sub = it & (bodies_per_group - 1)
            shift = (LANES - sub * per_body) & (LANES - 1)
            bg = pltpu.roll(bg_scr[grp], shift, 1)
            tok0 = pl.multiple_of(it * per_body, per_body)
            arows = a_scr[pl.ds(tok0, per_body), :]
            bgrows = bgt_scr[pl.ds(tok0, per_body), :]
            for k in range(per_body):
                arow = jnp.broadcast_to(arows[k:k + 1], (2 * SUBLANES, LANES)).astype(BF16)[None]
                pa = jnp.where(a_of_row == arow, one, zero).reshape(N_KEYS, LANES)
                if k % row_major_every == 0:
                    bgrow = jnp.broadcast_to(bgrows[k:k + 1], (2 * SUBLANES, LANES))
                    brow = jnp.floor(bgrow)
                    grow = (bgrow - brow).astype(BF16)[None]
                    qb = jnp.where(b_of_row == brow.astype(BF16)[None], grow, zero).reshape(N_KEYS, LANES)
                    tile = lax.dot_general(pa, qb, NT_DIMS, preferred_element_type=F32)
                else:
                    bgcol = jnp.broadcast_to(bg[:, k:k + 1], (N_KEYS, LANES))
                    bcol = jnp.floor(bgcol)
                    gcol = bgcol - bcol
                    qbt = jnp.where(bcol == b_of_lane, gcol, 0.0).astype(BF16)
                    tile = jnp.dot(pa, qbt, preferred_element_type=F32)
                row0 = pl.multiple_of((tok0 + k) * G_PITCH, SUBLANES)
                gmat[pl.ds(row0, half_keys), :] = pltpu.pack_elementwise(
                    [tile[0:half_keys], tile[half_keys:N_KEYS]], packed_dtype=BF16)
            return carry

        lax.fori_loop(0, t // per_body, body, 0)

    unit = jnp.minimum(e, n_half * PEER_HEADS - 1)
    half = lax.shift_right_logical(unit, PEER_HEADS.bit_length() - 1)
    h = unit & (PEER_HEADS - 1)
    rd = e & 1
    blk = jnp.minimum(e, n_steps - 2)
    eb = u_ref.shape[0]
    n_stages = 4
    exp_w = eb // n_stages
    out_w = v_ref.shape[1] // n_stages
    words_per_stage = exp_w // (2 * N_KEYS)

    def value_piece(c):
        ocols = slice(c * out_w, (c + 1) * out_w)
        o_ref[:, ocols] += jnp.dot(coef_scr[rd], v_ref[:, ocols], preferred_element_type=F32)

    def act_piece(c):
        ecols = slice(c * exp_w, (c + 1) * exp_w)
        act = lax.dot_general(xn_scr[...], u_ref[ecols, :], NT_DIMS, preferred_element_type=F32)
        gates = []
        for w in range(words_per_stage):
            word = gmat[pl.ds((blk * n_stages + c) * words_per_stage + w, t, stride=G_PITCH), :]
            gates.append(lax.bitcast_convert_type(word << 16, F32))
            gates.append(lax.bitcast_convert_type(word & jnp.int32(-65536), F32))
        coef_scr[1 - rd, :, ecols] = (_scaled_gelu(act) * jnp.concatenate(gates, axis=1)).astype(BF16)

    def retrieval():
        lists = []
        for p in range(2):
            st = lax.dot_general(sk_ref[h, p], q_scr[half, 2 * h + p], NT_DIMS, preferred_element_type=F32)
            yield from _top16_of_keys(st, lists)
            yield
        cand, expert = _pair_candidates(lists[0], lists[1])
        second_level = _TopK(cand, payload=expert)
        for _ in range(4):
            second_level.step(PEER_TOPK // 4)
            yield
        best, e_sel = second_level.result()
        ex = jnp.exp(best - best[0:1])
        slot0 = pl.multiple_of(h * PEER_TOPK, PEER_TOPK)
        sel_e[cur, half, pl.ds(slot0, PEER_TOPK), :] = e_sel
        sel_g[cur, half, pl.ds(slot0, PEER_TOPK), :] = ex / jnp.sum(ex, axis=0, keepdims=True)

    pieces = [functools.partial(f, c) for f in (value_piece, act_piece) for c in range(n_stages)]
    parts = retrieval()
    parts_per_piece = 3
    for piece in pieces:
        piece()
        for _ in range(parts_per_piece):
            next(parts, None)
    for _ in parts:
        pass

    @pl.when(e == n_steps - 1)
    def _():
        project_queries()
        if out_refs:
            first_ref, second_ref = out_refs

            @pl.when(i - 1 < split_blk)
            def _():
                first_ref[...] = o_ref[...]

            @pl.when(i - 1 >= split_blk)
            def _():
                second_ref[...] = o_ref[...]


def _peer(x, g, wq, sk, u_all, v_all, layer, t=512, eb=1024, split_rows=None):
    n, d = x.shape
    nslot = PEER_HEADS * PEER_TOPK
    n_tok_blk = n // t
    n_exp_blk = N_KEYS * N_KEYS // eb
    n_half = t // SEL_T
    assert n_exp_blk == n_half * PEER_HEADS
    last_tok = n_tok_blk - 1
    once = pl.Buffered(1)
    if split_rows is None:
        split_blk = None
        out_specs = pl.BlockSpec((t, d), lambda i, e: (jnp.maximum(i - 1, 0), 0))
        out_shape = jax.ShapeDtypeStruct((n, d), F32)
        acc = []
    else:
        split_blk = split_rows // t
        out_specs = [pl.BlockSpec((t, d), lambda i, e: (jnp.clip(i - 1, 0, split_blk - 1), 0)),
                     pl.BlockSpec((t, d), lambda i, e: (jnp.clip(i - 1 - split_blk, 0, last_tok - split_blk), 0))]
        out_shape = [jax.ShapeDtypeStruct((split_rows, d), F32), jax.ShapeDtypeStruct((n - split_rows, d), F32)]
        acc = [pltpu.VMEM((t, d), F32)]
    return pl.pallas_call(
        functools.partial(_peer_fused_kernel, split_blk=split_blk),
        grid=(n_tok_blk + 1, n_exp_blk + 1),
        in_specs=[pl.BlockSpec((t, d), lambda i, e: (jnp.minimum(i + e // n_exp_blk, last_tok), 0),
                               pipeline_mode=once),
                  pl.BlockSpec((t, d), lambda i, e: (jnp.maximum(i - 1, 0), 0)),
                  pl.BlockSpec((1, d), lambda i, e: (0, 0)),
                  pl.BlockSpec((d, 2 * PEER_HEADS * LANES), lambda i, e: (0, 0), pipeline_mode=once),
                  pl.BlockSpec((PEER_HEADS, 2, N_KEYS, LANES), lambda i, e: (0, 0, 0, 0), pipeline_mode=once),
                  pl.BlockSpec((None, eb, d), lambda i, e: (layer, jnp.minimum(e, n_exp_blk - 1), 0)),
                  pl.BlockSpec((None, eb, d), lambda i, e: (layer, jnp.maximum(e - 1, 0), 0))],
        out_specs=out_specs,
        out_shape=out_shape,
        scratch_shapes=[pltpu.VMEM((n_half, 2 * PEER_HEADS, SEL_T, LANES), BF16),
                        pltpu.VMEM((2, n_half, nslot, SEL_T), F32),
                        pltpu.VMEM((2, n_half, nslot, SEL_T), F32),
                        pltpu.VMEM((t, d), BF16),
                        pltpu.VMEM((t, nslot), F32),
                        pltpu.VMEM((t, nslot), F32),
                        pltpu.VMEM((t // LANES, nslot, LANES), F32),
                        pltpu.VMEM((t * G_PITCH, LANES), jnp.int32),
                        pltpu.VMEM((2, t, eb), BF16)] + acc,
        compiler_params=_cparams(2),
        name="peer",
    )(x, x, g.reshape(1, d), wq, sk, u_all, v_all)


def _rope_tables(pos):
    half = HEAD_DIM // 2
    inv_freq = ROPE_THETA ** (-jnp.arange(half, dtype=F32) / half)
    ang = pos.astype(F32)[:, None] * inv_freq[None, :]
    cos, sin = jnp.cos(ang), jnp.sin(ang)
    reps = LANES // HEAD_DIM
    cos_t = jnp.tile(jnp.concatenate([cos, cos], axis=1), (1, reps))
    sin_t = jnp.tile(jnp.concatenate([-sin, sin], axis=1), (1, reps))
    return cos_t, sin_t


def kernel(x_prompt, x_sample, cache_conv, cache_win_k, cache_win_v, state_mlstm_C, state_mlstm_n,
           state_mlstm_m, norm_mix, norm_ffn, ab_w_in, ab_conv_w, ab_q_gain, ab_k_gain, ab_sinks, ab_w_out,
           ml_w_in, ml_gate_bias, ml_out_gain, ml_w_out, peer_w_q, peer_sub_keys, peer_u, peer_v):
    n_batch, seq, d = x_prompt.shape
    n_seq, t_len, _ = x_sample.shape
    n_prompt = n_batch * seq
    assert d == D_MODEL and t_len == SUBLANES and norm_mix.shape[0] == 2

    xp = x_prompt.reshape(n_prompt, d)
    xs = x_sample.reshape(n_seq * t_len, d)

    cos_p, sin_p = _rope_tables(jnp.arange(seq, dtype=jnp.int32))
    cos_s, sin_s = _rope_tables(PAST_LEN + jnp.arange(t_len, dtype=jnp.int32))
    bt = 16
    cos_s, sin_s = jnp.tile(cos_s, (bt, 1)), jnp.tile(sin_s, (bt, 1))
    lane = jnp.arange(LANES)
    seg = (lane[:, None] // HEAD_DIM == lane[None, :] // HEAD_DIM).astype(BF16)
    reps = LANES // HEAD_DIM
    qg = jnp.tile(ab_q_gain[0], reps).reshape(1, LANES)
    kg = jnp.tile(ab_k_gain[0], reps).reshape(1, LANES)
    wo_ab = ab_w_out[0].astype(BF16)

    z = _norm_proj([xp, xs], norm_mix[0], ab_w_in[0].astype(BF16))
    y_p, k_p, v_p, c_p = _ab_prompt(z, xp, n_batch, seq, cos_p, sin_p, ab_conv_w[0], qg, kg, ab_sinks[0],
                                    seg, wo_ab)
    y_s, c_s, k_s, v_s = _ab_sample(z, xs, n_prompt, n_seq, t_len, cos_s, sin_s, ab_conv_w[0], qg, kg,
                                    ab_sinks[0], seg, wo_ab, cache_conv[0],
                                    cache_win_k[0].reshape(n_seq, WINDOW, LANES),
                                    cache_win_v[0].reshape(n_seq, WINDOW, LANES), bt=bt)
    x = jnp.concatenate([y_p.reshape(n_prompt, d), y_s], axis=0)
    u_all = (peer_u * RSQRT2).astype(BF16)
    v_all = peer_v.astype(BF16)
    x = _peer(x, norm_ffn[0], peer_w_q[0].astype(BF16), peer_sub_keys[0].astype(BF16), u_all, v_all, 0)

    n_gate = 2 * ML_HEADS
    w_in = jnp.pad(ml_w_in[0], ((0, 0), (0, ML_IN_PAD - ml_w_in.shape[2]))).astype(BF16)
    bias = jnp.pad(ml_gate_bias[0], (0, LANES - n_gate)).reshape(1, LANES)
    og = ml_out_gain[0].reshape(1, D_MODEL)
    idx = jnp.arange(ML_CHUNK)
    tril = (idx[None, :] <= idx[:, None]).astype(BF16)
    wo_ml = ml_w_out[0].astype(BF16)

    z = _norm_proj([x], norm_mix[1], w_in)
    y_p, cm_p, nm_p, mm_p = _mlstm_prompt(z, x, n_batch, seq, bias, og, tril, wo_ml)
    y_s, cm_s, nm_s, mm_s = _mlstm_sample(z, x, n_prompt, n_seq, t_len, bias, og, tril, wo_ml,
                                          state_mlstm_C[0], state_mlstm_n[0],
                                          state_mlstm_m[0].reshape(n_seq, 1, ML_HEADS))
    x = jnp.concatenate([y_p.reshape(n_prompt, d), y_s], axis=0)
    out_p, out_s = _peer(x, norm_ffn[1], peer_w_q[1].astype(BF16), peer_sub_keys[1].astype(BF16), u_all, v_all, 1,
                         split_rows=n_prompt)

    y_prompt = out_p.reshape(n_batch, seq, d)
    y_sample = out_s.reshape(n_seq, t_len, d)
    kv_shape_p = (1, n_batch, WINDOW, N_KV_HEADS, HEAD_DIM)
    kv_shape_s = (1, n_seq, WINDOW, N_KV_HEADS, HEAD_DIM)
    return (y_prompt, y_sample,
            c_p[:, SUBLANES - 2:, :][None], k_p.reshape(kv_shape_p), v_p.reshape(kv_shape_p),
            cm_p[None], nm_p[None], mm_p.reshape(1, n_batch, ML_HEADS),
            c_s[None], k_s.reshape(kv_shape_s), v_s.reshape(kv_shape_s),
            cm_s[None], nm_s[None], mm_s.reshape(1, n_seq, ML_HEADS))
```

```python
import functools

import jax
import jax.numpy as jnp
from jax import lax
from jax.experimental import pallas as pl
from jax.experimental.pallas import tpu as pltpu

F32 = jnp.float32
BF16 = jnp.bfloat16
EPS = 1e-6

D_MODEL = 1024
CONV_DIM = 512
N_Q_HEADS = 8
N_KV_HEADS = 2
HEAD_DIM = 64
WINDOW = 128
ROPE_THETA = 10000.0
AB_IN = 2304
ML_HEADS = 4
ML_QK = 128
ML_V = 256
ML_CHUNK = 128
ML_GATE_COL = 3072
ML_IN_PAD = ML_GATE_COL + 128
N_KEYS = 128
PEER_HEADS = 8
PEER_TOPK = 16
PAST_LEN = 16384

LANES = 128
SUBLANES = 8
G_PITCH = N_KEYS // 2 + SUBLANES
VMEM_LIMIT = 56 * 1024 * 1024

NT_DIMS = (((1,), (1,)), ((), ()))
TN_DIMS = (((0,), (0,)), ((), ()))


def _cparams(n_axes, vmem=VMEM_LIMIT):
    return pltpu.CompilerParams(dimension_semantics=("arbitrary",) * n_axes, vmem_limit_bytes=vmem)


def _rmsnorm(x, g):
    return x * lax.rsqrt(jnp.mean(x * x, axis=-1, keepdims=True) + EPS) * g


def _bf16_pieces(a, terms):
    pieces = []
    rem = a
    for _ in range(terms):
        piece = rem.astype(BF16)
        rem = rem - piece.astype(F32)
        pieces.append(piece)
    return pieces


def _split_dot(a, b_bf16, terms=2):
    return sum(jnp.dot(p, b_bf16, preferred_element_type=F32) for p in _bf16_pieces(a, terms))


def _norm_proj_kernel(*refs, starts):
    x_refs = refs[:len(starts)]
    g_ref, w_ref, o_ref = refs[len(starts):]
    i = pl.program_id(0)
    x = x_refs[0][...]
    for x_ref, start in zip(x_refs[1:], starts[1:]):
        x = jnp.where(i >= start, x_ref[...], x)
    r = _rmsnorm(x, g_ref[...])
    o_ref[...] = jnp.dot(r.astype(BF16), w_ref[...], preferred_element_type=F32)


def _norm_proj(x_parts, g, w_bf16, tm=512):
    d = x_parts[0].shape[1]
    nout = w_bf16.shape[1]
    blocks = [p.shape[0] // tm for p in x_parts]
    starts = tuple(sum(blocks[:k]) for k in range(len(blocks)))
    part_spec = lambda start, nblk: pl.BlockSpec((tm, d), lambda i: (jnp.clip(i - start, 0, nblk - 1), 0))
    return pl.pallas_call(
        functools.partial(_norm_proj_kernel, starts=starts),
        grid=(sum(blocks),),
        in_specs=[part_spec(s, nb) for s, nb in zip(starts, blocks)] + [
            pl.BlockSpec((1, d), lambda i: (0, 0)),
            pl.BlockSpec((d, nout), lambda i: (0, 0))],
        out_specs=pl.BlockSpec((tm, nout), lambda i: (i, 0)),
        out_shape=jax.ShapeDtypeStruct((sum(blocks) * tm, nout), F32),
        compiler_params=_cparams(1),
        name="norm_proj",
    )(*x_parts, g.reshape(1, d), w_bf16)


def _headnorm_rope(xc, gain, cos, sin, seg, hi_half):
    ss = _split_dot(xc * xc, seg)
    xn = xc * lax.rsqrt(ss * (1.0 / HEAD_DIM) + EPS) * gain
    partner = jnp.where(hi_half, pltpu.roll(xn, 32, 1), pltpu.roll(xn, 96, 1))
    return xn * cos + partner * sin


def _softmax_sink(s, mask, sink):
    s = jnp.where(mask, s, -1e30)
    m = jnp.maximum(jnp.max(s, axis=-1, keepdims=True), sink)
    p = jnp.exp(s - m)
    denom = jnp.sum(p, axis=-1, keepdims=True) + jnp.exp(sink - m)
    return (p / denom).astype(BF16)


def _ab_prompt_kernel(sink_ref, *refs, n_batch):
    z_refs, x_refs = refs[0:n_batch], refs[n_batch:2 * n_batch]
    (cos_ref, sin_ref, cw_ref, qg_ref, kg_ref, seg_ref, wo_ref,
     y_ref, kst_ref, vst_ref, cst_ref, pk_ref, pv_ref, pu_ref) = refs[2 * n_batch:]
    j = pl.program_id(0)

    @pl.when(j == 0)
    def _():
        pk_ref[...] = jnp.zeros_like(pk_ref)
        pv_ref[...] = jnp.zeros_like(pv_ref)
        pu_ref[...] = jnp.zeros_like(pu_ref)

    for b in range(n_batch):
        _ab_prompt_block(j, sink_ref, z_refs[b], x_refs[b], cos_ref, sin_ref, cw_ref, qg_ref, kg_ref, seg_ref,
                         wo_ref, y_ref.at[b], kst_ref.at[b], vst_ref.at[b], cst_ref.at[b],
                         pk_ref.at[b], pv_ref.at[b], pu_ref.at[b])


def _ab_prompt_block(j, sink_ref, z_ref, x_ref, cos_ref, sin_ref, cw_ref, qg_ref, kg_ref, seg_ref, wo_ref,
                     y_ref, kst_ref, vst_ref, cst_ref, pk_ref, pv_ref, pu_ref):
    blk = z_ref.shape[0]
    cos = cos_ref[...]
    sin = sin_ref[...]
    seg = seg_ref[...]
    hi_half = (lax.broadcasted_iota(jnp.int32, (blk, LANES), 1) & 32) != 0

    gate_b = z_ref[:, 0:CONV_DIM]
    u = z_ref[:, CONV_DIM:2 * CONV_DIM] * z_ref[:, 2 * CONV_DIM:3 * CONV_DIM]
    ng = blk // SUBLANES
    u3 = u.reshape(ng, SUBLANES, CONV_DIM)
    ext = jnp.concatenate([pu_ref[...][None], u3], axis=0)
    t8 = lax.broadcasted_iota(jnp.int32, (ng, SUBLANES, CONV_DIM), 1)
    r1 = pltpu.roll(ext, 1, 1)
    r2 = pltpu.roll(ext, 2, 1)
    um1 = jnp.where(t8 >= 1, r1[1:], r1[:-1])
    um2 = jnp.where(t8 >= 2, r2[1:], r2[:-1])
    cw = cw_ref[...]
    conv = cw[0:1][None] * um2 + cw[1:2][None] * um1 + cw[2:3][None] * u3
    yconv = gate_b * conv.reshape(blk, CONV_DIM)

    q0 = 3 * CONV_DIM
    k0 = q0 + N_Q_HEADS * HEAD_DIM
    v0 = k0 + N_KV_HEADS * HEAD_DIM
    qg = qg_ref[...]
    qr = [_headnorm_rope(z_ref[:, q0 + c * LANES:q0 + (c + 1) * LANES], qg, cos, sin, seg, hi_half)
          for c in range(N_Q_HEADS * HEAD_DIM // LANES)]
    kr = _headnorm_rope(z_ref[:, k0:k0 + LANES], kg_ref[...], cos, sin, seg, hi_half)
    v = z_ref[:, v0:v0 + LANES]
    pk = pk_ref[...]
    pv = pv_ref[...]

    row = lax.broadcasted_iota(jnp.int32, (blk, 2 * blk), 0)
    col = lax.broadcasted_iota(jnp.int32, (blk, 2 * blk), 1)
    row_prev = row + jnp.where(j == 0, 2 * blk, 0)
    mask = ((col < blk) & (col > row_prev)) | ((col >= blk) & (col - blk <= row))

    kks, vvs = [], []
    for g in range(N_KV_HEADS):
        sl = slice(g * HEAD_DIM, (g + 1) * HEAD_DIM)
        kks.append(jnp.concatenate([pk[:, sl], kr[:, sl]], axis=0).astype(BF16))
        vvs.append(jnp.concatenate([pv[:, sl], v[:, sl]], axis=0).astype(BF16))
    outs = []
    for h in range(N_Q_HEADS):
        g = h // (N_Q_HEADS // N_KV_HEADS)
        qh = qr[h // 2][:, (h % 2) * HEAD_DIM:(h % 2 + 1) * HEAD_DIM].astype(BF16)
        s = lax.dot_general(qh, kks[g], NT_DIMS, preferred_element_type=F32) * (HEAD_DIM ** -0.5)
        p = _softmax_sink(s, mask, sink_ref[h])
        outs.append(jnp.dot(p, vvs[g], preferred_element_type=F32))
    attn = jnp.concatenate(outs, axis=1)

    y = (jnp.dot(yconv.astype(BF16), wo_ref[0:CONV_DIM, :], preferred_element_type=F32)
         + jnp.dot(attn.astype(BF16), wo_ref[CONV_DIM:2 * CONV_DIM, :], preferred_element_type=F32))
    y_ref[...] = y + x_ref[...]

    pk_ref[...] = kr
    pv_ref[...] = v
    pu_ref[...] = u3[ng - 1]
    kst_ref[...] = kr
    vst_ref[...] = v
    cst_ref[...] = u3[ng - 1]


def _ab_prompt(z, x, n_batch, seq, cos, sin, cw, qg, kg, sinks, seg, wo):
    blk = WINDOW
    nb = seq // blk
    const = lambda j, s: (0, 0)
    whole = lambda j, s: (0, 0, 0)
    tok_specs = lambda width: [pl.BlockSpec((blk, width), functools.partial(lambda j, s, b: (b * nb + j, 0), b=b))
                               for b in range(n_batch)]
    grid_spec = pltpu.PrefetchScalarGridSpec(
        num_scalar_prefetch=1,
        grid=(nb,),
        in_specs=tok_specs(AB_IN) + tok_specs(D_MODEL) + [
            pl.BlockSpec((blk, LANES), lambda j, s: (j, 0)),
            pl.BlockSpec((blk, LANES), lambda j, s: (j, 0)),
            pl.BlockSpec((3, CONV_DIM), const),
            pl.BlockSpec((1, LANES), const),
            pl.BlockSpec((1, LANES), const),
            pl.BlockSpec((LANES, LANES), const),
            pl.BlockSpec((D_MODEL, D_MODEL), const)],
        out_specs=[pl.BlockSpec((n_batch, blk, D_MODEL), lambda j, s: (0, j, 0)),
                   pl.BlockSpec((n_batch, blk, LANES), whole),
                   pl.BlockSpec((n_batch, blk, LANES), whole),
                   pl.BlockSpec((n_batch, SUBLANES, CONV_DIM), whole)],
        scratch_shapes=[pltpu.VMEM((n_batch, blk, LANES), F32), pltpu.VMEM((n_batch, blk, LANES), F32),
                        pltpu.VMEM((n_batch, SUBLANES, CONV_DIM), F32)])
    return pl.pallas_call(
        functools.partial(_ab_prompt_kernel, n_batch=n_batch),
        grid_spec=grid_spec,
        out_shape=[jax.ShapeDtypeStruct((n_batch, seq, D_MODEL), F32),
                   jax.ShapeDtypeStruct((n_batch, blk, LANES), F32),
                   jax.ShapeDtypeStruct((n_batch, blk, LANES), F32),
                   jax.ShapeDtypeStruct((n_batch, SUBLANES, CONV_DIM), F32)],
        compiler_params=_cparams(1),
        name="ab_prompt",
    )(sinks, *([z] * n_batch), *([x] * n_batch), cos, sin, cw, qg, kg, seg, wo)


def _ab_sample_kernel(sink_ref, z_ref, x_ref, cos_ref, sin_ref, cw_ref, qg_ref, kg_ref, seg_ref, wo_ref,
                      cc_ref, ck_ref, cv_ref, y_ref, cs_ref, ks_ref, vs_ref):
    rows = z_ref.shape[0]
    t_len = SUBLANES
    bt = rows // t_len
    cos = cos_ref[...]
    sin = sin_ref[...]
    seg = seg_ref[...]
    hi_half = (lax.broadcasted_iota(jnp.int32, (rows, LANES), 1) & 32) != 0

    gate_b = z_ref[:, 0:CONV_DIM]
    u = z_ref[:, CONV_DIM:2 * CONV_DIM] * z_ref[:, 2 * CONV_DIM:3 * CONV_DIM]
    u3 = u.reshape(bt, t_len, CONV_DIM)
    cc = cc_ref[...]
    c0 = cc[:, 0:1, :]
    c1 = cc[:, 1:2, :]
    t8 = lax.broadcasted_iota(jnp.int32, (bt, t_len, CONV_DIM), 1)
    r1 = pltpu.roll(u3, 1, 1)
    r2 = pltpu.roll(u3, 2, 1)
    um1 = jnp.where(t8 >= 1, r1, c1)
    um2 = jnp.where(t8 >= 2, r2, jnp.where(t8 == 1, c1, c0))
    cw = cw_ref[...]
    conv = cw[0:1][None] * um2 + cw[1:2][None] * um1 + cw[2:3][None] * u3
    yconv = gate_b * conv.reshape(rows, CONV_DIM)
    cs_ref[...] = r2[:, 0:2, :]

    q0 = 3 * CONV_DIM
    k0 = q0 + N_Q_HEADS * HEAD_DIM
    v0 = k0 + N_KV_HEADS * HEAD_DIM
    qg = qg_ref[...]
    qr = [_headnorm_rope(z_ref[:, q0 + c * LANES:q0 + (c + 1) * LANES], qg, cos, sin, seg, hi_half)
          for c in range(N_Q_HEADS * HEAD_DIM // LANES)]
    kr = _headnorm_rope(z_ref[:, k0:k0 + LANES], kg_ref[...], cos, sin, seg, hi_half)
    v = z_ref[:, v0:v0 + LANES]

    group = N_Q_HEADS // N_KV_HEADS
    nq = group * t_len
    nk = 2 * WINDOW
    qrow = lax.broadcasted_iota(jnp.int32, (nq, nk), 0)
    t_q = qrow & (t_len - 1)
    col = lax.broadcasted_iota(jnp.int32, (nq, nk), 1)
    mask = (((col < WINDOW) & (col > t_q)) | ((col >= WINDOW) & (col - WINDOW <= t_q)))[None]
    hrow = lax.broadcasted_iota(jnp.int32, (nq, 1), 0) // t_len
    pad = jnp.zeros((bt, nk - WINDOW - t_len, HEAD_DIM), F32)

    outs = [None] * N_Q_HEADS
    for g in range(N_KV_HEADS):
        sl = slice(g * HEAD_DIM, (g + 1) * HEAD_DIM)
        qs = jnp.concatenate(
            [qr[h // 2][:, (h % 2) * HEAD_DIM:(h % 2 + 1) * HEAD_DIM].reshape(bt, t_len, HEAD_DIM)
             for h in range(g * group, (g + 1) * group)], axis=1)
        kk = jnp.concatenate([ck_ref[:, :, sl], kr[:, sl].reshape(bt, t_len, HEAD_DIM), pad], axis=1)
        vv = jnp.concatenate([cv_ref[:, :, sl], v[:, sl].reshape(bt, t_len, HEAD_DIM), pad], axis=1)
        s = jnp.einsum('bqd,bkd->bqk', qs.astype(BF16), kk.astype(BF16),
                       preferred_element_type=F32) * (HEAD_DIM ** -0.5)
        sink = jnp.zeros((nq, 1), F32)
        for hh in range(group):
            sink = jnp.where(hrow == hh, sink_ref[g * group + hh], sink)
        p = _softmax_sink(s, mask, sink[None])
        o = jnp.einsum('bqk,bkd->bqd', p, vv.astype(BF16), preferred_element_type=F32)
        for hh in range(group):
            outs[g * group + hh] = o[:, hh * t_len:(hh + 1) * t_len, :].reshape(rows, HEAD_DIM)
    attn = jnp.concatenate(outs, axis=1)

    y = (jnp.dot(yconv.astype(BF16), wo_ref[0:CONV_DIM, :], preferred_element_type=F32)
         + jnp.dot(attn.astype(BF16), wo_ref[CONV_DIM:2 * CONV_DIM, :], preferred_element_type=F32))
    y_ref[...] = y + x_ref[...]

    keep = WINDOW - t_len
    ks_ref[:, 0:keep, :] = ck_ref[:, t_len:WINDOW, :]
    ks_ref[:, keep:WINDOW, :] = kr.reshape(bt, t_len, LANES)
    vs_ref[:, 0:keep, :] = cv_ref[:, t_len:WINDOW, :]
    vs_ref[:, keep:WINDOW, :] = v.reshape(bt, t_len, LANES)


def _ab_sample(z, x, row0, n_seq, t_len, cos, sin, cw, qg, kg, sinks, seg, wo, cc, ck, cv, bt=16):
    rows = bt * t_len
    blk0 = row0 // rows
    tok = lambda i, s: (blk0 + i, 0)
    const = lambda i, s: (0, 0)
    seq3 = lambda i, s: (i, 0, 0)
    grid_spec = pltpu.PrefetchScalarGridSpec(
        num_scalar_prefetch=1,
        grid=(n_seq // bt,),
        in_specs=[pl.BlockSpec((rows, AB_IN), tok),
                  pl.BlockSpec((rows, D_MODEL), lambda i, s: (i, 0)),
                  pl.BlockSpec((rows, LANES), const),
                  pl.BlockSpec((rows, LANES), const),
                  pl.BlockSpec((3, CONV_DIM), const),
                  pl.BlockSpec((1, LANES), const),
                  pl.BlockSpec((1, LANES), const),
                  pl.BlockSpec((LANES, LANES), const),
                  pl.BlockSpec((D_MODEL, D_MODEL), const),
                  pl.BlockSpec((bt, 2, CONV_DIM), seq3),
                  pl.BlockSpec((bt, WINDOW, LANES), seq3),
                  pl.BlockSpec((bt, WINDOW, LANES), seq3)],
        out_specs=[pl.BlockSpec((rows, D_MODEL), lambda i, s: (i, 0)),
                   pl.BlockSpec((bt, 2, CONV_DIM), seq3),
                   pl.BlockSpec((bt, WINDOW, LANES), seq3),
                   pl.BlockSpec((bt, WINDOW, LANES), seq3)])
    return pl.pallas_call(
        _ab_sample_kernel,
        grid_spec=grid_spec,
        out_shape=[jax.ShapeDtypeStruct((n_seq * t_len, D_MODEL), F32),
                   jax.ShapeDtypeStruct((n_seq, 2, CONV_DIM), F32),
                   jax.ShapeDtypeStruct((n_seq, WINDOW, LANES), F32),
                   jax.ShapeDtypeStruct((n_seq, WINDOW, LANES), F32)],
        compiler_params=_cparams(1),
        name="ab_sample",
    )(sinks, z, x, cos, sin, cw, qg, kg, seg, wo, cc, ck, cv)


def _log_sigmoid(x):
    return jnp.minimum(x, 0.0) - jnp.log(1.0 + jnp.exp(-jnp.abs(x)))


def _mlstm_chunk(z, bias, og, tril, c_src, n_src, m_src, c_dst, n_dst, m_dst, n_real, rows):
    L = z.shape[0]
    gates = z[:, ML_GATE_COL:ML_GATE_COL + LANES] + bias
    if n_real < L:
        live = lax.broadcasted_iota(jnp.int32, (L, LANES), 0) < n_real
        li_all = jnp.where(live, gates, -1e30)
        lf_all = jnp.where(live, _log_sigmoid(gates), 0.0)
    else:
        li_all = gates
        lf_all = _log_sigmoid(gates)
    lf_pieces = _bf16_pieces(lf_all, 3)
    f_col_all = sum(jnp.dot(tril[0:rows], p, preferred_element_type=F32) for p in lf_pieces)
    f_row_all = sum(lax.dot_general(p, tril, (((0,), (1,)), ((), ())), preferred_element_type=F32)
                    for p in lf_pieces)
    li_t = li_all.T
    rr = lax.broadcasted_iota(jnp.int32, (rows, L), 0)
    cc = lax.broadcasted_iota(jnp.int32, (rows, L), 1)
    causal = cc <= rr

    outs, m_new_all = [], []
    for h in range(ML_HEADS):
        f_col = f_col_all[:, ML_HEADS + h:ML_HEADS + h + 1]
        f_row = f_row_all[ML_HEADS + h:ML_HEADS + h + 1, :]
        li_row = li_t[h:h + 1, :]
        li_col = li_all[0:rows, h:h + 1]
        m0 = m_src[0:1, h:h + 1]
        c0 = c_src[h]
        n0 = n_src[h:h + 1, :]
        qh = z[0:rows, h * ML_QK:(h + 1) * ML_QK]
        kh = z[:, ML_HEADS * ML_QK + h * ML_QK:ML_HEADS * ML_QK + (h + 1) * ML_QK] * (ML_QK ** -0.5)
        v_off = 2 * ML_HEADS * ML_QK
        vh = z[:, v_off + h * ML_V:v_off + (h + 1) * ML_V]
        o_off = v_off + ML_HEADS * ML_V
        oh = z[0:rows, o_off + h * ML_V:o_off + (h + 1) * ML_V]
        qb = qh.astype(BF16)
        vb = vh.astype(BF16)

        dmat = jnp.where(causal, f_col - f_row + li_row, -jnp.inf)
        gcar = f_col + m0
        m_t = jnp.maximum(jnp.max(dmat, axis=-1, keepdims=True), gcar)
        w = jnp.exp(dmat - m_t)
        s = lax.dot_general(qb, kh.astype(BF16), NT_DIMS, preferred_element_type=F32) * w
        carry = jnp.exp(gcar - m_t)
        num = (jnp.dot(s.astype(BF16), vb, preferred_element_type=F32)
               + jnp.dot(qb, c0.astype(BF16), preferred_element_type=F32) * carry)
        den = jnp.sum(s, axis=-1, keepdims=True) + carry * jnp.sum(qh * n0, axis=-1, keepdims=True)
        hout = num / jnp.maximum(jnp.abs(den), jnp.exp(-m_t))

        f_last = f_col[n_real - 1:n_real, :]
        w_end = f_last - f_col + li_col
        m_new = jnp.maximum(f_last + m0, jnp.max(w_end, axis=0, keepdims=True))
        a_end = jnp.exp(w_end - m_new)
        scale = jnp.exp(f_last + m0 - m_new)
        ka = kh[0:rows] * a_end
        c_dst[h] = scale * c0 + lax.dot_general(ka.astype(BF16), vb[0:rows], TN_DIMS,
                                                preferred_element_type=F32)
        n_dst[h:h + 1, :] = scale * n0 + jnp.sum(ka, axis=0, keepdims=True)
        m_new_all.append(m_new)

        hn = _rmsnorm(hout, og[:, h * ML_V:(h + 1) * ML_V])
        outs.append(jax.nn.sigmoid(oh) * hn)
    m_dst[...] = jnp.concatenate(m_new_all, axis=1)
    return jnp.concatenate(outs, axis=1)


def _mlstm_prompt_kernel(*refs, n_batch):
    z_refs, x_refs = refs[0:n_batch], refs[n_batch:2 * n_batch]
    bias_ref, og_ref, tril_ref, wo_ref, y_ref, c_ref, n_ref, m_ref = refs[2 * n_batch:]

    @pl.when(pl.program_id(0) == 0)
    def _():
        c_ref[...] = jnp.zeros_like(c_ref)
        n_ref[...] = jnp.zeros_like(n_ref)
        m_ref[...] = jnp.zeros_like(m_ref)

    for b in range(n_batch):
        state = (c_ref.at[b], n_ref.at[b], m_ref.at[b])
        out = _mlstm_chunk(z_refs[b][...], bias_ref[...], og_ref[...], tril_ref[...], *state, *state,
                           ML_CHUNK, ML_CHUNK)
        y_ref[b] = jnp.dot(out.astype(BF16), wo_ref[...], preferred_element_type=F32) + x_refs[b][...]


def _mlstm_sample_kernel(z_ref, x_ref, bias_ref, og_ref, tril_ref, wo_ref, c0_ref, n0_ref, m0_ref,
                         y_ref, c_ref, n_ref, m_ref, *, t_len):
    n_here = z_ref.shape[0] // t_len
    q_rows = 2 * SUBLANES
    outs = []
    for s in range(n_here):
        zpad = jnp.concatenate([z_ref[s * t_len:(s + 1) * t_len, :],
                                jnp.zeros((ML_CHUNK - t_len, ML_IN_PAD), F32)], axis=0)
        out = _mlstm_chunk(zpad, bias_ref[...], og_ref[...], tril_ref[...],
                           c0_ref.at[s], n0_ref.at[s], m0_ref.at[s], c_ref.at[s], n_ref.at[s], m_ref.at[s],
                           t_len, q_rows)
        outs.append(out[0:t_len])
    out_all = jnp.concatenate(outs, axis=0).astype(BF16)
    y_ref[...] = jnp.dot(out_all, wo_ref[...], preferred_element_type=F32) + x_ref[...]


def _mlstm_weight_specs(const):
    return [pl.BlockSpec((1, LANES), const),
            pl.BlockSpec((1, D_MODEL), const),
            pl.BlockSpec((ML_CHUNK, ML_CHUNK), const),
            pl.BlockSpec((D_MODEL, D_MODEL), const)]


def _mlstm_prompt(z, x, n_batch, seq, bias, og, tril, wo):
    nc = seq // ML_CHUNK
    const = lambda j: (0, 0)
    tok_specs = lambda width: [pl.BlockSpec((ML_CHUNK, width), functools.partial(lambda j, b: (b * nc + j, 0), b=b))
                               for b in range(n_batch)]
    return pl.pallas_call(
        functools.partial(_mlstm_prompt_kernel, n_batch=n_batch),
        grid=(nc,),
        in_specs=tok_specs(ML_IN_PAD) + tok_specs(D_MODEL) + _mlstm_weight_specs(const),
        out_specs=[pl.BlockSpec((n_batch, ML_CHUNK, D_MODEL), lambda j: (0, j, 0)),
                   pl.BlockSpec((n_batch, ML_HEADS, ML_QK, ML_V), lambda j: (0, 0, 0, 0)),
                   pl.BlockSpec((n_batch, ML_HEADS, ML_QK), lambda j: (0, 0, 0)),
                   pl.BlockSpec((n_batch, 1, ML_HEADS), lambda j: (0, 0, 0))],
        out_shape=[jax.ShapeDtypeStruct((n_batch, seq, D_MODEL), F32),
                   jax.ShapeDtypeStruct((n_batch, ML_HEADS, ML_QK, ML_V), F32),
                   jax.ShapeDtypeStruct((n_batch, ML_HEADS, ML_QK), F32),
                   jax.ShapeDtypeStruct((n_batch, 1, ML_HEADS), F32)],
        compiler_params=_cparams(1),
        name="mlstm_prompt",
    )(*([z] * n_batch), *([x] * n_batch), bias, og, tril, wo)


def _mlstm_sample(z, x, row0, n_seq, t_len, bias, og, tril, wo, c0, n0, m0, seqs_per_step=8):
    rows = seqs_per_step * t_len
    blk0 = row0 // rows
    tok = lambda i: (blk0 + i, 0)
    const = lambda i: (0, 0)
    st4 = lambda i: (i, 0, 0, 0)
    st3 = lambda i: (i, 0, 0)
    state_specs = [pl.BlockSpec((seqs_per_step, ML_HEADS, ML_QK, ML_V), st4),
                   pl.BlockSpec((seqs_per_step, ML_HEADS, ML_QK), st3),
                   pl.BlockSpec((seqs_per_step, 1, ML_HEADS), st3)]
    return pl.pallas_call(
        functools.partial(_mlstm_sample_kernel, t_len=t_len),
        grid=(n_seq // seqs_per_step,),
        in_specs=[pl.BlockSpec((rows, ML_IN_PAD), tok),
                  pl.BlockSpec((rows, D_MODEL), tok)] + _mlstm_weight_specs(const) + state_specs,
        out_specs=[pl.BlockSpec((rows, D_MODEL), lambda i: (i, 0))] + state_specs,
        out_shape=[jax.ShapeDtypeStruct((n_seq * t_len, D_MODEL), F32),
                   jax.ShapeDtypeStruct((n_seq, ML_HEADS, ML_QK, ML_V), F32),
                   jax.ShapeDtypeStruct((n_seq, ML_HEADS, ML_QK), F32),
                   jax.ShapeDtypeStruct((n_seq, 1, ML_HEADS), F32)],
        compiler_params=_cparams(1),
        name="mlstm_sample",
    )(z, x, bias, og, tril, wo, c0, n0, m0)


RSQRT2 = 0.7071067811865476


def _scaled_gelu(xs):
    return xs * (1.0 + lax.erf(xs))


SEL_T = 256


def _batcher_pairs(n):
    pairs = []
    p = 1
    while p < n:
        k = p
        while k >= 1:
            for j in range(k % p, n - k, 2 * k):
                for i in range(min(k, n - j - k)):
                    if (i + j) // (2 * p) == (i + j + k) // (2 * p):
                        pairs.append((i + j, i + j + k))
            k //= 2
        p *= 2
    return pairs


def _bitonic_merge_pairs(n):
    pairs = []
    s = n // 2
    while s >= 1:
        pairs += [(i, i + s) for i in range(n) if not i & s]
        s //= 2
    return pairs


def _top16_of_keys(s, out):
    n_vreg = N_KEYS // SUBLANES
    assert n_vreg == PEER_TOPK and N_KEYS == 128
    rounded = (s + 0.0).astype(BF16).astype(F32)
    bits = lax.bitcast_convert_type(rounded, jnp.int32)
    flip = (lax.shift_right_arithmetic(bits, 31) & jnp.int32(0x7FFF0000)) | jnp.int32(-2 ** 31)
    row = lax.broadcasted_iota(jnp.int32, s.shape, 0)
    keys = lax.bitcast_convert_type(
        lax.shift_right_logical(bits ^ flip, 9) | ((N_KEYS - 1 - row) | 0x4B000000), F32)

    x = [keys[v * SUBLANES:(v + 1) * SUBLANES] for v in range(n_vreg)]

    def exchange(i, j):
        x[i], x[j] = jnp.maximum(x[i], x[j]), jnp.minimum(x[i], x[j])

    for i, j in _batcher_pairs(n_vreg):
        exchange(i, j)
    yield
    for shift in (SUBLANES // 2, SUBLANES // 4, SUBLANES // 8):
        other = [pltpu.roll(a, shift, 0) for a in x]
        x = [jnp.maximum(x[i], other[n_vreg - 1 - i]) for i in range(n_vreg)]
        for i, j in _bitonic_merge_pairs(n_vreg):
            exchange(i, j)
        yield
    top = lax.bitcast_convert_type(jnp.concatenate([a[0:1] for a in x], axis=0), jnp.int32) & 0x7FFFFF
    rows = (N_KEYS - 1 - (top & (N_KEYS - 1))).astype(F32)
    code = lax.shift_right_logical(top, 7)
    b16 = jnp.where(code >= 0x8000, code ^ 0x8000, code ^ 0xFFFF)
    out.append((lax.bitcast_convert_type(b16 << 16, F32), rows))


class _TopK:
    PAYLOAD_RANGE = N_KEYS * N_KEYS

    def __init__(self, s, payload):
        assert s.shape[0] * self.PAYLOAD_RANGE < 2 ** 24
        self.s = s
        rows = lax.broadcasted_iota(jnp.int32, s.shape, 0).astype(F32)
        self.code = rows * self.PAYLOAD_RANGE + payload
        self.vals, self.codes = [], []

    def step(self, n):
        bound = float(self.s.shape[0] * self.PAYLOAD_RANGE)
        for _ in range(n):
            m = jnp.max(self.s, axis=0, keepdims=True)
            first = jnp.min(jnp.where(self.s == m, self.code, bound), axis=0, keepdims=True)
            self.vals.append(m)
            self.codes.append(first)
            self.s = jnp.where(self.code == first, -jnp.inf, self.s)

    def result(self):
        codes = jnp.concatenate(self.codes, axis=0)
        payload = codes - jnp.floor(codes * (1.0 / self.PAYLOAD_RANGE)) * self.PAYLOAD_RANGE
        return jnp.concatenate(self.vals, axis=0), payload


_PAIR_GROUPS = (
    ((0, 1, 0, 8),),
    ((0, 2, 0, 5), (5, 3, 0, 3)),
    ((0, 3, 3, 1), (1, 4, 0, 3), (4, 5, 0, 2), (6, 6, 0, 2)),
    ((0, 7, 0, 2), (2, 8, 0, -6)),
    ((0, 14, 0, -2),),
)


def _pair_candidates(first, second):
    h8 = SUBLANES
    (v1, i1), (v2, i2) = [tuple(a.astype(F32) for a in lst) for lst in (first, second)]
    row = lax.broadcasted_iota(jnp.int32, (h8, v1.shape[1]), 0)

    def build(a, b, combine, filler):
        groups = [combine(a[0:1], b)]
        for segments in _PAIR_GROUPS:
            group = jnp.full((h8, a.shape[1]), filler, F32)
            for r0, k1, k2, count in segments:
                if count > 0:
                    seg = combine(a[k1:k1 + 1], pltpu.roll(b[0:h8], (r0 - k2) % h8, 0))
                else:
                    seg = combine(pltpu.roll(a[h8:2 * h8], (r0 - (k1 - h8)) % h8, 0), b[0:1])
                group = jnp.where((row >= r0) & (row < r0 + abs(count)), seg, group)
            groups.append(group)
        return jnp.concatenate(groups, axis=0)

    return (build(v1, v2, lambda a, b: a + b, -jnp.inf),
            build(i1, i2, lambda a, b: a * N_KEYS + b, 0.0))


def _peer_fused_kernel(xs_ref, xm_ref, g_ref, wq_ref, sk_ref, u_ref, v_ref, *rest, split_blk):
    if split_blk is None:
        o_ref, q_scr, sel_e, sel_g, xn_scr, a_scr, bgt_scr, bg_scr, gmat, coef_scr = rest
        out_refs = ()
    else:
        *out_refs, q_scr, sel_e, sel_g, xn_scr, a_scr, bgt_scr, bg_scr, gmat, coef_scr, o_ref = rest
    i = pl.program_id(0)
    e = pl.program_id(1)
    n_steps = pl.num_programs(1)
    t = xm_ref.shape[0]
    n_half = t // SEL_T
    half_keys = N_KEYS // 2
    cur = i % 2

    def project_queries():
        xn = _rmsnorm(xs_ref[...], g_ref[...]).astype(BF16)
        q = jnp.dot(xn, wq_ref[...], preferred_element_type=F32).astype(BF16)
        for hf in range(n_half):
            for c in range(2 * PEER_HEADS):
                q_scr[hf, c] = q[hf * SEL_T:(hf + 1) * SEL_T, c * LANES:(c + 1) * LANES]

    @pl.when(e == 0)
    def _():
        @pl.when(i == 0)
        def _():
            project_queries()
            sel_e[...] = jnp.zeros_like(sel_e)
            sel_g[...] = jnp.zeros_like(sel_g)

        x = xm_ref[...]
        xn_scr[...] = _rmsnorm(x, g_ref[...]).astype(BF16)
        o_ref[...] = x
        coef_scr[0] = jnp.zeros(coef_scr.shape[1:], BF16)
        prev = 1 - cur
        lane_groups = SEL_T // LANES
        for hf in range(n_half):
            ef = sel_e[prev, hf]
            af = jnp.floor(ef * (1.0 / N_KEYS))
            bf = ef - af * N_KEYS
            gf = sel_g[prev, hf]
            a_scr[hf * SEL_T:(hf + 1) * SEL_T, :] = af.T
            bgf = bf + RSQRT2 * gf
            bgt_scr[hf * SEL_T:(hf + 1) * SEL_T, :] = bgf.T
            for lg in range(lane_groups):
                bg_scr[hf * lane_groups + lg] = bgf[:, lg * LANES:(lg + 1) * LANES]
        r = lax.broadcasted_iota(jnp.int32, (N_KEYS, LANES), 0)
        packed_shape = (N_KEYS // (2 * SUBLANES), 2 * SUBLANES, LANES)
        a_of_row = jnp.where(r < half_keys, 2 * r, 2 * (r - half_keys) + 1).astype(F32).astype(BF16)
        a_of_row = a_of_row.reshape(packed_shape)
        one = jnp.ones(packed_shape, BF16)
        zero = jnp.zeros(packed_shape, BF16)
        b_of_lane = lax.broadcasted_iota(jnp.int32, (N_KEYS, LANES), 1).astype(F32)
        b_of_row = r.astype(F32).astype(BF16).reshape(packed_shape)
        per_body = 32
        bodies_per_group = LANES // per_body
        row_major_every = 3

        def body(it, carry):
            grp = lax.shift_right_logical(it, bodies_per_group.bit_length() - 1)
            sub = it & (bodies_per_group - 1)
            shift = (LANES - sub * per_body) & (LANES - 1)
            bg = pltpu.roll(bg_scr[grp], shift, 1)
            tok0 = pl.multiple_of(it * per_body, per_body)
            arows = a_scr[pl.ds(tok0, per_body), :]
            bgrows = bgt_scr[pl.ds(tok0, per_body), :]
            for k in range(per_body):
                arow = jnp.broadcast_to(arows[k:k + 1], (2 * SUBLANES, LANES)).astype(BF16)[None]
                pa = jnp.where(a_of_row == arow, one, zero).reshape(N_KEYS, LANES)
                if k % row_major_every == 0:
                    bgrow = jnp.broadcast_to(bgrows[k:k + 1], (2 * SUBLANES, LANES))
                    brow = jnp.floor(bgrow)
                    grow = (bgrow - brow).astype(BF16)[None]
                    qb = jnp.where(b_of_row == brow.astype(BF16)[None], grow, zero).reshape(N_KEYS, LANES)
                    tile = lax.dot_general(pa, qb, NT_DIMS, preferred_element_type=F32)
                else:
                    bgcol = jnp.broadcast_to(bg[:, k:k + 1], (N_KEYS, LANES))
                    bcol = jnp.floor(bgcol)
                    gcol = bgcol - bcol
                    qbt = jnp.where(bcol == b_of_lane, gcol, 0.0).astype(BF16)
                    tile = jnp.dot(pa, qbt, preferred_element_type=F32)
                row0 = pl.multiple_of((tok0 + k) * G_PITCH, SUBLANES)
                gmat[pl.ds(row0, half_keys), :] = pltpu.pack_elementwise(
                    [tile[0:half_keys], tile[half_keys:N_KEYS]], packed_dtype=BF16)
            return carry

        lax.fori_loop(0, t // per_body, body, 0)

    unit = jnp.minimum(e, n_half * PEER_HEADS - 1)
    half = lax.shift_right_logical(unit, PEER_HEADS.bit_length() - 1)
    h = unit & (PEER_HEADS - 1)
    rd = e & 1
    blk = jnp.minimum(e, n_steps - 2)
    eb = u_ref.shape[0]
    n_stages = 4
    exp_w = eb // n_stages
    out_w = v_ref.shape[1] // n_stages
    words_per_stage = exp_w // (2 * N_KEYS)

    def value_piece(c):
        ocols = slice(c * out_w, (c + 1) * out_w)
        o_ref[:, ocols] += jnp.dot(coef_scr[rd], v_ref[:, ocols], preferred_element_type=F32)

    def act_piece(c):
        ecols = slice(c * exp_w, (c + 1) * exp_w)
        act = lax.dot_general(xn_scr[...], u_ref[ecols, :], NT_DIMS, preferred_element_type=F32)
        gates = []
        for w in range(words_per_stage):
            word = gmat[pl.ds((blk * n_stages + c) * words_per_stage + w, t, stride=G_PITCH), :]
            gates.append(lax.bitcast_convert_type(word << 16, F32))
            gates.append(lax.bitcast_convert_type(word & jnp.int32(-65536), F32))
        coef_scr[1 - rd, :, ecols] = (_scaled_gelu(act) * jnp.concatenate(gates, axis=1)).astype(BF16)

    def retrieval():
        lists = []
        for p in range(2):
            st = lax.dot_general(sk_ref[h, p], q_scr[half, 2 * h + p], NT_DIMS, preferred_element_type=F32)
            yield from _top16_of_keys(st, lists)
            yield
        cand, expert = _pair_candidates(lists[0], lists[1])
        second_level = _TopK(cand, payload=expert)
        for _ in range(4):
            second_level.step(PEER_TOPK // 4)
            yield
        best, e_sel = second_level.result()
        ex = jnp.exp(best - best[0:1])
        slot0 = pl.multiple_of(h * PEER_TOPK, PEER_TOPK)
        sel_e[cur, half, pl.ds(slot0, PEER_TOPK), :] = e_sel
        sel_g[cur, half, pl.ds(slot0, PEER_TOPK), :] = ex / jnp.sum(ex, axis=0, keepdims=True)

    pieces = [functools.partial(f, c) for f in (value_piece, act_piece) for c in range(n_stages)]
    parts = retrieval()
    parts_per_piece = 3
    for piece in pieces:
        piece()
        for _ in range(parts_per_piece):
            next(parts, None)
    for _ in parts:
        pass

    @pl.when(e == n_steps - 1)
    def _():
        project_queries()
        if out_refs:
            first_ref, second_ref = out_refs

            @pl.when(i - 1 < split_blk)
            def _():
                first_ref[...] = o_ref[...]

            @pl.when(i - 1 >= split_blk)
            def _():
                second_ref[...] = o_ref[...]


def _peer(x, g, wq, sk, u_all, v_all, layer, t=512, eb=1024, split_rows=None):
    n, d = x.shape
    nslot = PEER_HEADS * PEER_TOPK
    n_tok_blk = n // t
    n_exp_blk = N_KEYS * N_KEYS // eb
    n_half = t // SEL_T
    assert n_exp_blk == n_half * PEER_HEADS
    last_tok = n_tok_blk - 1
    once = pl.Buffered(1)
    if split_rows is None:
        split_blk = None
        out_specs = pl.BlockSpec((t, d), lambda i, e: (jnp.maximum(i - 1, 0), 0))
        out_shape = jax.ShapeDtypeStruct((n, d), F32)
        acc = []
    else:
        split_blk = split_rows // t
        out_specs = [pl.BlockSpec((t, d), lambda i, e: (jnp.clip(i - 1, 0, split_blk - 1), 0)),
                     pl.BlockSpec((t, d), lambda i, e: (jnp.clip(i - 1 - split_blk, 0, last_tok - split_blk), 0))]
        out_shape = [jax.ShapeDtypeStruct((split_rows, d), F32), jax.ShapeDtypeStruct((n - split_rows, d), F32)]
        acc = [pltpu.VMEM((t, d), F32)]
    return pl.pallas_call(
        functools.partial(_peer_fused_kernel, split_blk=split_blk),
        grid=(n_tok_blk + 1, n_exp_blk + 1),
        in_specs=[pl.BlockSpec((t, d), lambda i, e: (jnp.minimum(i + e // n_exp_blk, last_tok), 0),
                               pipeline_mode=once),
                  pl.BlockSpec((t, d), lambda i, e: (jnp.maximum(i - 1, 0), 0)),
                  pl.BlockSpec((1, d), lambda i, e: (0, 0)),
                  pl.BlockSpec((d, 2 * PEER_HEADS * LANES), lambda i, e: (0, 0), pipeline_mode=once),
                  pl.BlockSpec((PEER_HEADS, 2, N_KEYS, LANES), lambda i, e: (0, 0, 0, 0), pipeline_mode=once),
                  pl.BlockSpec((None, eb, d), lambda i, e: (layer, jnp.minimum(e, n_exp_blk - 1), 0)),
                  pl.BlockSpec((None, eb, d), lambda i, e: (layer, jnp.maximum(e - 1, 0), 0))],
        out_specs=out_specs,
        out_shape=out_shape,
        scratch_shapes=[pltpu.VMEM((n_half, 2 * PEER_HEADS, SEL_T, LANES), BF16),
                        pltpu.VMEM((2, n_half, nslot, SEL_T), F32),
                        pltpu.VMEM((2, n_half, nslot, SEL_T), F32),
                        pltpu.VMEM((t, d), BF16),
                        pltpu.VMEM((t, nslot), F32),
                        pltpu.VMEM((t, nslot), F32),
                        pltpu.VMEM((t // LANES, nslot, LANES), F32),
                        pltpu.VMEM((t * G_PITCH, LANES), jnp.int32),
                        pltpu.VMEM((2, t, eb), BF16)] + acc,
        compiler_params=_cparams(2),
        name="peer",
    )(x, x, g.reshape(1, d), wq, sk, u_all, v_all)


def _rope_tables(pos):
    half = HEAD_DIM // 2
    inv_freq = ROPE_THETA ** (-jnp.arange(half, dtype=F32) / half)
    ang = pos.astype(F32)[:, None] * inv_freq[None, :]
    cos, sin = jnp.cos(ang), jnp.sin(ang)
    reps = LANES // HEAD_DIM
    cos_t = jnp.tile(jnp.concatenate([cos, cos], axis=1), (1, reps))
    sin_t = jnp.tile(jnp.concatenate([-sin, sin], axis=1), (1, reps))
    return cos_t, sin_t


def kernel(x_prompt, x_sample, cache_conv, cache_win_k, cache_win_v, state_mlstm_C, state_mlstm_n,
           state_mlstm_m, norm_mix, norm_ffn, ab_w_in, ab_conv_w, ab_q_gain, ab_k_gain, ab_sinks, ab_w_out,
           ml_w_in, ml_gate_bias, ml_out_gain, ml_w_out, peer_w_q, peer_sub_keys, peer_u, peer_v):
    n_batch, seq, d = x_prompt.shape
    n_seq, t_len, _ = x_sample.shape
    n_prompt = n_batch * seq
    assert d == D_MODEL and t_len == SUBLANES and norm_mix.shape[0] == 2

    xp = x_prompt.reshape(n_prompt, d)
    xs = x_sample.reshape(n_seq * t_len, d)

    cos_p, sin_p = _rope_tables(jnp.arange(seq, dtype=jnp.int32))
    cos_s, sin_s = _rope_tables(PAST_LEN + jnp.arange(t_len, dtype=jnp.int32))
    bt = 16
    cos_s, sin_s = jnp.tile(cos_s, (bt, 1)), jnp.tile(sin_s, (bt, 1))
    lane = jnp.arange(LANES)
    seg = (lane[:, None] // HEAD_DIM == lane[None, :] // HEAD_DIM).astype(BF16)
    reps = LANES // HEAD_DIM
    qg = jnp.tile(ab_q_gain[0], reps).reshape(1, LANES)
    kg = jnp.tile(ab_k_gain[0], reps).reshape(1, LANES)
    wo_ab = ab_w_out[0].astype(BF16)

    z = _norm_proj([xp, xs], norm_mix[0], ab_w_in[0].astype(BF16))
    y_p, k_p, v_p, c_p = _ab_prompt(z, xp, n_batch, seq, cos_p, sin_p, ab_conv_w[0], qg, kg, ab_sinks[0],
                                    seg, wo_ab)
    y_s, c_s, k_s, v_s = _ab_sample(z, xs, n_prompt, n_seq, t_len, cos_s, sin_s, ab_conv_w[0], qg, kg,
                                    ab_sinks[0], seg, wo_ab, cache_conv[0],
                                    cache_win_k[0].reshape(n_seq, WINDOW, LANES),
                                    cache_win_v[0].reshape(n_seq, WINDOW, LANES), bt=bt)
    x = jnp.concatenate([y_p.reshape(n_prompt, d), y_s], axis=0)
    u_all = (peer_u * RSQRT2).astype(BF16)
    v_all = peer_v.astype(BF16)
    x = _peer(x, norm_ffn[0], peer_w_q[0].astype(BF16), peer_sub_keys[0].astype(BF16), u_all, v_all, 0)

    n_gate = 2 * ML_HEADS
    w_in = jnp.pad(ml_w_in[0], ((0, 0), (0, ML_IN_PAD - ml_w_in.shape[2]))).astype(BF16)
    bias = jnp.pad(ml_gate_bias[0], (0, LANES - n_gate)).reshape(1, LANES)
    og = ml_out_gain[0].reshape(1, D_MODEL)
    idx = jnp.arange(ML_CHUNK)
    tril = (idx[None, :] <= idx[:, None]).astype(BF16)
    wo_ml = ml_w_out[0].astype(BF16)

    z = _norm_proj([x], norm_mix[1], w_in)
    y_p, cm_p, nm_p, mm_p = _mlstm_prompt(z, x, n_batch, seq, bias, og, tril, wo_ml)
    y_s, cm_s, nm_s, mm_s = _mlstm_sample(z, x, n_prompt, n_seq, t_len, bias, og, tril, wo_ml,
                                          state_mlstm_C[0], state_mlstm_n[0],
                                          state_mlstm_m[0].reshape(n_seq, 1, ML_HEADS))
    x = jnp.concatenate([y_p.reshape(n_prompt, d), y_s], axis=0)
    out_p, out_s = _peer(x, norm_ffn[1], peer_w_q[1].astype(BF16), peer_sub_keys[1].astype(BF16), u_all, v_all, 1,
                         split_rows=n_prompt)

    y_prompt = out_p.reshape(n_batch, seq, d)
    y_sample = out_s.reshape(n_seq, t_len, d)
    kv_shape_p = (1, n_batch, WINDOW, N_KV_HEADS, HEAD_DIM)
    kv_shape_s = (1, n_seq, WINDOW, N_KV_HEADS, HEAD_DIM)
    return (y_prompt, y_sample,
            c_p[:, SUBLANES - 2:, :][None], k_p.reshape(kv_shape_p), v_p.reshape(kv_shape_p),
            cm_p[None], nm_p[None], mm_p.reshape(1, n_batch, ML_HEADS),
            c_s[None], k_s.reshape(kv_shape_s), v_s.reshape(kv_shape_s),
            cm_s[None], nm_s[None], mm_s.reshape(1, n_seq, ML_HEADS))
```

```python
import functools

import jax
import jax.numpy as jnp
from jax import lax
from jax.experimental import pallas as pl
from jax.experimental.pallas import tpu as pltpu

F32 = jnp.float32
BF16 = jnp.bfloat16
EPS = 1e-6

D_MODEL = 1024
CONV_DIM = 512
N_Q_HEADS = 8
N_KV_HEADS = 2
HEAD_DIM = 64
WINDOW = 128
ROPE_THETA = 10000.0
AB_IN = 2304
ML_HEADS = 4
ML_QK = 128
ML_V = 256
ML_CHUNK = 128
ML_GATE_COL = 3072
ML_IN_PAD = ML_GATE_COL + 128
N_KEYS = 128
PEER_HEADS = 8
PEER_TOPK = 16
PAST_LEN = 16384

LANES = 128
SUBLANES = 8
G_PITCH = N_KEYS // 2 + SUBLANES
VMEM_LIMIT = 56 * 1024 * 1024

NT_DIMS = (((1,), (1,)), ((), ()))
TN_DIMS = (((0,), (0,)), ((), ()))


def _cparams(n_axes, vmem=VMEM_LIMIT):
    return pltpu.CompilerParams(dimension_semantics=("arbitrary",) * n_axes, vmem_limit_bytes=vmem)


def _rmsnorm(x, g):
    return x * lax.rsqrt(jnp.mean(x * x, axis=-1, keepdims=True) + EPS) * g


def _bf16_pieces(a, terms):
    pieces = []
    rem = a
    for _ in range(terms):
        piece = rem.astype(BF16)
        rem = rem - piece.astype(F32)
        pieces.append(piece)
    return pieces


def _split_dot(a, b_bf16, terms=2):
    return sum(jnp.dot(p, b_bf16, preferred_element_type=F32) for p in _bf16_pieces(a, terms))


def _norm_proj_kernel(*refs, starts):
    x_refs = refs[:len(starts)]
    g_ref, w_ref, o_ref = refs[len(starts):]
    i = pl.program_id(0)
    x = x_refs[0][...]
    for x_ref, start in zip(x_refs[1:], starts[1:]):
        x = jnp.where(i >= start, x_ref[...], x)
    r = _rmsnorm(x, g_ref[...])
    o_ref[...] = jnp.dot(r.astype(BF16), w_ref[...], preferred_element_type=F32)


def _norm_proj(x_parts, g, w_bf16, tm=1024):
    d = x_parts[0].shape[1]
    nout = w_bf16.shape[1]
    blocks = [p.shape[0] // tm for p in x_parts]
    starts = tuple(sum(blocks[:k]) for k in range(len(blocks)))
    part_spec = lambda start, nblk: pl.BlockSpec((tm, d), lambda i: (jnp.clip(i - start, 0, nblk - 1), 0))
    return pl.pallas_call(
        functools.partial(_norm_proj_kernel, starts=starts),
        grid=(sum(blocks),),
        in_specs=[part_spec(s, nb) for s, nb in zip(starts, blocks)] + [
            pl.BlockSpec((1, d), lambda i: (0, 0)),
            pl.BlockSpec((d, nout), lambda i: (0, 0))],
        out_specs=pl.BlockSpec((tm, nout), lambda i: (i, 0)),
        out_shape=jax.ShapeDtypeStruct((sum(blocks) * tm, nout), F32),
        compiler_params=_cparams(1),
        name="norm_proj",
    )(*x_parts, g.reshape(1, d), w_bf16)


def _headnorm_rope(xc, gain, cos, sin, seg, hi_half):
    ss = _split_dot(xc * xc, seg)
    xn = xc * lax.rsqrt(ss * (1.0 / HEAD_DIM) + EPS) * gain
    partner = jnp.where(hi_half, pltpu.roll(xn, 32, 1), pltpu.roll(xn, 96, 1))
    return xn * cos + partner * sin


def _softmax_sink(s, mask, sink):
    s = jnp.where(mask, s, -1e30)
    m = jnp.maximum(jnp.max(s, axis=-1, keepdims=True), sink)
    p = jnp.exp(s - m)
    denom = jnp.sum(p, axis=-1, keepdims=True) + jnp.exp(sink - m)
    return (p / denom).astype(BF16)


def _ab_prompt_kernel(sink_ref, *refs, n_batch):
    z_refs, x_refs = refs[0:n_batch], refs[n_batch:2 * n_batch]
    (cos_ref, sin_ref, cw_ref, qg_ref, kg_ref, seg_ref, wo_ref,
     y_ref, kst_ref, vst_ref, cst_ref, pk_ref, pv_ref, pu_ref) = refs[2 * n_batch:]
    j = pl.program_id(0)

    @pl.when(j == 0)
    def _():
        pk_ref[...] = jnp.zeros_like(pk_ref)
        pv_ref[...] = jnp.zeros_like(pv_ref)
        pu_ref[...] = jnp.zeros_like(pu_ref)

    for b in range(n_batch):
        _ab_prompt_block(j, sink_ref, z_refs[b], x_refs[b], cos_ref, sin_ref, cw_ref, qg_ref, kg_ref, seg_ref,
                         wo_ref, y_ref.at[b], kst_ref.at[b], vst_ref.at[b], cst_ref.at[b],
                         pk_ref.at[b], pv_ref.at[b], pu_ref.at[b])


def _ab_prompt_block(j, sink_ref, z_ref, x_ref, cos_ref, sin_ref, cw_ref, qg_ref, kg_ref, seg_ref, wo_ref,
                     y_ref, kst_ref, vst_ref, cst_ref, pk_ref, pv_ref, pu_ref):
    blk = z_ref.shape[0]
    cos = cos_ref[...]
    sin = sin_ref[...]
    seg = seg_ref[...]
    hi_half = (lax.broadcasted_iota(jnp.int32, (blk, LANES), 1) & 32) != 0

    gate_b = z_ref[:, 0:CONV_DIM]
    u = z_ref[:, CONV_DIM:2 * CONV_DIM] * z_ref[:, 2 * CONV_DIM:3 * CONV_DIM]
    ng = blk // SUBLANES
    u3 = u.reshape(ng, SUBLANES, CONV_DIM)
    ext = jnp.concatenate([pu_ref[...][None], u3], axis=0)
    t8 = lax.broadcasted_iota(jnp.int32, (ng, SUBLANES, CONV_DIM), 1)
    r1 = pltpu.roll(ext, 1, 1)
    r2 = pltpu.roll(ext, 2, 1)
    um1 = jnp.where(t8 >= 1, r1[1:], r1[:-1])
    um2 = jnp.where(t8 >= 2, r2[1:], r2[:-1])
    cw = cw_ref[...]
    conv = cw[0:1][None] * um2 + cw[1:2][None] * um1 + cw[2:3][None] * u3
    yconv = gate_b * conv.reshape(blk, CONV_DIM)

    q0 = 3 * CONV_DIM
    k0 = q0 + N_Q_HEADS * HEAD_DIM
    v0 = k0 + N_KV_HEADS * HEAD_DIM
    qg = qg_ref[...]
    qr = [_headnorm_rope(z_ref[:, q0 + c * LANES:q0 + (c + 1) * LANES], qg, cos, sin, seg, hi_half)
          for c in range(N_Q_HEADS * HEAD_DIM // LANES)]
    kr = _headnorm_rope(z_ref[:, k0:k0 + LANES], kg_ref[...], cos, sin, seg, hi_half)
    v = z_ref[:, v0:v0 + LANES]
    pk = pk_ref[...]
    pv = pv_ref[...]

    row = lax.broadcasted_iota(jnp.int32, (blk, 2 * blk), 0)
    col = lax.broadcasted_iota(jnp.int32, (blk, 2 * blk), 1)
    row_prev = row + jnp.where(j == 0, 2 * blk, 0)
    mask = ((col < blk) & (col > row_prev)) | ((col >= blk) & (col - blk <= row))

    kks, vvs = [], []
    for g in range(N_KV_HEADS):
        sl = slice(g * HEAD_DIM, (g + 1) * HEAD_DIM)
        kks.append(jnp.concatenate([pk[:, sl], kr[:, sl]], axis=0).astype(BF16))
        vvs.append(jnp.concatenate([pv[:, sl], v[:, sl]], axis=0).astype(BF16))
    outs = []
    for h in range(N_Q_HEADS):
        g = h // (N_Q_HEADS // N_KV_HEADS)
        qh = qr[h // 2][:, (h % 2) * HEAD_DIM:(h % 2 + 1) * HEAD_DIM].astype(BF16)
        s = lax.dot_general(qh, kks[g], NT_DIMS, preferred_element_type=F32) * (HEAD_DIM ** -0.5)
        p = _softmax_sink(s, mask, sink_ref[h])
        outs.append(jnp.dot(p, vvs[g], preferred_element_type=F32))
    attn = jnp.concatenate(outs, axis=1)

    y = (jnp.dot(yconv.astype(BF16), wo_ref[0:CONV_DIM, :], preferred_element_type=F32)
         + jnp.dot(attn.astype(BF16), wo_ref[CONV_DIM:2 * CONV_DIM, :], preferred_element_type=F32))
    y_ref[...] = y + x_ref[...]

    pk_ref[...] = kr
    pv_ref[...] = v
    pu_ref[...] = u3[ng - 1]
    kst_ref[...] = kr
    vst_ref[...] = v
    cst_ref[...] = u3[ng - 1]


def _ab_prompt(z, x, n_batch, seq, cos, sin, cw, qg, kg, sinks, seg, wo):
    blk = WINDOW
    nb = seq // blk
    const = lambda j, s: (0, 0)
    whole = lambda j, s: (0, 0, 0)
    tok_specs = lambda width: [pl.BlockSpec((blk, width), functools.partial(lambda j, s, b: (b * nb + j, 0), b=b))
                               for b in range(n_batch)]
    grid_spec = pltpu.PrefetchScalarGridSpec(
        num_scalar_prefetch=1,
        grid=(nb,),
        in_specs=tok_specs(AB_IN) + tok_specs(D_MODEL) + [
            pl.BlockSpec((blk, LANES), lambda j, s: (j, 0)),
            pl.BlockSpec((blk, LANES), lambda j, s: (j, 0)),
            pl.BlockSpec((3, CONV_DIM), const),
            pl.BlockSpec((1, LANES), const),
            pl.BlockSpec((1, LANES), const),
            pl.BlockSpec((LANES, LANES), const),
            pl.BlockSpec((D_MODEL, D_MODEL), const)],
        out_specs=[pl.BlockSpec((n_batch, blk, D_MODEL), lambda j, s: (0, j, 0)),
                   pl.BlockSpec((n_batch, blk, LANES), whole),
                   pl.BlockSpec((n_batch, blk, LANES), whole),
                   pl.BlockSpec((n_batch, SUBLANES, CONV_DIM), whole)],
        scratch_shapes=[pltpu.VMEM((n_batch, blk, LANES), F32), pltpu.VMEM((n_batch, blk, LANES), F32),
                        pltpu.VMEM((n_batch, SUBLANES, CONV_DIM), F32)])
    return pl.pallas_call(
        functools.partial(_ab_prompt_kernel, n_batch=n_batch),
        grid_spec=grid_spec,
        out_shape=[jax.ShapeDtypeStruct((n_batch, seq, D_MODEL), F32),
                   jax.ShapeDtypeStruct((n_batch, blk, LANES), F32),
                   jax.ShapeDtypeStruct((n_batch, blk, LANES), F32),
                   jax.ShapeDtypeStruct((n_batch, SUBLANES, CONV_DIM), F32)],
        compiler_params=_cparams(1),
        name="ab_prompt",
    )(sinks, *([z] * n_batch), *([x] * n_batch), cos, sin, cw, qg, kg, seg, wo)


def _ab_sample_kernel(sink_ref, z_ref, x_ref, cos_ref, sin_ref, cw_ref, qg_ref, kg_ref, seg_ref, wo_ref,
                      cc_ref, ck_ref, cv_ref, y_ref, cs_ref, ks_ref, vs_ref):
    rows = z_ref.shape[0]
    t_len = SUBLANES
    bt = rows // t_len
    cos = cos_ref[...]
    sin = sin_ref[...]
    seg = seg_ref[...]
    hi_half = (lax.broadcasted_iota(jnp.int32, (rows, LANES), 1) & 32) != 0

    gate_b = z_ref[:, 0:CONV_DIM]
    u = z_ref[:, CONV_DIM:2 * CONV_DIM] * z_ref[:, 2 * CONV_DIM:3 * CONV_DIM]
    u3 = u.reshape(bt, t_len, CONV_DIM)
    cc = cc_ref[...]
    c0 = cc[:, 0:1, :]
    c1 = cc[:, 1:2, :]
    t8 = lax.broadcasted_iota(jnp.int32, (bt, t_len, CONV_DIM), 1)
    r1 = pltpu.roll(u3, 1, 1)
    r2 = pltpu.roll(u3, 2, 1)
    um1 = jnp.where(t8 >= 1, r1, c1)
    um2 = jnp.where(t8 >= 2, r2, jnp.where(t8 == 1, c1, c0))
    cw = cw_ref[...]
    conv = cw[0:1][None] * um2 + cw[1:2][None] * um1 + cw[2:3][None] * u3
    yconv = gate_b * conv.reshape(rows, CONV_DIM)
    cs_ref[...] = r2[:, 0:2, :]

    q0 = 3 * CONV_DIM
    k0 = q0 + N_Q_HEADS * HEAD_DIM
    v0 = k0 + N_KV_HEADS * HEAD_DIM
    qg = qg_ref[...]
    qr = [_headnorm_rope(z_ref[:, q0 + c * LANES:q0 + (c + 1) * LANES], qg, cos, sin, seg, hi_half)
          for c in range(N_Q_HEADS * HEAD_DIM // LANES)]
    kr = _headnorm_rope(z_ref[:, k0:k0 + LANES], kg_ref[...], cos, sin, seg, hi_half)
    v = z_ref[:, v0:v0 + LANES]

    group = N_Q_HEADS // N_KV_HEADS
    nq = group * t_len
    nk = 2 * WINDOW
    qrow = lax.broadcasted_iota(jnp.int32, (nq, nk), 0)
    t_q = qrow & (t_len - 1)
    col = lax.broadcasted_iota(jnp.int32, (nq, nk), 1)
    mask = (((col < WINDOW) & (col > t_q)) | ((col >= WINDOW) & (col - WINDOW <= t_q)))[None]
    hrow = lax.broadcasted_iota(jnp.int32, (nq, 1), 0) // t_len
    pad = jnp.zeros((bt, nk - WINDOW - t_len, HEAD_DIM), F32)

    outs = [None] * N_Q_HEADS
    for g in range(N_KV_HEADS):
        sl = slice(g * HEAD_DIM, (g + 1) * HEAD_DIM)
        qs = jnp.concatenate(
            [qr[h // 2][:, (h % 2) * HEAD_DIM:(h % 2 + 1) * HEAD_DIM].reshape(bt, t_len, HEAD_DIM)
             for h in range(g * group, (g + 1) * group)], axis=1)
        kk = jnp.concatenate([ck_ref[:, :, sl], kr[:, sl].reshape(bt, t_len, HEAD_DIM), pad], axis=1)
        vv = jnp.concatenate([cv_ref[:, :, sl], v[:, sl].reshape(bt, t_len, HEAD_DIM), pad], axis=1)
        s = jnp.einsum('bqd,bkd->bqk', qs.astype(BF16), kk.astype(BF16),
                       preferred_element_type=F32) * (HEAD_DIM ** -0.5)
        sink = jnp.zeros((nq, 1), F32)
        for hh in range(group):
            sink = jnp.where(hrow == hh, sink_ref[g * group + hh], sink)
        p = _softmax_sink(s, mask, sink[None])
        o = jnp.einsum('bqk,bkd->bqd', p, vv.astype(BF16), preferred_element_type=F32)
        for hh in range(group):
            outs[g * group + hh] = o[:, hh * t_len:(hh + 1) * t_len, :].reshape(rows, HEAD_DIM)
    attn = jnp.concatenate(outs, axis=1)

    y = (jnp.dot(yconv.astype(BF16), wo_ref[0:CONV_DIM, :], preferred_element_type=F32)
         + jnp.dot(attn.astype(BF16), wo_ref[CONV_DIM:2 * CONV_DIM, :], preferred_element_type=F32))
    y_ref[...] = y + x_ref[...]

    keep = WINDOW - t_len
    ks_ref[:, 0:keep, :] = ck_ref[:, t_len:WINDOW, :]
    ks_ref[:, keep:WINDOW, :] = kr.reshape(bt, t_len, LANES)
    vs_ref[:, 0:keep, :] = cv_ref[:, t_len:WINDOW, :]
    vs_ref[:, keep:WINDOW, :] = v.reshape(bt, t_len, LANES)


def _ab_sample(z, x, row0, n_seq, t_len, cos, sin, cw, qg, kg, sinks, seg, wo, cc, ck, cv, bt=16):
    rows = bt * t_len
    blk0 = row0 // rows
    tok = lambda i, s: (blk0 + i, 0)
    const = lambda i, s: (0, 0)
    seq3 = lambda i, s: (i, 0, 0)
    grid_spec = pltpu.PrefetchScalarGridSpec(
        num_scalar_prefetch=1,
        grid=(n_seq // bt,),
        in_specs=[pl.BlockSpec((rows, AB_IN), tok),
                  pl.BlockSpec((rows, D_MODEL), lambda i, s: (i, 0)),
                  pl.BlockSpec((rows, LANES), const),
                  pl.BlockSpec((rows, LANES), const),
                  pl.BlockSpec((3, CONV_DIM), const),
                  pl.BlockSpec((1, LANES), const),
                  pl.BlockSpec((1, LANES), const),
                  pl.BlockSpec((LANES, LANES), const),
                  pl.BlockSpec((D_MODEL, D_MODEL), const),
                  pl.BlockSpec((bt, 2, CONV_DIM), seq3),
                  pl.BlockSpec((bt, WINDOW, LANES), seq3),
                  pl.BlockSpec((bt, WINDOW, LANES), seq3)],
        out_specs=[pl.BlockSpec((rows, D_MODEL), lambda i, s: (i, 0)),
                   pl.BlockSpec((bt, 2, CONV_DIM), seq3),
                   pl.BlockSpec((bt, WINDOW, LANES), seq3),
                   pl.BlockSpec((bt, WINDOW, LANES), seq3)])
    return pl.pallas_call(
        _ab_sample_kernel,
        grid_spec=grid_spec,
        out_shape=[jax.ShapeDtypeStruct((n_seq * t_len, D_MODEL), F32),
                   jax.ShapeDtypeStruct((n_seq, 2, CONV_DIM), F32),
                   jax.ShapeDtypeStruct((n_seq, WINDOW, LANES), F32),
                   jax.ShapeDtypeStruct((n_seq, WINDOW, LANES), F32)],
        compiler_params=_cparams(1),
        name="ab_sample",
    )(sinks, z, x, cos, sin, cw, qg, kg, seg, wo, cc, ck, cv)


def _log_sigmoid(x):
    return jnp.minimum(x, 0.0) - jnp.log(1.0 + jnp.exp(-jnp.abs(x)))


def _mlstm_chunk(z, bias, og, tril, c_src, n_src, m_src, c_dst, n_dst, m_dst, n_real, rows):
    L = z.shape[0]
    gates = z[:, ML_GATE_COL:ML_GATE_COL + LANES] + bias
    if n_real < L:
        live = lax.broadcasted_iota(jnp.int32, (L, LANES), 0) < n_real
        li_all = jnp.where(live, gates, -1e30)
        lf_all = jnp.where(live, _log_sigmoid(gates), 0.0)
    else:
        li_all = gates
        lf_all = _log_sigmoid(gates)
    lf_pieces = _bf16_pieces(lf_all, 3)
    f_col_all = sum(jnp.dot(tril[0:rows], p, preferred_element_type=F32) for p in lf_pieces)
    f_row_all = sum(lax.dot_general(p, tril, (((0,), (1,)), ((), ())), preferred_element_type=F32)
                    for p in lf_pieces)
    li_t = li_all.T
    rr = lax.broadcasted_iota(jnp.int32, (rows, L), 0)
    cc = lax.broadcasted_iota(jnp.int32, (rows, L), 1)
    causal = cc <= rr

    outs, m_new_all = [], []
    for h in range(ML_HEADS):
        f_col = f_col_all[:, ML_HEADS + h:ML_HEADS + h + 1]
        f_row = f_row_all[ML_HEADS + h:ML_HEADS + h + 1, :]
        li_row = li_t[h:h + 1, :]
        li_col = li_all[0:rows, h:h + 1]
        m0 = m_src[0:1, h:h + 1]
        c0 = c_src[h]
        n0 = n_src[h:h + 1, :]
        qh = z[0:rows, h * ML_QK:(h + 1) * ML_QK]
        kh = z[:, ML_HEADS * ML_QK + h * ML_QK:ML_HEADS * ML_QK + (h + 1) * ML_QK] * (ML_QK ** -0.5)
        v_off = 2 * ML_HEADS * ML_QK
        vh = z[:, v_off + h * ML_V:v_off + (h + 1) * ML_V]
        o_off = v_off + ML_HEADS * ML_V
        oh = z[0:rows, o_off + h * ML_V:o_off + (h + 1) * ML_V]
        qb = qh.astype(BF16)
        vb = vh.astype(BF16)

        dmat = jnp.where(causal, f_col - f_row + li_row, -jnp.inf)
        gcar = f_col + m0
        m_t = jnp.maximum(jnp.max(dmat, axis=-1, keepdims=True), gcar)
        w = jnp.exp(dmat - m_t)
        s = lax.dot_general(qb, kh.astype(BF16), NT_DIMS, preferred_element_type=F32) * w
        carry = jnp.exp(gcar - m_t)
        num = (jnp.dot(s.astype(BF16), vb, preferred_element_type=F32)
               + jnp.dot(qb, c0.astype(BF16), preferred_element_type=F32) * carry)
        den = jnp.sum(s, axis=-1, keepdims=True) + carry * jnp.sum(qh * n0, axis=-1, keepdims=True)
        hout = num / jnp.maximum(jnp.abs(den), jnp.exp(-m_t))

        f_last = f_col[n_real - 1:n_real, :]
        w_end = f_last - f_col + li_col
        m_new = jnp.maximum(f_last + m0, jnp.max(w_end, axis=0, keepdims=True))
        a_end = jnp.exp(w_end - m_new)
        scale = jnp.exp(f_last + m0 - m_new)
        ka = kh[0:rows] * a_end
        c_dst[h] = scale * c0 + lax.dot_general(ka.astype(BF16), vb[0:rows], TN_DIMS,
                                                preferred_element_type=F32)
        n_dst[h:h + 1, :] = scale * n0 + jnp.sum(ka, axis=0, keepdims=True)
        m_new_all.append(m_new)

        hn = _rmsnorm(hout, og[:, h * ML_V:(h + 1) * ML_V])
        outs.append(jax.nn.sigmoid(oh) * hn)
    m_dst[...] = jnp.concatenate(m_new_all, axis=1)
    return jnp.concatenate(outs, axis=1)


def _mlstm_prompt_kernel(*refs, n_batch):
    z_refs, x_refs = refs[0:n_batch], refs[n_batch:2 * n_batch]
    bias_ref, og_ref, tril_ref, wo_ref, y_ref, c_ref, n_ref, m_ref = refs[2 * n_batch:]

    @pl.when(pl.program_id(0) == 0)
    def _():
        c_ref[...] = jnp.zeros_like(c_ref)
        n_ref[...] = jnp.zeros_like(n_ref)
        m_ref[...] = jnp.zeros_like(m_ref)

    for b in range(n_batch):
        state = (c_ref.at[b], n_ref.at[b], m_ref.at[b])
        out = _mlstm_chunk(z_refs[b][...], bias_ref[...], og_ref[...], tril_ref[...], *state, *state,
                           ML_CHUNK, ML_CHUNK)
        y_ref[b] = jnp.dot(out.astype(BF16), wo_ref[...], preferred_element_type=F32) + x_refs[b][...]


def _mlstm_sample_kernel(z_ref, x_ref, bias_ref, og_ref, tril_ref, wo_ref, c0_ref, n0_ref, m0_ref,
                         y_ref, c_ref, n_ref, m_ref, *, t_len):
    n_here = z_ref.shape[0] // t_len
    q_rows = 2 * SUBLANES
    outs = []
    for s in range(n_here):
        zpad = jnp.concatenate([z_ref[s * t_len:(s + 1) * t_len, :],
                                jnp.zeros((ML_CHUNK - t_len, ML_IN_PAD), F32)], axis=0)
        out = _mlstm_chunk(zpad, bias_ref[...], og_ref[...], tril_ref[...],
                           c0_ref.at[s], n0_ref.at[s], m0_ref.at[s], c_ref.at[s], n_ref.at[s], m_ref.at[s],
                           t_len, q_rows)
        outs.append(out[0:t_len])
    out_all = jnp.concatenate(outs, axis=0).astype(BF16)
    y_ref[...] = jnp.dot(out_all, wo_ref[...], preferred_element_type=F32) + x_ref[...]


def _mlstm_weight_specs(const):
    return [pl.BlockSpec((1, LANES), const),
            pl.BlockSpec((1, D_MODEL), const),
            pl.BlockSpec((ML_CHUNK, ML_CHUNK), const),
            pl.BlockSpec((D_MODEL, D_MODEL), const)]


def _mlstm_prompt(z, x, n_batch, seq, bias, og, tril, wo):
    nc = seq // ML_CHUNK
    const = lambda j: (0, 0)
    tok_specs = lambda width: [pl.BlockSpec((ML_CHUNK, width), functools.partial(lambda j, b: (b * nc + j, 0), b=b))
                               for b in range(n_batch)]
    return pl.pallas_call(
        functools.partial(_mlstm_prompt_kernel, n_batch=n_batch),
        grid=(nc,),
        in_specs=tok_specs(ML_IN_PAD) + tok_specs(D_MODEL) + _mlstm_weight_specs(const),
        out_specs=[pl.BlockSpec((n_batch, ML_CHUNK, D_MODEL), lambda j: (0, j, 0)),
                   pl.BlockSpec((n_batch, ML_HEADS, ML_QK, ML_V), lambda j: (0, 0, 0, 0)),
                   pl.BlockSpec((n_batch, ML_HEADS, ML_QK), lambda j: (0, 0, 0)),
                   pl.BlockSpec((n_batch, 1, ML_HEADS), lambda j: (0, 0, 0))],
        out_shape=[jax.ShapeDtypeStruct((n_batch, seq, D_MODEL), F32),
                   jax.ShapeDtypeStruct((n_batch, ML_HEADS, ML_QK, ML_V), F32),
                   jax.ShapeDtypeStruct((n_batch, ML_HEADS, ML_QK), F32),
                   jax.ShapeDtypeStruct((n_batch, 1, ML_HEADS), F32)],
        compiler_params=_cparams(1),
        name="mlstm_prompt",
    )(*([z] * n_batch), *([x] * n_batch), bias, og, tril, wo)


def _mlstm_sample(z, x, row0, n_seq, t_len, bias, og, tril, wo, c0, n0, m0, seqs_per_step=8):
    rows = seqs_per_step * t_len
    blk0 = row0 // rows
    tok = lambda i: (blk0 + i, 0)
    const = lambda i: (0, 0)
    st4 = lambda i: (i, 0, 0, 0)
    st3 = lambda i: (i, 0, 0)
    state_specs = [pl.BlockSpec((seqs_per_step, ML_HEADS, ML_QK, ML_V), st4),
                   pl.BlockSpec((seqs_per_step, ML_HEADS, ML_QK), st3),
                   pl.BlockSpec((seqs_per_step, 1, ML_HEADS), st3)]
    return pl.pallas_call(
        functools.partial(_mlstm_sample_kernel, t_len=t_len),
        grid=(n_seq // seqs_per_step,),
        in_specs=[pl.BlockSpec((rows, ML_IN_PAD), tok),
                  pl.BlockSpec((rows, D_MODEL), tok)] + _mlstm_weight_specs(const) + state_specs,
        out_specs=[pl.BlockSpec((rows, D_MODEL), lambda i: (i, 0))] + state_specs,
        out_shape=[jax.ShapeDtypeStruct((n_seq * t_len, D_MODEL), F32),
                   jax.ShapeDtypeStruct((n_seq, ML_HEADS, ML_QK, ML_V), F32),
                   jax.ShapeDtypeStruct((n_seq, ML_HEADS, ML_QK), F32),
                   jax.ShapeDtypeStruct((n_seq, 1, ML_HEADS), F32)],
        compiler_params=_cparams(1),
        name="mlstm_sample",
    )(z, x, bias, og, tril, wo, c0, n0, m0)


RSQRT2 = 0.7071067811865476


def _scaled_gelu(xs):
    return xs * (1.0 + lax.erf(xs))


SEL_T = 256


def _batcher_pairs(n):
    pairs = []
    p = 1
    while p < n:
        k = p
        while k >= 1:
            for j in range(k % p, n - k, 2 * k):
                for i in range(min(k, n - j - k)):
                    if (i + j) // (2 * p) == (i + j + k) // (2 * p):
                        pairs.append((i + j, i + j + k))
            k //= 2
        p *= 2
    return pairs


def _bitonic_merge_pairs(n):
    pairs = []
    s = n // 2
    while s >= 1:
        pairs += [(i, i + s) for i in range(n) if not i & s]
        s //= 2
    return pairs


def _top16_of_keys(s, out):
    n_vreg = N_KEYS // SUBLANES
    assert n_vreg == PEER_TOPK and N_KEYS == 128
    rounded = (s + 0.0).astype(BF16).astype(F32)
    bits = lax.bitcast_convert_type(rounded, jnp.int32)
    flip = (lax.shift_right_arithmetic(bits, 31) & jnp.int32(0x7FFF0000)) | jnp.int32(-2 ** 31)
    row = lax.broadcasted_iota(jnp.int32, s.shape, 0)
    keys = lax.bitcast_convert_type(
        lax.shift_right_logical(bits ^ flip, 9) | ((N_KEYS - 1 - row) | 0x4B000000), F32)

    x = [keys[v * SUBLANES:(v + 1) * SUBLANES] for v in range(n_vreg)]

    def exchange(i, j):
        x[i], x[j] = jnp.maximum(x[i], x[j]), jnp.minimum(x[i], x[j])

    for i, j in _batcher_pairs(n_vreg):
        exchange(i, j)
    yield
    for shift in (SUBLANES // 2, SUBLANES // 4, SUBLANES // 8):
        other = [pltpu.roll(a, shift, 0) for a in x]
        x = [jnp.maximum(x[i], other[n_vreg - 1 - i]) for i in range(n_vreg)]
        for i, j in _bitonic_merge_pairs(n_vreg):
            exchange(i, j)
        yield
    top = lax.bitcast_convert_type(jnp.concatenate([a[0:1] for a in x], axis=0), jnp.int32) & 0x7FFFFF
    rows = (N_KEYS - 1 - (top & (N_KEYS - 1))).astype(F32)
    code = lax.shift_right_logical(top, 7)
    b16 = jnp.where(code >= 0x8000, code ^ 0x8000, code ^ 0xFFFF)
    out.append((lax.bitcast_convert_type(b16 << 16, F32), rows))


class _TopK:
    PAYLOAD_RANGE = N_KEYS * N_KEYS

    def __init__(self, s, payload):
        assert s.shape[0] * self.PAYLOAD_RANGE < 2 ** 24
        self.s = s
        rows = lax.broadcasted_iota(jnp.int32, s.shape, 0).astype(F32)
        self.code = rows * self.PAYLOAD_RANGE + payload
        self.vals, self.codes = [], []

    def step(self, n):
        bound = float(self.s.shape[0] * self.PAYLOAD_RANGE)
        for _ in range(n):
            m = jnp.max(self.s, axis=0, keepdims=True)
            first = jnp.min(jnp.where(self.s == m, self.code, bound), axis=0, keepdims=True)
            self.vals.append(m)
            self.codes.append(first)
            self.s = jnp.where(self.code == first, -jnp.inf, self.s)

    def result(self):
        codes = jnp.concatenate(self.codes, axis=0)
        payload = codes - jnp.floor(codes * (1.0 / self.PAYLOAD_RANGE)) * self.PAYLOAD_RANGE
        return jnp.concatenate(self.vals, axis=0), payload


_PAIR_GROUPS = (
    ((0, 1, 0, 8),),
    ((0, 2, 0, 5), (5, 3, 0, 3)),
    ((0, 3, 3, 1), (1, 4, 0, 3), (4, 5, 0, 2), (6, 6, 0, 2)),
    ((0, 7, 0, 2), (2, 8, 0, -6)),
    ((0, 14, 0, -2),),
)


def _pair_candidates(first, second):
    h8 = SUBLANES
    (v1, i1), (v2, i2) = [tuple(a.astype(F32) for a in lst) for lst in (first, second)]
    row = lax.broadcasted_iota(jnp.int32, (h8, v1.shape[1]), 0)

    def build(a, b, combine, filler):
        groups = [combine(a[0:1], b)]
        for segments in _PAIR_GROUPS:
            group = jnp.full((h8, a.shape[1]), filler, F32)
            for r0, k1, k2, count in segments:
                if count > 0:
                    seg = combine(a[k1:k1 + 1], pltpu.roll(b[0:h8], (r0 - k2) % h8, 0))
                else:
                    seg = combine(pltpu.roll(a[h8:2 * h8], (r0 - (k1 - h8)) % h8, 0), b[0:1])
                group = jnp.where((row >= r0) & (row < r0 + abs(count)), seg, group)
            groups.append(group)
        return jnp.concatenate(groups, axis=0)

    return (build(v1, v2, lambda a, b: a + b, -jnp.inf),
            build(i1, i2, lambda a, b: a * N_KEYS + b, 0.0))


def _peer_fused_kernel(xs_ref, xm_ref, g_ref, wq_ref, sk_ref, u_ref, v_ref, *rest, split_blk):
    if split_blk is None:
        o_ref, q_scr, sel_e, sel_g, xn_scr, a_scr, bgt_scr, bg_scr, gmat, coef_scr = rest
        out_refs = ()
    else:
        *out_refs, q_scr, sel_e, sel_g, xn_scr, a_scr, bgt_scr, bg_scr, gmat, coef_scr, o_ref = rest
    i = pl.program_id(0)
    e = pl.program_id(1)
    n_steps = pl.num_programs(1)
    t = xm_ref.shape[0]
    n_half = t // SEL_T
    half_keys = N_KEYS // 2
    cur = i % 2

    def project_queries():
        xn = _rmsnorm(xs_ref[...], g_ref[...]).astype(BF16)
        q = jnp.dot(xn, wq_ref[...], preferred_element_type=F32).astype(BF16)
        for hf in range(n_half):
            for c in range(2 * PEER_HEADS):
                q_scr[hf, c] = q[hf * SEL_T:(hf + 1) * SEL_T, c * LANES:(c + 1) * LANES]

    @pl.when(e == 0)
    def _():
        @pl.when(i == 0)
        def _():
            project_queries()
            sel_e[...] = jnp.zeros_like(sel_e)
            sel_g[...] = jnp.zeros_like(sel_g)

        x = xm_ref[...]
        xn_scr[...] = _rmsnorm(x, g_ref[...]).astype(BF16)
        o_ref[...] = x
        coef_scr[0] = jnp.zeros(coef_scr.shape[1:], BF16)
        prev = 1 - cur
        lane_groups = SEL_T // LANES
        for hf in range(n_half):
            ef = sel_e[prev, hf]
            af = jnp.floor(ef * (1.0 / N_KEYS))
            bf = ef - af * N_KEYS
            gf = sel_g[prev, hf]
            a_scr[hf * SEL_T:(hf + 1) * SEL_T, :] = af.T
            bgf = bf + RSQRT2 * gf
            bgt_scr[hf * SEL_T:(hf + 1) * SEL_T, :] = bgf.T
            for lg in range(lane_groups):
                bg_scr[hf * lane_groups + lg] = bgf[:, lg * LANES:(lg + 1) * LANES]
        r = lax.broadcasted_iota(jnp.int32, (N_KEYS, LANES), 0)
        packed_shape = (N_KEYS // (2 * SUBLANES), 2 * SUBLANES, LANES)
        a_of_row = jnp.where(r < half_keys, 2 * r, 2 * (r - half_keys) + 1).astype(F32).astype(BF16)
        a_of_row = a_of_row.reshape(packed_shape)
        one = jnp.ones(packed_shape, BF16)
        zero = jnp.zeros(packed_shape, BF16)
        b_of_lane = lax.broadcasted_iota(jnp.int32, (N_KEYS, LANES), 1).astype(F32)
        b_of_row = r.astype(F32).astype(BF16).reshape(packed_shape)
        per_body = 32
        bodies_per_group = LANES // per_body
        row_major_every = 3

        def body(it, carry):
            grp = lax.shift_right_logical(it, bodies_per_group.bit_length() - 1)
            sub = it & (bodies_per_group - 1)
            shift = (LANES - sub * per_body) & (LANES - 1)
            bg = pltpu.roll(bg_scr[grp], shift, 1)
            tok0 = pl.multiple_of(it * per_body, per_body)
            arows = a_scr[pl.ds(tok0, per_body), :]
            bgrows = bgt_scr[pl.ds(tok0, per_body), :]
            for k in range(per_body):
                arow = jnp.broadcast_to(arows[k:k + 1], (2 * SUBLANES, LANES)).astype(BF16)[None]
                pa = jnp.where(a_of_row == arow, one, zero).reshape(N_KEYS, LANES)
                if k % row_major_every == 0:
                    bgrow = jnp.broadcast_to(bgrows[k:k + 1], (2 * SUBLANES, LANES))
                    brow = jnp.floor(bgrow)
                    grow = (bgrow - brow).astype(BF16)[None]
                    qb = jnp.where(b_of_row == brow.astype(BF16)[None], grow, zero).reshape(N_KEYS, LANES)
                    tile = lax.dot_general(pa, qb, NT_DIMS, preferred_element_type=F32)
                else:
                    bgcol = jnp.broadcast_to(bg[:, k:k + 1], (N_KEYS, LANES))
                    bcol = jnp.floor(bgcol)
                    gcol = bgcol - bcol
                    qbt = jnp.where(bcol == b_of_lane, gcol, 0.0).astype(BF16)
                    tile = jnp.dot(pa, qbt, preferred_element_type=F32)
                row0 = pl.multiple_of((tok0 + k) * G_PITCH, SUBLANES)
                gmat[pl.ds(row0, half_keys), :] = pltpu.pack_elementwise(
                    [tile[0:half_keys], tile[half_keys:N_KEYS]], packed_dtype=BF16)
            return carry

        lax.fori_loop(0, t // per_body, body, 0)

    unit = jnp.minimum(e, n_half * PEER_HEADS - 1)
    half = lax.shift_right_logical(unit, PEER_HEADS.bit_length() - 1)
    h = unit & (PEER_HEADS - 1)
    rd = e & 1
    blk = jnp.minimum(e, n_steps - 2)
    eb = u_ref.shape[0]
    n_stages = 4
    exp_w = eb // n_stages
    out_w = v_ref.shape[1] // n_stages
    words_per_stage = exp_w // (2 * N_KEYS)

    def value_piece(c):
        ocols = slice(c * out_w, (c + 1) * out_w)
        o_ref[:, ocols] += jnp.dot(coef_scr[rd], v_ref[:, ocols], preferred_element_type=F32)

    def act_piece(c):
        ecols = slice(c * exp_w, (c + 1) * exp_w)
        act = lax.dot_general(xn_scr[...], u_ref[ecols, :], NT_DIMS, preferred_element_type=F32)
        gates = []
        for w in range(words_per_stage):
            word = gmat[pl.ds((blk * n_stages + c) * words_per_stage + w, t, stride=G_PITCH), :]
            gates.append(lax.bitcast_convert_type(word << 16, F32))
            gates.append(lax.bitcast_convert_type(word & jnp.int32(-65536), F32))
        coef_scr[1 - rd, :, ecols] = (_scaled_gelu(act) * jnp.concatenate(gates, axis=1)).astype(BF16)

    def retrieval():
        lists = []
        for p in range(2):
            st = lax.dot_general(sk_ref[h, p], q_scr[half, 2 * h + p], NT_DIMS, preferred_element_type=F32)
            yield from _top16_of_keys(st, lists)
            yield
        cand, expert = _pair_candidates(lists[0], lists[1])
        second_level = _TopK(cand, payload=expert)
        for _ in range(4):
            second_level.step(PEER_TOPK // 4)
            yield
        best, e_sel = second_level.result()
        ex = jnp.exp(best - best[0:1])
        slot0 = pl.multiple_of(h * PEER_TOPK, PEER_TOPK)
        sel_e[cur, half, pl.ds(slot0, PEER_TOPK), :] = e_sel
        sel_g[cur, half, pl.ds(slot0, PEER_TOPK), :] = ex / jnp.sum(ex, axis=0, keepdims=True)

    pieces = [functools.partial(f, c) for f in (value_piece, act_piece) for c in range(n_stages)]
    parts = retrieval()
    parts_per_piece = 3
    for piece in pieces:
        piece()
        for _ in range(parts_per_piece):
            next(parts, None)
    for _ in parts:
        pass

    @pl.when(e == n_steps - 1)
    def _():
        project_queries()
        if out_refs:
            first_ref, second_ref = out_refs

            @pl.when(i - 1 < split_blk)
            def _():
                first_ref[...] = o_ref[...]

            @pl.when(i - 1 >= split_blk)
            def _():
                second_ref[...] = o_ref[...]


def _peer(x, g, wq, sk, u_all, v_all, layer, t=512, eb=1024, split_rows=None):
    n, d = x.shape
    nslot = PEER_HEADS * PEER_TOPK
    n_tok_blk = n // t
    n_exp_blk = N_KEYS * N_KEYS // eb
    n_half = t // SEL_T
    assert n_exp_blk == n_half * PEER_HEADS
    last_tok = n_tok_blk - 1
    once = pl.Buffered(1)
    if split_rows is None:
        split_blk = None
        out_specs = pl.BlockSpec((t, d), lambda i, e: (jnp.maximum(i - 1, 0), 0))
        out_shape = jax.ShapeDtypeStruct((n, d), F32)
        acc = []
    else:
        split_blk = split_rows // t
        out_specs = [pl.BlockSpec((t, d), lambda i, e: (jnp.clip(i - 1, 0, split_blk - 1), 0)),
                     pl.BlockSpec((t, d), lambda i, e: (jnp.clip(i - 1 - split_blk, 0, last_tok - split_blk), 0))]
        out_shape = [jax.ShapeDtypeStruct((split_rows, d), F32), jax.ShapeDtypeStruct((n - split_rows, d), F32)]
        acc = [pltpu.VMEM((t, d), F32)]
    return pl.pallas_call(
        functools.partial(_peer_fused_kernel, split_blk=split_blk),
        grid=(n_tok_blk + 1, n_exp_blk + 1),
        in_specs=[pl.BlockSpec((t, d), lambda i, e: (jnp.minimum(i + e // n_exp_blk, last_tok), 0),
                               pipeline_mode=once),
                  pl.BlockSpec((t, d), lambda i, e: (jnp.maximum(i - 1, 0), 0)),
                  pl.BlockSpec((1, d), lambda i, e: (0, 0)),
                  pl.BlockSpec((d, 2 * PEER_HEADS * LANES), lambda i, e: (0, 0), pipeline_mode=once),
                  pl.BlockSpec((PEER_HEADS, 2, N_KEYS, LANES), lambda i, e: (0, 0, 0, 0), pipeline_mode=once),
                  pl.BlockSpec((None, eb, d), lambda i, e: (layer, jnp.minimum(e, n_exp_blk - 1), 0)),
                  pl.BlockSpec((None, eb, d), lambda i, e: (layer, jnp.maximum(e - 1, 0), 0))],
        out_specs=out_specs,
        out_shape=out_shape,
        scratch_shapes=[pltpu.VMEM((n_half, 2 * PEER_HEADS, SEL_T, LANES), BF16),
                        pltpu.VMEM((2, n_half, nslot, SEL_T), F32),
                        pltpu.VMEM((2, n_half, nslot, SEL_T), F32),
                        pltpu.VMEM((t, d), BF16),
                        pltpu.VMEM((t, nslot), F32),
                        pltpu.VMEM((t, nslot), F32),
                        pltpu.VMEM((t // LANES, nslot, LANES), F32),
                        pltpu.VMEM((t * G_PITCH, LANES), jnp.int32),
                        pltpu.VMEM((2, t, eb), BF16)] + acc,
        compiler_params=_cparams(2),
        name="peer",
    )(x, x, g.reshape(1, d), wq, sk, u_all, v_all)


def _rope_tables(pos):
    half = HEAD_DIM // 2
    inv_freq = ROPE_THETA ** (-jnp.arange(half, dtype=F32) / half)
    ang = pos.astype(F32)[:, None] * inv_freq[None, :]
    cos, sin = jnp.cos(ang), jnp.sin(ang)
    reps = LANES // HEAD_DIM
    cos_t = jnp.tile(jnp.concatenate([cos, cos], axis=1), (1, reps))
    sin_t = jnp.tile(jnp.concatenate([-sin, sin], axis=1), (1, reps))
    return cos_t, sin_t


def kernel(x_prompt, x_sample, cache_conv, cache_win_k, cache_win_v, state_mlstm_C, state_mlstm_n,
           state_mlstm_m, norm_mix, norm_ffn, ab_w_in, ab_conv_w, ab_q_gain, ab_k_gain, ab_sinks, ab_w_out,
           ml_w_in, ml_gate_bias, ml_out_gain, ml_w_out, peer_w_q, peer_sub_keys, peer_u, peer_v):
    n_batch, seq, d = x_prompt.shape
    n_seq, t_len, _ = x_sample.shape
    n_prompt = n_batch * seq
    assert d == D_MODEL and t_len == SUBLANES and norm_mix.shape[0] == 2

    xp = x_prompt.reshape(n_prompt, d)
    xs = x_sample.reshape(n_seq * t_len, d)

    cos_p, sin_p = _rope_tables(jnp.arange(seq, dtype=jnp.int32))
    cos_s, sin_s = _rope_tables(PAST_LEN + jnp.arange(t_len, dtype=jnp.int32))
    bt = 16
    cos_s, sin_s = jnp.tile(cos_s, (bt, 1)), jnp.tile(sin_s, (bt, 1))
    lane = jnp.arange(LANES)
    seg = (lane[:, None] // HEAD_DIM == lane[None, :] // HEAD_DIM).astype(BF16)
    reps = LANES // HEAD_DIM
    qg = jnp.tile(ab_q_gain[0], reps).reshape(1, LANES)
    kg = jnp.tile(ab_k_gain[0], reps).reshape(1, LANES)
    wo_ab = ab_w_out[0].astype(BF16)

    z = _norm_proj([xp, xs], norm_mix[0], ab_w_in[0].astype(BF16))
    y_p, k_p, v_p, c_p = _ab_prompt(z, xp, n_batch, seq, cos_p, sin_p, ab_conv_w[0], qg, kg, ab_sinks[0],
                                    seg, wo_ab)
    y_s, c_s, k_s, v_s = _ab_sample(z, xs, n_prompt, n_seq, t_len, cos_s, sin_s, ab_conv_w[0], qg, kg,
                                    ab_sinks[0], seg, wo_ab, cache_conv[0],
                                    cache_win_k[0].reshape(n_seq, WINDOW, LANES),
                                    cache_win_v[0].reshape(n_seq, WINDOW, LANES), bt=bt)
    x = jnp.concatenate([y_p.reshape(n_prompt, d), y_s], axis=0)
    u_all = (peer_u * RSQRT2).astype(BF16)
    v_all = peer_v.astype(BF16)
    x = _peer(x, norm_ffn[0], peer_w_q[0].astype(BF16), peer_sub_keys[0].astype(BF16), u_all, v_all, 0)

    n_gate = 2 * ML_HEADS
    w_in = jnp.pad(ml_w_in[0], ((0, 0), (0, ML_IN_PAD - ml_w_in.shape[2]))).astype(BF16)
    bias = jnp.pad(ml_gate_bias[0], (0, LANES - n_gate)).reshape(1, LANES)
    og = ml_out_gain[0].reshape(1, D_MODEL)
    idx = jnp.arange(ML_CHUNK)
    tril = (idx[None, :] <= idx[:, None]).astype(BF16)
    wo_ml = ml_w_out[0].astype(BF16)

    z = _norm_proj([x], norm_mix[1], w_in)
    y_p, cm_p, nm_p, mm_p = _mlstm_prompt(z, x, n_batch, seq, bias, og, tril, wo_ml)
    y_s, cm_s, nm_s, mm_s = _mlstm_sample(z, x, n_prompt, n_seq, t_len, bias, og, tril, wo_ml,
                                          state_mlstm_C[0], state_mlstm_n[0],
                                          state_mlstm_m[0].reshape(n_seq, 1, ML_HEADS))
    x = jnp.concatenate([y_p.reshape(n_prompt, d), y_s], axis=0)
    out_p, out_s = _peer(x, norm_ffn[1], peer_w_q[1].astype(BF16), peer_sub_keys[1].astype(BF16), u_all, v_all, 1,
                         split_rows=n_prompt)

    y_prompt = out_p.reshape(n_batch, seq, d)
    y_sample = out_s.reshape(n_seq, t_len, d)
    kv_shape_p = (1, n_batch, WINDOW, N_KV_HEADS, HEAD_DIM)
    kv_shape_s = (1, n_seq, WINDOW, N_KV_HEADS, HEAD_DIM)
    return (y_prompt, y_sample,
            c_p[:, SUBLANES - 2:, :][None], k_p.reshape(kv_shape_p), v_p.reshape(kv_shape_p),
            cm_p[None], nm_p[None], mm_p.reshape(1, n_batch, ML_HEADS),
            c_s[None], k_s.reshape(kv_shape_s), v_s.reshape(kv_shape_s),
            cm_s[None], nm_s[None], mm_s.reshape(1, n_seq, ML_HEADS))
```
